```python
import math
import jax, jax.numpy as jnp
from jax import lax
import numpy as np

D_MODEL = 1024
BATCH = 8
SEQ = 16384
DEPTH = 2

MLA_HEADS = 8
MLA_Q_LORA = 256
MLA_KV_LORA = 128
MLA_NOPE = 64
MLA_ROPE = 32
MLA_V = 64
Q_BLOCK = 128
ROPE_THETA = 10000.0
SWA_HEADS = 8
SWA_KV_HEADS = 2
SWA_HD = 64
WINDOW = 128
BLOCK = 128
REL_BUCKETS = 32
REL_MAX_DIST = 128
MEM_LEN = 256
MEM_HEADS = 4
MEM_HD = 128
N_BRANCH = 3
D_FF = 4 * D_MODEL
EPS = 1e-6

IN_SIZES = (
    MLA_Q_LORA,
    MLA_KV_LORA + MLA_ROPE,
    SWA_HEADS * SWA_HD,
    SWA_KV_HEADS * SWA_HD,
    SWA_KV_HEADS * SWA_HD,
    MEM_HEADS * MEM_HD,
    N_BRANCH * D_MODEL,
)
IN_COLS = sum(IN_SIZES)

kernel_name = "hybrid_mla_swa_mem_gated_block"


def rms_norm(x, g):
    x32 = x.astype(jnp.float32)
    y = x32 * lax.rsqrt(jnp.mean(x32 * x32, axis=-1, keepdims=True) + EPS)
    return y.astype(x.dtype) * g


def split_points():
    pts, acc = [], 0
    for s in IN_SIZES[:-1]:
        acc += s
        pts.append(acc)
    return pts


def rope_tables(seq):
    pos = jnp.arange(seq, dtype=jnp.float32)
    inv = 1.0 / (ROPE_THETA ** (jnp.arange(0, MLA_ROPE, 2, dtype=jnp.float32) / MLA_ROPE))
    ang = pos[:, None] * inv[None, :]
    return jnp.cos(ang), jnp.sin(ang)


def apply_rope(t, cos, sin):
    cos = cos.astype(t.dtype)
    sin = sin.astype(t.dtype)
    t1, t2 = jnp.split(t, 2, axis=-1)
    return jnp.concatenate([t1 * cos - t2 * sin, t2 * cos + t1 * sin], axis=-1)


def t5_bucket(dist):
    n = jnp.maximum(dist, 0)
    max_exact = REL_BUCKETS // 2
    nf = jnp.maximum(n, 1).astype(jnp.float32)
    large = max_exact + (jnp.log(nf / max_exact) / math.log(REL_MAX_DIST / max_exact)
                         * (REL_BUCKETS - max_exact)).astype(jnp.int32)
    large = jnp.minimum(large, REL_BUCKETS - 1)
    return jnp.where(n < max_exact, n, large)


def mla_attention(q_nope, q_rope, k_nope, k_rope, v):
    B, S = q_nope.shape[0], q_nope.shape[1]
    nb = S // Q_BLOCK
    scale = (MLA_NOPE + MLA_ROPE) ** -0.5
    k_pos = jnp.arange(S)

    def to_blocks(t):
        return jnp.moveaxis(t.reshape(B, nb, Q_BLOCK, *t.shape[2:]), 1, 0)

    def one_block(args):
        qn, qr, i = args
        s = (jnp.einsum('bqhd,bkhd->bhqk', qn, k_nope)
             + jnp.einsum('bqhd,bkd->bhqk', qr, k_rope)).astype(jnp.float32) * scale
        q_pos = i * Q_BLOCK + jnp.arange(Q_BLOCK)
        s = jnp.where(k_pos[None, :] <= q_pos[:, None], s, -jnp.inf)
        p = jax.nn.softmax(s, axis=-1).astype(v.dtype)
        return jnp.einsum('bhqk,bkhd->bqhd', p, v)

    out = lax.map(one_block, (to_blocks(q_nope), to_blocks(q_rope), jnp.arange(nb)))
    return jnp.moveaxis(out, 0, 1).reshape(B, S, MLA_HEADS * MLA_V)


def swa_attention(q, k, v, rel_bias, sinks):
    B, S = q.shape[0], q.shape[1]
    nb = S // BLOCK
    G, R = SWA_KV_HEADS, SWA_HEADS // SWA_KV_HEADS
    qb = q.reshape(B, nb, BLOCK, G, R, SWA_HD)
    kb = k.reshape(B, nb, BLOCK, G, SWA_HD)
    vb = v.reshape(B, nb, BLOCK, G, SWA_HD)

    def with_prev(t):
        prev = jnp.pad(t, ((0, 0), (1, 0), (0, 0), (0, 0), (0, 0)))[:, :-1]
        return jnp.concatenate([prev, t], axis=2)

    kk, vv = with_prev(kb), with_prev(vb)
    s = jnp.einsum('bnqgrd,bnkgd->bngrqk', qb, kk).astype(jnp.float32) * (SWA_HD ** -0.5)

    qi = jnp.arange(BLOCK)[:, None]
    kj = jnp.arange(2 * BLOCK)[None, :]
    dist = qi + BLOCK - kj
    bias = rel_bias.astype(jnp.float32)[t5_bucket(dist)]
    bias = jnp.transpose(bias, (2, 0, 1)).reshape(G, R, BLOCK, 2 * BLOCK)
    band = (dist >= 0) & (dist < WINDOW)
    has_prev = (jnp.arange(nb) > 0)[:, None, None] | (kj >= BLOCK)[None]
    valid = band[None] & has_prev
    s = jnp.where(valid[None, :, None, None], s + bias, -jnp.inf)

    sink = jnp.broadcast_to(sinks.astype(jnp.float32).reshape(1, 1, G, R, 1, 1),
                            s.shape[:-1] + (1,))
    p = jax.nn.softmax(jnp.concatenate([s, sink], axis=-1), axis=-1)[..., :-1].astype(v.dtype)
    o = jnp.einsum('bngrqk,bnkgd->bnqgrd', p, vv)
    return o.reshape(B, S, SWA_HEADS * SWA_HD)


def mem_attention(q, km, vm):
    B, S = q.shape[0], q.shape[1]
    s = jnp.einsum('bshd,bmhd->bhsm', q, km).astype(jnp.float32) * (MEM_HD ** -0.5)
    p = jax.nn.softmax(s, axis=-1).astype(vm.dtype)
    return jnp.einsum('bhsm,bmhd->bshd', p, vm).reshape(B, S, MEM_HEADS * MEM_HD)


def _fwd_setup_inputs(seed: int = 0) -> dict:
    key = jax.random.key(seed)
    ks = jax.random.split(key, 24)

    def nrm(k, shape, scale):
        return jax.random.normal(k, shape, jnp.float32) * scale

    def gain(k, shape):
        return 1.0 + 0.02 * jax.random.normal(k, shape, jnp.float32)

    L, D = DEPTH, D_MODEL
    return {
        "x": nrm(ks[0], (BATCH, SEQ, D), 1.0),
        "mem": nrm(ks[1], (BATCH, MEM_LEN, D), 1.0),
        "rel_bias": nrm(ks[2], (REL_BUCKETS, SWA_HEADS), 0.5),
        "attn_norm": gain(ks[3], (L, D)),
        "mem_norm": gain(ks[4], (L, D)),
        "w_in": nrm(ks[5], (L, D, IN_COLS), D ** -0.5),
        "b_gate": nrm(ks[6], (L, N_BRANCH * D), 0.02),
        "mla_q_norm": gain(ks[7], (L, MLA_Q_LORA)),
        "w_uq": nrm(ks[8], (L, MLA_Q_LORA, MLA_HEADS * (MLA_NOPE + MLA_ROPE)), MLA_Q_LORA ** -0.5),
        "mla_kv_norm": gain(ks[9], (L, MLA_KV_LORA)),
        "w_ukv": nrm(ks[10], (L, MLA_KV_LORA, MLA_HEADS * (MLA_NOPE + MLA_V)), MLA_KV_LORA ** -0.5),
        "attn_sinks": nrm(ks[11], (L, SWA_HEADS), 0.5),
        "w_mem_kv": nrm(ks[12], (L, D, 2 * MEM_HEADS * MEM_HD), D ** -0.5),
        "w_o_mla": nrm(ks[13], (L, MLA_HEADS * MLA_V, D), (MLA_HEADS * MLA_V) ** -0.5),
        "w_o_swa": nrm(ks[14], (L, SWA_HEADS * SWA_HD, D), (SWA_HEADS * SWA_HD) ** -0.5),
        "w_o_mem": nrm(ks[15], (L, MEM_HEADS * MEM_HD, D), (MEM_HEADS * MEM_HD) ** -0.5),
        "w_out": nrm(ks[16], (L, D, D), D ** -0.5),
        "mlp_norm": gain(ks[17], (L, D)),
        "w_up": nrm(ks[18], (L, D, D_FF), D ** -0.5),
        "w_down": nrm(ks[19], (L, D_FF, D), D_FF ** -0.5),
        "final_norm": gain(ks[20], (D,)),
    }


def _fwd_reference(x, mem, rel_bias, attn_norm, mem_norm, w_in, b_gate, mla_q_norm, w_uq,
              mla_kv_norm, w_ukv, attn_sinks, w_mem_kv, w_o_mla, w_o_swa, w_o_mem,
              w_out, mlp_norm, w_up, w_down, final_norm):
    B, S, D = x.shape
    cos, sin = rope_tables(S)
    pts = split_points()

    for l in range(DEPTH):
        h = rms_norm(x, attn_norm[l])
        proj = h @ w_in[l]
        c_q, kv_a, q_s, k_s, v_s, q_m, gates = jnp.split(proj, pts, axis=-1)

        c_q = rms_norm(c_q, mla_q_norm[l])
        q = (c_q @ w_uq[l]).reshape(B, S, MLA_HEADS, MLA_NOPE + MLA_ROPE)
        q_nope = q[..., :MLA_NOPE]
        q_pe = apply_rope(q[..., MLA_NOPE:], cos[:, None, :], sin[:, None, :])
        c_kv = rms_norm(kv_a[..., :MLA_KV_LORA], mla_kv_norm[l])
        k_pe = apply_rope(kv_a[..., MLA_KV_LORA:], cos, sin)
        kv = (c_kv @ w_ukv[l]).reshape(B, S, MLA_HEADS, MLA_NOPE + MLA_V)
        o_mla = mla_attention(q_nope, q_pe, kv[..., :MLA_NOPE], k_pe, kv[..., MLA_NOPE:])

        o_swa = swa_attention(q_s.reshape(B, S, SWA_HEADS, SWA_HD),
                              k_s.reshape(B, S, SWA_KV_HEADS, SWA_HD),
                              v_s.reshape(B, S, SWA_KV_HEADS, SWA_HD),
                              rel_bias, attn_sinks[l])

        mn = rms_norm(mem, mem_norm[l])
        kvm = (mn @ w_mem_kv[l]).reshape(mem.shape[0], MEM_LEN, 2, MEM_HEADS, MEM_HD)
        o_mem = mem_attention(q_m.reshape(B, S, MEM_HEADS, MEM_HD), kvm[:, :, 0], kvm[:, :, 1])

        g = jax.nn.sigmoid(gates + b_gate[l]).reshape(B, S, N_BRANCH, D)
        y = (g[..., 0, :] * (o_mla @ w_o_mla[l])
             + g[..., 1, :] * (o_swa @ w_o_swa[l])
             + g[..., 2, :] * (o_mem @ w_o_mem[l]))
        x = x + y @ w_out[l]

        h = rms_norm(x, mlp_norm[l])
        x = x + jnp.square(jax.nn.relu(h @ w_up[l])) @ w_down[l]

    return rms_norm(x, final_norm)


import jax as _jax
import jax.numpy as _jnp

TWIN_FORMAT = 'train_step'
FWD_PARAMS = ['x', 'mem', 'rel_bias', 'attn_norm', 'mem_norm', 'w_in', 'b_gate', 'mla_q_norm', 'w_uq', 'mla_kv_norm', 'w_ukv', 'attn_sinks', 'w_mem_kv', 'w_o_mla', 'w_o_swa', 'w_o_mem', 'w_out', 'mlp_norm', 'w_up', 'w_down', 'final_norm']
TWIN_WEIGHTS = ['rel_bias', 'attn_norm', 'mem_norm', 'w_in', 'b_gate', 'mla_q_norm', 'w_uq', 'mla_kv_norm', 'w_ukv', 'attn_sinks', 'w_mem_kv', 'w_o_mla', 'w_o_swa', 'w_o_mem', 'w_out', 'mlp_norm', 'w_up', 'w_down', 'final_norm']
TWIN_DIFF_INPUT = 'x'
TWIN_INPUTS = ['x', 'mem', 'rel_bias', 'attn_norm', 'mem_norm', 'w_in', 'b_gate', 'mla_q_norm', 'w_uq', 'mla_kv_norm', 'w_ukv', 'attn_sinks', 'w_mem_kv', 'w_o_mla', 'w_o_swa', 'w_o_mem', 'w_out', 'mlp_norm', 'w_up', 'w_down', 'final_norm', 'loss_target', 'm_rel_bias', 'm_attn_norm', 'm_mem_norm', 'm_w_in', 'm_b_gate', 'm_mla_q_norm', 'm_w_uq', 'm_mla_kv_norm', 'm_w_ukv', 'm_attn_sinks', 'm_w_mem_kv', 'm_w_o_mla', 'm_w_o_swa', 'm_w_o_mem', 'm_w_out', 'm_mlp_norm', 'm_w_up', 'm_w_down', 'm_final_norm', 'v_rel_bias', 'v_attn_norm', 'v_mem_norm', 'v_w_in', 'v_b_gate', 'v_mla_q_norm', 'v_w_uq', 'v_mla_kv_norm', 'v_w_ukv', 'v_attn_sinks', 'v_w_mem_kv', 'v_w_o_mla', 'v_w_o_swa', 'v_w_o_mem', 'v_w_out', 'v_mlp_norm', 'v_w_up', 'v_w_down', 'v_final_norm']
TWIN_OUTPUTS = ['loss', 'grad_x', 'grad_rel_bias', 'grad_attn_norm', 'grad_mem_norm', 'grad_w_in', 'grad_b_gate', 'grad_mla_q_norm', 'grad_w_uq', 'grad_mla_kv_norm', 'grad_w_ukv', 'grad_attn_sinks', 'grad_w_mem_kv', 'grad_w_o_mla', 'grad_w_o_swa', 'grad_w_o_mem', 'grad_w_out', 'grad_mlp_norm', 'grad_w_up', 'grad_w_down', 'grad_final_norm', 'delta_rel_bias', 'delta_attn_norm', 'delta_mem_norm', 'delta_w_in', 'delta_b_gate', 'delta_mla_q_norm', 'delta_w_uq', 'delta_mla_kv_norm', 'delta_w_ukv', 'delta_attn_sinks', 'delta_w_mem_kv', 'delta_w_o_mla', 'delta_w_o_swa', 'delta_w_o_mem', 'delta_w_out', 'delta_mlp_norm', 'delta_w_up', 'delta_w_down', 'delta_final_norm', 'new_m_rel_bias', 'new_m_attn_norm', 'new_m_mem_norm', 'new_m_w_in', 'new_m_b_gate', 'new_m_mla_q_norm', 'new_m_w_uq', 'new_m_mla_kv_norm', 'new_m_w_ukv', 'new_m_attn_sinks', 'new_m_w_mem_kv', 'new_m_w_o_mla', 'new_m_w_o_swa', 'new_m_w_o_mem', 'new_m_w_out', 'new_m_mlp_norm', 'new_m_w_up', 'new_m_w_down', 'new_m_final_norm', 'new_v_rel_bias', 'new_v_attn_norm', 'new_v_mem_norm', 'new_v_w_in', 'new_v_b_gate', 'new_v_mla_q_norm', 'new_v_w_uq', 'new_v_mla_kv_norm', 'new_v_w_ukv', 'new_v_attn_sinks', 'new_v_w_mem_kv', 'new_v_w_o_mla', 'new_v_w_o_swa', 'new_v_w_o_mem', 'new_v_w_out', 'new_v_mlp_norm', 'new_v_w_up', 'new_v_w_down', 'new_v_final_norm']
TWIN_LEAF_KINDS = {'loss': 'loss', 'grad_x': 'grad_x', 'grad_rel_bias': 'grad_w', 'grad_attn_norm': 'grad_w', 'grad_mem_norm': 'grad_w', 'grad_w_in': 'grad_w', 'grad_b_gate': 'grad_w', 'grad_mla_q_norm': 'grad_w', 'grad_w_uq': 'grad_w', 'grad_mla_kv_norm': 'grad_w', 'grad_w_ukv': 'grad_w', 'grad_attn_sinks': 'grad_w', 'grad_w_mem_kv': 'grad_w', 'grad_w_o_mla': 'grad_w', 'grad_w_o_swa': 'grad_w', 'grad_w_o_mem': 'grad_w', 'grad_w_out': 'grad_w', 'grad_mlp_norm': 'grad_w', 'grad_w_up': 'grad_w', 'grad_w_down': 'grad_w', 'grad_final_norm': 'grad_w', 'delta_rel_bias': 'delta_w', 'delta_attn_norm': 'delta_w', 'delta_mem_norm': 'delta_w', 'delta_w_in': 'delta_w', 'delta_b_gate': 'delta_w', 'delta_mla_q_norm': 'delta_w', 'delta_w_uq': 'delta_w', 'delta_mla_kv_norm': 'delta_w', 'delta_w_ukv': 'delta_w', 'delta_attn_sinks': 'delta_w', 'delta_w_mem_kv': 'delta_w', 'delta_w_o_mla': 'delta_w', 'delta_w_o_swa': 'delta_w', 'delta_w_o_mem': 'delta_w', 'delta_w_out': 'delta_w', 'delta_mlp_norm': 'delta_w', 'delta_w_up': 'delta_w', 'delta_w_down': 'delta_w', 'delta_final_norm': 'delta_w', 'new_m_rel_bias': 'new_m', 'new_m_attn_norm': 'new_m', 'new_m_mem_norm': 'new_m', 'new_m_w_in': 'new_m', 'new_m_b_gate': 'new_m', 'new_m_mla_q_norm': 'new_m', 'new_m_w_uq': 'new_m', 'new_m_mla_kv_norm': 'new_m', 'new_m_w_ukv': 'new_m', 'new_m_attn_sinks': 'new_m', 'new_m_w_mem_kv': 'new_m', 'new_m_w_o_mla': 'new_m', 'new_m_w_o_swa': 'new_m', 'new_m_w_o_mem': 'new_m', 'new_m_w_out': 'new_m', 'new_m_mlp_norm': 'new_m', 'new_m_w_up': 'new_m', 'new_m_w_down': 'new_m', 'new_m_final_norm': 'new_m', 'new_v_rel_bias': 'new_v', 'new_v_attn_norm': 'new_v', 'new_v_mem_norm': 'new_v', 'new_v_w_in': 'new_v', 'new_v_b_gate': 'new_v', 'new_v_mla_q_norm': 'new_v', 'new_v_w_uq': 'new_v', 'new_v_mla_kv_norm': 'new_v', 'new_v_w_ukv': 'new_v', 'new_v_attn_sinks': 'new_v', 'new_v_w_mem_kv': 'new_v', 'new_v_w_o_mla': 'new_v', 'new_v_w_o_swa': 'new_v', 'new_v_w_o_mem': 'new_v', 'new_v_w_out': 'new_v', 'new_v_mlp_norm': 'new_v', 'new_v_w_up': 'new_v', 'new_v_w_down': 'new_v', 'new_v_final_norm': 'new_v'}


def _forward(args):
    return _fwd_reference(*[args[k] for k in FWD_PARAMS])


def _output_shape():
    def fwd():
        inp = _fwd_setup_inputs(0)
        return _fwd_reference(*[inp[k] for k in FWD_PARAMS])
    out = _jax.eval_shape(fwd)
    return out.shape, out.dtype

N_MICROBATCH = 1
ADAM_LR = 0.001
ADAM_B1 = 0.9
ADAM_B2 = 0.999
ADAM_EPS = 1e-08
ADAM_WD = 0.01
ADAM_STEP = 10
PER_EXAMPLE_BATCH_AXIS = {'x': 0, 'mem': 0, 'loss_target': 0}
SHARED_INPUTS = []
_WEIGHT_DTYPES = {'rel_bias': _jnp.float32, 'attn_norm': _jnp.float32, 'mem_norm': _jnp.float32, 'w_in': _jnp.float32, 'b_gate': _jnp.float32, 'mla_q_norm': _jnp.float32, 'w_uq': _jnp.float32, 'mla_kv_norm': _jnp.float32, 'w_ukv': _jnp.float32, 'attn_sinks': _jnp.float32, 'w_mem_kv': _jnp.float32, 'w_o_mla': _jnp.float32, 'w_o_swa': _jnp.float32, 'w_o_mem': _jnp.float32, 'w_out': _jnp.float32, 'mlp_norm': _jnp.float32, 'w_up': _jnp.float32, 'w_down': _jnp.float32, 'final_norm': _jnp.float32}
MOMENT_SCALE = {'rel_bias': 9.585895e-02, 'attn_norm': 1.000970e-01, 'mem_norm': 3.121925e-02, 'w_in': 4.713548e-02, 'b_gate': 1.928245e-02, 'mla_q_norm': 6.117368e-02, 'w_uq': 3.716782e-02, 'mla_kv_norm': 1.812595e-01, 'w_ukv': 5.808291e-02, 'attn_sinks': 5.112159e-02, 'w_mem_kv': 2.870661e-02, 'w_o_mla': 5.288855e-02, 'w_o_swa': 5.506281e-02, 'w_o_mem': 2.113832e-02, 'w_out': 7.565907e-02, 'mlp_norm': 3.118130e-01, 'w_up': 1.537018e-01, 'w_down': 3.726370e-01, 'final_norm': 1.310223e+02}


def _to_microbatches(a, axis):
    t = _jnp.moveaxis(a, axis, 0)
    t = t.reshape((N_MICROBATCH, t.shape[0] // N_MICROBATCH) + t.shape[1:])
    return _jnp.moveaxis(t, 1, axis + 1)


def setup_inputs(seed: int = 0) -> dict:
    inp = _fwd_setup_inputs(seed)
    key = _jax.random.fold_in(_jax.random.key(seed), 7919)
    shape, _ = _output_shape()
    out = dict(inp)
    out["loss_target"] = _jax.random.normal(_jax.random.fold_in(key, 0), shape, _jnp.float32)
    for i, name in enumerate(TWIN_WEIGHTS):
        w = inp[name].astype(_jnp.float32)
        if MOMENT_SCALE is None:
            s = _jnp.sqrt(_jnp.mean(_jnp.square(w)) + 1e-30)
        else:
            s = MOMENT_SCALE[name]
        km, kv = _jax.random.split(_jax.random.fold_in(key, i + 1))
        out[name] = w
        out["m_" + name] = s * _jax.random.normal(km, w.shape, _jnp.float32)
        out["v_" + name] = (s * s) * _jax.random.uniform(kv, w.shape, _jnp.float32, 0.5, 1.5)
    if N_MICROBATCH > 1:
        for name, axis in PER_EXAMPLE_BATCH_AXIS.items():
            out[name] = _to_microbatches(out[name], axis)
    return {'x': out['x'], 'mem': out['mem'], 'rel_bias': out['rel_bias'], 'attn_norm': out['attn_norm'], 'mem_norm': out['mem_norm'], 'w_in': out['w_in'], 'b_gate': out['b_gate'], 'mla_q_norm': out['mla_q_norm'], 'w_uq': out['w_uq'], 'mla_kv_norm': out['mla_kv_norm'], 'w_ukv': out['w_ukv'], 'attn_sinks': out['attn_sinks'], 'w_mem_kv': out['w_mem_kv'], 'w_o_mla': out['w_o_mla'], 'w_o_swa': out['w_o_swa'], 'w_o_mem': out['w_o_mem'], 'w_out': out['w_out'], 'mlp_norm': out['mlp_norm'], 'w_up': out['w_up'], 'w_down': out['w_down'], 'final_norm': out['final_norm'], 'loss_target': out['loss_target'], 'm_rel_bias': out['m_rel_bias'], 'm_attn_norm': out['m_attn_norm'], 'm_mem_norm': out['m_mem_norm'], 'm_w_in': out['m_w_in'], 'm_b_gate': out['m_b_gate'], 'm_mla_q_norm': out['m_mla_q_norm'], 'm_w_uq': out['m_w_uq'], 'm_mla_kv_norm': out['m_mla_kv_norm'], 'm_w_ukv': out['m_w_ukv'], 'm_attn_sinks': out['m_attn_sinks'], 'm_w_mem_kv': out['m_w_mem_kv'], 'm_w_o_mla': out['m_w_o_mla'], 'm_w_o_swa': out['m_w_o_swa'], 'm_w_o_mem': out['m_w_o_mem'], 'm_w_out': out['m_w_out'], 'm_mlp_norm': out['m_mlp_norm'], 'm_w_up': out['m_w_up'], 'm_w_down': out['m_w_down'], 'm_final_norm': out['m_final_norm'], 'v_rel_bias': out['v_rel_bias'], 'v_attn_norm': out['v_attn_norm'], 'v_mem_norm': out['v_mem_norm'], 'v_w_in': out['v_w_in'], 'v_b_gate': out['v_b_gate'], 'v_mla_q_norm': out['v_mla_q_norm'], 'v_w_uq': out['v_w_uq'], 'v_mla_kv_norm': out['v_mla_kv_norm'], 'v_w_ukv': out['v_w_ukv'], 'v_attn_sinks': out['v_attn_sinks'], 'v_w_mem_kv': out['v_w_mem_kv'], 'v_w_o_mla': out['v_w_o_mla'], 'v_w_o_swa': out['v_w_o_swa'], 'v_w_o_mem': out['v_w_o_mem'], 'v_w_out': out['v_w_out'], 'v_mlp_norm': out['v_mlp_norm'], 'v_w_up': out['v_w_up'], 'v_w_down': out['v_w_down'], 'v_final_norm': out['v_final_norm']}


def _loss(weights, diff, rest, loss_target):
    with _jax.named_scope("forward"):
        args = {**rest, TWIN_DIFF_INPUT: diff, **{k: w.astype(_WEIGHT_DTYPES[k]) for k, w in weights.items()}}
        y = _forward(args)
    with _jax.named_scope("loss_head"):
        err = _jnp.square(y.astype(_jnp.float32) - loss_target)
        return 0.5 * _jnp.sum(_jnp.mean(err, axis=-1)) if err.ndim else 0.5 * err


def _adamw(w, g, m, v):
    m = ADAM_B1 * m + (1.0 - ADAM_B1) * g
    v = ADAM_B2 * v + (1.0 - ADAM_B2) * _jnp.square(g)
    m_hat = m / (1.0 - ADAM_B1 ** ADAM_STEP)
    v_hat = v / (1.0 - ADAM_B2 ** ADAM_STEP)
    delta = -ADAM_LR * (m_hat / (_jnp.sqrt(v_hat) + ADAM_EPS) + ADAM_WD * w)
    return delta, m, v


def reference(x, mem, rel_bias, attn_norm, mem_norm, w_in, b_gate, mla_q_norm, w_uq, mla_kv_norm, w_ukv, attn_sinks, w_mem_kv, w_o_mla, w_o_swa, w_o_mem, w_out, mlp_norm, w_up, w_down, final_norm, loss_target, m_rel_bias, m_attn_norm, m_mem_norm, m_w_in, m_b_gate, m_mla_q_norm, m_w_uq, m_mla_kv_norm, m_w_ukv, m_attn_sinks, m_w_mem_kv, m_w_o_mla, m_w_o_swa, m_w_o_mem, m_w_out, m_mlp_norm, m_w_up, m_w_down, m_final_norm, v_rel_bias, v_attn_norm, v_mem_norm, v_w_in, v_b_gate, v_mla_q_norm, v_w_uq, v_mla_kv_norm, v_w_ukv, v_attn_sinks, v_w_mem_kv, v_w_o_mla, v_w_o_swa, v_w_o_mem, v_w_out, v_mlp_norm, v_w_up, v_w_down, v_final_norm):
    given = dict(x=x, mem=mem, rel_bias=rel_bias, attn_norm=attn_norm, mem_norm=mem_norm, w_in=w_in, b_gate=b_gate, mla_q_norm=mla_q_norm, w_uq=w_uq, mla_kv_norm=mla_kv_norm, w_ukv=w_ukv, attn_sinks=attn_sinks, w_mem_kv=w_mem_kv, w_o_mla=w_o_mla, w_o_swa=w_o_swa, w_o_mem=w_o_mem, w_out=w_out, mlp_norm=mlp_norm, w_up=w_up, w_down=w_down, final_norm=final_norm, loss_target=loss_target, m_rel_bias=m_rel_bias, m_attn_norm=m_attn_norm, m_mem_norm=m_mem_norm, m_w_in=m_w_in, m_b_gate=m_b_gate, m_mla_q_norm=m_mla_q_norm, m_w_uq=m_w_uq, m_mla_kv_norm=m_mla_kv_norm, m_w_ukv=m_w_ukv, m_attn_sinks=m_attn_sinks, m_w_mem_kv=m_w_mem_kv, m_w_o_mla=m_w_o_mla, m_w_o_swa=m_w_o_swa, m_w_o_mem=m_w_o_mem, m_w_out=m_w_out, m_mlp_norm=m_mlp_norm, m_w_up=m_w_up, m_w_down=m_w_down, m_final_norm=m_final_norm, v_rel_bias=v_rel_bias, v_attn_norm=v_attn_norm, v_mem_norm=v_mem_norm, v_w_in=v_w_in, v_b_gate=v_b_gate, v_mla_q_norm=v_mla_q_norm, v_w_uq=v_w_uq, v_mla_kv_norm=v_mla_kv_norm, v_w_ukv=v_w_ukv, v_attn_sinks=v_attn_sinks, v_w_mem_kv=v_w_mem_kv, v_w_o_mla=v_w_o_mla, v_w_o_swa=v_w_o_swa, v_w_o_mem=v_w_o_mem, v_w_out=v_w_out, v_mlp_norm=v_mlp_norm, v_w_up=v_w_up, v_w_down=v_w_down, v_final_norm=v_final_norm)
    weights = {n: given[n] for n in TWIN_WEIGHTS}
    shared = {n: given[n] for n in SHARED_INPUTS}
    per_example = {n: given[n] for n in ['x', 'mem']}
    grad_fn = _jax.value_and_grad(_loss, argnums=(0, 1))

    def one_microbatch(ex, loss_target):
        ex = dict(ex)
        diff = ex.pop(TWIN_DIFF_INPUT)
        return grad_fn(weights, diff, {**shared, **ex}, loss_target)

    if N_MICROBATCH == 1:
        loss, (grad_w, grad_x) = one_microbatch(per_example, given["loss_target"])
    else:
        def body(carry, xs):
            loss_sum, grad_sum = carry
            l_k, (gw_k, gx_k) = one_microbatch(xs[0], xs[1])
            with _jax.named_scope("update"):
                return (loss_sum + l_k, _jax.tree.map(_jnp.add, grad_sum, gw_k)), gx_k

        init = (_jnp.zeros((), _jnp.float32), _jax.tree.map(_jnp.zeros_like, weights))
        (loss, grad_w), grad_x = _jax.lax.scan(body, init, (per_example, given["loss_target"]))
    with _jax.named_scope("update"):
        delta_w, new_m, new_v = {}, {}, {}
        for n in TWIN_WEIGHTS:
            delta_w[n], new_m[n], new_v[n] = _adamw(weights[n], grad_w[n], given["m_" + n], given["v_" + n])
    return (loss, grad_x, *[grad_w[n] for n in TWIN_WEIGHTS], *[delta_w[n] for n in TWIN_WEIGHTS],
            *[new_m[n] for n in TWIN_WEIGHTS], *[new_v[n] for n in TWIN_WEIGHTS])
```

```python
import math

import jax
import jax.numpy as jnp
from jax import lax
from jax.experimental import pallas as pl
from jax.experimental.pallas import tpu as pltpu

F32 = jnp.float32
BF16 = jnp.bfloat16

N_DEV = 8
D_MODEL = 1024
DEPTH = 2
MLA_HEADS = 8
MLA_Q_LORA = 256
MLA_KV_LORA = 128
MLA_NOPE = 64
MLA_ROPE = 32
MLA_V = 64
ROPE_THETA = 10000.0
SWA_HEADS = 8
SWA_KV_HEADS = 2
SWA_HD = 64
WINDOW = 128
REL_BUCKETS = 32
REL_MAX_DIST = 128
MEM_LEN = 256
MEM_HEADS = 4
MEM_HD = 128
D_FF = 4 * D_MODEL
EPS = 1e-6
HEAD_PAD = 128
ADAM_LR = 0.001
ADAM_B1 = 0.9
ADAM_B2 = 0.999
ADAM_EPS = 1e-08
ADAM_WD = 0.01
ADAM_STEP = 10

NEG = -1e30
VMEM_LIMIT = 48 * 1024 * 1024

MM_TM = 1024
MM_TN = 512
MM_TK = 1024
TN_T1 = 512
TN_TN = 1024
TN_TS = 1024
ROW_TM = 256
MLA_TQ = 512
MLA_TK = 512
SWA_T = 256
MEM_TQ = 1024
ADAM_TM = 1200

WSPECS = (
    ("w_in", (DEPTH, D_MODEL, 4768), 2),
    ("w_uq", (DEPTH, MLA_Q_LORA, 768), 2),
    ("w_ukv", (DEPTH, MLA_KV_LORA, 1024), 2),
    ("w_mem_kv", (DEPTH, D_MODEL, 1024), 1),
    ("w_o_mla", (DEPTH, 512, D_MODEL), 2),
    ("w_o_swa", (DEPTH, 512, D_MODEL), 2),
    ("w_o_mem", (DEPTH, 512, D_MODEL), 2),
    ("w_out", (DEPTH, D_MODEL, D_MODEL), 1),
    ("w_up", (DEPTH, D_MODEL, D_FF), 2),
    ("w_down", (DEPTH, D_FF, D_MODEL), 1),
)
SMALL = (
    ("rel_bias", (REL_BUCKETS, SWA_HEADS)),
    ("attn_norm", (DEPTH, D_MODEL)),
    ("mem_norm", (DEPTH, D_MODEL)),
    ("b_gate", (DEPTH, 3 * D_MODEL)),
    ("mla_q_norm", (DEPTH, MLA_Q_LORA)),
    ("mla_kv_norm", (DEPTH, MLA_KV_LORA)),
    ("attn_sinks", (DEPTH, SWA_HEADS)),
    ("mlp_norm", (DEPTH, D_MODEL)),
    ("final_norm", (D_MODEL,)),
)
WEIGHT_ORDER = ("rel_bias", "attn_norm", "mem_norm", "w_in", "b_gate", "mla_q_norm", "w_uq", "mla_kv_norm",
                "w_ukv", "attn_sinks", "w_mem_kv", "w_o_mla", "w_o_swa", "w_o_mem", "w_out", "mlp_norm",
                "w_up", "w_down", "final_norm")


def _cparams(*sem):
    return pltpu.CompilerParams(dimension_semantics=sem, vmem_limit_bytes=VMEM_LIMIT)


def _shard_shape(shape, axis):
    s = list(shape)
    s[axis] //= N_DEV
    return tuple(s)


def _mm(name, a, b, out_dtypes, *, epi=None, extras=(), a_fn=None, cast=BF16, precision=None,
        tm=None, tn=None, tk=None):
    M, K = a.shape
    K2, N = b.shape
    assert K == K2, (name, a.shape, b.shape)
    tm = min(tm or MM_TM, M)
    tn = min(tn or MM_TN, N)
    tk = min(tk or MM_TK, K)
    assert M % tm == 0 and N % tn == 0 and K % tk == 0, (name, a.shape, b.shape, tm, tn, tk)
    nk = K // tk
    n_ex = len(extras)
    n_out = len(out_dtypes)

    def body(*refs):
        a_ref, b_ref = refs[0], refs[1]
        ex_refs = refs[2:2 + n_ex]
        out_refs = refs[2 + n_ex:2 + n_ex + n_out]
        av = a_ref[...]
        if a_fn is not None:
            av = a_fn(av)
        bv = b_ref[...]
        if cast is not None:
            av = av.astype(cast)
            bv = bv.astype(cast)
        part = jnp.dot(av, bv, preferred_element_type=F32, precision=precision)

        def finish(acc):
            outs = epi(acc, *[r[...] for r in ex_refs]) if epi is not None else (acc,)
            for r, o in zip(out_refs, outs):
                r[...] = o.astype(r.dtype)

        if nk == 1:
            finish(part)
        else:
            acc_ref = refs[-1]
            k = pl.program_id(2)

            @pl.when(k == 0)
            def _():
                acc_ref[...] = part

            @pl.when(k > 0)
            def _():
                acc_ref[...] += part

            @pl.when(k == nk - 1)
            def _():
                finish(acc_ref[...])

    in_specs = [pl.BlockSpec((tm, tk), lambda i, j, k: (i, k)),
                pl.BlockSpec((tk, tn), lambda i, j, k: (k, j))]
    for arr, kind in extras:
        if kind == "mn":
            in_specs.append(pl.BlockSpec((tm, tn), lambda i, j, k: (i, j)))
        elif kind == "m":
            in_specs.append(pl.BlockSpec((tm, arr.shape[1]), lambda i, j, k: (i, 0)))
        else:
            in_specs.append(pl.BlockSpec((1, tn), lambda i, j, k: (0, j)))
    outs = pl.pallas_call(
        body, name=name, grid=(M // tm, N // tn, nk),
        in_specs=in_specs,
        out_specs=[pl.BlockSpec((tm, tn), lambda i, j, k: (i, j)) for _ in out_dtypes],
        out_shape=[jax.ShapeDtypeStruct((M, N), dt) for dt in out_dtypes],
        scratch_shapes=[pltpu.VMEM((tm, tn), F32)] if nk > 1 else [],
        compiler_params=_cparams("parallel", "parallel", "arbitrary"),
    )(a, b, *[arr for arr, _ in extras])
    return outs[0] if n_out == 1 else outs


def _mm_tn(name, a, b, *, t1=None, tn=None, ts=None):
    S, K1 = a.shape
    S2, N = b.shape
    assert S == S2, (name, a.shape, b.shape)
    t1 = min(t1 or TN_T1, K1)
    tn = min(tn or TN_TN, N)
    ts = min(ts or TN_TS, S)
    assert K1 % t1 == 0 and N % tn == 0 and S % ts == 0, (name, a.shape, b.shape)

    def body(a_ref, b_ref, o_ref):
        s = pl.program_id(2)
        part = lax.dot_general(a_ref[...].astype(BF16), b_ref[...].astype(BF16),
                               (((0,), (0,)), ((), ())), preferred_element_type=F32)

        @pl.when(s == 0)
        def _():
            o_ref[...] = part

        @pl.when(s > 0)
        def _():
            o_ref[...] += part

    return pl.pallas_call(
        body, name=name, grid=(K1 // t1, N // tn, S // ts),
        in_specs=[pl.BlockSpec((ts, t1), lambda i, j, s: (s, i)),
                  pl.BlockSpec((ts, tn), lambda i, j, s: (s, j))],
        out_specs=pl.BlockSpec((t1, tn), lambda i, j, s: (i, j)),
        out_shape=jax.ShapeDtypeStruct((K1, N), F32),
        compiler_params=_cparams("parallel", "parallel", "arbitrary"),
    )(a, b)


def _rows(arr, width=None, blk=0):
    return (arr, ("rows", arr.shape[1] if width is None else width, blk))


def _full(arr):
    return (arr, ("full",))


def _rowwise(name, fn, ins, outs, *, rows, tm=None):
    tm = min(tm or ROW_TM, rows)
    assert rows % tm == 0, (name, rows, tm)
    n_in = len(ins)

    def body(*refs):
        i = pl.program_id(0)
        vals = fn(*[r[...] for r in refs[:n_in]])
        for (shape, dt, kind), r, v in zip(outs, refs[n_in:], vals):
            if kind == "rows":
                r[...] = v.astype(dt)
            else:
                @pl.when(i == 0)
                def _(r=r, v=v):
                    r[...] = v

                @pl.when(i > 0)
                def _(r=r, v=v):
                    r[...] += v

    in_specs = []
    for arr, spec in ins:
        if spec[0] == "rows":
            in_specs.append(pl.BlockSpec((tm, spec[1]), lambda i, b=spec[2]: (i, b)))
        else:
            in_specs.append(pl.BlockSpec(arr.shape, lambda i, n=arr.ndim: (0,) * n))
    out_specs = []
    for shape, dt, kind in outs:
        if kind == "rows":
            out_specs.append(pl.BlockSpec((tm, shape[1]), lambda i: (i, 0)))
        else:
            out_specs.append(pl.BlockSpec(shape, lambda i, n=len(shape): (0,) * n))
    res = pl.pallas_call(
        body, name=name, grid=(rows // tm,),
        in_specs=in_specs, out_specs=out_specs,
        out_shape=[jax.ShapeDtypeStruct(shape, dt) for shape, dt, _ in outs],
        compiler_params=_cparams("arbitrary"),
    )(*[arr for arr, _ in ins])
    return res


def _flash_fwd(name, q_arr, k_arr, v_arr, *, heads, q_off, k_off, v_off, group, mode, scale, tq, tk,
               bias=None, sinks=None):
    S = q_arr.shape[0]
    Sk = k_arr.shape[0]
    tq = min(tq, S)
    tk = min(tk, Sk)
    nq = S // tq
    if mode == "window":
        assert tq == tk
        nsteps = 2
    else:
        nsteps = Sk // tk

    def kblock(qi, kk):
        if mode == "causal":
            return jnp.minimum(kk, (qi * tq + tq - 1) // tk)
        if mode == "window":
            return jnp.maximum(qi + kk - 1, 0)
        return kk

    has_bias = bias is not None
    has_sink = sinks is not None

    def body(*refs):
        q_ref, k_ref, v_ref = refs[:3]
        pos = 3
        bias_ref = sink_ref = None
        if has_bias:
            bias_ref = refs[pos]
            pos += 1
        if has_sink:
            sink_ref = refs[pos]
            pos += 1
        o_ref, lse_ref, m_sc, l_sc, acc_sc = refs[pos:pos + 5]
        qi = pl.program_id(1)
        kk = pl.program_id(2)

        @pl.when(kk == 0)
        def _():
            if has_sink:
                m_sc[...] = jnp.zeros(m_sc.shape, F32) + sink_ref[0, 0:1, 0:1]
                l_sc[...] = jnp.ones(l_sc.shape, F32)
            else:
                m_sc[...] = jnp.full(m_sc.shape, NEG, F32)
                l_sc[...] = jnp.zeros(l_sc.shape, F32)
            acc_sc[...] = jnp.zeros(acc_sc.shape, F32)

        if mode == "causal":
            run = kk <= (qi * tq + tq - 1) // tk
        elif mode == "window":
            run = qi + kk >= 1
        else:
            run = None

        def step():
            s = lax.dot_general(q_ref[...], k_ref[...], (((1,), (1,)), ((), ())),
                                preferred_element_type=F32) * scale
            if has_bias:
                s = s + bias_ref[0, 0]
            if mode == "causal":
                rows = qi * tq + lax.broadcasted_iota(jnp.int32, (tq, tk), 0)
                cols = kk * tk + lax.broadcasted_iota(jnp.int32, (tq, tk), 1)
                s = jnp.where(cols <= rows, s, NEG)
            m_prev = m_sc[...]
            m_new = jnp.maximum(m_prev, jnp.max(s, axis=1, keepdims=True))
            alpha = jnp.exp(m_prev - m_new)
            p = jnp.exp(s - m_new)
            l_sc[...] = alpha * l_sc[...] + jnp.sum(p, axis=1, keepdims=True)
            acc_sc[...] = alpha * acc_sc[...] + jnp.dot(p.astype(BF16), v_ref[...],
                                                        preferred_element_type=F32)
            m_sc[...] = m_new

        if run is None:
            step()
        else:
            pl.when(run)(step)

        @pl.when(kk == nsteps - 1)
        def _():
            l = l_sc[...]
            o_ref[...] = (acc_sc[...] / l).astype(o_ref.dtype)
            lse_ref[0] = m_sc[...] + jnp.log(l)

    in_specs = [
        pl.BlockSpec((tq, HEAD_PAD), lambda h, qi, kk: (qi, q_off + h)),
        pl.BlockSpec((tk, HEAD_PAD), lambda h, qi, kk: (kblock(qi, kk), k_off + h // group)),
        pl.BlockSpec((tk, HEAD_PAD), lambda h, qi, kk: (kblock(qi, kk), v_off + h // group)),
    ]
    args = [q_arr, k_arr, v_arr]
    if has_bias:
        in_specs.append(pl.BlockSpec((1, 1, tq, tk), lambda h, qi, kk: (h, kk, 0, 0)))
        args.append(bias)
    if has_sink:
        in_specs.append(pl.BlockSpec((1, 8, 128), lambda h, qi, kk: (h, 0, 0)))
        args.append(sinks)
    o, lse = pl.pallas_call(
        body, name=name, grid=(heads, nq, nsteps),
        in_specs=in_specs,
        out_specs=[pl.BlockSpec((tq, HEAD_PAD), lambda h, qi, kk: (qi, h)),
                   pl.BlockSpec((1, tq, 1), lambda h, qi, kk: (h, qi, 0))],
        out_shape=[jax.ShapeDtypeStruct((S, heads * HEAD_PAD), BF16),
                   jax.ShapeDtypeStruct((heads, S, 1), F32)],
        scratch_shapes=[pltpu.VMEM((tq, 1), F32), pltpu.VMEM((tq, 1), F32), pltpu.VMEM((tq, HEAD_PAD), F32)],
        compiler_params=_cparams("parallel", "parallel", "arbitrary"),
    )(*args)
    return o, lse


def _flash_bwd(name, q_arr, k_arr, v_arr, o_arr, do_arr, lse, *, heads, q_off, k_off, v_off, group, mode,
               scale, tq, tk, bias=None, sinks=None):
    S = q_arr.shape[0]
    Sk = k_arr.shape[0]
    tq = min(tq, S)
    tk = min(tk, Sk)
    nq = S // tq
    nkb = Sk // tk
    if mode == "window":
        assert tq == tk
        nsteps = 2
    else:
        nsteps = nq

    def qblock(kj, qq):
        if mode == "causal":
            return jnp.maximum(qq, (kj * tk) // tq)
        if mode == "window":
            return jnp.minimum(kj + qq, nq - 1)
        return qq

    has_bias = bias is not None
    has_sink = sinks is not None

    def body(*refs):
        q_ref, k_ref, v_ref, o_ref, do_ref, lse_ref = refs[:6]
        pos = 6
        bias_ref = sink_ref = None
        if has_bias:
            bias_ref = refs[pos]
            pos += 1
        if has_sink:
            sink_ref = refs[pos]
            pos += 1
        dq_ref, dk_ref, dv_ref = refs[pos:pos + 3]
        pos += 3
        dbias_ref = dsink_ref = None
        if has_bias:
            dbias_ref = refs[pos]
            pos += 1
        if has_sink:
            dsink_ref = refs[pos]
            pos += 1
        dk_sc, dv_sc = refs[pos:pos + 2]
        kj = pl.program_id(1)
        qq = pl.program_id(2)
        qb = qblock(kj, qq)

        @pl.when((kj == 0) & (qq == 0))
        def _():
            dq_ref[...] = jnp.zeros(dq_ref.shape, F32)
            if has_bias:
                dbias_ref[...] = jnp.zeros(dbias_ref.shape, F32)
            if has_sink:
                dsink_ref[...] = jnp.zeros(dsink_ref.shape, F32)

        @pl.when(qq == 0)
        def _():
            dk_sc[...] = jnp.zeros(dk_sc.shape, F32)
            dv_sc[...] = jnp.zeros(dv_sc.shape, F32)

        if mode == "causal":
            run = qq >= (kj * tk) // tq
        elif mode == "window":
            run = kj + qq <= nq - 1
        else:
            run = None

        def step():
            q = q_ref[...]
            k = k_ref[...]
            do = do_ref[...]
            lse_v = lse_ref[0]
            s = lax.dot_general(q, k, (((1,), (1,)), ((), ())), preferred_element_type=F32) * scale
            if has_bias:
                s = s + bias_ref[0, 0]
            if mode == "causal":
                rows = qb * tq + lax.broadcasted_iota(jnp.int32, (tq, tk), 0)
                cols = kj * tk + lax.broadcasted_iota(jnp.int32, (tq, tk), 1)
                s = jnp.where(cols <= rows, s, NEG)
            p = jnp.exp(s - lse_v)
            delta = jnp.sum(do.astype(F32) * o_ref[...].astype(F32), axis=1, keepdims=True)
            dv_sc[...] += lax.dot_general(p.astype(BF16), do, (((0,), (0,)), ((), ())),
                                          preferred_element_type=F32)
            dp = lax.dot_general(do, v_ref[...], (((1,), (1,)), ((), ())), preferred_element_type=F32)
            dsp = p * (dp - delta)
            if has_bias:
                @pl.when(qq == 0)
                def _():
                    dbias_ref[0, 1] += dsp

                @pl.when(qq == 1)
                def _():
                    dbias_ref[0, 0] += dsp
            ds = (dsp * scale).astype(BF16)
            row0 = pl.multiple_of(qb * tq, tq)
            dq_ref[pl.ds(row0, tq), :] += jnp.dot(ds, k, preferred_element_type=F32)
            dk_sc[...] += lax.dot_general(ds, q, (((0,), (0,)), ((), ())), preferred_element_type=F32)
            if has_sink:
                @pl.when(qq == 0)
                def _():
                    ps = jnp.exp(sink_ref[0, 0:1, 0:1] - lse_v)
                    dsink_ref[...] += jnp.zeros(dsink_ref.shape, F32) - jnp.sum(ps * delta)

        if run is None:
            step()
        else:
            pl.when(run)(step)

        @pl.when(qq == nsteps - 1)
        def _():
            dk_ref[...] = dk_sc[...]
            dv_ref[...] = dv_sc[...]

    def bias_blk(h, kj, qq):
        return (h, 1 - qq, 0, 0)

    in_specs = [
        pl.BlockSpec((tq, HEAD_PAD), lambda h, kj, qq: (qblock(kj, qq), q_off + h)),
        pl.BlockSpec((tk, HEAD_PAD), lambda h, kj, qq: (kj, k_off + h // group)),
        pl.BlockSpec((tk, HEAD_PAD), lambda h, kj, qq: (kj, v_off + h // group)),
        pl.BlockSpec((tq, HEAD_PAD), lambda h, kj, qq: (qblock(kj, qq), h)),
        pl.BlockSpec((tq, HEAD_PAD), lambda h, kj, qq: (qblock(kj, qq), h)),
        pl.BlockSpec((1, tq, 1), lambda h, kj, qq: (h, qblock(kj, qq), 0)),
    ]
    args = [q_arr, k_arr, v_arr, o_arr, do_arr, lse]
    out_specs = [
        pl.BlockSpec((S, HEAD_PAD), lambda h, kj, qq: (0, h)),
        pl.BlockSpec((tk, HEAD_PAD), lambda h, kj, qq: (kj, h)),
        pl.BlockSpec((tk, HEAD_PAD), lambda h, kj, qq: (kj, h)),
    ]
    out_shape = [jax.ShapeDtypeStruct((S, heads * HEAD_PAD), F32),
                 jax.ShapeDtypeStruct((Sk, heads * HEAD_PAD), F32),
                 jax.ShapeDtypeStruct((Sk, heads * HEAD_PAD), F32)]
    if has_bias:
        in_specs.append(pl.BlockSpec((1, 1, tq, tk), bias_blk))
        args.append(bias)
        out_specs.append(pl.BlockSpec((1, 2, tq, tk), lambda h, kj, qq: (h, 0, 0, 0)))
        out_shape.append(jax.ShapeDtypeStruct((heads, 2, tq, tk), F32))
    if has_sink:
        in_specs.append(pl.BlockSpec((1, 8, 128), lambda h, kj, qq: (h, 0, 0)))
        args.append(sinks)
        out_specs.append(pl.BlockSpec((1, 8, 128), lambda h, kj, qq: (h, 0, 0)))
        out_shape.append(jax.ShapeDtypeStruct((heads, 8, 128), F32))
    return pl.pallas_call(
        body, name=name, grid=(heads, nkb, nsteps),
        in_specs=in_specs, out_specs=out_specs, out_shape=out_shape,
        scratch_shapes=[pltpu.VMEM((tk, HEAD_PAD), F32), pltpu.VMEM((tk, HEAD_PAD), F32)],
        compiler_params=_cparams("arbitrary", "arbitrary", "arbitrary"),
    )(*args)


def _exchange(name, send, *, per_peer):
    shape = send.shape[1:] if per_peer else send.shape

    def body(send_ref, recv_ref, send_sems, recv_sems, local_sem):
        x, y, c = lax.axis_index("x"), lax.axis_index("y"), lax.axis_index("c")
        me = 4 * x + 2 * y + c
        own = pltpu.make_async_copy(send_ref.at[me] if per_peer else send_ref, recv_ref.at[me], local_sem)
        own.start()
        copies = []
        for k in range(1, N_DEV):
            px = 1 - x if (k >> 2) & 1 else x
            py = 1 - y if (k >> 1) & 1 else y
            pc = 1 - c if k & 1 else c
            peer = 4 * px + 2 * py + pc
            out = pltpu.make_async_remote_copy(
                src_ref=send_ref.at[peer] if per_peer else send_ref, dst_ref=recv_ref.at[me],
                send_sem=send_sems.at[k - 1], recv_sem=recv_sems.at[k - 1],
                device_id=(px, py, pc), device_id_type=pl.DeviceIdType.MESH)
            out.start()
            back = pltpu.make_async_remote_copy(
                src_ref=send_ref.at[me] if per_peer else send_ref, dst_ref=recv_ref.at[peer],
                send_sem=send_sems.at[k - 1], recv_sem=recv_sems.at[k - 1],
                device_id=(px, py, pc), device_id_type=pl.DeviceIdType.MESH)
            copies.append((out, back))
        for out, back in copies:
            out.wait_send()
            back.wait_recv()
        own.wait()

    return pl.pallas_call(
        body, name=name,
        in_specs=[pl.BlockSpec(memory_space=pl.ANY)],
        out_specs=pl.BlockSpec(memory_space=pl.ANY),
        out_shape=jax.ShapeDtypeStruct((N_DEV,) + tuple(shape), send.dtype),
        scratch_shapes=[pltpu.SemaphoreType.DMA((N_DEV - 1,)), pltpu.SemaphoreType.DMA((N_DEV - 1,)),
                        pltpu.SemaphoreType.DMA(())],
    )(send)


def _adam(name, recv, w, m, v, *, tm=None):
    R = w.shape[0]
    tm = max(t for t in range(8, min(tm or ADAM_TM, R) + 1, 8) if R % t == 0)
    c1 = 1.0 / (1.0 - ADAM_B1 ** ADAM_STEP)
    c2 = 1.0 / (1.0 - ADAM_B2 ** ADAM_STEP)

    def body(r_ref, w_ref, m_ref, v_ref, g_ref, d_ref, nm_ref, nv_ref):
        g = r_ref[0].astype(F32)
        for j in range(1, N_DEV):
            g = g + r_ref[j].astype(F32)
        wv = w_ref[...]
        nm = ADAM_B1 * m_ref[...] + (1.0 - ADAM_B1) * g
        nv = ADAM_B2 * v_ref[...] + (1.0 - ADAM_B2) * (g * g)
        m_hat = nm * c1
        v_hat = nv * c2
        g_ref[...] = g
        d_ref[...] = -ADAM_LR * (m_hat / (jnp.sqrt(v_hat) + ADAM_EPS) + ADAM_WD * wv)
        nm_ref[...] = nm
        nv_ref[...] = nv

    row = pl.BlockSpec((tm, 128), lambda i: (i, 0))
    return pl.pallas_call(
        body, name=name, grid=(R // tm,),
        in_specs=[pl.BlockSpec((N_DEV, tm, 128), lambda i: (0, i, 0)), row, row, row],
        out_specs=[row, row, row, row],
        out_shape=[jax.ShapeDtypeStruct((R, 128), F32)] * 4,
        compiler_params=_cparams("parallel"),
    )(recv, w, m, v)


def _pack_rows(arrs):
    return jnp.concatenate([a.reshape(-1, 128) for a in arrs], axis=0)


def _unstack(g, shape, axis):
    t = jnp.moveaxis(g, 0, axis)
    return t.reshape(shape)


def _restack(full, axis):
    s = full.shape
    t = full.reshape(s[:axis] + (N_DEV, s[axis] // N_DEV) + s[axis + 1:])
    return jnp.moveaxis(t, axis, 0)


def _pad_heads(w, heads, hd, axis):
    s = w.shape
    t = w.reshape(s[:axis] + (heads, hd) + s[axis + 1:])
    pad = [(0, 0)] * t.ndim
    pad[axis + 1] = (0, HEAD_PAD - hd)
    t = jnp.pad(t, pad)
    return t.reshape(s[:axis] + (heads * HEAD_PAD,) + s[axis + 1:])


def _unpad_heads(w, heads, hd, axis):
    s = w.shape
    t = w.reshape(s[:axis] + (heads, HEAD_PAD) + s[axis + 1:])
    t = lax.slice_in_dim(t, 0, hd, axis=axis + 1)
    return t.reshape(s[:axis] + (heads * hd,) + s[axis + 1:])


def _layer_weights(full, l):
    w_in = full["w_in"][l]
    cq, kva, qs, ks, vs, qm, gates = (w_in[:, 0:256], w_in[:, 256:416], w_in[:, 416:928], w_in[:, 928:1056],
                                       w_in[:, 1056:1184], w_in[:, 1184:1696], w_in[:, 1696:4768])
    wa = jnp.concatenate([gates, cq, jnp.pad(kva, ((0, 0), (0, 96)))], axis=1)
    wb = jnp.concatenate([_pad_heads(qs, SWA_HEADS, SWA_HD, 1), qm, _pad_heads(ks, SWA_KV_HEADS, SWA_HD, 1),
                          _pad_heads(vs, SWA_KV_HEADS, SWA_HD, 1)], axis=1)
    wuq = _pad_heads(full["w_uq"][l], MLA_HEADS, MLA_NOPE + MLA_ROPE, 1)
    ukv = full["w_ukv"][l].reshape(MLA_KV_LORA, MLA_HEADS, MLA_NOPE + MLA_V)
    wuk = _pad_heads(ukv[:, :, :MLA_NOPE].reshape(MLA_KV_LORA, -1), MLA_HEADS, MLA_NOPE, 1)
    wuv = _pad_heads(ukv[:, :, MLA_NOPE:].reshape(MLA_KV_LORA, -1), MLA_HEADS, MLA_V, 1)
    wo_mla = _pad_heads(full["w_o_mla"][l], MLA_HEADS, MLA_V, 0)
    wo_swa = _pad_heads(full["w_o_swa"][l], SWA_HEADS, SWA_HD, 0)
    wo_mem = full["w_o_mem"][l]
    w = dict(wa=wa, wb=wb, wuq=wuq, wuk=wuk, wuv=wuv, wo_mla=wo_mla, wo_swa=wo_swa, wo_mem=wo_mem,
             wmem=full["w_mem_kv"][l], wout=full["w_out"][l], wup=full["w_up"][l], wdown=full["w_down"][l])
    wt = {k + "_t": v.T for k, v in w.items()}
    wt["wag_t"] = wt["wa_t"][:3072]
    wt["wat_t"] = wt["wa_t"][3072:]
    w.update(wt)
    return w


def _layer_weight_grads(g):
    dwa_g, dwa_t, dwb = g["wag"], g["wat"], g["wb"]
    d_in = jnp.concatenate([
        dwa_t[:, 0:256], dwa_t[:, 256:416],
        _unpad_heads(dwb[:, 0:1024], SWA_HEADS, SWA_HD, 1),
        _unpad_heads(dwb[:, 1536:1792], SWA_KV_HEADS, SWA_HD, 1),
        _unpad_heads(dwb[:, 1792:2048], SWA_KV_HEADS, SWA_HD, 1),
        dwb[:, 1024:1536], dwa_g], axis=1)
    duk = _unpad_heads(g["wuk"], MLA_HEADS, MLA_NOPE, 1).reshape(MLA_KV_LORA, MLA_HEADS, MLA_NOPE)
    duv = _unpad_heads(g["wuv"], MLA_HEADS, MLA_V, 1).reshape(MLA_KV_LORA, MLA_HEADS, MLA_V)
    return dict(
        w_in=d_in,
        w_uq=_unpad_heads(g["wuq"], MLA_HEADS, MLA_NOPE + MLA_ROPE, 1),
        w_ukv=jnp.concatenate([duk, duv], axis=2).reshape(MLA_KV_LORA, -1),
        w_mem_kv=g["wmem"],
        w_o_mla=_unpad_heads(g["wo_mla"], MLA_HEADS, MLA_V, 0),
        w_o_swa=_unpad_heads(g["wo_swa"], SWA_HEADS, SWA_HD, 0),
        w_o_mem=g["wo_mem"], w_out=g["wout"], w_up=g["wup"], w_down=g["wdown"])


def _rope_tables(S):
    pos = jnp.arange(S, dtype=F32)
    inv = 1.0 / (ROPE_THETA ** (jnp.arange(0, MLA_ROPE, 2, dtype=F32) / MLA_ROPE))
    ang = pos[:, None] * inv[None, :]
    cos, sin = jnp.cos(ang), jnp.sin(ang)
    z16 = jnp.zeros((S, 16), F32)
    z32 = jnp.zeros((S, 32), F32)
    c = jnp.concatenate([jnp.ones((S, 64), F32), cos, cos, z32], axis=1)
    ck = jnp.concatenate([jnp.zeros((S, 64), F32), cos, cos, z32], axis=1)
    s1 = jnp.concatenate([jnp.zeros((S, 80), F32), sin, z32], axis=1)
    s2 = jnp.concatenate([jnp.zeros((S, 64), F32), -sin, z16, z32], axis=1)
    return c, ck, s1, s2


def _t5_bucket(dist):
    n = jnp.maximum(dist, 0)
    max_exact = REL_BUCKETS // 2
    nf = jnp.maximum(n, 1).astype(F32)
    large = max_exact + (jnp.log(nf / max_exact) / math.log(REL_MAX_DIST / max_exact)
                         * (REL_BUCKETS - max_exact)).astype(jnp.int32)
    large = jnp.minimum(large, REL_BUCKETS - 1)
    return jnp.where(n < max_exact, n, large)


def _bias_onehot(t):
    a = jnp.arange(t)[None, :, None]
    b = jnp.arange(t)[None, None, :]
    kk = jnp.arange(2)[:, None, None]
    dist = a - b + (1 - kk) * t
    valid = (dist >= 0) & (dist < WINDOW)
    bucket = _t5_bucket(dist)
    onehot = (bucket[None] == jnp.arange(REL_BUCKETS)[:, None, None, None]) & valid[None]
    return (onehot.reshape(REL_BUCKETS, -1).astype(F32),
            jnp.where(valid, 0.0, NEG).astype(F32).reshape(1, -1))


def _rstd(x):
    return lax.rsqrt(jnp.mean(x * x, axis=-1, keepdims=True) + EPS)


def _norm_bwd(dh, x, g):
    r = _rstd(x)
    xh = x * r
    w = dh * g
    dx = r * (w - xh * jnp.mean(w * xh, axis=-1, keepdims=True))
    return dx, jnp.sum(dh * xh, axis=0, keepdims=True)


def _tile_lanes(t, n):
    return jnp.tile(t, (1, n // t.shape[1])) if n != t.shape[1] else t


def _rope_fwd(a, c, s1, s2):
    n = a.shape[1]
    return (a * _tile_lanes(c, n) + pltpu.roll(a, 16, 1) * _tile_lanes(s1, n)
            + pltpu.roll(a, n - 16, 1) * _tile_lanes(s2, n))


def _rope_bwd(d, c, s1, s2):
    n = d.shape[1]
    return (d * _tile_lanes(c, n) + pltpu.roll(d * _tile_lanes(s1, n), n - 16, 1)
            + pltpu.roll(d * _tile_lanes(s2, n), 16, 1))


def _sigmoid(x):
    return 1.0 / (1.0 + jnp.exp(-x))


def _rmsnorm(name, x, g, dtype):
    def fn(xv, gv):
        return ((xv * _rstd(xv)) * gv,)
    return _rowwise(name, fn, [_rows(x), _full(g)], [(x.shape, dtype, "rows")], rows=x.shape[0])[0]


def _residual_norm_bwd(name, dres, dh, x, g):
    def fn(dr, dhv, xv, gv):
        dx, dg = _norm_bwd(dhv, xv, gv)
        return dr + dx, dg
    return _rowwise(name, fn, [_rows(dres), _rows(dh), _rows(x), _full(g)],
                    [(x.shape, F32, "rows"), (g.shape, F32, "acc")], rows=x.shape[0])


def _layer_fwd(l, x, mem, w, p, tabs, swa_bias, S):
    c, ck, s1, s2 = tabs
    n = f"l{l}_"
    h = _rmsnorm(n + "attn_norm", x, p["attn_norm"], BF16)
    proj_a = _mm(n + "proj_a", h, w["wa"], [F32])
    proj_b = _mm(n + "proj_b", h, w["wb"], [BF16])

    def prep(cq, kva, qn, kvn, ckv, s1v, s2v):
        cqn = cq * _rstd(cq) * qn
        ckv_ = kva[:, :128]
        ckvn = ckv_ * _rstd(ckv_) * kvn
        pe = pltpu.roll(kva[:, 128:], 64, 1)
        return cqn, ckvn, _rope_fwd(pe, ckv, s1v, s2v)

    cqn, ckvn, kpe = _rowwise(
        n + "mla_prep", prep,
        [_rows(proj_a, 256, 12), _rows(proj_a, 256, 13), _full(p["mla_q_norm"]), _full(p["mla_kv_norm"]),
         _rows(ck), _rows(s1), _rows(s2)],
        [((S, 256), BF16, "rows"), ((S, 128), BF16, "rows"), ((S, 128), F32, "rows")], rows=S)

    q_mla = _mm(n + "q_mla", cqn, w["wuq"], [BF16], epi=lambda acc, cv, s1v, s2v: (_rope_fwd(acc, cv, s1v, s2v),),
                extras=[(c, "m"), (s1, "m"), (s2, "m")])
    k_mla = _mm(n + "k_mla", ckvn, w["wuk"], [BF16],
                epi=lambda acc, kp: (acc + _tile_lanes(kp, acc.shape[1]),), extras=[(kpe, "m")])
    v_mla = _mm(n + "v_mla", ckvn, w["wuv"], [BF16])
    o_mla, lse_mla = _flash_fwd(n + "mla_fwd", q_mla, k_mla, v_mla, heads=MLA_HEADS, q_off=0, k_off=0, v_off=0,
                                group=1, mode="causal", scale=(MLA_NOPE + MLA_ROPE) ** -0.5, tq=MLA_TQ, tk=MLA_TK)
    o_swa, lse_swa = _flash_fwd(n + "swa_fwd", proj_b, proj_b, proj_b, heads=SWA_HEADS, q_off=0, k_off=12, v_off=14,
                                group=SWA_HEADS // SWA_KV_HEADS, mode="window", scale=SWA_HD ** -0.5,
                                tq=SWA_T, tk=SWA_T, bias=swa_bias, sinks=p["sinks"])
    mn = _rmsnorm(n + "mem_norm", mem, p["mem_norm"], BF16)
    kvm = _mm(n + "kv_mem", mn, w["wmem"], [BF16])
    o_mem, lse_mem = _flash_fwd(n + "mem_fwd", proj_b, kvm, kvm, heads=MEM_HEADS, q_off=8, k_off=0, v_off=4,
                                group=1, mode="full", scale=MEM_HD ** -0.5, tq=MEM_TQ, tk=MEM_LEN)
    t0 = _mm(n + "t_mla", o_mla, w["wo_mla"], [F32])
    t1 = _mm(n + "t_swa", o_swa, w["wo_swa"], [F32])
    t2 = _mm(n + "t_mem", o_mem, w["wo_mem"], [F32])

    def merge(g0, g1, g2, bg, a0, a1, a2):
        y = (_sigmoid(g0 + bg[:, 0:1024]) * a0 + _sigmoid(g1 + bg[:, 1024:2048]) * a1
             + _sigmoid(g2 + bg[:, 2048:3072]) * a2)
        return (y,)

    y = _rowwise(n + "merge", merge,
                 [_rows(proj_a, 1024, 0), _rows(proj_a, 1024, 1), _rows(proj_a, 1024, 2), _full(p["b_gate"]),
                  _rows(t0), _rows(t1), _rows(t2)], [((S, D_MODEL), BF16, "rows")], rows=S)[0]
    x1 = _mm(n + "out_proj", y, w["wout"], [F32], epi=lambda acc, r: (acc + r,), extras=[(x, "mn")])
    h2 = _rmsnorm(n + "mlp_norm", x1, p["mlp_norm"], BF16)
    u, act = _mm(n + "mlp_up", h2, w["wup"], [F32, BF16],
                 epi=lambda acc: (acc, jnp.square(jnp.maximum(acc, 0.0))))
    x2 = _mm(n + "mlp_down", act, w["wdown"], [F32], epi=lambda acc, r: (acc + r,), extras=[(x1, "mn")])
    saved = dict(x=x, h=h, proj_a=proj_a, proj_b=proj_b, cqn=cqn, ckvn=ckvn, q_mla=q_mla, k_mla=k_mla, v_mla=v_mla,
                 o_mla=o_mla, lse_mla=lse_mla, o_swa=o_swa, lse_swa=lse_swa, mn=mn, kvm=kvm, o_mem=o_mem,
                 lse_mem=lse_mem, t0=t0, t1=t1, t2=t2, y=y, x1=x1, h2=h2, u=u, act=act)
    return x2, saved


def _layer_bwd(l, dx2, mem, w, p, tabs, swa_bias, sv, S):
    c, ck, s1, s2 = tabs
    n = f"l{l}_b_"
    gw = {}
    gs = {}
    du = _mm(n + "d_act", dx2, w["wdown_t"], [BF16],
             epi=lambda acc, uv: (acc * (2.0 * jnp.maximum(uv, 0.0)),), extras=[(sv["u"], "mn")])
    gw["wdown"] = _mm_tn(n + "g_wdown", sv["act"], dx2)
    gw["wup"] = _mm_tn(n + "g_wup", sv["h2"], du)
    dh2 = _mm(n + "d_h2", du, w["wup_t"], [F32])
    dx1, gs["mlp_norm"] = _residual_norm_bwd(n + "mlp_norm", dx2, dh2, sv["x1"], p["mlp_norm"])
    gw["wout"] = _mm_tn(n + "g_wout", sv["y"], dx1)
    dy = _mm(n + "d_y", dx1, w["wout_t"], [F32])

    def merge_bwd(dyv, g0, g1, g2, bg, a0, a1, a2):
        outs, dgs = [], []
        for b, (gv, av) in enumerate(((g0, a0), (g1, a1), (g2, a2))):
            sg = _sigmoid(gv + bg[:, b * 1024:(b + 1) * 1024])
            outs.append(dyv * sg)
            dgs.append(dyv * av * sg * (1.0 - sg))
        dg = jnp.concatenate(dgs, axis=1)
        return outs[0], outs[1], outs[2], dg, jnp.sum(dg, axis=0, keepdims=True)

    pa = sv["proj_a"]
    dt0, dt1, dt2, dgates, gs["b_gate"] = _rowwise(
        n + "merge", merge_bwd,
        [_rows(dy), _rows(pa, 1024, 0), _rows(pa, 1024, 1), _rows(pa, 1024, 2), _full(p["b_gate"]),
         _rows(sv["t0"]), _rows(sv["t1"]), _rows(sv["t2"])],
        [((S, D_MODEL), BF16, "rows")] * 3 + [((S, 3 * D_MODEL), BF16, "rows"), ((1, 3 * D_MODEL), F32, "acc")],
        rows=S)
    gw["wo_mla"] = _mm_tn(n + "g_wo_mla", sv["o_mla"], dt0)
    gw["wo_swa"] = _mm_tn(n + "g_wo_swa", sv["o_swa"], dt1)
    gw["wo_mem"] = _mm_tn(n + "g_wo_mem", sv["o_mem"], dt2)
    do_mla = _mm(n + "d_o_mla", dt0, w["wo_mla_t"], [BF16])
    do_swa = _mm(n + "d_o_swa", dt1, w["wo_swa_t"], [BF16])
    do_mem = _mm(n + "d_o_mem", dt2, w["wo_mem_t"], [BF16])
    pb = sv["proj_b"]
    dq_mla, dk_mla, dv_mla = _flash_bwd(
        n + "mla_bwd", sv["q_mla"], sv["k_mla"], sv["v_mla"], sv["o_mla"], do_mla, sv["lse_mla"], heads=MLA_HEADS,
        q_off=0, k_off=0, v_off=0, group=1, mode="causal", scale=(MLA_NOPE + MLA_ROPE) ** -0.5, tq=MLA_TQ, tk=MLA_TK)
    dq_swa, dk_swa, dv_swa, dbias, dsink = _flash_bwd(
        n + "swa_bwd", pb, pb, pb, sv["o_swa"], do_swa, sv["lse_swa"], heads=SWA_HEADS, q_off=0, k_off=12, v_off=14,
        group=SWA_HEADS // SWA_KV_HEADS, mode="window", scale=SWA_HD ** -0.5, tq=SWA_T, tk=SWA_T,
        bias=swa_bias, sinks=p["sinks"])
    dq_mem, dk_mem, dv_mem = _flash_bwd(
        n + "mem_bwd", pb, sv["kvm"], sv["kvm"], sv["o_mem"], do_mem, sv["lse_mem"], heads=MEM_HEADS,
        q_off=8, k_off=0, v_off=4, group=1, mode="full", scale=MEM_HD ** -0.5, tq=MEM_TQ, tk=MEM_LEN)
    gs["dbias"] = dbias
    gs["sinks"] = dsink[:, 0, 0]
    dkvm = jnp.concatenate([dk_mem, dv_mem], axis=1)
    gw["wmem"] = _mm_tn(n + "g_wmem", sv["mn"], dkvm)
    dmn = _mm(n + "d_mn", dkvm, w["wmem_t"], [F32])
    _, gs["mem_norm"] = _residual_norm_bwd(n + "mem_norm", dmn, dmn, mem, p["mem_norm"])
    dq_pre = _rowwise(n + "q_unrope", lambda d, cv, s1v, s2v: (_rope_bwd(d, cv, s1v, s2v),),
                      [_rows(dq_mla), _rows(c), _rows(s1), _rows(s2)], [((S, 1024), BF16, "rows")], rows=S)[0]
    gw["wuq"] = _mm_tn(n + "g_wuq", sv["cqn"], dq_pre)
    gw["wuk"] = _mm_tn(n + "g_wuk", sv["ckvn"], dk_mla)
    gw["wuv"] = _mm_tn(n + "g_wuv", sv["ckvn"], dv_mla)
    dcqn = _mm(n + "d_cqn", dq_pre, w["wuq_t"], [F32])
    dckvn = _mm(n + "d_ckvn_k", dk_mla, w["wuk_t"], [F32])
    dckvn = _mm(n + "d_ckvn_v", dv_mla, w["wuv_t"], [F32], epi=lambda acc, r: (acc + r,), extras=[(dckvn, "mn")])

    def mla_norm_bwd(dcq_n, dckv_n, dk, cq, kva, qn, kvn, ckv, s1v, s2v):
        dcq, dqn = _norm_bwd(dcq_n, cq, qn)
        dckv, dkvn = _norm_bwd(dckv_n, kva[:, :128], kvn)
        dkpe = dk[:, 0:128]
        for hh in range(1, MLA_HEADS):
            dkpe = dkpe + dk[:, hh * 128:(hh + 1) * 128]
        dpe = pltpu.roll(_rope_bwd(dkpe, ckv, s1v, s2v), 64, 1)
        return jnp.concatenate([dcq, dckv, dpe], axis=1), dqn, dkvn

    dtail, gs["mla_q_norm"], gs["mla_kv_norm"] = _rowwise(
        n + "mla_norm", mla_norm_bwd,
        [_rows(dcqn), _rows(dckvn), _rows(dk_mla), _rows(pa, 256, 12), _rows(pa, 256, 13),
         _full(p["mla_q_norm"]), _full(p["mla_kv_norm"]), _rows(ck), _rows(s1), _rows(s2)],
        [((S, 512), BF16, "rows"), ((1, 256), F32, "acc"), ((1, 128), F32, "acc")], rows=S)

    def gather_b(dqs, dqm, dks, dvs):
        r = SWA_HEADS // SWA_KV_HEADS
        parts = [dqs, dqm]
        for t in (dks, dvs):
            for g in range(SWA_KV_HEADS):
                acc = t[:, (g * r) * 128:(g * r + 1) * 128]
                for j in range(1, r):
                    acc = acc + t[:, (g * r + j) * 128:(g * r + j + 1) * 128]
                parts.append(acc)
        return (jnp.concatenate(parts, axis=1),)

    dproj_b = _rowwise(n + "dproj_b", gather_b, [_rows(dq_swa), _rows(dq_mem), _rows(dk_swa), _rows(dv_swa)],
                       [((S, 2048), BF16, "rows")], rows=S)[0]
    h = sv["h"]
    gw["wag"] = _mm_tn(n + "g_wa_gates", h, dgates)
    gw["wat"] = _mm_tn(n + "g_wa_tail", h, dtail)
    gw["wb"] = _mm_tn(n + "g_wb", h, dproj_b)
    dh = _mm(n + "d_h_gates", dgates, w["wag_t"], [F32])
    dh = _mm(n + "d_h_tail", dtail, w["wat_t"], [F32], epi=lambda acc, r: (acc + r,), extras=[(dh, "mn")])
    dh = _mm(n + "d_h_b", dproj_b, w["wb_t"], [F32], epi=lambda acc, r: (acc + r,), extras=[(dh, "mn")])
    dx, gs["attn_norm"] = _residual_norm_bwd(n + "attn_norm", dx1, dh, sv["x"], p["attn_norm"])
    return dx, gw, gs


def _local_step(x, mem, loss_target, full, small):
    S = x.shape[0]
    tabs = _rope_tables(S)
    onehot, band = _bias_onehot(min(SWA_T, S))
    t = min(SWA_T, S)
    hi = lax.Precision.HIGHEST
    swa_bias = _mm("swa_bias", small["rel_bias"].T, onehot, [F32], epi=lambda acc, mk: (acc + mk,),
                   extras=[(band, "n")], cast=None, precision=hi, tn=8192).reshape(SWA_HEADS, 2, t, t)
    ws, ps = [], []
    for l in range(DEPTH):
        ws.append(_layer_weights(full, l))
        ps.append(dict(
            attn_norm=small["attn_norm"][l][None], mem_norm=small["mem_norm"][l][None],
            b_gate=small["b_gate"][l][None], mla_q_norm=small["mla_q_norm"][l][None],
            mla_kv_norm=small["mla_kv_norm"][l][None], mlp_norm=small["mlp_norm"][l][None],
            sinks=jnp.broadcast_to(small["attn_sinks"][l][:, None, None], (SWA_HEADS, 8, 128))))
    saved = []
    xc = x
    for l in range(DEPTH):
        xc, sv = _layer_fwd(l, xc, mem, ws[l], ps[l], tabs, swa_bias, S)
        saved.append(sv)

    fn_g = small["final_norm"][None]

    def loss_fn(xv, gv, tv):
        r = _rstd(xv)
        xh = xv * r
        err = xh * gv - tv
        dyv = err * (1.0 / D_MODEL)
        wv = dyv * gv
        dx = r * (wv - xh * jnp.mean(wv * xh, axis=-1, keepdims=True))
        part = 0.5 * jnp.sum(err * err) * (1.0 / D_MODEL)
        return dx, jnp.sum(dyv * xh, axis=0, keepdims=True), jnp.zeros((8, 128), F32) + part

    dx, g_final, loss_acc = _rowwise(
        "loss", loss_fn, [_rows(xc), _full(fn_g), _rows(loss_target)],
        [((S, D_MODEL), F32, "rows"), ((1, D_MODEL), F32, "acc"), ((8, 128), F32, "acc")], rows=S)

    gws, gss = [None] * DEPTH, [None] * DEPTH
    for l in reversed(range(DEPTH)):
        dx, gw, gs = _layer_bwd(l, dx, mem, ws[l], ps[l], tabs, swa_bias, saved[l], S)
        gws[l] = _layer_weight_grads(gw)
        gss[l] = gs

    dbias = (gss[0]["dbias"] + gss[1]["dbias"]).reshape(SWA_HEADS, -1)
    g_rel = _mm("g_rel_bias", dbias, onehot.T, [F32], cast=None, precision=hi, tk=8192).T
    wgrads = {k: jnp.stack([gws[l][k] for l in range(DEPTH)]) for k in gws[0]}
    sgrads = dict(
        rel_bias=g_rel,
        final_norm=g_final[0],
        attn_sinks=jnp.stack([gss[l]["sinks"] for l in range(DEPTH)]),
        **{k: jnp.concatenate([gss[l][k] for l in range(DEPTH)], axis=0)
           for k in ("attn_norm", "mem_norm", "b_gate", "mla_q_norm", "mla_kv_norm", "mlp_norm")})
    return loss_acc[0, 0], dx, wgrads, sgrads


def _pack_small(vals, loss):
    rows = []
    for name, shape in SMALL:
        flat = vals[name].astype(F32).reshape(-1)
        pad = (-flat.shape[0]) % 1024
        rows.append(jnp.pad(flat, (0, pad)).reshape(-1, 128))
    rows.append(jnp.zeros((8, 128), F32) + loss)
    return jnp.concatenate(rows, axis=0)


def _unpack_small(packed):
    out, r = {}, 0
    for name, shape in SMALL:
        size = math.prod(shape)
        nrows = 8 * -(-size // 1024)
        out[name] = packed[r:r + nrows].reshape(-1)[:size].reshape(shape)
        r += nrows
    return out, packed[r, 0]


def kernel(x, mem, rel_bias, attn_norm, mem_norm, w_in, b_gate, mla_q_norm, w_uq, mla_kv_norm, w_ukv, attn_sinks, w_mem_kv, w_o_mla, w_o_swa, w_o_mem, w_out, mlp_norm, w_up, w_down, final_norm, loss_target, m_rel_bias, m_attn_norm, m_mem_norm, m_w_in, m_b_gate, m_mla_q_norm, m_w_uq, m_mla_kv_norm, m_w_ukv, m_attn_sinks, m_w_mem_kv, m_w_o_mla, m_w_o_swa, m_w_o_mem, m_w_out, m_mlp_norm, m_w_up, m_w_down, m_final_norm, v_rel_bias, v_attn_norm, v_mem_norm, v_w_in, v_b_gate, v_mla_q_norm, v_w_uq, v_mla_kv_norm, v_w_ukv, v_attn_sinks, v_w_mem_kv, v_w_o_mla, v_w_o_swa, v_w_o_mem, v_w_out, v_mlp_norm, v_w_up, v_w_down, v_final_norm):
    wv = dict(rel_bias=rel_bias, attn_norm=attn_norm, mem_norm=mem_norm, w_in=w_in, b_gate=b_gate,
              mla_q_norm=mla_q_norm, w_uq=w_uq, mla_kv_norm=mla_kv_norm, w_ukv=w_ukv, attn_sinks=attn_sinks,
              w_mem_kv=w_mem_kv, w_o_mla=w_o_mla, w_o_swa=w_o_swa, w_o_mem=w_o_mem, w_out=w_out,
              mlp_norm=mlp_norm, w_up=w_up, w_down=w_down, final_norm=final_norm)
    mv = dict(rel_bias=m_rel_bias, attn_norm=m_attn_norm, mem_norm=m_mem_norm, w_in=m_w_in, b_gate=m_b_gate,
              mla_q_norm=m_mla_q_norm, w_uq=m_w_uq, mla_kv_norm=m_mla_kv_norm, w_ukv=m_w_ukv,
              attn_sinks=m_attn_sinks, w_mem_kv=m_w_mem_kv, w_o_mla=m_w_o_mla, w_o_swa=m_w_o_swa,
              w_o_mem=m_w_o_mem, w_out=m_w_out, mlp_norm=m_mlp_norm, w_up=m_w_up, w_down=m_w_down,
              final_norm=m_final_norm)
    vv = dict(rel_bias=v_rel_bias, attn_norm=v_attn_norm, mem_norm=v_mem_norm, w_in=v_w_in, b_gate=v_b_gate,
              mla_q_norm=v_mla_q_norm, w_uq=v_w_uq, mla_kv_norm=v_mla_kv_norm, w_ukv=v_w_ukv,
              attn_sinks=v_attn_sinks, w_mem_kv=v_w_mem_kv, w_o_mla=v_w_o_mla, w_o_swa=v_w_o_swa,
              w_o_mem=v_w_o_mem, w_out=v_w_out, mlp_norm=v_mlp_norm, w_up=v_w_up, w_down=v_w_down,
              final_norm=v_final_norm)

    shard_rows = [math.prod(_shard_shape(shape, axis)) // 128 for _, shape, axis in WSPECS]
    gathered = _exchange("gather_weights", _pack_rows([wv[name].astype(BF16) for name, _, _ in WSPECS]),
                         per_peer=False)
    full, r = {}, 0
    for (name, shape, axis), nr in zip(WSPECS, shard_rows):
        full[name] = _unstack(gathered[:, r:r + nr].reshape((N_DEV,) + _shard_shape(shape, axis)), shape, axis)
        r += nr

    loss_part, grad_x, wgrads, sgrads = _local_step(x[0], mem[0], loss_target[0], full,
                                                    {name: wv[name] for name, _ in SMALL})

    send = jnp.concatenate([_restack(wgrads[name], axis).astype(BF16).reshape(N_DEV, -1, 128)
                            for name, _, axis in WSPECS], axis=1)
    recv = _exchange("scatter_grads", send, per_peer=True)
    outs = _adam("adam_sharded", recv, *[_pack_rows([d[name] for name, _, _ in WSPECS]) for d in (wv, mv, vv)])
    res = {}
    r = 0
    for (name, shape, axis), nr in zip(WSPECS, shard_rows):
        res[name] = [o[r:r + nr].reshape(_shard_shape(shape, axis)) for o in outs]
        r += nr

    small_recv = _exchange("gather_small", _pack_small(sgrads, loss_part), per_peer=False)
    zero = jnp.zeros((), F32)
    souts = _adam("adam_small", small_recv, *[_pack_small(d, zero) for d in (wv, mv, vv)])
    loss = None
    for i, o in enumerate(souts):
        vals, extra = _unpack_small(o)
        if i == 0:
            loss = extra
        for name, _ in SMALL:
            res.setdefault(name, []).append(vals[name])

    out = [loss, grad_x[None]]
    for i in range(4):
        out.extend(res[name][i] for name in WEIGHT_ORDER)
    return tuple(out)
```

```python
import math

import jax
import jax.numpy as jnp
from jax import lax
from jax.experimental import pallas as pl
from jax.experimental.pallas import tpu as pltpu

F32 = jnp.float32
BF16 = jnp.bfloat16

N_DEV = 8
D_MODEL = 1024
DEPTH = 2
MLA_HEADS = 8
MLA_Q_LORA = 256
MLA_KV_LORA = 128
MLA_NOPE = 64
MLA_ROPE = 32
MLA_V = 64
ROPE_THETA = 10000.0
SWA_HEADS = 8
SWA_KV_HEADS = 2
SWA_HD = 64
WINDOW = 128
REL_BUCKETS = 32
REL_MAX_DIST = 128
MEM_LEN = 256
MEM_HEADS = 4
MEM_HD = 128
D_FF = 4 * D_MODEL
EPS = 1e-6
HEAD_PAD = 128
ADAM_LR = 0.001
ADAM_B1 = 0.9
ADAM_B2 = 0.999
ADAM_EPS = 1e-08
ADAM_WD = 0.01
ADAM_STEP = 10

NEG = -1e30
VMEM_LIMIT = 48 * 1024 * 1024

MM_TM = 1024
MM_TN = 512
MM_TK = 1024
TN_T1 = 512
TN_TN = 1024
TN_TS = 1024
ROW_TM = 256
MLA_TILE = 1024
MLA_CHUNK = 256
MLA_SCALE = (MLA_NOPE + MLA_ROPE) ** -0.5
DEN_LANE = MLA_V
SWA_TQ = 512
MEM_TQ = 1024
ADAM_TM = 1200

WSPECS = (
    ("w_in", (DEPTH, D_MODEL, 4768), 2),
    ("w_uq", (DEPTH, MLA_Q_LORA, 768), 2),
    ("w_ukv", (DEPTH, MLA_KV_LORA, 1024), 2),
    ("w_mem_kv", (DEPTH, D_MODEL, 1024), 1),
    ("w_o_mla", (DEPTH, 512, D_MODEL), 2),
    ("w_o_swa", (DEPTH, 512, D_MODEL), 2),
    ("w_o_mem", (DEPTH, 512, D_MODEL), 2),
    ("w_out", (DEPTH, D_MODEL, D_MODEL), 1),
    ("w_up", (DEPTH, D_MODEL, D_FF), 2),
    ("w_down", (DEPTH, D_FF, D_MODEL), 1),
)
SMALL = (
    ("rel_bias", (REL_BUCKETS, SWA_HEADS)),
    ("attn_norm", (DEPTH, D_MODEL)),
    ("mem_norm", (DEPTH, D_MODEL)),
    ("b_gate", (DEPTH, 3 * D_MODEL)),
    ("mla_q_norm", (DEPTH, MLA_Q_LORA)),
    ("mla_kv_norm", (DEPTH, MLA_KV_LORA)),
    ("attn_sinks", (DEPTH, SWA_HEADS)),
    ("mlp_norm", (DEPTH, D_MODEL)),
    ("final_norm", (D_MODEL,)),
)
WEIGHT_ORDER = ("rel_bias", "attn_norm", "mem_norm", "w_in", "b_gate", "mla_q_norm", "w_uq", "mla_kv_norm",
                "w_ukv", "attn_sinks", "w_mem_kv", "w_o_mla", "w_o_swa", "w_o_mem", "w_out", "mlp_norm",
                "w_up", "w_down", "final_norm")


def _cparams(*sem):
    return pltpu.CompilerParams(dimension_semantics=sem, vmem_limit_bytes=VMEM_LIMIT)


def _shard_shape(shape, axis):
    s = list(shape)
    s[axis] //= N_DEV
    return tuple(s)


def _mm(name, a, b, out_dtypes, *, epi=None, extras=(), a_fn=None, cast=BF16, precision=None,
        tm=None, tn=None, tk=None):
    M, K = a.shape
    K2, N = b.shape
    assert K == K2, (name, a.shape, b.shape)
    tm = min(tm or MM_TM, M)
    tn = min(tn or MM_TN, N)
    tk = min(tk or MM_TK, K)
    assert M % tm == 0 and N % tn == 0 and K % tk == 0, (name, a.shape, b.shape, tm, tn, tk)
    nk = K // tk
    n_ex = len(extras)
    n_out = len(out_dtypes)

    def body(*refs):
        a_ref, b_ref = refs[0], refs[1]
        ex_refs = refs[2:2 + n_ex]
        out_refs = refs[2 + n_ex:2 + n_ex + n_out]
        av = a_ref[...]
        if a_fn is not None:
            av = a_fn(av)
        bv = b_ref[...]
        if cast is not None:
            av = av.astype(cast)
            bv = bv.astype(cast)
        part = jnp.dot(av, bv, preferred_element_type=F32, precision=precision)

        def finish(acc):
            outs = epi(acc, *[r[...] for r in ex_refs]) if epi is not None else (acc,)
            for r, o in zip(out_refs, outs):
                r[...] = o.astype(r.dtype)

        if nk == 1:
            finish(part)
        else:
            acc_ref = refs[-1]
            k = pl.program_id(2)

            @pl.when(k == 0)
            def _():
                acc_ref[...] = part

            @pl.when(k > 0)
            def _():
                acc_ref[...] += part

            @pl.when(k == nk - 1)
            def _():
                finish(acc_ref[...])

    in_specs = [pl.BlockSpec((tm, tk), lambda i, j, k: (i, k)),
                pl.BlockSpec((tk, tn), lambda i, j, k: (k, j))]
    for arr, kind in extras:
        if kind == "mn":
            in_specs.append(pl.BlockSpec((tm, tn), lambda i, j, k: (i, j)))
        elif kind == "m":
            in_specs.append(pl.BlockSpec((tm, arr.shape[1]), lambda i, j, k: (i, 0)))
        else:
            in_specs.append(pl.BlockSpec((1, tn), lambda i, j, k: (0, j)))
    outs = pl.pallas_call(
        body, name=name, grid=(M // tm, N // tn, nk),
        in_specs=in_specs,
        out_specs=[pl.BlockSpec((tm, tn), lambda i, j, k: (i, j)) for _ in out_dtypes],
        out_shape=[jax.ShapeDtypeStruct((M, N), dt) for dt in out_dtypes],
        scratch_shapes=[pltpu.VMEM((tm, tn), F32)] if nk > 1 else [],
        compiler_params=_cparams("parallel", "parallel", "arbitrary"),
    )(a, b, *[arr for arr, _ in extras])
    return outs[0] if n_out == 1 else outs


def _mm_tn(name, a, b, *, t1=None, tn=None, ts=None):
    S, K1 = a.shape
    S2, N = b.shape
    assert S == S2, (name, a.shape, b.shape)
    t1 = min(t1 or TN_T1, K1)
    tn = min(tn or TN_TN, N)
    ts = min(ts or TN_TS, S)
    assert K1 % t1 == 0 and N % tn == 0 and S % ts == 0, (name, a.shape, b.shape)

    def body(a_ref, b_ref, o_ref):
        s = pl.program_id(2)
        part = lax.dot_general(a_ref[...].astype(BF16), b_ref[...].astype(BF16),
                               (((0,), (0,)), ((), ())), preferred_element_type=F32)

        @pl.when(s == 0)
        def _():
            o_ref[...] = part

        @pl.when(s > 0)
        def _():
            o_ref[...] += part

    return pl.pallas_call(
        body, name=name, grid=(K1 // t1, N // tn, S // ts),
        in_specs=[pl.BlockSpec((ts, t1), lambda i, j, s: (s, i)),
                  pl.BlockSpec((ts, tn), lambda i, j, s: (s, j))],
        out_specs=pl.BlockSpec((t1, tn), lambda i, j, s: (i, j)),
        out_shape=jax.ShapeDtypeStruct((K1, N), F32),
        compiler_params=_cparams("parallel", "parallel", "arbitrary"),
    )(a, b)


def _rows(arr, width=None, blk=0):
    return (arr, ("rows", arr.shape[1] if width is None else width, blk))


def _full(arr):
    return (arr, ("full",))


def _rowwise(name, fn, ins, outs, *, rows, tm=None):
    tm = min(tm or ROW_TM, rows)
    assert rows % tm == 0, (name, rows, tm)
    n_in = len(ins)

    def body(*refs):
        i = pl.program_id(0)
        vals = fn(*[r[...] for r in refs[:n_in]])
        for (shape, dt, kind), r, v in zip(outs, refs[n_in:], vals):
            if kind == "rows":
                r[...] = v.astype(dt)
            else:
                @pl.when(i == 0)
                def _(r=r, v=v):
                    r[...] = v

                @pl.when(i > 0)
                def _(r=r, v=v):
                    r[...] += v

    in_specs = []
    for arr, spec in ins:
        if spec[0] == "rows":
            in_specs.append(pl.BlockSpec((tm, spec[1]), lambda i, b=spec[2]: (i, b)))
        else:
            in_specs.append(pl.BlockSpec(arr.shape, lambda i, n=arr.ndim: (0,) * n))
    out_specs = []
    for shape, dt, kind in outs:
        if kind == "rows":
            out_specs.append(pl.BlockSpec((tm, shape[1]), lambda i: (i, 0)))
        else:
            out_specs.append(pl.BlockSpec(shape, lambda i, n=len(shape): (0,) * n))
    res = pl.pallas_call(
        body, name=name, grid=(rows // tm,),
        in_specs=in_specs, out_specs=out_specs,
        out_shape=[jax.ShapeDtypeStruct(shape, dt) for shape, dt, _ in outs],
        compiler_params=_cparams("arbitrary"),
    )(*[arr for arr, _ in ins])
    return res


def _flash_fwd(name, q_arr, k_arr, v_arr, *, heads, q_off, k_off, v_off, group, mode, scale, tq, tk,
               bias=None, sinks=None):
    S = q_arr.shape[0]
    Sk = k_arr.shape[0]
    tq = min(tq, S)
    tk = min(tk, Sk)
    nq = S // tq
    if mode == "window":
        assert tq == tk
        nsteps = 2
    else:
        nsteps = Sk // tk

    def kblock(qi, kk):
        if mode == "causal":
            return jnp.minimum(kk, (qi * tq + tq - 1) // tk)
        if mode == "window":
            return jnp.maximum(qi + kk - 1, 0)
        return kk

    has_bias = bias is not None
    has_sink = sinks is not None

    def body(*refs):
        q_ref, k_ref, v_ref = refs[:3]
        pos = 3
        bias_ref = sink_ref = None
        if has_bias:
            bias_ref = refs[pos]
            pos += 1
        if has_sink:
            sink_ref = refs[pos]
            pos += 1
        o_ref, lse_ref, m_sc, l_sc, acc_sc = refs[pos:pos + 5]
        qi = pl.program_id(1)
        kk = pl.program_id(2)

        @pl.when(kk == 0)
        def _():
            if has_sink:
                m_sc[...] = jnp.zeros(m_sc.shape, F32) + sink_ref[0, 0:1, 0:1]
                l_sc[...] = jnp.ones(l_sc.shape, F32)
            else:
                m_sc[...] = jnp.full(m_sc.shape, NEG, F32)
                l_sc[...] = jnp.zeros(l_sc.shape, F32)
            acc_sc[...] = jnp.zeros(acc_sc.shape, F32)

        if mode == "causal":
            run = kk <= (qi * tq + tq - 1) // tk
        elif mode == "window":
            run = qi + kk >= 1
        else:
            run = None

        def step():
            s = lax.dot_general(q_ref[...], k_ref[...], (((1,), (1,)), ((), ())),
                                preferred_element_type=F32) * scale
            if has_bias:
                s = s + bias_ref[0, 0]
            if mode == "causal":
                rows = qi * tq + lax.broadcasted_iota(jnp.int32, (tq, tk), 0)
                cols = kk * tk + lax.broadcasted_iota(jnp.int32, (tq, tk), 1)
                s = jnp.where(cols <= rows, s, NEG)
            m_prev = m_sc[...]
            m_new = jnp.maximum(m_prev, jnp.max(s, axis=1, keepdims=True))
            alpha = jnp.exp(m_prev - m_new)
            p = jnp.exp(s - m_new)
            l_sc[...] = alpha * l_sc[...] + jnp.sum(p, axis=1, keepdims=True)
            acc_sc[...] = alpha * acc_sc[...] + jnp.dot(p.astype(BF16), v_ref[...],
                                                        preferred_element_type=F32)
            m_sc[...] = m_new

        if run is None:
            step()
        else:
            pl.when(run)(step)

        @pl.when(kk == nsteps - 1)
        def _():
            l = l_sc[...]
            o_ref[...] = (acc_sc[...] / l).astype(o_ref.dtype)
            lse_ref[0] = m_sc[...] + jnp.log(l)

    in_specs = [
        pl.BlockSpec((tq, HEAD_PAD), lambda h, qi, kk: (qi, q_off + h)),
        pl.BlockSpec((tk, HEAD_PAD), lambda h, qi, kk: (kblock(qi, kk), k_off + h // group)),
        pl.BlockSpec((tk, HEAD_PAD), lambda h, qi, kk: (kblock(qi, kk), v_off + h // group)),
    ]
    args = [q_arr, k_arr, v_arr]
    if has_bias:
        in_specs.append(pl.BlockSpec((1, 1, tq, tk), lambda h, qi, kk: (h, kk, 0, 0)))
        args.append(bias)
    if has_sink:
        in_specs.append(pl.BlockSpec((1, 8, 128), lambda h, qi, kk: (h, 0, 0)))
        args.append(sinks)
    o, lse = pl.pallas_call(
        body, name=name, grid=(heads, nq, nsteps),
        in_specs=in_specs,
        out_specs=[pl.BlockSpec((tq, HEAD_PAD), lambda h, qi, kk: (qi, h)),
                   pl.BlockSpec((1, tq, 1), lambda h, qi, kk: (h, qi, 0))],
        out_shape=[jax.ShapeDtypeStruct((S, heads * HEAD_PAD), BF16),
                   jax.ShapeDtypeStruct((heads, S, 1), F32)],
        scratch_shapes=[pltpu.VMEM((tq, 1), F32), pltpu.VMEM((tq, 1), F32), pltpu.VMEM((tq, HEAD_PAD), F32)],
        compiler_params=_cparams("parallel", "parallel", "arbitrary"),
    )(*args)
    return o, lse


def _flash_bwd(name, q_arr, k_arr, v_arr, o_arr, do_arr, lse, *, heads, q_off, k_off, v_off, group, mode,
               scale, tq, tk, bias=None, sinks=None):
    S = q_arr.shape[0]
    Sk = k_arr.shape[0]
    tq = min(tq, S)
    tk = min(tk, Sk)
    nq = S // tq
    nkb = Sk // tk
    if mode == "window":
        assert tq == tk
        nsteps = 2
    else:
        nsteps = nq

    def qblock(kj, qq):
        if mode == "causal":
            return jnp.maximum(qq, (kj * tk) // tq)
        if mode == "window":
            return jnp.minimum(kj + qq, nq - 1)
        return qq

    has_bias = bias is not None
    has_sink = sinks is not None

    def body(*refs):
        q_ref, k_ref, v_ref, o_ref, do_ref, lse_ref = refs[:6]
        pos = 6
        bias_ref = sink_ref = None
        if has_bias:
            bias_ref = refs[pos]
            pos += 1
        if has_sink:
            sink_ref = refs[pos]
            pos += 1
        dq_ref, dk_ref, dv_ref = refs[pos:pos + 3]
        pos += 3
        dbias_ref = dsink_ref = None
        if has_bias:
            dbias_ref = refs[pos]
            pos += 1
        if has_sink:
            dsink_ref = refs[pos]
            pos += 1
        dk_sc, dv_sc = refs[pos:pos + 2]
        kj = pl.program_id(1)
        qq = pl.program_id(2)
        qb = qblock(kj, qq)

        @pl.when((kj == 0) & (qq == 0))
        def _():
            dq_ref[...] = jnp.zeros(dq_ref.shape, F32)
            if has_bias:
                dbias_ref[...] = jnp.zeros(dbias_ref.shape, F32)
            if has_sink:
                dsink_ref[...] = jnp.zeros(dsink_ref.shape, F32)

        @pl.when(qq == 0)
        def _():
            dk_sc[...] = jnp.zeros(dk_sc.shape, F32)
            dv_sc[...] = jnp.zeros(dv_sc.shape, F32)

        if mode == "causal":
            run = qq >= (kj * tk) // tq
        elif mode == "window":
            run = kj + qq <= nq - 1
        else:
            run = None

        def step():
            q = q_ref[...]
            k = k_ref[...]
            do = do_ref[...]
            lse_v = lse_ref[0]
            s = lax.dot_general(q, k, (((1,), (1,)), ((), ())), preferred_element_type=F32) * scale
            if has_bias:
                s = s + bias_ref[0, 0]
            if mode == "causal":
                rows = qb * tq + lax.broadcasted_iota(jnp.int32, (tq, tk), 0)
                cols = kj * tk + lax.broadcasted_iota(jnp.int32, (tq, tk), 1)
                s = jnp.where(cols <= rows, s, NEG)
            p = jnp.exp(s - lse_v)
            delta = jnp.sum(do.astype(F32) * o_ref[...].astype(F32), axis=1, keepdims=True)
            dv_sc[...] += lax.dot_general(p.astype(BF16), do, (((0,), (0,)), ((), ())),
                                          preferred_element_type=F32)
            dp = lax.dot_general(do, v_ref[...], (((1,), (1,)), ((), ())), preferred_element_type=F32)
            dsp = p * (dp - delta)
            if has_bias:
                @pl.when(qq == 0)
                def _():
                    dbias_ref[0, 1] += dsp

                @pl.when(qq == 1)
                def _():
                    dbias_ref[0, 0] += dsp
            ds = (dsp * scale).astype(BF16)
            row0 = pl.multiple_of(qb * tq, tq)
            dq_ref[pl.ds(row0, tq), :] += jnp.dot(ds, k, preferred_element_type=F32)
            dk_sc[...] += lax.dot_general(ds, q, (((0,), (0,)), ((), ())), preferred_element_type=F32)
            if has_sink:
                @pl.when(qq == 0)
                def _():
                    ps = jnp.exp(sink_ref[0, 0:1, 0:1] - lse_v)
                    dsink_ref[...] += jnp.zeros(dsink_ref.shape, F32) - jnp.sum(ps * delta)

        if run is None:
            step()
        else:
            pl.when(run)(step)

        @pl.when(qq == nsteps - 1)
        def _():
            dk_ref[...] = dk_sc[...]
            dv_ref[...] = dv_sc[...]

    def bias_blk(h, kj, qq):
        return (h, 1 - qq, 0, 0)

    in_specs = [
        pl.BlockSpec((tq, HEAD_PAD), lambda h, kj, qq: (qblock(kj, qq), q_off + h)),
        pl.BlockSpec((tk, HEAD_PAD), lambda h, kj, qq: (kj, k_off + h // group)),
        pl.BlockSpec((tk, HEAD_PAD), lambda h, kj, qq: (kj, v_off + h // group)),
        pl.BlockSpec((tq, HEAD_PAD), lambda h, kj, qq: (qblock(kj, qq), h)),
        pl.BlockSpec((tq, HEAD_PAD), lambda h, kj, qq: (qblock(kj, qq), h)),
        pl.BlockSpec((1, tq, 1), lambda h, kj, qq: (h, qblock(kj, qq), 0)),
    ]
    args = [q_arr, k_arr, v_arr, o_arr, do_arr, lse]
    out_specs = [
        pl.BlockSpec((S, HEAD_PAD), lambda h, kj, qq: (0, h)),
        pl.BlockSpec((tk, HEAD_PAD), lambda h, kj, qq: (kj, h)),
        pl.BlockSpec((tk, HEAD_PAD), lambda h, kj, qq: (kj, h)),
    ]
    out_shape = [jax.ShapeDtypeStruct((S, heads * HEAD_PAD), F32),
                 jax.ShapeDtypeStruct((Sk, heads * HEAD_PAD), F32),
                 jax.ShapeDtypeStruct((Sk, heads * HEAD_PAD), F32)]
    if has_bias:
        in_specs.append(pl.BlockSpec((1, 1, tq, tk), bias_blk))
        args.append(bias)
        out_specs.append(pl.BlockSpec((1, 2, tq, tk), lambda h, kj, qq: (h, 0, 0, 0)))
        out_shape.append(jax.ShapeDtypeStruct((heads, 2, tq, tk), F32))
    if has_sink:
        in_specs.append(pl.BlockSpec((1, 8, 128), lambda h, kj, qq: (h, 0, 0)))
        args.append(sinks)
        out_specs.append(pl.BlockSpec((1, 8, 128), lambda h, kj, qq: (h, 0, 0)))
        out_shape.append(jax.ShapeDtypeStruct((heads, 8, 128), F32))
    return pl.pallas_call(
        body, name=name, grid=(heads, nkb, nsteps),
        in_specs=in_specs, out_specs=out_specs, out_shape=out_shape,
        scratch_shapes=[pltpu.VMEM((tk, HEAD_PAD), F32), pltpu.VMEM((tk, HEAD_PAD), F32)],
        compiler_params=_cparams("arbitrary", "arbitrary", "arbitrary"),
    )(*args)


def _causal_fwd(name, q_arr, k_arr, v_arr, *, heads, tile, chunk):
    S = q_arr.shape[0]
    T = min(tile, S)
    C = min(chunk, T)
    nt = S // T
    nc = T // C

    def body(q_ref, k_ref, v_ref, o_ref, lse_ref, m_sc, acc_sc):
        qi = pl.program_id(1)
        kk = pl.program_id(2)

        @pl.when(kk == 0)
        def _():
            m_sc[...] = jnp.full(m_sc.shape, NEG, F32)
            acc_sc[...] = jnp.zeros(acc_sc.shape, F32)

        def chunk_step(c, ncols, masked):
            rows = pl.ds(c * C, C)
            s = lax.dot_general(q_ref[rows, :], k_ref[0:ncols, :], (((1,), (1,)), ((), ())),
                                preferred_element_type=F32)
            if masked:
                r = c * C + lax.broadcasted_iota(jnp.int32, (C, ncols), 0)
                cidx = lax.broadcasted_iota(jnp.int32, (C, ncols), 1)
                s = jnp.where(cidx <= r, s, NEG)
            m_prev = m_sc[rows, :]
            m_new = jnp.maximum(m_prev, jnp.max(s, axis=1, keepdims=True))
            p = jnp.exp(s - m_new).astype(BF16)
            acc_sc[rows, :] = jnp.exp(m_prev - m_new) * acc_sc[rows, :] + jnp.dot(
                p, v_ref[0:ncols, :], preferred_element_type=F32)
            m_sc[rows, :] = m_new

        @pl.when(kk < qi)
        def _():
            for c in range(nc):
                chunk_step(c, T, False)

        @pl.when(kk == qi)
        def _():
            for c in range(nc):
                chunk_step(c, (c + 1) * C, True)

        @pl.when(kk == nt - 1)
        def _():
            acc = acc_sc[...]
            l = acc[:, DEN_LANE:DEN_LANE + 1]
            o_ref[...] = (acc / l).astype(o_ref.dtype)
            lse_ref[0] = m_sc[...] + jnp.log(l)

    return pl.pallas_call(
        body, name=name, grid=(heads, nt, nt),
        in_specs=[pl.BlockSpec((T, HEAD_PAD), lambda h, qi, kk: (qi, h)),
                  pl.BlockSpec((T, HEAD_PAD), lambda h, qi, kk: (jnp.minimum(kk, qi), h)),
                  pl.BlockSpec((T, HEAD_PAD), lambda h, qi, kk: (jnp.minimum(kk, qi), h))],
        out_specs=[pl.BlockSpec((T, HEAD_PAD), lambda h, qi, kk: (qi, h)),
                   pl.BlockSpec((1, T, 1), lambda h, qi, kk: (h, qi, 0))],
        out_shape=[jax.ShapeDtypeStruct((S, heads * HEAD_PAD), BF16),
                   jax.ShapeDtypeStruct((heads, S, 1), F32)],
        scratch_shapes=[pltpu.VMEM((T, 1), F32), pltpu.VMEM((T, HEAD_PAD), F32)],
        compiler_params=_cparams("parallel", "parallel", "arbitrary"),
    )(q_arr, k_arr, v_arr)


def _row_dot(name, a, b, *, heads, tm):
    S = a.shape[0]
    tm = min(tm, S)

    def body(a_ref, b_ref, o_ref):
        o_ref[0] = jnp.sum(a_ref[...].astype(F32) * b_ref[...].astype(F32), axis=1, keepdims=True)

    return pl.pallas_call(
        body, name=name, grid=(heads, S // tm),
        in_specs=[pl.BlockSpec((tm, HEAD_PAD), lambda h, i: (i, h)),
                  pl.BlockSpec((tm, HEAD_PAD), lambda h, i: (i, h))],
        out_specs=pl.BlockSpec((1, tm, 1), lambda h, i: (h, i, 0)),
        out_shape=jax.ShapeDtypeStruct((heads, S, 1), F32),
        compiler_params=_cparams("parallel", "parallel"),
    )(a, b)


def _causal_bwd(name, q_arr, k_arr, v_arr, do_arr, lse, delta, *, heads, tile, chunk):
    S = q_arr.shape[0]
    T = min(tile, S)
    C = min(chunk, T)
    nt = S // T
    nc = T // C

    def body(q_ref, k_ref, v_ref, do_ref, lse_ref, delta_ref, dq_ref, dk_ref, dv_ref, dk_sc, dv_sc):
        kj = pl.program_id(1)
        qq = pl.program_id(2)
        qb = jnp.maximum(qq, kj)

        @pl.when((kj == 0) & (qq == 0))
        def _():
            dq_ref[...] = jnp.zeros(dq_ref.shape, F32)

        @pl.when(qq == 0)
        def _():
            dk_sc[...] = jnp.zeros(dk_sc.shape, F32)
            dv_sc[...] = jnp.zeros(dv_sc.shape, F32)

        def chunk_step(c, ncols, masked):
            rows = pl.ds(c * C, C)
            q = q_ref[rows, :]
            do = do_ref[rows, :]
            k = k_ref[0:ncols, :]
            s = lax.dot_general(q, k, (((1,), (1,)), ((), ())), preferred_element_type=F32)
            if masked:
                r = c * C + lax.broadcasted_iota(jnp.int32, (C, ncols), 0)
                cidx = lax.broadcasted_iota(jnp.int32, (C, ncols), 1)
                s = jnp.where(cidx <= r, s, NEG)
            p = jnp.exp(s - lse_ref[0, rows, :])
            dp = lax.dot_general(do, v_ref[0:ncols, :], (((1,), (1,)), ((), ())), preferred_element_type=F32)
            ds = (p * (dp - delta_ref[0, rows, :])).astype(BF16)
            dv_sc[0:ncols, :] += lax.dot_general(p.astype(BF16), do, (((0,), (0,)), ((), ())),
                                                 preferred_element_type=F32)
            dk_sc[0:ncols, :] += lax.dot_general(ds, q, (((0,), (0,)), ((), ())), preferred_element_type=F32)
            row0 = pl.multiple_of(qb * T + c * C, C)
            dq_ref[pl.ds(row0, C), :] += jnp.dot(ds, k, preferred_element_type=F32)

        @pl.when(qq > kj)
        def _():
            for c in range(nc):
                chunk_step(c, T, False)

        @pl.when(qq == kj)
        def _():
            for c in range(nc):
                chunk_step(c, (c + 1) * C, True)

        @pl.when(qq == nt - 1)
        def _():
            dk_ref[...] = dk_sc[...]
            dv_ref[...] = dv_sc[...]

    qrow = lambda h, kj, qq: (jnp.maximum(qq, kj), h)
    return pl.pallas_call(
        body, name=name, grid=(heads, nt, nt),
        in_specs=[pl.BlockSpec((T, HEAD_PAD), qrow),
                  pl.BlockSpec((T, HEAD_PAD), lambda h, kj, qq: (kj, h)),
                  pl.BlockSpec((T, HEAD_PAD), lambda h, kj, qq: (kj, h)),
                  pl.BlockSpec((T, HEAD_PAD), qrow),
                  pl.BlockSpec((1, T, 1), lambda h, kj, qq: (h, jnp.maximum(qq, kj), 0)),
                  pl.BlockSpec((1, T, 1), lambda h, kj, qq: (h, jnp.maximum(qq, kj), 0))],
        out_specs=[pl.BlockSpec((S, HEAD_PAD), lambda h, kj, qq: (0, h)),
                   pl.BlockSpec((T, HEAD_PAD), lambda h, kj, qq: (kj, h)),
                   pl.BlockSpec((T, HEAD_PAD), lambda h, kj, qq: (kj, h))],
        out_shape=[jax.ShapeDtypeStruct((S, heads * HEAD_PAD), F32)] * 3,
        scratch_shapes=[pltpu.VMEM((T, HEAD_PAD), F32), pltpu.VMEM((T, HEAD_PAD), F32)],
        compiler_params=_cparams("arbitrary", "arbitrary", "arbitrary"),
    )(q_arr, k_arr, v_arr, do_arr, lse, delta)


SWA_R = SWA_HEADS // SWA_KV_HEADS
SWA_SCALE = SWA_HD ** -0.5
SWA_Q0, SWA_K0, SWA_V0 = 0, 12, 14


def _swa_specs(tq):
    nsb = tq // WINDOW
    return [
        pl.BlockSpec((tq, SWA_R * HEAD_PAD), lambda g, i: (i, g)),
        pl.BlockSpec((tq, HEAD_PAD), lambda g, i: (i, SWA_K0 + g)),
        pl.BlockSpec((WINDOW, HEAD_PAD), lambda g, i: (jnp.maximum(nsb * i - 1, 0), SWA_K0 + g)),
        pl.BlockSpec((tq, HEAD_PAD), lambda g, i: (i, SWA_V0 + g)),
        pl.BlockSpec((WINDOW, HEAD_PAD), lambda g, i: (jnp.maximum(nsb * i - 1, 0), SWA_V0 + g)),
        pl.BlockSpec((SWA_R, WINDOW, 2 * WINDOW), lambda g, i: (g, 0, 0)),
        pl.BlockSpec((SWA_R, 8, 128), lambda g, i: (g, 0, 0)),
    ]


def _swa_block(i, sb, q_ref, kc_ref, kp_ref, vc_ref, vp_ref, bias, sink):
    rows = slice(sb * WINDOW, (sb + 1) * WINDOW)
    qs = jnp.concatenate([q_ref[rows, hh * HEAD_PAD:(hh + 1) * HEAD_PAD] for hh in range(SWA_R)], axis=0)
    if sb == 0:
        kp, vp = kp_ref[...], vp_ref[...]
    else:
        prev = slice((sb - 1) * WINDOW, sb * WINDOW)
        kp, vp = kc_ref[prev, :], vc_ref[prev, :]
    kk = jnp.concatenate([kp, kc_ref[rows, :]], axis=0)
    vv = jnp.concatenate([vp, vc_ref[rows, :]], axis=0)
    s = lax.dot_general(qs, kk, (((1,), (1,)), ((), ())), preferred_element_type=F32) * SWA_SCALE + bias
    if sb == 0:
        col = lax.broadcasted_iota(jnp.int32, (1, 2 * WINDOW), 1)
        s = s + jnp.where((col < WINDOW) & (i == 0), NEG, 0.0)
    return rows, qs, kk, vv, s


def _stack_heads(ref, rows, lead=None):
    if lead is None:
        return jnp.concatenate([ref[rows, hh * HEAD_PAD:(hh + 1) * HEAD_PAD] for hh in range(SWA_R)], axis=0)
    return jnp.concatenate([ref[hh, rows, :] for hh in range(SWA_R)], axis=0)


def _swa_fwd(name, proj_b, bias, sinks, *, tq):
    S = proj_b.shape[0]
    tq = min(tq, S)
    nsb = tq // WINDOW

    def body(q_ref, kc_ref, kp_ref, vc_ref, vp_ref, bias_ref, sink_ref, o_ref, lse_ref):
        i = pl.program_id(1)
        bias_v = bias_ref[...].reshape(SWA_R * WINDOW, 2 * WINDOW)
        sink = jnp.concatenate([jnp.zeros((WINDOW, 1), F32) + sink_ref[hh, 0:1, 0:1] for hh in range(SWA_R)], axis=0)
        for sb in range(nsb):
            rows, _, _, vv, s = _swa_block(i, sb, q_ref, kc_ref, kp_ref, vc_ref, vp_ref, bias_v, sink)
            m = jnp.maximum(jnp.max(s, axis=1, keepdims=True), sink)
            p = jnp.exp(s - m)
            l = jnp.sum(p, axis=1, keepdims=True) + jnp.exp(sink - m)
            o = jnp.dot(p.astype(BF16), vv, preferred_element_type=F32) / l
            lse_v = m + jnp.log(l)
            for hh in range(SWA_R):
                o_ref[rows, hh * HEAD_PAD:(hh + 1) * HEAD_PAD] = o[hh * WINDOW:(hh + 1) * WINDOW].astype(o_ref.dtype)
                lse_ref[hh, rows, :] = lse_v[hh * WINDOW:(hh + 1) * WINDOW]

    return pl.pallas_call(
        body, name=name, grid=(SWA_KV_HEADS, S // tq),
        in_specs=_swa_specs(tq),
        out_specs=[pl.BlockSpec((tq, SWA_R * HEAD_PAD), lambda g, i: (i, g)),
                   pl.BlockSpec((SWA_R, tq, 1), lambda g, i: (g, i, 0))],
        out_shape=[jax.ShapeDtypeStruct((S, SWA_HEADS * HEAD_PAD), BF16),
                   jax.ShapeDtypeStruct((SWA_HEADS, S, 1), F32)],
        compiler_params=_cparams("parallel", "parallel"),
    )(proj_b, proj_b, proj_b, proj_b, proj_b, bias, sinks)


def _swa_bwd(name, proj_b, bias, sinks, o, do, lse, *, tq):
    S = proj_b.shape[0]
    tq = min(tq, S)
    nsb = tq // WINDOW
    nq = S // tq

    def body(q_ref, kc_ref, kp_ref, vc_ref, vp_ref, bias_ref, sink_ref, o_ref, do_ref, lse_ref,
             dq_ref, dk_ref, dv_ref, dke_ref, dve_ref, dbias_ref, dsink_ref):
        i = pl.program_id(1)

        @pl.when(i == 0)
        def _():
            dbias_ref[...] = jnp.zeros(dbias_ref.shape, F32)
            dsink_ref[...] = jnp.zeros(dsink_ref.shape, F32)

        bias_v = bias_ref[...].reshape(SWA_R * WINDOW, 2 * WINDOW)
        sink = jnp.concatenate([jnp.zeros((WINDOW, 1), F32) + sink_ref[hh, 0:1, 0:1] for hh in range(SWA_R)], axis=0)
        dk_own, dv_own, dk_prev, dv_prev = [], [], [], []
        dbias_acc = jnp.zeros((SWA_R * WINDOW, 2 * WINDOW), F32)
        for sb in range(nsb):
            rows, qs, kk, vv, s = _swa_block(i, sb, q_ref, kc_ref, kp_ref, vc_ref, vp_ref, bias_v, sink)
            lse_v = _stack_heads(lse_ref, rows, lead=True)
            do_s = _stack_heads(do_ref, rows)
            delta = jnp.sum(do_s.astype(F32) * _stack_heads(o_ref, rows).astype(F32), axis=1, keepdims=True)
            p = jnp.exp(s - lse_v)
            dp = lax.dot_general(do_s, vv, (((1,), (1,)), ((), ())), preferred_element_type=F32)
            dsp = p * (dp - delta)
            dbias_acc = dbias_acc + dsp
            ds = (dsp * SWA_SCALE).astype(BF16)
            dq = jnp.dot(ds, kk, preferred_element_type=F32)
            dkk = lax.dot_general(ds, qs, (((0,), (0,)), ((), ())), preferred_element_type=F32)
            dvv = lax.dot_general(p.astype(BF16), do_s, (((0,), (0,)), ((), ())), preferred_element_type=F32)
            dk_prev.append(dkk[:WINDOW])
            dk_own.append(dkk[WINDOW:])
            dv_prev.append(dvv[:WINDOW])
            dv_own.append(dvv[WINDOW:])
            psink = jnp.exp(sink - lse_v) * delta
            for hh in range(SWA_R):
                hrows = slice(hh * WINDOW, (hh + 1) * WINDOW)
                dq_ref[rows, hh * HEAD_PAD:(hh + 1) * HEAD_PAD] = dq[hrows].astype(dq_ref.dtype)
                dsink_ref[hh] += jnp.zeros((8, 128), F32) - jnp.sum(psink[hrows])
        dbias_ref[...] += dbias_acc.reshape(SWA_R, WINDOW, 2 * WINDOW)
        for sb in range(nsb):
            rows = slice(sb * WINDOW, (sb + 1) * WINDOW)
            if sb + 1 < nsb:
                dk_ref[rows, :] = dk_own[sb] + dk_prev[sb + 1]
                dv_ref[rows, :] = dv_own[sb] + dv_prev[sb + 1]
            else:
                dk_ref[rows, :] = dk_own[sb]
                dv_ref[rows, :] = dv_own[sb]
        dke_ref[...] = dk_prev[0]
        dve_ref[...] = dv_prev[0]

    in_specs = _swa_specs(tq) + [
        pl.BlockSpec((tq, SWA_R * HEAD_PAD), lambda g, i: (i, g)),
        pl.BlockSpec((tq, SWA_R * HEAD_PAD), lambda g, i: (i, g)),
        pl.BlockSpec((SWA_R, tq, 1), lambda g, i: (g, i, 0)),
    ]
    kv_blk = pl.BlockSpec((tq, HEAD_PAD), lambda g, i: (i, g))
    edge_blk = pl.BlockSpec((WINDOW, HEAD_PAD), lambda g, i: (i, g))
    return pl.pallas_call(
        body, name=name, grid=(SWA_KV_HEADS, nq),
        in_specs=in_specs,
        out_specs=[pl.BlockSpec((tq, SWA_R * HEAD_PAD), lambda g, i: (i, g)), kv_blk, kv_blk, edge_blk, edge_blk,
                   pl.BlockSpec((SWA_R, WINDOW, 2 * WINDOW), lambda g, i: (g, 0, 0)),
                   pl.BlockSpec((SWA_R, 8, 128), lambda g, i: (g, 0, 0))],
        out_shape=[jax.ShapeDtypeStruct((S, SWA_HEADS * HEAD_PAD), BF16),
                   jax.ShapeDtypeStruct((S, SWA_KV_HEADS * HEAD_PAD), F32),
                   jax.ShapeDtypeStruct((S, SWA_KV_HEADS * HEAD_PAD), F32),
                   jax.ShapeDtypeStruct((nq * WINDOW, SWA_KV_HEADS * HEAD_PAD), F32),
                   jax.ShapeDtypeStruct((nq * WINDOW, SWA_KV_HEADS * HEAD_PAD), F32),
                   jax.ShapeDtypeStruct((SWA_HEADS, WINDOW, 2 * WINDOW), F32),
                   jax.ShapeDtypeStruct((SWA_HEADS, 8, 128), F32)],
        compiler_params=_cparams("arbitrary", "arbitrary"),
    )(proj_b, proj_b, proj_b, proj_b, proj_b, bias, sinks, o, do, lse)


def _dproj_b(name, dq_swa, dq_mem, dk, dv, dk_edge, dv_edge, *, tq):
    S = dq_swa.shape[0]
    tq = min(tq, S)
    nq = S // tq

    def body(dqs_ref, dqm_ref, dk_ref, dv_ref, dke_ref, dve_ref, o_ref):
        i = pl.program_id(0)
        o_ref[:, 0:1024] = dqs_ref[...]
        o_ref[:, 1024:1536] = dqm_ref[...].astype(o_ref.dtype)
        o_ref[:, 1536:1792] = dk_ref[...].astype(o_ref.dtype)
        o_ref[:, 1792:2048] = dv_ref[...].astype(o_ref.dtype)

        @pl.when(i < nq - 1)
        def _():
            last = slice(tq - WINDOW, tq)
            o_ref[last, 1536:1792] = (dk_ref[last, :] + dke_ref[...]).astype(o_ref.dtype)
            o_ref[last, 1792:2048] = (dv_ref[last, :] + dve_ref[...]).astype(o_ref.dtype)

    edge = pl.BlockSpec((WINDOW, SWA_KV_HEADS * HEAD_PAD), lambda i: (jnp.minimum(i + 1, nq - 1), 0))
    return pl.pallas_call(
        body, name=name, grid=(nq,),
        in_specs=[pl.BlockSpec((tq, 1024), lambda i: (i, 0)), pl.BlockSpec((tq, 512), lambda i: (i, 0)),
                  pl.BlockSpec((tq, 256), lambda i: (i, 0)), pl.BlockSpec((tq, 256), lambda i: (i, 0)), edge, edge],
        out_specs=pl.BlockSpec((tq, 2048), lambda i: (i, 0)),
        out_shape=jax.ShapeDtypeStruct((S, 2048), BF16),
        compiler_params=_cparams("parallel"),
    )(dq_swa, dq_mem, dk, dv, dk_edge, dv_edge)


def _exchange(name, send, *, per_peer):
    shape = send.shape[1:] if per_peer else send.shape

    def body(send_ref, recv_ref, send_sems, recv_sems, local_sem):
        x, y, c = lax.axis_index("x"), lax.axis_index("y"), lax.axis_index("c")
        me = 4 * x + 2 * y + c
        own = pltpu.make_async_copy(send_ref.at[me] if per_peer else send_ref, recv_ref.at[me], local_sem)
        own.start()
        copies = []
        for k in range(1, N_DEV):
            px = 1 - x if (k >> 2) & 1 else x
            py = 1 - y if (k >> 1) & 1 else y
            pc = 1 - c if k & 1 else c
            peer = 4 * px + 2 * py + pc
            out = pltpu.make_async_remote_copy(
                src_ref=send_ref.at[peer] if per_peer else send_ref, dst_ref=recv_ref.at[me],
                send_sem=send_sems.at[k - 1], recv_sem=recv_sems.at[k - 1],
                device_id=(px, py, pc), device_id_type=pl.DeviceIdType.MESH)
            out.start()
            back = pltpu.make_async_remote_copy(
                src_ref=send_ref.at[me] if per_peer else send_ref, dst_ref=recv_ref.at[peer],
                send_sem=send_sems.at[k - 1], recv_sem=recv_sems.at[k - 1],
                device_id=(px, py, pc), device_id_type=pl.DeviceIdType.MESH)
            copies.append((out, back))
        for out, back in copies:
            out.wait_send()
            back.wait_recv()
        own.wait()

    return pl.pallas_call(
        body, name=name,
        in_specs=[pl.BlockSpec(memory_space=pl.ANY)],
        out_specs=pl.BlockSpec(memory_space=pl.ANY),
        out_shape=jax.ShapeDtypeStruct((N_DEV,) + tuple(shape), send.dtype),
        scratch_shapes=[pltpu.SemaphoreType.DMA((N_DEV - 1,)), pltpu.SemaphoreType.DMA((N_DEV - 1,)),
                        pltpu.SemaphoreType.DMA(())],
    )(send)


def _adam(name, recv, w, m, v, *, tm=None):
    R = w.shape[0]
    tm = max(t for t in range(8, min(tm or ADAM_TM, R) + 1, 8) if R % t == 0)
    c1 = 1.0 / (1.0 - ADAM_B1 ** ADAM_STEP)
    c2 = 1.0 / (1.0 - ADAM_B2 ** ADAM_STEP)

    def body(r_ref, w_ref, m_ref, v_ref, g_ref, d_ref, nm_ref, nv_ref):
        g = r_ref[0].astype(F32)
        for j in range(1, N_DEV):
            g = g + r_ref[j].astype(F32)
        wv = w_ref[...]
        nm = ADAM_B1 * m_ref[...] + (1.0 - ADAM_B1) * g
        nv = ADAM_B2 * v_ref[...] + (1.0 - ADAM_B2) * (g * g)
        m_hat = nm * c1
        v_hat = nv * c2
        g_ref[...] = g
        d_ref[...] = -ADAM_LR * (m_hat / (jnp.sqrt(v_hat) + ADAM_EPS) + ADAM_WD * wv)
        nm_ref[...] = nm
        nv_ref[...] = nv

    row = pl.BlockSpec((tm, 128), lambda i: (i, 0))
    return pl.pallas_call(
        body, name=name, grid=(R // tm,),
        in_specs=[pl.BlockSpec((N_DEV, tm, 128), lambda i: (0, i, 0)), row, row, row],
        out_specs=[row, row, row, row],
        out_shape=[jax.ShapeDtypeStruct((R, 128), F32)] * 4,
        compiler_params=_cparams("parallel"),
    )(recv, w, m, v)


def _pack_rows(arrs):
    return jnp.concatenate([a.reshape(-1, 128) for a in arrs], axis=0)


def _unstack(g, shape, axis):
    t = jnp.moveaxis(g, 0, axis)
    return t.reshape(shape)


def _restack(full, axis):
    s = full.shape
    t = full.reshape(s[:axis] + (N_DEV, s[axis] // N_DEV) + s[axis + 1:])
    return jnp.moveaxis(t, axis, 0)


def _pad_heads(w, heads, hd, axis):
    s = w.shape
    t = w.reshape(s[:axis] + (heads, hd) + s[axis + 1:])
    pad = [(0, 0)] * t.ndim
    pad[axis + 1] = (0, HEAD_PAD - hd)
    t = jnp.pad(t, pad)
    return t.reshape(s[:axis] + (heads * HEAD_PAD,) + s[axis + 1:])


def _unpad_heads(w, heads, hd, axis):
    s = w.shape
    t = w.reshape(s[:axis] + (heads, HEAD_PAD) + s[axis + 1:])
    t = lax.slice_in_dim(t, 0, hd, axis=axis + 1)
    return t.reshape(s[:axis] + (heads * hd,) + s[axis + 1:])


def _layer_weights(full, l):
    w_in = full["w_in"][l]
    cq, kva, qs, ks, vs, qm, gates = (w_in[:, 0:256], w_in[:, 256:416], w_in[:, 416:928], w_in[:, 928:1056],
                                       w_in[:, 1056:1184], w_in[:, 1184:1696], w_in[:, 1696:4768])
    wa = jnp.concatenate([gates, cq, jnp.pad(kva, ((0, 0), (0, 96)))], axis=1)
    wb = jnp.concatenate([_pad_heads(qs, SWA_HEADS, SWA_HD, 1), qm, _pad_heads(ks, SWA_KV_HEADS, SWA_HD, 1),
                          _pad_heads(vs, SWA_KV_HEADS, SWA_HD, 1)], axis=1)
    wuq = _pad_heads(full["w_uq"][l], MLA_HEADS, MLA_NOPE + MLA_ROPE, 1)
    ukv = full["w_ukv"][l].reshape(MLA_KV_LORA, MLA_HEADS, MLA_NOPE + MLA_V)
    wuk = _pad_heads(ukv[:, :, :MLA_NOPE].reshape(MLA_KV_LORA, -1), MLA_HEADS, MLA_NOPE, 1)
    wuv = _pad_heads(ukv[:, :, MLA_NOPE:].reshape(MLA_KV_LORA, -1), MLA_HEADS, MLA_V, 1)
    wo_mla = _pad_heads(full["w_o_mla"][l], MLA_HEADS, MLA_V, 0)
    wo_swa = _pad_heads(full["w_o_swa"][l], SWA_HEADS, SWA_HD, 0)
    wo_mem = full["w_o_mem"][l]
    w = dict(wa=wa, wb=wb, wuq=wuq, wuk=wuk, wuv=wuv, wo_mla=wo_mla, wo_swa=wo_swa, wo_mem=wo_mem,
             wmem=full["w_mem_kv"][l], wout=full["w_out"][l], wup=full["w_up"][l], wdown=full["w_down"][l])
    wt = {k + "_t": v.T for k, v in w.items()}
    wt["wag_t"] = wt["wa_t"][:3072]
    wt["wat_t"] = wt["wa_t"][3072:]
    w.update(wt)
    return w


def _layer_weight_grads(g):
    dwa_g, dwa_t, dwb = g["wag"], g["wat"], g["wb"]
    d_in = jnp.concatenate([
        dwa_t[:, 0:256], dwa_t[:, 256:416],
        _unpad_heads(dwb[:, 0:1024], SWA_HEADS, SWA_HD, 1),
        _unpad_heads(dwb[:, 1536:1792], SWA_KV_HEADS, SWA_HD, 1),
        _unpad_heads(dwb[:, 1792:2048], SWA_KV_HEADS, SWA_HD, 1),
        dwb[:, 1024:1536], dwa_g], axis=1)
    duk = _unpad_heads(g["wuk"], MLA_HEADS, MLA_NOPE, 1).reshape(MLA_KV_LORA, MLA_HEADS, MLA_NOPE)
    duv = _unpad_heads(g["wuv"], MLA_HEADS, MLA_V, 1).reshape(MLA_KV_LORA, MLA_HEADS, MLA_V)
    return dict(
        w_in=d_in,
        w_uq=_unpad_heads(g["wuq"], MLA_HEADS, MLA_NOPE + MLA_ROPE, 1),
        w_ukv=jnp.concatenate([duk, duv], axis=2).reshape(MLA_KV_LORA, -1),
        w_mem_kv=g["wmem"],
        w_o_mla=_unpad_heads(g["wo_mla"], MLA_HEADS, MLA_V, 0),
        w_o_swa=_unpad_heads(g["wo_swa"], SWA_HEADS, SWA_HD, 0),
        w_o_mem=g["wo_mem"], w_out=g["wout"], w_up=g["wup"], w_down=g["wdown"])


def _rope_tables(S):
    pos = jnp.arange(S, dtype=F32)
    inv = 1.0 / (ROPE_THETA ** (jnp.arange(0, MLA_ROPE, 2, dtype=F32) / MLA_ROPE))
    ang = pos[:, None] * inv[None, :]
    cos, sin = jnp.cos(ang), jnp.sin(ang)
    z16 = jnp.zeros((S, 16), F32)
    z32 = jnp.zeros((S, 32), F32)
    c = jnp.concatenate([jnp.ones((S, 64), F32), cos, cos, z32], axis=1)
    ck = jnp.concatenate([jnp.zeros((S, 64), F32), cos, cos, z32], axis=1)
    s1 = jnp.concatenate([jnp.zeros((S, 80), F32), sin, z32], axis=1)
    s2 = jnp.concatenate([jnp.zeros((S, 64), F32), -sin, z16, z32], axis=1)
    return c, ck, s1, s2


def _t5_bucket(dist):
    n = jnp.maximum(dist, 0)
    max_exact = REL_BUCKETS // 2
    nf = jnp.maximum(n, 1).astype(F32)
    large = max_exact + (jnp.log(nf / max_exact) / math.log(REL_MAX_DIST / max_exact)
                         * (REL_BUCKETS - max_exact)).astype(jnp.int32)
    large = jnp.minimum(large, REL_BUCKETS - 1)
    return jnp.where(n < max_exact, n, large)


def _bias_onehot():
    qi = jnp.arange(WINDOW)[:, None]
    kj = jnp.arange(2 * WINDOW)[None, :]
    dist = qi + WINDOW - kj
    valid = (dist >= 0) & (dist < WINDOW)
    bucket = _t5_bucket(dist)
    onehot = (bucket[None] == jnp.arange(REL_BUCKETS)[:, None, None]) & valid[None]
    return (onehot.reshape(REL_BUCKETS, -1).astype(F32),
            jnp.where(valid, 0.0, NEG).astype(F32).reshape(1, -1))


def _rstd(x):
    return lax.rsqrt(jnp.mean(x * x, axis=-1, keepdims=True) + EPS)


def _norm_bwd(dh, x, g):
    r = _rstd(x)
    xh = x * r
    w = dh * g
    dx = r * (w - xh * jnp.mean(w * xh, axis=-1, keepdims=True))
    return dx, jnp.sum(dh * xh, axis=0, keepdims=True)


def _tile_lanes(t, n):
    return jnp.tile(t, (1, n // t.shape[1])) if n != t.shape[1] else t


def _rope_fwd(a, c, s1, s2):
    n = a.shape[1]
    return (a * _tile_lanes(c, n) + pltpu.roll(a, 16, 1) * _tile_lanes(s1, n)
            + pltpu.roll(a, n - 16, 1) * _tile_lanes(s2, n))


def _rope_bwd(d, c, s1, s2):
    n = d.shape[1]
    return (d * _tile_lanes(c, n) + pltpu.roll(d * _tile_lanes(s1, n), n - 16, 1)
            + pltpu.roll(d * _tile_lanes(s2, n), 16, 1))


def _sigmoid(x):
    return 1.0 / (1.0 + jnp.exp(-x))


def _rmsnorm(name, x, g, dtype):
    def fn(xv, gv):
        return ((xv * _rstd(xv)) * gv,)
    return _rowwise(name, fn, [_rows(x), _full(g)], [(x.shape, dtype, "rows")], rows=x.shape[0])[0]


def _residual_norm_bwd(name, dres, dh, x, g):
    def fn(dr, dhv, xv, gv):
        dx, dg = _norm_bwd(dhv, xv, gv)
        return dr + dx, dg
    return _rowwise(name, fn, [_rows(dres), _rows(dh), _rows(x), _full(g)],
                    [(x.shape, F32, "rows"), (g.shape, F32, "acc")], rows=x.shape[0])


def _layer_fwd(l, x, mem, w, p, tabs, swa_bias, S):
    c, ck, s1, s2 = tabs
    n = f"l{l}_"
    h = _rmsnorm(n + "attn_norm", x, p["attn_norm"], BF16)
    proj_a = _mm(n + "proj_a", h, w["wa"], [F32])
    proj_b = _mm(n + "proj_b", h, w["wb"], [BF16])

    def prep(cq, kva, qn, kvn, ckv, s1v, s2v):
        cqn = cq * _rstd(cq) * qn
        ckv_ = kva[:, :128]
        ckvn = ckv_ * _rstd(ckv_) * kvn
        pe = pltpu.roll(kva[:, 128:], 64, 1)
        return cqn, ckvn, _rope_fwd(pe, ckv, s1v, s2v)

    cqn, ckvn, kpe = _rowwise(
        n + "mla_prep", prep,
        [_rows(proj_a, 256, 12), _rows(proj_a, 256, 13), _full(p["mla_q_norm"]), _full(p["mla_kv_norm"]),
         _rows(ck), _rows(s1), _rows(s2)],
        [((S, 256), BF16, "rows"), ((S, 128), BF16, "rows"), ((S, 128), F32, "rows")], rows=S)

    q_mla = _mm(n + "q_mla", cqn, w["wuq"], [BF16],
                epi=lambda acc, cv, s1v, s2v: (_rope_fwd(acc, cv, s1v, s2v) * MLA_SCALE,),
                extras=[(c, "m"), (s1, "m"), (s2, "m")])
    k_mla = _mm(n + "k_mla", ckvn, w["wuk"], [BF16],
                epi=lambda acc, kp: (acc + _tile_lanes(kp, acc.shape[1]),), extras=[(kpe, "m")])
    den = (jnp.arange(MLA_HEADS * HEAD_PAD) % HEAD_PAD == DEN_LANE).astype(F32)[None]
    v_mla = _mm(n + "v_mla", ckvn, w["wuv"], [BF16], epi=lambda acc, dv: (acc + dv,), extras=[(den, "n")])
    o_mla, lse_mla = _causal_fwd(n + "mla_fwd", q_mla, k_mla, v_mla, heads=MLA_HEADS, tile=MLA_TILE,
                                 chunk=MLA_CHUNK)
    o_swa, lse_swa = _swa_fwd(n + "swa_fwd", proj_b, swa_bias, p["sinks"], tq=SWA_TQ)
    mn = _rmsnorm(n + "mem_norm", mem, p["mem_norm"], BF16)
    kvm = _mm(n + "kv_mem", mn, w["wmem"], [BF16])
    o_mem, lse_mem = _flash_fwd(n + "mem_fwd", proj_b, kvm, kvm, heads=MEM_HEADS, q_off=8, k_off=0, v_off=4,
                                group=1, mode="full", scale=MEM_HD ** -0.5, tq=MEM_TQ, tk=MEM_LEN)
    t0 = _mm(n + "t_mla", o_mla, w["wo_mla"], [F32])
    t1 = _mm(n + "t_swa", o_swa, w["wo_swa"], [F32])
    t2 = _mm(n + "t_mem", o_mem, w["wo_mem"], [F32])

    def merge(g0, g1, g2, bg, a0, a1, a2):
        y = (_sigmoid(g0 + bg[:, 0:1024]) * a0 + _sigmoid(g1 + bg[:, 1024:2048]) * a1
             + _sigmoid(g2 + bg[:, 2048:3072]) * a2)
        return (y,)

    y = _rowwise(n + "merge", merge,
                 [_rows(proj_a, 1024, 0), _rows(proj_a, 1024, 1), _rows(proj_a, 1024, 2), _full(p["b_gate"]),
                  _rows(t0), _rows(t1), _rows(t2)], [((S, D_MODEL), BF16, "rows")], rows=S)[0]
    x1 = _mm(n + "out_proj", y, w["wout"], [F32], epi=lambda acc, r: (acc + r,), extras=[(x, "mn")])
    h2 = _rmsnorm(n + "mlp_norm", x1, p["mlp_norm"], BF16)
    u, act = _mm(n + "mlp_up", h2, w["wup"], [F32, BF16],
                 epi=lambda acc: (acc, jnp.square(jnp.maximum(acc, 0.0))))
    x2 = _mm(n + "mlp_down", act, w["wdown"], [F32], epi=lambda acc, r: (acc + r,), extras=[(x1, "mn")])
    saved = dict(x=x, h=h, proj_a=proj_a, proj_b=proj_b, cqn=cqn, ckvn=ckvn, q_mla=q_mla, k_mla=k_mla, v_mla=v_mla,
                 o_mla=o_mla, lse_mla=lse_mla, o_swa=o_swa, lse_swa=lse_swa, mn=mn, kvm=kvm, o_mem=o_mem,
                 lse_mem=lse_mem, t0=t0, t1=t1, t2=t2, y=y, x1=x1, h2=h2, u=u, act=act)
    return x2, saved


def _layer_bwd(l, dx2, mem, w, p, tabs, swa_bias, sv, S):
    c, ck, s1, s2 = tabs
    n = f"l{l}_b_"
    gw = {}
    gs = {}
    du = _mm(n + "d_act", dx2, w["wdown_t"], [BF16],
             epi=lambda acc, uv: (acc * (2.0 * jnp.maximum(uv, 0.0)),), extras=[(sv["u"], "mn")])
    gw["wdown"] = _mm_tn(n + "g_wdown", sv["act"], dx2)
    gw["wup"] = _mm_tn(n + "g_wup", sv["h2"], du)
    dh2 = _mm(n + "d_h2", du, w["wup_t"], [F32])
    dx1, gs["mlp_norm"] = _residual_norm_bwd(n + "mlp_norm", dx2, dh2, sv["x1"], p["mlp_norm"])
    gw["wout"] = _mm_tn(n + "g_wout", sv["y"], dx1)
    dy = _mm(n + "d_y", dx1, w["wout_t"], [F32])

    def merge_bwd(dyv, g0, g1, g2, bg, a0, a1, a2):
        outs, dgs = [], []
        for b, (gv, av) in enumerate(((g0, a0), (g1, a1), (g2, a2))):
            sg = _sigmoid(gv + bg[:, b * 1024:(b + 1) * 1024])
            outs.append(dyv * sg)
            dgs.append(dyv * av * sg * (1.0 - sg))
        dg = jnp.concatenate(dgs, axis=1)
        return outs[0], outs[1], outs[2], dg, jnp.sum(dg, axis=0, keepdims=True)

    pa = sv["proj_a"]
    dt0, dt1, dt2, dgates, gs["b_gate"] = _rowwise(
        n + "merge", merge_bwd,
        [_rows(dy), _rows(pa, 1024, 0), _rows(pa, 1024, 1), _rows(pa, 1024, 2), _full(p["b_gate"]),
         _rows(sv["t0"]), _rows(sv["t1"]), _rows(sv["t2"])],
        [((S, D_MODEL), BF16, "rows")] * 3 + [((S, 3 * D_MODEL), BF16, "rows"), ((1, 3 * D_MODEL), F32, "acc")],
        rows=S)
    gw["wo_mla"] = _mm_tn(n + "g_wo_mla", sv["o_mla"], dt0)
    gw["wo_swa"] = _mm_tn(n + "g_wo_swa", sv["o_swa"], dt1)
    gw["wo_mem"] = _mm_tn(n + "g_wo_mem", sv["o_mem"], dt2)
    do_mla = _mm(n + "d_o_mla", dt0, w["wo_mla_t"], [BF16])
    do_swa = _mm(n + "d_o_swa", dt1, w["wo_swa_t"], [BF16])
    do_mem = _mm(n + "d_o_mem", dt2, w["wo_mem_t"], [BF16])
    pb = sv["proj_b"]
    delta_mla = _row_dot(n + "mla_delta", sv["o_mla"], do_mla, heads=MLA_HEADS, tm=MLA_TILE)
    dq_mla, dk_mla, dv_mla = _causal_bwd(
        n + "mla_bwd", sv["q_mla"], sv["k_mla"], sv["v_mla"], do_mla, sv["lse_mla"], delta_mla, heads=MLA_HEADS,
        tile=MLA_TILE, chunk=MLA_CHUNK)
    dq_swa, dk_swa, dv_swa, dk_edge, dv_edge, dbias, dsink = _swa_bwd(
        n + "swa_bwd", pb, swa_bias, p["sinks"], sv["o_swa"], do_swa, sv["lse_swa"], tq=SWA_TQ)
    dq_mem, dk_mem, dv_mem = _flash_bwd(
        n + "mem_bwd", pb, sv["kvm"], sv["kvm"], sv["o_mem"], do_mem, sv["lse_mem"], heads=MEM_HEADS,
        q_off=8, k_off=0, v_off=4, group=1, mode="full", scale=MEM_HD ** -0.5, tq=MEM_TQ, tk=MEM_LEN)
    gs["dbias"] = dbias
    gs["sinks"] = dsink[:, 0, 0]
    dkvm = jnp.concatenate([dk_mem, dv_mem], axis=1)
    gw["wmem"] = _mm_tn(n + "g_wmem", sv["mn"], dkvm)
    dmn = _mm(n + "d_mn", dkvm, w["wmem_t"], [F32])
    _, gs["mem_norm"] = _residual_norm_bwd(n + "mem_norm", dmn, dmn, mem, p["mem_norm"])
    dq_pre = _rowwise(n + "q_unrope", lambda d, cv, s1v, s2v: (_rope_bwd(d * MLA_SCALE, cv, s1v, s2v),),
                      [_rows(dq_mla), _rows(c), _rows(s1), _rows(s2)], [((S, 1024), BF16, "rows")], rows=S)[0]
    gw["wuq"] = _mm_tn(n + "g_wuq", sv["cqn"], dq_pre)
    gw["wuk"] = _mm_tn(n + "g_wuk", sv["ckvn"], dk_mla)
    gw["wuv"] = _mm_tn(n + "g_wuv", sv["ckvn"], dv_mla)
    dcqn = _mm(n + "d_cqn", dq_pre, w["wuq_t"], [F32])
    dckvn = _mm(n + "d_ckvn_k", dk_mla, w["wuk_t"], [F32])
    dckvn = _mm(n + "d_ckvn_v", dv_mla, w["wuv_t"], [F32], epi=lambda acc, r: (acc + r,), extras=[(dckvn, "mn")])

    def mla_norm_bwd(dcq_n, dckv_n, dk, cq, kva, qn, kvn, ckv, s1v, s2v):
        dcq, dqn = _norm_bwd(dcq_n, cq, qn)
        dckv, dkvn = _norm_bwd(dckv_n, kva[:, :128], kvn)
        dkpe = dk[:, 0:128]
        for hh in range(1, MLA_HEADS):
            dkpe = dkpe + dk[:, hh * 128:(hh + 1) * 128]
        dpe = pltpu.roll(_rope_bwd(dkpe, ckv, s1v, s2v), 64, 1)
        return jnp.concatenate([dcq, dckv, dpe], axis=1), dqn, dkvn

    dtail, gs["mla_q_norm"], gs["mla_kv_norm"] = _rowwise(
        n + "mla_norm", mla_norm_bwd,
        [_rows(dcqn), _rows(dckvn), _rows(dk_mla), _rows(pa, 256, 12), _rows(pa, 256, 13),
         _full(p["mla_q_norm"]), _full(p["mla_kv_norm"]), _rows(ck), _rows(s1), _rows(s2)],
        [((S, 512), BF16, "rows"), ((1, 256), F32, "acc"), ((1, 128), F32, "acc")], rows=S)

    dproj_b = _dproj_b(n + "dproj_b", dq_swa, dq_mem, dk_swa, dv_swa, dk_edge, dv_edge, tq=SWA_TQ)
    h = sv["h"]
    gw["wag"] = _mm_tn(n + "g_wa_gates", h, dgates)
    gw["wat"] = _mm_tn(n + "g_wa_tail", h, dtail)
    gw["wb"] = _mm_tn(n + "g_wb", h, dproj_b)
    dh = _mm(n + "d_h_gates", dgates, w["wag_t"], [F32])
    dh = _mm(n + "d_h_tail", dtail, w["wat_t"], [F32], epi=lambda acc, r: (acc + r,), extras=[(dh, "mn")])
    dh = _mm(n + "d_h_b", dproj_b, w["wb_t"], [F32], epi=lambda acc, r: (acc + r,), extras=[(dh, "mn")])
    dx, gs["attn_norm"] = _residual_norm_bwd(n + "attn_norm", dx1, dh, sv["x"], p["attn_norm"])
    return dx, gw, gs


def _local_step(x, mem, loss_target, full, small):
    S = x.shape[0]
    tabs = _rope_tables(S)
    onehot, band = _bias_onehot()
    hi = lax.Precision.HIGHEST
    swa_bias = _mm("swa_bias", small["rel_bias"].T, onehot, [F32], epi=lambda acc, mk: (acc + mk,),
                   extras=[(band, "n")], cast=None, precision=hi, tn=8192).reshape(SWA_HEADS, WINDOW, 2 * WINDOW)
    ws, ps = [], []
    for l in range(DEPTH):
        ws.append(_layer_weights(full, l))
        ps.append(dict(
            attn_norm=small["attn_norm"][l][None], mem_norm=small["mem_norm"][l][None],
            b_gate=small["b_gate"][l][None], mla_q_norm=small["mla_q_norm"][l][None],
            mla_kv_norm=small["mla_kv_norm"][l][None], mlp_norm=small["mlp_norm"][l][None],
            sinks=jnp.broadcast_to(small["attn_sinks"][l][:, None, None], (SWA_HEADS, 8, 128))))
    saved = []
    xc = x
    for l in range(DEPTH):
        xc, sv = _layer_fwd(l, xc, mem, ws[l], ps[l], tabs, swa_bias, S)
        saved.append(sv)

    fn_g = small["final_norm"][None]

    def loss_fn(xv, gv, tv):
        r = _rstd(xv)
        xh = xv * r
        err = xh * gv - tv
        dyv = err * (1.0 / D_MODEL)
        wv = dyv * gv
        dx = r * (wv - xh * jnp.mean(wv * xh, axis=-1, keepdims=True))
        part = 0.5 * jnp.sum(err * err) * (1.0 / D_MODEL)
        return dx, jnp.sum(dyv * xh, axis=0, keepdims=True), jnp.zeros((8, 128), F32) + part

    dx, g_final, loss_acc = _rowwise(
        "loss", loss_fn, [_rows(xc), _full(fn_g), _rows(loss_target)],
        [((S, D_MODEL), F32, "rows"), ((1, D_MODEL), F32, "acc"), ((8, 128), F32, "acc")], rows=S)

    gws, gss = [None] * DEPTH, [None] * DEPTH
    for l in reversed(range(DEPTH)):
        dx, gw, gs = _layer_bwd(l, dx, mem, ws[l], ps[l], tabs, swa_bias, saved[l], S)
        gws[l] = _layer_weight_grads(gw)
        gss[l] = gs

    dbias = (gss[0]["dbias"] + gss[1]["dbias"]).reshape(SWA_HEADS, -1)
    g_rel = _mm("g_rel_bias", dbias, onehot.T, [F32], cast=None, precision=hi, tk=8192).T
    wgrads = {k: jnp.stack([gws[l][k] for l in range(DEPTH)]) for k in gws[0]}
    sgrads = dict(
        rel_bias=g_rel,
        final_norm=g_final[0],
        attn_sinks=jnp.stack([gss[l]["sinks"] for l in range(DEPTH)]),
        **{k: jnp.concatenate([gss[l][k] for l in range(DEPTH)], axis=0)
           for k in ("attn_norm", "mem_norm", "b_gate", "mla_q_norm", "mla_kv_norm", "mlp_norm")})
    return loss_acc[0, 0], dx, wgrads, sgrads


def _pack_small(vals, loss):
    rows = []
    for name, shape in SMALL:
        flat = vals[name].astype(F32).reshape(-1)
        pad = (-flat.shape[0]) % 1024
        rows.append(jnp.pad(flat, (0, pad)).reshape(-1, 128))
    rows.append(jnp.zeros((8, 128), F32) + loss)
    return jnp.concatenate(rows, axis=0)


def _unpack_small(packed):
    out, r = {}, 0
    for name, shape in SMALL:
        size = math.prod(shape)
        nrows = 8 * -(-size // 1024)
        out[name] = packed[r:r + nrows].reshape(-1)[:size].reshape(shape)
        r += nrows
    return out, packed[r, 0]


def kernel(x, mem, rel_bias, attn_norm, mem_norm, w_in, b_gate, mla_q_norm, w_uq, mla_kv_norm, w_ukv, attn_sinks, w_mem_kv, w_o_mla, w_o_swa, w_o_mem, w_out, mlp_norm, w_up, w_down, final_norm, loss_target, m_rel_bias, m_attn_norm, m_mem_norm, m_w_in, m_b_gate, m_mla_q_norm, m_w_uq, m_mla_kv_norm, m_w_ukv, m_attn_sinks, m_w_mem_kv, m_w_o_mla, m_w_o_swa, m_w_o_mem, m_w_out, m_mlp_norm, m_w_up, m_w_down, m_final_norm, v_rel_bias, v_attn_norm, v_mem_norm, v_w_in, v_b_gate, v_mla_q_norm, v_w_uq, v_mla_kv_norm, v_w_ukv, v_attn_sinks, v_w_mem_kv, v_w_o_mla, v_w_o_swa, v_w_o_mem, v_w_out, v_mlp_norm, v_w_up, v_w_down, v_final_norm):
    wv = dict(rel_bias=rel_bias, attn_norm=attn_norm, mem_norm=mem_norm, w_in=w_in, b_gate=b_gate,
              mla_q_norm=mla_q_norm, w_uq=w_uq, mla_kv_norm=mla_kv_norm, w_ukv=w_ukv, attn_sinks=attn_sinks,
              w_mem_kv=w_mem_kv, w_o_mla=w_o_mla, w_o_swa=w_o_swa, w_o_mem=w_o_mem, w_out=w_out,
              mlp_norm=mlp_norm, w_up=w_up, w_down=w_down, final_norm=final_norm)
    mv = dict(rel_bias=m_rel_bias, attn_norm=m_attn_norm, mem_norm=m_mem_norm, w_in=m_w_in, b_gate=m_b_gate,
              mla_q_norm=m_mla_q_norm, w_uq=m_w_uq, mla_kv_norm=m_mla_kv_norm, w_ukv=m_w_ukv,
              attn_sinks=m_attn_sinks, w_mem_kv=m_w_mem_kv, w_o_mla=m_w_o_mla, w_o_swa=m_w_o_swa,
              w_o_mem=m_w_o_mem, w_out=m_w_out, mlp_norm=m_mlp_norm, w_up=m_w_up, w_down=m_w_down,
              final_norm=m_final_norm)
    vv = dict(rel_bias=v_rel_bias, attn_norm=v_attn_norm, mem_norm=v_mem_norm, w_in=v_w_in, b_gate=v_b_gate,
              mla_q_norm=v_mla_q_norm, w_uq=v_w_uq, mla_kv_norm=v_mla_kv_norm, w_ukv=v_w_ukv,
              attn_sinks=v_attn_sinks, w_mem_kv=v_w_mem_kv, w_o_mla=v_w_o_mla, w_o_swa=v_w_o_swa,
              w_o_mem=v_w_o_mem, w_out=v_w_out, mlp_norm=v_mlp_norm, w_up=v_w_up, w_down=v_w_down,
              final_norm=v_final_norm)

    shard_rows = [math.prod(_shard_shape(shape, axis)) // 128 for _, shape, axis in WSPECS]
    gathered = _exchange("gather_weights", _pack_rows([wv[name].astype(BF16) for name, _, _ in WSPECS]),
                         per_peer=False)
    full, r = {}, 0
    for (name, shape, axis), nr in zip(WSPECS, shard_rows):
        full[name] = _unstack(gathered[:, r:r + nr].reshape((N_DEV,) + _shard_shape(shape, axis)), shape, axis)
        r += nr

    loss_part, grad_x, wgrads, sgrads = _local_step(x[0], mem[0], loss_target[0], full,
                                                    {name: wv[name] for name, _ in SMALL})

    send = jnp.concatenate([_restack(wgrads[name], axis).astype(BF16).reshape(N_DEV, -1, 128)
                            for name, _, axis in WSPECS], axis=1)
    recv = _exchange("scatter_grads", send, per_peer=True)
    outs = _adam("adam_sharded", recv, *[_pack_rows([d[name] for name, _, _ in WSPECS]) for d in (wv, mv, vv)])
    res = {}
    r = 0
    for (name, shape, axis), nr in zip(WSPECS, shard_rows):
        res[name] = [o[r:r + nr].reshape(_shard_shape(shape, axis)) for o in outs]
        r += nr

    small_recv = _exchange("gather_small", _pack_small(sgrads, loss_part), per_peer=False)
    zero = jnp.zeros((), F32)
    souts = _adam("adam_small", small_recv, *[_pack_small(d, zero) for d in (wv, mv, vv)])
    loss = None
    for i, o in enumerate(souts):
        vals, extra = _unpack_small(o)
        if i == 0:
            loss = extra
        for name, _ in SMALL:
            res.setdefault(name, []).append(vals[name])

    out = [loss, grad_x[None]]
    for i in range(4):
        out.extend(res[name][i] for name in WEIGHT_ORDER)
    return tuple(out)
```

```python
import math

import jax
import jax.numpy as jnp
from jax import lax
from jax.experimental import pallas as pl
from jax.experimental.pallas import tpu as pltpu

F32 = jnp.float32
BF16 = jnp.bfloat16

N_DEV = 8
D_MODEL = 1024
DEPTH = 2
MLA_HEADS = 8
MLA_Q_LORA = 256
MLA_KV_LORA = 128
MLA_NOPE = 64
MLA_ROPE = 32
MLA_V = 64
ROPE_THETA = 10000.0
SWA_HEADS = 8
SWA_KV_HEADS = 2
SWA_HD = 64
WINDOW = 128
REL_BUCKETS = 32
REL_MAX_DIST = 128
MEM_LEN = 256
MEM_HEADS = 4
MEM_HD = 128
D_FF = 4 * D_MODEL
EPS = 1e-6
HEAD_PAD = 128
ADAM_LR = 0.001
ADAM_B1 = 0.9
ADAM_B2 = 0.999
ADAM_EPS = 1e-08
ADAM_WD = 0.01
ADAM_STEP = 10

NEG = -1e30
VMEM_LIMIT = 48 * 1024 * 1024

MM_TM = 1024
MM_TN = 1024
MM_TK = 1024
TN_T1 = 1024
TN_TN = 1024
TN_TS = 1024
ROW_TM = 256
MLA_TILE = 1024
MLA_CHUNK = 256
MLA_CHUNK_FWD = 512
MLA_SCALE = (MLA_NOPE + MLA_ROPE) ** -0.5
LOG2E = math.log2(math.e)
DEN_LANE = MLA_V
SWA_TQ = 512
MEM_TQ = 1024
ADAM_TM = 1200

WSPECS = (
    ("w_in", (DEPTH, D_MODEL, 4768), 2),
    ("w_uq", (DEPTH, MLA_Q_LORA, 768), 2),
    ("w_ukv", (DEPTH, MLA_KV_LORA, 1024), 2),
    ("w_mem_kv", (DEPTH, D_MODEL, 1024), 1),
    ("w_o_mla", (DEPTH, 512, D_MODEL), 2),
    ("w_o_swa", (DEPTH, 512, D_MODEL), 2),
    ("w_o_mem", (DEPTH, 512, D_MODEL), 2),
    ("w_out", (DEPTH, D_MODEL, D_MODEL), 1),
    ("w_up", (DEPTH, D_MODEL, D_FF), 2),
    ("w_down", (DEPTH, D_FF, D_MODEL), 1),
)
SMALL = (
    ("rel_bias", (REL_BUCKETS, SWA_HEADS)),
    ("attn_norm", (DEPTH, D_MODEL)),
    ("mem_norm", (DEPTH, D_MODEL)),
    ("b_gate", (DEPTH, 3 * D_MODEL)),
    ("mla_q_norm", (DEPTH, MLA_Q_LORA)),
    ("mla_kv_norm", (DEPTH, MLA_KV_LORA)),
    ("attn_sinks", (DEPTH, SWA_HEADS)),
    ("mlp_norm", (DEPTH, D_MODEL)),
    ("final_norm", (D_MODEL,)),
)
WEIGHT_ORDER = ("rel_bias", "attn_norm", "mem_norm", "w_in", "b_gate", "mla_q_norm", "w_uq", "mla_kv_norm",
                "w_ukv", "attn_sinks", "w_mem_kv", "w_o_mla", "w_o_swa", "w_o_mem", "w_out", "mlp_norm",
                "w_up", "w_down", "final_norm")


def _cparams(*sem):
    return pltpu.CompilerParams(dimension_semantics=sem, vmem_limit_bytes=VMEM_LIMIT)


def _shard_shape(shape, axis):
    s = list(shape)
    s[axis] //= N_DEV
    return tuple(s)


def _mm(name, a, b, out_dtypes, *, epi=None, extras=(), a_fn=None, cast=BF16, precision=None,
        tm=None, tn=None, tk=None):
    M, K = a.shape
    K2, N = b.shape
    assert K == K2, (name, a.shape, b.shape)
    tm = min(tm or MM_TM, M)
    tn = min(tn or MM_TN, N)
    tk = min(tk or MM_TK, K)
    assert M % tm == 0 and N % tn == 0 and K % tk == 0, (name, a.shape, b.shape, tm, tn, tk)
    nk = K // tk
    n_ex = len(extras)
    n_out = len(out_dtypes)

    def body(*refs):
        a_ref, b_ref = refs[0], refs[1]
        ex_refs = refs[2:2 + n_ex]
        out_refs = refs[2 + n_ex:2 + n_ex + n_out]
        av = a_ref[...]
        if a_fn is not None:
            av = a_fn(av)
        bv = b_ref[...]
        if cast is not None:
            av = av.astype(cast)
            bv = bv.astype(cast)
        part = jnp.dot(av, bv, preferred_element_type=F32, precision=precision)

        def finish(acc):
            outs = epi(acc, *[r[...] for r in ex_refs]) if epi is not None else (acc,)
            for r, o in zip(out_refs, outs):
                r[...] = o.astype(r.dtype)

        if nk == 1:
            finish(part)
        else:
            acc_ref = refs[-1]
            k = pl.program_id(2)

            @pl.when(k == 0)
            def _():
                acc_ref[...] = part

            @pl.when(k > 0)
            def _():
                acc_ref[...] += part

            @pl.when(k == nk - 1)
            def _():
                finish(acc_ref[...])

    in_specs = [pl.BlockSpec((tm, tk), lambda i, j, k: (i, k)),
                pl.BlockSpec((tk, tn), lambda i, j, k: (k, j))]
    for arr, kind in extras:
        if kind == "mn":
            in_specs.append(pl.BlockSpec((tm, tn), lambda i, j, k: (i, j)))
        elif kind == "m":
            in_specs.append(pl.BlockSpec((tm, arr.shape[1]), lambda i, j, k: (i, 0)))
        else:
            in_specs.append(pl.BlockSpec((1, tn), lambda i, j, k: (0, j)))
    outs = pl.pallas_call(
        body, name=name, grid=(M // tm, N // tn, nk),
        in_specs=in_specs,
        out_specs=[pl.BlockSpec((tm, tn), lambda i, j, k: (i, j)) for _ in out_dtypes],
        out_shape=[jax.ShapeDtypeStruct((M, N), dt) for dt in out_dtypes],
        scratch_shapes=[pltpu.VMEM((tm, tn), F32)] if nk > 1 else [],
        compiler_params=_cparams("parallel", "parallel", "arbitrary"),
    )(a, b, *[arr for arr, _ in extras])
    return outs[0] if n_out == 1 else outs


def _mm_tn(name, a, b, *, t1=None, tn=None, ts=None):
    S, K1 = a.shape
    S2, N = b.shape
    assert S == S2, (name, a.shape, b.shape)
    t1 = min(t1 or TN_T1, K1)
    tn = min(tn or TN_TN, N)
    ts = min(ts or TN_TS, S)
    assert K1 % t1 == 0 and N % tn == 0 and S % ts == 0, (name, a.shape, b.shape)

    def body(a_ref, b_ref, o_ref):
        s = pl.program_id(2)
        part = lax.dot_general(a_ref[...].astype(BF16), b_ref[...].astype(BF16),
                               (((0,), (0,)), ((), ())), preferred_element_type=F32)

        @pl.when(s == 0)
        def _():
            o_ref[...] = part

        @pl.when(s > 0)
        def _():
            o_ref[...] += part

    return pl.pallas_call(
        body, name=name, grid=(K1 // t1, N // tn, S // ts),
        in_specs=[pl.BlockSpec((ts, t1), lambda i, j, s: (s, i)),
                  pl.BlockSpec((ts, tn), lambda i, j, s: (s, j))],
        out_specs=pl.BlockSpec((t1, tn), lambda i, j, s: (i, j)),
        out_shape=jax.ShapeDtypeStruct((K1, N), F32),
        compiler_params=_cparams("parallel", "parallel", "arbitrary"),
    )(a, b)


def _rows(arr, width=None, blk=0):
    return (arr, ("rows", arr.shape[1] if width is None else width, blk))


def _full(arr):
    return (arr, ("full",))


def _rowwise(name, fn, ins, outs, *, rows, tm=None):
    tm = min(tm or ROW_TM, rows)
    assert rows % tm == 0, (name, rows, tm)
    n_in = len(ins)

    def body(*refs):
        i = pl.program_id(0)
        vals = fn(*[r[...] for r in refs[:n_in]])
        for (shape, dt, kind), r, v in zip(outs, refs[n_in:], vals):
            if kind == "rows":
                r[...] = v.astype(dt)
            else:
                @pl.when(i == 0)
                def _(r=r, v=v):
                    r[...] = v

                @pl.when(i > 0)
                def _(r=r, v=v):
                    r[...] += v

    in_specs = []
    for arr, spec in ins:
        if spec[0] == "rows":
            in_specs.append(pl.BlockSpec((tm, spec[1]), lambda i, b=spec[2]: (i, b)))
        else:
            in_specs.append(pl.BlockSpec(arr.shape, lambda i, n=arr.ndim: (0,) * n))
    out_specs = []
    for shape, dt, kind in outs:
        if kind == "rows":
            out_specs.append(pl.BlockSpec((tm, shape[1]), lambda i: (i, 0)))
        else:
            out_specs.append(pl.BlockSpec(shape, lambda i, n=len(shape): (0,) * n))
    res = pl.pallas_call(
        body, name=name, grid=(rows // tm,),
        in_specs=in_specs, out_specs=out_specs,
        out_shape=[jax.ShapeDtypeStruct(shape, dt) for shape, dt, _ in outs],
        compiler_params=_cparams("arbitrary"),
    )(*[arr for arr, _ in ins])
    return res


def _flash_fwd(name, q_arr, k_arr, v_arr, *, heads, q_off, k_off, v_off, group, mode, scale, tq, tk,
               bias=None, sinks=None):
    S = q_arr.shape[0]
    Sk = k_arr.shape[0]
    tq = min(tq, S)
    tk = min(tk, Sk)
    nq = S // tq
    if mode == "window":
        assert tq == tk
        nsteps = 2
    else:
        nsteps = Sk // tk

    def kblock(qi, kk):
        if mode == "causal":
            return jnp.minimum(kk, (qi * tq + tq - 1) // tk)
        if mode == "window":
            return jnp.maximum(qi + kk - 1, 0)
        return kk

    has_bias = bias is not None
    has_sink = sinks is not None

    def body(*refs):
        q_ref, k_ref, v_ref = refs[:3]
        pos = 3
        bias_ref = sink_ref = None
        if has_bias:
            bias_ref = refs[pos]
            pos += 1
        if has_sink:
            sink_ref = refs[pos]
            pos += 1
        o_ref, lse_ref, m_sc, l_sc, acc_sc = refs[pos:pos + 5]
        qi = pl.program_id(1)
        kk = pl.program_id(2)

        @pl.when(kk == 0)
        def _():
            if has_sink:
                m_sc[...] = jnp.zeros(m_sc.shape, F32) + sink_ref[0, 0:1, 0:1]
                l_sc[...] = jnp.ones(l_sc.shape, F32)
            else:
                m_sc[...] = jnp.full(m_sc.shape, NEG, F32)
                l_sc[...] = jnp.zeros(l_sc.shape, F32)
            acc_sc[...] = jnp.zeros(acc_sc.shape, F32)

        if mode == "causal":
            run = kk <= (qi * tq + tq - 1) // tk
        elif mode == "window":
            run = qi + kk >= 1
        else:
            run = None

        def step():
            s = lax.dot_general(q_ref[...], k_ref[...], (((1,), (1,)), ((), ())),
                                preferred_element_type=F32) * scale
            if has_bias:
                s = s + bias_ref[0, 0]
            if mode == "causal":
                rows = qi * tq + lax.broadcasted_iota(jnp.int32, (tq, tk), 0)
                cols = kk * tk + lax.broadcasted_iota(jnp.int32, (tq, tk), 1)
                s = jnp.where(cols <= rows, s, NEG)
            m_prev = m_sc[...]
            m_new = jnp.maximum(m_prev, jnp.max(s, axis=1, keepdims=True))
            alpha = jnp.exp(m_prev - m_new)
            p = jnp.exp(s - m_new)
            l_sc[...] = alpha * l_sc[...] + jnp.sum(p, axis=1, keepdims=True)
            acc_sc[...] = alpha * acc_sc[...] + jnp.dot(p.astype(BF16), v_ref[...],
                                                        preferred_element_type=F32)
            m_sc[...] = m_new

        if run is None:
            step()
        else:
            pl.when(run)(step)

        @pl.when(kk == nsteps - 1)
        def _():
            l = l_sc[...]
            o_ref[...] = (acc_sc[...] / l).astype(o_ref.dtype)
            lse_ref[0] = m_sc[...] + jnp.log(l)

    in_specs = [
        pl.BlockSpec((tq, HEAD_PAD), lambda h, qi, kk: (qi, q_off + h)),
        pl.BlockSpec((tk, HEAD_PAD), lambda h, qi, kk: (kblock(qi, kk), k_off + h // group)),
        pl.BlockSpec((tk, HEAD_PAD), lambda h, qi, kk: (kblock(qi, kk), v_off + h // group)),
    ]
    args = [q_arr, k_arr, v_arr]
    if has_bias:
        in_specs.append(pl.BlockSpec((1, 1, tq, tk), lambda h, qi, kk: (h, kk, 0, 0)))
        args.append(bias)
    if has_sink:
        in_specs.append(pl.BlockSpec((1, 8, 128), lambda h, qi, kk: (h, 0, 0)))
        args.append(sinks)
    o, lse = pl.pallas_call(
        body, name=name, grid=(heads, nq, nsteps),
        in_specs=in_specs,
        out_specs=[pl.BlockSpec((tq, HEAD_PAD), lambda h, qi, kk: (qi, h)),
                   pl.BlockSpec((1, tq, 1), lambda h, qi, kk: (h, qi, 0))],
        out_shape=[jax.ShapeDtypeStruct((S, heads * HEAD_PAD), BF16),
                   jax.ShapeDtypeStruct((heads, S, 1), F32)],
        scratch_shapes=[pltpu.VMEM((tq, 1), F32), pltpu.VMEM((tq, 1), F32), pltpu.VMEM((tq, HEAD_PAD), F32)],
        compiler_params=_cparams("parallel", "parallel", "arbitrary"),
    )(*args)
    return o, lse


def _flash_bwd(name, q_arr, k_arr, v_arr, o_arr, do_arr, lse, *, heads, q_off, k_off, v_off, group, mode,
               scale, tq, tk, bias=None, sinks=None):
    S = q_arr.shape[0]
    Sk = k_arr.shape[0]
    tq = min(tq, S)
    tk = min(tk, Sk)
    nq = S // tq
    nkb = Sk // tk
    if mode == "window":
        assert tq == tk
        nsteps = 2
    else:
        nsteps = nq

    def qblock(kj, qq):
        if mode == "causal":
            return jnp.maximum(qq, (kj * tk) // tq)
        if mode == "window":
            return jnp.minimum(kj + qq, nq - 1)
        return qq

    has_bias = bias is not None
    has_sink = sinks is not None

    def body(*refs):
        q_ref, k_ref, v_ref, o_ref, do_ref, lse_ref = refs[:6]
        pos = 6
        bias_ref = sink_ref = None
        if has_bias:
            bias_ref = refs[pos]
            pos += 1
        if has_sink:
            sink_ref = refs[pos]
            pos += 1
        dq_ref, dk_ref, dv_ref = refs[pos:pos + 3]
        pos += 3
        dbias_ref = dsink_ref = None
        if has_bias:
            dbias_ref = refs[pos]
            pos += 1
        if has_sink:
            dsink_ref = refs[pos]
            pos += 1
        dk_sc, dv_sc = refs[pos:pos + 2]
        kj = pl.program_id(1)
        qq = pl.program_id(2)
        qb = qblock(kj, qq)

        @pl.when((kj == 0) & (qq == 0))
        def _():
            dq_ref[...] = jnp.zeros(dq_ref.shape, F32)
            if has_bias:
                dbias_ref[...] = jnp.zeros(dbias_ref.shape, F32)
            if has_sink:
                dsink_ref[...] = jnp.zeros(dsink_ref.shape, F32)

        @pl.when(qq == 0)
        def _():
            dk_sc[...] = jnp.zeros(dk_sc.shape, F32)
            dv_sc[...] = jnp.zeros(dv_sc.shape, F32)

        if mode == "causal":
            run = qq >= (kj * tk) // tq
        elif mode == "window":
            run = kj + qq <= nq - 1
        else:
            run = None

        def step():
            q = q_ref[...]
            k = k_ref[...]
            do = do_ref[...]
            lse_v = lse_ref[0]
            s = lax.dot_general(q, k, (((1,), (1,)), ((), ())), preferred_element_type=F32) * scale
            if has_bias:
                s = s + bias_ref[0, 0]
            if mode == "causal":
                rows = qb * tq + lax.broadcasted_iota(jnp.int32, (tq, tk), 0)
                cols = kj * tk + lax.broadcasted_iota(jnp.int32, (tq, tk), 1)
                s = jnp.where(cols <= rows, s, NEG)
            p = jnp.exp(s - lse_v)
            delta = jnp.sum(do.astype(F32) * o_ref[...].astype(F32), axis=1, keepdims=True)
            dv_sc[...] += lax.dot_general(p.astype(BF16), do, (((0,), (0,)), ((), ())),
                                          preferred_element_type=F32)
            dp = lax.dot_general(do, v_ref[...], (((1,), (1,)), ((), ())), preferred_element_type=F32)
            dsp = p * (dp - delta)
            if has_bias:
                @pl.when(qq == 0)
                def _():
                    dbias_ref[0, 1] += dsp

                @pl.when(qq == 1)
                def _():
                    dbias_ref[0, 0] += dsp
            ds = (dsp * scale).astype(BF16)
            row0 = pl.multiple_of(qb * tq, tq)
            dq_ref[pl.ds(row0, tq), :] += jnp.dot(ds, k, preferred_element_type=F32)
            dk_sc[...] += lax.dot_general(ds, q, (((0,), (0,)), ((), ())), preferred_element_type=F32)
            if has_sink:
                @pl.when(qq == 0)
                def _():
                    ps = jnp.exp(sink_ref[0, 0:1, 0:1] - lse_v)
                    dsink_ref[...] += jnp.zeros(dsink_ref.shape, F32) - jnp.sum(ps * delta)

        if run is None:
            step()
        else:
            pl.when(run)(step)

        @pl.when(qq == nsteps - 1)
        def _():
            dk_ref[...] = dk_sc[...]
            dv_ref[...] = dv_sc[...]

    def bias_blk(h, kj, qq):
        return (h, 1 - qq, 0, 0)

    in_specs = [
        pl.BlockSpec((tq, HEAD_PAD), lambda h, kj, qq: (qblock(kj, qq), q_off + h)),
        pl.BlockSpec((tk, HEAD_PAD), lambda h, kj, qq: (kj, k_off + h // group)),
        pl.BlockSpec((tk, HEAD_PAD), lambda h, kj, qq: (kj, v_off + h // group)),
        pl.BlockSpec((tq, HEAD_PAD), lambda h, kj, qq: (qblock(kj, qq), h)),
        pl.BlockSpec((tq, HEAD_PAD), lambda h, kj, qq: (qblock(kj, qq), h)),
        pl.BlockSpec((1, tq, 1), lambda h, kj, qq: (h, qblock(kj, qq), 0)),
    ]
    args = [q_arr, k_arr, v_arr, o_arr, do_arr, lse]
    out_specs = [
        pl.BlockSpec((S, HEAD_PAD), lambda h, kj, qq: (0, h)),
        pl.BlockSpec((tk, HEAD_PAD), lambda h, kj, qq: (kj, h)),
        pl.BlockSpec((tk, HEAD_PAD), lambda h, kj, qq: (kj, h)),
    ]
    out_shape = [jax.ShapeDtypeStruct((S, heads * HEAD_PAD), F32),
                 jax.ShapeDtypeStruct((Sk, heads * HEAD_PAD), F32),
                 jax.ShapeDtypeStruct((Sk, heads * HEAD_PAD), F32)]
    if has_bias:
        in_specs.append(pl.BlockSpec((1, 1, tq, tk), bias_blk))
        args.append(bias)
        out_specs.append(pl.BlockSpec((1, 2, tq, tk), lambda h, kj, qq: (h, 0, 0, 0)))
        out_shape.append(jax.ShapeDtypeStruct((heads, 2, tq, tk), F32))
    if has_sink:
        in_specs.append(pl.BlockSpec((1, 8, 128), lambda h, kj, qq: (h, 0, 0)))
        args.append(sinks)
        out_specs.append(pl.BlockSpec((1, 8, 128), lambda h, kj, qq: (h, 0, 0)))
        out_shape.append(jax.ShapeDtypeStruct((heads, 8, 128), F32))
    return pl.pallas_call(
        body, name=name, grid=(heads, nkb, nsteps),
        in_specs=in_specs, out_specs=out_specs, out_shape=out_shape,
        scratch_shapes=[pltpu.VMEM((tk, HEAD_PAD), F32), pltpu.VMEM((tk, HEAD_PAD), F32)],
        compiler_params=_cparams("arbitrary", "arbitrary", "arbitrary"),
    )(*args)


def _causal_fwd(name, q_arr, k_arr, v_arr, *, heads, tile, chunk):
    S = q_arr.shape[0]
    T = min(tile, S)
    C = min(chunk, T)
    nt = S // T
    nc = T // C

    pairs = [(qi, kk) for qi in range(nt) for kk in range(qi + 1)]
    q_tab = jnp.asarray([p[0] for p in pairs], jnp.int32)
    k_tab = jnp.asarray([p[1] for p in pairs], jnp.int32)

    def body(qt_ref, kt_ref, q_ref, k_ref, v_ref, o_ref, lse_ref, m_sc, acc_sc):
        t = pl.program_id(1)
        qi = qt_ref[t]
        kk = kt_ref[t]

        @pl.when(kk == 0)
        def _():
            m_sc[...] = jnp.full(m_sc.shape, NEG, F32)
            acc_sc[...] = jnp.zeros(acc_sc.shape, F32)

        def logits(c, ncols, masked):
            s = lax.dot_general(q_ref[pl.ds(c * C, C), :], k_ref[0:ncols, :], (((1,), (1,)), ((), ())),
                                preferred_element_type=F32)
            if masked:
                r = c * C + lax.broadcasted_iota(jnp.int32, (C, ncols), 0)
                cidx = lax.broadcasted_iota(jnp.int32, (C, ncols), 1)
                s = jnp.where(cidx <= r, s, NEG)
            return s

        def update(c, ncols, s):
            rows = pl.ds(c * C, C)
            m_prev = m_sc[rows, :]
            m_new = jnp.maximum(m_prev, jnp.max(s, axis=1, keepdims=True))
            p = jnp.exp2(s - m_new).astype(BF16)
            acc_sc[rows, :] = jnp.exp2(m_prev - m_new) * acc_sc[rows, :] + jnp.dot(
                p, v_ref[0:ncols, :], preferred_element_type=F32)
            m_sc[rows, :] = m_new

        def tile_step(ncols_of, masked):
            s = logits(0, ncols_of(0), masked)
            for c in range(nc):
                s_next = logits(c + 1, ncols_of(c + 1), masked) if c + 1 < nc else None
                update(c, ncols_of(c), s)
                s = s_next

        @pl.when(kk < qi)
        def _():
            tile_step(lambda c: T, False)

        @pl.when(kk == qi)
        def _():
            tile_step(lambda c: (c + 1) * C, True)

        @pl.when(kk == qi)
        def _():
            acc = acc_sc[...]
            l = acc[:, DEN_LANE:DEN_LANE + 1]
            o_ref[...] = (acc / l).astype(o_ref.dtype)
            lse_ref[0] = m_sc[...] + jnp.log2(l)

    grid_spec = pltpu.PrefetchScalarGridSpec(
        num_scalar_prefetch=2, grid=(heads, len(pairs)),
        in_specs=[pl.BlockSpec((T, HEAD_PAD), lambda h, t, qt, kt: (qt[t], h)),
                  pl.BlockSpec((T, HEAD_PAD), lambda h, t, qt, kt: (kt[t], h)),
                  pl.BlockSpec((T, HEAD_PAD), lambda h, t, qt, kt: (kt[t], h))],
        out_specs=[pl.BlockSpec((T, HEAD_PAD), lambda h, t, qt, kt: (qt[t], h)),
                   pl.BlockSpec((1, T, 1), lambda h, t, qt, kt: (h, qt[t], 0))],
        scratch_shapes=[pltpu.VMEM((T, 1), F32), pltpu.VMEM((T, HEAD_PAD), F32)])
    return pl.pallas_call(
        body, name=name, grid_spec=grid_spec,
        out_shape=[jax.ShapeDtypeStruct((S, heads * HEAD_PAD), BF16),
                   jax.ShapeDtypeStruct((heads, S, 1), F32)],
        compiler_params=_cparams("parallel", "arbitrary"),
    )(q_tab, k_tab, q_arr, k_arr, v_arr)


def _row_dot(name, a, b, *, heads, tm):
    S = a.shape[0]
    tm = min(tm, S)

    def body(a_ref, b_ref, o_ref):
        o_ref[0] = jnp.sum(a_ref[...].astype(F32) * b_ref[...].astype(F32), axis=1, keepdims=True)

    return pl.pallas_call(
        body, name=name, grid=(heads, S // tm),
        in_specs=[pl.BlockSpec((tm, HEAD_PAD), lambda h, i: (i, h)),
                  pl.BlockSpec((tm, HEAD_PAD), lambda h, i: (i, h))],
        out_specs=pl.BlockSpec((1, tm, 1), lambda h, i: (h, i, 0)),
        out_shape=jax.ShapeDtypeStruct((heads, S, 1), F32),
        compiler_params=_cparams("parallel", "parallel"),
    )(a, b)


def _causal_bwd(name, q_arr, k_arr, v_arr, do_arr, lse, delta, *, heads, tile, chunk):
    S = q_arr.shape[0]
    T = min(tile, S)
    C = min(chunk, T)
    nt = S // T
    nc = T // C

    pairs = [(kj, qq) for kj in range(nt) for qq in range(kj, nt)]
    k_tab = jnp.asarray([p[0] for p in pairs], jnp.int32)
    q_tab = jnp.asarray([p[1] for p in pairs], jnp.int32)

    def body(kt_ref, qt_ref, q_ref, k_ref, v_ref, do_ref, lse_ref, delta_ref, dq_ref, dk_ref, dv_ref, dk_sc, dv_sc):
        t = pl.program_id(1)
        kj = kt_ref[t]
        qq = qt_ref[t]
        qb = qq

        @pl.when(t == 0)
        def _():
            dq_ref[...] = jnp.zeros(dq_ref.shape, F32)

        @pl.when(qq == kj)
        def _():
            dk_sc[...] = jnp.zeros(dk_sc.shape, F32)
            dv_sc[...] = jnp.zeros(dv_sc.shape, F32)

        def logits(c, ncols, masked):
            rows = pl.ds(c * C, C)
            s = lax.dot_general(q_ref[rows, :], k_ref[0:ncols, :], (((1,), (1,)), ((), ())),
                                preferred_element_type=F32)
            if masked:
                r = c * C + lax.broadcasted_iota(jnp.int32, (C, ncols), 0)
                cidx = lax.broadcasted_iota(jnp.int32, (C, ncols), 1)
                s = jnp.where(cidx <= r, s, NEG)
            dp = lax.dot_general(do_ref[rows, :], v_ref[0:ncols, :], (((1,), (1,)), ((), ())),
                                 preferred_element_type=F32)
            return s, dp

        def update(c, ncols, s, dp):
            rows = pl.ds(c * C, C)
            p = jnp.exp2(s - lse_ref[0, rows, :])
            ds = (p * (dp - delta_ref[0, rows, :])).astype(BF16)
            dv_sc[0:ncols, :] += lax.dot_general(p.astype(BF16), do_ref[rows, :], (((0,), (0,)), ((), ())),
                                                 preferred_element_type=F32)
            dk_sc[0:ncols, :] += lax.dot_general(ds, q_ref[rows, :], (((0,), (0,)), ((), ())),
                                                 preferred_element_type=F32)
            row0 = pl.multiple_of(qb * T + c * C, C)
            dq_ref[pl.ds(row0, C), :] += jnp.dot(ds, k_ref[0:ncols, :], preferred_element_type=F32)

        def tile_step(ncols_of, masked):
            cur = logits(0, ncols_of(0), masked)
            for c in range(nc):
                nxt = logits(c + 1, ncols_of(c + 1), masked) if c + 1 < nc else None
                update(c, ncols_of(c), *cur)
                cur = nxt

        @pl.when(qq > kj)
        def _():
            tile_step(lambda c: T, False)

        @pl.when(qq == kj)
        def _():
            tile_step(lambda c: (c + 1) * C, True)

        @pl.when(qq == nt - 1)
        def _():
            dk_ref[...] = dk_sc[...] * math.log(2.0)
            dv_ref[...] = dv_sc[...]

    qrow = pl.BlockSpec((T, HEAD_PAD), lambda h, t, kt, qt: (qt[t], h))
    krow = pl.BlockSpec((T, HEAD_PAD), lambda h, t, kt, qt: (kt[t], h))
    qcol = pl.BlockSpec((1, T, 1), lambda h, t, kt, qt: (h, qt[t], 0))
    grid_spec = pltpu.PrefetchScalarGridSpec(
        num_scalar_prefetch=2, grid=(heads, len(pairs)),
        in_specs=[qrow, krow, krow, qrow, qcol, qcol],
        out_specs=[pl.BlockSpec((S, HEAD_PAD), lambda h, t, kt, qt: (0, h)), krow, krow],
        scratch_shapes=[pltpu.VMEM((T, HEAD_PAD), F32), pltpu.VMEM((T, HEAD_PAD), F32)])
    return pl.pallas_call(
        body, name=name, grid_spec=grid_spec,
        out_shape=[jax.ShapeDtypeStruct((S, heads * HEAD_PAD), F32)] * 3,
        compiler_params=_cparams("arbitrary", "arbitrary"),
    )(k_tab, q_tab, q_arr, k_arr, v_arr, do_arr, lse, delta)


SWA_R = SWA_HEADS // SWA_KV_HEADS
SWA_SCALE = SWA_HD ** -0.5
SWA_Q0, SWA_K0, SWA_V0 = 0, 12, 14


def _swa_specs(tq):
    nsb = tq // WINDOW
    return [
        pl.BlockSpec((tq, SWA_R * HEAD_PAD), lambda g, i: (i, g)),
        pl.BlockSpec((tq, HEAD_PAD), lambda g, i: (i, SWA_K0 + g)),
        pl.BlockSpec((WINDOW, HEAD_PAD), lambda g, i: (jnp.maximum(nsb * i - 1, 0), SWA_K0 + g)),
        pl.BlockSpec((tq, HEAD_PAD), lambda g, i: (i, SWA_V0 + g)),
        pl.BlockSpec((WINDOW, HEAD_PAD), lambda g, i: (jnp.maximum(nsb * i - 1, 0), SWA_V0 + g)),
        pl.BlockSpec((SWA_R, WINDOW, 2 * WINDOW), lambda g, i: (g, 0, 0)),
        pl.BlockSpec((SWA_R, 8, 128), lambda g, i: (g, 0, 0)),
    ]


def _swa_block(i, sb, q_ref, kc_ref, kp_ref, vc_ref, vp_ref, bias, sink):
    rows = slice(sb * WINDOW, (sb + 1) * WINDOW)
    qs = jnp.concatenate([q_ref[rows, hh * HEAD_PAD:(hh + 1) * HEAD_PAD] for hh in range(SWA_R)], axis=0)
    if sb == 0:
        kp, vp = kp_ref[...], vp_ref[...]
    else:
        prev = slice((sb - 1) * WINDOW, sb * WINDOW)
        kp, vp = kc_ref[prev, :], vc_ref[prev, :]
    kk = jnp.concatenate([kp, kc_ref[rows, :]], axis=0)
    vv = jnp.concatenate([vp, vc_ref[rows, :]], axis=0)
    s = lax.dot_general(qs, kk, (((1,), (1,)), ((), ())), preferred_element_type=F32) * SWA_SCALE + bias
    if sb == 0:
        col = lax.broadcasted_iota(jnp.int32, (1, 2 * WINDOW), 1)
        s = s + jnp.where((col < WINDOW) & (i == 0), NEG, 0.0)
    return rows, qs, kk, vv, s


def _stack_heads(ref, rows, lead=None):
    if lead is None:
        return jnp.concatenate([ref[rows, hh * HEAD_PAD:(hh + 1) * HEAD_PAD] for hh in range(SWA_R)], axis=0)
    return jnp.concatenate([ref[hh, rows, :] for hh in range(SWA_R)], axis=0)


def _swa_fwd(name, proj_b, bias, sinks, *, tq):
    S = proj_b.shape[0]
    tq = min(tq, S)
    nsb = tq // WINDOW

    def body(q_ref, kc_ref, kp_ref, vc_ref, vp_ref, bias_ref, sink_ref, o_ref, lse_ref):
        i = pl.program_id(1)
        bias_v = bias_ref[...].reshape(SWA_R * WINDOW, 2 * WINDOW)
        sink = jnp.concatenate([jnp.zeros((WINDOW, 1), F32) + sink_ref[hh, 0:1, 0:1] for hh in range(SWA_R)], axis=0)
        for sb in range(nsb):
            rows, _, _, vv, s = _swa_block(i, sb, q_ref, kc_ref, kp_ref, vc_ref, vp_ref, bias_v, sink)
            m = jnp.maximum(jnp.max(s, axis=1, keepdims=True), sink)
            p = jnp.exp(s - m)
            l = jnp.sum(p, axis=1, keepdims=True) + jnp.exp(sink - m)
            o = jnp.dot(p.astype(BF16), vv, preferred_element_type=F32) / l
            lse_v = m + jnp.log(l)
            for hh in range(SWA_R):
                o_ref[rows, hh * HEAD_PAD:(hh + 1) * HEAD_PAD] = o[hh * WINDOW:(hh + 1) * WINDOW].astype(o_ref.dtype)
                lse_ref[hh, rows, :] = lse_v[hh * WINDOW:(hh + 1) * WINDOW]

    return pl.pallas_call(
        body, name=name, grid=(SWA_KV_HEADS, S // tq),
        in_specs=_swa_specs(tq),
        out_specs=[pl.BlockSpec((tq, SWA_R * HEAD_PAD), lambda g, i: (i, g)),
                   pl.BlockSpec((SWA_R, tq, 1), lambda g, i: (g, i, 0))],
        out_shape=[jax.ShapeDtypeStruct((S, SWA_HEADS * HEAD_PAD), BF16),
                   jax.ShapeDtypeStruct((SWA_HEADS, S, 1), F32)],
        compiler_params=_cparams("parallel", "parallel"),
    )(proj_b, proj_b, proj_b, proj_b, proj_b, bias, sinks)


def _swa_bwd(name, proj_b, bias, sinks, o, do, lse, *, tq):
    S = proj_b.shape[0]
    tq = min(tq, S)
    nsb = tq // WINDOW
    nq = S // tq

    def body(q_ref, kc_ref, kp_ref, vc_ref, vp_ref, bias_ref, sink_ref, o_ref, do_ref, lse_ref,
             dq_ref, dk_ref, dv_ref, dke_ref, dve_ref, dbias_ref, dsink_ref):
        i = pl.program_id(1)

        @pl.when(i == 0)
        def _():
            dbias_ref[...] = jnp.zeros(dbias_ref.shape, F32)
            dsink_ref[...] = jnp.zeros(dsink_ref.shape, F32)

        bias_v = bias_ref[...].reshape(SWA_R * WINDOW, 2 * WINDOW)
        sink = jnp.concatenate([jnp.zeros((WINDOW, 1), F32) + sink_ref[hh, 0:1, 0:1] for hh in range(SWA_R)], axis=0)
        dk_own, dv_own, dk_prev, dv_prev = [], [], [], []
        dbias_acc = jnp.zeros((SWA_R * WINDOW, 2 * WINDOW), F32)
        for sb in range(nsb):
            rows, qs, kk, vv, s = _swa_block(i, sb, q_ref, kc_ref, kp_ref, vc_ref, vp_ref, bias_v, sink)
            lse_v = _stack_heads(lse_ref, rows, lead=True)
            do_s = _stack_heads(do_ref, rows)
            delta = jnp.sum(do_s.astype(F32) * _stack_heads(o_ref, rows).astype(F32), axis=1, keepdims=True)
            p = jnp.exp(s - lse_v)
            dp = lax.dot_general(do_s, vv, (((1,), (1,)), ((), ())), preferred_element_type=F32)
            dsp = p * (dp - delta)
            dbias_acc = dbias_acc + dsp
            ds = (dsp * SWA_SCALE).astype(BF16)
            dq = jnp.dot(ds, kk, preferred_element_type=F32)
            dkk = lax.dot_general(ds, qs, (((0,), (0,)), ((), ())), preferred_element_type=F32)
            dvv = lax.dot_general(p.astype(BF16), do_s, (((0,), (0,)), ((), ())), preferred_element_type=F32)
            dk_prev.append(dkk[:WINDOW])
            dk_own.append(dkk[WINDOW:])
            dv_prev.append(dvv[:WINDOW])
            dv_own.append(dvv[WINDOW:])
            psink = jnp.exp(sink - lse_v) * delta
            for hh in range(SWA_R):
                hrows = slice(hh * WINDOW, (hh + 1) * WINDOW)
                dq_ref[rows, hh * HEAD_PAD:(hh + 1) * HEAD_PAD] = dq[hrows].astype(dq_ref.dtype)
                dsink_ref[hh] += jnp.zeros((8, 128), F32) - jnp.sum(psink[hrows])
        dbias_ref[...] += dbias_acc.reshape(SWA_R, WINDOW, 2 * WINDOW)
        for sb in range(nsb):
            rows = slice(sb * WINDOW, (sb + 1) * WINDOW)
            if sb + 1 < nsb:
                dk_ref[rows, :] = dk_own[sb] + dk_prev[sb + 1]
                dv_ref[rows, :] = dv_own[sb] + dv_prev[sb + 1]
            else:
                dk_ref[rows, :] = dk_own[sb]
                dv_ref[rows, :] = dv_own[sb]
        dke_ref[...] = dk_prev[0]
        dve_ref[...] = dv_prev[0]

    in_specs = _swa_specs(tq) + [
        pl.BlockSpec((tq, SWA_R * HEAD_PAD), lambda g, i: (i, g)),
        pl.BlockSpec((tq, SWA_R * HEAD_PAD), lambda g, i: (i, g)),
        pl.BlockSpec((SWA_R, tq, 1), lambda g, i: (g, i, 0)),
    ]
    kv_blk = pl.BlockSpec((tq, HEAD_PAD), lambda g, i: (i, g))
    edge_blk = pl.BlockSpec((WINDOW, HEAD_PAD), lambda g, i: (i, g))
    return pl.pallas_call(
        body, name=name, grid=(SWA_KV_HEADS, nq),
        in_specs=in_specs,
        out_specs=[pl.BlockSpec((tq, SWA_R * HEAD_PAD), lambda g, i: (i, g)), kv_blk, kv_blk, edge_blk, edge_blk,
                   pl.BlockSpec((SWA_R, WINDOW, 2 * WINDOW), lambda g, i: (g, 0, 0)),
                   pl.BlockSpec((SWA_R, 8, 128), lambda g, i: (g, 0, 0))],
        out_shape=[jax.ShapeDtypeStruct((S, SWA_HEADS * HEAD_PAD), BF16),
                   jax.ShapeDtypeStruct((S, SWA_KV_HEADS * HEAD_PAD), F32),
                   jax.ShapeDtypeStruct((S, SWA_KV_HEADS * HEAD_PAD), F32),
                   jax.ShapeDtypeStruct((nq * WINDOW, SWA_KV_HEADS * HEAD_PAD), F32),
                   jax.ShapeDtypeStruct((nq * WINDOW, SWA_KV_HEADS * HEAD_PAD), F32),
                   jax.ShapeDtypeStruct((SWA_HEADS, WINDOW, 2 * WINDOW), F32),
                   jax.ShapeDtypeStruct((SWA_HEADS, 8, 128), F32)],
        compiler_params=_cparams("arbitrary", "arbitrary"),
    )(proj_b, proj_b, proj_b, proj_b, proj_b, bias, sinks, o, do, lse)


def _dproj_b(name, dq_swa, dq_mem, dk, dv, dk_edge, dv_edge, *, tq):
    S = dq_swa.shape[0]
    tq = min(tq, S)
    nq = S // tq

    def body(dqs_ref, dqm_ref, dk_ref, dv_ref, dke_ref, dve_ref, o_ref):
        i = pl.program_id(0)
        o_ref[:, 0:1024] = dqs_ref[...]
        o_ref[:, 1024:1536] = dqm_ref[...].astype(o_ref.dtype)
        o_ref[:, 1536:1792] = dk_ref[...].astype(o_ref.dtype)
        o_ref[:, 1792:2048] = dv_ref[...].astype(o_ref.dtype)

        @pl.when(i < nq - 1)
        def _():
            last = slice(tq - WINDOW, tq)
            o_ref[last, 1536:1792] = (dk_ref[last, :] + dke_ref[...]).astype(o_ref.dtype)
            o_ref[last, 1792:2048] = (dv_ref[last, :] + dve_ref[...]).astype(o_ref.dtype)

    edge = pl.BlockSpec((WINDOW, SWA_KV_HEADS * HEAD_PAD), lambda i: (jnp.minimum(i + 1, nq - 1), 0))
    return pl.pallas_call(
        body, name=name, grid=(nq,),
        in_specs=[pl.BlockSpec((tq, 1024), lambda i: (i, 0)), pl.BlockSpec((tq, 512), lambda i: (i, 0)),
                  pl.BlockSpec((tq, 256), lambda i: (i, 0)), pl.BlockSpec((tq, 256), lambda i: (i, 0)), edge, edge],
        out_specs=pl.BlockSpec((tq, 2048), lambda i: (i, 0)),
        out_shape=jax.ShapeDtypeStruct((S, 2048), BF16),
        compiler_params=_cparams("parallel"),
    )(dq_swa, dq_mem, dk, dv, dk_edge, dv_edge)


def _exchange(name, send, *, per_peer):
    shape = send.shape[1:] if per_peer else send.shape

    def body(send_ref, recv_ref, send_sems, recv_sems, local_sem):
        x, y, c = lax.axis_index("x"), lax.axis_index("y"), lax.axis_index("c")
        me = 4 * x + 2 * y + c
        own = pltpu.make_async_copy(send_ref.at[me] if per_peer else send_ref, recv_ref.at[me], local_sem)
        own.start()
        copies = []
        for k in range(1, N_DEV):
            px = 1 - x if (k >> 2) & 1 else x
            py = 1 - y if (k >> 1) & 1 else y
            pc = 1 - c if k & 1 else c
            peer = 4 * px + 2 * py + pc
            out = pltpu.make_async_remote_copy(
                src_ref=send_ref.at[peer] if per_peer else send_ref, dst_ref=recv_ref.at[me],
                send_sem=send_sems.at[k - 1], recv_sem=recv_sems.at[k - 1],
                device_id=(px, py, pc), device_id_type=pl.DeviceIdType.MESH)
            out.start()
            back = pltpu.make_async_remote_copy(
                src_ref=send_ref.at[me] if per_peer else send_ref, dst_ref=recv_ref.at[peer],
                send_sem=send_sems.at[k - 1], recv_sem=recv_sems.at[k - 1],
                device_id=(px, py, pc), device_id_type=pl.DeviceIdType.MESH)
            copies.append((out, back))
        for out, back in copies:
            out.wait_send()
            back.wait_recv()
        own.wait()

    return pl.pallas_call(
        body, name=name,
        in_specs=[pl.BlockSpec(memory_space=pl.ANY)],
        out_specs=pl.BlockSpec(memory_space=pl.ANY),
        out_shape=jax.ShapeDtypeStruct((N_DEV,) + tuple(shape), send.dtype),
        scratch_shapes=[pltpu.SemaphoreType.DMA((N_DEV - 1,)), pltpu.SemaphoreType.DMA((N_DEV - 1,)),
                        pltpu.SemaphoreType.DMA(())],
    )(send)


def _adam(name, recv, w, m, v, *, tm=None):
    R = w.shape[0]
    tm = max(t for t in range(8, min(tm or ADAM_TM, R) + 1, 8) if R % t == 0)
    c1 = 1.0 / (1.0 - ADAM_B1 ** ADAM_STEP)
    c2 = 1.0 / (1.0 - ADAM_B2 ** ADAM_STEP)

    def body(r_ref, w_ref, m_ref, v_ref, g_ref, d_ref, nm_ref, nv_ref):
        g = r_ref[0].astype(F32)
        for j in range(1, N_DEV):
            g = g + r_ref[j].astype(F32)
        wv = w_ref[...]
        nm = ADAM_B1 * m_ref[...] + (1.0 - ADAM_B1) * g
        nv = ADAM_B2 * v_ref[...] + (1.0 - ADAM_B2) * (g * g)
        m_hat = nm * c1
        v_hat = nv * c2
        g_ref[...] = g
        d_ref[...] = -ADAM_LR * (m_hat / (jnp.sqrt(v_hat) + ADAM_EPS) + ADAM_WD * wv)
        nm_ref[...] = nm
        nv_ref[...] = nv

    row = pl.BlockSpec((tm, 128), lambda i: (i, 0))
    return pl.pallas_call(
        body, name=name, grid=(R // tm,),
        in_specs=[pl.BlockSpec((N_DEV, tm, 128), lambda i: (0, i, 0)), row, row, row],
        out_specs=[row, row, row, row],
        out_shape=[jax.ShapeDtypeStruct((R, 128), F32)] * 4,
        compiler_params=_cparams("parallel"),
    )(recv, w, m, v)


def _pack_rows(arrs):
    return jnp.concatenate([a.reshape(-1, 128) for a in arrs], axis=0)


def _unstack(g, shape, axis):
    t = jnp.moveaxis(g, 0, axis)
    return t.reshape(shape)


def _restack(full, axis):
    s = full.shape
    t = full.reshape(s[:axis] + (N_DEV, s[axis] // N_DEV) + s[axis + 1:])
    return jnp.moveaxis(t, axis, 0)


def _pad_heads(w, heads, hd, axis):
    s = w.shape
    t = w.reshape(s[:axis] + (heads, hd) + s[axis + 1:])
    pad = [(0, 0)] * t.ndim
    pad[axis + 1] = (0, HEAD_PAD - hd)
    t = jnp.pad(t, pad)
    return t.reshape(s[:axis] + (heads * HEAD_PAD,) + s[axis + 1:])


def _unpad_heads(w, heads, hd, axis):
    s = w.shape
    t = w.reshape(s[:axis] + (heads, HEAD_PAD) + s[axis + 1:])
    t = lax.slice_in_dim(t, 0, hd, axis=axis + 1)
    return t.reshape(s[:axis] + (heads * hd,) + s[axis + 1:])


def _layer_weights(full, l):
    w_in = full["w_in"][l]
    cq, kva, qs, ks, vs, qm, gates = (w_in[:, 0:256], w_in[:, 256:416], w_in[:, 416:928], w_in[:, 928:1056],
                                       w_in[:, 1056:1184], w_in[:, 1184:1696], w_in[:, 1696:4768])
    wa = jnp.concatenate([gates, cq, jnp.pad(kva, ((0, 0), (0, 96)))], axis=1)
    wb = jnp.concatenate([_pad_heads(qs, SWA_HEADS, SWA_HD, 1), qm, _pad_heads(ks, SWA_KV_HEADS, SWA_HD, 1),
                          _pad_heads(vs, SWA_KV_HEADS, SWA_HD, 1)], axis=1)
    wuq = _pad_heads(full["w_uq"][l], MLA_HEADS, MLA_NOPE + MLA_ROPE, 1)
    ukv = full["w_ukv"][l].reshape(MLA_KV_LORA, MLA_HEADS, MLA_NOPE + MLA_V)
    wuk = _pad_heads(ukv[:, :, :MLA_NOPE].reshape(MLA_KV_LORA, -1), MLA_HEADS, MLA_NOPE, 1)
    wuv = _pad_heads(ukv[:, :, MLA_NOPE:].reshape(MLA_KV_LORA, -1), MLA_HEADS, MLA_V, 1)
    wo_mla = _pad_heads(full["w_o_mla"][l], MLA_HEADS, MLA_V, 0)
    wo_swa = _pad_heads(full["w_o_swa"][l], SWA_HEADS, SWA_HD, 0)
    wo_mem = full["w_o_mem"][l]
    w = dict(wag=wa[:, :3072], wat=wa[:, 3072:], wb=wb, wuq=wuq, wuk=wuk, wuv=wuv, wo_mla=wo_mla, wo_swa=wo_swa,
             wo_mem=wo_mem, wmem=full["w_mem_kv"][l], wout=full["w_out"][l], wup=full["w_up"][l],
             wdown=full["w_down"][l])
    w.update({k + "_t": v.T for k, v in w.items()})
    return w


def _layer_weight_grads(g):
    dwa_g, dwa_t, dwb = g["wag"], g["wat"], g["wb"]
    d_in = jnp.concatenate([
        dwa_t[:, 0:256], dwa_t[:, 256:416],
        _unpad_heads(dwb[:, 0:1024], SWA_HEADS, SWA_HD, 1),
        _unpad_heads(dwb[:, 1536:1792], SWA_KV_HEADS, SWA_HD, 1),
        _unpad_heads(dwb[:, 1792:2048], SWA_KV_HEADS, SWA_HD, 1),
        dwb[:, 1024:1536], dwa_g], axis=1)
    duk = _unpad_heads(g["wuk"], MLA_HEADS, MLA_NOPE, 1).reshape(MLA_KV_LORA, MLA_HEADS, MLA_NOPE)
    duv = _unpad_heads(g["wuv"], MLA_HEADS, MLA_V, 1).reshape(MLA_KV_LORA, MLA_HEADS, MLA_V)
    return dict(
        w_in=d_in,
        w_uq=_unpad_heads(g["wuq"], MLA_HEADS, MLA_NOPE + MLA_ROPE, 1),
        w_ukv=jnp.concatenate([duk, duv], axis=2).reshape(MLA_KV_LORA, -1),
        w_mem_kv=g["wmem"],
        w_o_mla=_unpad_heads(g["wo_mla"], MLA_HEADS, MLA_V, 0),
        w_o_swa=_unpad_heads(g["wo_swa"], SWA_HEADS, SWA_HD, 0),
        w_o_mem=g["wo_mem"], w_out=g["wout"], w_up=g["wup"], w_down=g["wdown"])


def _rope_tables(S):
    pos = jnp.arange(S, dtype=F32)
    inv = 1.0 / (ROPE_THETA ** (jnp.arange(0, MLA_ROPE, 2, dtype=F32) / MLA_ROPE))
    ang = pos[:, None] * inv[None, :]
    cos, sin = jnp.cos(ang), jnp.sin(ang)
    z16 = jnp.zeros((S, 16), F32)
    z32 = jnp.zeros((S, 32), F32)
    c = jnp.concatenate([jnp.ones((S, 64), F32), cos, cos, z32], axis=1)
    ck = jnp.concatenate([jnp.zeros((S, 64), F32), cos, cos, z32], axis=1)
    s1 = jnp.concatenate([jnp.zeros((S, 80), F32), sin, z32], axis=1)
    s2 = jnp.concatenate([jnp.zeros((S, 64), F32), -sin, z16, z32], axis=1)
    return c, ck, s1, s2


def _t5_bucket(dist):
    n = jnp.maximum(dist, 0)
    max_exact = REL_BUCKETS // 2
    nf = jnp.maximum(n, 1).astype(F32)
    large = max_exact + (jnp.log(nf / max_exact) / math.log(REL_MAX_DIST / max_exact)
                         * (REL_BUCKETS - max_exact)).astype(jnp.int32)
    large = jnp.minimum(large, REL_BUCKETS - 1)
    return jnp.where(n < max_exact, n, large)


def _bias_onehot():
    qi = jnp.arange(WINDOW)[:, None]
    kj = jnp.arange(2 * WINDOW)[None, :]
    dist = qi + WINDOW - kj
    valid = (dist >= 0) & (dist < WINDOW)
    bucket = _t5_bucket(dist)
    onehot = (bucket[None] == jnp.arange(REL_BUCKETS)[:, None, None]) & valid[None]
    return (onehot.reshape(REL_BUCKETS, -1).astype(F32),
            jnp.where(valid, 0.0, NEG).astype(F32).reshape(1, -1))


def _rstd(x):
    return lax.rsqrt(jnp.mean(x * x, axis=-1, keepdims=True) + EPS)


def _norm_bwd(dh, x, g):
    r = _rstd(x)
    xh = x * r
    w = dh * g
    dx = r * (w - xh * jnp.mean(w * xh, axis=-1, keepdims=True))
    return dx, jnp.sum(dh * xh, axis=0, keepdims=True)


def _tile_lanes(t, n):
    return jnp.tile(t, (1, n // t.shape[1])) if n != t.shape[1] else t


def _rope_fwd(a, c, s1, s2):
    n = a.shape[1]
    return (a * _tile_lanes(c, n) + pltpu.roll(a, 16, 1) * _tile_lanes(s1, n)
            + pltpu.roll(a, n - 16, 1) * _tile_lanes(s2, n))


def _rope_bwd(d, c, s1, s2):
    n = d.shape[1]
    return (d * _tile_lanes(c, n) + pltpu.roll(d * _tile_lanes(s1, n), n - 16, 1)
            + pltpu.roll(d * _tile_lanes(s2, n), 16, 1))


def _sigmoid(x):
    return 1.0 / (1.0 + jnp.exp(-x))


def _rmsnorm(name, x, g, dtype):
    def fn(xv, gv):
        return ((xv * _rstd(xv)) * gv,)
    return _rowwise(name, fn, [_rows(x), _full(g)], [(x.shape, dtype, "rows")], rows=x.shape[0])[0]


def _residual_norm_bwd(name, dres, dh, x, g):
    def fn(dr, dhv, xv, gv):
        dx, dg = _norm_bwd(dhv, xv, gv)
        return dr + dx, dg
    return _rowwise(name, fn, [_rows(dres), _rows(dh), _rows(x), _full(g)],
                    [(x.shape, F32, "rows"), (g.shape, F32, "acc")], rows=x.shape[0])


def _layer_fwd(l, x, mem, w, p, tabs, swa_bias, S):
    c, ck, s1, s2 = tabs
    n = f"l{l}_"
    h = _rmsnorm(n + "attn_norm", x, p["attn_norm"], BF16)
    gates = _mm(n + "proj_gates", h, w["wag"], [BF16], tn=1024)
    proj_a = _mm(n + "proj_tail", h, w["wat"], [F32])
    proj_b = _mm(n + "proj_b", h, w["wb"], [BF16])

    def prep(cq, kva, qn, kvn, ckv, s1v, s2v):
        cqn = cq * _rstd(cq) * qn
        ckv_ = kva[:, :128]
        ckvn = ckv_ * _rstd(ckv_) * kvn
        pe = pltpu.roll(kva[:, 128:], 64, 1)
        return cqn, ckvn, _rope_fwd(pe, ckv, s1v, s2v)

    cqn, ckvn, kpe = _rowwise(
        n + "mla_prep", prep,
        [_rows(proj_a, 256, 0), _rows(proj_a, 256, 1), _full(p["mla_q_norm"]), _full(p["mla_kv_norm"]),
         _rows(ck), _rows(s1), _rows(s2)],
        [((S, 256), BF16, "rows"), ((S, 128), BF16, "rows"), ((S, 128), F32, "rows")], rows=S)

    q_mla = _mm(n + "q_mla", cqn, w["wuq"], [BF16],
                epi=lambda acc, cv, s1v, s2v: (_rope_fwd(acc, cv, s1v, s2v) * (MLA_SCALE * LOG2E),),
                extras=[(c, "m"), (s1, "m"), (s2, "m")])
    k_mla = _mm(n + "k_mla", ckvn, w["wuk"], [BF16],
                epi=lambda acc, kp: (acc + _tile_lanes(kp, acc.shape[1]),), extras=[(kpe, "m")])
    den = (jnp.arange(MLA_HEADS * HEAD_PAD) % HEAD_PAD == DEN_LANE).astype(F32)[None]
    v_mla = _mm(n + "v_mla", ckvn, w["wuv"], [BF16], epi=lambda acc, dv: (acc + dv,), extras=[(den, "n")])
    o_mla, lse_mla = _causal_fwd(n + "mla_fwd", q_mla, k_mla, v_mla, heads=MLA_HEADS, tile=MLA_TILE,
                                 chunk=MLA_CHUNK_FWD)
    o_swa, lse_swa = _swa_fwd(n + "swa_fwd", proj_b, swa_bias, p["sinks"], tq=SWA_TQ)
    mn = _rmsnorm(n + "mem_norm", mem, p["mem_norm"], BF16)
    kvm = _mm(n + "kv_mem", mn, w["wmem"], [BF16])
    o_mem, lse_mem = _flash_fwd(n + "mem_fwd", proj_b, kvm, kvm, heads=MEM_HEADS, q_off=8, k_off=0, v_off=4,
                                group=1, mode="full", scale=MEM_HD ** -0.5, tq=MEM_TQ, tk=MEM_LEN)
    t0 = _mm(n + "t_mla", o_mla, w["wo_mla"], [BF16], tn=1024)
    t1 = _mm(n + "t_swa", o_swa, w["wo_swa"], [BF16], tn=1024)
    t2 = _mm(n + "t_mem", o_mem, w["wo_mem"], [BF16], tn=1024)

    def merge(g0, g1, g2, bg, a0, a1, a2):
        y = (_sigmoid(g0 + bg[:, 0:1024]) * a0 + _sigmoid(g1 + bg[:, 1024:2048]) * a1
             + _sigmoid(g2 + bg[:, 2048:3072]) * a2)
        return (y,)

    y = _rowwise(n + "merge", merge,
                 [_rows(gates, 1024, 0), _rows(gates, 1024, 1), _rows(gates, 1024, 2), _full(p["b_gate"]),
                  _rows(t0), _rows(t1), _rows(t2)], [((S, D_MODEL), BF16, "rows")], rows=S)[0]
    x1 = _mm(n + "out_proj", y, w["wout"], [F32], epi=lambda acc, r: (acc + r,), extras=[(x, "mn")])
    h2 = _rmsnorm(n + "mlp_norm", x1, p["mlp_norm"], BF16)
    act = _mm(n + "mlp_up", h2, w["wup"], [BF16], epi=lambda acc: (jnp.square(jnp.maximum(acc, 0.0)),), tn=1024)
    x2 = _mm(n + "mlp_down", act, w["wdown"], [F32], epi=lambda acc, r: (acc + r,), extras=[(x1, "mn")])
    saved = dict(x=x, h=h, gates=gates, proj_a=proj_a, proj_b=proj_b, cqn=cqn, ckvn=ckvn, q_mla=q_mla, k_mla=k_mla, v_mla=v_mla,
                 o_mla=o_mla, lse_mla=lse_mla, o_swa=o_swa, lse_swa=lse_swa, mn=mn, kvm=kvm, o_mem=o_mem,
                 lse_mem=lse_mem, t0=t0, t1=t1, t2=t2, y=y, x1=x1, h2=h2, act=act)
    return x2, saved


def _layer_bwd(l, dx2, mem, w, p, tabs, swa_bias, sv, S):
    c, ck, s1, s2 = tabs
    n = f"l{l}_b_"
    gw = {}
    gs = {}
    du = _mm(n + "d_act", dx2, w["wdown_t"], [BF16],
             epi=lambda acc, av: (acc * (2.0 * jnp.sqrt(av.astype(F32))),), extras=[(sv["act"], "mn")], tn=1024)
    gw["wdown"] = _mm_tn(n + "g_wdown", sv["act"], dx2)
    gw["wup"] = _mm_tn(n + "g_wup", sv["h2"], du)
    dh2 = _mm(n + "d_h2", du, w["wup_t"], [F32])
    dx1, gs["mlp_norm"] = _residual_norm_bwd(n + "mlp_norm", dx2, dh2, sv["x1"], p["mlp_norm"])
    gw["wout"] = _mm_tn(n + "g_wout", sv["y"], dx1)
    dy = _mm(n + "d_y", dx1, w["wout_t"], [F32])

    def merge_bwd(dyv, g0, g1, g2, bg, a0, a1, a2):
        outs, dgs = [], []
        for b, (gv, av) in enumerate(((g0, a0), (g1, a1), (g2, a2))):
            sg = _sigmoid(gv + bg[:, b * 1024:(b + 1) * 1024])
            outs.append(dyv * sg)
            dgs.append(dyv * av * sg * (1.0 - sg))
        dg = jnp.concatenate(dgs, axis=1)
        return outs[0], outs[1], outs[2], dg, jnp.sum(dg, axis=0, keepdims=True)

    pa = sv["proj_a"]
    gt = sv["gates"]
    dt0, dt1, dt2, dgates, gs["b_gate"] = _rowwise(
        n + "merge", merge_bwd,
        [_rows(dy), _rows(gt, 1024, 0), _rows(gt, 1024, 1), _rows(gt, 1024, 2), _full(p["b_gate"]),
         _rows(sv["t0"]), _rows(sv["t1"]), _rows(sv["t2"])],
        [((S, D_MODEL), BF16, "rows")] * 3 + [((S, 3 * D_MODEL), BF16, "rows"), ((1, 3 * D_MODEL), F32, "acc")],
        rows=S)
    gw["wo_mla"] = _mm_tn(n + "g_wo_mla", sv["o_mla"], dt0)
    gw["wo_swa"] = _mm_tn(n + "g_wo_swa", sv["o_swa"], dt1)
    gw["wo_mem"] = _mm_tn(n + "g_wo_mem", sv["o_mem"], dt2)
    do_mla = _mm(n + "d_o_mla", dt0, w["wo_mla_t"], [BF16])
    do_swa = _mm(n + "d_o_swa", dt1, w["wo_swa_t"], [BF16])
    do_mem = _mm(n + "d_o_mem", dt2, w["wo_mem_t"], [BF16])
    pb = sv["proj_b"]
    delta_mla = _row_dot(n + "mla_delta", sv["o_mla"], do_mla, heads=MLA_HEADS, tm=MLA_TILE)
    dq_mla, dk_mla, dv_mla = _causal_bwd(
        n + "mla_bwd", sv["q_mla"], sv["k_mla"], sv["v_mla"], do_mla, sv["lse_mla"], delta_mla, heads=MLA_HEADS,
        tile=MLA_TILE, chunk=MLA_CHUNK)
    dq_swa, dk_swa, dv_swa, dk_edge, dv_edge, dbias, dsink = _swa_bwd(
        n + "swa_bwd", pb, swa_bias, p["sinks"], sv["o_swa"], do_swa, sv["lse_swa"], tq=SWA_TQ)
    dq_mem, dk_mem, dv_mem = _flash_bwd(
        n + "mem_bwd", pb, sv["kvm"], sv["kvm"], sv["o_mem"], do_mem, sv["lse_mem"], heads=MEM_HEADS,
        q_off=8, k_off=0, v_off=4, group=1, mode="full", scale=MEM_HD ** -0.5, tq=MEM_TQ, tk=MEM_LEN)
    gs["dbias"] = dbias
    gs["sinks"] = dsink[:, 0, 0]
    dkvm = jnp.concatenate([dk_mem, dv_mem], axis=1)
    gw["wmem"] = _mm_tn(n + "g_wmem", sv["mn"], dkvm)
    dmn = _mm(n + "d_mn", dkvm, w["wmem_t"], [F32])
    _, gs["mem_norm"] = _residual_norm_bwd(n + "mem_norm", dmn, dmn, mem, p["mem_norm"])
    dq_pre = _rowwise(n + "q_unrope", lambda d, cv, s1v, s2v: (_rope_bwd(d * MLA_SCALE, cv, s1v, s2v),),
                      [_rows(dq_mla), _rows(c), _rows(s1), _rows(s2)], [((S, 1024), BF16, "rows")], rows=S)[0]
    gw["wuq"] = _mm_tn(n + "g_wuq", sv["cqn"], dq_pre)
    gw["wuk"] = _mm_tn(n + "g_wuk", sv["ckvn"], dk_mla)
    gw["wuv"] = _mm_tn(n + "g_wuv", sv["ckvn"], dv_mla)
    dcqn = _mm(n + "d_cqn", dq_pre, w["wuq_t"], [F32])
    dckvn = _mm(n + "d_ckvn_k", dk_mla, w["wuk_t"], [F32])
    dckvn = _mm(n + "d_ckvn_v", dv_mla, w["wuv_t"], [F32], epi=lambda acc, r: (acc + r,), extras=[(dckvn, "mn")])

    def mla_norm_bwd(dcq_n, dckv_n, dk, cq, kva, qn, kvn, ckv, s1v, s2v):
        dcq, dqn = _norm_bwd(dcq_n, cq, qn)
        dckv, dkvn = _norm_bwd(dckv_n, kva[:, :128], kvn)
        dkpe = dk[:, 0:128]
        for hh in range(1, MLA_HEADS):
            dkpe = dkpe + dk[:, hh * 128:(hh + 1) * 128]
        dpe = pltpu.roll(_rope_bwd(dkpe, ckv, s1v, s2v), 64, 1)
        return jnp.concatenate([dcq, dckv, dpe], axis=1), dqn, dkvn

    dtail, gs["mla_q_norm"], gs["mla_kv_norm"] = _rowwise(
        n + "mla_norm", mla_norm_bwd,
        [_rows(dcqn), _rows(dckvn), _rows(dk_mla), _rows(pa, 256, 0), _rows(pa, 256, 1),
         _full(p["mla_q_norm"]), _full(p["mla_kv_norm"]), _rows(ck), _rows(s1), _rows(s2)],
        [((S, 512), BF16, "rows"), ((1, 256), F32, "acc"), ((1, 128), F32, "acc")], rows=S)

    dproj_b = _dproj_b(n + "dproj_b", dq_swa, dq_mem, dk_swa, dv_swa, dk_edge, dv_edge, tq=SWA_TQ)
    h = sv["h"]
    gw["wag"] = _mm_tn(n + "g_wa_gates", h, dgates)
    gw["wat"] = _mm_tn(n + "g_wa_tail", h, dtail)
    gw["wb"] = _mm_tn(n + "g_wb", h, dproj_b)
    dh = _mm(n + "d_h_gates", dgates, w["wag_t"], [F32])
    dh = _mm(n + "d_h_tail", dtail, w["wat_t"], [F32], epi=lambda acc, r: (acc + r,), extras=[(dh, "mn")])
    dh = _mm(n + "d_h_b", dproj_b, w["wb_t"], [F32], epi=lambda acc, r: (acc + r,), extras=[(dh, "mn")])
    dx, gs["attn_norm"] = _residual_norm_bwd(n + "attn_norm", dx1, dh, sv["x"], p["attn_norm"])
    return dx, gw, gs


def _local_step(x, mem, loss_target, full, small):
    S = x.shape[0]
    tabs = _rope_tables(S)
    onehot, band = _bias_onehot()
    hi = lax.Precision.HIGHEST
    swa_bias = _mm("swa_bias", small["rel_bias"].T, onehot, [F32], epi=lambda acc, mk: (acc + mk,),
                   extras=[(band, "n")], cast=None, precision=hi, tn=8192).reshape(SWA_HEADS, WINDOW, 2 * WINDOW)
    ws, ps = [], []
    for l in range(DEPTH):
        ws.append(_layer_weights(full, l))
        ps.append(dict(
            attn_norm=small["attn_norm"][l][None], mem_norm=small["mem_norm"][l][None],
            b_gate=small["b_gate"][l][None], mla_q_norm=small["mla_q_norm"][l][None],
            mla_kv_norm=small["mla_kv_norm"][l][None], mlp_norm=small["mlp_norm"][l][None],
            sinks=jnp.broadcast_to(small["attn_sinks"][l][:, None, None], (SWA_HEADS, 8, 128))))
    saved = []
    xc = x
    for l in range(DEPTH):
        xc, sv = _layer_fwd(l, xc, mem, ws[l], ps[l], tabs, swa_bias, S)
        saved.append(sv)

    fn_g = small["final_norm"][None]

    def loss_fn(xv, gv, tv):
        r = _rstd(xv)
        xh = xv * r
        err = xh * gv - tv
        dyv = err * (1.0 / D_MODEL)
        wv = dyv * gv
        dx = r * (wv - xh * jnp.mean(wv * xh, axis=-1, keepdims=True))
        part = 0.5 * jnp.sum(err * err) * (1.0 / D_MODEL)
        return dx, jnp.sum(dyv * xh, axis=0, keepdims=True), jnp.zeros((8, 128), F32) + part

    dx, g_final, loss_acc = _rowwise(
        "loss", loss_fn, [_rows(xc), _full(fn_g), _rows(loss_target)],
        [((S, D_MODEL), F32, "rows"), ((1, D_MODEL), F32, "acc"), ((8, 128), F32, "acc")], rows=S)

    gws, gss = [None] * DEPTH, [None] * DEPTH
    for l in reversed(range(DEPTH)):
        dx, gw, gs = _layer_bwd(l, dx, mem, ws[l], ps[l], tabs, swa_bias, saved[l], S)
        gws[l] = _layer_weight_grads(gw)
        gss[l] = gs

    dbias = (gss[0]["dbias"] + gss[1]["dbias"]).reshape(SWA_HEADS, -1)
    g_rel = _mm("g_rel_bias", dbias, onehot.T, [F32], cast=None, precision=hi, tk=8192).T
    wgrads = {k: jnp.stack([gws[l][k] for l in range(DEPTH)]) for k in gws[0]}
    sgrads = dict(
        rel_bias=g_rel,
        final_norm=g_final[0],
        attn_sinks=jnp.stack([gss[l]["sinks"] for l in range(DEPTH)]),
        **{k: jnp.concatenate([gss[l][k] for l in range(DEPTH)], axis=0)
           for k in ("attn_norm", "mem_norm", "b_gate", "mla_q_norm", "mla_kv_norm", "mlp_norm")})
    return loss_acc[0, 0], dx, wgrads, sgrads


def _pack_small(vals, loss):
    rows = []
    for name, shape in SMALL:
        flat = vals[name].astype(F32).reshape(-1)
        pad = (-flat.shape[0]) % 1024
        rows.append(jnp.pad(flat, (0, pad)).reshape(-1, 128))
    rows.append(jnp.zeros((8, 128), F32) + loss)
    return jnp.concatenate(rows, axis=0)


def _unpack_small(packed):
    out, r = {}, 0
    for name, shape in SMALL:
        size = math.prod(shape)
        nrows = 8 * -(-size // 1024)
        out[name] = packed[r:r + nrows].reshape(-1)[:size].reshape(shape)
        r += nrows
    return out, packed[r, 0]


def kernel(x, mem, rel_bias, attn_norm, mem_norm, w_in, b_gate, mla_q_norm, w_uq, mla_kv_norm, w_ukv, attn_sinks, w_mem_kv, w_o_mla, w_o_swa, w_o_mem, w_out, mlp_norm, w_up, w_down, final_norm, loss_target, m_rel_bias, m_attn_norm, m_mem_norm, m_w_in, m_b_gate, m_mla_q_norm, m_w_uq, m_mla_kv_norm, m_w_ukv, m_attn_sinks, m_w_mem_kv, m_w_o_mla, m_w_o_swa, m_w_o_mem, m_w_out, m_mlp_norm, m_w_up, m_w_down, m_final_norm, v_rel_bias, v_attn_norm, v_mem_norm, v_w_in, v_b_gate, v_mla_q_norm, v_w_uq, v_mla_kv_norm, v_w_ukv, v_attn_sinks, v_w_mem_kv, v_w_o_mla, v_w_o_swa, v_w_o_mem, v_w_out, v_mlp_norm, v_w_up, v_w_down, v_final_norm):
    wv = dict(rel_bias=rel_bias, attn_norm=attn_norm, mem_norm=mem_norm, w_in=w_in, b_gate=b_gate,
              mla_q_norm=mla_q_norm, w_uq=w_uq, mla_kv_norm=mla_kv_norm, w_ukv=w_ukv, attn_sinks=attn_sinks,
              w_mem_kv=w_mem_kv, w_o_mla=w_o_mla, w_o_swa=w_o_swa, w_o_mem=w_o_mem, w_out=w_out,
              mlp_norm=mlp_norm, w_up=w_up, w_down=w_down, final_norm=final_norm)
    mv = dict(rel_bias=m_rel_bias, attn_norm=m_attn_norm, mem_norm=m_mem_norm, w_in=m_w_in, b_gate=m_b_gate,
              mla_q_norm=m_mla_q_norm, w_uq=m_w_uq, mla_kv_norm=m_mla_kv_norm, w_ukv=m_w_ukv,
              attn_sinks=m_attn_sinks, w_mem_kv=m_w_mem_kv, w_o_mla=m_w_o_mla, w_o_swa=m_w_o_swa,
              w_o_mem=m_w_o_mem, w_out=m_w_out, mlp_norm=m_mlp_norm, w_up=m_w_up, w_down=m_w_down,
              final_norm=m_final_norm)
    vv = dict(rel_bias=v_rel_bias, attn_norm=v_attn_norm, mem_norm=v_mem_norm, w_in=v_w_in, b_gate=v_b_gate,
              mla_q_norm=v_mla_q_norm, w_uq=v_w_uq, mla_kv_norm=v_mla_kv_norm, w_ukv=v_w_ukv,
              attn_sinks=v_attn_sinks, w_mem_kv=v_w_mem_kv, w_o_mla=v_w_o_mla, w_o_swa=v_w_o_swa,
              w_o_mem=v_w_o_mem, w_out=v_w_out, mlp_norm=v_mlp_norm, w_up=v_w_up, w_down=v_w_down,
              final_norm=v_final_norm)

    shard_rows = [math.prod(_shard_shape(shape, axis)) // 128 for _, shape, axis in WSPECS]
    gathered = _exchange("gather_weights", _pack_rows([wv[name].astype(BF16) for name, _, _ in WSPECS]),
                         per_peer=False)
    full, r = {}, 0
    for (name, shape, axis), nr in zip(WSPECS, shard_rows):
        full[name] = _unstack(gathered[:, r:r + nr].reshape((N_DEV,) + _shard_shape(shape, axis)), shape, axis)
        r += nr

    loss_part, grad_x, wgrads, sgrads = _local_step(x[0], mem[0], loss_target[0], full,
                                                    {name: wv[name] for name, _ in SMALL})

    send = jnp.concatenate([_restack(wgrads[name], axis).astype(BF16).reshape(N_DEV, -1, 128)
                            for name, _, axis in WSPECS], axis=1)
    recv = _exchange("scatter_grads", send, per_peer=True)
    outs = _adam("adam_sharded", recv, *[_pack_rows([d[name] for name, _, _ in WSPECS]) for d in (wv, mv, vv)])
    res = {}
    r = 0
    for (name, shape, axis), nr in zip(WSPECS, shard_rows):
        res[name] = [o[r:r + nr].reshape(_shard_shape(shape, axis)) for o in outs]
        r += nr

    small_recv = _exchange("gather_small", _pack_small(sgrads, loss_part), per_peer=False)
    zero = jnp.zeros((), F32)
    souts = _adam("adam_small", small_recv, *[_pack_small(d, zero) for d in (wv, mv, vv)])
    loss = None
    for i, o in enumerate(souts):
        vals, extra = _unpack_small(o)
        if i == 0:
            loss = extra
        for name, _ in SMALL:
            res.setdefault(name, []).append(vals[name])

    out = [loss, grad_x[None]]
    for i in range(4):
        out.extend(res[name][i] for name in WEIGHT_ORDER)
    return tuple(out)
```

```python
import math

import jax
import jax.numpy as jnp
from jax import lax
from jax.experimental import pallas as pl
from jax.experimental.pallas import tpu as pltpu

F32 = jnp.float32
BF16 = jnp.bfloat16

N_DEV = 8
D_MODEL = 1024
DEPTH = 2
MLA_HEADS = 8
MLA_Q_LORA = 256
MLA_KV_LORA = 128
MLA_NOPE = 64
MLA_ROPE = 32
MLA_V = 64
ROPE_THETA = 10000.0
SWA_HEADS = 8
SWA_KV_HEADS = 2
SWA_HD = 64
WINDOW = 128
REL_BUCKETS = 32
REL_MAX_DIST = 128
MEM_LEN = 256
MEM_HEADS = 4
MEM_HD = 128
D_FF = 4 * D_MODEL
EPS = 1e-6
HEAD_PAD = 128
ADAM_LR = 0.001
ADAM_B1 = 0.9
ADAM_B2 = 0.999
ADAM_EPS = 1e-08
ADAM_WD = 0.01
ADAM_STEP = 10

NEG = -1e30
VMEM_LIMIT = 48 * 1024 * 1024

MM_TM = 1024
MM_TN = 1024
MM_TK = 1024
TN_T1 = 1024
TN_TN = 1024
TN_TS = 1024
ROW_TM = 256
MLA_TILE = 1024
MLA_CHUNK = 256
MLA_CHUNK_FWD = 512
MLA_SCALE = (MLA_NOPE + MLA_ROPE) ** -0.5
LOG2E = math.log2(math.e)
DEN_LANE = MLA_V
SWA_TQ = 512
MEM_TQ = 1024
ADAM_TM = 1200

WSPECS = (
    ("w_in", (DEPTH, D_MODEL, 4768), 2),
    ("w_uq", (DEPTH, MLA_Q_LORA, 768), 2),
    ("w_ukv", (DEPTH, MLA_KV_LORA, 1024), 2),
    ("w_mem_kv", (DEPTH, D_MODEL, 1024), 1),
    ("w_o_mla", (DEPTH, 512, D_MODEL), 2),
    ("w_o_swa", (DEPTH, 512, D_MODEL), 2),
    ("w_o_mem", (DEPTH, 512, D_MODEL), 2),
    ("w_out", (DEPTH, D_MODEL, D_MODEL), 1),
    ("w_up", (DEPTH, D_MODEL, D_FF), 2),
    ("w_down", (DEPTH, D_FF, D_MODEL), 1),
)
SMALL = (
    ("rel_bias", (REL_BUCKETS, SWA_HEADS)),
    ("attn_norm", (DEPTH, D_MODEL)),
    ("mem_norm", (DEPTH, D_MODEL)),
    ("b_gate", (DEPTH, 3 * D_MODEL)),
    ("mla_q_norm", (DEPTH, MLA_Q_LORA)),
    ("mla_kv_norm", (DEPTH, MLA_KV_LORA)),
    ("attn_sinks", (DEPTH, SWA_HEADS)),
    ("mlp_norm", (DEPTH, D_MODEL)),
    ("final_norm", (D_MODEL,)),
)
WEIGHT_ORDER = ("rel_bias", "attn_norm", "mem_norm", "w_in", "b_gate", "mla_q_norm", "w_uq", "mla_kv_norm",
                "w_ukv", "attn_sinks", "w_mem_kv", "w_o_mla", "w_o_swa", "w_o_mem", "w_out", "mlp_norm",
                "w_up", "w_down", "final_norm")


def _cparams(*sem):
    return pltpu.CompilerParams(dimension_semantics=sem, vmem_limit_bytes=VMEM_LIMIT)


def _shard_shape(shape, axis):
    s = list(shape)
    s[axis] //= N_DEV
    return tuple(s)


def _mm(name, a, b, out_dtypes, *, epi=None, extras=(), a_fn=None, cast=BF16, precision=None,
        tm=None, tn=None, tk=None):
    M, K = a.shape
    K2, N = b.shape
    assert K == K2, (name, a.shape, b.shape)
    tm = min(tm or MM_TM, M)
    tn = min(tn or MM_TN, N)
    tk = min(tk or MM_TK, K)
    assert M % tm == 0 and N % tn == 0 and K % tk == 0, (name, a.shape, b.shape, tm, tn, tk)
    nk = K // tk
    n_ex = len(extras)
    n_out = len(out_dtypes)

    def body(*refs):
        a_ref, b_ref = refs[0], refs[1]
        ex_refs = refs[2:2 + n_ex]
        out_refs = refs[2 + n_ex:2 + n_ex + n_out]
        av = a_ref[...]
        if a_fn is not None:
            av = a_fn(av)
        bv = b_ref[...]
        if cast is not None:
            av = av.astype(cast)
            bv = bv.astype(cast)
        part = jnp.dot(av, bv, preferred_element_type=F32, precision=precision)

        def finish(acc):
            outs = epi(acc, *[r[...] for r in ex_refs]) if epi is not None else (acc,)
            for r, o in zip(out_refs, outs):
                r[...] = o.astype(r.dtype)

        if nk == 1:
            finish(part)
        else:
            acc_ref = refs[-1]
            k = pl.program_id(2)

            @pl.when(k == 0)
            def _():
                acc_ref[...] = part

            @pl.when(k > 0)
            def _():
                acc_ref[...] += part

            @pl.when(k == nk - 1)
            def _():
                finish(acc_ref[...])

    in_specs = [pl.BlockSpec((tm, tk), lambda i, j, k: (i, k)),
                pl.BlockSpec((tk, tn), lambda i, j, k: (k, j))]
    for arr, kind in extras:
        if kind == "mn":
            in_specs.append(pl.BlockSpec((tm, tn), lambda i, j, k: (i, j)))
        elif kind == "m":
            in_specs.append(pl.BlockSpec((tm, arr.shape[1]), lambda i, j, k: (i, 0)))
        else:
            in_specs.append(pl.BlockSpec((1, tn), lambda i, j, k: (0, j)))
    outs = pl.pallas_call(
        body, name=name, grid=(M // tm, N // tn, nk),
        in_specs=in_specs,
        out_specs=[pl.BlockSpec((tm, tn), lambda i, j, k: (i, j)) for _ in out_dtypes],
        out_shape=[jax.ShapeDtypeStruct((M, N), dt) for dt in out_dtypes],
        scratch_shapes=[pltpu.VMEM((tm, tn), F32)] if nk > 1 else [],
        compiler_params=_cparams("parallel", "parallel", "arbitrary"),
    )(a, b, *[arr for arr, _ in extras])
    return outs[0] if n_out == 1 else outs


def _mm_tn(name, a, b, *, t1=None, tn=None, ts=None):
    S, K1 = a.shape
    S2, N = b.shape
    assert S == S2, (name, a.shape, b.shape)
    t1 = min(t1 or TN_T1, K1)
    tn = min(tn or TN_TN, N)
    ts = min(ts or TN_TS, S)
    assert K1 % t1 == 0 and N % tn == 0 and S % ts == 0, (name, a.shape, b.shape)

    def body(a_ref, b_ref, o_ref):
        s = pl.program_id(2)
        part = lax.dot_general(a_ref[...].astype(BF16), b_ref[...].astype(BF16),
                               (((0,), (0,)), ((), ())), preferred_element_type=F32)

        @pl.when(s == 0)
        def _():
            o_ref[...] = part

        @pl.when(s > 0)
        def _():
            o_ref[...] += part

    return pl.pallas_call(
        body, name=name, grid=(K1 // t1, N // tn, S // ts),
        in_specs=[pl.BlockSpec((ts, t1), lambda i, j, s: (s, i)),
                  pl.BlockSpec((ts, tn), lambda i, j, s: (s, j))],
        out_specs=pl.BlockSpec((t1, tn), lambda i, j, s: (i, j)),
        out_shape=jax.ShapeDtypeStruct((K1, N), F32),
        compiler_params=_cparams("parallel", "parallel", "arbitrary"),
    )(a, b)


def _rows(arr, width=None, blk=0):
    return (arr, ("rows", arr.shape[1] if width is None else width, blk))


def _full(arr):
    return (arr, ("full",))


def _rowwise(name, fn, ins, outs, *, rows, tm=None):
    tm = min(tm or ROW_TM, rows)
    assert rows % tm == 0, (name, rows, tm)
    n_in = len(ins)

    def body(*refs):
        i = pl.program_id(0)
        vals = fn(*[r[...] for r in refs[:n_in]])
        for (shape, dt, kind), r, v in zip(outs, refs[n_in:], vals):
            if kind == "rows":
                r[...] = v.astype(dt)
            else:
                @pl.when(i == 0)
                def _(r=r, v=v):
                    r[...] = v

                @pl.when(i > 0)
                def _(r=r, v=v):
                    r[...] += v

    in_specs = []
    for arr, spec in ins:
        if spec[0] == "rows":
            in_specs.append(pl.BlockSpec((tm, spec[1]), lambda i, b=spec[2]: (i, b)))
        else:
            in_specs.append(pl.BlockSpec(arr.shape, lambda i, n=arr.ndim: (0,) * n))
    out_specs = []
    for shape, dt, kind in outs:
        if kind == "rows":
            out_specs.append(pl.BlockSpec((tm, shape[1]), lambda i: (i, 0)))
        else:
            out_specs.append(pl.BlockSpec(shape, lambda i, n=len(shape): (0,) * n))
    res = pl.pallas_call(
        body, name=name, grid=(rows // tm,),
        in_specs=in_specs, out_specs=out_specs,
        out_shape=[jax.ShapeDtypeStruct(shape, dt) for shape, dt, _ in outs],
        compiler_params=_cparams("arbitrary"),
    )(*[arr for arr, _ in ins])
    return res


def _flash_fwd(name, q_arr, k_arr, v_arr, *, heads, q_off, k_off, v_off, group, mode, scale, tq, tk,
               bias=None, sinks=None):
    S = q_arr.shape[0]
    Sk = k_arr.shape[0]
    tq = min(tq, S)
    tk = min(tk, Sk)
    nq = S // tq
    if mode == "window":
        assert tq == tk
        nsteps = 2
    else:
        nsteps = Sk // tk

    def kblock(qi, kk):
        if mode == "causal":
            return jnp.minimum(kk, (qi * tq + tq - 1) // tk)
        if mode == "window":
            return jnp.maximum(qi + kk - 1, 0)
        return kk

    has_bias = bias is not None
    has_sink = sinks is not None

    def body(*refs):
        q_ref, k_ref, v_ref = refs[:3]
        pos = 3
        bias_ref = sink_ref = None
        if has_bias:
            bias_ref = refs[pos]
            pos += 1
        if has_sink:
            sink_ref = refs[pos]
            pos += 1
        o_ref, lse_ref, m_sc, l_sc, acc_sc = refs[pos:pos + 5]
        qi = pl.program_id(1)
        kk = pl.program_id(2)

        @pl.when(kk == 0)
        def _():
            if has_sink:
                m_sc[...] = jnp.zeros(m_sc.shape, F32) + sink_ref[0, 0:1, 0:1]
                l_sc[...] = jnp.ones(l_sc.shape, F32)
            else:
                m_sc[...] = jnp.full(m_sc.shape, NEG, F32)
                l_sc[...] = jnp.zeros(l_sc.shape, F32)
            acc_sc[...] = jnp.zeros(acc_sc.shape, F32)

        if mode == "causal":
            run = kk <= (qi * tq + tq - 1) // tk
        elif mode == "window":
            run = qi + kk >= 1
        else:
            run = None

        def step():
            s = lax.dot_general(q_ref[...], k_ref[...], (((1,), (1,)), ((), ())),
                                preferred_element_type=F32) * scale
            if has_bias:
                s = s + bias_ref[0, 0]
            if mode == "causal":
                rows = qi * tq + lax.broadcasted_iota(jnp.int32, (tq, tk), 0)
                cols = kk * tk + lax.broadcasted_iota(jnp.int32, (tq, tk), 1)
                s = jnp.where(cols <= rows, s, NEG)
            m_prev = m_sc[...]
            m_new = jnp.maximum(m_prev, jnp.max(s, axis=1, keepdims=True))
            alpha = jnp.exp(m_prev - m_new)
            p = jnp.exp(s - m_new)
            l_sc[...] = alpha * l_sc[...] + jnp.sum(p, axis=1, keepdims=True)
            acc_sc[...] = alpha * acc_sc[...] + jnp.dot(p.astype(BF16), v_ref[...],
                                                        preferred_element_type=F32)
            m_sc[...] = m_new

        if run is None:
            step()
        else:
            pl.when(run)(step)

        @pl.when(kk == nsteps - 1)
        def _():
            l = l_sc[...]
            o_ref[...] = (acc_sc[...] / l).astype(o_ref.dtype)
            lse_ref[0] = m_sc[...] + jnp.log(l)

    in_specs = [
        pl.BlockSpec((tq, HEAD_PAD), lambda h, qi, kk: (qi, q_off + h)),
        pl.BlockSpec((tk, HEAD_PAD), lambda h, qi, kk: (kblock(qi, kk), k_off + h // group)),
        pl.BlockSpec((tk, HEAD_PAD), lambda h, qi, kk: (kblock(qi, kk), v_off + h // group)),
    ]
    args = [q_arr, k_arr, v_arr]
    if has_bias:
        in_specs.append(pl.BlockSpec((1, 1, tq, tk), lambda h, qi, kk: (h, kk, 0, 0)))
        args.append(bias)
    if has_sink:
        in_specs.append(pl.BlockSpec((1, 8, 128), lambda h, qi, kk: (h, 0, 0)))
        args.append(sinks)
    o, lse = pl.pallas_call(
        body, name=name, grid=(heads, nq, nsteps),
        in_specs=in_specs,
        out_specs=[pl.BlockSpec((tq, HEAD_PAD), lambda h, qi, kk: (qi, h)),
                   pl.BlockSpec((1, tq, 1), lambda h, qi, kk: (h, qi, 0))],
        out_shape=[jax.ShapeDtypeStruct((S, heads * HEAD_PAD), BF16),
                   jax.ShapeDtypeStruct((heads, S, 1), F32)],
        scratch_shapes=[pltpu.VMEM((tq, 1), F32), pltpu.VMEM((tq, 1), F32), pltpu.VMEM((tq, HEAD_PAD), F32)],
        compiler_params=_cparams("parallel", "parallel", "arbitrary"),
    )(*args)
    return o, lse


def _flash_bwd(name, q_arr, k_arr, v_arr, o_arr, do_arr, lse, *, heads, q_off, k_off, v_off, group, mode,
               scale, tq, tk, bias=None, sinks=None):
    S = q_arr.shape[0]
    Sk = k_arr.shape[0]
    tq = min(tq, S)
    tk = min(tk, Sk)
    nq = S // tq
    nkb = Sk // tk
    if mode == "window":
        assert tq == tk
        nsteps = 2
    else:
        nsteps = nq

    def qblock(kj, qq):
        if mode == "causal":
            return jnp.maximum(qq, (kj * tk) // tq)
        if mode == "window":
            return jnp.minimum(kj + qq, nq - 1)
        return qq

    has_bias = bias is not None
    has_sink = sinks is not None

    def body(*refs):
        q_ref, k_ref, v_ref, o_ref, do_ref, lse_ref = refs[:6]
        pos = 6
        bias_ref = sink_ref = None
        if has_bias:
            bias_ref = refs[pos]
            pos += 1
        if has_sink:
            sink_ref = refs[pos]
            pos += 1
        dq_ref, dk_ref, dv_ref = refs[pos:pos + 3]
        pos += 3
        dbias_ref = dsink_ref = None
        if has_bias:
            dbias_ref = refs[pos]
            pos += 1
        if has_sink:
            dsink_ref = refs[pos]
            pos += 1
        dk_sc, dv_sc = refs[pos:pos + 2]
        kj = pl.program_id(1)
        qq = pl.program_id(2)
        qb = qblock(kj, qq)

        @pl.when((kj == 0) & (qq == 0))
        def _():
            dq_ref[...] = jnp.zeros(dq_ref.shape, F32)
            if has_bias:
                dbias_ref[...] = jnp.zeros(dbias_ref.shape, F32)
            if has_sink:
                dsink_ref[...] = jnp.zeros(dsink_ref.shape, F32)

        @pl.when(qq == 0)
        def _():
            dk_sc[...] = jnp.zeros(dk_sc.shape, F32)
            dv_sc[...] = jnp.zeros(dv_sc.shape, F32)

        if mode == "causal":
            run = qq >= (kj * tk) // tq
        elif mode == "window":
            run = kj + qq <= nq - 1
        else:
            run = None

        def step():
            q = q_ref[...]
            k = k_ref[...]
            do = do_ref[...]
            lse_v = lse_ref[0]
            s = lax.dot_general(q, k, (((1,), (1,)), ((), ())), preferred_element_type=F32) * scale
            if has_bias:
                s = s + bias_ref[0, 0]
            if mode == "causal":
                rows = qb * tq + lax.broadcasted_iota(jnp.int32, (tq, tk), 0)
                cols = kj * tk + lax.broadcasted_iota(jnp.int32, (tq, tk), 1)
                s = jnp.where(cols <= rows, s, NEG)
            p = jnp.exp(s - lse_v)
            delta = jnp.sum(do.astype(F32) * o_ref[...].astype(F32), axis=1, keepdims=True)
            dv_sc[...] += lax.dot_general(p.astype(BF16), do, (((0,), (0,)), ((), ())),
                                          preferred_element_type=F32)
            dp = lax.dot_general(do, v_ref[...], (((1,), (1,)), ((), ())), preferred_element_type=F32)
            dsp = p * (dp - delta)
            if has_bias:
                @pl.when(qq == 0)
                def _():
                    dbias_ref[0, 1] += dsp

                @pl.when(qq == 1)
                def _():
                    dbias_ref[0, 0] += dsp
            ds = (dsp * scale).astype(BF16)
            row0 = pl.multiple_of(qb * tq, tq)
            dq_ref[pl.ds(row0, tq), :] += jnp.dot(ds, k, preferred_element_type=F32)
            dk_sc[...] += lax.dot_general(ds, q, (((0,), (0,)), ((), ())), preferred_element_type=F32)
            if has_sink:
                @pl.when(qq == 0)
                def _():
                    ps = jnp.exp(sink_ref[0, 0:1, 0:1] - lse_v)
                    dsink_ref[...] += jnp.zeros(dsink_ref.shape, F32) - jnp.sum(ps * delta)

        if run is None:
            step()
        else:
            pl.when(run)(step)

        @pl.when(qq == nsteps - 1)
        def _():
            dk_ref[...] = dk_sc[...]
            dv_ref[...] = dv_sc[...]

    def bias_blk(h, kj, qq):
        return (h, 1 - qq, 0, 0)

    in_specs = [
        pl.BlockSpec((tq, HEAD_PAD), lambda h, kj, qq: (qblock(kj, qq), q_off + h)),
        pl.BlockSpec((tk, HEAD_PAD), lambda h, kj, qq: (kj, k_off + h // group)),
        pl.BlockSpec((tk, HEAD_PAD), lambda h, kj, qq: (kj, v_off + h // group)),
        pl.BlockSpec((tq, HEAD_PAD), lambda h, kj, qq: (qblock(kj, qq), h)),
        pl.BlockSpec((tq, HEAD_PAD), lambda h, kj, qq: (qblock(kj, qq), h)),
        pl.BlockSpec((1, tq, 1), lambda h, kj, qq: (h, qblock(kj, qq), 0)),
    ]
    args = [q_arr, k_arr, v_arr, o_arr, do_arr, lse]
    out_specs = [
        pl.BlockSpec((S, HEAD_PAD), lambda h, kj, qq: (0, h)),
        pl.BlockSpec((tk, HEAD_PAD), lambda h, kj, qq: (kj, h)),
        pl.BlockSpec((tk, HEAD_PAD), lambda h, kj, qq: (kj, h)),
    ]
    out_shape = [jax.ShapeDtypeStruct((S, heads * HEAD_PAD), F32),
                 jax.ShapeDtypeStruct((Sk, heads * HEAD_PAD), F32),
                 jax.ShapeDtypeStruct((Sk, heads * HEAD_PAD), F32)]
    if has_bias:
        in_specs.append(pl.BlockSpec((1, 1, tq, tk), bias_blk))
        args.append(bias)
        out_specs.append(pl.BlockSpec((1, 2, tq, tk), lambda h, kj, qq: (h, 0, 0, 0)))
        out_shape.append(jax.ShapeDtypeStruct((heads, 2, tq, tk), F32))
    if has_sink:
        in_specs.append(pl.BlockSpec((1, 8, 128), lambda h, kj, qq: (h, 0, 0)))
        args.append(sinks)
        out_specs.append(pl.BlockSpec((1, 8, 128), lambda h, kj, qq: (h, 0, 0)))
        out_shape.append(jax.ShapeDtypeStruct((heads, 8, 128), F32))
    return pl.pallas_call(
        body, name=name, grid=(heads, nkb, nsteps),
        in_specs=in_specs, out_specs=out_specs, out_shape=out_shape,
        scratch_shapes=[pltpu.VMEM((tk, HEAD_PAD), F32), pltpu.VMEM((tk, HEAD_PAD), F32)],
        compiler_params=_cparams("arbitrary", "arbitrary", "arbitrary"),
    )(*args)


def _causal_fwd(name, q_arr, k_arr, v_arr, *, heads, tile, chunk):
    S = q_arr.shape[0]
    T = min(tile, S)
    C = min(chunk, T)
    nt = S // T
    nc = T // C

    pairs = [(qi, kk) for qi in range(nt) for kk in range(qi + 1)]
    q_tab = jnp.asarray([p[0] for p in pairs], jnp.int32)
    k_tab = jnp.asarray([p[1] for p in pairs], jnp.int32)

    def body(qt_ref, kt_ref, q_ref, k_ref, v_ref, o_ref, lse_ref, m_sc, acc_sc):
        t = pl.program_id(1)
        qi = qt_ref[t]
        kk = kt_ref[t]

        @pl.when(kk == 0)
        def _():
            m_sc[...] = jnp.full(m_sc.shape, NEG, F32)
            acc_sc[...] = jnp.zeros(acc_sc.shape, F32)

        def logits(c, ncols, masked):
            s = lax.dot_general(q_ref[pl.ds(c * C, C), :], k_ref[0:ncols, :], (((1,), (1,)), ((), ())),
                                preferred_element_type=F32)
            if masked:
                r = c * C + lax.broadcasted_iota(jnp.int32, (C, ncols), 0)
                cidx = lax.broadcasted_iota(jnp.int32, (C, ncols), 1)
                s = jnp.where(cidx <= r, s, NEG)
            return s

        def update(c, ncols, s):
            rows = pl.ds(c * C, C)
            m_prev = m_sc[rows, :]
            m_new = jnp.maximum(m_prev, jnp.max(s, axis=1, keepdims=True))
            p = jnp.exp2(s - m_new).astype(BF16)
            acc_sc[rows, :] = jnp.exp2(m_prev - m_new) * acc_sc[rows, :] + jnp.dot(
                p, v_ref[0:ncols, :], preferred_element_type=F32)
            m_sc[rows, :] = m_new

        def tile_step(ncols_of, masked):
            s = logits(0, ncols_of(0), masked)
            for c in range(nc):
                s_next = logits(c + 1, ncols_of(c + 1), masked) if c + 1 < nc else None
                update(c, ncols_of(c), s)
                s = s_next

        @pl.when(kk < qi)
        def _():
            tile_step(lambda c: T, False)

        @pl.when(kk == qi)
        def _():
            tile_step(lambda c: (c + 1) * C, True)

        @pl.when(kk == qi)
        def _():
            acc = acc_sc[...]
            l = acc[:, DEN_LANE:DEN_LANE + 1]
            o_ref[...] = (acc / l).astype(o_ref.dtype)
            lse_ref[0] = m_sc[...] + jnp.log2(l)

    grid_spec = pltpu.PrefetchScalarGridSpec(
        num_scalar_prefetch=2, grid=(heads, len(pairs)),
        in_specs=[pl.BlockSpec((T, HEAD_PAD), lambda h, t, qt, kt: (qt[t], h)),
                  pl.BlockSpec((T, HEAD_PAD), lambda h, t, qt, kt: (kt[t], h)),
                  pl.BlockSpec((T, HEAD_PAD), lambda h, t, qt, kt: (kt[t], h))],
        out_specs=[pl.BlockSpec((T, HEAD_PAD), lambda h, t, qt, kt: (qt[t], h)),
                   pl.BlockSpec((1, T, 1), lambda h, t, qt, kt: (h, qt[t], 0))],
        scratch_shapes=[pltpu.VMEM((T, 1), F32), pltpu.VMEM((T, HEAD_PAD), F32)])
    return pl.pallas_call(
        body, name=name, grid_spec=grid_spec,
        out_shape=[jax.ShapeDtypeStruct((S, heads * HEAD_PAD), BF16),
                   jax.ShapeDtypeStruct((heads, S, 1), F32)],
        compiler_params=_cparams("parallel", "arbitrary"),
    )(q_tab, k_tab, q_arr, k_arr, v_arr)


def _row_dot(name, a, b, *, heads, tm):
    S = a.shape[0]
    tm = min(tm, S)

    def body(a_ref, b_ref, o_ref):
        o_ref[0] = jnp.sum(a_ref[...].astype(F32) * b_ref[...].astype(F32), axis=1, keepdims=True)

    return pl.pallas_call(
        body, name=name, grid=(heads, S // tm),
        in_specs=[pl.BlockSpec((tm, HEAD_PAD), lambda h, i: (i, h)),
                  pl.BlockSpec((tm, HEAD_PAD), lambda h, i: (i, h))],
        out_specs=pl.BlockSpec((1, tm, 1), lambda h, i: (h, i, 0)),
        out_shape=jax.ShapeDtypeStruct((heads, S, 1), F32),
        compiler_params=_cparams("parallel", "parallel"),
    )(a, b)


def _causal_bwd(name, q_arr, k_arr, v_arr, do_arr, lse, delta, *, heads, tile, chunk):
    S = q_arr.shape[0]
    T = min(tile, S)
    C = min(chunk, T)
    nt = S // T
    nc = T // C

    pairs = [(kj, qq) for kj in range(nt) for qq in range(kj, nt)]
    k_tab = jnp.asarray([p[0] for p in pairs], jnp.int32)
    q_tab = jnp.asarray([p[1] for p in pairs], jnp.int32)

    def body(kt_ref, qt_ref, q_ref, k_ref, v_ref, do_ref, lse_ref, delta_ref, dq_ref, dk_ref, dv_ref, dk_sc, dv_sc):
        t = pl.program_id(1)
        kj = kt_ref[t]
        qq = qt_ref[t]
        qb = qq

        @pl.when(t == 0)
        def _():
            dq_ref[...] = jnp.zeros(dq_ref.shape, F32)

        @pl.when(qq == kj)
        def _():
            dk_sc[...] = jnp.zeros(dk_sc.shape, F32)
            dv_sc[...] = jnp.zeros(dv_sc.shape, F32)

        def logits(c, ncols, masked):
            rows = pl.ds(c * C, C)
            s = lax.dot_general(q_ref[rows, :], k_ref[0:ncols, :], (((1,), (1,)), ((), ())),
                                preferred_element_type=F32)
            if masked:
                r = c * C + lax.broadcasted_iota(jnp.int32, (C, ncols), 0)
                cidx = lax.broadcasted_iota(jnp.int32, (C, ncols), 1)
                s = jnp.where(cidx <= r, s, NEG)
            dp = lax.dot_general(do_ref[rows, :], v_ref[0:ncols, :], (((1,), (1,)), ((), ())),
                                 preferred_element_type=F32)
            return s, dp

        def update(c, ncols, s, dp):
            rows = pl.ds(c * C, C)
            p = jnp.exp2(s - lse_ref[0, rows, :])
            ds = (p * (dp - delta_ref[0, rows, :])).astype(BF16)
            dv_sc[:, 0:ncols] += jnp.dot(do_ref[rows, :].T, p.astype(BF16), preferred_element_type=F32)
            dk_sc[:, 0:ncols] += jnp.dot(q_ref[rows, :].T, ds, preferred_element_type=F32)
            row0 = pl.multiple_of(qb * T + c * C, C)
            dq_ref[pl.ds(row0, C), :] += jnp.dot(ds, k_ref[0:ncols, :], preferred_element_type=F32)

        def tile_step(ncols_of, masked):
            cur = logits(0, ncols_of(0), masked)
            for c in range(nc):
                nxt = logits(c + 1, ncols_of(c + 1), masked) if c + 1 < nc else None
                update(c, ncols_of(c), *cur)
                cur = nxt

        @pl.when(qq > kj)
        def _():
            tile_step(lambda c: T, False)

        @pl.when(qq == kj)
        def _():
            tile_step(lambda c: (c + 1) * C, True)

        @pl.when(qq == nt - 1)
        def _():
            dk_ref[...] = dk_sc[...].T * math.log(2.0)
            dv_ref[...] = dv_sc[...].T

    qrow = pl.BlockSpec((T, HEAD_PAD), lambda h, t, kt, qt: (qt[t], h))
    krow = pl.BlockSpec((T, HEAD_PAD), lambda h, t, kt, qt: (kt[t], h))
    qcol = pl.BlockSpec((1, T, 1), lambda h, t, kt, qt: (h, qt[t], 0))
    grid_spec = pltpu.PrefetchScalarGridSpec(
        num_scalar_prefetch=2, grid=(heads, len(pairs)),
        in_specs=[qrow, krow, krow, qrow, qcol, qcol],
        out_specs=[pl.BlockSpec((S, HEAD_PAD), lambda h, t, kt, qt: (0, h)), krow, krow],
        scratch_shapes=[pltpu.VMEM((HEAD_PAD, T), F32), pltpu.VMEM((HEAD_PAD, T), F32)])
    return pl.pallas_call(
        body, name=name, grid_spec=grid_spec,
        out_shape=[jax.ShapeDtypeStruct((S, heads * HEAD_PAD), F32)] * 3,
        compiler_params=_cparams("arbitrary", "arbitrary"),
    )(k_tab, q_tab, q_arr, k_arr, v_arr, do_arr, lse, delta)


SWA_R = SWA_HEADS // SWA_KV_HEADS
SWA_SCALE = SWA_HD ** -0.5
SWA_Q0, SWA_K0, SWA_V0 = 0, 12, 14


def _swa_specs(tq):
    nsb = tq // WINDOW
    return [
        pl.BlockSpec((tq, SWA_R * HEAD_PAD), lambda g, i: (i, g)),
        pl.BlockSpec((tq, HEAD_PAD), lambda g, i: (i, SWA_K0 + g)),
        pl.BlockSpec((WINDOW, HEAD_PAD), lambda g, i: (jnp.maximum(nsb * i - 1, 0), SWA_K0 + g)),
        pl.BlockSpec((tq, HEAD_PAD), lambda g, i: (i, SWA_V0 + g)),
        pl.BlockSpec((WINDOW, HEAD_PAD), lambda g, i: (jnp.maximum(nsb * i - 1, 0), SWA_V0 + g)),
        pl.BlockSpec((SWA_R, WINDOW, 2 * WINDOW), lambda g, i: (g, 0, 0)),
        pl.BlockSpec((SWA_R, 8, 128), lambda g, i: (g, 0, 0)),
    ]


def _swa_block(i, sb, q_ref, kc_ref, kp_ref, vc_ref, vp_ref, bias, sink):
    rows = slice(sb * WINDOW, (sb + 1) * WINDOW)
    qs = jnp.concatenate([q_ref[rows, hh * HEAD_PAD:(hh + 1) * HEAD_PAD] for hh in range(SWA_R)], axis=0)
    if sb == 0:
        kp, vp = kp_ref[...], vp_ref[...]
    else:
        prev = slice((sb - 1) * WINDOW, sb * WINDOW)
        kp, vp = kc_ref[prev, :], vc_ref[prev, :]
    kk = jnp.concatenate([kp, kc_ref[rows, :]], axis=0)
    vv = jnp.concatenate([vp, vc_ref[rows, :]], axis=0)
    s = lax.dot_general(qs, kk, (((1,), (1,)), ((), ())), preferred_element_type=F32) * SWA_SCALE + bias
    if sb == 0:
        col = lax.broadcasted_iota(jnp.int32, (1, 2 * WINDOW), 1)
        s = s + jnp.where((col < WINDOW) & (i == 0), NEG, 0.0)
    return rows, qs, kk, vv, s


def _stack_heads(ref, rows, lead=None):
    if lead is None:
        return jnp.concatenate([ref[rows, hh * HEAD_PAD:(hh + 1) * HEAD_PAD] for hh in range(SWA_R)], axis=0)
    return jnp.concatenate([ref[hh, rows, :] for hh in range(SWA_R)], axis=0)


def _swa_fwd(name, proj_b, bias, sinks, *, tq):
    S = proj_b.shape[0]
    tq = min(tq, S)
    nsb = tq // WINDOW

    def body(q_ref, kc_ref, kp_ref, vc_ref, vp_ref, bias_ref, sink_ref, o_ref, lse_ref):
        i = pl.program_id(1)
        bias_v = bias_ref[...].reshape(SWA_R * WINDOW, 2 * WINDOW)
        sink = jnp.concatenate([jnp.zeros((WINDOW, 1), F32) + sink_ref[hh, 0:1, 0:1] for hh in range(SWA_R)], axis=0)
        for sb in range(nsb):
            rows, _, _, vv, s = _swa_block(i, sb, q_ref, kc_ref, kp_ref, vc_ref, vp_ref, bias_v, sink)
            m = jnp.maximum(jnp.max(s, axis=1, keepdims=True), sink)
            p = jnp.exp(s - m)
            l = jnp.sum(p, axis=1, keepdims=True) + jnp.exp(sink - m)
            o = jnp.dot(p.astype(BF16), vv, preferred_element_type=F32) / l
            lse_v = m + jnp.log(l)
            for hh in range(SWA_R):
                o_ref[rows, hh * HEAD_PAD:(hh + 1) * HEAD_PAD] = o[hh * WINDOW:(hh + 1) * WINDOW].astype(o_ref.dtype)
                lse_ref[hh, rows, :] = lse_v[hh * WINDOW:(hh + 1) * WINDOW]

    return pl.pallas_call(
        body, name=name, grid=(SWA_KV_HEADS, S // tq),
        in_specs=_swa_specs(tq),
        out_specs=[pl.BlockSpec((tq, SWA_R * HEAD_PAD), lambda g, i: (i, g)),
                   pl.BlockSpec((SWA_R, tq, 1), lambda g, i: (g, i, 0))],
        out_shape=[jax.ShapeDtypeStruct((S, SWA_HEADS * HEAD_PAD), BF16),
                   jax.ShapeDtypeStruct((SWA_HEADS, S, 1), F32)],
        compiler_params=_cparams("parallel", "parallel"),
    )(proj_b, proj_b, proj_b, proj_b, proj_b, bias, sinks)


def _swa_bwd(name, proj_b, bias, sinks, o, do, lse, *, tq):
    S = proj_b.shape[0]
    tq = min(tq, S)
    nsb = tq // WINDOW
    nq = S // tq

    def body(q_ref, kc_ref, kp_ref, vc_ref, vp_ref, bias_ref, sink_ref, o_ref, do_ref, lse_ref,
             dq_ref, dk_ref, dv_ref, dke_ref, dve_ref, dbias_ref, dsink_ref):
        i = pl.program_id(1)

        @pl.when(i == 0)
        def _():
            dbias_ref[...] = jnp.zeros(dbias_ref.shape, F32)
            dsink_ref[...] = jnp.zeros(dsink_ref.shape, F32)

        bias_v = bias_ref[...].reshape(SWA_R * WINDOW, 2 * WINDOW)
        sink = jnp.concatenate([jnp.zeros((WINDOW, 1), F32) + sink_ref[hh, 0:1, 0:1] for hh in range(SWA_R)], axis=0)
        dk_own, dv_own, dk_prev, dv_prev = [], [], [], []
        dbias_acc = jnp.zeros((SWA_R * WINDOW, 2 * WINDOW), F32)
        for sb in range(nsb):
            rows, qs, kk, vv, s = _swa_block(i, sb, q_ref, kc_ref, kp_ref, vc_ref, vp_ref, bias_v, sink)
            lse_v = _stack_heads(lse_ref, rows, lead=True)
            do_s = _stack_heads(do_ref, rows)
            delta = jnp.sum(do_s.astype(F32) * _stack_heads(o_ref, rows).astype(F32), axis=1, keepdims=True)
            p = jnp.exp(s - lse_v)
            dp = lax.dot_general(do_s, vv, (((1,), (1,)), ((), ())), preferred_element_type=F32)
            dsp = p * (dp - delta)
            dbias_acc = dbias_acc + dsp
            ds = (dsp * SWA_SCALE).astype(BF16)
            dq = jnp.dot(ds, kk, preferred_element_type=F32)
            dkk = lax.dot_general(ds, qs, (((0,), (0,)), ((), ())), preferred_element_type=F32)
            dvv = lax.dot_general(p.astype(BF16), do_s, (((0,), (0,)), ((), ())), preferred_element_type=F32)
            dk_prev.append(dkk[:WINDOW])
            dk_own.append(dkk[WINDOW:])
            dv_prev.append(dvv[:WINDOW])
            dv_own.append(dvv[WINDOW:])
            psink = jnp.exp(sink - lse_v) * delta
            for hh in range(SWA_R):
                hrows = slice(hh * WINDOW, (hh + 1) * WINDOW)
                dq_ref[rows, hh * HEAD_PAD:(hh + 1) * HEAD_PAD] = dq[hrows].astype(dq_ref.dtype)
                dsink_ref[hh] += jnp.zeros((8, 128), F32) - jnp.sum(psink[hrows])
        dbias_ref[...] += dbias_acc.reshape(SWA_R, WINDOW, 2 * WINDOW)
        for sb in range(nsb):
            rows = slice(sb * WINDOW, (sb + 1) * WINDOW)
            if sb + 1 < nsb:
                dk_ref[rows, :] = dk_own[sb] + dk_prev[sb + 1]
                dv_ref[rows, :] = dv_own[sb] + dv_prev[sb + 1]
            else:
                dk_ref[rows, :] = dk_own[sb]
                dv_ref[rows, :] = dv_own[sb]
        dke_ref[...] = dk_prev[0]
        dve_ref[...] = dv_prev[0]

    in_specs = _swa_specs(tq) + [
        pl.BlockSpec((tq, SWA_R * HEAD_PAD), lambda g, i: (i, g)),
        pl.BlockSpec((tq, SWA_R * HEAD_PAD), lambda g, i: (i, g)),
        pl.BlockSpec((SWA_R, tq, 1), lambda g, i: (g, i, 0)),
    ]
    kv_blk = pl.BlockSpec((tq, HEAD_PAD), lambda g, i: (i, g))
    edge_blk = pl.BlockSpec((WINDOW, HEAD_PAD), lambda g, i: (i, g))
    return pl.pallas_call(
        body, name=name, grid=(SWA_KV_HEADS, nq),
        in_specs=in_specs,
        out_specs=[pl.BlockSpec((tq, SWA_R * HEAD_PAD), lambda g, i: (i, g)), kv_blk, kv_blk, edge_blk, edge_blk,
                   pl.BlockSpec((SWA_R, WINDOW, 2 * WINDOW), lambda g, i: (g, 0, 0)),
                   pl.BlockSpec((SWA_R, 8, 128), lambda g, i: (g, 0, 0))],
        out_shape=[jax.ShapeDtypeStruct((S, SWA_HEADS * HEAD_PAD), BF16),
                   jax.ShapeDtypeStruct((S, SWA_KV_HEADS * HEAD_PAD), F32),
                   jax.ShapeDtypeStruct((S, SWA_KV_HEADS * HEAD_PAD), F32),
                   jax.ShapeDtypeStruct((nq * WINDOW, SWA_KV_HEADS * HEAD_PAD), F32),
                   jax.ShapeDtypeStruct((nq * WINDOW, SWA_KV_HEADS * HEAD_PAD), F32),
                   jax.ShapeDtypeStruct((SWA_HEADS, WINDOW, 2 * WINDOW), F32),
                   jax.ShapeDtypeStruct((SWA_HEADS, 8, 128), F32)],
        compiler_params=_cparams("arbitrary", "arbitrary"),
    )(proj_b, proj_b, proj_b, proj_b, proj_b, bias, sinks, o, do, lse)


def _dproj_b(name, dq_swa, dq_mem, dk, dv, dk_edge, dv_edge, *, tq):
    S = dq_swa.shape[0]
    tq = min(tq, S)
    nq = S // tq

    def body(dqs_ref, dqm_ref, dk_ref, dv_ref, dke_ref, dve_ref, o_ref):
        i = pl.program_id(0)
        o_ref[:, 0:1024] = dqs_ref[...]
        o_ref[:, 1024:1536] = dqm_ref[...].astype(o_ref.dtype)
        o_ref[:, 1536:1792] = dk_ref[...].astype(o_ref.dtype)
        o_ref[:, 1792:2048] = dv_ref[...].astype(o_ref.dtype)

        @pl.when(i < nq - 1)
        def _():
            last = slice(tq - WINDOW, tq)
            o_ref[last, 1536:1792] = (dk_ref[last, :] + dke_ref[...]).astype(o_ref.dtype)
            o_ref[last, 1792:2048] = (dv_ref[last, :] + dve_ref[...]).astype(o_ref.dtype)

    edge = pl.BlockSpec((WINDOW, SWA_KV_HEADS * HEAD_PAD), lambda i: (jnp.minimum(i + 1, nq - 1), 0))
    return pl.pallas_call(
        body, name=name, grid=(nq,),
        in_specs=[pl.BlockSpec((tq, 1024), lambda i: (i, 0)), pl.BlockSpec((tq, 512), lambda i: (i, 0)),
                  pl.BlockSpec((tq, 256), lambda i: (i, 0)), pl.BlockSpec((tq, 256), lambda i: (i, 0)), edge, edge],
        out_specs=pl.BlockSpec((tq, 2048), lambda i: (i, 0)),
        out_shape=jax.ShapeDtypeStruct((S, 2048), BF16),
        compiler_params=_cparams("parallel"),
    )(dq_swa, dq_mem, dk, dv, dk_edge, dv_edge)


def _exchange(name, send, *, per_peer):
    shape = send.shape[1:] if per_peer else send.shape

    def body(send_ref, recv_ref, send_sems, recv_sems, local_sem):
        x, y, c = lax.axis_index("x"), lax.axis_index("y"), lax.axis_index("c")
        me = 4 * x + 2 * y + c
        own = pltpu.make_async_copy(send_ref.at[me] if per_peer else send_ref, recv_ref.at[me], local_sem)
        own.start()
        copies = []
        for k in range(1, N_DEV):
            px = 1 - x if (k >> 2) & 1 else x
            py = 1 - y if (k >> 1) & 1 else y
            pc = 1 - c if k & 1 else c
            peer = 4 * px + 2 * py + pc
            out = pltpu.make_async_remote_copy(
                src_ref=send_ref.at[peer] if per_peer else send_ref, dst_ref=recv_ref.at[me],
                send_sem=send_sems.at[k - 1], recv_sem=recv_sems.at[k - 1],
                device_id=(px, py, pc), device_id_type=pl.DeviceIdType.MESH)
            out.start()
            back = pltpu.make_async_remote_copy(
                src_ref=send_ref.at[me] if per_peer else send_ref, dst_ref=recv_ref.at[peer],
                send_sem=send_sems.at[k - 1], recv_sem=recv_sems.at[k - 1],
                device_id=(px, py, pc), device_id_type=pl.DeviceIdType.MESH)
            copies.append((out, back))
        for out, back in copies:
            out.wait_send()
            back.wait_recv()
        own.wait()

    return pl.pallas_call(
        body, name=name,
        in_specs=[pl.BlockSpec(memory_space=pl.ANY)],
        out_specs=pl.BlockSpec(memory_space=pl.ANY),
        out_shape=jax.ShapeDtypeStruct((N_DEV,) + tuple(shape), send.dtype),
        scratch_shapes=[pltpu.SemaphoreType.DMA((N_DEV - 1,)), pltpu.SemaphoreType.DMA((N_DEV - 1,)),
                        pltpu.SemaphoreType.DMA(())],
    )(send)


def _gather_forwarded(name, block):
    def body(x_ref, out_ref, send_sems, recv_sems, local_sem):
        x, y, c = lax.axis_index("x"), lax.axis_index("y"), lax.axis_index("c")
        me, sibling = (x, y, c), (x, y, 1 - c)
        chips = [(1 - x, y), (x, 1 - y), (1 - x, 1 - y)]

        def slot(px, py, pc):
            return out_ref.at[4 * px + 2 * py + pc]

        def copy(k, blk, to, src=None):
            return pltpu.make_async_remote_copy(
                src_ref=slot(*blk) if src is None else src, dst_ref=slot(*blk),
                send_sem=send_sems.at[k], recv_sem=recv_sems.at[k],
                device_id=to, device_id_type=pl.DeviceIdType.MESH)

        mine = pltpu.make_async_copy(x_ref, slot(*me), local_sem)
        mine.start()
        first = [copy(0, me, sibling, src=x_ref)]
        first += [copy(1 + j, me, (*chip, c), src=x_ref) for j, chip in enumerate(chips)]
        for cp in first:
            cp.start()
        passed = [copy(4 + j, (*chip, c), sibling) for j, chip in enumerate(chips)]
        for j, chip in enumerate(chips):
            copy(1 + j, (*chip, c), me).wait_recv()
            passed[j].start()
        copy(0, sibling, me).wait_recv()
        for j, chip in enumerate(chips):
            copy(4 + j, (*chip, 1 - c), me).wait_recv()
        for cp in first + passed:
            cp.wait_send()
        mine.wait()

    return pl.pallas_call(
        body, name=name,
        in_specs=[pl.BlockSpec(memory_space=pl.ANY)],
        out_specs=pl.BlockSpec(memory_space=pl.ANY),
        out_shape=jax.ShapeDtypeStruct((N_DEV,) + tuple(block.shape), block.dtype),
        scratch_shapes=[pltpu.SemaphoreType.DMA((N_DEV - 1,)), pltpu.SemaphoreType.DMA((N_DEV - 1,)),
                        pltpu.SemaphoreType.DMA(())],
    )(block)


def _sibling_swap(name, block):
    def body(x_ref, out_ref, send_sem, recv_sem):
        x, y, c = lax.axis_index("x"), lax.axis_index("y"), lax.axis_index("c")
        cp = pltpu.make_async_remote_copy(src_ref=x_ref, dst_ref=out_ref, send_sem=send_sem, recv_sem=recv_sem,
                                          device_id=(x, y, 1 - c), device_id_type=pl.DeviceIdType.MESH)
        cp.start()
        cp.wait()

    return pl.pallas_call(
        body, name=name,
        in_specs=[pl.BlockSpec(memory_space=pl.ANY)],
        out_specs=pl.BlockSpec(memory_space=pl.ANY),
        out_shape=jax.ShapeDtypeStruct(block.shape, block.dtype),
        scratch_shapes=[pltpu.SemaphoreType.DMA(()), pltpu.SemaphoreType.DMA(())],
    )(block)


def _chip_exchange(name, send):
    def body(send_ref, recv_ref, send_sems, recv_sems, local_sem):
        x, y, c = lax.axis_index("x"), lax.axis_index("y"), lax.axis_index("c")
        me = 2 * x + y
        own = pltpu.make_async_copy(send_ref.at[me], recv_ref.at[me], local_sem)
        own.start()
        copies = []
        for k in range(1, 4):
            px = 1 - x if (k >> 1) & 1 else x
            py = 1 - y if k & 1 else y
            peer = 2 * px + py
            out = pltpu.make_async_remote_copy(
                src_ref=send_ref.at[peer], dst_ref=recv_ref.at[me],
                send_sem=send_sems.at[k - 1], recv_sem=recv_sems.at[k - 1],
                device_id=(px, py, c), device_id_type=pl.DeviceIdType.MESH)
            out.start()
            back = pltpu.make_async_remote_copy(
                src_ref=send_ref.at[me], dst_ref=recv_ref.at[peer],
                send_sem=send_sems.at[k - 1], recv_sem=recv_sems.at[k - 1],
                device_id=(px, py, c), device_id_type=pl.DeviceIdType.MESH)
            copies.append((out, back))
        for out, back in copies:
            out.wait_send()
            back.wait_recv()
        own.wait()

    return pl.pallas_call(
        body, name=name,
        in_specs=[pl.BlockSpec(memory_space=pl.ANY)],
        out_specs=pl.BlockSpec(memory_space=pl.ANY),
        out_shape=jax.ShapeDtypeStruct(send.shape, send.dtype),
        scratch_shapes=[pltpu.SemaphoreType.DMA((3,)), pltpu.SemaphoreType.DMA((3,)), pltpu.SemaphoreType.DMA(())],
    )(send)


def _reduce_scatter(send):
    n, rows, lanes = send.shape
    c = lax.axis_index("c")
    by_core = send.reshape(4, 2, rows, lanes)
    mine = lax.dynamic_index_in_dim(by_core, c, axis=1, keepdims=False).reshape(4 * rows, lanes)
    theirs = lax.dynamic_index_in_dim(by_core, 1 - c, axis=1, keepdims=False).reshape(4 * rows, lanes)
    from_sibling = _sibling_swap("grads_to_sibling", theirs)
    tm = max(t for t in range(16, ADAM_TM + 1, 16) if rows % t == 0)
    chip_sum = _rowwise("grads_chip_sum", lambda a, b: (a.astype(F32) + b.astype(F32),),
                        [_rows(mine), _rows(from_sibling)], [((4 * rows, lanes), BF16, "rows")],
                        rows=4 * rows, tm=tm)[0]
    return _chip_exchange("scatter_grads", chip_sum.reshape(4, rows, lanes))


def _adam(name, recv, w, m, v, *, tm=None):
    R = w.shape[0]
    n_parts = recv.shape[0]
    tm = max(t for t in range(8, min(tm or ADAM_TM, R) + 1, 8) if R % t == 0)
    c1 = 1.0 / (1.0 - ADAM_B1 ** ADAM_STEP)
    c2 = 1.0 / (1.0 - ADAM_B2 ** ADAM_STEP)

    def body(r_ref, w_ref, m_ref, v_ref, g_ref, d_ref, nm_ref, nv_ref):
        g = r_ref[0].astype(F32)
        for j in range(1, n_parts):
            g = g + r_ref[j].astype(F32)
        wv = w_ref[...]
        nm = ADAM_B1 * m_ref[...] + (1.0 - ADAM_B1) * g
        nv = ADAM_B2 * v_ref[...] + (1.0 - ADAM_B2) * (g * g)
        m_hat = nm * c1
        v_hat = nv * c2
        g_ref[...] = g
        d_ref[...] = -ADAM_LR * (m_hat / (jnp.sqrt(v_hat) + ADAM_EPS) + ADAM_WD * wv)
        nm_ref[...] = nm
        nv_ref[...] = nv

    row = pl.BlockSpec((tm, 128), lambda i: (i, 0))
    return pl.pallas_call(
        body, name=name, grid=(R // tm,),
        in_specs=[pl.BlockSpec((n_parts, tm, 128), lambda i: (0, i, 0)), row, row, row],
        out_specs=[row, row, row, row],
        out_shape=[jax.ShapeDtypeStruct((R, 128), F32)] * 4,
        compiler_params=_cparams("parallel"),
    )(recv, w, m, v)


def _pack_rows(arrs):
    return jnp.concatenate([a.reshape(-1, 128) for a in arrs], axis=0)


def _unstack(g, shape, axis):
    t = jnp.moveaxis(g, 0, axis)
    return t.reshape(shape)


def _restack(full, axis):
    s = full.shape
    t = full.reshape(s[:axis] + (N_DEV, s[axis] // N_DEV) + s[axis + 1:])
    return jnp.moveaxis(t, axis, 0)


def _pad_heads(w, heads, hd, axis):
    s = w.shape
    t = w.reshape(s[:axis] + (heads, hd) + s[axis + 1:])
    pad = [(0, 0)] * t.ndim
    pad[axis + 1] = (0, HEAD_PAD - hd)
    t = jnp.pad(t, pad)
    return t.reshape(s[:axis] + (heads * HEAD_PAD,) + s[axis + 1:])


def _unpad_heads(w, heads, hd, axis):
    s = w.shape
    t = w.reshape(s[:axis] + (heads, HEAD_PAD) + s[axis + 1:])
    t = lax.slice_in_dim(t, 0, hd, axis=axis + 1)
    return t.reshape(s[:axis] + (heads * hd,) + s[axis + 1:])


def _layer_weights(full, l):
    w_in = full["w_in"][l]
    cq, kva, qs, ks, vs, qm, gates = (w_in[:, 0:256], w_in[:, 256:416], w_in[:, 416:928], w_in[:, 928:1056],
                                       w_in[:, 1056:1184], w_in[:, 1184:1696], w_in[:, 1696:4768])
    wa = jnp.concatenate([gates, cq, jnp.pad(kva, ((0, 0), (0, 96)))], axis=1)
    wb = jnp.concatenate([_pad_heads(qs, SWA_HEADS, SWA_HD, 1), qm, _pad_heads(ks, SWA_KV_HEADS, SWA_HD, 1),
                          _pad_heads(vs, SWA_KV_HEADS, SWA_HD, 1)], axis=1)
    wuq = _pad_heads(full["w_uq"][l], MLA_HEADS, MLA_NOPE + MLA_ROPE, 1)
    ukv = full["w_ukv"][l].reshape(MLA_KV_LORA, MLA_HEADS, MLA_NOPE + MLA_V)
    wuk = _pad_heads(ukv[:, :, :MLA_NOPE].reshape(MLA_KV_LORA, -1), MLA_HEADS, MLA_NOPE, 1)
    wuv = _pad_heads(ukv[:, :, MLA_NOPE:].reshape(MLA_KV_LORA, -1), MLA_HEADS, MLA_V, 1)
    wo_mla = _pad_heads(full["w_o_mla"][l], MLA_HEADS, MLA_V, 0)
    wo_swa = _pad_heads(full["w_o_swa"][l], SWA_HEADS, SWA_HD, 0)
    wo_mem = full["w_o_mem"][l]
    w = dict(wag=wa[:, :3072], wat=wa[:, 3072:], wb=wb, wuq=wuq, wuk=wuk, wuv=wuv, wo_mla=wo_mla, wo_swa=wo_swa,
             wo_mem=wo_mem, wmem=full["w_mem_kv"][l], wout=full["w_out"][l], wup=full["w_up"][l],
             wdown=full["w_down"][l])
    w.update({k + "_t": v.T for k, v in w.items()})
    return w


def _layer_weight_grads(g):
    dwa_g, dwa_t, dwb = g["wag"], g["wat"], g["wb"]
    d_in = jnp.concatenate([
        dwa_t[:, 0:256], dwa_t[:, 256:416],
        _unpad_heads(dwb[:, 0:1024], SWA_HEADS, SWA_HD, 1),
        _unpad_heads(dwb[:, 1536:1792], SWA_KV_HEADS, SWA_HD, 1),
        _unpad_heads(dwb[:, 1792:2048], SWA_KV_HEADS, SWA_HD, 1),
        dwb[:, 1024:1536], dwa_g], axis=1)
    duk = _unpad_heads(g["wuk"], MLA_HEADS, MLA_NOPE, 1).reshape(MLA_KV_LORA, MLA_HEADS, MLA_NOPE)
    duv = _unpad_heads(g["wuv"], MLA_HEADS, MLA_V, 1).reshape(MLA_KV_LORA, MLA_HEADS, MLA_V)
    return dict(
        w_in=d_in,
        w_uq=_unpad_heads(g["wuq"], MLA_HEADS, MLA_NOPE + MLA_ROPE, 1),
        w_ukv=jnp.concatenate([duk, duv], axis=2).reshape(MLA_KV_LORA, -1),
        w_mem_kv=g["wmem"],
        w_o_mla=_unpad_heads(g["wo_mla"], MLA_HEADS, MLA_V, 0),
        w_o_swa=_unpad_heads(g["wo_swa"], SWA_HEADS, SWA_HD, 0),
        w_o_mem=g["wo_mem"], w_out=g["wout"], w_up=g["wup"], w_down=g["wdown"])


def _rope_tables(S):
    pos = jnp.arange(S, dtype=F32)
    inv = 1.0 / (ROPE_THETA ** (jnp.arange(0, MLA_ROPE, 2, dtype=F32) / MLA_ROPE))
    ang = pos[:, None] * inv[None, :]
    cos, sin = jnp.cos(ang), jnp.sin(ang)
    z16 = jnp.zeros((S, 16), F32)
    z32 = jnp.zeros((S, 32), F32)
    c = jnp.concatenate([jnp.ones((S, 64), F32), cos, cos, z32], axis=1)
    ck = jnp.concatenate([jnp.zeros((S, 64), F32), cos, cos, z32], axis=1)
    s1 = jnp.concatenate([jnp.zeros((S, 80), F32), sin, z32], axis=1)
    s2 = jnp.concatenate([jnp.zeros((S, 64), F32), -sin, z16, z32], axis=1)
    return c, ck, s1, s2


def _t5_bucket(dist):
    n = jnp.maximum(dist, 0)
    max_exact = REL_BUCKETS // 2
    nf = jnp.maximum(n, 1).astype(F32)
    large = max_exact + (jnp.log(nf / max_exact) / math.log(REL_MAX_DIST / max_exact)
                         * (REL_BUCKETS - max_exact)).astype(jnp.int32)
    large = jnp.minimum(large, REL_BUCKETS - 1)
    return jnp.where(n < max_exact, n, large)


def _bias_onehot():
    qi = jnp.arange(WINDOW)[:, None]
    kj = jnp.arange(2 * WINDOW)[None, :]
    dist = qi + WINDOW - kj
    valid = (dist >= 0) & (dist < WINDOW)
    bucket = _t5_bucket(dist)
    onehot = (bucket[None] == jnp.arange(REL_BUCKETS)[:, None, None]) & valid[None]
    return (onehot.reshape(REL_BUCKETS, -1).astype(F32),
            jnp.where(valid, 0.0, NEG).astype(F32).reshape(1, -1))


def _rstd(x):
    return lax.rsqrt(jnp.mean(x * x, axis=-1, keepdims=True) + EPS)


def _norm_bwd(dh, x, g):
    r = _rstd(x)
    xh = x * r
    w = dh * g
    dx = r * (w - xh * jnp.mean(w * xh, axis=-1, keepdims=True))
    return dx, jnp.sum(dh * xh, axis=0, keepdims=True)


def _tile_lanes(t, n):
    return jnp.tile(t, (1, n // t.shape[1])) if n != t.shape[1] else t


def _rope_fwd(a, c, s1, s2):
    n = a.shape[1]
    return (a * _tile_lanes(c, n) + pltpu.roll(a, 16, 1) * _tile_lanes(s1, n)
            + pltpu.roll(a, n - 16, 1) * _tile_lanes(s2, n))


def _rope_bwd(d, c, s1, s2):
    n = d.shape[1]
    return (d * _tile_lanes(c, n) + pltpu.roll(d * _tile_lanes(s1, n), n - 16, 1)
            + pltpu.roll(d * _tile_lanes(s2, n), 16, 1))


def _sigmoid(x):
    return 1.0 / (1.0 + jnp.exp(-x))


def _rmsnorm(name, x, g, dtype):
    def fn(xv, gv):
        return ((xv * _rstd(xv)) * gv,)
    return _rowwise(name, fn, [_rows(x), _full(g)], [(x.shape, dtype, "rows")], rows=x.shape[0])[0]


def _residual_norm_bwd(name, dres, dh, x, g):
    def fn(dr, dhv, xv, gv):
        dx, dg = _norm_bwd(dhv, xv, gv)
        return dr + dx, dg
    return _rowwise(name, fn, [_rows(dres), _rows(dh), _rows(x), _full(g)],
                    [(x.shape, F32, "rows"), (g.shape, F32, "acc")], rows=x.shape[0])


def _layer_fwd(l, x, mem, w, p, tabs, swa_bias, S):
    c, ck, s1, s2 = tabs
    n = f"l{l}_"
    h = _rmsnorm(n + "attn_norm", x, p["attn_norm"], BF16)
    gates = _mm(n + "proj_gates", h, w["wag"], [BF16], tn=1024)
    proj_a = _mm(n + "proj_tail", h, w["wat"], [F32])
    proj_b = _mm(n + "proj_b", h, w["wb"], [BF16])

    def prep(cq, kva, qn, kvn, ckv, s1v, s2v):
        cqn = cq * _rstd(cq) * qn
        ckv_ = kva[:, :128]
        ckvn = ckv_ * _rstd(ckv_) * kvn
        pe = pltpu.roll(kva[:, 128:], 64, 1)
        return cqn, ckvn, _rope_fwd(pe, ckv, s1v, s2v)

    cqn, ckvn, kpe = _rowwise(
        n + "mla_prep", prep,
        [_rows(proj_a, 256, 0), _rows(proj_a, 256, 1), _full(p["mla_q_norm"]), _full(p["mla_kv_norm"]),
         _rows(ck), _rows(s1), _rows(s2)],
        [((S, 256), BF16, "rows"), ((S, 128), BF16, "rows"), ((S, 128), F32, "rows")], rows=S)

    q_mla = _mm(n + "q_mla", cqn, w["wuq"], [BF16],
                epi=lambda acc, cv, s1v, s2v: (_rope_fwd(acc, cv, s1v, s2v) * (MLA_SCALE * LOG2E),),
                extras=[(c, "m"), (s1, "m"), (s2, "m")])
    k_mla = _mm(n + "k_mla", ckvn, w["wuk"], [BF16],
                epi=lambda acc, kp: (acc + _tile_lanes(kp, acc.shape[1]),), extras=[(kpe, "m")])
    den = (jnp.arange(MLA_HEADS * HEAD_PAD) % HEAD_PAD == DEN_LANE).astype(F32)[None]
    v_mla = _mm(n + "v_mla", ckvn, w["wuv"], [BF16], epi=lambda acc, dv: (acc + dv,), extras=[(den, "n")])
    o_mla, lse_mla = _causal_fwd(n + "mla_fwd", q_mla, k_mla, v_mla, heads=MLA_HEADS, tile=MLA_TILE,
                                 chunk=MLA_CHUNK_FWD)
    o_swa, lse_swa = _swa_fwd(n + "swa_fwd", proj_b, swa_bias, p["sinks"], tq=SWA_TQ)
    mn = _rmsnorm(n + "mem_norm", mem, p["mem_norm"], BF16)
    kvm = _mm(n + "kv_mem", mn, w["wmem"], [BF16])
    o_mem, lse_mem = _flash_fwd(n + "mem_fwd", proj_b, kvm, kvm, heads=MEM_HEADS, q_off=8, k_off=0, v_off=4,
                                group=1, mode="full", scale=MEM_HD ** -0.5, tq=MEM_TQ, tk=MEM_LEN)
    t0 = _mm(n + "t_mla", o_mla, w["wo_mla"], [BF16], tn=1024)
    t1 = _mm(n + "t_swa", o_swa, w["wo_swa"], [BF16], tn=1024)
    t2 = _mm(n + "t_mem", o_mem, w["wo_mem"], [BF16], tn=1024)

    def merge(g0, g1, g2, bg, a0, a1, a2):
        y = (_sigmoid(g0 + bg[:, 0:1024]) * a0 + _sigmoid(g1 + bg[:, 1024:2048]) * a1
             + _sigmoid(g2 + bg[:, 2048:3072]) * a2)
        return (y,)

    y = _rowwise(n + "merge", merge,
                 [_rows(gates, 1024, 0), _rows(gates, 1024, 1), _rows(gates, 1024, 2), _full(p["b_gate"]),
                  _rows(t0), _rows(t1), _rows(t2)], [((S, D_MODEL), BF16, "rows")], rows=S)[0]
    x1 = _mm(n + "out_proj", y, w["wout"], [F32], epi=lambda acc, r: (acc + r,), extras=[(x, "mn")])
    h2 = _rmsnorm(n + "mlp_norm", x1, p["mlp_norm"], BF16)
    act = _mm(n + "mlp_up", h2, w["wup"], [BF16], epi=lambda acc: (jnp.square(jnp.maximum(acc, 0.0)),), tn=1024)
    x2 = _mm(n + "mlp_down", act, w["wdown"], [F32], epi=lambda acc, r: (acc + r,), extras=[(x1, "mn")])
    saved = dict(x=x, h=h, gates=gates, proj_a=proj_a, proj_b=proj_b, cqn=cqn, ckvn=ckvn, q_mla=q_mla, k_mla=k_mla, v_mla=v_mla,
                 o_mla=o_mla, lse_mla=lse_mla, o_swa=o_swa, lse_swa=lse_swa, mn=mn, kvm=kvm, o_mem=o_mem,
                 lse_mem=lse_mem, t0=t0, t1=t1, t2=t2, y=y, x1=x1, h2=h2, act=act)
    return x2, saved


def _layer_bwd(l, dx2, mem, w, p, tabs, swa_bias, sv, S):
    c, ck, s1, s2 = tabs
    n = f"l{l}_b_"
    gw = {}
    gs = {}
    du = _mm(n + "d_act", dx2, w["wdown_t"], [BF16],
             epi=lambda acc, av: (acc * (2.0 * jnp.sqrt(av.astype(F32))),), extras=[(sv["act"], "mn")], tn=1024)
    gw["wdown"] = _mm_tn(n + "g_wdown", sv["act"], dx2)
    gw["wup"] = _mm_tn(n + "g_wup", sv["h2"], du)
    dh2 = _mm(n + "d_h2", du, w["wup_t"], [F32])
    dx1, gs["mlp_norm"] = _residual_norm_bwd(n + "mlp_norm", dx2, dh2, sv["x1"], p["mlp_norm"])
    gw["wout"] = _mm_tn(n + "g_wout", sv["y"], dx1)
    dy = _mm(n + "d_y", dx1, w["wout_t"], [F32])

    def merge_bwd(dyv, g0, g1, g2, bg, a0, a1, a2):
        outs, dgs = [], []
        for b, (gv, av) in enumerate(((g0, a0), (g1, a1), (g2, a2))):
            sg = _sigmoid(gv + bg[:, b * 1024:(b + 1) * 1024])
            outs.append(dyv * sg)
            dgs.append(dyv * av * sg * (1.0 - sg))
        dg = jnp.concatenate(dgs, axis=1)
        return outs[0], outs[1], outs[2], dg, jnp.sum(dg, axis=0, keepdims=True)

    pa = sv["proj_a"]
    gt = sv["gates"]
    dt0, dt1, dt2, dgates, gs["b_gate"] = _rowwise(
        n + "merge", merge_bwd,
        [_rows(dy), _rows(gt, 1024, 0), _rows(gt, 1024, 1), _rows(gt, 1024, 2), _full(p["b_gate"]),
         _rows(sv["t0"]), _rows(sv["t1"]), _rows(sv["t2"])],
        [((S, D_MODEL), BF16, "rows")] * 3 + [((S, 3 * D_MODEL), BF16, "rows"), ((1, 3 * D_MODEL), F32, "acc")],
        rows=S)
    gw["wo_mla"] = _mm_tn(n + "g_wo_mla", sv["o_mla"], dt0)
    gw["wo_swa"] = _mm_tn(n + "g_wo_swa", sv["o_swa"], dt1)
    gw["wo_mem"] = _mm_tn(n + "g_wo_mem", sv["o_mem"], dt2)
    do_mla = _mm(n + "d_o_mla", dt0, w["wo_mla_t"], [BF16])
    do_swa = _mm(n + "d_o_swa", dt1, w["wo_swa_t"], [BF16])
    do_mem = _mm(n + "d_o_mem", dt2, w["wo_mem_t"], [BF16])
    pb = sv["proj_b"]
    delta_mla = _row_dot(n + "mla_delta", sv["o_mla"], do_mla, heads=MLA_HEADS, tm=MLA_TILE)
    dq_mla, dk_mla, dv_mla = _causal_bwd(
        n + "mla_bwd", sv["q_mla"], sv["k_mla"], sv["v_mla"], do_mla, sv["lse_mla"], delta_mla, heads=MLA_HEADS,
        tile=MLA_TILE, chunk=MLA_CHUNK)
    dq_swa, dk_swa, dv_swa, dk_edge, dv_edge, dbias, dsink = _swa_bwd(
        n + "swa_bwd", pb, swa_bias, p["sinks"], sv["o_swa"], do_swa, sv["lse_swa"], tq=SWA_TQ)
    dq_mem, dk_mem, dv_mem = _flash_bwd(
        n + "mem_bwd", pb, sv["kvm"], sv["kvm"], sv["o_mem"], do_mem, sv["lse_mem"], heads=MEM_HEADS,
        q_off=8, k_off=0, v_off=4, group=1, mode="full", scale=MEM_HD ** -0.5, tq=MEM_TQ, tk=MEM_LEN)
    gs["dbias"] = dbias
    gs["sinks"] = dsink[:, 0, 0]
    dkvm = jnp.concatenate([dk_mem, dv_mem], axis=1)
    gw["wmem"] = _mm_tn(n + "g_wmem", sv["mn"], dkvm)
    dmn = _mm(n + "d_mn", dkvm, w["wmem_t"], [F32])
    _, gs["mem_norm"] = _residual_norm_bwd(n + "mem_norm", dmn, dmn, mem, p["mem_norm"])
    dq_pre = _rowwise(n + "q_unrope", lambda d, cv, s1v, s2v: (_rope_bwd(d * MLA_SCALE, cv, s1v, s2v),),
                      [_rows(dq_mla), _rows(c), _rows(s1), _rows(s2)], [((S, 1024), BF16, "rows")], rows=S)[0]
    gw["wuq"] = _mm_tn(n + "g_wuq", sv["cqn"], dq_pre)
    gw["wuk"] = _mm_tn(n + "g_wuk", sv["ckvn"], dk_mla)
    gw["wuv"] = _mm_tn(n + "g_wuv", sv["ckvn"], dv_mla)
    dcqn = _mm(n + "d_cqn", dq_pre, w["wuq_t"], [F32])
    dckvn = _mm(n + "d_ckvn_k", dk_mla, w["wuk_t"], [F32])
    dckvn = _mm(n + "d_ckvn_v", dv_mla, w["wuv_t"], [F32], epi=lambda acc, r: (acc + r,), extras=[(dckvn, "mn")])

    def mla_norm_bwd(dcq_n, dckv_n, dk, cq, kva, qn, kvn, ckv, s1v, s2v):
        dcq, dqn = _norm_bwd(dcq_n, cq, qn)
        dckv, dkvn = _norm_bwd(dckv_n, kva[:, :128], kvn)
        dkpe = dk[:, 0:128]
        for hh in range(1, MLA_HEADS):
            dkpe = dkpe + dk[:, hh * 128:(hh + 1) * 128]
        dpe = pltpu.roll(_rope_bwd(dkpe, ckv, s1v, s2v), 64, 1)
        return jnp.concatenate([dcq, dckv, dpe], axis=1), dqn, dkvn

    dtail, gs["mla_q_norm"], gs["mla_kv_norm"] = _rowwise(
        n + "mla_norm", mla_norm_bwd,
        [_rows(dcqn), _rows(dckvn), _rows(dk_mla), _rows(pa, 256, 0), _rows(pa, 256, 1),
         _full(p["mla_q_norm"]), _full(p["mla_kv_norm"]), _rows(ck), _rows(s1), _rows(s2)],
        [((S, 512), BF16, "rows"), ((1, 256), F32, "acc"), ((1, 128), F32, "acc")], rows=S)

    dproj_b = _dproj_b(n + "dproj_b", dq_swa, dq_mem, dk_swa, dv_swa, dk_edge, dv_edge, tq=SWA_TQ)
    h = sv["h"]
    gw["wag"] = _mm_tn(n + "g_wa_gates", h, dgates)
    gw["wat"] = _mm_tn(n + "g_wa_tail", h, dtail)
    gw["wb"] = _mm_tn(n + "g_wb", h, dproj_b)
    dh = _mm(n + "d_h_gates", dgates, w["wag_t"], [F32])
    dh = _mm(n + "d_h_tail", dtail, w["wat_t"], [F32], epi=lambda acc, r: (acc + r,), extras=[(dh, "mn")])
    dh = _mm(n + "d_h_b", dproj_b, w["wb_t"], [F32], epi=lambda acc, r: (acc + r,), extras=[(dh, "mn")])
    dx, gs["attn_norm"] = _residual_norm_bwd(n + "attn_norm", dx1, dh, sv["x"], p["attn_norm"])
    return dx, gw, gs


def _local_step(x, mem, loss_target, full, small):
    S = x.shape[0]
    tabs = _rope_tables(S)
    onehot, band = _bias_onehot()
    hi = lax.Precision.HIGHEST
    swa_bias = _mm("swa_bias", small["rel_bias"].T, onehot, [F32], epi=lambda acc, mk: (acc + mk,),
                   extras=[(band, "n")], cast=None, precision=hi, tn=8192).reshape(SWA_HEADS, WINDOW, 2 * WINDOW)
    ws, ps = [], []
    for l in range(DEPTH):
        ws.append(_layer_weights(full, l))
        ps.append(dict(
            attn_norm=small["attn_norm"][l][None], mem_norm=small["mem_norm"][l][None],
            b_gate=small["b_gate"][l][None], mla_q_norm=small["mla_q_norm"][l][None],
            mla_kv_norm=small["mla_kv_norm"][l][None], mlp_norm=small["mlp_norm"][l][None],
            sinks=jnp.broadcast_to(small["attn_sinks"][l][:, None, None], (SWA_HEADS, 8, 128))))
    saved = []
    xc = x
    for l in range(DEPTH):
        xc, sv = _layer_fwd(l, xc, mem, ws[l], ps[l], tabs, swa_bias, S)
        saved.append(sv)

    fn_g = small["final_norm"][None]

    def loss_fn(xv, gv, tv):
        r = _rstd(xv)
        xh = xv * r
        err = xh * gv - tv
        dyv = err * (1.0 / D_MODEL)
        wv = dyv * gv
        dx = r * (wv - xh * jnp.mean(wv * xh, axis=-1, keepdims=True))
        part = 0.5 * jnp.sum(err * err) * (1.0 / D_MODEL)
        return dx, jnp.sum(dyv * xh, axis=0, keepdims=True), jnp.zeros((8, 128), F32) + part

    dx, g_final, loss_acc = _rowwise(
        "loss", loss_fn, [_rows(xc), _full(fn_g), _rows(loss_target)],
        [((S, D_MODEL), F32, "rows"), ((1, D_MODEL), F32, "acc"), ((8, 128), F32, "acc")], rows=S)

    gws, gss = [None] * DEPTH, [None] * DEPTH
    for l in reversed(range(DEPTH)):
        dx, gw, gs = _layer_bwd(l, dx, mem, ws[l], ps[l], tabs, swa_bias, saved[l], S)
        gws[l] = _layer_weight_grads(gw)
        gss[l] = gs

    dbias = (gss[0]["dbias"] + gss[1]["dbias"]).reshape(SWA_HEADS, -1)
    g_rel = _mm("g_rel_bias", dbias, onehot.T, [F32], cast=None, precision=hi, tk=8192).T
    wgrads = {k: jnp.stack([gws[l][k] for l in range(DEPTH)]) for k in gws[0]}
    sgrads = dict(
        rel_bias=g_rel,
        final_norm=g_final[0],
        attn_sinks=jnp.stack([gss[l]["sinks"] for l in range(DEPTH)]),
        **{k: jnp.concatenate([gss[l][k] for l in range(DEPTH)], axis=0)
           for k in ("attn_norm", "mem_norm", "b_gate", "mla_q_norm", "mla_kv_norm", "mlp_norm")})
    return loss_acc[0, 0], dx, wgrads, sgrads


def _pack_small(vals, loss):
    rows = []
    for name, shape in SMALL:
        flat = vals[name].astype(F32).reshape(-1)
        pad = (-flat.shape[0]) % 1024
        rows.append(jnp.pad(flat, (0, pad)).reshape(-1, 128))
    rows.append(jnp.zeros((8, 128), F32) + loss)
    return jnp.concatenate(rows, axis=0)


def _unpack_small(packed):
    out, r = {}, 0
    for name, shape in SMALL:
        size = math.prod(shape)
        nrows = 8 * -(-size // 1024)
        out[name] = packed[r:r + nrows].reshape(-1)[:size].reshape(shape)
        r += nrows
    return out, packed[r, 0]


def kernel(x, mem, rel_bias, attn_norm, mem_norm, w_in, b_gate, mla_q_norm, w_uq, mla_kv_norm, w_ukv, attn_sinks, w_mem_kv, w_o_mla, w_o_swa, w_o_mem, w_out, mlp_norm, w_up, w_down, final_norm, loss_target, m_rel_bias, m_attn_norm, m_mem_norm, m_w_in, m_b_gate, m_mla_q_norm, m_w_uq, m_mla_kv_norm, m_w_ukv, m_attn_sinks, m_w_mem_kv, m_w_o_mla, m_w_o_swa, m_w_o_mem, m_w_out, m_mlp_norm, m_w_up, m_w_down, m_final_norm, v_rel_bias, v_attn_norm, v_mem_norm, v_w_in, v_b_gate, v_mla_q_norm, v_w_uq, v_mla_kv_norm, v_w_ukv, v_attn_sinks, v_w_mem_kv, v_w_o_mla, v_w_o_swa, v_w_o_mem, v_w_out, v_mlp_norm, v_w_up, v_w_down, v_final_norm):
    wv = dict(rel_bias=rel_bias, attn_norm=attn_norm, mem_norm=mem_norm, w_in=w_in, b_gate=b_gate,
              mla_q_norm=mla_q_norm, w_uq=w_uq, mla_kv_norm=mla_kv_norm, w_ukv=w_ukv, attn_sinks=attn_sinks,
              w_mem_kv=w_mem_kv, w_o_mla=w_o_mla, w_o_swa=w_o_swa, w_o_mem=w_o_mem, w_out=w_out,
              mlp_norm=mlp_norm, w_up=w_up, w_down=w_down, final_norm=final_norm)
    mv = dict(rel_bias=m_rel_bias, attn_norm=m_attn_norm, mem_norm=m_mem_norm, w_in=m_w_in, b_gate=m_b_gate,
              mla_q_norm=m_mla_q_norm, w_uq=m_w_uq, mla_kv_norm=m_mla_kv_norm, w_ukv=m_w_ukv,
              attn_sinks=m_attn_sinks, w_mem_kv=m_w_mem_kv, w_o_mla=m_w_o_mla, w_o_swa=m_w_o_swa,
              w_o_mem=m_w_o_mem, w_out=m_w_out, mlp_norm=m_mlp_norm, w_up=m_w_up, w_down=m_w_down,
              final_norm=m_final_norm)
    vv = dict(rel_bias=v_rel_bias, attn_norm=v_attn_norm, mem_norm=v_mem_norm, w_in=v_w_in, b_gate=v_b_gate,
              mla_q_norm=v_mla_q_norm, w_uq=v_w_uq, mla_kv_norm=v_mla_kv_norm, w_ukv=v_w_ukv,
              attn_sinks=v_attn_sinks, w_mem_kv=v_w_mem_kv, w_o_mla=v_w_o_mla, w_o_swa=v_w_o_swa,
              w_o_mem=v_w_o_mem, w_out=v_w_out, mlp_norm=v_mlp_norm, w_up=v_w_up, w_down=v_w_down,
              final_norm=v_final_norm)

    shard_rows = [math.prod(_shard_shape(shape, axis)) // 128 for _, shape, axis in WSPECS]
    gathered = _gather_forwarded("gather_weights", _pack_rows([wv[name].astype(BF16) for name, _, _ in WSPECS]))
    full, r = {}, 0
    for (name, shape, axis), nr in zip(WSPECS, shard_rows):
        full[name] = _unstack(gathered[:, r:r + nr].reshape((N_DEV,) + _shard_shape(shape, axis)), shape, axis)
        r += nr

    loss_part, grad_x, wgrads, sgrads = _local_step(x[0], mem[0], loss_target[0], full,
                                                    {name: wv[name] for name, _ in SMALL})

    send = jnp.concatenate([_restack(wgrads[name], axis).astype(BF16).reshape(N_DEV, -1, 128)
                            for name, _, axis in WSPECS], axis=1)
    recv = _reduce_scatter(send)
    outs = _adam("adam_sharded", recv, *[_pack_rows([d[name] for name, _, _ in WSPECS]) for d in (wv, mv, vv)])
    res = {}
    r = 0
    for (name, shape, axis), nr in zip(WSPECS, shard_rows):
        res[name] = [o[r:r + nr].reshape(_shard_shape(shape, axis)) for o in outs]
        r += nr

    small_recv = _exchange("gather_small", _pack_small(sgrads, loss_part), per_peer=False)
    zero = jnp.zeros((), F32)
    souts = _adam("adam_small", small_recv, *[_pack_small(d, zero) for d in (wv, mv, vv)])
    loss = None
    for i, o in enumerate(souts):
        vals, extra = _unpack_small(o)
        if i == 0:
            loss = extra
        for name, _ in SMALL:
            res.setdefault(name, []).append(vals[name])

    out = [loss, grad_x[None]]
    for i in range(4):
        out.extend(res[name][i] for name in WEIGHT_ORDER)
    return tuple(out)
```

```python
import math

import jax
import jax.numpy as jnp
from jax import lax
from jax.experimental import pallas as pl
from jax.experimental.pallas import tpu as pltpu

F32 = jnp.float32
BF16 = jnp.bfloat16

N_DEV = 8
D_MODEL = 1024
DEPTH = 2
MLA_HEADS = 8
MLA_Q_LORA = 256
MLA_KV_LORA = 128
MLA_NOPE = 64
MLA_ROPE = 32
MLA_V = 64
ROPE_THETA = 10000.0
SWA_HEADS = 8
SWA_KV_HEADS = 2
SWA_HD = 64
WINDOW = 128
REL_BUCKETS = 32
REL_MAX_DIST = 128
MEM_LEN = 256
MEM_HEADS = 4
MEM_HD = 128
D_FF = 4 * D_MODEL
EPS = 1e-6
HEAD_PAD = 128
ADAM_LR = 0.001
ADAM_B1 = 0.9
ADAM_B2 = 0.999
ADAM_EPS = 1e-08
ADAM_WD = 0.01
ADAM_STEP = 10

NEG = -1e30
VMEM_LIMIT = 48 * 1024 * 1024

MM_TM = 1024
MM_TN = 1024
MM_TK = 1024
TN_T1 = 1024
TN_TN = 1024
TN_TS = 1024
ROW_TM = 256
MLA_TILE = 1024
MLA_CHUNK = 256
MLA_CHUNK_FWD = 512
MLA_SCALE = (MLA_NOPE + MLA_ROPE) ** -0.5
LOG2E = math.log2(math.e)
DEN_LANE = MLA_V
SWA_TQ = 1024
MEM_TQ = 1024
MEM_CHUNK = 256
ADAM_TM = 1200

WSPECS = (
    ("w_in", (DEPTH, D_MODEL, 4768), 2),
    ("w_uq", (DEPTH, MLA_Q_LORA, 768), 2),
    ("w_ukv", (DEPTH, MLA_KV_LORA, 1024), 2),
    ("w_mem_kv", (DEPTH, D_MODEL, 1024), 1),
    ("w_o_mla", (DEPTH, 512, D_MODEL), 2),
    ("w_o_swa", (DEPTH, 512, D_MODEL), 2),
    ("w_o_mem", (DEPTH, 512, D_MODEL), 2),
    ("w_out", (DEPTH, D_MODEL, D_MODEL), 1),
    ("w_up", (DEPTH, D_MODEL, D_FF), 2),
    ("w_down", (DEPTH, D_FF, D_MODEL), 1),
)
SMALL = (
    ("rel_bias", (REL_BUCKETS, SWA_HEADS)),
    ("attn_norm", (DEPTH, D_MODEL)),
    ("mem_norm", (DEPTH, D_MODEL)),
    ("b_gate", (DEPTH, 3 * D_MODEL)),
    ("mla_q_norm", (DEPTH, MLA_Q_LORA)),
    ("mla_kv_norm", (DEPTH, MLA_KV_LORA)),
    ("attn_sinks", (DEPTH, SWA_HEADS)),
    ("mlp_norm", (DEPTH, D_MODEL)),
    ("final_norm", (D_MODEL,)),
)
WEIGHT_ORDER = ("rel_bias", "attn_norm", "mem_norm", "w_in", "b_gate", "mla_q_norm", "w_uq", "mla_kv_norm",
                "w_ukv", "attn_sinks", "w_mem_kv", "w_o_mla", "w_o_swa", "w_o_mem", "w_out", "mlp_norm",
                "w_up", "w_down", "final_norm")


def _cparams(*sem):
    return pltpu.CompilerParams(dimension_semantics=sem, vmem_limit_bytes=VMEM_LIMIT)


def _shard_shape(shape, axis):
    s = list(shape)
    s[axis] //= N_DEV
    return tuple(s)


def _mm(name, a, b, out_dtypes, *, epi=None, extras=(), a_fn=None, cast=BF16, precision=None,
        tm=None, tn=None, tk=None):
    M, K = a.shape
    K2, N = b.shape
    assert K == K2, (name, a.shape, b.shape)
    tm = min(tm or MM_TM, M)
    tn = min(tn or MM_TN, N)
    tk = min(tk or MM_TK, K)
    assert M % tm == 0 and N % tn == 0 and K % tk == 0, (name, a.shape, b.shape, tm, tn, tk)
    nk = K // tk
    n_ex = len(extras)
    n_out = len(out_dtypes)

    def body(*refs):
        a_ref, b_ref = refs[0], refs[1]
        ex_refs = refs[2:2 + n_ex]
        out_refs = refs[2 + n_ex:2 + n_ex + n_out]
        av = a_ref[...]
        if a_fn is not None:
            av = a_fn(av)
        bv = b_ref[...]
        if cast is not None:
            av = av.astype(cast)
            bv = bv.astype(cast)
        part = jnp.dot(av, bv, preferred_element_type=F32, precision=precision)

        def finish(acc):
            outs = epi(acc, *[r[...] for r in ex_refs]) if epi is not None else (acc,)
            for r, o in zip(out_refs, outs):
                r[...] = o.astype(r.dtype)

        if nk == 1:
            finish(part)
        else:
            acc_ref = refs[-1]
            k = pl.program_id(2)

            @pl.when(k == 0)
            def _():
                acc_ref[...] = part

            @pl.when(k > 0)
            def _():
                acc_ref[...] += part

            @pl.when(k == nk - 1)
            def _():
                finish(acc_ref[...])

    in_specs = [pl.BlockSpec((tm, tk), lambda i, j, k: (i, k)),
                pl.BlockSpec((tk, tn), lambda i, j, k: (k, j))]
    for arr, kind in extras:
        if kind == "mn":
            in_specs.append(pl.BlockSpec((tm, tn), lambda i, j, k: (i, j)))
        elif kind == "m":
            in_specs.append(pl.BlockSpec((tm, arr.shape[1]), lambda i, j, k: (i, 0)))
        else:
            in_specs.append(pl.BlockSpec((1, tn), lambda i, j, k: (0, j)))
    outs = pl.pallas_call(
        body, name=name, grid=(M // tm, N // tn, nk),
        in_specs=in_specs,
        out_specs=[pl.BlockSpec((tm, tn), lambda i, j, k: (i, j)) for _ in out_dtypes],
        out_shape=[jax.ShapeDtypeStruct((M, N), dt) for dt in out_dtypes],
        scratch_shapes=[pltpu.VMEM((tm, tn), F32)] if nk > 1 else [],
        compiler_params=_cparams("parallel", "parallel", "arbitrary"),
    )(a, b, *[arr for arr, _ in extras])
    return outs[0] if n_out == 1 else outs


def _mm_tn(name, a, b, *, t1=None, tn=None, ts=None):
    S, K1 = a.shape
    S2, N = b.shape
    assert S == S2, (name, a.shape, b.shape)
    t1 = min(t1 or TN_T1, K1)
    tn = min(tn or TN_TN, N)
    ts = min(ts or TN_TS, S)
    assert K1 % t1 == 0 and N % tn == 0 and S % ts == 0, (name, a.shape, b.shape)

    def body(a_ref, b_ref, o_ref):
        s = pl.program_id(2)
        part = lax.dot_general(a_ref[...].astype(BF16), b_ref[...].astype(BF16),
                               (((0,), (0,)), ((), ())), preferred_element_type=F32)

        @pl.when(s == 0)
        def _():
            o_ref[...] = part

        @pl.when(s > 0)
        def _():
            o_ref[...] += part

    return pl.pallas_call(
        body, name=name, grid=(K1 // t1, N // tn, S // ts),
        in_specs=[pl.BlockSpec((ts, t1), lambda i, j, s: (s, i)),
                  pl.BlockSpec((ts, tn), lambda i, j, s: (s, j))],
        out_specs=pl.BlockSpec((t1, tn), lambda i, j, s: (i, j)),
        out_shape=jax.ShapeDtypeStruct((K1, N), F32),
        compiler_params=_cparams("parallel", "parallel", "arbitrary"),
    )(a, b)


def _rows(arr, width=None, blk=0):
    return (arr, ("rows", arr.shape[1] if width is None else width, blk))


def _full(arr):
    return (arr, ("full",))


def _rowwise(name, fn, ins, outs, *, rows, tm=None):
    tm = min(tm or ROW_TM, rows)
    assert rows % tm == 0, (name, rows, tm)
    n_in = len(ins)

    def body(*refs):
        i = pl.program_id(0)
        vals = fn(*[r[...] for r in refs[:n_in]])
        for (shape, dt, kind), r, v in zip(outs, refs[n_in:], vals):
            if kind == "rows":
                r[...] = v.astype(dt)
            else:
                @pl.when(i == 0)
                def _(r=r, v=v):
                    r[...] = v

                @pl.when(i > 0)
                def _(r=r, v=v):
                    r[...] += v

    in_specs = []
    for arr, spec in ins:
        if spec[0] == "rows":
            in_specs.append(pl.BlockSpec((tm, spec[1]), lambda i, b=spec[2]: (i, b)))
        else:
            in_specs.append(pl.BlockSpec(arr.shape, lambda i, n=arr.ndim: (0,) * n))
    out_specs = []
    for shape, dt, kind in outs:
        if kind == "rows":
            out_specs.append(pl.BlockSpec((tm, shape[1]), lambda i: (i, 0)))
        else:
            out_specs.append(pl.BlockSpec(shape, lambda i, n=len(shape): (0,) * n))
    res = pl.pallas_call(
        body, name=name, grid=(rows // tm,),
        in_specs=in_specs, out_specs=out_specs,
        out_shape=[jax.ShapeDtypeStruct(shape, dt) for shape, dt, _ in outs],
        compiler_params=_cparams("arbitrary"),
    )(*[arr for arr, _ in ins])
    return res


MEM_SCALE = MEM_HD ** -0.5
MEM_Q0 = 2
NT_DIMS = (((1,), (1,)), ((), ()))


def _head_lanes(h):
    return slice(h * HEAD_PAD, (h + 1) * HEAD_PAD)


def _mem_fwd(name, proj_b, kvm, *, tq, chunk):
    S = proj_b.shape[0]
    tq = min(tq, S)
    C = min(chunk, tq)
    tiles = [(h, c) for c in range(tq // C) for h in range(MEM_HEADS)]

    def body(q_ref, kv_ref, o_ref, lse_ref):
        def logits(h, c):
            return lax.dot_general(q_ref[c * C:(c + 1) * C, _head_lanes(h)], kv_ref[:, _head_lanes(h)], NT_DIMS,
                                   preferred_element_type=F32) * MEM_SCALE

        nxt = logits(*tiles[0])
        for n, (h, c) in enumerate(tiles):
            s = nxt
            if n + 1 < len(tiles):
                nxt = logits(*tiles[n + 1])
            rows = slice(c * C, (c + 1) * C)
            m = jnp.max(s, axis=1, keepdims=True)
            p = jnp.exp(s - m)
            l = jnp.sum(p, axis=1, keepdims=True)
            o = jnp.dot(p.astype(BF16), kv_ref[:, _head_lanes(MEM_HEADS + h)], preferred_element_type=F32) / l
            o_ref[rows, _head_lanes(h)] = o.astype(o_ref.dtype)
            lse_ref[h, rows, :] = m + jnp.log(l)

    return pl.pallas_call(
        body, name=name, grid=(S // tq,),
        in_specs=[pl.BlockSpec((tq, MEM_HEADS * HEAD_PAD), lambda i: (i, MEM_Q0)),
                  pl.BlockSpec(kvm.shape, lambda i: (0, 0))],
        out_specs=[pl.BlockSpec((tq, MEM_HEADS * HEAD_PAD), lambda i: (i, 0)),
                   pl.BlockSpec((MEM_HEADS, tq, 1), lambda i: (0, i, 0))],
        out_shape=[jax.ShapeDtypeStruct((S, MEM_HEADS * HEAD_PAD), BF16),
                   jax.ShapeDtypeStruct((MEM_HEADS, S, 1), F32)],
        compiler_params=_cparams("parallel"),
    )(proj_b, kvm)


def _mem_bwd(name, proj_b, kvm, o, do, lse, *, tq, chunk):
    S = proj_b.shape[0]
    tq = min(tq, S)
    C = min(chunk, tq)
    nq = S // tq
    tiles = [(h, c) for c in range(tq // C) for h in range(MEM_HEADS)]

    def body(q_ref, kv_ref, o_ref, do_ref, lse_ref, dq_ref, dkv_ref, acc_sc):
        i = pl.program_id(0)

        @pl.when(i == 0)
        def _():
            acc_sc[...] = jnp.zeros(acc_sc.shape, F32)

        def mats(h, c):
            rows = slice(c * C, (c + 1) * C)
            q = q_ref[rows, _head_lanes(h)]
            dov = do_ref[rows, _head_lanes(h)]
            s = lax.dot_general(q, kv_ref[:, _head_lanes(h)], NT_DIMS, preferred_element_type=F32) * MEM_SCALE
            dp = lax.dot_general(dov, kv_ref[:, _head_lanes(MEM_HEADS + h)], NT_DIMS, preferred_element_type=F32)
            return q, dov, s, dp

        nxt = mats(*tiles[0])
        for n, (h, c) in enumerate(tiles):
            q, dov, s, dp = nxt
            if n + 1 < len(tiles):
                nxt = mats(*tiles[n + 1])
            rows = slice(c * C, (c + 1) * C)
            p = jnp.exp(s - lse_ref[h, rows, :])
            delta = jnp.sum(dov.astype(F32) * o_ref[rows, _head_lanes(h)].astype(F32), axis=1, keepdims=True)
            ds = (p * (dp - delta) * MEM_SCALE).astype(BF16)
            dq_ref[rows, _head_lanes(h)] = jnp.dot(ds, kv_ref[:, _head_lanes(h)],
                                                   preferred_element_type=F32).astype(dq_ref.dtype)
            acc_sc[_head_lanes(h), :] += jnp.dot(q.T, ds, preferred_element_type=F32)
            acc_sc[_head_lanes(MEM_HEADS + h), :] += jnp.dot(dov.T, p.astype(BF16), preferred_element_type=F32)

        @pl.when(i == nq - 1)
        def _():
            dkv_ref[...] = acc_sc[...].T

    qblk = pl.BlockSpec((tq, MEM_HEADS * HEAD_PAD), lambda i: (i, 0))
    return pl.pallas_call(
        body, name=name, grid=(nq,),
        in_specs=[pl.BlockSpec((tq, MEM_HEADS * HEAD_PAD), lambda i: (i, MEM_Q0)),
                  pl.BlockSpec(kvm.shape, lambda i: (0, 0)), qblk, qblk,
                  pl.BlockSpec((MEM_HEADS, tq, 1), lambda i: (0, i, 0))],
        out_specs=[qblk, pl.BlockSpec(kvm.shape, lambda i: (0, 0))],
        out_shape=[jax.ShapeDtypeStruct((S, MEM_HEADS * HEAD_PAD), BF16), jax.ShapeDtypeStruct(kvm.shape, F32)],
        scratch_shapes=[pltpu.VMEM((kvm.shape[1], kvm.shape[0]), F32)],
        compiler_params=_cparams("arbitrary"),
    )(proj_b, kvm, o, do, lse)


def _causal_fwd(name, q_arr, k_arr, v_arr, *, heads, tile, chunk):
    S = q_arr.shape[0]
    T = min(tile, S)
    C = min(chunk, T)
    nt = S // T
    nc = T // C

    pairs = [(qi, kk) for qi in range(nt) for kk in range(qi + 1)]
    q_tab = jnp.asarray([p[0] for p in pairs], jnp.int32)
    k_tab = jnp.asarray([p[1] for p in pairs], jnp.int32)

    def body(qt_ref, kt_ref, q_ref, k_ref, v_ref, o_ref, lse_ref, m_sc, acc_sc):
        t = pl.program_id(1)
        qi = qt_ref[t]
        kk = kt_ref[t]

        @pl.when(kk == 0)
        def _():
            m_sc[...] = jnp.full(m_sc.shape, NEG, F32)
            acc_sc[...] = jnp.zeros(acc_sc.shape, F32)

        def logits(c, ncols, masked):
            s = lax.dot_general(q_ref[pl.ds(c * C, C), :], k_ref[0:ncols, :], (((1,), (1,)), ((), ())),
                                preferred_element_type=F32)
            if masked:
                r = c * C + lax.broadcasted_iota(jnp.int32, (C, ncols), 0)
                cidx = lax.broadcasted_iota(jnp.int32, (C, ncols), 1)
                s = jnp.where(cidx <= r, s, NEG)
            return s

        def update(c, ncols, s):
            rows = pl.ds(c * C, C)
            m_prev = m_sc[rows, :]
            m_new = jnp.maximum(m_prev, jnp.max(s, axis=1, keepdims=True))
            p = jnp.exp2(s - m_new).astype(BF16)
            acc_sc[rows, :] = jnp.exp2(m_prev - m_new) * acc_sc[rows, :] + jnp.dot(
                p, v_ref[0:ncols, :], preferred_element_type=F32)
            m_sc[rows, :] = m_new

        def tile_step(ncols_of, masked):
            s = logits(0, ncols_of(0), masked)
            for c in range(nc):
                s_next = logits(c + 1, ncols_of(c + 1), masked) if c + 1 < nc else None
                update(c, ncols_of(c), s)
                s = s_next

        @pl.when(kk < qi)
        def _():
            tile_step(lambda c: T, False)

        @pl.when(kk == qi)
        def _():
            tile_step(lambda c: (c + 1) * C, True)

        @pl.when(kk == qi)
        def _():
            acc = acc_sc[...]
            l = acc[:, DEN_LANE:DEN_LANE + 1]
            o_ref[...] = (acc / l).astype(o_ref.dtype)
            lse_ref[0] = m_sc[...] + jnp.log2(l)

    grid_spec = pltpu.PrefetchScalarGridSpec(
        num_scalar_prefetch=2, grid=(heads, len(pairs)),
        in_specs=[pl.BlockSpec((T, HEAD_PAD), lambda h, t, qt, kt: (qt[t], h)),
                  pl.BlockSpec((T, HEAD_PAD), lambda h, t, qt, kt: (kt[t], h)),
                  pl.BlockSpec((T, HEAD_PAD), lambda h, t, qt, kt: (kt[t], h))],
        out_specs=[pl.BlockSpec((T, HEAD_PAD), lambda h, t, qt, kt: (qt[t], h)),
                   pl.BlockSpec((1, T, 1), lambda h, t, qt, kt: (h, qt[t], 0))],
        scratch_shapes=[pltpu.VMEM((T, 1), F32), pltpu.VMEM((T, HEAD_PAD), F32)])
    return pl.pallas_call(
        body, name=name, grid_spec=grid_spec,
        out_shape=[jax.ShapeDtypeStruct((S, heads * HEAD_PAD), BF16),
                   jax.ShapeDtypeStruct((heads, S, 1), F32)],
        compiler_params=_cparams("parallel", "arbitrary"),
    )(q_tab, k_tab, q_arr, k_arr, v_arr)


def _row_dot(name, a, b, *, heads, tm):
    S = a.shape[0]
    tm = min(tm, S)

    def body(a_ref, b_ref, o_ref):
        o_ref[0] = jnp.sum(a_ref[...].astype(F32) * b_ref[...].astype(F32), axis=1, keepdims=True)

    return pl.pallas_call(
        body, name=name, grid=(heads, S // tm),
        in_specs=[pl.BlockSpec((tm, HEAD_PAD), lambda h, i: (i, h)),
                  pl.BlockSpec((tm, HEAD_PAD), lambda h, i: (i, h))],
        out_specs=pl.BlockSpec((1, tm, 1), lambda h, i: (h, i, 0)),
        out_shape=jax.ShapeDtypeStruct((heads, S, 1), F32),
        compiler_params=_cparams("parallel", "parallel"),
    )(a, b)


def _causal_bwd(name, q_arr, k_arr, v_arr, do_arr, lse, delta, *, heads, tile, chunk):
    S = q_arr.shape[0]
    T = min(tile, S)
    C = min(chunk, T)
    nt = S // T
    nc = T // C

    pairs = [(kj, qq) for kj in range(nt) for qq in range(kj, nt)]
    k_tab = jnp.asarray([p[0] for p in pairs], jnp.int32)
    q_tab = jnp.asarray([p[1] for p in pairs], jnp.int32)

    def body(kt_ref, qt_ref, q_ref, k_ref, v_ref, do_ref, lse_ref, delta_ref, dq_ref, dk_ref, dv_ref, dk_sc, dv_sc):
        t = pl.program_id(1)
        kj = kt_ref[t]
        qq = qt_ref[t]
        qb = qq

        @pl.when(t == 0)
        def _():
            dq_ref[...] = jnp.zeros(dq_ref.shape, F32)

        @pl.when(qq == kj)
        def _():
            dk_sc[...] = jnp.zeros(dk_sc.shape, F32)
            dv_sc[...] = jnp.zeros(dv_sc.shape, F32)

        def logits(c, ncols, masked):
            rows = pl.ds(c * C, C)
            s = lax.dot_general(q_ref[rows, :], k_ref[0:ncols, :], (((1,), (1,)), ((), ())),
                                preferred_element_type=F32)
            if masked:
                r = c * C + lax.broadcasted_iota(jnp.int32, (C, ncols), 0)
                cidx = lax.broadcasted_iota(jnp.int32, (C, ncols), 1)
                s = jnp.where(cidx <= r, s, NEG)
            dp = lax.dot_general(do_ref[rows, :], v_ref[0:ncols, :], (((1,), (1,)), ((), ())),
                                 preferred_element_type=F32)
            return s, dp

        def update(c, ncols, s, dp):
            rows = pl.ds(c * C, C)
            p = jnp.exp2(s - lse_ref[0, rows, :])
            ds = (p * (dp - delta_ref[0, rows, :])).astype(BF16)
            dv_sc[:, 0:ncols] += jnp.dot(do_ref[rows, :].T, p.astype(BF16), preferred_element_type=F32)
            dk_sc[:, 0:ncols] += jnp.dot(q_ref[rows, :].T, ds, preferred_element_type=F32)
            row0 = pl.multiple_of(qb * T + c * C, C)
            dq_ref[pl.ds(row0, C), :] += jnp.dot(ds, k_ref[0:ncols, :], preferred_element_type=F32)

        def tile_step(ncols_of, masked):
            cur = logits(0, ncols_of(0), masked)
            for c in range(nc):
                nxt = logits(c + 1, ncols_of(c + 1), masked) if c + 1 < nc else None
                update(c, ncols_of(c), *cur)
                cur = nxt

        @pl.when(qq > kj)
        def _():
            tile_step(lambda c: T, False)

        @pl.when(qq == kj)
        def _():
            tile_step(lambda c: (c + 1) * C, True)

        @pl.when(qq == nt - 1)
        def _():
            dk_ref[...] = dk_sc[...].T * math.log(2.0)
            dv_ref[...] = dv_sc[...].T

    qrow = pl.BlockSpec((T, HEAD_PAD), lambda h, t, kt, qt: (qt[t], h))
    krow = pl.BlockSpec((T, HEAD_PAD), lambda h, t, kt, qt: (kt[t], h))
    qcol = pl.BlockSpec((1, T, 1), lambda h, t, kt, qt: (h, qt[t], 0))
    grid_spec = pltpu.PrefetchScalarGridSpec(
        num_scalar_prefetch=2, grid=(heads, len(pairs)),
        in_specs=[qrow, krow, krow, qrow, qcol, qcol],
        out_specs=[pl.BlockSpec((S, HEAD_PAD), lambda h, t, kt, qt: (0, h)), krow, krow],
        scratch_shapes=[pltpu.VMEM((HEAD_PAD, T), F32), pltpu.VMEM((HEAD_PAD, T), F32)])
    return pl.pallas_call(
        body, name=name, grid_spec=grid_spec,
        out_shape=[jax.ShapeDtypeStruct((S, heads * HEAD_PAD), F32)] * 3,
        compiler_params=_cparams("arbitrary", "arbitrary"),
    )(k_tab, q_tab, q_arr, k_arr, v_arr, do_arr, lse, delta)


SWA_R = SWA_HEADS // SWA_KV_HEADS
SWA_SCALE = SWA_HD ** -0.5
SWA_Q0, SWA_K0, SWA_V0 = 0, 12, 14


def _swa_specs(tq):
    nsb = tq // WINDOW
    return [
        pl.BlockSpec((tq, SWA_R * HEAD_PAD), lambda g, i: (i, g)),
        pl.BlockSpec((tq, HEAD_PAD), lambda g, i: (i, SWA_K0 + g)),
        pl.BlockSpec((WINDOW, HEAD_PAD), lambda g, i: (jnp.maximum(nsb * i - 1, 0), SWA_K0 + g)),
        pl.BlockSpec((tq, HEAD_PAD), lambda g, i: (i, SWA_V0 + g)),
        pl.BlockSpec((WINDOW, HEAD_PAD), lambda g, i: (jnp.maximum(nsb * i - 1, 0), SWA_V0 + g)),
        pl.BlockSpec((SWA_R, WINDOW, 2 * WINDOW), lambda g, i: (g, 0, 0)),
        pl.BlockSpec((SWA_R, 8, 128), lambda g, i: (g, 0, 0)),
    ]


def _swa_block(i, sb, q_ref, kc_ref, kp_ref, vc_ref, vp_ref, bias, sink):
    rows = slice(sb * WINDOW, (sb + 1) * WINDOW)
    qs = jnp.concatenate([q_ref[rows, hh * HEAD_PAD:(hh + 1) * HEAD_PAD] for hh in range(SWA_R)], axis=0)
    if sb == 0:
        kp, vp = kp_ref[...], vp_ref[...]
    else:
        prev = slice((sb - 1) * WINDOW, sb * WINDOW)
        kp, vp = kc_ref[prev, :], vc_ref[prev, :]
    kk = jnp.concatenate([kp, kc_ref[rows, :]], axis=0)
    vv = jnp.concatenate([vp, vc_ref[rows, :]], axis=0)
    s = lax.dot_general(qs, kk, (((1,), (1,)), ((), ())), preferred_element_type=F32) * SWA_SCALE + bias
    if sb == 0:
        col = lax.broadcasted_iota(jnp.int32, (1, 2 * WINDOW), 1)
        s = s + jnp.where((col < WINDOW) & (i == 0), NEG, 0.0)
    return rows, qs, kk, vv, s


def _stack_heads(ref, rows, lead=None):
    if lead is None:
        return jnp.concatenate([ref[rows, hh * HEAD_PAD:(hh + 1) * HEAD_PAD] for hh in range(SWA_R)], axis=0)
    return jnp.concatenate([ref[hh, rows, :] for hh in range(SWA_R)], axis=0)


def _swa_fwd(name, proj_b, bias, sinks, *, tq):
    S = proj_b.shape[0]
    tq = min(tq, S)
    nsb = tq // WINDOW

    def body(q_ref, kc_ref, kp_ref, vc_ref, vp_ref, bias_ref, sink_ref, o_ref, lse_ref):
        i = pl.program_id(1)
        bias_v = bias_ref[...].reshape(SWA_R * WINDOW, 2 * WINDOW)
        sink = jnp.concatenate([jnp.zeros((WINDOW, 1), F32) + sink_ref[hh, 0:1, 0:1] for hh in range(SWA_R)], axis=0)
        nxt = _swa_block(i, 0, q_ref, kc_ref, kp_ref, vc_ref, vp_ref, bias_v, sink)
        for sb in range(nsb):
            rows, _, _, vv, s = nxt
            if sb + 1 < nsb:
                nxt = _swa_block(i, sb + 1, q_ref, kc_ref, kp_ref, vc_ref, vp_ref, bias_v, sink)
            m = jnp.maximum(jnp.max(s, axis=1, keepdims=True), sink)
            p = jnp.exp(s - m)
            l = jnp.sum(p, axis=1, keepdims=True) + jnp.exp(sink - m)
            o = jnp.dot(p.astype(BF16), vv, preferred_element_type=F32) / l
            lse_v = m + jnp.log(l)
            for hh in range(SWA_R):
                o_ref[rows, hh * HEAD_PAD:(hh + 1) * HEAD_PAD] = o[hh * WINDOW:(hh + 1) * WINDOW].astype(o_ref.dtype)
                lse_ref[hh, rows, :] = lse_v[hh * WINDOW:(hh + 1) * WINDOW]

    return pl.pallas_call(
        body, name=name, grid=(SWA_KV_HEADS, S // tq),
        in_specs=_swa_specs(tq),
        out_specs=[pl.BlockSpec((tq, SWA_R * HEAD_PAD), lambda g, i: (i, g)),
                   pl.BlockSpec((SWA_R, tq, 1), lambda g, i: (g, i, 0))],
        out_shape=[jax.ShapeDtypeStruct((S, SWA_HEADS * HEAD_PAD), BF16),
                   jax.ShapeDtypeStruct((SWA_HEADS, S, 1), F32)],
        compiler_params=_cparams("parallel", "parallel"),
    )(proj_b, proj_b, proj_b, proj_b, proj_b, bias, sinks)


def _swa_bwd(name, proj_b, bias, sinks, o, do, lse, *, tq):
    S = proj_b.shape[0]
    tq = min(tq, S)
    nsb = tq // WINDOW
    nq = S // tq

    def body(q_ref, kc_ref, kp_ref, vc_ref, vp_ref, bias_ref, sink_ref, o_ref, do_ref, lse_ref,
             dq_ref, dk_ref, dv_ref, dke_ref, dve_ref, dbias_ref, dsink_ref):
        i = pl.program_id(1)

        @pl.when(i == 0)
        def _():
            dbias_ref[...] = jnp.zeros(dbias_ref.shape, F32)
            dsink_ref[...] = jnp.zeros(dsink_ref.shape, F32)

        bias_v = bias_ref[...].reshape(SWA_R * WINDOW, 2 * WINDOW)
        sink = jnp.concatenate([jnp.zeros((WINDOW, 1), F32) + sink_ref[hh, 0:1, 0:1] for hh in range(SWA_R)], axis=0)
        dk_own, dv_own, dk_prev, dv_prev = [], [], [], []
        dbias_acc = jnp.zeros((SWA_R * WINDOW, 2 * WINDOW), F32)
        def block(sb):
            rows, qs, kk, vv, s = _swa_block(i, sb, q_ref, kc_ref, kp_ref, vc_ref, vp_ref, bias_v, sink)
            do_s = _stack_heads(do_ref, rows)
            dp = lax.dot_general(do_s, vv, (((1,), (1,)), ((), ())), preferred_element_type=F32)
            return rows, qs, kk, do_s, s, dp

        nxt = block(0)
        for sb in range(nsb):
            rows, qs, kk, do_s, s, dp = nxt
            if sb + 1 < nsb:
                nxt = block(sb + 1)
            lse_v = _stack_heads(lse_ref, rows, lead=True)
            delta = jnp.sum(do_s.astype(F32) * _stack_heads(o_ref, rows).astype(F32), axis=1, keepdims=True)
            p = jnp.exp(s - lse_v)
            dsp = p * (dp - delta)
            dbias_acc = dbias_acc + dsp
            ds = (dsp * SWA_SCALE).astype(BF16)
            dq = jnp.dot(ds, kk, preferred_element_type=F32)
            dkk = jnp.dot(qs.T, ds, preferred_element_type=F32)
            dvv = jnp.dot(do_s.T, p.astype(BF16), preferred_element_type=F32)
            dk_prev.append(dkk[:, :WINDOW].T)
            dk_own.append(dkk[:, WINDOW:].T)
            dv_prev.append(dvv[:, :WINDOW].T)
            dv_own.append(dvv[:, WINDOW:].T)
            psink = jnp.exp(sink - lse_v) * delta
            for hh in range(SWA_R):
                hrows = slice(hh * WINDOW, (hh + 1) * WINDOW)
                dq_ref[rows, hh * HEAD_PAD:(hh + 1) * HEAD_PAD] = dq[hrows].astype(dq_ref.dtype)
                dsink_ref[hh] += jnp.zeros((8, 128), F32) - jnp.sum(psink[hrows])
        dbias_ref[...] += dbias_acc.reshape(SWA_R, WINDOW, 2 * WINDOW)
        for sb in range(nsb):
            rows = slice(sb * WINDOW, (sb + 1) * WINDOW)
            if sb + 1 < nsb:
                dk_ref[rows, :] = dk_own[sb] + dk_prev[sb + 1]
                dv_ref[rows, :] = dv_own[sb] + dv_prev[sb + 1]
            else:
                dk_ref[rows, :] = dk_own[sb]
                dv_ref[rows, :] = dv_own[sb]
        dke_ref[...] = dk_prev[0]
        dve_ref[...] = dv_prev[0]

    in_specs = _swa_specs(tq) + [
        pl.BlockSpec((tq, SWA_R * HEAD_PAD), lambda g, i: (i, g)),
        pl.BlockSpec((tq, SWA_R * HEAD_PAD), lambda g, i: (i, g)),
        pl.BlockSpec((SWA_R, tq, 1), lambda g, i: (g, i, 0)),
    ]
    kv_blk = pl.BlockSpec((tq, HEAD_PAD), lambda g, i: (i, g))
    edge_blk = pl.BlockSpec((WINDOW, HEAD_PAD), lambda g, i: (i, g))
    return pl.pallas_call(
        body, name=name, grid=(SWA_KV_HEADS, nq),
        in_specs=in_specs,
        out_specs=[pl.BlockSpec((tq, SWA_R * HEAD_PAD), lambda g, i: (i, g)), kv_blk, kv_blk, edge_blk, edge_blk,
                   pl.BlockSpec((SWA_R, WINDOW, 2 * WINDOW), lambda g, i: (g, 0, 0)),
                   pl.BlockSpec((SWA_R, 8, 128), lambda g, i: (g, 0, 0))],
        out_shape=[jax.ShapeDtypeStruct((S, SWA_HEADS * HEAD_PAD), BF16),
                   jax.ShapeDtypeStruct((S, SWA_KV_HEADS * HEAD_PAD), F32),
                   jax.ShapeDtypeStruct((S, SWA_KV_HEADS * HEAD_PAD), F32),
                   jax.ShapeDtypeStruct((nq * WINDOW, SWA_KV_HEADS * HEAD_PAD), F32),
                   jax.ShapeDtypeStruct((nq * WINDOW, SWA_KV_HEADS * HEAD_PAD), F32),
                   jax.ShapeDtypeStruct((SWA_HEADS, WINDOW, 2 * WINDOW), F32),
                   jax.ShapeDtypeStruct((SWA_HEADS, 8, 128), F32)],
        compiler_params=_cparams("arbitrary", "arbitrary"),
    )(proj_b, proj_b, proj_b, proj_b, proj_b, bias, sinks, o, do, lse)


def _dproj_b(name, dq_swa, dq_mem, dk, dv, dk_edge, dv_edge, *, tq):
    S = dq_swa.shape[0]
    tq = min(tq, S)
    nq = S // tq

    def body(dqs_ref, dqm_ref, dk_ref, dv_ref, dke_ref, dve_ref, o_ref):
        i = pl.program_id(0)
        o_ref[:, 0:1024] = dqs_ref[...]
        o_ref[:, 1024:1536] = dqm_ref[...].astype(o_ref.dtype)
        o_ref[:, 1536:1792] = dk_ref[...].astype(o_ref.dtype)
        o_ref[:, 1792:2048] = dv_ref[...].astype(o_ref.dtype)

        @pl.when(i < nq - 1)
        def _():
            last = slice(tq - WINDOW, tq)
            o_ref[last, 1536:1792] = (dk_ref[last, :] + dke_ref[...]).astype(o_ref.dtype)
            o_ref[last, 1792:2048] = (dv_ref[last, :] + dve_ref[...]).astype(o_ref.dtype)

    edge = pl.BlockSpec((WINDOW, SWA_KV_HEADS * HEAD_PAD), lambda i: (jnp.minimum(i + 1, nq - 1), 0))
    return pl.pallas_call(
        body, name=name, grid=(nq,),
        in_specs=[pl.BlockSpec((tq, 1024), lambda i: (i, 0)), pl.BlockSpec((tq, 512), lambda i: (i, 0)),
                  pl.BlockSpec((tq, 256), lambda i: (i, 0)), pl.BlockSpec((tq, 256), lambda i: (i, 0)), edge, edge],
        out_specs=pl.BlockSpec((tq, 2048), lambda i: (i, 0)),
        out_shape=jax.ShapeDtypeStruct((S, 2048), BF16),
        compiler_params=_cparams("parallel"),
    )(dq_swa, dq_mem, dk, dv, dk_edge, dv_edge)


def _exchange(name, send, *, per_peer):
    shape = send.shape[1:] if per_peer else send.shape

    def body(send_ref, recv_ref, send_sems, recv_sems, local_sem):
        x, y, c = lax.axis_index("x"), lax.axis_index("y"), lax.axis_index("c")
        me = 4 * x + 2 * y + c
        own = pltpu.make_async_copy(send_ref.at[me] if per_peer else send_ref, recv_ref.at[me], local_sem)
        own.start()
        copies = []
        for k in range(1, N_DEV):
            px = 1 - x if (k >> 2) & 1 else x
            py = 1 - y if (k >> 1) & 1 else y
            pc = 1 - c if k & 1 else c
            peer = 4 * px + 2 * py + pc
            out = pltpu.make_async_remote_copy(
                src_ref=send_ref.at[peer] if per_peer else send_ref, dst_ref=recv_ref.at[me],
                send_sem=send_sems.at[k - 1], recv_sem=recv_sems.at[k - 1],
                device_id=(px, py, pc), device_id_type=pl.DeviceIdType.MESH)
            out.start()
            back = pltpu.make_async_remote_copy(
                src_ref=send_ref.at[me] if per_peer else send_ref, dst_ref=recv_ref.at[peer],
                send_sem=send_sems.at[k - 1], recv_sem=recv_sems.at[k - 1],
                device_id=(px, py, pc), device_id_type=pl.DeviceIdType.MESH)
            copies.append((out, back))
        for out, back in copies:
            out.wait_send()
            back.wait_recv()
        own.wait()

    return pl.pallas_call(
        body, name=name,
        in_specs=[pl.BlockSpec(memory_space=pl.ANY)],
        out_specs=pl.BlockSpec(memory_space=pl.ANY),
        out_shape=jax.ShapeDtypeStruct((N_DEV,) + tuple(shape), send.dtype),
        scratch_shapes=[pltpu.SemaphoreType.DMA((N_DEV - 1,)), pltpu.SemaphoreType.DMA((N_DEV - 1,)),
                        pltpu.SemaphoreType.DMA(())],
    )(send)


def _gather_forwarded(name, block):
    def body(x_ref, out_ref, send_sems, recv_sems, local_sem):
        x, y, c = lax.axis_index("x"), lax.axis_index("y"), lax.axis_index("c")
        me, sibling = (x, y, c), (x, y, 1 - c)
        chips = [(1 - x, y), (x, 1 - y), (1 - x, 1 - y)]

        def slot(px, py, pc):
            return out_ref.at[4 * px + 2 * py + pc]

        def copy(k, blk, to, src=None):
            return pltpu.make_async_remote_copy(
                src_ref=slot(*blk) if src is None else src, dst_ref=slot(*blk),
                send_sem=send_sems.at[k], recv_sem=recv_sems.at[k],
                device_id=to, device_id_type=pl.DeviceIdType.MESH)

        mine = pltpu.make_async_copy(x_ref, slot(*me), local_sem)
        mine.start()
        first = [copy(0, me, sibling, src=x_ref)]
        first += [copy(1 + j, me, (*chip, c), src=x_ref) for j, chip in enumerate(chips)]
        for cp in first:
            cp.start()
        passed = [copy(4 + j, (*chip, c), sibling) for j, chip in enumerate(chips)]
        for j, chip in enumerate(chips):
            copy(1 + j, (*chip, c), me).wait_recv()
            passed[j].start()
        copy(0, sibling, me).wait_recv()
        for j, chip in enumerate(chips):
            copy(4 + j, (*chip, 1 - c), me).wait_recv()
        for cp in first + passed:
            cp.wait_send()
        mine.wait()

    return pl.pallas_call(
        body, name=name,
        in_specs=[pl.BlockSpec(memory_space=pl.ANY)],
        out_specs=pl.BlockSpec(memory_space=pl.ANY),
        out_shape=jax.ShapeDtypeStruct((N_DEV,) + tuple(block.shape), block.dtype),
        scratch_shapes=[pltpu.SemaphoreType.DMA((N_DEV - 1,)), pltpu.SemaphoreType.DMA((N_DEV - 1,)),
                        pltpu.SemaphoreType.DMA(())],
    )(block)


def _sibling_swap(name, block):
    def body(x_ref, out_ref, send_sem, recv_sem):
        x, y, c = lax.axis_index("x"), lax.axis_index("y"), lax.axis_index("c")
        cp = pltpu.make_async_remote_copy(src_ref=x_ref, dst_ref=out_ref, send_sem=send_sem, recv_sem=recv_sem,
                                          device_id=(x, y, 1 - c), device_id_type=pl.DeviceIdType.MESH)
        cp.start()
        cp.wait()

    return pl.pallas_call(
        body, name=name,
        in_specs=[pl.BlockSpec(memory_space=pl.ANY)],
        out_specs=pl.BlockSpec(memory_space=pl.ANY),
        out_shape=jax.ShapeDtypeStruct(block.shape, block.dtype),
        scratch_shapes=[pltpu.SemaphoreType.DMA(()), pltpu.SemaphoreType.DMA(())],
    )(block)


def _chip_exchange(name, send):
    def body(send_ref, recv_ref, send_sems, recv_sems, local_sem):
        x, y, c = lax.axis_index("x"), lax.axis_index("y"), lax.axis_index("c")
        me = 2 * x + y
        own = pltpu.make_async_copy(send_ref.at[me], recv_ref.at[me], local_sem)
        own.start()
        copies = []
        for k in range(1, 4):
            px = 1 - x if (k >> 1) & 1 else x
            py = 1 - y if k & 1 else y
            peer = 2 * px + py
            out = pltpu.make_async_remote_copy(
                src_ref=send_ref.at[peer], dst_ref=recv_ref.at[me],
                send_sem=send_sems.at[k - 1], recv_sem=recv_sems.at[k - 1],
                device_id=(px, py, c), device_id_type=pl.DeviceIdType.MESH)
            out.start()
            back = pltpu.make_async_remote_copy(
                src_ref=send_ref.at[me], dst_ref=recv_ref.at[peer],
                send_sem=send_sems.at[k - 1], recv_sem=recv_sems.at[k - 1],
                device_id=(px, py, c), device_id_type=pl.DeviceIdType.MESH)
            copies.append((out, back))
        for out, back in copies:
            out.wait_send()
            back.wait_recv()
        own.wait()

    return pl.pallas_call(
        body, name=name,
        in_specs=[pl.BlockSpec(memory_space=pl.ANY)],
        out_specs=pl.BlockSpec(memory_space=pl.ANY),
        out_shape=jax.ShapeDtypeStruct(send.shape, send.dtype),
        scratch_shapes=[pltpu.SemaphoreType.DMA((3,)), pltpu.SemaphoreType.DMA((3,)), pltpu.SemaphoreType.DMA(())],
    )(send)


def _reduce_scatter(send):
    n, rows, lanes = send.shape
    c = lax.axis_index("c")
    by_core = send.reshape(4, 2, rows, lanes)
    mine = lax.dynamic_index_in_dim(by_core, c, axis=1, keepdims=False).reshape(4 * rows, lanes)
    theirs = lax.dynamic_index_in_dim(by_core, 1 - c, axis=1, keepdims=False).reshape(4 * rows, lanes)
    from_sibling = _sibling_swap("grads_to_sibling", theirs)
    tm = max(t for t in range(16, ADAM_TM + 1, 16) if rows % t == 0)
    chip_sum = _rowwise("grads_chip_sum", lambda a, b: (a.astype(F32) + b.astype(F32),),
                        [_rows(mine), _rows(from_sibling)], [((4 * rows, lanes), BF16, "rows")],
                        rows=4 * rows, tm=tm)[0]
    return _chip_exchange("scatter_grads", chip_sum.reshape(4, rows, lanes))


def _adam(name, recv, w, m, v, *, tm=None):
    R = w.shape[0]
    n_parts = recv.shape[0]
    tm = max(t for t in range(8, min(tm or ADAM_TM, R) + 1, 8) if R % t == 0)
    c1 = 1.0 / (1.0 - ADAM_B1 ** ADAM_STEP)
    c2 = 1.0 / (1.0 - ADAM_B2 ** ADAM_STEP)

    def body(r_ref, w_ref, m_ref, v_ref, g_ref, d_ref, nm_ref, nv_ref):
        g = r_ref[0].astype(F32)
        for j in range(1, n_parts):
            g = g + r_ref[j].astype(F32)
        wv = w_ref[...]
        nm = ADAM_B1 * m_ref[...] + (1.0 - ADAM_B1) * g
        nv = ADAM_B2 * v_ref[...] + (1.0 - ADAM_B2) * (g * g)
        m_hat = nm * c1
        v_hat = nv * c2
        g_ref[...] = g
        d_ref[...] = -ADAM_LR * (m_hat / (jnp.sqrt(v_hat) + ADAM_EPS) + ADAM_WD * wv)
        nm_ref[...] = nm
        nv_ref[...] = nv

    row = pl.BlockSpec((tm, 128), lambda i: (i, 0))
    return pl.pallas_call(
        body, name=name, grid=(R // tm,),
        in_specs=[pl.BlockSpec((n_parts, tm, 128), lambda i: (0, i, 0)), row, row, row],
        out_specs=[row, row, row, row],
        out_shape=[jax.ShapeDtypeStruct((R, 128), F32)] * 4,
        compiler_params=_cparams("parallel"),
    )(recv, w, m, v)


def _pack_rows(arrs):
    return jnp.concatenate([a.reshape(-1, 128) for a in arrs], axis=0)


def _unstack(g, shape, axis):
    t = jnp.moveaxis(g, 0, axis)
    return t.reshape(shape)


def _restack(full, axis):
    s = full.shape
    t = full.reshape(s[:axis] + (N_DEV, s[axis] // N_DEV) + s[axis + 1:])
    return jnp.moveaxis(t, axis, 0)


def _pad_heads(w, heads, hd, axis):
    s = w.shape
    t = w.reshape(s[:axis] + (heads, hd) + s[axis + 1:])
    pad = [(0, 0)] * t.ndim
    pad[axis + 1] = (0, HEAD_PAD - hd)
    t = jnp.pad(t, pad)
    return t.reshape(s[:axis] + (heads * HEAD_PAD,) + s[axis + 1:])


def _unpad_heads(w, heads, hd, axis):
    s = w.shape
    t = w.reshape(s[:axis] + (heads, HEAD_PAD) + s[axis + 1:])
    t = lax.slice_in_dim(t, 0, hd, axis=axis + 1)
    return t.reshape(s[:axis] + (heads * hd,) + s[axis + 1:])


def _layer_weights(full, l):
    w_in = full["w_in"][l]
    cq, kva, qs, ks, vs, qm, gates = (w_in[:, 0:256], w_in[:, 256:416], w_in[:, 416:928], w_in[:, 928:1056],
                                       w_in[:, 1056:1184], w_in[:, 1184:1696], w_in[:, 1696:4768])
    wa = jnp.concatenate([gates, cq, jnp.pad(kva, ((0, 0), (0, 96)))], axis=1)
    wb = jnp.concatenate([_pad_heads(qs, SWA_HEADS, SWA_HD, 1), qm, _pad_heads(ks, SWA_KV_HEADS, SWA_HD, 1),
                          _pad_heads(vs, SWA_KV_HEADS, SWA_HD, 1)], axis=1)
    wuq = _pad_heads(full["w_uq"][l], MLA_HEADS, MLA_NOPE + MLA_ROPE, 1)
    ukv = full["w_ukv"][l].reshape(MLA_KV_LORA, MLA_HEADS, MLA_NOPE + MLA_V)
    wuk = _pad_heads(ukv[:, :, :MLA_NOPE].reshape(MLA_KV_LORA, -1), MLA_HEADS, MLA_NOPE, 1)
    wuv = _pad_heads(ukv[:, :, MLA_NOPE:].reshape(MLA_KV_LORA, -1), MLA_HEADS, MLA_V, 1)
    wo_mla = _pad_heads(full["w_o_mla"][l], MLA_HEADS, MLA_V, 0)
    wo_swa = _pad_heads(full["w_o_swa"][l], SWA_HEADS, SWA_HD, 0)
    wo_mem = full["w_o_mem"][l]
    w = dict(wag=wa[:, :3072], wat=wa[:, 3072:], wb=wb, wuq=wuq, wuk=wuk, wuv=wuv, wo_mla=wo_mla, wo_swa=wo_swa,
             wo_mem=wo_mem, wmem=full["w_mem_kv"][l], wout=full["w_out"][l], wup=full["w_up"][l],
             wdown=full["w_down"][l])
    w.update({k + "_t": v.T for k, v in w.items()})
    return w


def _layer_weight_grads(g):
    dwa_g, dwa_t, dwb = g["wag"], g["wat"], g["wb"]
    d_in = jnp.concatenate([
        dwa_t[:, 0:256], dwa_t[:, 256:416],
        _unpad_heads(dwb[:, 0:1024], SWA_HEADS, SWA_HD, 1),
        _unpad_heads(dwb[:, 1536:1792], SWA_KV_HEADS, SWA_HD, 1),
        _unpad_heads(dwb[:, 1792:2048], SWA_KV_HEADS, SWA_HD, 1),
        dwb[:, 1024:1536], dwa_g], axis=1)
    duk = _unpad_heads(g["wuk"], MLA_HEADS, MLA_NOPE, 1).reshape(MLA_KV_LORA, MLA_HEADS, MLA_NOPE)
    duv = _unpad_heads(g["wuv"], MLA_HEADS, MLA_V, 1).reshape(MLA_KV_LORA, MLA_HEADS, MLA_V)
    return dict(
        w_in=d_in,
        w_uq=_unpad_heads(g["wuq"], MLA_HEADS, MLA_NOPE + MLA_ROPE, 1),
        w_ukv=jnp.concatenate([duk, duv], axis=2).reshape(MLA_KV_LORA, -1),
        w_mem_kv=g["wmem"],
        w_o_mla=_unpad_heads(g["wo_mla"], MLA_HEADS, MLA_V, 0),
        w_o_swa=_unpad_heads(g["wo_swa"], SWA_HEADS, SWA_HD, 0),
        w_o_mem=g["wo_mem"], w_out=g["wout"], w_up=g["wup"], w_down=g["wdown"])


def _rope_tables(S):
    pos = jnp.arange(S, dtype=F32)
    inv = 1.0 / (ROPE_THETA ** (jnp.arange(0, MLA_ROPE, 2, dtype=F32) / MLA_ROPE))
    ang = pos[:, None] * inv[None, :]
    cos, sin = jnp.cos(ang), jnp.sin(ang)
    z16 = jnp.zeros((S, 16), F32)
    z32 = jnp.zeros((S, 32), F32)
    c = jnp.concatenate([jnp.ones((S, 64), F32), cos, cos, z32], axis=1)
    ck = jnp.concatenate([jnp.zeros((S, 64), F32), cos, cos, z32], axis=1)
    s1 = jnp.concatenate([jnp.zeros((S, 80), F32), sin, z32], axis=1)
    s2 = jnp.concatenate([jnp.zeros((S, 64), F32), -sin, z16, z32], axis=1)
    return c, ck, s1, s2


def _t5_bucket(dist):
    n = jnp.maximum(dist, 0)
    max_exact = REL_BUCKETS // 2
    nf = jnp.maximum(n, 1).astype(F32)
    large = max_exact + (jnp.log(nf / max_exact) / math.log(REL_MAX_DIST / max_exact)
                         * (REL_BUCKETS - max_exact)).astype(jnp.int32)
    large = jnp.minimum(large, REL_BUCKETS - 1)
    return jnp.where(n < max_exact, n, large)


def _bias_onehot():
    qi = jnp.arange(WINDOW)[:, None]
    kj = jnp.arange(2 * WINDOW)[None, :]
    dist = qi + WINDOW - kj
    valid = (dist >= 0) & (dist < WINDOW)
    bucket = _t5_bucket(dist)
    onehot = (bucket[None] == jnp.arange(REL_BUCKETS)[:, None, None]) & valid[None]
    return (onehot.reshape(REL_BUCKETS, -1).astype(F32),
            jnp.where(valid, 0.0, NEG).astype(F32).reshape(1, -1))


def _rstd(x):
    return lax.rsqrt(jnp.mean(x * x, axis=-1, keepdims=True) + EPS)


def _norm_bwd(dh, x, g):
    r = _rstd(x)
    xh = x * r
    w = dh * g
    dx = r * (w - xh * jnp.mean(w * xh, axis=-1, keepdims=True))
    return dx, jnp.sum(dh * xh, axis=0, keepdims=True)


def _tile_lanes(t, n):
    return jnp.tile(t, (1, n // t.shape[1])) if n != t.shape[1] else t


def _rope_fwd(a, c, s1, s2):
    n = a.shape[1]
    return (a * _tile_lanes(c, n) + pltpu.roll(a, 16, 1) * _tile_lanes(s1, n)
            + pltpu.roll(a, n - 16, 1) * _tile_lanes(s2, n))


def _rope_bwd(d, c, s1, s2):
    n = d.shape[1]
    return (d * _tile_lanes(c, n) + pltpu.roll(d * _tile_lanes(s1, n), n - 16, 1)
            + pltpu.roll(d * _tile_lanes(s2, n), 16, 1))


def _sigmoid(x):
    return 1.0 / (1.0 + jnp.exp(-x))


def _rmsnorm(name, x, g, dtype):
    def fn(xv, gv):
        return ((xv * _rstd(xv)) * gv,)
    return _rowwise(name, fn, [_rows(x), _full(g)], [(x.shape, dtype, "rows")], rows=x.shape[0])[0]


def _residual_norm_bwd(name, dres, dh, x, g):
    def fn(dr, dhv, xv, gv):
        dx, dg = _norm_bwd(dhv, xv, gv)
        return dr + dx, dg
    return _rowwise(name, fn, [_rows(dres), _rows(dh), _rows(x), _full(g)],
                    [(x.shape, F32, "rows"), (g.shape, F32, "acc")], rows=x.shape[0])


def _layer_fwd(l, x, mem, w, p, tabs, swa_bias, S):
    c, ck, s1, s2 = tabs
    n = f"l{l}_"
    h = _rmsnorm(n + "attn_norm", x, p["attn_norm"], BF16)
    gates = _mm(n + "proj_gates", h, w["wag"], [BF16], tn=1024)
    proj_a = _mm(n + "proj_tail", h, w["wat"], [F32])
    proj_b = _mm(n + "proj_b", h, w["wb"], [BF16])

    def prep(cq, kva, qn, kvn, ckv, s1v, s2v):
        cqn = cq * _rstd(cq) * qn
        ckv_ = kva[:, :128]
        ckvn = ckv_ * _rstd(ckv_) * kvn
        pe = pltpu.roll(kva[:, 128:], 64, 1)
        return cqn, ckvn, _rope_fwd(pe, ckv, s1v, s2v)

    cqn, ckvn, kpe = _rowwise(
        n + "mla_prep", prep,
        [_rows(proj_a, 256, 0), _rows(proj_a, 256, 1), _full(p["mla_q_norm"]), _full(p["mla_kv_norm"]),
         _rows(ck), _rows(s1), _rows(s2)],
        [((S, 256), BF16, "rows"), ((S, 128), BF16, "rows"), ((S, 128), F32, "rows")], rows=S)

    q_mla = _mm(n + "q_mla", cqn, w["wuq"], [BF16],
                epi=lambda acc, cv, s1v, s2v: (_rope_fwd(acc, cv, s1v, s2v) * (MLA_SCALE * LOG2E),),
                extras=[(c, "m"), (s1, "m"), (s2, "m")])
    k_mla = _mm(n + "k_mla", ckvn, w["wuk"], [BF16],
                epi=lambda acc, kp: (acc + _tile_lanes(kp, acc.shape[1]),), extras=[(kpe, "m")])
    den = (jnp.arange(MLA_HEADS * HEAD_PAD) % HEAD_PAD == DEN_LANE).astype(F32)[None]
    v_mla = _mm(n + "v_mla", ckvn, w["wuv"], [BF16], epi=lambda acc, dv: (acc + dv,), extras=[(den, "n")])
    o_mla, lse_mla = _causal_fwd(n + "mla_fwd", q_mla, k_mla, v_mla, heads=MLA_HEADS, tile=MLA_TILE,
                                 chunk=MLA_CHUNK_FWD)
    o_swa, lse_swa = _swa_fwd(n + "swa_fwd", proj_b, swa_bias, p["sinks"], tq=SWA_TQ)
    mn = _rmsnorm(n + "mem_norm", mem, p["mem_norm"], BF16)
    kvm = _mm(n + "kv_mem", mn, w["wmem"], [BF16])
    o_mem, lse_mem = _mem_fwd(n + "mem_fwd", proj_b, kvm, tq=MEM_TQ, chunk=MEM_CHUNK)
    t0 = _mm(n + "t_mla", o_mla, w["wo_mla"], [BF16], tn=1024)
    t1 = _mm(n + "t_swa", o_swa, w["wo_swa"], [BF16], tn=1024)
    t2 = _mm(n + "t_mem", o_mem, w["wo_mem"], [BF16], tn=1024)

    def merge(g0, g1, g2, bg, a0, a1, a2):
        y = (_sigmoid(g0 + bg[:, 0:1024]) * a0 + _sigmoid(g1 + bg[:, 1024:2048]) * a1
             + _sigmoid(g2 + bg[:, 2048:3072]) * a2)
        return (y,)

    y = _rowwise(n + "merge", merge,
                 [_rows(gates, 1024, 0), _rows(gates, 1024, 1), _rows(gates, 1024, 2), _full(p["b_gate"]),
                  _rows(t0), _rows(t1), _rows(t2)], [((S, D_MODEL), BF16, "rows")], rows=S)[0]
    x1 = _mm(n + "out_proj", y, w["wout"], [F32], epi=lambda acc, r: (acc + r,), extras=[(x, "mn")])
    h2 = _rmsnorm(n + "mlp_norm", x1, p["mlp_norm"], BF16)
    act = _mm(n + "mlp_up", h2, w["wup"], [BF16], epi=lambda acc: (jnp.square(jnp.maximum(acc, 0.0)),), tn=1024)
    x2 = _mm(n + "mlp_down", act, w["wdown"], [F32], epi=lambda acc, r: (acc + r,), extras=[(x1, "mn")])
    saved = dict(x=x, h=h, gates=gates, proj_a=proj_a, proj_b=proj_b, cqn=cqn, ckvn=ckvn, q_mla=q_mla, k_mla=k_mla, v_mla=v_mla,
                 o_mla=o_mla, lse_mla=lse_mla, o_swa=o_swa, lse_swa=lse_swa, mn=mn, kvm=kvm, o_mem=o_mem,
                 lse_mem=lse_mem, t0=t0, t1=t1, t2=t2, y=y, x1=x1, h2=h2, act=act)
    return x2, saved


def _layer_bwd(l, dx2, mem, w, p, tabs, swa_bias, sv, S):
    c, ck, s1, s2 = tabs
    n = f"l{l}_b_"
    gw = {}
    gs = {}
    du = _mm(n + "d_act", dx2, w["wdown_t"], [BF16],
             epi=lambda acc, av: (acc * (2.0 * jnp.sqrt(av.astype(F32))),), extras=[(sv["act"], "mn")], tn=1024)
    gw["wdown"] = _mm_tn(n + "g_wdown", sv["act"], dx2)
    gw["wup"] = _mm_tn(n + "g_wup", sv["h2"], du)
    dh2 = _mm(n + "d_h2", du, w["wup_t"], [F32])
    dx1, gs["mlp_norm"] = _residual_norm_bwd(n + "mlp_norm", dx2, dh2, sv["x1"], p["mlp_norm"])
    gw["wout"] = _mm_tn(n + "g_wout", sv["y"], dx1)
    dy = _mm(n + "d_y", dx1, w["wout_t"], [F32])

    def merge_bwd(dyv, g0, g1, g2, bg, a0, a1, a2):
        outs, dgs = [], []
        for b, (gv, av) in enumerate(((g0, a0), (g1, a1), (g2, a2))):
            sg = _sigmoid(gv + bg[:, b * 1024:(b + 1) * 1024])
            outs.append(dyv * sg)
            dgs.append(dyv * av * sg * (1.0 - sg))
        dg = jnp.concatenate(dgs, axis=1)
        return outs[0], outs[1], outs[2], dg, jnp.sum(dg, axis=0, keepdims=True)

    pa = sv["proj_a"]
    gt = sv["gates"]
    dt0, dt1, dt2, dgates, gs["b_gate"] = _rowwise(
        n + "merge", merge_bwd,
        [_rows(dy), _rows(gt, 1024, 0), _rows(gt, 1024, 1), _rows(gt, 1024, 2), _full(p["b_gate"]),
         _rows(sv["t0"]), _rows(sv["t1"]), _rows(sv["t2"])],
        [((S, D_MODEL), BF16, "rows")] * 3 + [((S, 3 * D_MODEL), BF16, "rows"), ((1, 3 * D_MODEL), F32, "acc")],
        rows=S)
    gw["wo_mla"] = _mm_tn(n + "g_wo_mla", sv["o_mla"], dt0)
    gw["wo_swa"] = _mm_tn(n + "g_wo_swa", sv["o_swa"], dt1)
    gw["wo_mem"] = _mm_tn(n + "g_wo_mem", sv["o_mem"], dt2)
    do_mla = _mm(n + "d_o_mla", dt0, w["wo_mla_t"], [BF16])
    do_swa = _mm(n + "d_o_swa", dt1, w["wo_swa_t"], [BF16])
    do_mem = _mm(n + "d_o_mem", dt2, w["wo_mem_t"], [BF16])
    pb = sv["proj_b"]
    delta_mla = _row_dot(n + "mla_delta", sv["o_mla"], do_mla, heads=MLA_HEADS, tm=MLA_TILE)
    dq_mla, dk_mla, dv_mla = _causal_bwd(
        n + "mla_bwd", sv["q_mla"], sv["k_mla"], sv["v_mla"], do_mla, sv["lse_mla"], delta_mla, heads=MLA_HEADS,
        tile=MLA_TILE, chunk=MLA_CHUNK)
    dq_swa, dk_swa, dv_swa, dk_edge, dv_edge, dbias, dsink = _swa_bwd(
        n + "swa_bwd", pb, swa_bias, p["sinks"], sv["o_swa"], do_swa, sv["lse_swa"], tq=SWA_TQ)
    dq_mem, dkvm = _mem_bwd(n + "mem_bwd", pb, sv["kvm"], sv["o_mem"], do_mem, sv["lse_mem"], tq=MEM_TQ,
                            chunk=MEM_CHUNK)
    gs["dbias"] = dbias
    gs["sinks"] = dsink[:, 0, 0]
    gw["wmem"] = _mm_tn(n + "g_wmem", sv["mn"], dkvm)
    dmn = _mm(n + "d_mn", dkvm, w["wmem_t"], [F32])
    _, gs["mem_norm"] = _residual_norm_bwd(n + "mem_norm", dmn, dmn, mem, p["mem_norm"])
    dq_pre = _rowwise(n + "q_unrope", lambda d, cv, s1v, s2v: (_rope_bwd(d * MLA_SCALE, cv, s1v, s2v),),
                      [_rows(dq_mla), _rows(c), _rows(s1), _rows(s2)], [((S, 1024), BF16, "rows")], rows=S)[0]
    gw["wuq"] = _mm_tn(n + "g_wuq", sv["cqn"], dq_pre)
    gw["wuk"] = _mm_tn(n + "g_wuk", sv["ckvn"], dk_mla)
    gw["wuv"] = _mm_tn(n + "g_wuv", sv["ckvn"], dv_mla)
    dcqn = _mm(n + "d_cqn", dq_pre, w["wuq_t"], [F32])
    dckvn = _mm(n + "d_ckvn_k", dk_mla, w["wuk_t"], [F32])
    dckvn = _mm(n + "d_ckvn_v", dv_mla, w["wuv_t"], [F32], epi=lambda acc, r: (acc + r,), extras=[(dckvn, "mn")])

    def mla_norm_bwd(dcq_n, dckv_n, dk, cq, kva, qn, kvn, ckv, s1v, s2v):
        dcq, dqn = _norm_bwd(dcq_n, cq, qn)
        dckv, dkvn = _norm_bwd(dckv_n, kva[:, :128], kvn)
        dkpe = dk[:, 0:128]
        for hh in range(1, MLA_HEADS):
            dkpe = dkpe + dk[:, hh * 128:(hh + 1) * 128]
        dpe = pltpu.roll(_rope_bwd(dkpe, ckv, s1v, s2v), 64, 1)
        return jnp.concatenate([dcq, dckv, dpe], axis=1), dqn, dkvn

    dtail, gs["mla_q_norm"], gs["mla_kv_norm"] = _rowwise(
        n + "mla_norm", mla_norm_bwd,
        [_rows(dcqn), _rows(dckvn), _rows(dk_mla), _rows(pa, 256, 0), _rows(pa, 256, 1),
         _full(p["mla_q_norm"]), _full(p["mla_kv_norm"]), _rows(ck), _rows(s1), _rows(s2)],
        [((S, 512), BF16, "rows"), ((1, 256), F32, "acc"), ((1, 128), F32, "acc")], rows=S)

    dproj_b = _dproj_b(n + "dproj_b", dq_swa, dq_mem, dk_swa, dv_swa, dk_edge, dv_edge, tq=SWA_TQ)
    h = sv["h"]
    gw["wag"] = _mm_tn(n + "g_wa_gates", h, dgates)
    gw["wat"] = _mm_tn(n + "g_wa_tail", h, dtail)
    gw["wb"] = _mm_tn(n + "g_wb", h, dproj_b)
    dh = _mm(n + "d_h_gates", dgates, w["wag_t"], [F32])
    dh = _mm(n + "d_h_tail", dtail, w["wat_t"], [F32], epi=lambda acc, r: (acc + r,), extras=[(dh, "mn")])
    dh = _mm(n + "d_h_b", dproj_b, w["wb_t"], [F32], epi=lambda acc, r: (acc + r,), extras=[(dh, "mn")])
    dx, gs["attn_norm"] = _residual_norm_bwd(n + "attn_norm", dx1, dh, sv["x"], p["attn_norm"])
    return dx, gw, gs


def _local_step(x, mem, loss_target, full, small):
    S = x.shape[0]
    tabs = _rope_tables(S)
    onehot, band = _bias_onehot()
    hi = lax.Precision.HIGHEST
    swa_bias = _mm("swa_bias", small["rel_bias"].T, onehot, [F32], epi=lambda acc, mk: (acc + mk,),
                   extras=[(band, "n")], cast=None, precision=hi, tn=8192).reshape(SWA_HEADS, WINDOW, 2 * WINDOW)
    ws, ps = [], []
    for l in range(DEPTH):
        ws.append(_layer_weights(full, l))
        ps.append(dict(
            attn_norm=small["attn_norm"][l][None], mem_norm=small["mem_norm"][l][None],
            b_gate=small["b_gate"][l][None], mla_q_norm=small["mla_q_norm"][l][None],
            mla_kv_norm=small["mla_kv_norm"][l][None], mlp_norm=small["mlp_norm"][l][None],
            sinks=jnp.broadcast_to(small["attn_sinks"][l][:, None, None], (SWA_HEADS, 8, 128))))
    saved = []
    xc = x
    for l in range(DEPTH):
        xc, sv = _layer_fwd(l, xc, mem, ws[l], ps[l], tabs, swa_bias, S)
        saved.append(sv)

    fn_g = small["final_norm"][None]

    def loss_fn(xv, gv, tv):
        r = _rstd(xv)
        xh = xv * r
        err = xh * gv - tv
        dyv = err * (1.0 / D_MODEL)
        wv = dyv * gv
        dx = r * (wv - xh * jnp.mean(wv * xh, axis=-1, keepdims=True))
        part = 0.5 * jnp.sum(err * err) * (1.0 / D_MODEL)
        return dx, jnp.sum(dyv * xh, axis=0, keepdims=True), jnp.zeros((8, 128), F32) + part

    dx, g_final, loss_acc = _rowwise(
        "loss", loss_fn, [_rows(xc), _full(fn_g), _rows(loss_target)],
        [((S, D_MODEL), F32, "rows"), ((1, D_MODEL), F32, "acc"), ((8, 128), F32, "acc")], rows=S)

    gws, gss = [None] * DEPTH, [None] * DEPTH
    for l in reversed(range(DEPTH)):
        dx, gw, gs = _layer_bwd(l, dx, mem, ws[l], ps[l], tabs, swa_bias, saved[l], S)
        gws[l] = _layer_weight_grads(gw)
        gss[l] = gs

    dbias = (gss[0]["dbias"] + gss[1]["dbias"]).reshape(SWA_HEADS, -1)
    g_rel = _mm("g_rel_bias", dbias, onehot.T, [F32], cast=None, precision=hi, tk=8192).T
    wgrads = {k: jnp.stack([gws[l][k] for l in range(DEPTH)]) for k in gws[0]}
    sgrads = dict(
        rel_bias=g_rel,
        final_norm=g_final[0],
        attn_sinks=jnp.stack([gss[l]["sinks"] for l in range(DEPTH)]),
        **{k: jnp.concatenate([gss[l][k] for l in range(DEPTH)], axis=0)
           for k in ("attn_norm", "mem_norm", "b_gate", "mla_q_norm", "mla_kv_norm", "mlp_norm")})
    return loss_acc[0, 0], dx, wgrads, sgrads


def _pack_small(vals, loss):
    rows = []
    for name, shape in SMALL:
        flat = vals[name].astype(F32).reshape(-1)
        pad = (-flat.shape[0]) % 1024
        rows.append(jnp.pad(flat, (0, pad)).reshape(-1, 128))
    rows.append(jnp.zeros((8, 128), F32) + loss)
    return jnp.concatenate(rows, axis=0)


def _unpack_small(packed):
    out, r = {}, 0
    for name, shape in SMALL:
        size = math.prod(shape)
        nrows = 8 * -(-size // 1024)
        out[name] = packed[r:r + nrows].reshape(-1)[:size].reshape(shape)
        r += nrows
    return out, packed[r, 0]


def kernel(x, mem, rel_bias, attn_norm, mem_norm, w_in, b_gate, mla_q_norm, w_uq, mla_kv_norm, w_ukv, attn_sinks, w_mem_kv, w_o_mla, w_o_swa, w_o_mem, w_out, mlp_norm, w_up, w_down, final_norm, loss_target, m_rel_bias, m_attn_norm, m_mem_norm, m_w_in, m_b_gate, m_mla_q_norm, m_w_uq, m_mla_kv_norm, m_w_ukv, m_attn_sinks, m_w_mem_kv, m_w_o_mla, m_w_o_swa, m_w_o_mem, m_w_out, m_mlp_norm, m_w_up, m_w_down, m_final_norm, v_rel_bias, v_attn_norm, v_mem_norm, v_w_in, v_b_gate, v_mla_q_norm, v_w_uq, v_mla_kv_norm, v_w_ukv, v_attn_sinks, v_w_mem_kv, v_w_o_mla, v_w_o_swa, v_w_o_mem, v_w_out, v_mlp_norm, v_w_up, v_w_down, v_final_norm):
    wv = dict(rel_bias=rel_bias, attn_norm=attn_norm, mem_norm=mem_norm, w_in=w_in, b_gate=b_gate,
              mla_q_norm=mla_q_norm, w_uq=w_uq, mla_kv_norm=mla_kv_norm, w_ukv=w_ukv, attn_sinks=attn_sinks,
              w_mem_kv=w_mem_kv, w_o_mla=w_o_mla, w_o_swa=w_o_swa, w_o_mem=w_o_mem, w_out=w_out,
              mlp_norm=mlp_norm, w_up=w_up, w_down=w_down, final_norm=final_norm)
    mv = dict(rel_bias=m_rel_bias, attn_norm=m_attn_norm, mem_norm=m_mem_norm, w_in=m_w_in, b_gate=m_b_gate,
              mla_q_norm=m_mla_q_norm, w_uq=m_w_uq, mla_kv_norm=m_mla_kv_norm, w_ukv=m_w_ukv,
              attn_sinks=m_attn_sinks, w_mem_kv=m_w_mem_kv, w_o_mla=m_w_o_mla, w_o_swa=m_w_o_swa,
              w_o_mem=m_w_o_mem, w_out=m_w_out, mlp_norm=m_mlp_norm, w_up=m_w_up, w_down=m_w_down,
              final_norm=m_final_norm)
    vv = dict(rel_bias=v_rel_bias, attn_norm=v_attn_norm, mem_norm=v_mem_norm, w_in=v_w_in, b_gate=v_b_gate,
              mla_q_norm=v_mla_q_norm, w_uq=v_w_uq, mla_kv_norm=v_mla_kv_norm, w_ukv=v_w_ukv,
              attn_sinks=v_attn_sinks, w_mem_kv=v_w_mem_kv, w_o_mla=v_w_o_mla, w_o_swa=v_w_o_swa,
              w_o_mem=v_w_o_mem, w_out=v_w_out, mlp_norm=v_mlp_norm, w_up=v_w_up, w_down=v_w_down,
              final_norm=v_final_norm)

    shard_rows = [math.prod(_shard_shape(shape, axis)) // 128 for _, shape, axis in WSPECS]
    gathered = _gather_forwarded("gather_weights", _pack_rows([wv[name].astype(BF16) for name, _, _ in WSPECS]))
    full, r = {}, 0
    for (name, shape, axis), nr in zip(WSPECS, shard_rows):
        full[name] = _unstack(gathered[:, r:r + nr].reshape((N_DEV,) + _shard_shape(shape, axis)), shape, axis)
        r += nr

    loss_part, grad_x, wgrads, sgrads = _local_step(x[0], mem[0], loss_target[0], full,
                                                    {name: wv[name] for name, _ in SMALL})

    send = jnp.concatenate([_restack(wgrads[name], axis).astype(BF16).reshape(N_DEV, -1, 128)
                            for name, _, axis in WSPECS], axis=1)
    recv = _reduce_scatter(send)
    outs = _adam("adam_sharded", recv, *[_pack_rows([d[name] for name, _, _ in WSPECS]) for d in (wv, mv, vv)])
    res = {}
    r = 0
    for (name, shape, axis), nr in zip(WSPECS, shard_rows):
        res[name] = [o[r:r + nr].reshape(_shard_shape(shape, axis)) for o in outs]
        r += nr

    small_recv = _exchange("gather_small", _pack_small(sgrads, loss_part), per_peer=False)
    zero = jnp.zeros((), F32)
    souts = _adam("adam_small", small_recv, *[_pack_small(d, zero) for d in (wv, mv, vv)])
    loss = None
    for i, o in enumerate(souts):
        vals, extra = _unpack_small(o)
        if i == 0:
            loss = extra
        for name, _ in SMALL:
            res.setdefault(name, []).append(vals[name])

    out = [loss, grad_x[None]]
    for i in range(4):
        out.extend(res[name][i] for name in WEIGHT_ORDER)
    return tuple(out)
```

```python
import math

import jax
import jax.numpy as jnp
from jax import lax
from jax.experimental import pallas as pl
from jax.experimental.pallas import tpu as pltpu

F32 = jnp.float32
BF16 = jnp.bfloat16

N_DEV = 8
D_MODEL = 1024
DEPTH = 2
MLA_HEADS = 8
MLA_Q_LORA = 256
MLA_KV_LORA = 128
MLA_NOPE = 64
MLA_ROPE = 32
MLA_V = 64
ROPE_THETA = 10000.0
SWA_HEADS = 8
SWA_KV_HEADS = 2
SWA_HD = 64
WINDOW = 128
REL_BUCKETS = 32
REL_MAX_DIST = 128
MEM_LEN = 256
MEM_HEADS = 4
MEM_HD = 128
D_FF = 4 * D_MODEL
EPS = 1e-6
HEAD_PAD = 128
ADAM_LR = 0.001
ADAM_B1 = 0.9
ADAM_B2 = 0.999
ADAM_EPS = 1e-08
ADAM_WD = 0.01
ADAM_STEP = 10

NEG = -1e30
VMEM_LIMIT = 48 * 1024 * 1024

MM_TM = 1024
MM_TN = 1024
MM_TK = 1024
TN_T1 = 1024
TN_TN = 1024
TN_TS = 1024
ROW_TM = 256
MLA_TILE = 1024
MLA_CHUNK = 256
MLA_CHUNK_FWD = 512
MLA_SCALE = (MLA_NOPE + MLA_ROPE) ** -0.5
LOG2E = math.log2(math.e)
DEN_LANE = MLA_V
SWA_TQ = 1024
MEM_TQ = 1024
MEM_CHUNK = 256
ADAM_TM = 1200

WSPECS = (
    ("w_in", (DEPTH, D_MODEL, 4768), 2),
    ("w_uq", (DEPTH, MLA_Q_LORA, 768), 2),
    ("w_ukv", (DEPTH, MLA_KV_LORA, 1024), 2),
    ("w_mem_kv", (DEPTH, D_MODEL, 1024), 1),
    ("w_o_mla", (DEPTH, 512, D_MODEL), 2),
    ("w_o_swa", (DEPTH, 512, D_MODEL), 2),
    ("w_o_mem", (DEPTH, 512, D_MODEL), 2),
    ("w_out", (DEPTH, D_MODEL, D_MODEL), 1),
    ("w_up", (DEPTH, D_MODEL, D_FF), 2),
    ("w_down", (DEPTH, D_FF, D_MODEL), 1),
)
SMALL = (
    ("rel_bias", (REL_BUCKETS, SWA_HEADS)),
    ("attn_norm", (DEPTH, D_MODEL)),
    ("mem_norm", (DEPTH, D_MODEL)),
    ("b_gate", (DEPTH, 3 * D_MODEL)),
    ("mla_q_norm", (DEPTH, MLA_Q_LORA)),
    ("mla_kv_norm", (DEPTH, MLA_KV_LORA)),
    ("attn_sinks", (DEPTH, SWA_HEADS)),
    ("mlp_norm", (DEPTH, D_MODEL)),
    ("final_norm", (D_MODEL,)),
)
WEIGHT_ORDER = ("rel_bias", "attn_norm", "mem_norm", "w_in", "b_gate", "mla_q_norm", "w_uq", "mla_kv_norm",
                "w_ukv", "attn_sinks", "w_mem_kv", "w_o_mla", "w_o_swa", "w_o_mem", "w_out", "mlp_norm",
                "w_up", "w_down", "final_norm")


def _cparams(*sem):
    return pltpu.CompilerParams(dimension_semantics=sem, vmem_limit_bytes=VMEM_LIMIT)


def _shard_shape(shape, axis):
    s = list(shape)
    s[axis] //= N_DEV
    return tuple(s)


def _mm(name, a, b, out_dtypes, *, epi=None, extras=(), a_fn=None, cast=BF16, precision=None,
        tm=None, tn=None, tk=None):
    M, K = a.shape
    K2, N = b.shape
    assert K == K2, (name, a.shape, b.shape)
    tm = min(tm or MM_TM, M)
    tn = min(tn or MM_TN, N)
    tk = min(tk or MM_TK, K)
    assert M % tm == 0 and N % tn == 0 and K % tk == 0, (name, a.shape, b.shape, tm, tn, tk)
    nk = K // tk
    n_ex = len(extras)
    n_out = len(out_dtypes)

    def body(*refs):
        a_ref, b_ref = refs[0], refs[1]
        ex_refs = refs[2:2 + n_ex]
        out_refs = refs[2 + n_ex:2 + n_ex + n_out]
        av = a_ref[...]
        if a_fn is not None:
            av = a_fn(av)
        bv = b_ref[...]
        if cast is not None:
            av = av.astype(cast)
            bv = bv.astype(cast)
        part = jnp.dot(av, bv, preferred_element_type=F32, precision=precision)

        def finish(acc):
            outs = epi(acc, *[r[...] for r in ex_refs]) if epi is not None else (acc,)
            for r, o in zip(out_refs, outs):
                r[...] = o.astype(r.dtype)

        if nk == 1:
            finish(part)
        else:
            acc_ref = refs[-1]
            k = pl.program_id(2)

            @pl.when(k == 0)
            def _():
                acc_ref[...] = part

            @pl.when(k > 0)
            def _():
                acc_ref[...] += part

            @pl.when(k == nk - 1)
            def _():
                finish(acc_ref[...])

    in_specs = [pl.BlockSpec((tm, tk), lambda i, j, k: (i, k)),
                pl.BlockSpec((tk, tn), lambda i, j, k: (k, j))]
    for arr, kind in extras:
        if kind == "mn":
            in_specs.append(pl.BlockSpec((tm, tn), lambda i, j, k: (i, j)))
        elif kind == "m":
            in_specs.append(pl.BlockSpec((tm, arr.shape[1]), lambda i, j, k: (i, 0)))
        else:
            in_specs.append(pl.BlockSpec((1, tn), lambda i, j, k: (0, j)))
    outs = pl.pallas_call(
        body, name=name, grid=(M // tm, N // tn, nk),
        in_specs=in_specs,
        out_specs=[pl.BlockSpec((tm, tn), lambda i, j, k: (i, j)) for _ in out_dtypes],
        out_shape=[jax.ShapeDtypeStruct((M, N), dt) for dt in out_dtypes],
        scratch_shapes=[pltpu.VMEM((tm, tn), F32)] if nk > 1 else [],
        compiler_params=_cparams("parallel", "parallel", "arbitrary"),
    )(a, b, *[arr for arr, _ in extras])
    return outs[0] if n_out == 1 else outs


def _mm_tn(name, a, b, *, t1=None, tn=None, ts=None):
    S, K1 = a.shape
    S2, N = b.shape
    assert S == S2, (name, a.shape, b.shape)
    t1 = min(t1 or TN_T1, K1)
    tn = min(tn or TN_TN, N)
    ts = min(ts or TN_TS, S)
    assert K1 % t1 == 0 and N % tn == 0 and S % ts == 0, (name, a.shape, b.shape)

    def body(a_ref, b_ref, o_ref):
        s = pl.program_id(2)
        part = lax.dot_general(a_ref[...].astype(BF16), b_ref[...].astype(BF16),
                               (((0,), (0,)), ((), ())), preferred_element_type=F32)

        @pl.when(s == 0)
        def _():
            o_ref[...] = part

        @pl.when(s > 0)
        def _():
            o_ref[...] += part

    return pl.pallas_call(
        body, name=name, grid=(K1 // t1, N // tn, S // ts),
        in_specs=[pl.BlockSpec((ts, t1), lambda i, j, s: (s, i)),
                  pl.BlockSpec((ts, tn), lambda i, j, s: (s, j))],
        out_specs=pl.BlockSpec((t1, tn), lambda i, j, s: (i, j)),
        out_shape=jax.ShapeDtypeStruct((K1, N), F32),
        compiler_params=_cparams("parallel", "parallel", "arbitrary"),
    )(a, b)


def _rows(arr, width=None, blk=0):
    return (arr, ("rows", arr.shape[1] if width is None else width, blk))


def _full(arr):
    return (arr, ("full",))


def _rowwise(name, fn, ins, outs, *, rows, tm=None):
    tm = min(tm or ROW_TM, rows)
    assert rows % tm == 0, (name, rows, tm)
    n_in = len(ins)

    def body(*refs):
        i = pl.program_id(0)
        vals = fn(*[r[...] for r in refs[:n_in]])
        for (shape, dt, kind), r, v in zip(outs, refs[n_in:], vals):
            if kind == "rows":
                r[...] = v.astype(dt)
            else:
                @pl.when(i == 0)
                def _(r=r, v=v):
                    r[...] = v

                @pl.when(i > 0)
                def _(r=r, v=v):
                    r[...] += v

    in_specs = []
    for arr, spec in ins:
        if spec[0] == "rows":
            in_specs.append(pl.BlockSpec((tm, spec[1]), lambda i, b=spec[2]: (i, b)))
        else:
            in_specs.append(pl.BlockSpec(arr.shape, lambda i, n=arr.ndim: (0,) * n))
    out_specs = []
    for shape, dt, kind in outs:
        if kind == "rows":
            out_specs.append(pl.BlockSpec((tm, shape[1]), lambda i: (i, 0)))
        else:
            out_specs.append(pl.BlockSpec(shape, lambda i, n=len(shape): (0,) * n))
    res = pl.pallas_call(
        body, name=name, grid=(rows // tm,),
        in_specs=in_specs, out_specs=out_specs,
        out_shape=[jax.ShapeDtypeStruct(shape, dt) for shape, dt, _ in outs],
        compiler_params=_cparams("arbitrary"),
    )(*[arr for arr, _ in ins])
    return res


MEM_SCALE = MEM_HD ** -0.5
MEM_Q0 = 2
NT_DIMS = (((1,), (1,)), ((), ()))


def _head_lanes(h):
    return slice(h * HEAD_PAD, (h + 1) * HEAD_PAD)


def _mem_fwd(name, proj_b, kvm, *, tq, chunk):
    S = proj_b.shape[0]
    tq = min(tq, S)
    C = min(chunk, tq)
    tiles = [(h, c) for c in range(tq // C) for h in range(MEM_HEADS)]

    def body(q_ref, kv_ref, o_ref, lse_ref):
        def logits(h, c):
            return lax.dot_general(q_ref[c * C:(c + 1) * C, _head_lanes(h)], kv_ref[:, _head_lanes(h)], NT_DIMS,
                                   preferred_element_type=F32) * MEM_SCALE

        nxt = logits(*tiles[0])
        for n, (h, c) in enumerate(tiles):
            s = nxt
            if n + 1 < len(tiles):
                nxt = logits(*tiles[n + 1])
            rows = slice(c * C, (c + 1) * C)
            m = jnp.max(s, axis=1, keepdims=True)
            p = jnp.exp(s - m)
            l = jnp.sum(p, axis=1, keepdims=True)
            o = jnp.dot(p.astype(BF16), kv_ref[:, _head_lanes(MEM_HEADS + h)], preferred_element_type=F32) / l
            o_ref[rows, _head_lanes(h)] = o.astype(o_ref.dtype)
            lse_ref[h, rows, :] = m + jnp.log(l)

    return pl.pallas_call(
        body, name=name, grid=(S // tq,),
        in_specs=[pl.BlockSpec((tq, MEM_HEADS * HEAD_PAD), lambda i: (i, MEM_Q0)),
                  pl.BlockSpec(kvm.shape, lambda i: (0, 0))],
        out_specs=[pl.BlockSpec((tq, MEM_HEADS * HEAD_PAD), lambda i: (i, 0)),
                   pl.BlockSpec((MEM_HEADS, tq, 1), lambda i: (0, i, 0))],
        out_shape=[jax.ShapeDtypeStruct((S, MEM_HEADS * HEAD_PAD), BF16),
                   jax.ShapeDtypeStruct((MEM_HEADS, S, 1), F32)],
        compiler_params=_cparams("parallel"),
    )(proj_b, kvm)


def _mem_bwd(name, proj_b, kvm, o, do, lse, *, tq, chunk):
    S = proj_b.shape[0]
    tq = min(tq, S)
    C = min(chunk, tq)
    nq = S // tq
    tiles = [(h, c) for c in range(tq // C) for h in range(MEM_HEADS)]

    def body(q_ref, kv_ref, o_ref, do_ref, lse_ref, dq_ref, dkv_ref, acc_sc):
        i = pl.program_id(0)

        @pl.when(i == 0)
        def _():
            acc_sc[...] = jnp.zeros(acc_sc.shape, F32)

        def mats(h, c):
            rows = slice(c * C, (c + 1) * C)
            q = q_ref[rows, _head_lanes(h)]
            dov = do_ref[rows, _head_lanes(h)]
            s = lax.dot_general(q, kv_ref[:, _head_lanes(h)], NT_DIMS, preferred_element_type=F32) * MEM_SCALE
            dp = lax.dot_general(dov, kv_ref[:, _head_lanes(MEM_HEADS + h)], NT_DIMS, preferred_element_type=F32)
            return q, dov, s, dp

        nxt = mats(*tiles[0])
        for n, (h, c) in enumerate(tiles):
            q, dov, s, dp = nxt
            if n + 1 < len(tiles):
                nxt = mats(*tiles[n + 1])
            rows = slice(c * C, (c + 1) * C)
            p = jnp.exp(s - lse_ref[h, rows, :])
            delta = jnp.sum(dov.astype(F32) * o_ref[rows, _head_lanes(h)].astype(F32), axis=1, keepdims=True)
            ds = (p * (dp - delta) * MEM_SCALE).astype(BF16)
            dq_ref[rows, _head_lanes(h)] = jnp.dot(ds, kv_ref[:, _head_lanes(h)],
                                                   preferred_element_type=F32).astype(dq_ref.dtype)
            acc_sc[_head_lanes(h), :] += jnp.dot(q.T, ds, preferred_element_type=F32)
            acc_sc[_head_lanes(MEM_HEADS + h), :] += jnp.dot(dov.T, p.astype(BF16), preferred_element_type=F32)

        @pl.when(i == nq - 1)
        def _():
            dkv_ref[...] = acc_sc[...].T

    qblk = pl.BlockSpec((tq, MEM_HEADS * HEAD_PAD), lambda i: (i, 0))
    return pl.pallas_call(
        body, name=name, grid=(nq,),
        in_specs=[pl.BlockSpec((tq, MEM_HEADS * HEAD_PAD), lambda i: (i, MEM_Q0)),
                  pl.BlockSpec(kvm.shape, lambda i: (0, 0)), qblk, qblk,
                  pl.BlockSpec((MEM_HEADS, tq, 1), lambda i: (0, i, 0))],
        out_specs=[qblk, pl.BlockSpec(kvm.shape, lambda i: (0, 0))],
        out_shape=[jax.ShapeDtypeStruct((S, MEM_HEADS * HEAD_PAD), BF16), jax.ShapeDtypeStruct(kvm.shape, F32)],
        scratch_shapes=[pltpu.VMEM((kvm.shape[1], kvm.shape[0]), F32)],
        compiler_params=_cparams("arbitrary"),
    )(proj_b, kvm, o, do, lse)


def _causal_fwd(name, q_arr, k_arr, v_arr, *, heads, tile, chunk):
    S = q_arr.shape[0]
    T = min(tile, S)
    C = min(chunk, T)
    nt = S // T
    nc = T // C

    pairs = [(qi, kk) for qi in range(nt) for kk in range(qi + 1)]
    q_tab = jnp.asarray([p[0] for p in pairs], jnp.int32)
    k_tab = jnp.asarray([p[1] for p in pairs], jnp.int32)

    def body(qt_ref, kt_ref, q_ref, k_ref, v_ref, o_ref, lse_ref, m_sc, acc_sc):
        t = pl.program_id(1)
        qi = qt_ref[t]
        kk = kt_ref[t]

        @pl.when(kk == 0)
        def _():
            m_sc[...] = jnp.full(m_sc.shape, NEG, F32)
            acc_sc[...] = jnp.zeros(acc_sc.shape, F32)

        def logits(c, ncols, masked):
            s = lax.dot_general(q_ref[pl.ds(c * C, C), :], k_ref[0:ncols, :], (((1,), (1,)), ((), ())),
                                preferred_element_type=F32)
            if masked:
                r = c * C + lax.broadcasted_iota(jnp.int32, (C, ncols), 0)
                cidx = lax.broadcasted_iota(jnp.int32, (C, ncols), 1)
                s = jnp.where(cidx <= r, s, NEG)
            return s

        def update(c, ncols, s):
            rows = pl.ds(c * C, C)
            m_prev = m_sc[rows, :]
            m_new = jnp.maximum(m_prev, jnp.max(s, axis=1, keepdims=True))
            p = jnp.exp2(s - m_new).astype(BF16)
            acc_sc[rows, :] = jnp.exp2(m_prev - m_new) * acc_sc[rows, :] + jnp.dot(
                p, v_ref[0:ncols, :], preferred_element_type=F32)
            m_sc[rows, :] = m_new

        def tile_step(ncols_of, masked):
            s = logits(0, ncols_of(0), masked)
            for c in range(nc):
                s_next = logits(c + 1, ncols_of(c + 1), masked) if c + 1 < nc else None
                update(c, ncols_of(c), s)
                s = s_next

        @pl.when(kk < qi)
        def _():
            tile_step(lambda c: T, False)

        @pl.when(kk == qi)
        def _():
            tile_step(lambda c: (c + 1) * C, True)

        @pl.when(kk == qi)
        def _():
            acc = acc_sc[...]
            l = acc[:, DEN_LANE:DEN_LANE + 1]
            o_ref[...] = (acc / l).astype(o_ref.dtype)
            lse_ref[0] = m_sc[...] + jnp.log2(l)

    grid_spec = pltpu.PrefetchScalarGridSpec(
        num_scalar_prefetch=2, grid=(heads, len(pairs)),
        in_specs=[pl.BlockSpec((T, HEAD_PAD), lambda h, t, qt, kt: (qt[t], h)),
                  pl.BlockSpec((T, HEAD_PAD), lambda h, t, qt, kt: (kt[t], h)),
                  pl.BlockSpec((T, HEAD_PAD), lambda h, t, qt, kt: (kt[t], h))],
        out_specs=[pl.BlockSpec((T, HEAD_PAD), lambda h, t, qt, kt: (qt[t], h)),
                   pl.BlockSpec((1, T, 1), lambda h, t, qt, kt: (h, qt[t], 0))],
        scratch_shapes=[pltpu.VMEM((T, 1), F32), pltpu.VMEM((T, HEAD_PAD), F32)])
    return pl.pallas_call(
        body, name=name, grid_spec=grid_spec,
        out_shape=[jax.ShapeDtypeStruct((S, heads * HEAD_PAD), BF16),
                   jax.ShapeDtypeStruct((heads, S, 1), F32)],
        compiler_params=_cparams("parallel", "arbitrary"),
    )(q_tab, k_tab, q_arr, k_arr, v_arr)


def _with_neg_delta(do, o):
    lane = lax.broadcasted_iota(jnp.int32, (1, HEAD_PAD), 1)
    outs = []
    for h in range(do.shape[1] // HEAD_PAD):
        a = do[:, _head_lanes(h)]
        nd = -jnp.sum(a * o[:, _head_lanes(h)].astype(F32), axis=1, keepdims=True)
        hi = nd.astype(BF16).astype(F32)
        a = jnp.where(lane == DEN_LANE, hi, a)
        outs.append(jnp.where(lane == DEN_LANE + 1, nd - hi, a))
    return jnp.concatenate(outs, axis=1)


def _causal_bwd(name, q_arr, k_arr, v_arr, do_arr, lse, *, heads, tile, chunk):
    S = q_arr.shape[0]
    T = min(tile, S)
    C = min(chunk, T)
    nt = S // T
    nc = T // C

    pairs = [(kj, qq) for kj in range(nt) for qq in range(kj, nt)]
    k_tab = jnp.asarray([p[0] for p in pairs], jnp.int32)
    q_tab = jnp.asarray([p[1] for p in pairs], jnp.int32)

    def body(kt_ref, qt_ref, q_ref, k_ref, v_ref, do_ref, lse_ref, dq_ref, dk_ref, dv_ref, dk_sc, dv_sc):
        t = pl.program_id(1)
        kj = kt_ref[t]
        qq = qt_ref[t]
        qb = qq

        @pl.when(t == 0)
        def _():
            dq_ref[...] = jnp.zeros(dq_ref.shape, F32)

        @pl.when(qq == kj)
        def _():
            dk_sc[...] = jnp.zeros(dk_sc.shape, F32)
            dv_sc[...] = jnp.zeros(dv_sc.shape, F32)

        def logits(c, ncols, masked):
            rows = pl.ds(c * C, C)
            s = lax.dot_general(q_ref[rows, :], k_ref[0:ncols, :], (((1,), (1,)), ((), ())),
                                preferred_element_type=F32)
            if masked:
                r = c * C + lax.broadcasted_iota(jnp.int32, (C, ncols), 0)
                cidx = lax.broadcasted_iota(jnp.int32, (C, ncols), 1)
                s = jnp.where(cidx <= r, s, NEG)
            dp = lax.dot_general(do_ref[rows, :], v_ref[0:ncols, :], (((1,), (1,)), ((), ())),
                                 preferred_element_type=F32)
            return s, dp

        def update(c, ncols, s, dp):
            rows = pl.ds(c * C, C)
            p = jnp.exp2(s - lse_ref[0, rows, :])
            ds = (p * dp).astype(BF16)
            dv_sc[:, 0:ncols] += jnp.dot(do_ref[rows, :].T, p.astype(BF16), preferred_element_type=F32)
            dk_sc[:, 0:ncols] += jnp.dot(q_ref[rows, :].T, ds, preferred_element_type=F32)
            row0 = pl.multiple_of(qb * T + c * C, C)
            dq_ref[pl.ds(row0, C), :] += jnp.dot(ds, k_ref[0:ncols, :], preferred_element_type=F32)

        def tile_step(ncols_of, masked):
            cur = logits(0, ncols_of(0), masked)
            for c in range(nc):
                nxt = logits(c + 1, ncols_of(c + 1), masked) if c + 1 < nc else None
                update(c, ncols_of(c), *cur)
                cur = nxt

        @pl.when(qq > kj)
        def _():
            tile_step(lambda c: T, False)

        @pl.when(qq == kj)
        def _():
            tile_step(lambda c: (c + 1) * C, True)

        @pl.when(qq == nt - 1)
        def _():
            dk_ref[...] = dk_sc[...].T * math.log(2.0)
            dv_ref[...] = dv_sc[...].T

    qrow = pl.BlockSpec((T, HEAD_PAD), lambda h, t, kt, qt: (qt[t], h))
    krow = pl.BlockSpec((T, HEAD_PAD), lambda h, t, kt, qt: (kt[t], h))
    qcol = pl.BlockSpec((1, T, 1), lambda h, t, kt, qt: (h, qt[t], 0))
    grid_spec = pltpu.PrefetchScalarGridSpec(
        num_scalar_prefetch=2, grid=(heads, len(pairs)),
        in_specs=[qrow, krow, krow, qrow, qcol],
        out_specs=[pl.BlockSpec((S, HEAD_PAD), lambda h, t, kt, qt: (0, h)), krow, krow],
        scratch_shapes=[pltpu.VMEM((HEAD_PAD, T), F32), pltpu.VMEM((HEAD_PAD, T), F32)])
    return pl.pallas_call(
        body, name=name, grid_spec=grid_spec,
        out_shape=[jax.ShapeDtypeStruct((S, heads * HEAD_PAD), F32)] * 3,
        compiler_params=_cparams("arbitrary", "arbitrary"),
    )(k_tab, q_tab, q_arr, k_arr, v_arr, do_arr, lse)


SWA_R = SWA_HEADS // SWA_KV_HEADS
SWA_SCALE = SWA_HD ** -0.5
SWA_Q0, SWA_K0, SWA_V0 = 0, 12, 14


def _swa_specs(tq):
    nsb = tq // WINDOW
    return [
        pl.BlockSpec((tq, SWA_R * HEAD_PAD), lambda g, i: (i, g)),
        pl.BlockSpec((tq, HEAD_PAD), lambda g, i: (i, SWA_K0 + g)),
        pl.BlockSpec((WINDOW, HEAD_PAD), lambda g, i: (jnp.maximum(nsb * i - 1, 0), SWA_K0 + g)),
        pl.BlockSpec((tq, HEAD_PAD), lambda g, i: (i, SWA_V0 + g)),
        pl.BlockSpec((WINDOW, HEAD_PAD), lambda g, i: (jnp.maximum(nsb * i - 1, 0), SWA_V0 + g)),
        pl.BlockSpec((SWA_R, WINDOW, 2 * WINDOW), lambda g, i: (g, 0, 0)),
        pl.BlockSpec((SWA_R, 8, 128), lambda g, i: (g, 0, 0)),
    ]


def _swa_block(i, sb, q_ref, kc_ref, kp_ref, vc_ref, vp_ref, bias, sink):
    rows = slice(sb * WINDOW, (sb + 1) * WINDOW)
    qs = jnp.concatenate([q_ref[rows, hh * HEAD_PAD:(hh + 1) * HEAD_PAD] for hh in range(SWA_R)], axis=0)
    if sb == 0:
        kp, vp = kp_ref[...], vp_ref[...]
    else:
        prev = slice((sb - 1) * WINDOW, sb * WINDOW)
        kp, vp = kc_ref[prev, :], vc_ref[prev, :]
    kk = jnp.concatenate([kp, kc_ref[rows, :]], axis=0)
    vv = jnp.concatenate([vp, vc_ref[rows, :]], axis=0)
    s = lax.dot_general(qs, kk, (((1,), (1,)), ((), ())), preferred_element_type=F32) * SWA_SCALE + bias
    if sb == 0:
        col = lax.broadcasted_iota(jnp.int32, (1, 2 * WINDOW), 1)
        s = s + jnp.where((col < WINDOW) & (i == 0), NEG, 0.0)
    return rows, qs, kk, vv, s


def _stack_heads(ref, rows, lead=None):
    if lead is None:
        return jnp.concatenate([ref[rows, hh * HEAD_PAD:(hh + 1) * HEAD_PAD] for hh in range(SWA_R)], axis=0)
    return jnp.concatenate([ref[hh, rows, :] for hh in range(SWA_R)], axis=0)


def _swa_fwd(name, proj_b, bias, sinks, *, tq):
    S = proj_b.shape[0]
    tq = min(tq, S)
    nsb = tq // WINDOW

    def body(q_ref, kc_ref, kp_ref, vc_ref, vp_ref, bias_ref, sink_ref, o_ref, lse_ref):
        i = pl.program_id(1)
        bias_v = bias_ref[...].reshape(SWA_R * WINDOW, 2 * WINDOW)
        sink = jnp.concatenate([jnp.zeros((WINDOW, 1), F32) + sink_ref[hh, 0:1, 0:1] for hh in range(SWA_R)], axis=0)
        nxt = _swa_block(i, 0, q_ref, kc_ref, kp_ref, vc_ref, vp_ref, bias_v, sink)
        for sb in range(nsb):
            rows, _, _, vv, s = nxt
            if sb + 1 < nsb:
                nxt = _swa_block(i, sb + 1, q_ref, kc_ref, kp_ref, vc_ref, vp_ref, bias_v, sink)
            m = jnp.maximum(jnp.max(s, axis=1, keepdims=True), sink)
            p = jnp.exp(s - m)
            l = jnp.sum(p, axis=1, keepdims=True) + jnp.exp(sink - m)
            o = jnp.dot(p.astype(BF16), vv, preferred_element_type=F32) / l
            lse_v = m + jnp.log(l)
            for hh in range(SWA_R):
                o_ref[rows, hh * HEAD_PAD:(hh + 1) * HEAD_PAD] = o[hh * WINDOW:(hh + 1) * WINDOW].astype(o_ref.dtype)
                lse_ref[hh, rows, :] = lse_v[hh * WINDOW:(hh + 1) * WINDOW]

    return pl.pallas_call(
        body, name=name, grid=(SWA_KV_HEADS, S // tq),
        in_specs=_swa_specs(tq),
        out_specs=[pl.BlockSpec((tq, SWA_R * HEAD_PAD), lambda g, i: (i, g)),
                   pl.BlockSpec((SWA_R, tq, 1), lambda g, i: (g, i, 0))],
        out_shape=[jax.ShapeDtypeStruct((S, SWA_HEADS * HEAD_PAD), BF16),
                   jax.ShapeDtypeStruct((SWA_HEADS, S, 1), F32)],
        compiler_params=_cparams("parallel", "parallel"),
    )(proj_b, proj_b, proj_b, proj_b, proj_b, bias, sinks)


def _swa_bwd(name, proj_b, bias, sinks, o, do, lse, *, tq):
    S = proj_b.shape[0]
    tq = min(tq, S)
    nsb = tq // WINDOW
    nq = S // tq

    def body(q_ref, kc_ref, kp_ref, vc_ref, vp_ref, bias_ref, sink_ref, o_ref, do_ref, lse_ref,
             dq_ref, dk_ref, dv_ref, dke_ref, dve_ref, dbias_ref, dsink_ref):
        i = pl.program_id(1)

        @pl.when(i == 0)
        def _():
            dbias_ref[...] = jnp.zeros(dbias_ref.shape, F32)
            dsink_ref[...] = jnp.zeros(dsink_ref.shape, F32)

        bias_v = bias_ref[...].reshape(SWA_R * WINDOW, 2 * WINDOW)
        sink = jnp.concatenate([jnp.zeros((WINDOW, 1), F32) + sink_ref[hh, 0:1, 0:1] for hh in range(SWA_R)], axis=0)
        dk_own, dv_own, dk_prev, dv_prev = [], [], [], []
        dbias_acc = jnp.zeros((SWA_R * WINDOW, 2 * WINDOW), F32)
        def block(sb):
            rows, qs, kk, vv, s = _swa_block(i, sb, q_ref, kc_ref, kp_ref, vc_ref, vp_ref, bias_v, sink)
            do_s = _stack_heads(do_ref, rows)
            dp = lax.dot_general(do_s, vv, (((1,), (1,)), ((), ())), preferred_element_type=F32)
            return rows, qs, kk, do_s, s, dp

        nxt = block(0)
        for sb in range(nsb):
            rows, qs, kk, do_s, s, dp = nxt
            if sb + 1 < nsb:
                nxt = block(sb + 1)
            lse_v = _stack_heads(lse_ref, rows, lead=True)
            delta = jnp.sum(do_s.astype(F32) * _stack_heads(o_ref, rows).astype(F32), axis=1, keepdims=True)
            p = jnp.exp(s - lse_v)
            dsp = p * (dp - delta)
            dbias_acc = dbias_acc + dsp
            ds = (dsp * SWA_SCALE).astype(BF16)
            dq = jnp.dot(ds, kk, preferred_element_type=F32)
            dkk = jnp.dot(qs.T, ds, preferred_element_type=F32)
            dvv = jnp.dot(do_s.T, p.astype(BF16), preferred_element_type=F32)
            dk_prev.append(dkk[:, :WINDOW].T)
            dk_own.append(dkk[:, WINDOW:].T)
            dv_prev.append(dvv[:, :WINDOW].T)
            dv_own.append(dvv[:, WINDOW:].T)
            psink = jnp.exp(sink - lse_v) * delta
            for hh in range(SWA_R):
                hrows = slice(hh * WINDOW, (hh + 1) * WINDOW)
                dq_ref[rows, hh * HEAD_PAD:(hh + 1) * HEAD_PAD] = dq[hrows].astype(dq_ref.dtype)
                dsink_ref[hh] += jnp.zeros((8, 128), F32) - jnp.sum(psink[hrows])
        dbias_ref[...] += dbias_acc.reshape(SWA_R, WINDOW, 2 * WINDOW)
        for sb in range(nsb):
            rows = slice(sb * WINDOW, (sb + 1) * WINDOW)
            if sb + 1 < nsb:
                dk_ref[rows, :] = dk_own[sb] + dk_prev[sb + 1]
                dv_ref[rows, :] = dv_own[sb] + dv_prev[sb + 1]
            else:
                dk_ref[rows, :] = dk_own[sb]
                dv_ref[rows, :] = dv_own[sb]
        dke_ref[...] = dk_prev[0]
        dve_ref[...] = dv_prev[0]

    in_specs = _swa_specs(tq) + [
        pl.BlockSpec((tq, SWA_R * HEAD_PAD), lambda g, i: (i, g)),
        pl.BlockSpec((tq, SWA_R * HEAD_PAD), lambda g, i: (i, g)),
        pl.BlockSpec((SWA_R, tq, 1), lambda g, i: (g, i, 0)),
    ]
    kv_blk = pl.BlockSpec((tq, HEAD_PAD), lambda g, i: (i, g))
    edge_blk = pl.BlockSpec((WINDOW, HEAD_PAD), lambda g, i: (i, g))
    return pl.pallas_call(
        body, name=name, grid=(SWA_KV_HEADS, nq),
        in_specs=in_specs,
        out_specs=[pl.BlockSpec((tq, SWA_R * HEAD_PAD), lambda g, i: (i, g)), kv_blk, kv_blk, edge_blk, edge_blk,
                   pl.BlockSpec((SWA_R, WINDOW, 2 * WINDOW), lambda g, i: (g, 0, 0)),
                   pl.BlockSpec((SWA_R, 8, 128), lambda g, i: (g, 0, 0))],
        out_shape=[jax.ShapeDtypeStruct((S, SWA_HEADS * HEAD_PAD), BF16),
                   jax.ShapeDtypeStruct((S, SWA_KV_HEADS * HEAD_PAD), F32),
                   jax.ShapeDtypeStruct((S, SWA_KV_HEADS * HEAD_PAD), F32),
                   jax.ShapeDtypeStruct((nq * WINDOW, SWA_KV_HEADS * HEAD_PAD), F32),
                   jax.ShapeDtypeStruct((nq * WINDOW, SWA_KV_HEADS * HEAD_PAD), F32),
                   jax.ShapeDtypeStruct((SWA_HEADS, WINDOW, 2 * WINDOW), F32),
                   jax.ShapeDtypeStruct((SWA_HEADS, 8, 128), F32)],
        compiler_params=_cparams("arbitrary", "arbitrary"),
    )(proj_b, proj_b, proj_b, proj_b, proj_b, bias, sinks, o, do, lse)


def _dproj_b(name, dq_swa, dq_mem, dk, dv, dk_edge, dv_edge, *, tq):
    S = dq_swa.shape[0]
    tq = min(tq, S)
    nq = S // tq

    def body(dqs_ref, dqm_ref, dk_ref, dv_ref, dke_ref, dve_ref, o_ref):
        i = pl.program_id(0)
        o_ref[:, 0:1024] = dqs_ref[...]
        o_ref[:, 1024:1536] = dqm_ref[...].astype(o_ref.dtype)
        o_ref[:, 1536:1792] = dk_ref[...].astype(o_ref.dtype)
        o_ref[:, 1792:2048] = dv_ref[...].astype(o_ref.dtype)

        @pl.when(i < nq - 1)
        def _():
            last = slice(tq - WINDOW, tq)
            o_ref[last, 1536:1792] = (dk_ref[last, :] + dke_ref[...]).astype(o_ref.dtype)
            o_ref[last, 1792:2048] = (dv_ref[last, :] + dve_ref[...]).astype(o_ref.dtype)

    edge = pl.BlockSpec((WINDOW, SWA_KV_HEADS * HEAD_PAD), lambda i: (jnp.minimum(i + 1, nq - 1), 0))
    return pl.pallas_call(
        body, name=name, grid=(nq,),
        in_specs=[pl.BlockSpec((tq, 1024), lambda i: (i, 0)), pl.BlockSpec((tq, 512), lambda i: (i, 0)),
                  pl.BlockSpec((tq, 256), lambda i: (i, 0)), pl.BlockSpec((tq, 256), lambda i: (i, 0)), edge, edge],
        out_specs=pl.BlockSpec((tq, 2048), lambda i: (i, 0)),
        out_shape=jax.ShapeDtypeStruct((S, 2048), BF16),
        compiler_params=_cparams("parallel"),
    )(dq_swa, dq_mem, dk, dv, dk_edge, dv_edge)


def _exchange(name, send, *, per_peer):
    shape = send.shape[1:] if per_peer else send.shape

    def body(send_ref, recv_ref, send_sems, recv_sems, local_sem):
        x, y, c = lax.axis_index("x"), lax.axis_index("y"), lax.axis_index("c")
        me = 4 * x + 2 * y + c
        own = pltpu.make_async_copy(send_ref.at[me] if per_peer else send_ref, recv_ref.at[me], local_sem)
        own.start()
        copies = []
        for k in range(1, N_DEV):
            px = 1 - x if (k >> 2) & 1 else x
            py = 1 - y if (k >> 1) & 1 else y
            pc = 1 - c if k & 1 else c
            peer = 4 * px + 2 * py + pc
            out = pltpu.make_async_remote_copy(
                src_ref=send_ref.at[peer] if per_peer else send_ref, dst_ref=recv_ref.at[me],
                send_sem=send_sems.at[k - 1], recv_sem=recv_sems.at[k - 1],
                device_id=(px, py, pc), device_id_type=pl.DeviceIdType.MESH)
            out.start()
            back = pltpu.make_async_remote_copy(
                src_ref=send_ref.at[me] if per_peer else send_ref, dst_ref=recv_ref.at[peer],
                send_sem=send_sems.at[k - 1], recv_sem=recv_sems.at[k - 1],
                device_id=(px, py, pc), device_id_type=pl.DeviceIdType.MESH)
            copies.append((out, back))
        for out, back in copies:
            out.wait_send()
            back.wait_recv()
        own.wait()

    return pl.pallas_call(
        body, name=name,
        in_specs=[pl.BlockSpec(memory_space=pl.ANY)],
        out_specs=pl.BlockSpec(memory_space=pl.ANY),
        out_shape=jax.ShapeDtypeStruct((N_DEV,) + tuple(shape), send.dtype),
        scratch_shapes=[pltpu.SemaphoreType.DMA((N_DEV - 1,)), pltpu.SemaphoreType.DMA((N_DEV - 1,)),
                        pltpu.SemaphoreType.DMA(())],
    )(send)


def _gather_forwarded(name, block):
    def body(x_ref, out_ref, send_sems, recv_sems, local_sem):
        x, y, c = lax.axis_index("x"), lax.axis_index("y"), lax.axis_index("c")
        me, sibling = (x, y, c), (x, y, 1 - c)
        chips = [(1 - x, y), (x, 1 - y), (1 - x, 1 - y)]

        def slot(px, py, pc):
            return out_ref.at[4 * px + 2 * py + pc]

        def copy(k, blk, to, src=None):
            return pltpu.make_async_remote_copy(
                src_ref=slot(*blk) if src is None else src, dst_ref=slot(*blk),
                send_sem=send_sems.at[k], recv_sem=recv_sems.at[k],
                device_id=to, device_id_type=pl.DeviceIdType.MESH)

        mine = pltpu.make_async_copy(x_ref, slot(*me), local_sem)
        mine.start()
        first = [copy(0, me, sibling, src=x_ref)]
        first += [copy(1 + j, me, (*chip, c), src=x_ref) for j, chip in enumerate(chips)]
        for cp in first:
            cp.start()
        passed = [copy(4 + j, (*chip, c), sibling) for j, chip in enumerate(chips)]
        for j, chip in enumerate(chips):
            copy(1 + j, (*chip, c), me).wait_recv()
            passed[j].start()
        copy(0, sibling, me).wait_recv()
        for j, chip in enumerate(chips):
            copy(4 + j, (*chip, 1 - c), me).wait_recv()
        for cp in first + passed:
            cp.wait_send()
        mine.wait()

    return pl.pallas_call(
        body, name=name,
        in_specs=[pl.BlockSpec(memory_space=pl.ANY)],
        out_specs=pl.BlockSpec(memory_space=pl.ANY),
        out_shape=jax.ShapeDtypeStruct((N_DEV,) + tuple(block.shape), block.dtype),
        scratch_shapes=[pltpu.SemaphoreType.DMA((N_DEV - 1,)), pltpu.SemaphoreType.DMA((N_DEV - 1,)),
                        pltpu.SemaphoreType.DMA(())],
    )(block)


def _sibling_swap(name, block):
    def body(x_ref, out_ref, send_sem, recv_sem):
        x, y, c = lax.axis_index("x"), lax.axis_index("y"), lax.axis_index("c")
        cp = pltpu.make_async_remote_copy(src_ref=x_ref, dst_ref=out_ref, send_sem=send_sem, recv_sem=recv_sem,
                                          device_id=(x, y, 1 - c), device_id_type=pl.DeviceIdType.MESH)
        cp.start()
        cp.wait()

    return pl.pallas_call(
        body, name=name,
        in_specs=[pl.BlockSpec(memory_space=pl.ANY)],
        out_specs=pl.BlockSpec(memory_space=pl.ANY),
        out_shape=jax.ShapeDtypeStruct(block.shape, block.dtype),
        scratch_shapes=[pltpu.SemaphoreType.DMA(()), pltpu.SemaphoreType.DMA(())],
    )(block)


def _chip_exchange(name, send):
    def body(send_ref, recv_ref, send_sems, recv_sems, local_sem):
        x, y, c = lax.axis_index("x"), lax.axis_index("y"), lax.axis_index("c")
        me = 2 * x + y
        own = pltpu.make_async_copy(send_ref.at[me], recv_ref.at[me], local_sem)
        own.start()
        copies = []
        for k in range(1, 4):
            px = 1 - x if (k >> 1) & 1 else x
            py = 1 - y if k & 1 else y
            peer = 2 * px + py
            out = pltpu.make_async_remote_copy(
                src_ref=send_ref.at[peer], dst_ref=recv_ref.at[me],
                send_sem=send_sems.at[k - 1], recv_sem=recv_sems.at[k - 1],
                device_id=(px, py, c), device_id_type=pl.DeviceIdType.MESH)
            out.start()
            back = pltpu.make_async_remote_copy(
                src_ref=send_ref.at[me], dst_ref=recv_ref.at[peer],
                send_sem=send_sems.at[k - 1], recv_sem=recv_sems.at[k - 1],
                device_id=(px, py, c), device_id_type=pl.DeviceIdType.MESH)
            copies.append((out, back))
        for out, back in copies:
            out.wait_send()
            back.wait_recv()
        own.wait()

    return pl.pallas_call(
        body, name=name,
        in_specs=[pl.BlockSpec(memory_space=pl.ANY)],
        out_specs=pl.BlockSpec(memory_space=pl.ANY),
        out_shape=jax.ShapeDtypeStruct(send.shape, send.dtype),
        scratch_shapes=[pltpu.SemaphoreType.DMA((3,)), pltpu.SemaphoreType.DMA((3,)), pltpu.SemaphoreType.DMA(())],
    )(send)


def _reduce_scatter(parts):
    lanes = 128
    c = lax.axis_index("c")

    def core_half(core):
        return jnp.concatenate(
            [lax.dynamic_index_in_dim(p.reshape(4, 2, p.shape[1], lanes), core, axis=1, keepdims=False)
             for p in parts], axis=1)

    mine = core_half(c)
    rows = mine.shape[1]
    mine = mine.reshape(4 * rows, lanes)
    theirs = core_half(1 - c).reshape(4 * rows, lanes)
    from_sibling = _sibling_swap("grads_to_sibling", theirs)
    tm = max(t for t in range(16, ADAM_TM + 1, 16) if rows % t == 0)
    chip_sum = _rowwise("grads_chip_sum", lambda a, b: (a.astype(F32) + b.astype(F32),),
                        [_rows(mine), _rows(from_sibling)], [((4 * rows, lanes), BF16, "rows")],
                        rows=4 * rows, tm=tm)[0]
    return _chip_exchange("scatter_grads", chip_sum.reshape(4, rows, lanes))


def _adam(name, recv, w, m, v, *, tm=None):
    R = w.shape[0]
    n_parts = recv.shape[0]
    tm = max(t for t in range(8, min(tm or ADAM_TM, R) + 1, 8) if R % t == 0)
    c1 = 1.0 / (1.0 - ADAM_B1 ** ADAM_STEP)
    c2 = 1.0 / (1.0 - ADAM_B2 ** ADAM_STEP)

    def body(r_ref, w_ref, m_ref, v_ref, g_ref, d_ref, nm_ref, nv_ref):
        g = r_ref[0].astype(F32)
        for j in range(1, n_parts):
            g = g + r_ref[j].astype(F32)
        wv = w_ref[...]
        nm = ADAM_B1 * m_ref[...] + (1.0 - ADAM_B1) * g
        nv = ADAM_B2 * v_ref[...] + (1.0 - ADAM_B2) * (g * g)
        m_hat = nm * c1
        v_hat = nv * c2
        g_ref[...] = g
        d_ref[...] = -ADAM_LR * (m_hat / (jnp.sqrt(v_hat) + ADAM_EPS) + ADAM_WD * wv)
        nm_ref[...] = nm
        nv_ref[...] = nv

    row = pl.BlockSpec((tm, 128), lambda i: (i, 0))
    return pl.pallas_call(
        body, name=name, grid=(R // tm,),
        in_specs=[pl.BlockSpec((n_parts, tm, 128), lambda i: (0, i, 0)), row, row, row],
        out_specs=[row, row, row, row],
        out_shape=[jax.ShapeDtypeStruct((R, 128), F32)] * 4,
        compiler_params=_cparams("parallel"),
    )(recv, w, m, v)


def _pack_rows(arrs):
    return jnp.concatenate([a.reshape(-1, 128) for a in arrs], axis=0)


def _unstack(g, shape, axis):
    t = jnp.moveaxis(g, 0, axis)
    return t.reshape(shape)


def _restack(full, axis):
    s = full.shape
    t = full.reshape(s[:axis] + (N_DEV, s[axis] // N_DEV) + s[axis + 1:])
    return jnp.moveaxis(t, axis, 0)


def _pad_heads(w, heads, hd, axis):
    s = w.shape
    t = w.reshape(s[:axis] + (heads, hd) + s[axis + 1:])
    pad = [(0, 0)] * t.ndim
    pad[axis + 1] = (0, HEAD_PAD - hd)
    t = jnp.pad(t, pad)
    return t.reshape(s[:axis] + (heads * HEAD_PAD,) + s[axis + 1:])


def _unpad_heads(w, heads, hd, axis):
    s = w.shape
    t = w.reshape(s[:axis] + (heads, HEAD_PAD) + s[axis + 1:])
    t = lax.slice_in_dim(t, 0, hd, axis=axis + 1)
    return t.reshape(s[:axis] + (heads * hd,) + s[axis + 1:])


def _layer_weights(full, l):
    w_in = full["w_in"][l]
    cq, kva, qs, ks, vs, qm, gates = (w_in[:, 0:256], w_in[:, 256:416], w_in[:, 416:928], w_in[:, 928:1056],
                                       w_in[:, 1056:1184], w_in[:, 1184:1696], w_in[:, 1696:4768])
    wa = jnp.concatenate([gates, cq, jnp.pad(kva, ((0, 0), (0, 96)))], axis=1)
    wb = jnp.concatenate([_pad_heads(qs, SWA_HEADS, SWA_HD, 1), qm, _pad_heads(ks, SWA_KV_HEADS, SWA_HD, 1),
                          _pad_heads(vs, SWA_KV_HEADS, SWA_HD, 1)], axis=1)
    wuq = _pad_heads(full["w_uq"][l], MLA_HEADS, MLA_NOPE + MLA_ROPE, 1)
    ukv = full["w_ukv"][l].reshape(MLA_KV_LORA, MLA_HEADS, MLA_NOPE + MLA_V)
    wuk = _pad_heads(ukv[:, :, :MLA_NOPE].reshape(MLA_KV_LORA, -1), MLA_HEADS, MLA_NOPE, 1)
    wuv = _pad_heads(ukv[:, :, MLA_NOPE:].reshape(MLA_KV_LORA, -1), MLA_HEADS, MLA_V, 1)
    wo_mla = _pad_heads(full["w_o_mla"][l], MLA_HEADS, MLA_V, 0)
    wo_swa = _pad_heads(full["w_o_swa"][l], SWA_HEADS, SWA_HD, 0)
    wo_mem = full["w_o_mem"][l]
    w = dict(wag=wa[:, :3072], wat=wa[:, 3072:], wb=wb, wuq=wuq, wuk=wuk, wuv=wuv, wo_mla=wo_mla, wo_swa=wo_swa,
             wo_mem=wo_mem, wmem=full["w_mem_kv"][l], wout=full["w_out"][l], wup=full["w_up"][l],
             wdown=full["w_down"][l])
    w.update({k + "_t": v.T for k, v in w.items()})
    return w


def _layer_weight_grads(g):
    dwa_g, dwa_t, dwb = g["wag"], g["wat"], g["wb"]
    d_in = jnp.concatenate([
        dwa_t[:, 0:256], dwa_t[:, 256:416],
        _unpad_heads(dwb[:, 0:1024], SWA_HEADS, SWA_HD, 1),
        _unpad_heads(dwb[:, 1536:1792], SWA_KV_HEADS, SWA_HD, 1),
        _unpad_heads(dwb[:, 1792:2048], SWA_KV_HEADS, SWA_HD, 1),
        dwb[:, 1024:1536], dwa_g], axis=1)
    duk = _unpad_heads(g["wuk"], MLA_HEADS, MLA_NOPE, 1).reshape(MLA_KV_LORA, MLA_HEADS, MLA_NOPE)
    duv = _unpad_heads(g["wuv"], MLA_HEADS, MLA_V, 1).reshape(MLA_KV_LORA, MLA_HEADS, MLA_V)
    return dict(
        w_in=d_in,
        w_uq=_unpad_heads(g["wuq"], MLA_HEADS, MLA_NOPE + MLA_ROPE, 1),
        w_ukv=jnp.concatenate([duk, duv], axis=2).reshape(MLA_KV_LORA, -1),
        w_mem_kv=g["wmem"],
        w_o_mla=_unpad_heads(g["wo_mla"], MLA_HEADS, MLA_V, 0),
        w_o_swa=_unpad_heads(g["wo_swa"], SWA_HEADS, SWA_HD, 0),
        w_o_mem=g["wo_mem"], w_out=g["wout"], w_up=g["wup"], w_down=g["wdown"])


def _rope_tables(S):
    pos = jnp.arange(S, dtype=F32)
    inv = 1.0 / (ROPE_THETA ** (jnp.arange(0, MLA_ROPE, 2, dtype=F32) / MLA_ROPE))
    ang = pos[:, None] * inv[None, :]
    cos, sin = jnp.cos(ang), jnp.sin(ang)
    z16 = jnp.zeros((S, 16), F32)
    z32 = jnp.zeros((S, 32), F32)
    c = jnp.concatenate([jnp.ones((S, 64), F32), cos, cos, z32], axis=1)
    ck = jnp.concatenate([jnp.zeros((S, 64), F32), cos, cos, z32], axis=1)
    s1 = jnp.concatenate([jnp.zeros((S, 80), F32), sin, z32], axis=1)
    s2 = jnp.concatenate([jnp.zeros((S, 64), F32), -sin, z16, z32], axis=1)
    return c, ck, s1, s2


def _t5_bucket(dist):
    n = jnp.maximum(dist, 0)
    max_exact = REL_BUCKETS // 2
    nf = jnp.maximum(n, 1).astype(F32)
    large = max_exact + (jnp.log(nf / max_exact) / math.log(REL_MAX_DIST / max_exact)
                         * (REL_BUCKETS - max_exact)).astype(jnp.int32)
    large = jnp.minimum(large, REL_BUCKETS - 1)
    return jnp.where(n < max_exact, n, large)


def _bias_onehot():
    qi = jnp.arange(WINDOW)[:, None]
    kj = jnp.arange(2 * WINDOW)[None, :]
    dist = qi + WINDOW - kj
    valid = (dist >= 0) & (dist < WINDOW)
    bucket = _t5_bucket(dist)
    onehot = (bucket[None] == jnp.arange(REL_BUCKETS)[:, None, None]) & valid[None]
    return (onehot.reshape(REL_BUCKETS, -1).astype(F32),
            jnp.where(valid, 0.0, NEG).astype(F32).reshape(1, -1))


def _rstd(x):
    return lax.rsqrt(jnp.mean(x * x, axis=-1, keepdims=True) + EPS)


def _norm_bwd(dh, x, g):
    r = _rstd(x)
    xh = x * r
    w = dh * g
    dx = r * (w - xh * jnp.mean(w * xh, axis=-1, keepdims=True))
    return dx, jnp.sum(dh * xh, axis=0, keepdims=True)


def _tile_lanes(t, n):
    return jnp.tile(t, (1, n // t.shape[1])) if n != t.shape[1] else t


def _rope_fwd(a, c, s1, s2):
    n = a.shape[1]
    return (a * _tile_lanes(c, n) + pltpu.roll(a, 16, 1) * _tile_lanes(s1, n)
            + pltpu.roll(a, n - 16, 1) * _tile_lanes(s2, n))


def _rope_bwd(d, c, s1, s2):
    n = d.shape[1]
    return (d * _tile_lanes(c, n) + pltpu.roll(d * _tile_lanes(s1, n), n - 16, 1)
            + pltpu.roll(d * _tile_lanes(s2, n), 16, 1))


def _sigmoid(x):
    return 1.0 / (1.0 + jnp.exp(-x))


def _rmsnorm(name, x, g, dtype):
    def fn(xv, gv):
        return ((xv * _rstd(xv)) * gv,)
    return _rowwise(name, fn, [_rows(x), _full(g)], [(x.shape, dtype, "rows")], rows=x.shape[0])[0]


def _residual_norm_bwd(name, dres, dh, x, g):
    def fn(dr, dhv, xv, gv):
        dx, dg = _norm_bwd(dhv, xv, gv)
        return dr + dx, dg
    return _rowwise(name, fn, [_rows(dres), _rows(dh), _rows(x), _full(g)],
                    [(x.shape, F32, "rows"), (g.shape, F32, "acc")], rows=x.shape[0])


def _layer_fwd(l, x, mem, w, p, tabs, swa_bias, S):
    c, ck, s1, s2 = tabs
    n = f"l{l}_"
    h = _rmsnorm(n + "attn_norm", x, p["attn_norm"], BF16)
    gates = _mm(n + "proj_gates", h, w["wag"], [BF16], tn=1024)
    proj_a = _mm(n + "proj_tail", h, w["wat"], [F32])
    proj_b = _mm(n + "proj_b", h, w["wb"], [BF16])

    def prep(cq, kva, qn, kvn, ckv, s1v, s2v):
        cqn = cq * _rstd(cq) * qn
        ckv_ = kva[:, :128]
        ckvn = ckv_ * _rstd(ckv_) * kvn
        pe = pltpu.roll(kva[:, 128:], 64, 1)
        return cqn, ckvn, _rope_fwd(pe, ckv, s1v, s2v)

    cqn, ckvn, kpe = _rowwise(
        n + "mla_prep", prep,
        [_rows(proj_a, 256, 0), _rows(proj_a, 256, 1), _full(p["mla_q_norm"]), _full(p["mla_kv_norm"]),
         _rows(ck), _rows(s1), _rows(s2)],
        [((S, 256), BF16, "rows"), ((S, 128), BF16, "rows"), ((S, 128), F32, "rows")], rows=S)

    q_mla = _mm(n + "q_mla", cqn, w["wuq"], [BF16],
                epi=lambda acc, cv, s1v, s2v: (_rope_fwd(acc, cv, s1v, s2v) * (MLA_SCALE * LOG2E),),
                extras=[(c, "m"), (s1, "m"), (s2, "m")])
    k_mla = _mm(n + "k_mla", ckvn, w["wuk"], [BF16],
                epi=lambda acc, kp: (acc + _tile_lanes(kp, acc.shape[1]),), extras=[(kpe, "m")])
    den = ((jnp.arange(MLA_HEADS * HEAD_PAD) % HEAD_PAD) // 2 == DEN_LANE // 2).astype(F32)[None]
    v_mla = _mm(n + "v_mla", ckvn, w["wuv"], [BF16], epi=lambda acc, dv: (acc + dv,), extras=[(den, "n")])
    o_mla, lse_mla = _causal_fwd(n + "mla_fwd", q_mla, k_mla, v_mla, heads=MLA_HEADS, tile=MLA_TILE,
                                 chunk=MLA_CHUNK_FWD)
    o_swa, lse_swa = _swa_fwd(n + "swa_fwd", proj_b, swa_bias, p["sinks"], tq=SWA_TQ)
    mn = _rmsnorm(n + "mem_norm", mem, p["mem_norm"], BF16)
    kvm = _mm(n + "kv_mem", mn, w["wmem"], [BF16])
    o_mem, lse_mem = _mem_fwd(n + "mem_fwd", proj_b, kvm, tq=MEM_TQ, chunk=MEM_CHUNK)
    t0 = _mm(n + "t_mla", o_mla, w["wo_mla"], [BF16], tn=1024)
    t1 = _mm(n + "t_swa", o_swa, w["wo_swa"], [BF16], tn=1024)
    t2 = _mm(n + "t_mem", o_mem, w["wo_mem"], [BF16], tn=1024)

    def merge(g0, g1, g2, bg, a0, a1, a2):
        y = (_sigmoid(g0 + bg[:, 0:1024]) * a0 + _sigmoid(g1 + bg[:, 1024:2048]) * a1
             + _sigmoid(g2 + bg[:, 2048:3072]) * a2)
        return (y,)

    y = _rowwise(n + "merge", merge,
                 [_rows(gates, 1024, 0), _rows(gates, 1024, 1), _rows(gates, 1024, 2), _full(p["b_gate"]),
                  _rows(t0), _rows(t1), _rows(t2)], [((S, D_MODEL), BF16, "rows")], rows=S)[0]
    x1 = _mm(n + "out_proj", y, w["wout"], [F32], epi=lambda acc, r: (acc + r,), extras=[(x, "mn")])
    h2 = _rmsnorm(n + "mlp_norm", x1, p["mlp_norm"], BF16)
    act = _mm(n + "mlp_up", h2, w["wup"], [BF16], epi=lambda acc: (jnp.square(jnp.maximum(acc, 0.0)),), tn=1024)
    x2 = _mm(n + "mlp_down", act, w["wdown"], [F32], epi=lambda acc, r: (acc + r,), extras=[(x1, "mn")])
    saved = dict(x=x, h=h, gates=gates, proj_a=proj_a, proj_b=proj_b, cqn=cqn, ckvn=ckvn, q_mla=q_mla, k_mla=k_mla, v_mla=v_mla,
                 o_mla=o_mla, lse_mla=lse_mla, o_swa=o_swa, lse_swa=lse_swa, mn=mn, kvm=kvm, o_mem=o_mem,
                 lse_mem=lse_mem, t0=t0, t1=t1, t2=t2, y=y, x1=x1, h2=h2, act=act)
    return x2, saved


def _layer_bwd(l, dx2, mem, w, p, tabs, swa_bias, sv, S):
    c, ck, s1, s2 = tabs
    n = f"l{l}_b_"
    gw = {}
    gs = {}
    du = _mm(n + "d_act", dx2, w["wdown_t"], [BF16],
             epi=lambda acc, av: (acc * (2.0 * jnp.sqrt(av.astype(F32))),), extras=[(sv["act"], "mn")], tn=1024)
    gw["wdown"] = _mm_tn(n + "g_wdown", sv["act"], dx2)
    gw["wup"] = _mm_tn(n + "g_wup", sv["h2"], du)
    dh2 = _mm(n + "d_h2", du, w["wup_t"], [F32])
    dx1, gs["mlp_norm"] = _residual_norm_bwd(n + "mlp_norm", dx2, dh2, sv["x1"], p["mlp_norm"])
    gw["wout"] = _mm_tn(n + "g_wout", sv["y"], dx1)
    dy = _mm(n + "d_y", dx1, w["wout_t"], [F32])

    def merge_bwd(dyv, g0, g1, g2, bg, a0, a1, a2):
        outs, dgs = [], []
        for b, (gv, av) in enumerate(((g0, a0), (g1, a1), (g2, a2))):
            sg = _sigmoid(gv + bg[:, b * 1024:(b + 1) * 1024])
            outs.append(dyv * sg)
            dgs.append(dyv * av * sg * (1.0 - sg))
        dg = jnp.concatenate(dgs, axis=1)
        return outs[0], outs[1], outs[2], dg, jnp.sum(dg, axis=0, keepdims=True)

    pa = sv["proj_a"]
    gt = sv["gates"]
    dt0, dt1, dt2, dgates, gs["b_gate"] = _rowwise(
        n + "merge", merge_bwd,
        [_rows(dy), _rows(gt, 1024, 0), _rows(gt, 1024, 1), _rows(gt, 1024, 2), _full(p["b_gate"]),
         _rows(sv["t0"]), _rows(sv["t1"]), _rows(sv["t2"])],
        [((S, D_MODEL), BF16, "rows")] * 3 + [((S, 3 * D_MODEL), BF16, "rows"), ((1, 3 * D_MODEL), F32, "acc")],
        rows=S)
    gw["wo_mla"] = _mm_tn(n + "g_wo_mla", sv["o_mla"], dt0)
    gw["wo_swa"] = _mm_tn(n + "g_wo_swa", sv["o_swa"], dt1)
    gw["wo_mem"] = _mm_tn(n + "g_wo_mem", sv["o_mem"], dt2)
    do_mla = _mm(n + "d_o_mla", dt0, w["wo_mla_t"], [BF16], epi=lambda acc, ov: (_with_neg_delta(acc, ov),),
                 extras=[(sv["o_mla"], "mn")], tn=MLA_HEADS * HEAD_PAD)
    do_swa = _mm(n + "d_o_swa", dt1, w["wo_swa_t"], [BF16])
    do_mem = _mm(n + "d_o_mem", dt2, w["wo_mem_t"], [BF16])
    pb = sv["proj_b"]
    dq_mla, dk_mla, dv_mla = _causal_bwd(
        n + "mla_bwd", sv["q_mla"], sv["k_mla"], sv["v_mla"], do_mla, sv["lse_mla"], heads=MLA_HEADS,
        tile=MLA_TILE, chunk=MLA_CHUNK)
    dq_swa, dk_swa, dv_swa, dk_edge, dv_edge, dbias, dsink = _swa_bwd(
        n + "swa_bwd", pb, swa_bias, p["sinks"], sv["o_swa"], do_swa, sv["lse_swa"], tq=SWA_TQ)
    dq_mem, dkvm = _mem_bwd(n + "mem_bwd", pb, sv["kvm"], sv["o_mem"], do_mem, sv["lse_mem"], tq=MEM_TQ,
                            chunk=MEM_CHUNK)
    gs["dbias"] = dbias
    gs["sinks"] = dsink[:, 0, 0]
    gw["wmem"] = _mm_tn(n + "g_wmem", sv["mn"], dkvm)
    dmn = _mm(n + "d_mn", dkvm, w["wmem_t"], [F32])
    _, gs["mem_norm"] = _residual_norm_bwd(n + "mem_norm", dmn, dmn, mem, p["mem_norm"])
    dq_pre = _rowwise(n + "q_unrope", lambda d, cv, s1v, s2v: (_rope_bwd(d * MLA_SCALE, cv, s1v, s2v),),
                      [_rows(dq_mla), _rows(c), _rows(s1), _rows(s2)], [((S, 1024), BF16, "rows")], rows=S)[0]
    gw["wuq"] = _mm_tn(n + "g_wuq", sv["cqn"], dq_pre)
    gw["wuk"] = _mm_tn(n + "g_wuk", sv["ckvn"], dk_mla)
    gw["wuv"] = _mm_tn(n + "g_wuv", sv["ckvn"], dv_mla)
    dcqn = _mm(n + "d_cqn", dq_pre, w["wuq_t"], [F32])
    dckvn = _mm(n + "d_ckvn_k", dk_mla, w["wuk_t"], [F32])
    dckvn = _mm(n + "d_ckvn_v", dv_mla, w["wuv_t"], [F32], epi=lambda acc, r: (acc + r,), extras=[(dckvn, "mn")])

    def mla_norm_bwd(dcq_n, dckv_n, dk, cq, kva, qn, kvn, ckv, s1v, s2v):
        dcq, dqn = _norm_bwd(dcq_n, cq, qn)
        dckv, dkvn = _norm_bwd(dckv_n, kva[:, :128], kvn)
        dkpe = dk[:, 0:128]
        for hh in range(1, MLA_HEADS):
            dkpe = dkpe + dk[:, hh * 128:(hh + 1) * 128]
        dpe = pltpu.roll(_rope_bwd(dkpe, ckv, s1v, s2v), 64, 1)
        return jnp.concatenate([dcq, dckv, dpe], axis=1), dqn, dkvn

    dtail, gs["mla_q_norm"], gs["mla_kv_norm"] = _rowwise(
        n + "mla_norm", mla_norm_bwd,
        [_rows(dcqn), _rows(dckvn), _rows(dk_mla), _rows(pa, 256, 0), _rows(pa, 256, 1),
         _full(p["mla_q_norm"]), _full(p["mla_kv_norm"]), _rows(ck), _rows(s1), _rows(s2)],
        [((S, 512), BF16, "rows"), ((1, 256), F32, "acc"), ((1, 128), F32, "acc")], rows=S)

    dproj_b = _dproj_b(n + "dproj_b", dq_swa, dq_mem, dk_swa, dv_swa, dk_edge, dv_edge, tq=SWA_TQ)
    h = sv["h"]
    gw["wag"] = _mm_tn(n + "g_wa_gates", h, dgates)
    gw["wat"] = _mm_tn(n + "g_wa_tail", h, dtail)
    gw["wb"] = _mm_tn(n + "g_wb", h, dproj_b)
    dh = _mm(n + "d_h_gates", dgates, w["wag_t"], [F32])
    dh = _mm(n + "d_h_tail", dtail, w["wat_t"], [F32], epi=lambda acc, r: (acc + r,), extras=[(dh, "mn")])
    dh = _mm(n + "d_h_b", dproj_b, w["wb_t"], [F32], epi=lambda acc, r: (acc + r,), extras=[(dh, "mn")])
    dx, gs["attn_norm"] = _residual_norm_bwd(n + "attn_norm", dx1, dh, sv["x"], p["attn_norm"])
    return dx, gw, gs


def _local_step(x, mem, loss_target, full, small):
    S = x.shape[0]
    tabs = _rope_tables(S)
    onehot, band = _bias_onehot()
    hi = lax.Precision.HIGHEST
    swa_bias = _mm("swa_bias", small["rel_bias"].T, onehot, [F32], epi=lambda acc, mk: (acc + mk,),
                   extras=[(band, "n")], cast=None, precision=hi, tn=8192).reshape(SWA_HEADS, WINDOW, 2 * WINDOW)
    ws, ps = [], []
    for l in range(DEPTH):
        ws.append(_layer_weights(full, l))
        ps.append(dict(
            attn_norm=small["attn_norm"][l][None], mem_norm=small["mem_norm"][l][None],
            b_gate=small["b_gate"][l][None], mla_q_norm=small["mla_q_norm"][l][None],
            mla_kv_norm=small["mla_kv_norm"][l][None], mlp_norm=small["mlp_norm"][l][None],
            sinks=jnp.broadcast_to(small["attn_sinks"][l][:, None, None], (SWA_HEADS, 8, 128))))
    saved = []
    xc = x
    for l in range(DEPTH):
        xc, sv = _layer_fwd(l, xc, mem, ws[l], ps[l], tabs, swa_bias, S)
        saved.append(sv)

    fn_g = small["final_norm"][None]

    def loss_fn(xv, gv, tv):
        r = _rstd(xv)
        xh = xv * r
        err = xh * gv - tv
        dyv = err * (1.0 / D_MODEL)
        wv = dyv * gv
        dx = r * (wv - xh * jnp.mean(wv * xh, axis=-1, keepdims=True))
        part = 0.5 * jnp.sum(err * err) * (1.0 / D_MODEL)
        return dx, jnp.sum(dyv * xh, axis=0, keepdims=True), jnp.zeros((8, 128), F32) + part

    dx, g_final, loss_acc = _rowwise(
        "loss", loss_fn, [_rows(xc), _full(fn_g), _rows(loss_target)],
        [((S, D_MODEL), F32, "rows"), ((1, D_MODEL), F32, "acc"), ((8, 128), F32, "acc")], rows=S)

    gws, gss = [None] * DEPTH, [None] * DEPTH
    for l in reversed(range(DEPTH)):
        dx, gw, gs = _layer_bwd(l, dx, mem, ws[l], ps[l], tabs, swa_bias, saved[l], S)
        gws[l] = _layer_weight_grads(gw)
        gss[l] = gs

    dbias = (gss[0]["dbias"] + gss[1]["dbias"]).reshape(SWA_HEADS, -1)
    g_rel = _mm("g_rel_bias", dbias, onehot.T, [F32], cast=None, precision=hi, tk=8192).T
    wgrads = {k: jnp.stack([gws[l][k] for l in range(DEPTH)]) for k in gws[0]}
    sgrads = dict(
        rel_bias=g_rel,
        final_norm=g_final[0],
        attn_sinks=jnp.stack([gss[l]["sinks"] for l in range(DEPTH)]),
        **{k: jnp.concatenate([gss[l][k] for l in range(DEPTH)], axis=0)
           for k in ("attn_norm", "mem_norm", "b_gate", "mla_q_norm", "mla_kv_norm", "mlp_norm")})
    return loss_acc[0, 0], dx, wgrads, sgrads


def _pack_small(vals, loss):
    rows = []
    for name, shape in SMALL:
        flat = vals[name].astype(F32).reshape(-1)
        pad = (-flat.shape[0]) % 1024
        rows.append(jnp.pad(flat, (0, pad)).reshape(-1, 128))
    rows.append(jnp.zeros((8, 128), F32) + loss)
    return jnp.concatenate(rows, axis=0)


def _unpack_small(packed):
    out, r = {}, 0
    for name, shape in SMALL:
        size = math.prod(shape)
        nrows = 8 * -(-size // 1024)
        out[name] = packed[r:r + nrows].reshape(-1)[:size].reshape(shape)
        r += nrows
    return out, packed[r, 0]


def kernel(x, mem, rel_bias, attn_norm, mem_norm, w_in, b_gate, mla_q_norm, w_uq, mla_kv_norm, w_ukv, attn_sinks, w_mem_kv, w_o_mla, w_o_swa, w_o_mem, w_out, mlp_norm, w_up, w_down, final_norm, loss_target, m_rel_bias, m_attn_norm, m_mem_norm, m_w_in, m_b_gate, m_mla_q_norm, m_w_uq, m_mla_kv_norm, m_w_ukv, m_attn_sinks, m_w_mem_kv, m_w_o_mla, m_w_o_swa, m_w_o_mem, m_w_out, m_mlp_norm, m_w_up, m_w_down, m_final_norm, v_rel_bias, v_attn_norm, v_mem_norm, v_w_in, v_b_gate, v_mla_q_norm, v_w_uq, v_mla_kv_norm, v_w_ukv, v_attn_sinks, v_w_mem_kv, v_w_o_mla, v_w_o_swa, v_w_o_mem, v_w_out, v_mlp_norm, v_w_up, v_w_down, v_final_norm):
    wv = dict(rel_bias=rel_bias, attn_norm=attn_norm, mem_norm=mem_norm, w_in=w_in, b_gate=b_gate,
              mla_q_norm=mla_q_norm, w_uq=w_uq, mla_kv_norm=mla_kv_norm, w_ukv=w_ukv, attn_sinks=attn_sinks,
              w_mem_kv=w_mem_kv, w_o_mla=w_o_mla, w_o_swa=w_o_swa, w_o_mem=w_o_mem, w_out=w_out,
              mlp_norm=mlp_norm, w_up=w_up, w_down=w_down, final_norm=final_norm)
    mv = dict(rel_bias=m_rel_bias, attn_norm=m_attn_norm, mem_norm=m_mem_norm, w_in=m_w_in, b_gate=m_b_gate,
              mla_q_norm=m_mla_q_norm, w_uq=m_w_uq, mla_kv_norm=m_mla_kv_norm, w_ukv=m_w_ukv,
              attn_sinks=m_attn_sinks, w_mem_kv=m_w_mem_kv, w_o_mla=m_w_o_mla, w_o_swa=m_w_o_swa,
              w_o_mem=m_w_o_mem, w_out=m_w_out, mlp_norm=m_mlp_norm, w_up=m_w_up, w_down=m_w_down,
              final_norm=m_final_norm)
    vv = dict(rel_bias=v_rel_bias, attn_norm=v_attn_norm, mem_norm=v_mem_norm, w_in=v_w_in, b_gate=v_b_gate,
              mla_q_norm=v_mla_q_norm, w_uq=v_w_uq, mla_kv_norm=v_mla_kv_norm, w_ukv=v_w_ukv,
              attn_sinks=v_attn_sinks, w_mem_kv=v_w_mem_kv, w_o_mla=v_w_o_mla, w_o_swa=v_w_o_swa,
              w_o_mem=v_w_o_mem, w_out=v_w_out, mlp_norm=v_mlp_norm, w_up=v_w_up, w_down=v_w_down,
              final_norm=v_final_norm)

    shard_rows = [math.prod(_shard_shape(shape, axis)) // 128 for _, shape, axis in WSPECS]
    gathered = _gather_forwarded("gather_weights", _pack_rows([wv[name].astype(BF16) for name, _, _ in WSPECS]))
    full, r = {}, 0
    for (name, shape, axis), nr in zip(WSPECS, shard_rows):
        full[name] = _unstack(gathered[:, r:r + nr].reshape((N_DEV,) + _shard_shape(shape, axis)), shape, axis)
        r += nr

    loss_part, grad_x, wgrads, sgrads = _local_step(x[0], mem[0], loss_target[0], full,
                                                    {name: wv[name] for name, _ in SMALL})

    recv = _reduce_scatter([_restack(wgrads[name], axis).astype(BF16).reshape(N_DEV, -1, 128)
                            for name, _, axis in WSPECS])
    outs = _adam("adam_sharded", recv, *[_pack_rows([d[name] for name, _, _ in WSPECS]) for d in (wv, mv, vv)])
    res = {}
    r = 0
    for (name, shape, axis), nr in zip(WSPECS, shard_rows):
        res[name] = [o[r:r + nr].reshape(_shard_shape(shape, axis)) for o in outs]
        r += nr

    small_recv = _exchange("gather_small", _pack_small(sgrads, loss_part), per_peer=False)
    zero = jnp.zeros((), F32)
    souts = _adam("adam_small", small_recv, *[_pack_small(d, zero) for d in (wv, mv, vv)])
    loss = None
    for i, o in enumerate(souts):
        vals, extra = _unpack_small(o)
        if i == 0:
            loss = extra
        for name, _ in SMALL:
            res.setdefault(name, []).append(vals[name])

    out = [loss, grad_x[None]]
    for i in range(4):
        out.extend(res[name][i] for name in WEIGHT_ORDER)
    return tuple(out)
```

```python
import math

import jax
import jax.numpy as jnp
from jax import lax
from jax.experimental import pallas as pl
from jax.experimental.pallas import tpu as pltpu

F32 = jnp.float32
BF16 = jnp.bfloat16

N_DEV = 8
D_MODEL = 1024
DEPTH = 2
MLA_HEADS = 8
MLA_Q_LORA = 256
MLA_KV_LORA = 128
MLA_NOPE = 64
MLA_ROPE = 32
MLA_V = 64
ROPE_THETA = 10000.0
SWA_HEADS = 8
SWA_KV_HEADS = 2
SWA_HD = 64
WINDOW = 128
REL_BUCKETS = 32
REL_MAX_DIST = 128
MEM_LEN = 256
MEM_HEADS = 4
MEM_HD = 128
D_FF = 4 * D_MODEL
EPS = 1e-6
HEAD_PAD = 128
ADAM_LR = 0.001
ADAM_B1 = 0.9
ADAM_B2 = 0.999
ADAM_EPS = 1e-08
ADAM_WD = 0.01
ADAM_STEP = 10

NEG = -1e30
VMEM_LIMIT = 48 * 1024 * 1024

MM_TM = 1024
MM_TN = 1024
MM_TK = 1024
TN_T1 = 1024
TN_TN = 1024
TN_TS = 1024
ROW_TM = 256
MLA_TILE = 4096
MLA_TILE_BWD = 2048
MLA_CHUNK = 256
MLA_CHUNK_FWD = 512
MLA_QK = MLA_NOPE + MLA_ROPE
MLA_SCALE = MLA_QK ** -0.5
LOG2E = math.log2(math.e)
DEN_LANE = MLA_V
SWA_TQ = 1024
MEM_TQ = 1024
MEM_CHUNK = 256
ADAM_TM = 1200

WSPECS = (
    ("w_in", (DEPTH, D_MODEL, 4768), 2),
    ("w_uq", (DEPTH, MLA_Q_LORA, 768), 2),
    ("w_ukv", (DEPTH, MLA_KV_LORA, 1024), 2),
    ("w_mem_kv", (DEPTH, D_MODEL, 1024), 1),
    ("w_o_mla", (DEPTH, 512, D_MODEL), 2),
    ("w_o_swa", (DEPTH, 512, D_MODEL), 2),
    ("w_o_mem", (DEPTH, 512, D_MODEL), 2),
    ("w_out", (DEPTH, D_MODEL, D_MODEL), 1),
    ("w_up", (DEPTH, D_MODEL, D_FF), 2),
    ("w_down", (DEPTH, D_FF, D_MODEL), 1),
)
SMALL = (
    ("rel_bias", (REL_BUCKETS, SWA_HEADS)),
    ("attn_norm", (DEPTH, D_MODEL)),
    ("mem_norm", (DEPTH, D_MODEL)),
    ("b_gate", (DEPTH, 3 * D_MODEL)),
    ("mla_q_norm", (DEPTH, MLA_Q_LORA)),
    ("mla_kv_norm", (DEPTH, MLA_KV_LORA)),
    ("attn_sinks", (DEPTH, SWA_HEADS)),
    ("mlp_norm", (DEPTH, D_MODEL)),
    ("final_norm", (D_MODEL,)),
)
WEIGHT_ORDER = ("rel_bias", "attn_norm", "mem_norm", "w_in", "b_gate", "mla_q_norm", "w_uq", "mla_kv_norm",
                "w_ukv", "attn_sinks", "w_mem_kv", "w_o_mla", "w_o_swa", "w_o_mem", "w_out", "mlp_norm",
                "w_up", "w_down", "final_norm")


def _cparams(*sem):
    return pltpu.CompilerParams(dimension_semantics=sem, vmem_limit_bytes=VMEM_LIMIT)


def _shard_shape(shape, axis):
    s = list(shape)
    s[axis] //= N_DEV
    return tuple(s)


def _mm(name, a, b, out_dtypes, *, epi=None, extras=(), a_fn=None, cast=BF16, precision=None,
        tm=None, tn=None, tk=None):
    M, K = a.shape
    K2, N = b.shape
    assert K == K2, (name, a.shape, b.shape)
    tm = min(tm or MM_TM, M)
    tn = min(tn or MM_TN, N)
    tk = min(tk or MM_TK, K)
    assert M % tm == 0 and N % tn == 0 and K % tk == 0, (name, a.shape, b.shape, tm, tn, tk)
    nk = K // tk
    n_ex = len(extras)
    n_out = len(out_dtypes)

    def body(*refs):
        a_ref, b_ref = refs[0], refs[1]
        ex_refs = refs[2:2 + n_ex]
        out_refs = refs[2 + n_ex:2 + n_ex + n_out]
        av = a_ref[...]
        if a_fn is not None:
            av = a_fn(av)
        bv = b_ref[...]
        if cast is not None:
            av = av.astype(cast)
            bv = bv.astype(cast)
        part = jnp.dot(av, bv, preferred_element_type=F32, precision=precision)

        def finish(acc):
            outs = epi(acc, *[r[...] for r in ex_refs]) if epi is not None else (acc,)
            for r, o in zip(out_refs, outs):
                r[...] = o.astype(r.dtype)

        if nk == 1:
            finish(part)
        else:
            acc_ref = refs[-1]
            k = pl.program_id(2)

            @pl.when(k == 0)
            def _():
                acc_ref[...] = part

            @pl.when(k > 0)
            def _():
                acc_ref[...] += part

            @pl.when(k == nk - 1)
            def _():
                finish(acc_ref[...])

    in_specs = [pl.BlockSpec((tm, tk), lambda i, j, k: (i, k)),
                pl.BlockSpec((tk, tn), lambda i, j, k: (k, j))]
    for arr, kind in extras:
        if kind == "mn":
            in_specs.append(pl.BlockSpec((tm, tn), lambda i, j, k: (i, j)))
        elif kind == "m":
            in_specs.append(pl.BlockSpec((tm, arr.shape[1]), lambda i, j, k: (i, 0)))
        else:
            in_specs.append(pl.BlockSpec((1, tn), lambda i, j, k: (0, j)))
    outs = pl.pallas_call(
        body, name=name, grid=(M // tm, N // tn, nk),
        in_specs=in_specs,
        out_specs=[pl.BlockSpec((tm, tn), lambda i, j, k: (i, j)) for _ in out_dtypes],
        out_shape=[jax.ShapeDtypeStruct((M, N), dt) for dt in out_dtypes],
        scratch_shapes=[pltpu.VMEM((tm, tn), F32)] if nk > 1 else [],
        compiler_params=_cparams("parallel", "parallel", "arbitrary"),
    )(a, b, *[arr for arr, _ in extras])
    return outs[0] if n_out == 1 else outs


def _mm_tn(name, a, b, *, t1=None, tn=None, ts=None):
    S, K1 = a.shape
    S2, N = b.shape
    assert S == S2, (name, a.shape, b.shape)
    t1 = min(t1 or TN_T1, K1)
    tn = min(tn or TN_TN, N)
    ts = min(ts or TN_TS, S)
    assert K1 % t1 == 0 and N % tn == 0 and S % ts == 0, (name, a.shape, b.shape)

    def body(a_ref, b_ref, o_ref):
        s = pl.program_id(2)
        part = lax.dot_general(a_ref[...].astype(BF16), b_ref[...].astype(BF16),
                               (((0,), (0,)), ((), ())), preferred_element_type=F32)

        @pl.when(s == 0)
        def _():
            o_ref[...] = part

        @pl.when(s > 0)
        def _():
            o_ref[...] += part

    return pl.pallas_call(
        body, name=name, grid=(K1 // t1, N // tn, S // ts),
        in_specs=[pl.BlockSpec((ts, t1), lambda i, j, s: (s, i)),
                  pl.BlockSpec((ts, tn), lambda i, j, s: (s, j))],
        out_specs=pl.BlockSpec((t1, tn), lambda i, j, s: (i, j)),
        out_shape=jax.ShapeDtypeStruct((K1, N), F32),
        compiler_params=_cparams("parallel", "parallel", "arbitrary"),
    )(a, b)


def _rows(arr, width=None, blk=0):
    return (arr, ("rows", arr.shape[1] if width is None else width, blk))


def _full(arr):
    return (arr, ("full",))


def _rowwise(name, fn, ins, outs, *, rows, tm=None):
    tm = min(tm or ROW_TM, rows)
    assert rows % tm == 0, (name, rows, tm)
    n_in = len(ins)

    def body(*refs):
        i = pl.program_id(0)
        vals = fn(*[r[...] for r in refs[:n_in]])
        for (shape, dt, kind), r, v in zip(outs, refs[n_in:], vals):
            if kind == "rows":
                r[...] = v.astype(dt)
            else:
                @pl.when(i == 0)
                def _(r=r, v=v):
                    r[...] = v

                @pl.when(i > 0)
                def _(r=r, v=v):
                    r[...] += v

    in_specs = []
    for arr, spec in ins:
        if spec[0] == "rows":
            in_specs.append(pl.BlockSpec((tm, spec[1]), lambda i, b=spec[2]: (i, b)))
        else:
            in_specs.append(pl.BlockSpec(arr.shape, lambda i, n=arr.ndim: (0,) * n))
    out_specs = []
    for shape, dt, kind in outs:
        if kind == "rows":
            out_specs.append(pl.BlockSpec((tm, shape[1]), lambda i: (i, 0)))
        else:
            out_specs.append(pl.BlockSpec(shape, lambda i, n=len(shape): (0,) * n))
    res = pl.pallas_call(
        body, name=name, grid=(rows // tm,),
        in_specs=in_specs, out_specs=out_specs,
        out_shape=[jax.ShapeDtypeStruct(shape, dt) for shape, dt, _ in outs],
        compiler_params=_cparams("arbitrary"),
    )(*[arr for arr, _ in ins])
    return res


MEM_SCALE = MEM_HD ** -0.5
MEM_Q0 = 2
NT_DIMS = (((1,), (1,)), ((), ()))


def _head_lanes(h):
    return slice(h * HEAD_PAD, (h + 1) * HEAD_PAD)


def _mem_fwd(name, proj_b, kvm, *, tq, chunk):
    S = proj_b.shape[0]
    tq = min(tq, S)
    C = min(chunk, tq)
    tiles = [(h, c) for c in range(tq // C) for h in range(MEM_HEADS)]

    def body(q_ref, kv_ref, o_ref, lse_ref):
        def logits(h, c):
            return lax.dot_general(q_ref[c * C:(c + 1) * C, _head_lanes(h)], kv_ref[:, _head_lanes(h)], NT_DIMS,
                                   preferred_element_type=F32) * MEM_SCALE

        nxt = logits(*tiles[0])
        for n, (h, c) in enumerate(tiles):
            s = nxt
            if n + 1 < len(tiles):
                nxt = logits(*tiles[n + 1])
            rows = slice(c * C, (c + 1) * C)
            m = jnp.max(s, axis=1, keepdims=True)
            p = jnp.exp(s - m)
            l = jnp.sum(p, axis=1, keepdims=True)
            o = jnp.dot(p.astype(BF16), kv_ref[:, _head_lanes(MEM_HEADS + h)], preferred_element_type=F32) / l
            o_ref[rows, _head_lanes(h)] = o.astype(o_ref.dtype)
            lse_ref[h, rows, :] = m + jnp.log(l)

    return pl.pallas_call(
        body, name=name, grid=(S // tq,),
        in_specs=[pl.BlockSpec((tq, MEM_HEADS * HEAD_PAD), lambda i: (i, MEM_Q0)),
                  pl.BlockSpec(kvm.shape, lambda i: (0, 0))],
        out_specs=[pl.BlockSpec((tq, MEM_HEADS * HEAD_PAD), lambda i: (i, 0)),
                   pl.BlockSpec((MEM_HEADS, tq, 1), lambda i: (0, i, 0))],
        out_shape=[jax.ShapeDtypeStruct((S, MEM_HEADS * HEAD_PAD), BF16),
                   jax.ShapeDtypeStruct((MEM_HEADS, S, 1), F32)],
        compiler_params=_cparams("parallel"),
    )(proj_b, kvm)


def _mem_bwd(name, proj_b, kvm, o, do, lse, *, tq, chunk):
    S = proj_b.shape[0]
    tq = min(tq, S)
    C = min(chunk, tq)
    nq = S // tq
    tiles = [(h, c) for c in range(tq // C) for h in range(MEM_HEADS)]

    def body(q_ref, kv_ref, o_ref, do_ref, lse_ref, dq_ref, dkv_ref, acc_sc):
        i = pl.program_id(0)

        @pl.when(i == 0)
        def _():
            acc_sc[...] = jnp.zeros(acc_sc.shape, F32)

        def mats(h, c):
            rows = slice(c * C, (c + 1) * C)
            q = q_ref[rows, _head_lanes(h)]
            dov = do_ref[rows, _head_lanes(h)]
            s = lax.dot_general(q, kv_ref[:, _head_lanes(h)], NT_DIMS, preferred_element_type=F32) * MEM_SCALE
            dp = lax.dot_general(dov, kv_ref[:, _head_lanes(MEM_HEADS + h)], NT_DIMS, preferred_element_type=F32)
            return q, dov, s, dp

        nxt = mats(*tiles[0])
        for n, (h, c) in enumerate(tiles):
            q, dov, s, dp = nxt
            if n + 1 < len(tiles):
                nxt = mats(*tiles[n + 1])
            rows = slice(c * C, (c + 1) * C)
            p = jnp.exp(s - lse_ref[h, rows, :])
            delta = jnp.sum(dov.astype(F32) * o_ref[rows, _head_lanes(h)].astype(F32), axis=1, keepdims=True)
            ds = (p * (dp - delta) * MEM_SCALE).astype(BF16)
            dq_ref[rows, _head_lanes(h)] = jnp.dot(ds, kv_ref[:, _head_lanes(h)],
                                                   preferred_element_type=F32).astype(dq_ref.dtype)
            acc_sc[_head_lanes(h), :] += jnp.dot(q.T, ds, preferred_element_type=F32)
            acc_sc[_head_lanes(MEM_HEADS + h), :] += jnp.dot(dov.T, p.astype(BF16), preferred_element_type=F32)

        @pl.when(i == nq - 1)
        def _():
            dkv_ref[...] = acc_sc[...].T

    qblk = pl.BlockSpec((tq, MEM_HEADS * HEAD_PAD), lambda i: (i, 0))
    return pl.pallas_call(
        body, name=name, grid=(nq,),
        in_specs=[pl.BlockSpec((tq, MEM_HEADS * HEAD_PAD), lambda i: (i, MEM_Q0)),
                  pl.BlockSpec(kvm.shape, lambda i: (0, 0)), qblk, qblk,
                  pl.BlockSpec((MEM_HEADS, tq, 1), lambda i: (0, i, 0))],
        out_specs=[qblk, pl.BlockSpec(kvm.shape, lambda i: (0, 0))],
        out_shape=[jax.ShapeDtypeStruct((S, MEM_HEADS * HEAD_PAD), BF16), jax.ShapeDtypeStruct(kvm.shape, F32)],
        scratch_shapes=[pltpu.VMEM((kvm.shape[1], kvm.shape[0]), F32)],
        compiler_params=_cparams("arbitrary"),
    )(proj_b, kvm, o, do, lse)


def _causal_fwd(name, q_arr, k_arr, v_arr, *, heads, tile, chunk):
    S = q_arr.shape[0]
    T = min(tile, S)
    C = min(chunk, T)
    nt = S // T
    nc = T // C

    pairs = [(qi, kk) for qi in range(nt) for kk in range(qi + 1)]
    q_tab = jnp.asarray([p[0] for p in pairs], jnp.int32)
    k_tab = jnp.asarray([p[1] for p in pairs], jnp.int32)

    def body(qt_ref, kt_ref, q_ref, k_ref, v_ref, o_ref, lse_ref, m_sc, acc_sc):
        t = pl.program_id(1)
        qi = qt_ref[t]
        kk = kt_ref[t]

        @pl.when(kk == 0)
        def _():
            m_sc[...] = jnp.full(m_sc.shape, NEG, F32)
            acc_sc[...] = jnp.zeros(acc_sc.shape, F32)

        def logits(c, ncols, masked):
            s = lax.dot_general(q_ref[pl.ds(c * C, C), :], k_ref[0:ncols, :], (((1,), (1,)), ((), ())),
                                preferred_element_type=F32)
            if masked:
                r = c * C + lax.broadcasted_iota(jnp.int32, (C, ncols), 0)
                cidx = lax.broadcasted_iota(jnp.int32, (C, ncols), 1)
                s = jnp.where(cidx <= r, s, NEG)
            return s

        def update(c, ncols, s):
            rows = pl.ds(c * C, C)
            m_prev = m_sc[rows, :]
            m_new = jnp.maximum(m_prev, jnp.max(s, axis=1, keepdims=True))
            p = jnp.exp2(s - m_new).astype(BF16)
            acc_sc[rows, :] = jnp.exp2(m_prev - m_new) * acc_sc[rows, :] + jnp.dot(
                p, v_ref[0:ncols, :], preferred_element_type=F32)
            m_sc[rows, :] = m_new

        def tile_step(ncols_of, masked):
            s = logits(0, ncols_of(0), masked)
            for c in range(nc):
                s_next = logits(c + 1, ncols_of(c + 1), masked) if c + 1 < nc else None
                update(c, ncols_of(c), s)
                s = s_next

        @pl.when(kk < qi)
        def _():
            tile_step(lambda c: T, False)

        @pl.when(kk == qi)
        def _():
            tile_step(lambda c: (c + 1) * C, True)

        @pl.when(kk == qi)
        def _():
            acc = acc_sc[...]
            l = acc[:, DEN_LANE:DEN_LANE + 1]
            o_ref[...] = (acc / l).astype(o_ref.dtype)
            lse_ref[0] = m_sc[...] + jnp.log2(l)

    grid_spec = pltpu.PrefetchScalarGridSpec(
        num_scalar_prefetch=2, grid=(heads, len(pairs)),
        in_specs=[pl.BlockSpec((T, HEAD_PAD), lambda h, t, qt, kt: (qt[t], h)),
                  pl.BlockSpec((T, HEAD_PAD), lambda h, t, qt, kt: (kt[t], h)),
                  pl.BlockSpec((T, HEAD_PAD), lambda h, t, qt, kt: (kt[t], h))],
        out_specs=[pl.BlockSpec((T, HEAD_PAD), lambda h, t, qt, kt: (qt[t], h)),
                   pl.BlockSpec((1, T, 1), lambda h, t, qt, kt: (h, qt[t], 0))],
        scratch_shapes=[pltpu.VMEM((T, 1), F32), pltpu.VMEM((T, HEAD_PAD), F32)])
    return pl.pallas_call(
        body, name=name, grid_spec=grid_spec,
        out_shape=[jax.ShapeDtypeStruct((S, heads * HEAD_PAD), BF16),
                   jax.ShapeDtypeStruct((heads, S, 1), F32)],
        compiler_params=_cparams("parallel", "arbitrary"),
    )(q_tab, k_tab, q_arr, k_arr, v_arr)


def _with_neg_delta(do, o):
    lane = lax.broadcasted_iota(jnp.int32, (1, HEAD_PAD), 1)
    outs = []
    for h in range(do.shape[1] // HEAD_PAD):
        a = do[:, _head_lanes(h)]
        nd = -jnp.sum(a * o[:, _head_lanes(h)].astype(F32), axis=1, keepdims=True)
        hi = nd.astype(BF16).astype(F32)
        a = jnp.where(lane == DEN_LANE, hi, a)
        outs.append(jnp.where(lane == DEN_LANE + 1, nd - hi, a))
    return jnp.concatenate(outs, axis=1)


def _causal_bwd(name, q_arr, k_arr, v_arr, do_arr, lse, *, heads, tile, chunk):
    S = q_arr.shape[0]
    T = min(tile, S)
    C = min(chunk, T)
    nt = S // T
    nc = T // C

    pairs = [(kj, qq) for kj in range(nt) for qq in range(kj, nt)]
    k_tab = jnp.asarray([p[0] for p in pairs], jnp.int32)
    q_tab = jnp.asarray([p[1] for p in pairs], jnp.int32)

    def body(kt_ref, qt_ref, q_ref, k_ref, v_ref, do_ref, lse_ref, dq_ref, dk_ref, dv_ref, dk_sc, dv_sc):
        t = pl.program_id(1)
        kj = kt_ref[t]
        qq = qt_ref[t]
        qb = qq

        @pl.when(t == 0)
        def _():
            dq_ref[...] = jnp.zeros(dq_ref.shape, F32)

        @pl.when(qq == kj)
        def _():
            dk_sc[...] = jnp.zeros(dk_sc.shape, F32)
            dv_sc[...] = jnp.zeros(dv_sc.shape, F32)

        def logits(c, ncols, masked):
            rows = pl.ds(c * C, C)
            s = lax.dot_general(q_ref[rows, :], k_ref[0:ncols, :], (((1,), (1,)), ((), ())),
                                preferred_element_type=F32)
            if masked:
                r = c * C + lax.broadcasted_iota(jnp.int32, (C, ncols), 0)
                cidx = lax.broadcasted_iota(jnp.int32, (C, ncols), 1)
                s = jnp.where(cidx <= r, s, NEG)
            dp = lax.dot_general(do_ref[rows, :], v_ref[0:ncols, :], (((1,), (1,)), ((), ())),
                                 preferred_element_type=F32)
            return s, dp

        def update(c, ncols, s, dp):
            rows = pl.ds(c * C, C)
            p = jnp.exp2(s - lse_ref[0, rows, :])
            ds = (p * dp).astype(BF16)
            dv_sc[:, 0:ncols] += jnp.dot(do_ref[rows, 0:MLA_V].T, p.astype(BF16), preferred_element_type=F32)
            dk_sc[:, 0:ncols] += jnp.dot(q_ref[rows, 0:MLA_QK].T, ds, preferred_element_type=F32)
            row0 = pl.multiple_of(qb * T + c * C, C)
            dq_ref[pl.ds(row0, C), :] += jnp.dot(ds, k_ref[0:ncols, :], preferred_element_type=F32)

        def tile_step(ncols_of, masked):
            cur = logits(0, ncols_of(0), masked)
            for c in range(nc):
                nxt = logits(c + 1, ncols_of(c + 1), masked) if c + 1 < nc else None
                update(c, ncols_of(c), *cur)
                cur = nxt

        @pl.when(qq > kj)
        def _():
            tile_step(lambda c: T, False)

        @pl.when(qq == kj)
        def _():
            tile_step(lambda c: (c + 1) * C, True)

        @pl.when(qq == nt - 1)
        def _():
            dk_ref[:, 0:MLA_QK] = dk_sc[...].T * math.log(2.0)
            dk_ref[:, MLA_QK:] = jnp.zeros((T, HEAD_PAD - MLA_QK), F32)
            dv_ref[:, 0:MLA_V] = dv_sc[...].T
            dv_ref[:, MLA_V:] = jnp.zeros((T, HEAD_PAD - MLA_V), F32)

    qrow = pl.BlockSpec((T, HEAD_PAD), lambda h, t, kt, qt: (qt[t], h))
    krow = pl.BlockSpec((T, HEAD_PAD), lambda h, t, kt, qt: (kt[t], h))
    qcol = pl.BlockSpec((1, T, 1), lambda h, t, kt, qt: (h, qt[t], 0))
    grid_spec = pltpu.PrefetchScalarGridSpec(
        num_scalar_prefetch=2, grid=(heads, len(pairs)),
        in_specs=[qrow, krow, krow, qrow, qcol],
        out_specs=[pl.BlockSpec((S, HEAD_PAD), lambda h, t, kt, qt: (0, h)), krow, krow],
        scratch_shapes=[pltpu.VMEM((MLA_QK, T), F32), pltpu.VMEM((MLA_V, T), F32)])
    return pl.pallas_call(
        body, name=name, grid_spec=grid_spec,
        out_shape=[jax.ShapeDtypeStruct((S, heads * HEAD_PAD), F32)] * 3,
        compiler_params=_cparams("arbitrary", "arbitrary"),
    )(k_tab, q_tab, q_arr, k_arr, v_arr, do_arr, lse)


SWA_R = SWA_HEADS // SWA_KV_HEADS
SWA_SCALE = SWA_HD ** -0.5
SWA_Q0, SWA_K0, SWA_V0 = 0, 12, 14


def _swa_specs(tq):
    nsb = tq // WINDOW
    return [
        pl.BlockSpec((tq, SWA_R * HEAD_PAD), lambda g, i: (i, g)),
        pl.BlockSpec((tq, HEAD_PAD), lambda g, i: (i, SWA_K0 + g)),
        pl.BlockSpec((WINDOW, HEAD_PAD), lambda g, i: (jnp.maximum(nsb * i - 1, 0), SWA_K0 + g)),
        pl.BlockSpec((tq, HEAD_PAD), lambda g, i: (i, SWA_V0 + g)),
        pl.BlockSpec((WINDOW, HEAD_PAD), lambda g, i: (jnp.maximum(nsb * i - 1, 0), SWA_V0 + g)),
        pl.BlockSpec((SWA_R, WINDOW, 2 * WINDOW), lambda g, i: (g, 0, 0)),
        pl.BlockSpec((SWA_R, 8, 128), lambda g, i: (g, 0, 0)),
    ]


def _swa_block(i, sb, q_ref, kc_ref, kp_ref, vc_ref, vp_ref, bias, sink):
    rows = slice(sb * WINDOW, (sb + 1) * WINDOW)
    qs = jnp.concatenate([q_ref[rows, hh * HEAD_PAD:(hh + 1) * HEAD_PAD] for hh in range(SWA_R)], axis=0)
    if sb == 0:
        kp, vp = kp_ref[...], vp_ref[...]
    else:
        prev = slice((sb - 1) * WINDOW, sb * WINDOW)
        kp, vp = kc_ref[prev, :], vc_ref[prev, :]
    kk = jnp.concatenate([kp, kc_ref[rows, :]], axis=0)
    vv = jnp.concatenate([vp, vc_ref[rows, :]], axis=0)
    s = lax.dot_general(qs, kk, (((1,), (1,)), ((), ())), preferred_element_type=F32) * SWA_SCALE + bias
    if sb == 0:
        col = lax.broadcasted_iota(jnp.int32, (1, 2 * WINDOW), 1)
        s = s + jnp.where((col < WINDOW) & (i == 0), NEG, 0.0)
    return rows, qs, kk, vv, s


def _stack_heads(ref, rows, lead=None):
    if lead is None:
        return jnp.concatenate([ref[rows, hh * HEAD_PAD:(hh + 1) * HEAD_PAD] for hh in range(SWA_R)], axis=0)
    return jnp.concatenate([ref[hh, rows, :] for hh in range(SWA_R)], axis=0)


def _swa_fwd(name, proj_b, bias, sinks, *, tq):
    S = proj_b.shape[0]
    tq = min(tq, S)
    nsb = tq // WINDOW

    def body(q_ref, kc_ref, kp_ref, vc_ref, vp_ref, bias_ref, sink_ref, o_ref, lse_ref):
        i = pl.program_id(1)
        bias_v = bias_ref[...].reshape(SWA_R * WINDOW, 2 * WINDOW)
        sink = jnp.concatenate([jnp.zeros((WINDOW, 1), F32) + sink_ref[hh, 0:1, 0:1] for hh in range(SWA_R)], axis=0)
        nxt = _swa_block(i, 0, q_ref, kc_ref, kp_ref, vc_ref, vp_ref, bias_v, sink)
        for sb in range(nsb):
            rows, _, _, vv, s = nxt
            if sb + 1 < nsb:
                nxt = _swa_block(i, sb + 1, q_ref, kc_ref, kp_ref, vc_ref, vp_ref, bias_v, sink)
            m = jnp.maximum(jnp.max(s, axis=1, keepdims=True), sink)
            p = jnp.exp(s - m)
            l = jnp.sum(p, axis=1, keepdims=True) + jnp.exp(sink - m)
            o = jnp.dot(p.astype(BF16), vv, preferred_element_type=F32) / l
            lse_v = m + jnp.log(l)
            for hh in range(SWA_R):
                o_ref[rows, hh * HEAD_PAD:(hh + 1) * HEAD_PAD] = o[hh * WINDOW:(hh + 1) * WINDOW].astype(o_ref.dtype)
                lse_ref[hh, rows, :] = lse_v[hh * WINDOW:(hh + 1) * WINDOW]

    return pl.pallas_call(
        body, name=name, grid=(SWA_KV_HEADS, S // tq),
        in_specs=_swa_specs(tq),
        out_specs=[pl.BlockSpec((tq, SWA_R * HEAD_PAD), lambda g, i: (i, g)),
                   pl.BlockSpec((SWA_R, tq, 1), lambda g, i: (g, i, 0))],
        out_shape=[jax.ShapeDtypeStruct((S, SWA_HEADS * HEAD_PAD), BF16),
                   jax.ShapeDtypeStruct((SWA_HEADS, S, 1), F32)],
        compiler_params=_cparams("parallel", "parallel"),
    )(proj_b, proj_b, proj_b, proj_b, proj_b, bias, sinks)


def _swa_bwd(name, proj_b, bias, sinks, o, do, lse, *, tq):
    S = proj_b.shape[0]
    tq = min(tq, S)
    nsb = tq // WINDOW
    nq = S // tq

    def body(q_ref, kc_ref, kp_ref, vc_ref, vp_ref, bias_ref, sink_ref, o_ref, do_ref, lse_ref,
             dq_ref, dk_ref, dv_ref, dke_ref, dve_ref, dbias_ref, dsink_ref):
        i = pl.program_id(1)

        @pl.when(i == 0)
        def _():
            dbias_ref[...] = jnp.zeros(dbias_ref.shape, F32)
            dsink_ref[...] = jnp.zeros(dsink_ref.shape, F32)

        bias_v = bias_ref[...].reshape(SWA_R * WINDOW, 2 * WINDOW)
        sink = jnp.concatenate([jnp.zeros((WINDOW, 1), F32) + sink_ref[hh, 0:1, 0:1] for hh in range(SWA_R)], axis=0)
        dk_own, dv_own, dk_prev, dv_prev = [], [], [], []
        dbias_acc = jnp.zeros((SWA_R * WINDOW, 2 * WINDOW), F32)
        def block(sb):
            rows, qs, kk, vv, s = _swa_block(i, sb, q_ref, kc_ref, kp_ref, vc_ref, vp_ref, bias_v, sink)
            do_s = _stack_heads(do_ref, rows)
            dp = lax.dot_general(do_s, vv, (((1,), (1,)), ((), ())), preferred_element_type=F32)
            return rows, qs, kk, do_s, s, dp

        nxt = block(0)
        for sb in range(nsb):
            rows, qs, kk, do_s, s, dp = nxt
            if sb + 1 < nsb:
                nxt = block(sb + 1)
            lse_v = _stack_heads(lse_ref, rows, lead=True)
            delta = jnp.sum(do_s.astype(F32) * _stack_heads(o_ref, rows).astype(F32), axis=1, keepdims=True)
            p = jnp.exp(s - lse_v)
            dsp = p * (dp - delta)
            dbias_acc = dbias_acc + dsp
            ds = (dsp * SWA_SCALE).astype(BF16)
            dq = jnp.dot(ds, kk, preferred_element_type=F32)
            dkk = jnp.dot(qs.T, ds, preferred_element_type=F32)
            dvv = jnp.dot(do_s.T, p.astype(BF16), preferred_element_type=F32)
            dk_prev.append(dkk[:, :WINDOW].T)
            dk_own.append(dkk[:, WINDOW:].T)
            dv_prev.append(dvv[:, :WINDOW].T)
            dv_own.append(dvv[:, WINDOW:].T)
            psink = jnp.exp(sink - lse_v) * delta
            for hh in range(SWA_R):
                hrows = slice(hh * WINDOW, (hh + 1) * WINDOW)
                dq_ref[rows, hh * HEAD_PAD:(hh + 1) * HEAD_PAD] = dq[hrows].astype(dq_ref.dtype)
                dsink_ref[hh] += jnp.zeros((8, 128), F32) - jnp.sum(psink[hrows])
        dbias_ref[...] += dbias_acc.reshape(SWA_R, WINDOW, 2 * WINDOW)
        for sb in range(nsb):
            rows = slice(sb * WINDOW, (sb + 1) * WINDOW)
            if sb + 1 < nsb:
                dk_ref[rows, :] = dk_own[sb] + dk_prev[sb + 1]
                dv_ref[rows, :] = dv_own[sb] + dv_prev[sb + 1]
            else:
                dk_ref[rows, :] = dk_own[sb]
                dv_ref[rows, :] = dv_own[sb]
        dke_ref[...] = dk_prev[0]
        dve_ref[...] = dv_prev[0]

    in_specs = _swa_specs(tq) + [
        pl.BlockSpec((tq, SWA_R * HEAD_PAD), lambda g, i: (i, g)),
        pl.BlockSpec((tq, SWA_R * HEAD_PAD), lambda g, i: (i, g)),
        pl.BlockSpec((SWA_R, tq, 1), lambda g, i: (g, i, 0)),
    ]
    kv_blk = pl.BlockSpec((tq, HEAD_PAD), lambda g, i: (i, g))
    edge_blk = pl.BlockSpec((WINDOW, HEAD_PAD), lambda g, i: (i, g))
    return pl.pallas_call(
        body, name=name, grid=(SWA_KV_HEADS, nq),
        in_specs=in_specs,
        out_specs=[pl.BlockSpec((tq, SWA_R * HEAD_PAD), lambda g, i: (i, g)), kv_blk, kv_blk, edge_blk, edge_blk,
                   pl.BlockSpec((SWA_R, WINDOW, 2 * WINDOW), lambda g, i: (g, 0, 0)),
                   pl.BlockSpec((SWA_R, 8, 128), lambda g, i: (g, 0, 0))],
        out_shape=[jax.ShapeDtypeStruct((S, SWA_HEADS * HEAD_PAD), BF16),
                   jax.ShapeDtypeStruct((S, SWA_KV_HEADS * HEAD_PAD), F32),
                   jax.ShapeDtypeStruct((S, SWA_KV_HEADS * HEAD_PAD), F32),
                   jax.ShapeDtypeStruct((nq * WINDOW, SWA_KV_HEADS * HEAD_PAD), F32),
                   jax.ShapeDtypeStruct((nq * WINDOW, SWA_KV_HEADS * HEAD_PAD), F32),
                   jax.ShapeDtypeStruct((SWA_HEADS, WINDOW, 2 * WINDOW), F32),
                   jax.ShapeDtypeStruct((SWA_HEADS, 8, 128), F32)],
        compiler_params=_cparams("arbitrary", "arbitrary"),
    )(proj_b, proj_b, proj_b, proj_b, proj_b, bias, sinks, o, do, lse)


def _dproj_b(name, dq_swa, dq_mem, dk, dv, dk_edge, dv_edge, *, tq):
    S = dq_swa.shape[0]
    tq = min(tq, S)
    nq = S // tq

    def body(dqs_ref, dqm_ref, dk_ref, dv_ref, dke_ref, dve_ref, o_ref):
        i = pl.program_id(0)
        o_ref[:, 0:1024] = dqs_ref[...]
        o_ref[:, 1024:1536] = dqm_ref[...].astype(o_ref.dtype)
        o_ref[:, 1536:1792] = dk_ref[...].astype(o_ref.dtype)
        o_ref[:, 1792:2048] = dv_ref[...].astype(o_ref.dtype)

        @pl.when(i < nq - 1)
        def _():
            last = slice(tq - WINDOW, tq)
            o_ref[last, 1536:1792] = (dk_ref[last, :] + dke_ref[...]).astype(o_ref.dtype)
            o_ref[last, 1792:2048] = (dv_ref[last, :] + dve_ref[...]).astype(o_ref.dtype)

    edge = pl.BlockSpec((WINDOW, SWA_KV_HEADS * HEAD_PAD), lambda i: (jnp.minimum(i + 1, nq - 1), 0))
    return pl.pallas_call(
        body, name=name, grid=(nq,),
        in_specs=[pl.BlockSpec((tq, 1024), lambda i: (i, 0)), pl.BlockSpec((tq, 512), lambda i: (i, 0)),
                  pl.BlockSpec((tq, 256), lambda i: (i, 0)), pl.BlockSpec((tq, 256), lambda i: (i, 0)), edge, edge],
        out_specs=pl.BlockSpec((tq, 2048), lambda i: (i, 0)),
        out_shape=jax.ShapeDtypeStruct((S, 2048), BF16),
        compiler_params=_cparams("parallel"),
    )(dq_swa, dq_mem, dk, dv, dk_edge, dv_edge)


def _exchange(name, send, *, per_peer):
    shape = send.shape[1:] if per_peer else send.shape

    def body(send_ref, recv_ref, send_sems, recv_sems, local_sem):
        x, y, c = lax.axis_index("x"), lax.axis_index("y"), lax.axis_index("c")
        me = 4 * x + 2 * y + c
        own = pltpu.make_async_copy(send_ref.at[me] if per_peer else send_ref, recv_ref.at[me], local_sem)
        own.start()
        copies = []
        for k in range(1, N_DEV):
            px = 1 - x if (k >> 2) & 1 else x
            py = 1 - y if (k >> 1) & 1 else y
            pc = 1 - c if k & 1 else c
            peer = 4 * px + 2 * py + pc
            out = pltpu.make_async_remote_copy(
                src_ref=send_ref.at[peer] if per_peer else send_ref, dst_ref=recv_ref.at[me],
                send_sem=send_sems.at[k - 1], recv_sem=recv_sems.at[k - 1],
                device_id=(px, py, pc), device_id_type=pl.DeviceIdType.MESH)
            out.start()
            back = pltpu.make_async_remote_copy(
                src_ref=send_ref.at[me] if per_peer else send_ref, dst_ref=recv_ref.at[peer],
                send_sem=send_sems.at[k - 1], recv_sem=recv_sems.at[k - 1],
                device_id=(px, py, pc), device_id_type=pl.DeviceIdType.MESH)
            copies.append((out, back))
        for out, back in copies:
            out.wait_send()
            back.wait_recv()
        own.wait()

    return pl.pallas_call(
        body, name=name,
        in_specs=[pl.BlockSpec(memory_space=pl.ANY)],
        out_specs=pl.BlockSpec(memory_space=pl.ANY),
        out_shape=jax.ShapeDtypeStruct((N_DEV,) + tuple(shape), send.dtype),
        scratch_shapes=[pltpu.SemaphoreType.DMA((N_DEV - 1,)), pltpu.SemaphoreType.DMA((N_DEV - 1,)),
                        pltpu.SemaphoreType.DMA(())],
    )(send)


def _gather_forwarded(name, block):
    def body(x_ref, out_ref, send_sems, recv_sems, local_sem):
        x, y, c = lax.axis_index("x"), lax.axis_index("y"), lax.axis_index("c")
        me, sibling = (x, y, c), (x, y, 1 - c)
        chips = [(1 - x, y), (x, 1 - y), (1 - x, 1 - y)]

        def slot(px, py, pc):
            return out_ref.at[4 * px + 2 * py + pc]

        def copy(k, blk, to, src=None):
            return pltpu.make_async_remote_copy(
                src_ref=slot(*blk) if src is None else src, dst_ref=slot(*blk),
                send_sem=send_sems.at[k], recv_sem=recv_sems.at[k],
                device_id=to, device_id_type=pl.DeviceIdType.MESH)

        mine = pltpu.make_async_copy(x_ref, slot(*me), local_sem)
        mine.start()
        first = [copy(0, me, sibling, src=x_ref)]
        first += [copy(1 + j, me, (*chip, c), src=x_ref) for j, chip in enumerate(chips)]
        for cp in first:
            cp.start()
        passed = [copy(4 + j, (*chip, c), sibling) for j, chip in enumerate(chips)]
        for j, chip in enumerate(chips):
            copy(1 + j, (*chip, c), me).wait_recv()
            passed[j].start()
        copy(0, sibling, me).wait_recv()
        for j, chip in enumerate(chips):
            copy(4 + j, (*chip, 1 - c), me).wait_recv()
        for cp in first + passed:
            cp.wait_send()
        mine.wait()

    return pl.pallas_call(
        body, name=name,
        in_specs=[pl.BlockSpec(memory_space=pl.ANY)],
        out_specs=pl.BlockSpec(memory_space=pl.ANY),
        out_shape=jax.ShapeDtypeStruct((N_DEV,) + tuple(block.shape), block.dtype),
        scratch_shapes=[pltpu.SemaphoreType.DMA((N_DEV - 1,)), pltpu.SemaphoreType.DMA((N_DEV - 1,)),
                        pltpu.SemaphoreType.DMA(())],
    )(block)


def _sibling_swap(name, block):
    def body(x_ref, out_ref, send_sem, recv_sem):
        x, y, c = lax.axis_index("x"), lax.axis_index("y"), lax.axis_index("c")
        cp = pltpu.make_async_remote_copy(src_ref=x_ref, dst_ref=out_ref, send_sem=send_sem, recv_sem=recv_sem,
                                          device_id=(x, y, 1 - c), device_id_type=pl.DeviceIdType.MESH)
        cp.start()
        cp.wait()

    return pl.pallas_call(
        body, name=name,
        in_specs=[pl.BlockSpec(memory_space=pl.ANY)],
        out_specs=pl.BlockSpec(memory_space=pl.ANY),
        out_shape=jax.ShapeDtypeStruct(block.shape, block.dtype),
        scratch_shapes=[pltpu.SemaphoreType.DMA(()), pltpu.SemaphoreType.DMA(())],
    )(block)


def _chip_exchange(name, send):
    def body(send_ref, recv_ref, send_sems, recv_sems, local_sem):
        x, y, c = lax.axis_index("x"), lax.axis_index("y"), lax.axis_index("c")
        me = 2 * x + y
        own = pltpu.make_async_copy(send_ref.at[me], recv_ref.at[me], local_sem)
        own.start()
        copies = []
        for k in range(1, 4):
            px = 1 - x if (k >> 1) & 1 else x
            py = 1 - y if k & 1 else y
            peer = 2 * px + py
            out = pltpu.make_async_remote_copy(
                src_ref=send_ref.at[peer], dst_ref=recv_ref.at[me],
                send_sem=send_sems.at[k - 1], recv_sem=recv_sems.at[k - 1],
                device_id=(px, py, c), device_id_type=pl.DeviceIdType.MESH)
            out.start()
            back = pltpu.make_async_remote_copy(
                src_ref=send_ref.at[me], dst_ref=recv_ref.at[peer],
                send_sem=send_sems.at[k - 1], recv_sem=recv_sems.at[k - 1],
                device_id=(px, py, c), device_id_type=pl.DeviceIdType.MESH)
            copies.append((out, back))
        for out, back in copies:
            out.wait_send()
            back.wait_recv()
        own.wait()

    return pl.pallas_call(
        body, name=name,
        in_specs=[pl.BlockSpec(memory_space=pl.ANY)],
        out_specs=pl.BlockSpec(memory_space=pl.ANY),
        out_shape=jax.ShapeDtypeStruct(send.shape, send.dtype),
        scratch_shapes=[pltpu.SemaphoreType.DMA((3,)), pltpu.SemaphoreType.DMA((3,)), pltpu.SemaphoreType.DMA(())],
    )(send)


def _reduce_scatter(parts):
    lanes = 128
    c = lax.axis_index("c")

    def core_half(core):
        return jnp.concatenate(
            [lax.dynamic_index_in_dim(p.reshape(4, 2, p.shape[1], lanes), core, axis=1, keepdims=False)
             for p in parts], axis=1)

    mine = core_half(c)
    rows = mine.shape[1]
    mine = mine.reshape(4 * rows, lanes)
    theirs = core_half(1 - c).reshape(4 * rows, lanes)
    from_sibling = _sibling_swap("grads_to_sibling", theirs)
    tm = max(t for t in range(16, ADAM_TM + 1, 16) if rows % t == 0)
    chip_sum = _rowwise("grads_chip_sum", lambda a, b: (a.astype(F32) + b.astype(F32),),
                        [_rows(mine), _rows(from_sibling)], [((4 * rows, lanes), BF16, "rows")],
                        rows=4 * rows, tm=tm)[0]
    return _chip_exchange("scatter_grads", chip_sum.reshape(4, rows, lanes))


def _adam(name, recv, w, m, v, *, tm=None):
    R = w.shape[0]
    n_parts = recv.shape[0]
    tm = max(t for t in range(8, min(tm or ADAM_TM, R) + 1, 8) if R % t == 0)
    c1 = 1.0 / (1.0 - ADAM_B1 ** ADAM_STEP)
    c2 = 1.0 / (1.0 - ADAM_B2 ** ADAM_STEP)

    def body(r_ref, w_ref, m_ref, v_ref, g_ref, d_ref, nm_ref, nv_ref):
        g = r_ref[0].astype(F32)
        for j in range(1, n_parts):
            g = g + r_ref[j].astype(F32)
        wv = w_ref[...]
        nm = ADAM_B1 * m_ref[...] + (1.0 - ADAM_B1) * g
        nv = ADAM_B2 * v_ref[...] + (1.0 - ADAM_B2) * (g * g)
        m_hat = nm * c1
        v_hat = nv * c2
        g_ref[...] = g
        d_ref[...] = -ADAM_LR * (m_hat / (jnp.sqrt(v_hat) + ADAM_EPS) + ADAM_WD * wv)
        nm_ref[...] = nm
        nv_ref[...] = nv

    row = pl.BlockSpec((tm, 128), lambda i: (i, 0))
    return pl.pallas_call(
        body, name=name, grid=(R // tm,),
        in_specs=[pl.BlockSpec((n_parts, tm, 128), lambda i: (0, i, 0)), row, row, row],
        out_specs=[row, row, row, row],
        out_shape=[jax.ShapeDtypeStruct((R, 128), F32)] * 4,
        compiler_params=_cparams("parallel"),
    )(recv, w, m, v)


def _pack_rows(arrs):
    return jnp.concatenate([a.reshape(-1, 128) for a in arrs], axis=0)


def _unstack(g, shape, axis):
    t = jnp.moveaxis(g, 0, axis)
    return t.reshape(shape)


def _restack(full, axis):
    s = full.shape
    t = full.reshape(s[:axis] + (N_DEV, s[axis] // N_DEV) + s[axis + 1:])
    return jnp.moveaxis(t, axis, 0)


def _pad_heads(w, heads, hd, axis):
    s = w.shape
    t = w.reshape(s[:axis] + (heads, hd) + s[axis + 1:])
    pad = [(0, 0)] * t.ndim
    pad[axis + 1] = (0, HEAD_PAD - hd)
    t = jnp.pad(t, pad)
    return t.reshape(s[:axis] + (heads * HEAD_PAD,) + s[axis + 1:])


def _unpad_heads(w, heads, hd, axis):
    s = w.shape
    t = w.reshape(s[:axis] + (heads, HEAD_PAD) + s[axis + 1:])
    t = lax.slice_in_dim(t, 0, hd, axis=axis + 1)
    return t.reshape(s[:axis] + (heads * hd,) + s[axis + 1:])


def _layer_weights(full, l):
    w_in = full["w_in"][l]
    cq, kva, qs, ks, vs, qm, gates = (w_in[:, 0:256], w_in[:, 256:416], w_in[:, 416:928], w_in[:, 928:1056],
                                       w_in[:, 1056:1184], w_in[:, 1184:1696], w_in[:, 1696:4768])
    wa = jnp.concatenate([gates, cq, jnp.pad(kva, ((0, 0), (0, 96)))], axis=1)
    wb = jnp.concatenate([_pad_heads(qs, SWA_HEADS, SWA_HD, 1), qm, _pad_heads(ks, SWA_KV_HEADS, SWA_HD, 1),
                          _pad_heads(vs, SWA_KV_HEADS, SWA_HD, 1)], axis=1)
    wuq = _pad_heads(full["w_uq"][l], MLA_HEADS, MLA_NOPE + MLA_ROPE, 1)
    ukv = full["w_ukv"][l].reshape(MLA_KV_LORA, MLA_HEADS, MLA_NOPE + MLA_V)
    wuk = _pad_heads(ukv[:, :, :MLA_NOPE].reshape(MLA_KV_LORA, -1), MLA_HEADS, MLA_NOPE, 1)
    wuv = _pad_heads(ukv[:, :, MLA_NOPE:].reshape(MLA_KV_LORA, -1), MLA_HEADS, MLA_V, 1)
    wo_mla = _pad_heads(full["w_o_mla"][l], MLA_HEADS, MLA_V, 0)
    wo_swa = _pad_heads(full["w_o_swa"][l], SWA_HEADS, SWA_HD, 0)
    wo_mem = full["w_o_mem"][l]
    w = dict(wag=wa[:, :3072], wat=wa[:, 3072:], wb=wb, wuq=wuq, wuk=wuk, wuv=wuv, wo_mla=wo_mla, wo_swa=wo_swa,
             wo_mem=wo_mem, wmem=full["w_mem_kv"][l], wout=full["w_out"][l], wup=full["w_up"][l],
             wdown=full["w_down"][l])
    w.update({k + "_t": v.T for k, v in w.items()})
    return w


def _layer_weight_grads(g):
    dwa_g, dwa_t, dwb = g["wag"], g["wat"], g["wb"]
    d_in = jnp.concatenate([
        dwa_t[:, 0:256], dwa_t[:, 256:416],
        _unpad_heads(dwb[:, 0:1024], SWA_HEADS, SWA_HD, 1),
        _unpad_heads(dwb[:, 1536:1792], SWA_KV_HEADS, SWA_HD, 1),
        _unpad_heads(dwb[:, 1792:2048], SWA_KV_HEADS, SWA_HD, 1),
        dwb[:, 1024:1536], dwa_g], axis=1)
    duk = _unpad_heads(g["wuk"], MLA_HEADS, MLA_NOPE, 1).reshape(MLA_KV_LORA, MLA_HEADS, MLA_NOPE)
    duv = _unpad_heads(g["wuv"], MLA_HEADS, MLA_V, 1).reshape(MLA_KV_LORA, MLA_HEADS, MLA_V)
    return dict(
        w_in=d_in,
        w_uq=_unpad_heads(g["wuq"], MLA_HEADS, MLA_NOPE + MLA_ROPE, 1),
        w_ukv=jnp.concatenate([duk, duv], axis=2).reshape(MLA_KV_LORA, -1),
        w_mem_kv=g["wmem"],
        w_o_mla=_unpad_heads(g["wo_mla"], MLA_HEADS, MLA_V, 0),
        w_o_swa=_unpad_heads(g["wo_swa"], SWA_HEADS, SWA_HD, 0),
        w_o_mem=g["wo_mem"], w_out=g["wout"], w_up=g["wup"], w_down=g["wdown"])


def _rope_tables(S):
    pos = jnp.arange(S, dtype=F32)
    inv = 1.0 / (ROPE_THETA ** (jnp.arange(0, MLA_ROPE, 2, dtype=F32) / MLA_ROPE))
    ang = pos[:, None] * inv[None, :]
    cos, sin = jnp.cos(ang), jnp.sin(ang)
    z16 = jnp.zeros((S, 16), F32)
    z32 = jnp.zeros((S, 32), F32)
    c = jnp.concatenate([jnp.ones((S, 64), F32), cos, cos, z32], axis=1)
    ck = jnp.concatenate([jnp.zeros((S, 64), F32), cos, cos, z32], axis=1)
    s1 = jnp.concatenate([jnp.zeros((S, 80), F32), sin, z32], axis=1)
    s2 = jnp.concatenate([jnp.zeros((S, 64), F32), -sin, z16, z32], axis=1)
    return c, ck, s1, s2


def _t5_bucket(dist):
    n = jnp.maximum(dist, 0)
    max_exact = REL_BUCKETS // 2
    nf = jnp.maximum(n, 1).astype(F32)
    large = max_exact + (jnp.log(nf / max_exact) / math.log(REL_MAX_DIST / max_exact)
                         * (REL_BUCKETS - max_exact)).astype(jnp.int32)
    large = jnp.minimum(large, REL_BUCKETS - 1)
    return jnp.where(n < max_exact, n, large)


def _bias_onehot():
    qi = jnp.arange(WINDOW)[:, None]
    kj = jnp.arange(2 * WINDOW)[None, :]
    dist = qi + WINDOW - kj
    valid = (dist >= 0) & (dist < WINDOW)
    bucket = _t5_bucket(dist)
    onehot = (bucket[None] == jnp.arange(REL_BUCKETS)[:, None, None]) & valid[None]
    return (onehot.reshape(REL_BUCKETS, -1).astype(F32),
            jnp.where(valid, 0.0, NEG).astype(F32).reshape(1, -1))


def _rstd(x):
    return lax.rsqrt(jnp.mean(x * x, axis=-1, keepdims=True) + EPS)


def _norm_bwd(dh, x, g):
    r = _rstd(x)
    xh = x * r
    w = dh * g
    dx = r * (w - xh * jnp.mean(w * xh, axis=-1, keepdims=True))
    return dx, jnp.sum(dh * xh, axis=0, keepdims=True)


def _tile_lanes(t, n):
    return jnp.tile(t, (1, n // t.shape[1])) if n != t.shape[1] else t


def _rope_fwd(a, c, s1, s2):
    n = a.shape[1]
    return (a * _tile_lanes(c, n) + pltpu.roll(a, 16, 1) * _tile_lanes(s1, n)
            + pltpu.roll(a, n - 16, 1) * _tile_lanes(s2, n))


def _rope_bwd(d, c, s1, s2):
    n = d.shape[1]
    return (d * _tile_lanes(c, n) + pltpu.roll(d * _tile_lanes(s1, n), n - 16, 1)
            + pltpu.roll(d * _tile_lanes(s2, n), 16, 1))


def _sigmoid(x):
    return 1.0 / (1.0 + jnp.exp(-x))


def _rmsnorm(name, x, g, dtype):
    def fn(xv, gv):
        return ((xv * _rstd(xv)) * gv,)
    return _rowwise(name, fn, [_rows(x), _full(g)], [(x.shape, dtype, "rows")], rows=x.shape[0])[0]


def _residual_norm_bwd(name, dres, dh, x, g):
    def fn(dr, dhv, xv, gv):
        dx, dg = _norm_bwd(dhv, xv, gv)
        return dr + dx, dg
    return _rowwise(name, fn, [_rows(dres), _rows(dh), _rows(x), _full(g)],
                    [(x.shape, F32, "rows"), (g.shape, F32, "acc")], rows=x.shape[0])


def _layer_fwd(l, x, mem, w, p, tabs, swa_bias, S):
    c, ck, s1, s2 = tabs
    n = f"l{l}_"
    h = _rmsnorm(n + "attn_norm", x, p["attn_norm"], BF16)
    gates = _mm(n + "proj_gates", h, w["wag"], [BF16], tn=1024)
    proj_a = _mm(n + "proj_tail", h, w["wat"], [F32])
    proj_b = _mm(n + "proj_b", h, w["wb"], [BF16])

    def prep(cq, kva, qn, kvn, ckv, s1v, s2v):
        cqn = cq * _rstd(cq) * qn
        ckv_ = kva[:, :128]
        ckvn = ckv_ * _rstd(ckv_) * kvn
        pe = pltpu.roll(kva[:, 128:], 64, 1)
        return cqn, ckvn, _rope_fwd(pe, ckv, s1v, s2v)

    cqn, ckvn, kpe = _rowwise(
        n + "mla_prep", prep,
        [_rows(proj_a, 256, 0), _rows(proj_a, 256, 1), _full(p["mla_q_norm"]), _full(p["mla_kv_norm"]),
         _rows(ck), _rows(s1), _rows(s2)],
        [((S, 256), BF16, "rows"), ((S, 128), BF16, "rows"), ((S, 128), F32, "rows")], rows=S)

    q_mla = _mm(n + "q_mla", cqn, w["wuq"], [BF16],
                epi=lambda acc, cv, s1v, s2v: (_rope_fwd(acc, cv, s1v, s2v) * (MLA_SCALE * LOG2E),),
                extras=[(c, "m"), (s1, "m"), (s2, "m")])
    k_mla = _mm(n + "k_mla", ckvn, w["wuk"], [BF16],
                epi=lambda acc, kp: (acc + _tile_lanes(kp, acc.shape[1]),), extras=[(kpe, "m")])
    den = ((jnp.arange(MLA_HEADS * HEAD_PAD) % HEAD_PAD) // 2 == DEN_LANE // 2).astype(F32)[None]
    v_mla = _mm(n + "v_mla", ckvn, w["wuv"], [BF16], epi=lambda acc, dv: (acc + dv,), extras=[(den, "n")])
    o_mla, lse_mla = _causal_fwd(n + "mla_fwd", q_mla, k_mla, v_mla, heads=MLA_HEADS, tile=MLA_TILE,
                                 chunk=MLA_CHUNK_FWD)
    o_swa, lse_swa = _swa_fwd(n + "swa_fwd", proj_b, swa_bias, p["sinks"], tq=SWA_TQ)
    mn = _rmsnorm(n + "mem_norm", mem, p["mem_norm"], BF16)
    kvm = _mm(n + "kv_mem", mn, w["wmem"], [BF16])
    o_mem, lse_mem = _mem_fwd(n + "mem_fwd", proj_b, kvm, tq=MEM_TQ, chunk=MEM_CHUNK)
    t0 = _mm(n + "t_mla", o_mla, w["wo_mla"], [BF16], tn=1024)
    t1 = _mm(n + "t_swa", o_swa, w["wo_swa"], [BF16], tn=1024)
    t2 = _mm(n + "t_mem", o_mem, w["wo_mem"], [BF16], tn=1024)

    def merge(g0, g1, g2, bg, a0, a1, a2):
        y = (_sigmoid(g0 + bg[:, 0:1024]) * a0 + _sigmoid(g1 + bg[:, 1024:2048]) * a1
             + _sigmoid(g2 + bg[:, 2048:3072]) * a2)
        return (y,)

    y = _rowwise(n + "merge", merge,
                 [_rows(gates, 1024, 0), _rows(gates, 1024, 1), _rows(gates, 1024, 2), _full(p["b_gate"]),
                  _rows(t0), _rows(t1), _rows(t2)], [((S, D_MODEL), BF16, "rows")], rows=S)[0]
    x1 = _mm(n + "out_proj", y, w["wout"], [F32], epi=lambda acc, r: (acc + r,), extras=[(x, "mn")])
    h2 = _rmsnorm(n + "mlp_norm", x1, p["mlp_norm"], BF16)
    act = _mm(n + "mlp_up", h2, w["wup"], [BF16], epi=lambda acc: (jnp.square(jnp.maximum(acc, 0.0)),), tn=1024)
    x2 = _mm(n + "mlp_down", act, w["wdown"], [F32], epi=lambda acc, r: (acc + r,), extras=[(x1, "mn")])
    saved = dict(x=x, h=h, gates=gates, proj_a=proj_a, proj_b=proj_b, cqn=cqn, ckvn=ckvn, q_mla=q_mla, k_mla=k_mla, v_mla=v_mla,
                 o_mla=o_mla, lse_mla=lse_mla, o_swa=o_swa, lse_swa=lse_swa, mn=mn, kvm=kvm, o_mem=o_mem,
                 lse_mem=lse_mem, t0=t0, t1=t1, t2=t2, y=y, x1=x1, h2=h2, act=act)
    return x2, saved


def _layer_bwd(l, dx2, mem, w, p, tabs, swa_bias, sv, S):
    c, ck, s1, s2 = tabs
    n = f"l{l}_b_"
    gw = {}
    gs = {}
    du = _mm(n + "d_act", dx2, w["wdown_t"], [BF16],
             epi=lambda acc, av: (acc * (2.0 * jnp.sqrt(av.astype(F32))),), extras=[(sv["act"], "mn")], tn=1024)
    gw["wdown"] = _mm_tn(n + "g_wdown", sv["act"], dx2)
    gw["wup"] = _mm_tn(n + "g_wup", sv["h2"], du)
    dh2 = _mm(n + "d_h2", du, w["wup_t"], [F32])
    dx1, gs["mlp_norm"] = _residual_norm_bwd(n + "mlp_norm", dx2, dh2, sv["x1"], p["mlp_norm"])
    gw["wout"] = _mm_tn(n + "g_wout", sv["y"], dx1)
    dy = _mm(n + "d_y", dx1, w["wout_t"], [F32])

    def merge_bwd(dyv, g0, g1, g2, bg, a0, a1, a2):
        outs, dgs = [], []
        for b, (gv, av) in enumerate(((g0, a0), (g1, a1), (g2, a2))):
            sg = _sigmoid(gv + bg[:, b * 1024:(b + 1) * 1024])
            outs.append(dyv * sg)
            dgs.append(dyv * av * sg * (1.0 - sg))
        dg = jnp.concatenate(dgs, axis=1)
        return outs[0], outs[1], outs[2], dg, jnp.sum(dg, axis=0, keepdims=True)

    pa = sv["proj_a"]
    gt = sv["gates"]
    dt0, dt1, dt2, dgates, gs["b_gate"] = _rowwise(
        n + "merge", merge_bwd,
        [_rows(dy), _rows(gt, 1024, 0), _rows(gt, 1024, 1), _rows(gt, 1024, 2), _full(p["b_gate"]),
         _rows(sv["t0"]), _rows(sv["t1"]), _rows(sv["t2"])],
        [((S, D_MODEL), BF16, "rows")] * 3 + [((S, 3 * D_MODEL), BF16, "rows"), ((1, 3 * D_MODEL), F32, "acc")],
        rows=S)
    gw["wo_mla"] = _mm_tn(n + "g_wo_mla", sv["o_mla"], dt0)
    gw["wo_swa"] = _mm_tn(n + "g_wo_swa", sv["o_swa"], dt1)
    gw["wo_mem"] = _mm_tn(n + "g_wo_mem", sv["o_mem"], dt2)
    do_mla = _mm(n + "d_o_mla", dt0, w["wo_mla_t"], [BF16], epi=lambda acc, ov: (_with_neg_delta(acc, ov),),
                 extras=[(sv["o_mla"], "mn")], tn=MLA_HEADS * HEAD_PAD)
    do_swa = _mm(n + "d_o_swa", dt1, w["wo_swa_t"], [BF16])
    do_mem = _mm(n + "d_o_mem", dt2, w["wo_mem_t"], [BF16])
    pb = sv["proj_b"]
    dq_mla, dk_mla, dv_mla = _causal_bwd(
        n + "mla_bwd", sv["q_mla"], sv["k_mla"], sv["v_mla"], do_mla, sv["lse_mla"], heads=MLA_HEADS,
        tile=MLA_TILE_BWD, chunk=MLA_CHUNK)
    dq_swa, dk_swa, dv_swa, dk_edge, dv_edge, dbias, dsink = _swa_bwd(
        n + "swa_bwd", pb, swa_bias, p["sinks"], sv["o_swa"], do_swa, sv["lse_swa"], tq=SWA_TQ)
    dq_mem, dkvm = _mem_bwd(n + "mem_bwd", pb, sv["kvm"], sv["o_mem"], do_mem, sv["lse_mem"], tq=MEM_TQ,
                            chunk=MEM_CHUNK)
    gs["dbias"] = dbias
    gs["sinks"] = dsink[:, 0, 0]
    gw["wmem"] = _mm_tn(n + "g_wmem", sv["mn"], dkvm)
    dmn = _mm(n + "d_mn", dkvm, w["wmem_t"], [F32])
    _, gs["mem_norm"] = _residual_norm_bwd(n + "mem_norm", dmn, dmn, mem, p["mem_norm"])
    dq_pre = _rowwise(n + "q_unrope", lambda d, cv, s1v, s2v: (_rope_bwd(d * MLA_SCALE, cv, s1v, s2v),),
                      [_rows(dq_mla), _rows(c), _rows(s1), _rows(s2)], [((S, 1024), BF16, "rows")], rows=S)[0]
    gw["wuq"] = _mm_tn(n + "g_wuq", sv["cqn"], dq_pre)
    gw["wuk"] = _mm_tn(n + "g_wuk", sv["ckvn"], dk_mla)
    gw["wuv"] = _mm_tn(n + "g_wuv", sv["ckvn"], dv_mla)
    dcqn = _mm(n + "d_cqn", dq_pre, w["wuq_t"], [F32])
    dckvn = _mm(n + "d_ckvn_k", dk_mla, w["wuk_t"], [F32])
    dckvn = _mm(n + "d_ckvn_v", dv_mla, w["wuv_t"], [F32], epi=lambda acc, r: (acc + r,), extras=[(dckvn, "mn")])

    def mla_norm_bwd(dcq_n, dckv_n, dk, cq, kva, qn, kvn, ckv, s1v, s2v):
        dcq, dqn = _norm_bwd(dcq_n, cq, qn)
        dckv, dkvn = _norm_bwd(dckv_n, kva[:, :128], kvn)
        dkpe = dk[:, 0:128]
        for hh in range(1, MLA_HEADS):
            dkpe = dkpe + dk[:, hh * 128:(hh + 1) * 128]
        dpe = pltpu.roll(_rope_bwd(dkpe, ckv, s1v, s2v), 64, 1)
        return jnp.concatenate([dcq, dckv, dpe], axis=1), dqn, dkvn

    dtail, gs["mla_q_norm"], gs["mla_kv_norm"] = _rowwise(
        n + "mla_norm", mla_norm_bwd,
        [_rows(dcqn), _rows(dckvn), _rows(dk_mla), _rows(pa, 256, 0), _rows(pa, 256, 1),
         _full(p["mla_q_norm"]), _full(p["mla_kv_norm"]), _rows(ck), _rows(s1), _rows(s2)],
        [((S, 512), BF16, "rows"), ((1, 256), F32, "acc"), ((1, 128), F32, "acc")], rows=S)

    dproj_b = _dproj_b(n + "dproj_b", dq_swa, dq_mem, dk_swa, dv_swa, dk_edge, dv_edge, tq=SWA_TQ)
    h = sv["h"]
    gw["wag"] = _mm_tn(n + "g_wa_gates", h, dgates)
    gw["wat"] = _mm_tn(n + "g_wa_tail", h, dtail)
    gw["wb"] = _mm_tn(n + "g_wb", h, dproj_b)
    dh = _mm(n + "d_h_gates", dgates, w["wag_t"], [F32])
    dh = _mm(n + "d_h_tail", dtail, w["wat_t"], [F32], epi=lambda acc, r: (acc + r,), extras=[(dh, "mn")])
    dh = _mm(n + "d_h_b", dproj_b, w["wb_t"], [F32], epi=lambda acc, r: (acc + r,), extras=[(dh, "mn")])
    dx, gs["attn_norm"] = _residual_norm_bwd(n + "attn_norm", dx1, dh, sv["x"], p["attn_norm"])
    return dx, gw, gs


def _local_step(x, mem, loss_target, full, small):
    S = x.shape[0]
    tabs = _rope_tables(S)
    onehot, band = _bias_onehot()
    hi = lax.Precision.HIGHEST
    swa_bias = _mm("swa_bias", small["rel_bias"].T, onehot, [F32], epi=lambda acc, mk: (acc + mk,),
                   extras=[(band, "n")], cast=None, precision=hi, tn=8192).reshape(SWA_HEADS, WINDOW, 2 * WINDOW)
    ws, ps = [], []
    for l in range(DEPTH):
        ws.append(_layer_weights(full, l))
        ps.append(dict(
            attn_norm=small["attn_norm"][l][None], mem_norm=small["mem_norm"][l][None],
            b_gate=small["b_gate"][l][None], mla_q_norm=small["mla_q_norm"][l][None],
            mla_kv_norm=small["mla_kv_norm"][l][None], mlp_norm=small["mlp_norm"][l][None],
            sinks=jnp.broadcast_to(small["attn_sinks"][l][:, None, None], (SWA_HEADS, 8, 128))))
    saved = []
    xc = x
    for l in range(DEPTH):
        xc, sv = _layer_fwd(l, xc, mem, ws[l], ps[l], tabs, swa_bias, S)
        saved.append(sv)

    fn_g = small["final_norm"][None]

    def loss_fn(xv, gv, tv):
        r = _rstd(xv)
        xh = xv * r
        err = xh * gv - tv
        dyv = err * (1.0 / D_MODEL)
        wv = dyv * gv
        dx = r * (wv - xh * jnp.mean(wv * xh, axis=-1, keepdims=True))
        part = 0.5 * jnp.sum(err * err) * (1.0 / D_MODEL)
        return dx, jnp.sum(dyv * xh, axis=0, keepdims=True), jnp.zeros((8, 128), F32) + part

    dx, g_final, loss_acc = _rowwise(
        "loss", loss_fn, [_rows(xc), _full(fn_g), _rows(loss_target)],
        [((S, D_MODEL), F32, "rows"), ((1, D_MODEL), F32, "acc"), ((8, 128), F32, "acc")], rows=S)

    gws, gss = [None] * DEPTH, [None] * DEPTH
    for l in reversed(range(DEPTH)):
        dx, gw, gs = _layer_bwd(l, dx, mem, ws[l], ps[l], tabs, swa_bias, saved[l], S)
        gws[l] = _layer_weight_grads(gw)
        gss[l] = gs

    dbias = (gss[0]["dbias"] + gss[1]["dbias"]).reshape(SWA_HEADS, -1)
    g_rel = _mm("g_rel_bias", dbias, onehot.T, [F32], cast=None, precision=hi, tk=8192).T
    wgrads = {k: jnp.stack([gws[l][k] for l in range(DEPTH)]) for k in gws[0]}
    sgrads = dict(
        rel_bias=g_rel,
        final_norm=g_final[0],
        attn_sinks=jnp.stack([gss[l]["sinks"] for l in range(DEPTH)]),
        **{k: jnp.concatenate([gss[l][k] for l in range(DEPTH)], axis=0)
           for k in ("attn_norm", "mem_norm", "b_gate", "mla_q_norm", "mla_kv_norm", "mlp_norm")})
    return loss_acc[0, 0], dx, wgrads, sgrads


def _pack_small(vals, loss):
    rows = []
    for name, shape in SMALL:
        flat = vals[name].astype(F32).reshape(-1)
        pad = (-flat.shape[0]) % 1024
        rows.append(jnp.pad(flat, (0, pad)).reshape(-1, 128))
    rows.append(jnp.zeros((8, 128), F32) + loss)
    return jnp.concatenate(rows, axis=0)


def _unpack_small(packed):
    out, r = {}, 0
    for name, shape in SMALL:
        size = math.prod(shape)
        nrows = 8 * -(-size // 1024)
        out[name] = packed[r:r + nrows].reshape(-1)[:size].reshape(shape)
        r += nrows
    return out, packed[r, 0]


def kernel(x, mem, rel_bias, attn_norm, mem_norm, w_in, b_gate, mla_q_norm, w_uq, mla_kv_norm, w_ukv, attn_sinks, w_mem_kv, w_o_mla, w_o_swa, w_o_mem, w_out, mlp_norm, w_up, w_down, final_norm, loss_target, m_rel_bias, m_attn_norm, m_mem_norm, m_w_in, m_b_gate, m_mla_q_norm, m_w_uq, m_mla_kv_norm, m_w_ukv, m_attn_sinks, m_w_mem_kv, m_w_o_mla, m_w_o_swa, m_w_o_mem, m_w_out, m_mlp_norm, m_w_up, m_w_down, m_final_norm, v_rel_bias, v_attn_norm, v_mem_norm, v_w_in, v_b_gate, v_mla_q_norm, v_w_uq, v_mla_kv_norm, v_w_ukv, v_attn_sinks, v_w_mem_kv, v_w_o_mla, v_w_o_swa, v_w_o_mem, v_w_out, v_mlp_norm, v_w_up, v_w_down, v_final_norm):
    wv = dict(rel_bias=rel_bias, attn_norm=attn_norm, mem_norm=mem_norm, w_in=w_in, b_gate=b_gate,
              mla_q_norm=mla_q_norm, w_uq=w_uq, mla_kv_norm=mla_kv_norm, w_ukv=w_ukv, attn_sinks=attn_sinks,
              w_mem_kv=w_mem_kv, w_o_mla=w_o_mla, w_o_swa=w_o_swa, w_o_mem=w_o_mem, w_out=w_out,
              mlp_norm=mlp_norm, w_up=w_up, w_down=w_down, final_norm=final_norm)
    mv = dict(rel_bias=m_rel_bias, attn_norm=m_attn_norm, mem_norm=m_mem_norm, w_in=m_w_in, b_gate=m_b_gate,
              mla_q_norm=m_mla_q_norm, w_uq=m_w_uq, mla_kv_norm=m_mla_kv_norm, w_ukv=m_w_ukv,
              attn_sinks=m_attn_sinks, w_mem_kv=m_w_mem_kv, w_o_mla=m_w_o_mla, w_o_swa=m_w_o_swa,
              w_o_mem=m_w_o_mem, w_out=m_w_out, mlp_norm=m_mlp_norm, w_up=m_w_up, w_down=m_w_down,
              final_norm=m_final_norm)
    vv = dict(rel_bias=v_rel_bias, attn_norm=v_attn_norm, mem_norm=v_mem_norm, w_in=v_w_in, b_gate=v_b_gate,
              mla_q_norm=v_mla_q_norm, w_uq=v_w_uq, mla_kv_norm=v_mla_kv_norm, w_ukv=v_w_ukv,
              attn_sinks=v_attn_sinks, w_mem_kv=v_w_mem_kv, w_o_mla=v_w_o_mla, w_o_swa=v_w_o_swa,
              w_o_mem=v_w_o_mem, w_out=v_w_out, mlp_norm=v_mlp_norm, w_up=v_w_up, w_down=v_w_down,
              final_norm=v_final_norm)

    shard_rows = [math.prod(_shard_shape(shape, axis)) // 128 for _, shape, axis in WSPECS]
    gathered = _gather_forwarded("gather_weights", _pack_rows([wv[name].astype(BF16) for name, _, _ in WSPECS]))
    full, r = {}, 0
    for (name, shape, axis), nr in zip(WSPECS, shard_rows):
        full[name] = _unstack(gathered[:, r:r + nr].reshape((N_DEV,) + _shard_shape(shape, axis)), shape, axis)
        r += nr

    loss_part, grad_x, wgrads, sgrads = _local_step(x[0], mem[0], loss_target[0], full,
                                                    {name: wv[name] for name, _ in SMALL})

    recv = _reduce_scatter([_restack(wgrads[name], axis).astype(BF16).reshape(N_DEV, -1, 128)
                            for name, _, axis in WSPECS])
    outs = _adam("adam_sharded", recv, *[_pack_rows([d[name] for name, _, _ in WSPECS]) for d in (wv, mv, vv)])
    res = {}
    r = 0
    for (name, shape, axis), nr in zip(WSPECS, shard_rows):
        res[name] = [o[r:r + nr].reshape(_shard_shape(shape, axis)) for o in outs]
        r += nr

    small_recv = _exchange("gather_small", _pack_small(sgrads, loss_part), per_peer=False)
    zero = jnp.zeros((), F32)
    souts = _adam("adam_small", small_recv, *[_pack_small(d, zero) for d in (wv, mv, vv)])
    loss = None
    for i, o in enumerate(souts):
        vals, extra = _unpack_small(o)
        if i == 0:
            loss = extra
        for name, _ in SMALL:
            res.setdefault(name, []).append(vals[name])

    out = [loss, grad_x[None]]
    for i in range(4):
        out.extend(res[name][i] for name in WEIGHT_ORDER)
    return tuple(out)
```

```python
import math

import jax
import jax.numpy as jnp
from jax import lax
from jax.experimental import pallas as pl
from jax.experimental.pallas import tpu as pltpu

F32 = jnp.float32
BF16 = jnp.bfloat16

N_DEV = 8
D_MODEL = 1024
DEPTH = 2
MLA_HEADS = 8
MLA_Q_LORA = 256
MLA_KV_LORA = 128
MLA_NOPE = 64
MLA_ROPE = 32
MLA_V = 64
ROPE_THETA = 10000.0
SWA_HEADS = 8
SWA_KV_HEADS = 2
SWA_HD = 64
WINDOW = 128
REL_BUCKETS = 32
REL_MAX_DIST = 128
MEM_LEN = 256
MEM_HEADS = 4
MEM_HD = 128
D_FF = 4 * D_MODEL
EPS = 1e-6
HEAD_PAD = 128
ADAM_LR = 0.001
ADAM_B1 = 0.9
ADAM_B2 = 0.999
ADAM_EPS = 1e-08
ADAM_WD = 0.01
ADAM_STEP = 10

NEG = -1e30
VMEM_LIMIT = 48 * 1024 * 1024

MM_TM = 1024
MM_TN = 1024
MM_TK = 1024
TN_T1 = 1024
TN_TN = 1024
TN_TS = 2048
MM_TM_BF16 = 2048
ROW_TM = 256
MLA_TILE = 4096
MLA_TILE_BWD = 2048
MLA_CHUNK = 256
MLA_CHUNK_FWD = 512
MLA_QK = MLA_NOPE + MLA_ROPE
MLA_SCALE = MLA_QK ** -0.5
LOG2E = math.log2(math.e)
DEN_LANE = MLA_V
SWA_TQ = 1024
MEM_TQ = 1024
MEM_CHUNK = 256
ADAM_TM = 1200

WSPECS = (
    ("w_in", (DEPTH, D_MODEL, 4768), 2),
    ("w_uq", (DEPTH, MLA_Q_LORA, 768), 2),
    ("w_ukv", (DEPTH, MLA_KV_LORA, 1024), 2),
    ("w_mem_kv", (DEPTH, D_MODEL, 1024), 1),
    ("w_o_mla", (DEPTH, 512, D_MODEL), 2),
    ("w_o_swa", (DEPTH, 512, D_MODEL), 2),
    ("w_o_mem", (DEPTH, 512, D_MODEL), 2),
    ("w_out", (DEPTH, D_MODEL, D_MODEL), 1),
    ("w_up", (DEPTH, D_MODEL, D_FF), 2),
    ("w_down", (DEPTH, D_FF, D_MODEL), 1),
)
SMALL = (
    ("rel_bias", (REL_BUCKETS, SWA_HEADS)),
    ("attn_norm", (DEPTH, D_MODEL)),
    ("mem_norm", (DEPTH, D_MODEL)),
    ("b_gate", (DEPTH, 3 * D_MODEL)),
    ("mla_q_norm", (DEPTH, MLA_Q_LORA)),
    ("mla_kv_norm", (DEPTH, MLA_KV_LORA)),
    ("attn_sinks", (DEPTH, SWA_HEADS)),
    ("mlp_norm", (DEPTH, D_MODEL)),
    ("final_norm", (D_MODEL,)),
)
WEIGHT_ORDER = ("rel_bias", "attn_norm", "mem_norm", "w_in", "b_gate", "mla_q_norm", "w_uq", "mla_kv_norm",
                "w_ukv", "attn_sinks", "w_mem_kv", "w_o_mla", "w_o_swa", "w_o_mem", "w_out", "mlp_norm",
                "w_up", "w_down", "final_norm")


def _cparams(*sem):
    return pltpu.CompilerParams(dimension_semantics=sem, vmem_limit_bytes=VMEM_LIMIT)


def _shard_shape(shape, axis):
    s = list(shape)
    s[axis] //= N_DEV
    return tuple(s)


def _mm(name, a, b, out_dtypes, *, epi=None, extras=(), a_fn=None, cast=BF16, precision=None,
        tm=None, tn=None, tk=None, col_sums=0):
    M, K = a.shape
    K2, N = b.shape
    assert K == K2, (name, a.shape, b.shape)
    tm = min(tm or MM_TM, M)
    tn = min(tn or MM_TN, N)
    tk = min(tk or MM_TK, K)
    assert M % tm == 0 and N % tn == 0 and K % tk == 0, (name, a.shape, b.shape, tm, tn, tk)
    assert col_sums == 0 or tn == N, (name, tn, N)
    nk = K // tk
    n_ex = len(extras)
    n_out = len(out_dtypes)

    def body(*refs):
        a_ref, b_ref = refs[0], refs[1]
        ex_refs = refs[2:2 + n_ex]
        out_refs = refs[2 + n_ex:2 + n_ex + n_out]
        av = a_ref[...]
        if a_fn is not None:
            av = a_fn(av)
        bv = b_ref[...]
        if cast is not None:
            av = av.astype(cast)
            bv = bv.astype(cast)
        part = jnp.dot(av, bv, preferred_element_type=F32, precision=precision)

        def finish(acc):
            outs = epi(acc, *[r[...] for r in ex_refs]) if epi is not None else (acc,)
            for r, o in zip(out_refs, outs[:n_out]):
                r[...] = o.astype(r.dtype)
            i = pl.program_id(0)
            for r, o in zip(refs[2 + n_ex + n_out:2 + n_ex + n_out + col_sums], outs[n_out:]):
                @pl.when(i == 0)
                def _(r=r, o=o):
                    r[...] = o

                @pl.when(i > 0)
                def _(r=r, o=o):
                    r[...] += o

        if nk == 1:
            finish(part)
        else:
            acc_ref = refs[-1]
            k = pl.program_id(2)

            @pl.when(k == 0)
            def _():
                acc_ref[...] = part

            @pl.when(k > 0)
            def _():
                acc_ref[...] += part

            @pl.when(k == nk - 1)
            def _():
                finish(acc_ref[...])

    in_specs = [pl.BlockSpec((tm, tk), lambda i, j, k: (i, k)),
                pl.BlockSpec((tk, tn), lambda i, j, k: (k, j))]
    for arr, kind in extras:
        if kind == "mn":
            in_specs.append(pl.BlockSpec((tm, tn), lambda i, j, k: (i, j)))
        elif kind == "m":
            in_specs.append(pl.BlockSpec((tm, arr.shape[1]), lambda i, j, k: (i, 0)))
        else:
            in_specs.append(pl.BlockSpec((1, tn), lambda i, j, k: (0, j)))
    outs = pl.pallas_call(
        body, name=name, grid=(M // tm, N // tn, nk),
        in_specs=in_specs,
        out_specs=([pl.BlockSpec((tm, tn), lambda i, j, k: (i, j)) for _ in out_dtypes]
                   + [pl.BlockSpec((1, tn), lambda i, j, k: (0, 0))] * col_sums),
        out_shape=([jax.ShapeDtypeStruct((M, N), dt) for dt in out_dtypes]
                   + [jax.ShapeDtypeStruct((1, N), F32)] * col_sums),
        scratch_shapes=[pltpu.VMEM((tm, tn), F32)] if nk > 1 else [],
        compiler_params=(_cparams("arbitrary", "arbitrary", "arbitrary") if col_sums
                         else _cparams("parallel", "parallel", "arbitrary")),
    )(a, b, *[arr for arr, _ in extras])
    return outs[0] if n_out + col_sums == 1 else outs


def _mm_tn(name, a, b, *, t1=None, tn=None, ts=None):
    S, K1 = a.shape
    S2, N = b.shape
    assert S == S2, (name, a.shape, b.shape)
    t1 = min(t1 or TN_T1, K1)
    tn = min(tn or TN_TN, N)
    ts = min(ts or TN_TS, S)
    assert K1 % t1 == 0 and N % tn == 0 and S % ts == 0, (name, a.shape, b.shape)

    def body(a_ref, b_ref, o_ref):
        s = pl.program_id(2)
        part = lax.dot_general(a_ref[...].astype(BF16), b_ref[...].astype(BF16),
                               (((0,), (0,)), ((), ())), preferred_element_type=F32)

        @pl.when(s == 0)
        def _():
            o_ref[...] = part

        @pl.when(s > 0)
        def _():
            o_ref[...] += part

    return pl.pallas_call(
        body, name=name, grid=(K1 // t1, N // tn, S // ts),
        in_specs=[pl.BlockSpec((ts, t1), lambda i, j, s: (s, i)),
                  pl.BlockSpec((ts, tn), lambda i, j, s: (s, j))],
        out_specs=pl.BlockSpec((t1, tn), lambda i, j, s: (i, j)),
        out_shape=jax.ShapeDtypeStruct((K1, N), F32),
        compiler_params=_cparams("parallel", "parallel", "arbitrary"),
    )(a, b)


def _rows(arr, width=None, blk=0):
    return (arr, ("rows", arr.shape[1] if width is None else width, blk))


def _full(arr):
    return (arr, ("full",))


def _rowwise(name, fn, ins, outs, *, rows, tm=None):
    tm = min(tm or ROW_TM, rows)
    assert rows % tm == 0, (name, rows, tm)
    n_in = len(ins)

    def body(*refs):
        i = pl.program_id(0)
        vals = fn(*[r[...] for r in refs[:n_in]])
        for (shape, dt, kind), r, v in zip(outs, refs[n_in:], vals):
            if kind == "rows":
                r[...] = v.astype(dt)
            else:
                @pl.when(i == 0)
                def _(r=r, v=v):
                    r[...] = v

                @pl.when(i > 0)
                def _(r=r, v=v):
                    r[...] += v

    in_specs = []
    for arr, spec in ins:
        if spec[0] == "rows":
            in_specs.append(pl.BlockSpec((tm, spec[1]), lambda i, b=spec[2]: (i, b)))
        else:
            in_specs.append(pl.BlockSpec(arr.shape, lambda i, n=arr.ndim: (0,) * n))
    out_specs = []
    for shape, dt, kind in outs:
        if kind == "rows":
            out_specs.append(pl.BlockSpec((tm, shape[1]), lambda i: (i, 0)))
        else:
            out_specs.append(pl.BlockSpec(shape, lambda i, n=len(shape): (0,) * n))
    res = pl.pallas_call(
        body, name=name, grid=(rows // tm,),
        in_specs=in_specs, out_specs=out_specs,
        out_shape=[jax.ShapeDtypeStruct(shape, dt) for shape, dt, _ in outs],
        compiler_params=_cparams("arbitrary"),
    )(*[arr for arr, _ in ins])
    return res


MEM_SCALE = MEM_HD ** -0.5
MEM_Q0 = 2
NT_DIMS = (((1,), (1,)), ((), ()))


def _head_lanes(h):
    return slice(h * HEAD_PAD, (h + 1) * HEAD_PAD)


def _mem_fwd(name, proj_b, kvm, *, tq, chunk):
    S = proj_b.shape[0]
    tq = min(tq, S)
    C = min(chunk, tq)
    tiles = [(h, c) for c in range(tq // C) for h in range(MEM_HEADS)]

    def body(q_ref, kv_ref, o_ref, lse_ref):
        def logits(h, c):
            return lax.dot_general(q_ref[c * C:(c + 1) * C, _head_lanes(h)], kv_ref[:, _head_lanes(h)], NT_DIMS,
                                   preferred_element_type=F32) * MEM_SCALE

        nxt = logits(*tiles[0])
        for n, (h, c) in enumerate(tiles):
            s = nxt
            if n + 1 < len(tiles):
                nxt = logits(*tiles[n + 1])
            rows = slice(c * C, (c + 1) * C)
            m = jnp.max(s, axis=1, keepdims=True)
            p = jnp.exp(s - m)
            l = jnp.sum(p, axis=1, keepdims=True)
            o = jnp.dot(p.astype(BF16), kv_ref[:, _head_lanes(MEM_HEADS + h)], preferred_element_type=F32) / l
            o_ref[rows, _head_lanes(h)] = o.astype(o_ref.dtype)
            lse_ref[h, rows, :] = m + jnp.log(l)

    return pl.pallas_call(
        body, name=name, grid=(S // tq,),
        in_specs=[pl.BlockSpec((tq, MEM_HEADS * HEAD_PAD), lambda i: (i, MEM_Q0)),
                  pl.BlockSpec(kvm.shape, lambda i: (0, 0))],
        out_specs=[pl.BlockSpec((tq, MEM_HEADS * HEAD_PAD), lambda i: (i, 0)),
                   pl.BlockSpec((MEM_HEADS, tq, 1), lambda i: (0, i, 0))],
        out_shape=[jax.ShapeDtypeStruct((S, MEM_HEADS * HEAD_PAD), BF16),
                   jax.ShapeDtypeStruct((MEM_HEADS, S, 1), F32)],
        compiler_params=_cparams("parallel"),
    )(proj_b, kvm)


def _mem_bwd(name, proj_b, kvm, o, do, lse, *, tq, chunk):
    S = proj_b.shape[0]
    tq = min(tq, S)
    C = min(chunk, tq)
    nq = S // tq
    tiles = [(h, c) for c in range(tq // C) for h in range(MEM_HEADS)]

    def body(q_ref, kv_ref, o_ref, do_ref, lse_ref, dq_ref, dkv_ref, acc_sc):
        i = pl.program_id(0)

        @pl.when(i == 0)
        def _():
            acc_sc[...] = jnp.zeros(acc_sc.shape, F32)

        def mats(h, c):
            rows = slice(c * C, (c + 1) * C)
            q = q_ref[rows, _head_lanes(h)]
            dov = do_ref[rows, _head_lanes(h)]
            s = lax.dot_general(q, kv_ref[:, _head_lanes(h)], NT_DIMS, preferred_element_type=F32) * MEM_SCALE
            dp = lax.dot_general(dov, kv_ref[:, _head_lanes(MEM_HEADS + h)], NT_DIMS, preferred_element_type=F32)
            return q, dov, s, dp

        nxt = mats(*tiles[0])
        for n, (h, c) in enumerate(tiles):
            q, dov, s, dp = nxt
            if n + 1 < len(tiles):
                nxt = mats(*tiles[n + 1])
            rows = slice(c * C, (c + 1) * C)
            p = jnp.exp(s - lse_ref[h, rows, :])
            delta = jnp.sum(dov.astype(F32) * o_ref[rows, _head_lanes(h)].astype(F32), axis=1, keepdims=True)
            ds = (p * (dp - delta) * MEM_SCALE).astype(BF16)
            dq_ref[rows, _head_lanes(h)] = jnp.dot(ds, kv_ref[:, _head_lanes(h)],
                                                   preferred_element_type=F32).astype(dq_ref.dtype)
            acc_sc[_head_lanes(h), :] += jnp.dot(q.T, ds, preferred_element_type=F32)
            acc_sc[_head_lanes(MEM_HEADS + h), :] += jnp.dot(dov.T, p.astype(BF16), preferred_element_type=F32)

        @pl.when(i == nq - 1)
        def _():
            dkv_ref[...] = acc_sc[...].T

    qblk = pl.BlockSpec((tq, MEM_HEADS * HEAD_PAD), lambda i: (i, 0))
    return pl.pallas_call(
        body, name=name, grid=(nq,),
        in_specs=[pl.BlockSpec((tq, MEM_HEADS * HEAD_PAD), lambda i: (i, MEM_Q0)),
                  pl.BlockSpec(kvm.shape, lambda i: (0, 0)), qblk, qblk,
                  pl.BlockSpec((MEM_HEADS, tq, 1), lambda i: (0, i, 0))],
        out_specs=[qblk, pl.BlockSpec(kvm.shape, lambda i: (0, 0))],
        out_shape=[jax.ShapeDtypeStruct((S, MEM_HEADS * HEAD_PAD), BF16), jax.ShapeDtypeStruct(kvm.shape, F32)],
        scratch_shapes=[pltpu.VMEM((kvm.shape[1], kvm.shape[0]), F32)],
        compiler_params=_cparams("arbitrary"),
    )(proj_b, kvm, o, do, lse)


def _causal_fwd(name, q_arr, k_arr, v_arr, *, heads, tile, chunk):
    S = q_arr.shape[0]
    T = min(tile, S)
    C = min(chunk, T)
    nt = S // T
    nc = T // C

    pairs = [(qi, kk) for qi in range(nt) for kk in range(qi + 1)]
    q_tab = jnp.asarray([p[0] for p in pairs], jnp.int32)
    k_tab = jnp.asarray([p[1] for p in pairs], jnp.int32)

    def body(qt_ref, kt_ref, q_ref, k_ref, v_ref, o_ref, lse_ref, m_sc, acc_sc):
        t = pl.program_id(1)
        qi = qt_ref[t]
        kk = kt_ref[t]

        @pl.when(kk == 0)
        def _():
            m_sc[...] = jnp.full(m_sc.shape, NEG, F32)
            acc_sc[...] = jnp.zeros(acc_sc.shape, F32)

        def logits(c, ncols, masked):
            s = lax.dot_general(q_ref[pl.ds(c * C, C), :], k_ref[0:ncols, :], (((1,), (1,)), ((), ())),
                                preferred_element_type=F32)
            if masked:
                r = c * C + lax.broadcasted_iota(jnp.int32, (C, ncols), 0)
                cidx = lax.broadcasted_iota(jnp.int32, (C, ncols), 1)
                s = jnp.where(cidx <= r, s, NEG)
            return s

        def update(c, ncols, s):
            rows = pl.ds(c * C, C)
            m_prev = m_sc[rows, :]
            m_new = jnp.maximum(m_prev, jnp.max(s, axis=1, keepdims=True))
            p = jnp.exp2(s - m_new).astype(BF16)
            acc_sc[rows, :] = jnp.exp2(m_prev - m_new) * acc_sc[rows, :] + jnp.dot(
                p, v_ref[0:ncols, :], preferred_element_type=F32)
            m_sc[rows, :] = m_new

        def tile_step(ncols_of, masked):
            s = logits(0, ncols_of(0), masked)
            for c in range(nc):
                s_next = logits(c + 1, ncols_of(c + 1), masked) if c + 1 < nc else None
                update(c, ncols_of(c), s)
                s = s_next

        @pl.when(kk < qi)
        def _():
            tile_step(lambda c: T, False)

        @pl.when(kk == qi)
        def _():
            tile_step(lambda c: (c + 1) * C, True)

        @pl.when(kk == qi)
        def _():
            acc = acc_sc[...]
            l = acc[:, DEN_LANE:DEN_LANE + 1]
            o_ref[...] = (acc / l).astype(o_ref.dtype)
            lse_ref[0] = m_sc[...] + jnp.log2(l)

    grid_spec = pltpu.PrefetchScalarGridSpec(
        num_scalar_prefetch=2, grid=(heads, len(pairs)),
        in_specs=[pl.BlockSpec((T, HEAD_PAD), lambda h, t, qt, kt: (qt[t], h)),
                  pl.BlockSpec((T, HEAD_PAD), lambda h, t, qt, kt: (kt[t], h)),
                  pl.BlockSpec((T, HEAD_PAD), lambda h, t, qt, kt: (kt[t], h))],
        out_specs=[pl.BlockSpec((T, HEAD_PAD), lambda h, t, qt, kt: (qt[t], h)),
                   pl.BlockSpec((1, T, 1), lambda h, t, qt, kt: (h, qt[t], 0))],
        scratch_shapes=[pltpu.VMEM((T, 1), F32), pltpu.VMEM((T, HEAD_PAD), F32)])
    return pl.pallas_call(
        body, name=name, grid_spec=grid_spec,
        out_shape=[jax.ShapeDtypeStruct((S, heads * HEAD_PAD), BF16),
                   jax.ShapeDtypeStruct((heads, S, 1), F32)],
        compiler_params=_cparams("parallel", "arbitrary"),
    )(q_tab, k_tab, q_arr, k_arr, v_arr)


def _with_neg_delta(do, o):
    lane = lax.broadcasted_iota(jnp.int32, (1, HEAD_PAD), 1)
    outs = []
    for h in range(do.shape[1] // HEAD_PAD):
        a = do[:, _head_lanes(h)]
        nd = -jnp.sum(a * o[:, _head_lanes(h)].astype(F32), axis=1, keepdims=True)
        hi = nd.astype(BF16).astype(F32)
        a = jnp.where(lane == DEN_LANE, hi, a)
        outs.append(jnp.where(lane == DEN_LANE + 1, nd - hi, a))
    return jnp.concatenate(outs, axis=1)


def _causal_bwd(name, q_arr, k_arr, v_arr, do_arr, lse, *, heads, tile, chunk):
    S = q_arr.shape[0]
    T = min(tile, S)
    C = min(chunk, T)
    nt = S // T
    nc = T // C

    pairs = [(kj, qq) for kj in range(nt) for qq in range(kj, nt)]
    k_tab = jnp.asarray([p[0] for p in pairs], jnp.int32)
    q_tab = jnp.asarray([p[1] for p in pairs], jnp.int32)

    def body(kt_ref, qt_ref, q_ref, k_ref, v_ref, do_ref, lse_ref, dq_ref, dk_ref, dv_ref, dk_sc, dv_sc):
        t = pl.program_id(1)
        kj = kt_ref[t]
        qq = qt_ref[t]
        qb = qq

        @pl.when(t == 0)
        def _():
            dq_ref[...] = jnp.zeros(dq_ref.shape, F32)

        @pl.when(qq == kj)
        def _():
            dk_sc[...] = jnp.zeros(dk_sc.shape, F32)
            dv_sc[...] = jnp.zeros(dv_sc.shape, F32)

        def logits(c, ncols, masked):
            rows = pl.ds(c * C, C)
            s = lax.dot_general(q_ref[rows, :], k_ref[0:ncols, :], (((1,), (1,)), ((), ())),
                                preferred_element_type=F32)
            if masked:
                r = c * C + lax.broadcasted_iota(jnp.int32, (C, ncols), 0)
                cidx = lax.broadcasted_iota(jnp.int32, (C, ncols), 1)
                s = jnp.where(cidx <= r, s, NEG)
            dp = lax.dot_general(do_ref[rows, :], v_ref[0:ncols, :], (((1,), (1,)), ((), ())),
                                 preferred_element_type=F32)
            return s, dp

        def update(c, ncols, s, dp):
            rows = pl.ds(c * C, C)
            p = jnp.exp2(s - lse_ref[0, rows, :])
            ds = (p * dp).astype(BF16)
            dv_sc[:, 0:ncols] += jnp.dot(do_ref[rows, 0:MLA_V].T, p.astype(BF16), preferred_element_type=F32)
            dk_sc[:, 0:ncols] += jnp.dot(q_ref[rows, 0:MLA_QK].T, ds, preferred_element_type=F32)
            row0 = pl.multiple_of(qb * T + c * C, C)
            dq_ref[pl.ds(row0, C), :] += jnp.dot(ds, k_ref[0:ncols, :], preferred_element_type=F32)

        def tile_step(ncols_of, masked):
            cur = logits(0, ncols_of(0), masked)
            for c in range(nc):
                nxt = logits(c + 1, ncols_of(c + 1), masked) if c + 1 < nc else None
                update(c, ncols_of(c), *cur)
                cur = nxt

        @pl.when(qq > kj)
        def _():
            tile_step(lambda c: T, False)

        @pl.when(qq == kj)
        def _():
            tile_step(lambda c: (c + 1) * C, True)

        @pl.when(qq == nt - 1)
        def _():
            dk_ref[:, 0:MLA_QK] = dk_sc[...].T * math.log(2.0)
            dk_ref[:, MLA_QK:] = jnp.zeros((T, HEAD_PAD - MLA_QK), F32)
            dv_ref[:, 0:MLA_V] = dv_sc[...].T
            dv_ref[:, MLA_V:] = jnp.zeros((T, HEAD_PAD - MLA_V), F32)

    qrow = pl.BlockSpec((T, HEAD_PAD), lambda h, t, kt, qt: (qt[t], h))
    krow = pl.BlockSpec((T, HEAD_PAD), lambda h, t, kt, qt: (kt[t], h))
    qcol = pl.BlockSpec((1, T, 1), lambda h, t, kt, qt: (h, qt[t], 0))
    grid_spec = pltpu.PrefetchScalarGridSpec(
        num_scalar_prefetch=2, grid=(heads, len(pairs)),
        in_specs=[qrow, krow, krow, qrow, qcol],
        out_specs=[pl.BlockSpec((S, HEAD_PAD), lambda h, t, kt, qt: (0, h)), krow, krow],
        scratch_shapes=[pltpu.VMEM((MLA_QK, T), F32), pltpu.VMEM((MLA_V, T), F32)])
    return pl.pallas_call(
        body, name=name, grid_spec=grid_spec,
        out_shape=[jax.ShapeDtypeStruct((S, heads * HEAD_PAD), F32)] * 3,
        compiler_params=_cparams("arbitrary", "arbitrary"),
    )(k_tab, q_tab, q_arr, k_arr, v_arr, do_arr, lse)


SWA_R = SWA_HEADS // SWA_KV_HEADS
SWA_SCALE = SWA_HD ** -0.5
SWA_Q0, SWA_K0, SWA_V0 = 0, 12, 14


def _swa_specs(tq):
    nsb = tq // WINDOW
    return [
        pl.BlockSpec((tq, SWA_R * HEAD_PAD), lambda g, i: (i, g)),
        pl.BlockSpec((tq, HEAD_PAD), lambda g, i: (i, SWA_K0 + g)),
        pl.BlockSpec((WINDOW, HEAD_PAD), lambda g, i: (jnp.maximum(nsb * i - 1, 0), SWA_K0 + g)),
        pl.BlockSpec((tq, HEAD_PAD), lambda g, i: (i, SWA_V0 + g)),
        pl.BlockSpec((WINDOW, HEAD_PAD), lambda g, i: (jnp.maximum(nsb * i - 1, 0), SWA_V0 + g)),
        pl.BlockSpec((SWA_R, WINDOW, 2 * WINDOW), lambda g, i: (g, 0, 0)),
        pl.BlockSpec((SWA_R, 8, 128), lambda g, i: (g, 0, 0)),
    ]


def _swa_block(i, sb, q_ref, kc_ref, kp_ref, vc_ref, vp_ref, bias, sink):
    rows = slice(sb * WINDOW, (sb + 1) * WINDOW)
    qs = jnp.concatenate([q_ref[rows, hh * HEAD_PAD:(hh + 1) * HEAD_PAD] for hh in range(SWA_R)], axis=0)
    if sb == 0:
        kp, vp = kp_ref[...], vp_ref[...]
    else:
        prev = slice((sb - 1) * WINDOW, sb * WINDOW)
        kp, vp = kc_ref[prev, :], vc_ref[prev, :]
    kk = jnp.concatenate([kp, kc_ref[rows, :]], axis=0)
    vv = jnp.concatenate([vp, vc_ref[rows, :]], axis=0)
    s = lax.dot_general(qs, kk, (((1,), (1,)), ((), ())), preferred_element_type=F32) * SWA_SCALE + bias
    if sb == 0:
        col = lax.broadcasted_iota(jnp.int32, (1, 2 * WINDOW), 1)
        s = s + jnp.where((col < WINDOW) & (i == 0), NEG, 0.0)
    return rows, qs, kk, vv, s


def _stack_heads(ref, rows, lead=None):
    if lead is None:
        return jnp.concatenate([ref[rows, hh * HEAD_PAD:(hh + 1) * HEAD_PAD] for hh in range(SWA_R)], axis=0)
    return jnp.concatenate([ref[hh, rows, :] for hh in range(SWA_R)], axis=0)


def _swa_fwd(name, proj_b, bias, sinks, *, tq):
    S = proj_b.shape[0]
    tq = min(tq, S)
    nsb = tq // WINDOW

    def body(q_ref, kc_ref, kp_ref, vc_ref, vp_ref, bias_ref, sink_ref, o_ref, lse_ref):
        i = pl.program_id(1)
        bias_v = bias_ref[...].reshape(SWA_R * WINDOW, 2 * WINDOW)
        sink = jnp.concatenate([jnp.zeros((WINDOW, 1), F32) + sink_ref[hh, 0:1, 0:1] for hh in range(SWA_R)], axis=0)
        nxt = _swa_block(i, 0, q_ref, kc_ref, kp_ref, vc_ref, vp_ref, bias_v, sink)
        for sb in range(nsb):
            rows, _, _, vv, s = nxt
            if sb + 1 < nsb:
                nxt = _swa_block(i, sb + 1, q_ref, kc_ref, kp_ref, vc_ref, vp_ref, bias_v, sink)
            m = jnp.maximum(jnp.max(s, axis=1, keepdims=True), sink)
            p = jnp.exp(s - m)
            l = jnp.sum(p, axis=1, keepdims=True) + jnp.exp(sink - m)
            o = jnp.dot(p.astype(BF16), vv, preferred_element_type=F32) / l
            lse_v = m + jnp.log(l)
            for hh in range(SWA_R):
                o_ref[rows, hh * HEAD_PAD:(hh + 1) * HEAD_PAD] = o[hh * WINDOW:(hh + 1) * WINDOW].astype(o_ref.dtype)
                lse_ref[hh, rows, :] = lse_v[hh * WINDOW:(hh + 1) * WINDOW]

    return pl.pallas_call(
        body, name=name, grid=(SWA_KV_HEADS, S // tq),
        in_specs=_swa_specs(tq),
        out_specs=[pl.BlockSpec((tq, SWA_R * HEAD_PAD), lambda g, i: (i, g)),
                   pl.BlockSpec((SWA_R, tq, 1), lambda g, i: (g, i, 0))],
        out_shape=[jax.ShapeDtypeStruct((S, SWA_HEADS * HEAD_PAD), BF16),
                   jax.ShapeDtypeStruct((SWA_HEADS, S, 1), F32)],
        compiler_params=_cparams("parallel", "parallel"),
    )(proj_b, proj_b, proj_b, proj_b, proj_b, bias, sinks)


def _swa_bwd(name, proj_b, bias, sinks, o, do, lse, *, tq):
    S = proj_b.shape[0]
    tq = min(tq, S)
    nsb = tq // WINDOW
    nq = S // tq

    def body(q_ref, kc_ref, kp_ref, vc_ref, vp_ref, bias_ref, sink_ref, o_ref, do_ref, lse_ref,
             dq_ref, dk_ref, dv_ref, dke_ref, dve_ref, dbias_ref, dsink_ref):
        i = pl.program_id(1)

        @pl.when(i == 0)
        def _():
            dbias_ref[...] = jnp.zeros(dbias_ref.shape, F32)
            dsink_ref[...] = jnp.zeros(dsink_ref.shape, F32)

        bias_v = bias_ref[...].reshape(SWA_R * WINDOW, 2 * WINDOW)
        sink = jnp.concatenate([jnp.zeros((WINDOW, 1), F32) + sink_ref[hh, 0:1, 0:1] for hh in range(SWA_R)], axis=0)
        dk_own, dv_own, dk_prev, dv_prev = [], [], [], []
        dbias_acc = jnp.zeros((SWA_R * WINDOW, 2 * WINDOW), F32)
        def block(sb):
            rows, qs, kk, vv, s = _swa_block(i, sb, q_ref, kc_ref, kp_ref, vc_ref, vp_ref, bias_v, sink)
            do_s = _stack_heads(do_ref, rows)
            dp = lax.dot_general(do_s, vv, (((1,), (1,)), ((), ())), preferred_element_type=F32)
            return rows, qs, kk, do_s, s, dp

        nxt = block(0)
        for sb in range(nsb):
            rows, qs, kk, do_s, s, dp = nxt
            if sb + 1 < nsb:
                nxt = block(sb + 1)
            lse_v = _stack_heads(lse_ref, rows, lead=True)
            delta = jnp.sum(do_s.astype(F32) * _stack_heads(o_ref, rows).astype(F32), axis=1, keepdims=True)
            p = jnp.exp(s - lse_v)
            dsp = p * (dp - delta)
            dbias_acc = dbias_acc + dsp
            ds = (dsp * SWA_SCALE).astype(BF16)
            dq = jnp.dot(ds, kk, preferred_element_type=F32)
            dkk = jnp.dot(qs.T, ds, preferred_element_type=F32)
            dvv = jnp.dot(do_s.T, p.astype(BF16), preferred_element_type=F32)
            dk_prev.append(dkk[:, :WINDOW].T)
            dk_own.append(dkk[:, WINDOW:].T)
            dv_prev.append(dvv[:, :WINDOW].T)
            dv_own.append(dvv[:, WINDOW:].T)
            psink = jnp.exp(sink - lse_v) * delta
            for hh in range(SWA_R):
                hrows = slice(hh * WINDOW, (hh + 1) * WINDOW)
                dq_ref[rows, hh * HEAD_PAD:(hh + 1) * HEAD_PAD] = dq[hrows].astype(dq_ref.dtype)
                dsink_ref[hh] += jnp.zeros((8, 128), F32) - jnp.sum(psink[hrows])
        dbias_ref[...] += dbias_acc.reshape(SWA_R, WINDOW, 2 * WINDOW)
        for sb in range(nsb):
            rows = slice(sb * WINDOW, (sb + 1) * WINDOW)
            if sb + 1 < nsb:
                dk_ref[rows, :] = dk_own[sb] + dk_prev[sb + 1]
                dv_ref[rows, :] = dv_own[sb] + dv_prev[sb + 1]
            else:
                dk_ref[rows, :] = dk_own[sb]
                dv_ref[rows, :] = dv_own[sb]
        dke_ref[...] = dk_prev[0]
        dve_ref[...] = dv_prev[0]

    in_specs = _swa_specs(tq) + [
        pl.BlockSpec((tq, SWA_R * HEAD_PAD), lambda g, i: (i, g)),
        pl.BlockSpec((tq, SWA_R * HEAD_PAD), lambda g, i: (i, g)),
        pl.BlockSpec((SWA_R, tq, 1), lambda g, i: (g, i, 0)),
    ]
    kv_blk = pl.BlockSpec((tq, HEAD_PAD), lambda g, i: (i, g))
    edge_blk = pl.BlockSpec((WINDOW, HEAD_PAD), lambda g, i: (i, g))
    return pl.pallas_call(
        body, name=name, grid=(SWA_KV_HEADS, nq),
        in_specs=in_specs,
        out_specs=[pl.BlockSpec((tq, SWA_R * HEAD_PAD), lambda g, i: (i, g)), kv_blk, kv_blk, edge_blk, edge_blk,
                   pl.BlockSpec((SWA_R, WINDOW, 2 * WINDOW), lambda g, i: (g, 0, 0)),
                   pl.BlockSpec((SWA_R, 8, 128), lambda g, i: (g, 0, 0))],
        out_shape=[jax.ShapeDtypeStruct((S, SWA_HEADS * HEAD_PAD), BF16),
                   jax.ShapeDtypeStruct((S, SWA_KV_HEADS * HEAD_PAD), F32),
                   jax.ShapeDtypeStruct((S, SWA_KV_HEADS * HEAD_PAD), F32),
                   jax.ShapeDtypeStruct((nq * WINDOW, SWA_KV_HEADS * HEAD_PAD), F32),
                   jax.ShapeDtypeStruct((nq * WINDOW, SWA_KV_HEADS * HEAD_PAD), F32),
                   jax.ShapeDtypeStruct((SWA_HEADS, WINDOW, 2 * WINDOW), F32),
                   jax.ShapeDtypeStruct((SWA_HEADS, 8, 128), F32)],
        compiler_params=_cparams("arbitrary", "arbitrary"),
    )(proj_b, proj_b, proj_b, proj_b, proj_b, bias, sinks, o, do, lse)


def _dproj_b(name, dq_swa, dq_mem, dk, dv, dk_edge, dv_edge, *, tq):
    S = dq_swa.shape[0]
    tq = min(tq, S)
    nq = S // tq

    def body(dqs_ref, dqm_ref, dk_ref, dv_ref, dke_ref, dve_ref, o_ref):
        i = pl.program_id(0)
        o_ref[:, 0:1024] = dqs_ref[...]
        o_ref[:, 1024:1536] = dqm_ref[...].astype(o_ref.dtype)
        o_ref[:, 1536:1792] = dk_ref[...].astype(o_ref.dtype)
        o_ref[:, 1792:2048] = dv_ref[...].astype(o_ref.dtype)

        @pl.when(i < nq - 1)
        def _():
            last = slice(tq - WINDOW, tq)
            o_ref[last, 1536:1792] = (dk_ref[last, :] + dke_ref[...]).astype(o_ref.dtype)
            o_ref[last, 1792:2048] = (dv_ref[last, :] + dve_ref[...]).astype(o_ref.dtype)

    edge = pl.BlockSpec((WINDOW, SWA_KV_HEADS * HEAD_PAD), lambda i: (jnp.minimum(i + 1, nq - 1), 0))
    return pl.pallas_call(
        body, name=name, grid=(nq,),
        in_specs=[pl.BlockSpec((tq, 1024), lambda i: (i, 0)), pl.BlockSpec((tq, 512), lambda i: (i, 0)),
                  pl.BlockSpec((tq, 256), lambda i: (i, 0)), pl.BlockSpec((tq, 256), lambda i: (i, 0)), edge, edge],
        out_specs=pl.BlockSpec((tq, 2048), lambda i: (i, 0)),
        out_shape=jax.ShapeDtypeStruct((S, 2048), BF16),
        compiler_params=_cparams("parallel"),
    )(dq_swa, dq_mem, dk, dv, dk_edge, dv_edge)


def _exchange(name, send, *, per_peer):
    shape = send.shape[1:] if per_peer else send.shape

    def body(send_ref, recv_ref, send_sems, recv_sems, local_sem):
        x, y, c = lax.axis_index("x"), lax.axis_index("y"), lax.axis_index("c")
        me = 4 * x + 2 * y + c
        own = pltpu.make_async_copy(send_ref.at[me] if per_peer else send_ref, recv_ref.at[me], local_sem)
        own.start()
        copies = []
        for k in range(1, N_DEV):
            px = 1 - x if (k >> 2) & 1 else x
            py = 1 - y if (k >> 1) & 1 else y
            pc = 1 - c if k & 1 else c
            peer = 4 * px + 2 * py + pc
            out = pltpu.make_async_remote_copy(
                src_ref=send_ref.at[peer] if per_peer else send_ref, dst_ref=recv_ref.at[me],
                send_sem=send_sems.at[k - 1], recv_sem=recv_sems.at[k - 1],
                device_id=(px, py, pc), device_id_type=pl.DeviceIdType.MESH)
            out.start()
            back = pltpu.make_async_remote_copy(
                src_ref=send_ref.at[me] if per_peer else send_ref, dst_ref=recv_ref.at[peer],
                send_sem=send_sems.at[k - 1], recv_sem=recv_sems.at[k - 1],
                device_id=(px, py, pc), device_id_type=pl.DeviceIdType.MESH)
            copies.append((out, back))
        for out, back in copies:
            out.wait_send()
            back.wait_recv()
        own.wait()

    return pl.pallas_call(
        body, name=name,
        in_specs=[pl.BlockSpec(memory_space=pl.ANY)],
        out_specs=pl.BlockSpec(memory_space=pl.ANY),
        out_shape=jax.ShapeDtypeStruct((N_DEV,) + tuple(shape), send.dtype),
        scratch_shapes=[pltpu.SemaphoreType.DMA((N_DEV - 1,)), pltpu.SemaphoreType.DMA((N_DEV - 1,)),
                        pltpu.SemaphoreType.DMA(())],
    )(send)


def _gather_forwarded(name, block):
    def body(x_ref, out_ref, send_sems, recv_sems, local_sem):
        x, y, c = lax.axis_index("x"), lax.axis_index("y"), lax.axis_index("c")
        me, sibling = (x, y, c), (x, y, 1 - c)
        chips = [(1 - x, y), (x, 1 - y), (1 - x, 1 - y)]

        def slot(px, py, pc):
            return out_ref.at[4 * px + 2 * py + pc]

        def copy(k, blk, to, src=None):
            return pltpu.make_async_remote_copy(
                src_ref=slot(*blk) if src is None else src, dst_ref=slot(*blk),
                send_sem=send_sems.at[k], recv_sem=recv_sems.at[k],
                device_id=to, device_id_type=pl.DeviceIdType.MESH)

        mine = pltpu.make_async_copy(x_ref, slot(*me), local_sem)
        mine.start()
        first = [copy(0, me, sibling, src=x_ref)]
        first += [copy(1 + j, me, (*chip, c), src=x_ref) for j, chip in enumerate(chips)]
        for cp in first:
            cp.start()
        passed = [copy(4 + j, (*chip, c), sibling) for j, chip in enumerate(chips)]
        for j, chip in enumerate(chips):
            copy(1 + j, (*chip, c), me).wait_recv()
            passed[j].start()
        copy(0, sibling, me).wait_recv()
        for j, chip in enumerate(chips):
            copy(4 + j, (*chip, 1 - c), me).wait_recv()
        for cp in first + passed:
            cp.wait_send()
        mine.wait()

    return pl.pallas_call(
        body, name=name,
        in_specs=[pl.BlockSpec(memory_space=pl.ANY)],
        out_specs=pl.BlockSpec(memory_space=pl.ANY),
        out_shape=jax.ShapeDtypeStruct((N_DEV,) + tuple(block.shape), block.dtype),
        scratch_shapes=[pltpu.SemaphoreType.DMA((N_DEV - 1,)), pltpu.SemaphoreType.DMA((N_DEV - 1,)),
                        pltpu.SemaphoreType.DMA(())],
    )(block)


def _sibling_swap(name, block):
    def body(x_ref, out_ref, send_sem, recv_sem):
        x, y, c = lax.axis_index("x"), lax.axis_index("y"), lax.axis_index("c")
        cp = pltpu.make_async_remote_copy(src_ref=x_ref, dst_ref=out_ref, send_sem=send_sem, recv_sem=recv_sem,
                                          device_id=(x, y, 1 - c), device_id_type=pl.DeviceIdType.MESH)
        cp.start()
        cp.wait()

    return pl.pallas_call(
        body, name=name,
        in_specs=[pl.BlockSpec(memory_space=pl.ANY)],
        out_specs=pl.BlockSpec(memory_space=pl.ANY),
        out_shape=jax.ShapeDtypeStruct(block.shape, block.dtype),
        scratch_shapes=[pltpu.SemaphoreType.DMA(()), pltpu.SemaphoreType.DMA(())],
    )(block)


def _chip_exchange(name, send):
    def body(send_ref, recv_ref, send_sems, recv_sems, local_sem):
        x, y, c = lax.axis_index("x"), lax.axis_index("y"), lax.axis_index("c")
        me = 2 * x + y
        own = pltpu.make_async_copy(send_ref.at[me], recv_ref.at[me], local_sem)
        own.start()
        copies = []
        for k in range(1, 4):
            px = 1 - x if (k >> 1) & 1 else x
            py = 1 - y if k & 1 else y
            peer = 2 * px + py
            out = pltpu.make_async_remote_copy(
                src_ref=send_ref.at[peer], dst_ref=recv_ref.at[me],
                send_sem=send_sems.at[k - 1], recv_sem=recv_sems.at[k - 1],
                device_id=(px, py, c), device_id_type=pl.DeviceIdType.MESH)
            out.start()
            back = pltpu.make_async_remote_copy(
                src_ref=send_ref.at[me], dst_ref=recv_ref.at[peer],
                send_sem=send_sems.at[k - 1], recv_sem=recv_sems.at[k - 1],
                device_id=(px, py, c), device_id_type=pl.DeviceIdType.MESH)
            copies.append((out, back))
        for out, back in copies:
            out.wait_send()
            back.wait_recv()
        own.wait()

    return pl.pallas_call(
        body, name=name,
        in_specs=[pl.BlockSpec(memory_space=pl.ANY)],
        out_specs=pl.BlockSpec(memory_space=pl.ANY),
        out_shape=jax.ShapeDtypeStruct(send.shape, send.dtype),
        scratch_shapes=[pltpu.SemaphoreType.DMA((3,)), pltpu.SemaphoreType.DMA((3,)), pltpu.SemaphoreType.DMA(())],
    )(send)


def _reduce_scatter(parts):
    lanes = 128
    c = lax.axis_index("c")

    def core_half(core):
        return jnp.concatenate(
            [lax.dynamic_index_in_dim(p.reshape(4, 2, p.shape[1], lanes), core, axis=1, keepdims=False)
             for p in parts], axis=1)

    mine = core_half(c)
    rows = mine.shape[1]
    mine = mine.reshape(4 * rows, lanes)
    theirs = core_half(1 - c).reshape(4 * rows, lanes)
    from_sibling = _sibling_swap("grads_to_sibling", theirs)
    tm = max(t for t in range(16, ADAM_TM + 1, 16) if rows % t == 0)
    chip_sum = _rowwise("grads_chip_sum", lambda a, b: (a.astype(F32) + b.astype(F32),),
                        [_rows(mine), _rows(from_sibling)], [((4 * rows, lanes), BF16, "rows")],
                        rows=4 * rows, tm=tm)[0]
    return _chip_exchange("scatter_grads", chip_sum.reshape(4, rows, lanes))


def _adam(name, recv, w, m, v, *, tm=None):
    R = w.shape[0]
    n_parts = recv.shape[0]
    tm = max(t for t in range(8, min(tm or ADAM_TM, R) + 1, 8) if R % t == 0)
    c1 = 1.0 / (1.0 - ADAM_B1 ** ADAM_STEP)
    c2 = 1.0 / (1.0 - ADAM_B2 ** ADAM_STEP)

    def body(r_ref, w_ref, m_ref, v_ref, g_ref, d_ref, nm_ref, nv_ref):
        g = r_ref[0].astype(F32)
        for j in range(1, n_parts):
            g = g + r_ref[j].astype(F32)
        wv = w_ref[...]
        nm = ADAM_B1 * m_ref[...] + (1.0 - ADAM_B1) * g
        nv = ADAM_B2 * v_ref[...] + (1.0 - ADAM_B2) * (g * g)
        m_hat = nm * c1
        v_hat = nv * c2
        g_ref[...] = g
        d_ref[...] = -ADAM_LR * (m_hat / (jnp.sqrt(v_hat) + ADAM_EPS) + ADAM_WD * wv)
        nm_ref[...] = nm
        nv_ref[...] = nv

    row = pl.BlockSpec((tm, 128), lambda i: (i, 0))
    return pl.pallas_call(
        body, name=name, grid=(R // tm,),
        in_specs=[pl.BlockSpec((n_parts, tm, 128), lambda i: (0, i, 0)), row, row, row],
        out_specs=[row, row, row, row],
        out_shape=[jax.ShapeDtypeStruct((R, 128), F32)] * 4,
        compiler_params=_cparams("parallel"),
    )(recv, w, m, v)


def _pack_rows(arrs):
    return jnp.concatenate([a.reshape(-1, 128) for a in arrs], axis=0)


def _unstack(g, shape, axis):
    t = jnp.moveaxis(g, 0, axis)
    return t.reshape(shape)


def _restack(full, axis):
    s = full.shape
    t = full.reshape(s[:axis] + (N_DEV, s[axis] // N_DEV) + s[axis + 1:])
    return jnp.moveaxis(t, axis, 0)


def _pad_heads(w, heads, hd, axis):
    s = w.shape
    t = w.reshape(s[:axis] + (heads, hd) + s[axis + 1:])
    pad = [(0, 0)] * t.ndim
    pad[axis + 1] = (0, HEAD_PAD - hd)
    t = jnp.pad(t, pad)
    return t.reshape(s[:axis] + (heads * HEAD_PAD,) + s[axis + 1:])


def _unpad_heads(w, heads, hd, axis):
    s = w.shape
    t = w.reshape(s[:axis] + (heads, HEAD_PAD) + s[axis + 1:])
    t = lax.slice_in_dim(t, 0, hd, axis=axis + 1)
    return t.reshape(s[:axis] + (heads * hd,) + s[axis + 1:])


def _layer_weights(full, l):
    w_in = full["w_in"][l]
    cq, kva, qs, ks, vs, qm, gates = (w_in[:, 0:256], w_in[:, 256:416], w_in[:, 416:928], w_in[:, 928:1056],
                                       w_in[:, 1056:1184], w_in[:, 1184:1696], w_in[:, 1696:4768])
    wa = jnp.concatenate([gates, cq, jnp.pad(kva, ((0, 0), (0, 96)))], axis=1)
    wb = jnp.concatenate([_pad_heads(qs, SWA_HEADS, SWA_HD, 1), qm, _pad_heads(ks, SWA_KV_HEADS, SWA_HD, 1),
                          _pad_heads(vs, SWA_KV_HEADS, SWA_HD, 1)], axis=1)
    wuq = _pad_heads(full["w_uq"][l], MLA_HEADS, MLA_NOPE + MLA_ROPE, 1)
    ukv = full["w_ukv"][l].reshape(MLA_KV_LORA, MLA_HEADS, MLA_NOPE + MLA_V)
    wuk = _pad_heads(ukv[:, :, :MLA_NOPE].reshape(MLA_KV_LORA, -1), MLA_HEADS, MLA_NOPE, 1)
    wuv = _pad_heads(ukv[:, :, MLA_NOPE:].reshape(MLA_KV_LORA, -1), MLA_HEADS, MLA_V, 1)
    wo_mla = _pad_heads(full["w_o_mla"][l], MLA_HEADS, MLA_V, 0)
    wo_swa = _pad_heads(full["w_o_swa"][l], SWA_HEADS, SWA_HD, 0)
    wo_mem = full["w_o_mem"][l]
    w = dict(wag=wa[:, :3072], wat=wa[:, 3072:], wb=wb, wuq=wuq, wuk=wuk, wuv=wuv, wo_mla=wo_mla, wo_swa=wo_swa,
             wo_mem=wo_mem, wmem=full["w_mem_kv"][l], wout=full["w_out"][l], wup=full["w_up"][l],
             wdown=full["w_down"][l])
    w.update({k + "_t": v.T for k, v in w.items()})
    return w


def _layer_weight_grads(g):
    dwa_g, dwa_t, dwb = g["wag"], g["wat"], g["wb"]
    d_in = jnp.concatenate([
        dwa_t[:, 0:256], dwa_t[:, 256:416],
        _unpad_heads(dwb[:, 0:1024], SWA_HEADS, SWA_HD, 1),
        _unpad_heads(dwb[:, 1536:1792], SWA_KV_HEADS, SWA_HD, 1),
        _unpad_heads(dwb[:, 1792:2048], SWA_KV_HEADS, SWA_HD, 1),
        dwb[:, 1024:1536], dwa_g], axis=1)
    duk = _unpad_heads(g["wuk"], MLA_HEADS, MLA_NOPE, 1).reshape(MLA_KV_LORA, MLA_HEADS, MLA_NOPE)
    duv = _unpad_heads(g["wuv"], MLA_HEADS, MLA_V, 1).reshape(MLA_KV_LORA, MLA_HEADS, MLA_V)
    return dict(
        w_in=d_in,
        w_uq=_unpad_heads(g["wuq"], MLA_HEADS, MLA_NOPE + MLA_ROPE, 1),
        w_ukv=jnp.concatenate([duk, duv], axis=2).reshape(MLA_KV_LORA, -1),
        w_mem_kv=g["wmem"],
        w_o_mla=_unpad_heads(g["wo_mla"], MLA_HEADS, MLA_V, 0),
        w_o_swa=_unpad_heads(g["wo_swa"], SWA_HEADS, SWA_HD, 0),
        w_o_mem=g["wo_mem"], w_out=g["wout"], w_up=g["wup"], w_down=g["wdown"])


def _rope_tables(S):
    pos = jnp.arange(S, dtype=F32)
    inv = 1.0 / (ROPE_THETA ** (jnp.arange(0, MLA_ROPE, 2, dtype=F32) / MLA_ROPE))
    ang = pos[:, None] * inv[None, :]
    cos, sin = jnp.cos(ang), jnp.sin(ang)
    z16 = jnp.zeros((S, 16), F32)
    z32 = jnp.zeros((S, 32), F32)
    c = jnp.concatenate([jnp.ones((S, 64), F32), cos, cos, z32], axis=1)
    ck = jnp.concatenate([jnp.zeros((S, 64), F32), cos, cos, z32], axis=1)
    s1 = jnp.concatenate([jnp.zeros((S, 80), F32), sin, z32], axis=1)
    s2 = jnp.concatenate([jnp.zeros((S, 64), F32), -sin, z16, z32], axis=1)
    return c, ck, s1, s2


def _t5_bucket(dist):
    n = jnp.maximum(dist, 0)
    max_exact = REL_BUCKETS // 2
    nf = jnp.maximum(n, 1).astype(F32)
    large = max_exact + (jnp.log(nf / max_exact) / math.log(REL_MAX_DIST / max_exact)
                         * (REL_BUCKETS - max_exact)).astype(jnp.int32)
    large = jnp.minimum(large, REL_BUCKETS - 1)
    return jnp.where(n < max_exact, n, large)


def _bias_onehot():
    qi = jnp.arange(WINDOW)[:, None]
    kj = jnp.arange(2 * WINDOW)[None, :]
    dist = qi + WINDOW - kj
    valid = (dist >= 0) & (dist < WINDOW)
    bucket = _t5_bucket(dist)
    onehot = (bucket[None] == jnp.arange(REL_BUCKETS)[:, None, None]) & valid[None]
    return (onehot.reshape(REL_BUCKETS, -1).astype(F32),
            jnp.where(valid, 0.0, NEG).astype(F32).reshape(1, -1))


def _rstd(x):
    return lax.rsqrt(jnp.mean(x * x, axis=-1, keepdims=True) + EPS)


def _norm_bwd(dh, x, g):
    r = _rstd(x)
    xh = x * r
    w = dh * g
    dx = r * (w - xh * jnp.mean(w * xh, axis=-1, keepdims=True))
    return dx, jnp.sum(dh * xh, axis=0, keepdims=True)


def _tile_lanes(t, n):
    return jnp.tile(t, (1, n // t.shape[1])) if n != t.shape[1] else t


def _rope_fwd(a, c, s1, s2):
    n = a.shape[1]
    return (a * _tile_lanes(c, n) + pltpu.roll(a, 16, 1) * _tile_lanes(s1, n)
            + pltpu.roll(a, n - 16, 1) * _tile_lanes(s2, n))


def _rope_bwd(d, c, s1, s2):
    n = d.shape[1]
    return (d * _tile_lanes(c, n) + pltpu.roll(d * _tile_lanes(s1, n), n - 16, 1)
            + pltpu.roll(d * _tile_lanes(s2, n), 16, 1))


def _sigmoid(x):
    return 1.0 / (1.0 + jnp.exp(-x))


def _rmsnorm(name, x, g, dtype):
    def fn(xv, gv):
        return ((xv * _rstd(xv)) * gv,)
    return _rowwise(name, fn, [_rows(x), _full(g)], [(x.shape, dtype, "rows")], rows=x.shape[0])[0]


def _residual_norm_bwd(name, dres, dh, x, g):
    def fn(dr, dhv, xv, gv):
        dx, dg = _norm_bwd(dhv, xv, gv)
        return dr + dx, dg
    return _rowwise(name, fn, [_rows(dres), _rows(dh), _rows(x), _full(g)],
                    [(x.shape, F32, "rows"), (g.shape, F32, "acc")], rows=x.shape[0])


def _norm_bwd_epilogue(dh, dres, x, g):
    dx, dg = _norm_bwd(dh, x, g)
    return dres + dx, dg


def _add_and_norm(acc, r, g):
    xs = acc + r
    return xs, xs * _rstd(xs) * g


def _layer_fwd(l, x, h, mem, w, p, next_norm, tabs, swa_bias, S):
    c, ck, s1, s2 = tabs
    n = f"l{l}_"
    gates = _mm(n + "proj_gates", h, w["wag"], [BF16], tm=MM_TM_BF16)
    proj_a = _mm(n + "proj_tail", h, w["wat"], [F32])
    proj_b = _mm(n + "proj_b", h, w["wb"], [BF16], tm=MM_TM_BF16)

    def prep(cq, kva, qn, kvn, ckv, s1v, s2v):
        cqn = cq * _rstd(cq) * qn
        ckv_ = kva[:, :128]
        ckvn = ckv_ * _rstd(ckv_) * kvn
        pe = pltpu.roll(kva[:, 128:], 64, 1)
        return cqn, ckvn, _rope_fwd(pe, ckv, s1v, s2v)

    cqn, ckvn, kpe = _rowwise(
        n + "mla_prep", prep,
        [_rows(proj_a, 256, 0), _rows(proj_a, 256, 1), _full(p["mla_q_norm"]), _full(p["mla_kv_norm"]),
         _rows(ck), _rows(s1), _rows(s2)],
        [((S, 256), BF16, "rows"), ((S, 128), BF16, "rows"), ((S, 128), F32, "rows")], rows=S)

    q_mla = _mm(n + "q_mla", cqn, w["wuq"], [BF16],
                epi=lambda acc, cv, s1v, s2v: (_rope_fwd(acc, cv, s1v, s2v) * (MLA_SCALE * LOG2E),),
                extras=[(c, "m"), (s1, "m"), (s2, "m")])
    k_mla = _mm(n + "k_mla", ckvn, w["wuk"], [BF16],
                epi=lambda acc, kp: (acc + _tile_lanes(kp, acc.shape[1]),), extras=[(kpe, "m")])
    den = ((jnp.arange(MLA_HEADS * HEAD_PAD) % HEAD_PAD) // 2 == DEN_LANE // 2).astype(F32)[None]
    v_mla = _mm(n + "v_mla", ckvn, w["wuv"], [BF16], epi=lambda acc, dv: (acc + dv,), extras=[(den, "n")])
    o_mla, lse_mla = _causal_fwd(n + "mla_fwd", q_mla, k_mla, v_mla, heads=MLA_HEADS, tile=MLA_TILE,
                                 chunk=MLA_CHUNK_FWD)
    o_swa, lse_swa = _swa_fwd(n + "swa_fwd", proj_b, swa_bias, p["sinks"], tq=SWA_TQ)
    mn = _rmsnorm(n + "mem_norm", mem, p["mem_norm"], BF16)
    kvm = _mm(n + "kv_mem", mn, w["wmem"], [BF16])
    o_mem, lse_mem = _mem_fwd(n + "mem_fwd", proj_b, kvm, tq=MEM_TQ, chunk=MEM_CHUNK)
    t0 = _mm(n + "t_mla", o_mla, w["wo_mla"], [BF16], tm=MM_TM_BF16)
    t1 = _mm(n + "t_swa", o_swa, w["wo_swa"], [BF16], tm=MM_TM_BF16)
    t2 = _mm(n + "t_mem", o_mem, w["wo_mem"], [BF16], tm=MM_TM_BF16)

    def merge(g0, g1, g2, bg, a0, a1, a2):
        y = (_sigmoid(g0 + bg[:, 0:1024]) * a0 + _sigmoid(g1 + bg[:, 1024:2048]) * a1
             + _sigmoid(g2 + bg[:, 2048:3072]) * a2)
        return (y,)

    y = _rowwise(n + "merge", merge,
                 [_rows(gates, 1024, 0), _rows(gates, 1024, 1), _rows(gates, 1024, 2), _full(p["b_gate"]),
                  _rows(t0), _rows(t1), _rows(t2)], [((S, D_MODEL), BF16, "rows")], rows=S)[0]
    x1, h2 = _mm(n + "out_proj", y, w["wout"], [F32, BF16], epi=_add_and_norm,
                 extras=[(x, "mn"), (p["mlp_norm"], "n")], tn=D_MODEL)
    act = _mm(n + "mlp_up", h2, w["wup"], [BF16], epi=lambda acc: (jnp.square(jnp.maximum(acc, 0.0)),),
              tm=MM_TM_BF16)
    if next_norm is None:
        x2 = _mm(n + "mlp_down", act, w["wdown"], [F32], epi=lambda acc, r: (acc + r,), extras=[(x1, "mn")])
        h_next = None
    else:
        x2, h_next = _mm(n + "mlp_down", act, w["wdown"], [F32, BF16], epi=_add_and_norm,
                         extras=[(x1, "mn"), (next_norm, "n")], tn=D_MODEL)
    saved = dict(x=x, h=h, gates=gates, proj_a=proj_a, proj_b=proj_b, cqn=cqn, ckvn=ckvn, q_mla=q_mla, k_mla=k_mla, v_mla=v_mla,
                 o_mla=o_mla, lse_mla=lse_mla, o_swa=o_swa, lse_swa=lse_swa, mn=mn, kvm=kvm, o_mem=o_mem,
                 lse_mem=lse_mem, t0=t0, t1=t1, t2=t2, y=y, x1=x1, h2=h2, act=act)
    return x2, h_next, saved


def _layer_bwd(l, dx2, mem, w, p, tabs, swa_bias, sv, S):
    c, ck, s1, s2 = tabs
    n = f"l{l}_b_"
    gw = {}
    gs = {}
    du = _mm(n + "d_act", dx2, w["wdown_t"], [BF16],
             epi=lambda acc, av: (acc * (2.0 * jnp.sqrt(av.astype(F32))),), extras=[(sv["act"], "mn")], tn=1024)
    gw["wdown"] = _mm_tn(n + "g_wdown", sv["act"], dx2)
    gw["wup"] = _mm_tn(n + "g_wup", sv["h2"], du)
    dx1, gs["mlp_norm"] = _mm(n + "d_h2", du, w["wup_t"], [F32], epi=_norm_bwd_epilogue,
                              extras=[(dx2, "mn"), (sv["x1"], "mn"), (p["mlp_norm"], "n")], tn=D_MODEL, col_sums=1)
    gw["wout"] = _mm_tn(n + "g_wout", sv["y"], dx1)
    dy = _mm(n + "d_y", dx1, w["wout_t"], [F32])

    def merge_bwd(dyv, g0, g1, g2, bg, a0, a1, a2):
        outs, dgs = [], []
        for b, (gv, av) in enumerate(((g0, a0), (g1, a1), (g2, a2))):
            sg = _sigmoid(gv + bg[:, b * 1024:(b + 1) * 1024])
            outs.append(dyv * sg)
            dgs.append(dyv * av * sg * (1.0 - sg))
        dg = jnp.concatenate(dgs, axis=1)
        return outs[0], outs[1], outs[2], dg, jnp.sum(dg, axis=0, keepdims=True)

    pa = sv["proj_a"]
    gt = sv["gates"]
    dt0, dt1, dt2, dgates, gs["b_gate"] = _rowwise(
        n + "merge", merge_bwd,
        [_rows(dy), _rows(gt, 1024, 0), _rows(gt, 1024, 1), _rows(gt, 1024, 2), _full(p["b_gate"]),
         _rows(sv["t0"]), _rows(sv["t1"]), _rows(sv["t2"])],
        [((S, D_MODEL), BF16, "rows")] * 3 + [((S, 3 * D_MODEL), BF16, "rows"), ((1, 3 * D_MODEL), F32, "acc")],
        rows=S)
    gw["wo_mla"] = _mm_tn(n + "g_wo_mla", sv["o_mla"], dt0)
    gw["wo_swa"] = _mm_tn(n + "g_wo_swa", sv["o_swa"], dt1)
    gw["wo_mem"] = _mm_tn(n + "g_wo_mem", sv["o_mem"], dt2)
    do_mla = _mm(n + "d_o_mla", dt0, w["wo_mla_t"], [BF16], epi=lambda acc, ov: (_with_neg_delta(acc, ov),),
                 extras=[(sv["o_mla"], "mn")], tn=MLA_HEADS * HEAD_PAD)
    do_swa = _mm(n + "d_o_swa", dt1, w["wo_swa_t"], [BF16])
    do_mem = _mm(n + "d_o_mem", dt2, w["wo_mem_t"], [BF16])
    pb = sv["proj_b"]
    dq_mla, dk_mla, dv_mla = _causal_bwd(
        n + "mla_bwd", sv["q_mla"], sv["k_mla"], sv["v_mla"], do_mla, sv["lse_mla"], heads=MLA_HEADS,
        tile=MLA_TILE_BWD, chunk=MLA_CHUNK)
    dq_swa, dk_swa, dv_swa, dk_edge, dv_edge, dbias, dsink = _swa_bwd(
        n + "swa_bwd", pb, swa_bias, p["sinks"], sv["o_swa"], do_swa, sv["lse_swa"], tq=SWA_TQ)
    dq_mem, dkvm = _mem_bwd(n + "mem_bwd", pb, sv["kvm"], sv["o_mem"], do_mem, sv["lse_mem"], tq=MEM_TQ,
                            chunk=MEM_CHUNK)
    gs["dbias"] = dbias
    gs["sinks"] = dsink[:, 0, 0]
    gw["wmem"] = _mm_tn(n + "g_wmem", sv["mn"], dkvm)
    dmn = _mm(n + "d_mn", dkvm, w["wmem_t"], [F32])
    _, gs["mem_norm"] = _residual_norm_bwd(n + "mem_norm", dmn, dmn, mem, p["mem_norm"])
    dq_pre = _rowwise(n + "q_unrope", lambda d, cv, s1v, s2v: (_rope_bwd(d * MLA_SCALE, cv, s1v, s2v),),
                      [_rows(dq_mla), _rows(c), _rows(s1), _rows(s2)], [((S, 1024), BF16, "rows")], rows=S)[0]
    gw["wuq"] = _mm_tn(n + "g_wuq", sv["cqn"], dq_pre)
    gw["wuk"] = _mm_tn(n + "g_wuk", sv["ckvn"], dk_mla)
    gw["wuv"] = _mm_tn(n + "g_wuv", sv["ckvn"], dv_mla)
    dcqn = _mm(n + "d_cqn", dq_pre, w["wuq_t"], [F32])
    dckvn = _mm(n + "d_ckvn_k", dk_mla, w["wuk_t"], [F32])
    dckvn = _mm(n + "d_ckvn_v", dv_mla, w["wuv_t"], [F32], epi=lambda acc, r: (acc + r,), extras=[(dckvn, "mn")])

    def mla_norm_bwd(dcq_n, dckv_n, dk, cq, kva, qn, kvn, ckv, s1v, s2v):
        dcq, dqn = _norm_bwd(dcq_n, cq, qn)
        dckv, dkvn = _norm_bwd(dckv_n, kva[:, :128], kvn)
        dkpe = dk[:, 0:128]
        for hh in range(1, MLA_HEADS):
            dkpe = dkpe + dk[:, hh * 128:(hh + 1) * 128]
        dpe = pltpu.roll(_rope_bwd(dkpe, ckv, s1v, s2v), 64, 1)
        return jnp.concatenate([dcq, dckv, dpe], axis=1), dqn, dkvn

    dtail, gs["mla_q_norm"], gs["mla_kv_norm"] = _rowwise(
        n + "mla_norm", mla_norm_bwd,
        [_rows(dcqn), _rows(dckvn), _rows(dk_mla), _rows(pa, 256, 0), _rows(pa, 256, 1),
         _full(p["mla_q_norm"]), _full(p["mla_kv_norm"]), _rows(ck), _rows(s1), _rows(s2)],
        [((S, 512), BF16, "rows"), ((1, 256), F32, "acc"), ((1, 128), F32, "acc")], rows=S)

    dproj_b = _dproj_b(n + "dproj_b", dq_swa, dq_mem, dk_swa, dv_swa, dk_edge, dv_edge, tq=SWA_TQ)
    h = sv["h"]
    gw["wag"] = _mm_tn(n + "g_wa_gates", h, dgates)
    gw["wat"] = _mm_tn(n + "g_wa_tail", h, dtail)
    gw["wb"] = _mm_tn(n + "g_wb", h, dproj_b)
    dh = _mm(n + "d_h_gates", dgates, w["wag_t"], [F32])
    dh = _mm(n + "d_h_tail", dtail, w["wat_t"], [F32], epi=lambda acc, r: (acc + r,), extras=[(dh, "mn")])
    dx, gs["attn_norm"] = _mm(n + "d_h_b", dproj_b, w["wb_t"], [F32],
                              epi=lambda acc, prev, dr, xv, gv: _norm_bwd_epilogue(acc + prev, dr, xv, gv),
                              extras=[(dh, "mn"), (dx1, "mn"), (sv["x"], "mn"), (p["attn_norm"], "n")],
                              tn=D_MODEL, tm=MM_TM // 2, col_sums=1)
    return dx, gw, gs


def _local_step(x, mem, loss_target, full, small):
    S = x.shape[0]
    tabs = _rope_tables(S)
    onehot, band = _bias_onehot()
    hi = lax.Precision.HIGHEST
    swa_bias = _mm("swa_bias", small["rel_bias"].T, onehot, [F32], epi=lambda acc, mk: (acc + mk,),
                   extras=[(band, "n")], cast=None, precision=hi, tn=8192).reshape(SWA_HEADS, WINDOW, 2 * WINDOW)
    ws, ps = [], []
    for l in range(DEPTH):
        ws.append(_layer_weights(full, l))
        ps.append(dict(
            attn_norm=small["attn_norm"][l][None], mem_norm=small["mem_norm"][l][None],
            b_gate=small["b_gate"][l][None], mla_q_norm=small["mla_q_norm"][l][None],
            mla_kv_norm=small["mla_kv_norm"][l][None], mlp_norm=small["mlp_norm"][l][None],
            sinks=jnp.broadcast_to(small["attn_sinks"][l][:, None, None], (SWA_HEADS, 8, 128))))
    saved = []
    xc = x
    hc = _rmsnorm("l0_attn_norm", x, ps[0]["attn_norm"], BF16)
    for l in range(DEPTH):
        next_norm = ps[l + 1]["attn_norm"] if l + 1 < DEPTH else None
        xc, hc, sv = _layer_fwd(l, xc, hc, mem, ws[l], ps[l], next_norm, tabs, swa_bias, S)
        saved.append(sv)

    fn_g = small["final_norm"][None]

    def loss_fn(xv, gv, tv):
        r = _rstd(xv)
        xh = xv * r
        err = xh * gv - tv
        dyv = err * (1.0 / D_MODEL)
        wv = dyv * gv
        dx = r * (wv - xh * jnp.mean(wv * xh, axis=-1, keepdims=True))
        part = 0.5 * jnp.sum(err * err) * (1.0 / D_MODEL)
        return dx, jnp.sum(dyv * xh, axis=0, keepdims=True), jnp.zeros((8, 128), F32) + part

    dx, g_final, loss_acc = _rowwise(
        "loss", loss_fn, [_rows(xc), _full(fn_g), _rows(loss_target)],
        [((S, D_MODEL), F32, "rows"), ((1, D_MODEL), F32, "acc"), ((8, 128), F32, "acc")], rows=S)

    gws, gss = [None] * DEPTH, [None] * DEPTH
    for l in reversed(range(DEPTH)):
        dx, gw, gs = _layer_bwd(l, dx, mem, ws[l], ps[l], tabs, swa_bias, saved[l], S)
        gws[l] = _layer_weight_grads(gw)
        gss[l] = gs

    dbias = (gss[0]["dbias"] + gss[1]["dbias"]).reshape(SWA_HEADS, -1)
    g_rel = _mm("g_rel_bias", dbias, onehot.T, [F32], cast=None, precision=hi, tk=8192).T
    wgrads = {k: jnp.stack([gws[l][k] for l in range(DEPTH)]) for k in gws[0]}
    sgrads = dict(
        rel_bias=g_rel,
        final_norm=g_final[0],
        attn_sinks=jnp.stack([gss[l]["sinks"] for l in range(DEPTH)]),
        **{k: jnp.concatenate([gss[l][k] for l in range(DEPTH)], axis=0)
           for k in ("attn_norm", "mem_norm", "b_gate", "mla_q_norm", "mla_kv_norm", "mlp_norm")})
    return loss_acc[0, 0], dx, wgrads, sgrads


def _pack_small(vals, loss):
    rows = []
    for name, shape in SMALL:
        flat = vals[name].astype(F32).reshape(-1)
        pad = (-flat.shape[0]) % 1024
        rows.append(jnp.pad(flat, (0, pad)).reshape(-1, 128))
    rows.append(jnp.zeros((8, 128), F32) + loss)
    return jnp.concatenate(rows, axis=0)


def _unpack_small(packed):
    out, r = {}, 0
    for name, shape in SMALL:
        size = math.prod(shape)
        nrows = 8 * -(-size // 1024)
        out[name] = packed[r:r + nrows].reshape(-1)[:size].reshape(shape)
        r += nrows
    return out, packed[r, 0]


def kernel(x, mem, rel_bias, attn_norm, mem_norm, w_in, b_gate, mla_q_norm, w_uq, mla_kv_norm, w_ukv, attn_sinks, w_mem_kv, w_o_mla, w_o_swa, w_o_mem, w_out, mlp_norm, w_up, w_down, final_norm, loss_target, m_rel_bias, m_attn_norm, m_mem_norm, m_w_in, m_b_gate, m_mla_q_norm, m_w_uq, m_mla_kv_norm, m_w_ukv, m_attn_sinks, m_w_mem_kv, m_w_o_mla, m_w_o_swa, m_w_o_mem, m_w_out, m_mlp_norm, m_w_up, m_w_down, m_final_norm, v_rel_bias, v_attn_norm, v_mem_norm, v_w_in, v_b_gate, v_mla_q_norm, v_w_uq, v_mla_kv_norm, v_w_ukv, v_attn_sinks, v_w_mem_kv, v_w_o_mla, v_w_o_swa, v_w_o_mem, v_w_out, v_mlp_norm, v_w_up, v_w_down, v_final_norm):
    wv = dict(rel_bias=rel_bias, attn_norm=attn_norm, mem_norm=mem_norm, w_in=w_in, b_gate=b_gate,
              mla_q_norm=mla_q_norm, w_uq=w_uq, mla_kv_norm=mla_kv_norm, w_ukv=w_ukv, attn_sinks=attn_sinks,
              w_mem_kv=w_mem_kv, w_o_mla=w_o_mla, w_o_swa=w_o_swa, w_o_mem=w_o_mem, w_out=w_out,
              mlp_norm=mlp_norm, w_up=w_up, w_down=w_down, final_norm=final_norm)
    mv = dict(rel_bias=m_rel_bias, attn_norm=m_attn_norm, mem_norm=m_mem_norm, w_in=m_w_in, b_gate=m_b_gate,
              mla_q_norm=m_mla_q_norm, w_uq=m_w_uq, mla_kv_norm=m_mla_kv_norm, w_ukv=m_w_ukv,
              attn_sinks=m_attn_sinks, w_mem_kv=m_w_mem_kv, w_o_mla=m_w_o_mla, w_o_swa=m_w_o_swa,
              w_o_mem=m_w_o_mem, w_out=m_w_out, mlp_norm=m_mlp_norm, w_up=m_w_up, w_down=m_w_down,
              final_norm=m_final_norm)
    vv = dict(rel_bias=v_rel_bias, attn_norm=v_attn_norm, mem_norm=v_mem_norm, w_in=v_w_in, b_gate=v_b_gate,
              mla_q_norm=v_mla_q_norm, w_uq=v_w_uq, mla_kv_norm=v_mla_kv_norm, w_ukv=v_w_ukv,
              attn_sinks=v_attn_sinks, w_mem_kv=v_w_mem_kv, w_o_mla=v_w_o_mla, w_o_swa=v_w_o_swa,
              w_o_mem=v_w_o_mem, w_out=v_w_out, mlp_norm=v_mlp_norm, w_up=v_w_up, w_down=v_w_down,
              final_norm=v_final_norm)

    shard_rows = [math.prod(_shard_shape(shape, axis)) // 128 for _, shape, axis in WSPECS]
    gathered = _gather_forwarded("gather_weights", _pack_rows([wv[name].astype(BF16) for name, _, _ in WSPECS]))
    full, r = {}, 0
    for (name, shape, axis), nr in zip(WSPECS, shard_rows):
        full[name] = _unstack(gathered[:, r:r + nr].reshape((N_DEV,) + _shard_shape(shape, axis)), shape, axis)
        r += nr

    loss_part, grad_x, wgrads, sgrads = _local_step(x[0], mem[0], loss_target[0], full,
                                                    {name: wv[name] for name, _ in SMALL})

    recv = _reduce_scatter([_restack(wgrads[name], axis).astype(BF16).reshape(N_DEV, -1, 128)
                            for name, _, axis in WSPECS])
    outs = _adam("adam_sharded", recv, *[_pack_rows([d[name] for name, _, _ in WSPECS]) for d in (wv, mv, vv)])
    res = {}
    r = 0
    for (name, shape, axis), nr in zip(WSPECS, shard_rows):
        res[name] = [o[r:r + nr].reshape(_shard_shape(shape, axis)) for o in outs]
        r += nr

    small_recv = _exchange("gather_small", _pack_small(sgrads, loss_part), per_peer=False)
    zero = jnp.zeros((), F32)
    souts = _adam("adam_small", small_recv, *[_pack_small(d, zero) for d in (wv, mv, vv)])
    loss = None
    for i, o in enumerate(souts):
        vals, extra = _unpack_small(o)
        if i == 0:
            loss = extra
        for name, _ in SMALL:
            res.setdefault(name, []).append(vals[name])

    out = [loss, grad_x[None]]
    for i in range(4):
        out.extend(res[name][i] for name in WEIGHT_ORDER)
    return tuple(out)
```

```python
import math

import jax
import jax.numpy as jnp
from jax import lax
from jax.experimental import pallas as pl
from jax.experimental.pallas import tpu as pltpu

F32 = jnp.float32
BF16 = jnp.bfloat16

N_DEV = 8
D_MODEL = 1024
DEPTH = 2
MLA_HEADS = 8
MLA_Q_LORA = 256
MLA_KV_LORA = 128
MLA_NOPE = 64
MLA_ROPE = 32
MLA_V = 64
ROPE_THETA = 10000.0
SWA_HEADS = 8
SWA_KV_HEADS = 2
SWA_HD = 64
WINDOW = 128
REL_BUCKETS = 32
REL_MAX_DIST = 128
MEM_LEN = 256
MEM_HEADS = 4
MEM_HD = 128
D_FF = 4 * D_MODEL
EPS = 1e-6
HEAD_PAD = 128
ADAM_LR = 0.001
ADAM_B1 = 0.9
ADAM_B2 = 0.999
ADAM_EPS = 1e-08
ADAM_WD = 0.01
ADAM_STEP = 10

NEG = -1e30
VMEM_LIMIT = 48 * 1024 * 1024

MM_TM = 1024
MM_TN = 1024
MM_TK = 1024
TN_T1 = 1024
TN_TN = 1024
TN_TS = 2048
MM_TK_DEEP = 2048
MM_TM_BF16 = 2048
ROW_TM = 256
MLA_TILE = 4096
MLA_TILE_BWD = 4096
MLA_BWD_VMEM_LIMIT = 60 * 1024 * 1024
MLA_CHUNK = 256
MLA_CHUNK_FWD = 512
MLA_QK = MLA_NOPE + MLA_ROPE
MLA_SCALE = MLA_QK ** -0.5
LOG2E = math.log2(math.e)
DEN_LANE = MLA_V
SWA_TQ = 1024
MEM_TQ = 1024
MEM_CHUNK = 256
ADAM_TM = 1200

WSPECS = (
    ("w_in", (DEPTH, D_MODEL, 4768), 2),
    ("w_uq", (DEPTH, MLA_Q_LORA, 768), 2),
    ("w_ukv", (DEPTH, MLA_KV_LORA, 1024), 2),
    ("w_mem_kv", (DEPTH, D_MODEL, 1024), 1),
    ("w_o_mla", (DEPTH, 512, D_MODEL), 2),
    ("w_o_swa", (DEPTH, 512, D_MODEL), 2),
    ("w_o_mem", (DEPTH, 512, D_MODEL), 2),
    ("w_out", (DEPTH, D_MODEL, D_MODEL), 1),
    ("w_up", (DEPTH, D_MODEL, D_FF), 2),
    ("w_down", (DEPTH, D_FF, D_MODEL), 1),
)
SMALL = (
    ("rel_bias", (REL_BUCKETS, SWA_HEADS)),
    ("attn_norm", (DEPTH, D_MODEL)),
    ("mem_norm", (DEPTH, D_MODEL)),
    ("b_gate", (DEPTH, 3 * D_MODEL)),
    ("mla_q_norm", (DEPTH, MLA_Q_LORA)),
    ("mla_kv_norm", (DEPTH, MLA_KV_LORA)),
    ("attn_sinks", (DEPTH, SWA_HEADS)),
    ("mlp_norm", (DEPTH, D_MODEL)),
    ("final_norm", (D_MODEL,)),
)
WEIGHT_ORDER = ("rel_bias", "attn_norm", "mem_norm", "w_in", "b_gate", "mla_q_norm", "w_uq", "mla_kv_norm",
                "w_ukv", "attn_sinks", "w_mem_kv", "w_o_mla", "w_o_swa", "w_o_mem", "w_out", "mlp_norm",
                "w_up", "w_down", "final_norm")


def _cparams(*sem):
    return pltpu.CompilerParams(dimension_semantics=sem, vmem_limit_bytes=VMEM_LIMIT)


def _shard_shape(shape, axis):
    s = list(shape)
    s[axis] //= N_DEV
    return tuple(s)


def _mm(name, a, b, out_dtypes, *, epi=None, extras=(), a_fn=None, cast=BF16, precision=None,
        tm=None, tn=None, tk=None, col_sums=0):
    M, K = a.shape
    K2, N = b.shape
    assert K == K2, (name, a.shape, b.shape)
    tm = min(tm or MM_TM, M)
    tn = min(tn or MM_TN, N)
    tk = min(tk or MM_TK, K)
    assert M % tm == 0 and N % tn == 0 and K % tk == 0, (name, a.shape, b.shape, tm, tn, tk)
    assert col_sums == 0 or tn == N, (name, tn, N)
    nk = K // tk
    n_ex = len(extras)
    n_out = len(out_dtypes)

    def body(*refs):
        a_ref, b_ref = refs[0], refs[1]
        ex_refs = refs[2:2 + n_ex]
        out_refs = refs[2 + n_ex:2 + n_ex + n_out]
        av = a_ref[...]
        if a_fn is not None:
            av = a_fn(av)
        bv = b_ref[...]
        if cast is not None:
            av = av.astype(cast)
            bv = bv.astype(cast)
        part = jnp.dot(av, bv, preferred_element_type=F32, precision=precision)

        def finish(acc):
            outs = epi(acc, *[r[...] for r in ex_refs]) if epi is not None else (acc,)
            for r, o in zip(out_refs, outs[:n_out]):
                r[...] = o.astype(r.dtype)
            i = pl.program_id(0)
            for r, o in zip(refs[2 + n_ex + n_out:2 + n_ex + n_out + col_sums], outs[n_out:]):
                @pl.when(i == 0)
                def _(r=r, o=o):
                    r[...] = o

                @pl.when(i > 0)
                def _(r=r, o=o):
                    r[...] += o

        if nk == 1:
            finish(part)
        else:
            acc_ref = refs[-1]
            k = pl.program_id(2)

            @pl.when(k == 0)
            def _():
                acc_ref[...] = part

            @pl.when(k > 0)
            def _():
                acc_ref[...] += part

            @pl.when(k == nk - 1)
            def _():
                finish(acc_ref[...])

    in_specs = [pl.BlockSpec((tm, tk), lambda i, j, k: (i, k)),
                pl.BlockSpec((tk, tn), lambda i, j, k: (k, j))]
    for arr, kind in extras:
        if kind == "mn":
            in_specs.append(pl.BlockSpec((tm, tn), lambda i, j, k: (i, j)))
        elif kind == "m":
            in_specs.append(pl.BlockSpec((tm, arr.shape[1]), lambda i, j, k: (i, 0)))
        else:
            in_specs.append(pl.BlockSpec((1, tn), lambda i, j, k: (0, j)))
    outs = pl.pallas_call(
        body, name=name, grid=(M // tm, N // tn, nk),
        in_specs=in_specs,
        out_specs=([pl.BlockSpec((tm, tn), lambda i, j, k: (i, j)) for _ in out_dtypes]
                   + [pl.BlockSpec((1, tn), lambda i, j, k: (0, 0))] * col_sums),
        out_shape=([jax.ShapeDtypeStruct((M, N), dt) for dt in out_dtypes]
                   + [jax.ShapeDtypeStruct((1, N), F32)] * col_sums),
        scratch_shapes=[pltpu.VMEM((tm, tn), F32)] if nk > 1 else [],
        compiler_params=(_cparams("arbitrary", "arbitrary", "arbitrary") if col_sums
                         else _cparams("parallel", "parallel", "arbitrary")),
    )(a, b, *[arr for arr, _ in extras])
    return outs[0] if n_out + col_sums == 1 else outs


def _mm_tn(name, a, b, *, t1=None, tn=None, ts=None):
    S, K1 = a.shape
    S2, N = b.shape
    assert S == S2, (name, a.shape, b.shape)
    t1 = min(t1 or TN_T1, K1)
    tn = min(tn or TN_TN, N)
    ts = min(ts or TN_TS, S)
    assert K1 % t1 == 0 and N % tn == 0 and S % ts == 0, (name, a.shape, b.shape)

    def body(a_ref, b_ref, o_ref):
        s = pl.program_id(2)
        part = lax.dot_general(a_ref[...].astype(BF16), b_ref[...].astype(BF16),
                               (((0,), (0,)), ((), ())), preferred_element_type=F32)

        @pl.when(s == 0)
        def _():
            o_ref[...] = part

        @pl.when(s > 0)
        def _():
            o_ref[...] += part

    return pl.pallas_call(
        body, name=name, grid=(K1 // t1, N // tn, S // ts),
        in_specs=[pl.BlockSpec((ts, t1), lambda i, j, s: (s, i)),
                  pl.BlockSpec((ts, tn), lambda i, j, s: (s, j))],
        out_specs=pl.BlockSpec((t1, tn), lambda i, j, s: (i, j)),
        out_shape=jax.ShapeDtypeStruct((K1, N), F32),
        compiler_params=_cparams("parallel", "parallel", "arbitrary"),
    )(a, b)


def _rows(arr, width=None, blk=0):
    return (arr, ("rows", arr.shape[1] if width is None else width, blk))


def _full(arr):
    return (arr, ("full",))


def _rowwise(name, fn, ins, outs, *, rows, tm=None):
    tm = min(tm or ROW_TM, rows)
    assert rows % tm == 0, (name, rows, tm)
    n_in = len(ins)

    def body(*refs):
        i = pl.program_id(0)
        vals = fn(*[r[...] for r in refs[:n_in]])
        for (shape, dt, kind), r, v in zip(outs, refs[n_in:], vals):
            if kind == "rows":
                r[...] = v.astype(dt)
            else:
                @pl.when(i == 0)
                def _(r=r, v=v):
                    r[...] = v

                @pl.when(i > 0)
                def _(r=r, v=v):
                    r[...] += v

    in_specs = []
    for arr, spec in ins:
        if spec[0] == "rows":
            in_specs.append(pl.BlockSpec((tm, spec[1]), lambda i, b=spec[2]: (i, b)))
        else:
            in_specs.append(pl.BlockSpec(arr.shape, lambda i, n=arr.ndim: (0,) * n))
    out_specs = []
    for shape, dt, kind in outs:
        if kind == "rows":
            out_specs.append(pl.BlockSpec((tm, shape[1]), lambda i: (i, 0)))
        else:
            out_specs.append(pl.BlockSpec(shape, lambda i, n=len(shape): (0,) * n))
    res = pl.pallas_call(
        body, name=name, grid=(rows // tm,),
        in_specs=in_specs, out_specs=out_specs,
        out_shape=[jax.ShapeDtypeStruct(shape, dt) for shape, dt, _ in outs],
        compiler_params=_cparams("arbitrary"),
    )(*[arr for arr, _ in ins])
    return res


MEM_SCALE = MEM_HD ** -0.5
MEM_Q0 = 2
NT_DIMS = (((1,), (1,)), ((), ()))


def _head_lanes(h):
    return slice(h * HEAD_PAD, (h + 1) * HEAD_PAD)


def _mem_fwd(name, proj_b, kvm, *, tq, chunk):
    S = proj_b.shape[0]
    tq = min(tq, S)
    C = min(chunk, tq)
    tiles = [(h, c) for c in range(tq // C) for h in range(MEM_HEADS)]

    def body(q_ref, kv_ref, o_ref, lse_ref):
        def logits(h, c):
            return lax.dot_general(q_ref[c * C:(c + 1) * C, _head_lanes(h)], kv_ref[:, _head_lanes(h)], NT_DIMS,
                                   preferred_element_type=F32) * MEM_SCALE

        nxt = logits(*tiles[0])
        for n, (h, c) in enumerate(tiles):
            s = nxt
            if n + 1 < len(tiles):
                nxt = logits(*tiles[n + 1])
            rows = slice(c * C, (c + 1) * C)
            m = jnp.max(s, axis=1, keepdims=True)
            p = jnp.exp(s - m)
            l = jnp.sum(p, axis=1, keepdims=True)
            o = jnp.dot(p.astype(BF16), kv_ref[:, _head_lanes(MEM_HEADS + h)], preferred_element_type=F32) / l
            o_ref[rows, _head_lanes(h)] = o.astype(o_ref.dtype)
            lse_ref[h, rows, :] = m + jnp.log(l)

    return pl.pallas_call(
        body, name=name, grid=(S // tq,),
        in_specs=[pl.BlockSpec((tq, MEM_HEADS * HEAD_PAD), lambda i: (i, MEM_Q0)),
                  pl.BlockSpec(kvm.shape, lambda i: (0, 0))],
        out_specs=[pl.BlockSpec((tq, MEM_HEADS * HEAD_PAD), lambda i: (i, 0)),
                   pl.BlockSpec((MEM_HEADS, tq, 1), lambda i: (0, i, 0))],
        out_shape=[jax.ShapeDtypeStruct((S, MEM_HEADS * HEAD_PAD), BF16),
                   jax.ShapeDtypeStruct((MEM_HEADS, S, 1), F32)],
        compiler_params=_cparams("parallel"),
    )(proj_b, kvm)


def _mem_bwd(name, proj_b, kvm, o, do, lse, *, tq, chunk):
    S = proj_b.shape[0]
    tq = min(tq, S)
    C = min(chunk, tq)
    nq = S // tq
    tiles = [(h, c) for c in range(tq // C) for h in range(MEM_HEADS)]

    def body(q_ref, kv_ref, o_ref, do_ref, lse_ref, dq_ref, dkv_ref, acc_sc):
        i = pl.program_id(0)

        @pl.when(i == 0)
        def _():
            acc_sc[...] = jnp.zeros(acc_sc.shape, F32)

        def mats(h, c):
            rows = slice(c * C, (c + 1) * C)
            q = q_ref[rows, _head_lanes(h)]
            dov = do_ref[rows, _head_lanes(h)]
            s = lax.dot_general(q, kv_ref[:, _head_lanes(h)], NT_DIMS, preferred_element_type=F32) * MEM_SCALE
            dp = lax.dot_general(dov, kv_ref[:, _head_lanes(MEM_HEADS + h)], NT_DIMS, preferred_element_type=F32)
            return q, dov, s, dp

        nxt = mats(*tiles[0])
        for n, (h, c) in enumerate(tiles):
            q, dov, s, dp = nxt
            if n + 1 < len(tiles):
                nxt = mats(*tiles[n + 1])
            rows = slice(c * C, (c + 1) * C)
            p = jnp.exp(s - lse_ref[h, rows, :])
            delta = jnp.sum(dov.astype(F32) * o_ref[rows, _head_lanes(h)].astype(F32), axis=1, keepdims=True)
            ds = (p * (dp - delta) * MEM_SCALE).astype(BF16)
            dq_ref[rows, _head_lanes(h)] = jnp.dot(ds, kv_ref[:, _head_lanes(h)],
                                                   preferred_element_type=F32).astype(dq_ref.dtype)
            acc_sc[_head_lanes(h), :] += jnp.dot(q.T, ds, preferred_element_type=F32)
            acc_sc[_head_lanes(MEM_HEADS + h), :] += jnp.dot(dov.T, p.astype(BF16), preferred_element_type=F32)

        @pl.when(i == nq - 1)
        def _():
            dkv_ref[...] = acc_sc[...].T

    qblk = pl.BlockSpec((tq, MEM_HEADS * HEAD_PAD), lambda i: (i, 0))
    return pl.pallas_call(
        body, name=name, grid=(nq,),
        in_specs=[pl.BlockSpec((tq, MEM_HEADS * HEAD_PAD), lambda i: (i, MEM_Q0)),
                  pl.BlockSpec(kvm.shape, lambda i: (0, 0)), qblk, qblk,
                  pl.BlockSpec((MEM_HEADS, tq, 1), lambda i: (0, i, 0))],
        out_specs=[qblk, pl.BlockSpec(kvm.shape, lambda i: (0, 0))],
        out_shape=[jax.ShapeDtypeStruct((S, MEM_HEADS * HEAD_PAD), BF16), jax.ShapeDtypeStruct(kvm.shape, F32)],
        scratch_shapes=[pltpu.VMEM((kvm.shape[1], kvm.shape[0]), F32)],
        compiler_params=_cparams("arbitrary"),
    )(proj_b, kvm, o, do, lse)


def _causal_fwd(name, q_arr, k_arr, v_arr, *, heads, tile, chunk):
    S = q_arr.shape[0]
    T = min(tile, S)
    C = min(chunk, T)
    nt = S // T
    nc = T // C

    pairs = [(qi, kk) for qi in range(nt) for kk in range(qi + 1)]
    q_tab = jnp.asarray([p[0] for p in pairs], jnp.int32)
    k_tab = jnp.asarray([p[1] for p in pairs], jnp.int32)

    def body(qt_ref, kt_ref, q_ref, k_ref, v_ref, o_ref, lse_ref, m_sc, acc_sc):
        t = pl.program_id(1)
        qi = qt_ref[t]
        kk = kt_ref[t]

        @pl.when(kk == 0)
        def _():
            m_sc[...] = jnp.full(m_sc.shape, NEG, F32)
            acc_sc[...] = jnp.zeros(acc_sc.shape, F32)

        def logits(c, ncols, masked):
            s = lax.dot_general(q_ref[pl.ds(c * C, C), :], k_ref[0:ncols, :], (((1,), (1,)), ((), ())),
                                preferred_element_type=F32)
            if masked:
                r = c * C + lax.broadcasted_iota(jnp.int32, (C, ncols), 0)
                cidx = lax.broadcasted_iota(jnp.int32, (C, ncols), 1)
                s = jnp.where(cidx <= r, s, NEG)
            return s

        def update(c, ncols, s):
            rows = pl.ds(c * C, C)
            m_prev = m_sc[rows, :]
            m_new = jnp.maximum(m_prev, jnp.max(s, axis=1, keepdims=True))
            p = jnp.exp2(s - m_new).astype(BF16)
            acc_sc[rows, :] = jnp.exp2(m_prev - m_new) * acc_sc[rows, :] + jnp.dot(
                p, v_ref[0:ncols, :], preferred_element_type=F32)
            m_sc[rows, :] = m_new

        def tile_step(ncols_of, masked):
            s = logits(0, ncols_of(0), masked)
            for c in range(nc):
                s_next = logits(c + 1, ncols_of(c + 1), masked) if c + 1 < nc else None
                update(c, ncols_of(c), s)
                s = s_next

        @pl.when(kk < qi)
        def _():
            tile_step(lambda c: T, False)

        @pl.when(kk == qi)
        def _():
            tile_step(lambda c: (c + 1) * C, True)

        @pl.when(kk == qi)
        def _():
            acc = acc_sc[...]
            l = acc[:, DEN_LANE:DEN_LANE + 1]
            o_ref[...] = (acc / l).astype(o_ref.dtype)
            lse_ref[0] = m_sc[...] + jnp.log2(l)

    grid_spec = pltpu.PrefetchScalarGridSpec(
        num_scalar_prefetch=2, grid=(heads, len(pairs)),
        in_specs=[pl.BlockSpec((T, HEAD_PAD), lambda h, t, qt, kt: (qt[t], h)),
                  pl.BlockSpec((T, HEAD_PAD), lambda h, t, qt, kt: (kt[t], h)),
                  pl.BlockSpec((T, HEAD_PAD), lambda h, t, qt, kt: (kt[t], h))],
        out_specs=[pl.BlockSpec((T, HEAD_PAD), lambda h, t, qt, kt: (qt[t], h)),
                   pl.BlockSpec((1, T, 1), lambda h, t, qt, kt: (h, qt[t], 0))],
        scratch_shapes=[pltpu.VMEM((T, 1), F32), pltpu.VMEM((T, HEAD_PAD), F32)])
    return pl.pallas_call(
        body, name=name, grid_spec=grid_spec,
        out_shape=[jax.ShapeDtypeStruct((S, heads * HEAD_PAD), BF16),
                   jax.ShapeDtypeStruct((heads, S, 1), F32)],
        compiler_params=_cparams("parallel", "arbitrary"),
    )(q_tab, k_tab, q_arr, k_arr, v_arr)


def _with_neg_delta(do, o):
    lane = lax.broadcasted_iota(jnp.int32, (1, HEAD_PAD), 1)
    outs = []
    for h in range(do.shape[1] // HEAD_PAD):
        a = do[:, _head_lanes(h)]
        nd = -jnp.sum(a * o[:, _head_lanes(h)].astype(F32), axis=1, keepdims=True)
        hi = nd.astype(BF16).astype(F32)
        a = jnp.where(lane == DEN_LANE, hi, a)
        outs.append(jnp.where(lane == DEN_LANE + 1, nd - hi, a))
    return jnp.concatenate(outs, axis=1)


def _causal_bwd(name, q_arr, k_arr, v_arr, do_arr, lse, *, heads, tile, chunk):
    S = q_arr.shape[0]
    T = min(tile, S)
    C = min(chunk, T)
    nt = S // T
    nc = T // C

    pairs = [(kj, qq) for kj in range(nt) for qq in range(kj, nt)]
    k_tab = jnp.asarray([p[0] for p in pairs], jnp.int32)
    q_tab = jnp.asarray([p[1] for p in pairs], jnp.int32)

    def body(kt_ref, qt_ref, q_ref, k_ref, v_ref, do_ref, lse_ref, dq_ref, dk_ref, dv_ref, dk_sc, dv_sc):
        t = pl.program_id(1)
        kj = kt_ref[t]
        qq = qt_ref[t]
        qb = qq

        @pl.when(t == 0)
        def _():
            dq_ref[...] = jnp.zeros(dq_ref.shape, F32)

        @pl.when(qq == kj)
        def _():
            dk_sc[...] = jnp.zeros(dk_sc.shape, F32)
            dv_sc[...] = jnp.zeros(dv_sc.shape, F32)

        def logits(c, ncols, masked):
            rows = pl.ds(c * C, C)
            s = lax.dot_general(q_ref[rows, :], k_ref[0:ncols, :], (((1,), (1,)), ((), ())),
                                preferred_element_type=F32)
            if masked:
                r = c * C + lax.broadcasted_iota(jnp.int32, (C, ncols), 0)
                cidx = lax.broadcasted_iota(jnp.int32, (C, ncols), 1)
                s = jnp.where(cidx <= r, s, NEG)
            dp = lax.dot_general(do_ref[rows, :], v_ref[0:ncols, :], (((1,), (1,)), ((), ())),
                                 preferred_element_type=F32)
            return s, dp

        def update(c, ncols, s, dp):
            rows = pl.ds(c * C, C)
            p = jnp.exp2(s - lse_ref[0, rows, :])
            ds = (p * dp).astype(BF16)
            dv_sc[:, 0:ncols] += jnp.dot(do_ref[rows, 0:MLA_V].T, p.astype(BF16), preferred_element_type=F32)
            dk_sc[:, 0:ncols] += jnp.dot(q_ref[rows, 0:MLA_QK].T, ds, preferred_element_type=F32)
            row0 = pl.multiple_of(qb * T + c * C, C)
            dq_ref[pl.ds(row0, C), :] += jnp.dot(ds, k_ref[0:ncols, :], preferred_element_type=F32)

        def tile_step(ncols_of, masked):
            cur = logits(0, ncols_of(0), masked)
            for c in range(nc):
                nxt = logits(c + 1, ncols_of(c + 1), masked) if c + 1 < nc else None
                update(c, ncols_of(c), *cur)
                cur = nxt

        @pl.when(qq > kj)
        def _():
            tile_step(lambda c: T, False)

        @pl.when(qq == kj)
        def _():
            tile_step(lambda c: (c + 1) * C, True)

        @pl.when(qq == nt - 1)
        def _():
            dk_ref[:, 0:MLA_QK] = dk_sc[...].T * math.log(2.0)
            dk_ref[:, MLA_QK:] = jnp.zeros((T, HEAD_PAD - MLA_QK), F32)
            dv_ref[:, 0:MLA_V] = dv_sc[...].T
            dv_ref[:, MLA_V:] = jnp.zeros((T, HEAD_PAD - MLA_V), F32)

    qrow = pl.BlockSpec((T, HEAD_PAD), lambda h, t, kt, qt: (qt[t], h))
    krow = pl.BlockSpec((T, HEAD_PAD), lambda h, t, kt, qt: (kt[t], h))
    qcol = pl.BlockSpec((1, T, 1), lambda h, t, kt, qt: (h, qt[t], 0))
    grid_spec = pltpu.PrefetchScalarGridSpec(
        num_scalar_prefetch=2, grid=(heads, len(pairs)),
        in_specs=[qrow, krow, krow, qrow, qcol],
        out_specs=[pl.BlockSpec((S, HEAD_PAD), lambda h, t, kt, qt: (0, h)), krow, krow],
        scratch_shapes=[pltpu.VMEM((MLA_QK, T), F32), pltpu.VMEM((MLA_V, T), F32)])
    return pl.pallas_call(
        body, name=name, grid_spec=grid_spec,
        out_shape=[jax.ShapeDtypeStruct((S, heads * HEAD_PAD), F32)] * 3,
        compiler_params=pltpu.CompilerParams(dimension_semantics=("arbitrary", "arbitrary"),
                                             vmem_limit_bytes=MLA_BWD_VMEM_LIMIT),
    )(k_tab, q_tab, q_arr, k_arr, v_arr, do_arr, lse)


SWA_R = SWA_HEADS // SWA_KV_HEADS
SWA_SCALE = SWA_HD ** -0.5
SWA_Q0, SWA_K0, SWA_V0 = 0, 12, 14


def _swa_specs(tq):
    nsb = tq // WINDOW
    return [
        pl.BlockSpec((tq, SWA_R * HEAD_PAD), lambda g, i: (i, g)),
        pl.BlockSpec((tq, HEAD_PAD), lambda g, i: (i, SWA_K0 + g)),
        pl.BlockSpec((WINDOW, HEAD_PAD), lambda g, i: (jnp.maximum(nsb * i - 1, 0), SWA_K0 + g)),
        pl.BlockSpec((tq, HEAD_PAD), lambda g, i: (i, SWA_V0 + g)),
        pl.BlockSpec((WINDOW, HEAD_PAD), lambda g, i: (jnp.maximum(nsb * i - 1, 0), SWA_V0 + g)),
        pl.BlockSpec((SWA_R, WINDOW, 2 * WINDOW), lambda g, i: (g, 0, 0)),
        pl.BlockSpec((SWA_R, 8, 128), lambda g, i: (g, 0, 0)),
    ]


def _swa_block(i, sb, q_ref, kc_ref, kp_ref, vc_ref, vp_ref, bias, sink):
    rows = slice(sb * WINDOW, (sb + 1) * WINDOW)
    qs = jnp.concatenate([q_ref[rows, hh * HEAD_PAD:(hh + 1) * HEAD_PAD] for hh in range(SWA_R)], axis=0)
    if sb == 0:
        kp, vp = kp_ref[...], vp_ref[...]
    else:
        prev = slice((sb - 1) * WINDOW, sb * WINDOW)
        kp, vp = kc_ref[prev, :], vc_ref[prev, :]
    kk = jnp.concatenate([kp, kc_ref[rows, :]], axis=0)
    vv = jnp.concatenate([vp, vc_ref[rows, :]], axis=0)
    s = lax.dot_general(qs, kk, (((1,), (1,)), ((), ())), preferred_element_type=F32) * SWA_SCALE + bias
    if sb == 0:
        col = lax.broadcasted_iota(jnp.int32, (1, 2 * WINDOW), 1)
        s = s + jnp.where((col < WINDOW) & (i == 0), NEG, 0.0)
    return rows, qs, kk, vv, s


def _stack_heads(ref, rows, lead=None):
    if lead is None:
        return jnp.concatenate([ref[rows, hh * HEAD_PAD:(hh + 1) * HEAD_PAD] for hh in range(SWA_R)], axis=0)
    return jnp.concatenate([ref[hh, rows, :] for hh in range(SWA_R)], axis=0)


def _swa_fwd(name, proj_b, bias, sinks, *, tq):
    S = proj_b.shape[0]
    tq = min(tq, S)
    nsb = tq // WINDOW

    def body(q_ref, kc_ref, kp_ref, vc_ref, vp_ref, bias_ref, sink_ref, o_ref, lse_ref):
        i = pl.program_id(1)
        bias_v = bias_ref[...].reshape(SWA_R * WINDOW, 2 * WINDOW)
        sink = jnp.concatenate([jnp.zeros((WINDOW, 1), F32) + sink_ref[hh, 0:1, 0:1] for hh in range(SWA_R)], axis=0)
        nxt = _swa_block(i, 0, q_ref, kc_ref, kp_ref, vc_ref, vp_ref, bias_v, sink)
        for sb in range(nsb):
            rows, _, _, vv, s = nxt
            if sb + 1 < nsb:
                nxt = _swa_block(i, sb + 1, q_ref, kc_ref, kp_ref, vc_ref, vp_ref, bias_v, sink)
            m = jnp.maximum(jnp.max(s, axis=1, keepdims=True), sink)
            p = jnp.exp(s - m)
            l = jnp.sum(p, axis=1, keepdims=True) + jnp.exp(sink - m)
            o = jnp.dot(p.astype(BF16), vv, preferred_element_type=F32) / l
            lse_v = m + jnp.log(l)
            for hh in range(SWA_R):
                o_ref[rows, hh * HEAD_PAD:(hh + 1) * HEAD_PAD] = o[hh * WINDOW:(hh + 1) * WINDOW].astype(o_ref.dtype)
                lse_ref[hh, rows, :] = lse_v[hh * WINDOW:(hh + 1) * WINDOW]

    return pl.pallas_call(
        body, name=name, grid=(SWA_KV_HEADS, S // tq),
        in_specs=_swa_specs(tq),
        out_specs=[pl.BlockSpec((tq, SWA_R * HEAD_PAD), lambda g, i: (i, g)),
                   pl.BlockSpec((SWA_R, tq, 1), lambda g, i: (g, i, 0))],
        out_shape=[jax.ShapeDtypeStruct((S, SWA_HEADS * HEAD_PAD), BF16),
                   jax.ShapeDtypeStruct((SWA_HEADS, S, 1), F32)],
        compiler_params=_cparams("parallel", "parallel"),
    )(proj_b, proj_b, proj_b, proj_b, proj_b, bias, sinks)


def _swa_bwd(name, proj_b, bias, sinks, o, do, lse, *, tq):
    S = proj_b.shape[0]
    tq = min(tq, S)
    nsb = tq // WINDOW
    nq = S // tq

    def body(q_ref, kc_ref, kp_ref, vc_ref, vp_ref, bias_ref, sink_ref, o_ref, do_ref, lse_ref,
             dq_ref, dk_ref, dv_ref, dke_ref, dve_ref, dbias_ref, dsink_ref):
        i = pl.program_id(1)

        @pl.when(i == 0)
        def _():
            dbias_ref[...] = jnp.zeros(dbias_ref.shape, F32)
            dsink_ref[...] = jnp.zeros(dsink_ref.shape, F32)

        bias_v = bias_ref[...].reshape(SWA_R * WINDOW, 2 * WINDOW)
        sink = jnp.concatenate([jnp.zeros((WINDOW, 1), F32) + sink_ref[hh, 0:1, 0:1] for hh in range(SWA_R)], axis=0)
        dk_own, dv_own, dk_prev, dv_prev = [], [], [], []
        dbias_acc = jnp.zeros((SWA_R * WINDOW, 2 * WINDOW), F32)
        def block(sb):
            rows, qs, kk, vv, s = _swa_block(i, sb, q_ref, kc_ref, kp_ref, vc_ref, vp_ref, bias_v, sink)
            do_s = _stack_heads(do_ref, rows)
            dp = lax.dot_general(do_s, vv, (((1,), (1,)), ((), ())), preferred_element_type=F32)
            return rows, qs, kk, do_s, s, dp

        nxt = block(0)
        for sb in range(nsb):
            rows, qs, kk, do_s, s, dp = nxt
            if sb + 1 < nsb:
                nxt = block(sb + 1)
            lse_v = _stack_heads(lse_ref, rows, lead=True)
            delta = jnp.sum(do_s.astype(F32) * _stack_heads(o_ref, rows).astype(F32), axis=1, keepdims=True)
            p = jnp.exp(s - lse_v)
            dsp = p * (dp - delta)
            dbias_acc = dbias_acc + dsp
            ds = (dsp * SWA_SCALE).astype(BF16)
            dq = jnp.dot(ds, kk, preferred_element_type=F32)
            dkk = jnp.dot(qs.T, ds, preferred_element_type=F32)
            dvv = jnp.dot(do_s.T, p.astype(BF16), preferred_element_type=F32)
            dk_prev.append(dkk[:, :WINDOW].T)
            dk_own.append(dkk[:, WINDOW:].T)
            dv_prev.append(dvv[:, :WINDOW].T)
            dv_own.append(dvv[:, WINDOW:].T)
            psink = jnp.exp(sink - lse_v) * delta
            for hh in range(SWA_R):
                hrows = slice(hh * WINDOW, (hh + 1) * WINDOW)
                dq_ref[rows, hh * HEAD_PAD:(hh + 1) * HEAD_PAD] = dq[hrows].astype(dq_ref.dtype)
                dsink_ref[hh] += jnp.zeros((8, 128), F32) - jnp.sum(psink[hrows])
        dbias_ref[...] += dbias_acc.reshape(SWA_R, WINDOW, 2 * WINDOW)
        for sb in range(nsb):
            rows = slice(sb * WINDOW, (sb + 1) * WINDOW)
            if sb + 1 < nsb:
                dk_ref[rows, :] = dk_own[sb] + dk_prev[sb + 1]
                dv_ref[rows, :] = dv_own[sb] + dv_prev[sb + 1]
            else:
                dk_ref[rows, :] = dk_own[sb]
                dv_ref[rows, :] = dv_own[sb]
        dke_ref[...] = dk_prev[0]
        dve_ref[...] = dv_prev[0]

    in_specs = _swa_specs(tq) + [
        pl.BlockSpec((tq, SWA_R * HEAD_PAD), lambda g, i: (i, g)),
        pl.BlockSpec((tq, SWA_R * HEAD_PAD), lambda g, i: (i, g)),
        pl.BlockSpec((SWA_R, tq, 1), lambda g, i: (g, i, 0)),
    ]
    kv_blk = pl.BlockSpec((tq, HEAD_PAD), lambda g, i: (i, g))
    edge_blk = pl.BlockSpec((WINDOW, HEAD_PAD), lambda g, i: (i, g))
    return pl.pallas_call(
        body, name=name, grid=(SWA_KV_HEADS, nq),
        in_specs=in_specs,
        out_specs=[pl.BlockSpec((tq, SWA_R * HEAD_PAD), lambda g, i: (i, g)), kv_blk, kv_blk, edge_blk, edge_blk,
                   pl.BlockSpec((SWA_R, WINDOW, 2 * WINDOW), lambda g, i: (g, 0, 0)),
                   pl.BlockSpec((SWA_R, 8, 128), lambda g, i: (g, 0, 0))],
        out_shape=[jax.ShapeDtypeStruct((S, SWA_HEADS * HEAD_PAD), BF16),
                   jax.ShapeDtypeStruct((S, SWA_KV_HEADS * HEAD_PAD), F32),
                   jax.ShapeDtypeStruct((S, SWA_KV_HEADS * HEAD_PAD), F32),
                   jax.ShapeDtypeStruct((nq * WINDOW, SWA_KV_HEADS * HEAD_PAD), F32),
                   jax.ShapeDtypeStruct((nq * WINDOW, SWA_KV_HEADS * HEAD_PAD), F32),
                   jax.ShapeDtypeStruct((SWA_HEADS, WINDOW, 2 * WINDOW), F32),
                   jax.ShapeDtypeStruct((SWA_HEADS, 8, 128), F32)],
        compiler_params=_cparams("arbitrary", "arbitrary"),
    )(proj_b, proj_b, proj_b, proj_b, proj_b, bias, sinks, o, do, lse)


def _dproj_b(name, dq_swa, dq_mem, dk, dv, dk_edge, dv_edge, *, tq):
    S = dq_swa.shape[0]
    tq = min(tq, S)
    nq = S // tq

    def body(dqs_ref, dqm_ref, dk_ref, dv_ref, dke_ref, dve_ref, o_ref):
        i = pl.program_id(0)
        o_ref[:, 0:1024] = dqs_ref[...]
        o_ref[:, 1024:1536] = dqm_ref[...].astype(o_ref.dtype)
        o_ref[:, 1536:1792] = dk_ref[...].astype(o_ref.dtype)
        o_ref[:, 1792:2048] = dv_ref[...].astype(o_ref.dtype)

        @pl.when(i < nq - 1)
        def _():
            last = slice(tq - WINDOW, tq)
            o_ref[last, 1536:1792] = (dk_ref[last, :] + dke_ref[...]).astype(o_ref.dtype)
            o_ref[last, 1792:2048] = (dv_ref[last, :] + dve_ref[...]).astype(o_ref.dtype)

    edge = pl.BlockSpec((WINDOW, SWA_KV_HEADS * HEAD_PAD), lambda i: (jnp.minimum(i + 1, nq - 1), 0))
    return pl.pallas_call(
        body, name=name, grid=(nq,),
        in_specs=[pl.BlockSpec((tq, 1024), lambda i: (i, 0)), pl.BlockSpec((tq, 512), lambda i: (i, 0)),
                  pl.BlockSpec((tq, 256), lambda i: (i, 0)), pl.BlockSpec((tq, 256), lambda i: (i, 0)), edge, edge],
        out_specs=pl.BlockSpec((tq, 2048), lambda i: (i, 0)),
        out_shape=jax.ShapeDtypeStruct((S, 2048), BF16),
        compiler_params=_cparams("parallel"),
    )(dq_swa, dq_mem, dk, dv, dk_edge, dv_edge)


def _exchange(name, send, *, per_peer):
    shape = send.shape[1:] if per_peer else send.shape

    def body(send_ref, recv_ref, send_sems, recv_sems, local_sem):
        x, y, c = lax.axis_index("x"), lax.axis_index("y"), lax.axis_index("c")
        me = 4 * x + 2 * y + c
        own = pltpu.make_async_copy(send_ref.at[me] if per_peer else send_ref, recv_ref.at[me], local_sem)
        own.start()
        copies = []
        for k in range(1, N_DEV):
            px = 1 - x if (k >> 2) & 1 else x
            py = 1 - y if (k >> 1) & 1 else y
            pc = 1 - c if k & 1 else c
            peer = 4 * px + 2 * py + pc
            out = pltpu.make_async_remote_copy(
                src_ref=send_ref.at[peer] if per_peer else send_ref, dst_ref=recv_ref.at[me],
                send_sem=send_sems.at[k - 1], recv_sem=recv_sems.at[k - 1],
                device_id=(px, py, pc), device_id_type=pl.DeviceIdType.MESH)
            out.start()
            back = pltpu.make_async_remote_copy(
                src_ref=send_ref.at[me] if per_peer else send_ref, dst_ref=recv_ref.at[peer],
                send_sem=send_sems.at[k - 1], recv_sem=recv_sems.at[k - 1],
                device_id=(px, py, pc), device_id_type=pl.DeviceIdType.MESH)
            copies.append((out, back))
        for out, back in copies:
            out.wait_send()
            back.wait_recv()
        own.wait()

    return pl.pallas_call(
        body, name=name,
        in_specs=[pl.BlockSpec(memory_space=pl.ANY)],
        out_specs=pl.BlockSpec(memory_space=pl.ANY),
        out_shape=jax.ShapeDtypeStruct((N_DEV,) + tuple(shape), send.dtype),
        scratch_shapes=[pltpu.SemaphoreType.DMA((N_DEV - 1,)), pltpu.SemaphoreType.DMA((N_DEV - 1,)),
                        pltpu.SemaphoreType.DMA(())],
    )(send)


def _gather_forwarded(name, block):
    def body(x_ref, out_ref, send_sems, recv_sems, local_sem):
        x, y, c = lax.axis_index("x"), lax.axis_index("y"), lax.axis_index("c")
        me, sibling = (x, y, c), (x, y, 1 - c)
        chips = [(1 - x, y), (x, 1 - y), (1 - x, 1 - y)]

        def slot(px, py, pc):
            return out_ref.at[4 * px + 2 * py + pc]

        def copy(k, blk, to, src=None):
            return pltpu.make_async_remote_copy(
                src_ref=slot(*blk) if src is None else src, dst_ref=slot(*blk),
                send_sem=send_sems.at[k], recv_sem=recv_sems.at[k],
                device_id=to, device_id_type=pl.DeviceIdType.MESH)

        mine = pltpu.make_async_copy(x_ref, slot(*me), local_sem)
        mine.start()
        first = [copy(0, me, sibling, src=x_ref)]
        first += [copy(1 + j, me, (*chip, c), src=x_ref) for j, chip in enumerate(chips)]
        for cp in first:
            cp.start()
        passed = [copy(4 + j, (*chip, c), sibling) for j, chip in enumerate(chips)]
        for j, chip in enumerate(chips):
            copy(1 + j, (*chip, c), me).wait_recv()
            passed[j].start()
        copy(0, sibling, me).wait_recv()
        for j, chip in enumerate(chips):
            copy(4 + j, (*chip, 1 - c), me).wait_recv()
        for cp in first + passed:
            cp.wait_send()
        mine.wait()

    return pl.pallas_call(
        body, name=name,
        in_specs=[pl.BlockSpec(memory_space=pl.ANY)],
        out_specs=pl.BlockSpec(memory_space=pl.ANY),
        out_shape=jax.ShapeDtypeStruct((N_DEV,) + tuple(block.shape), block.dtype),
        scratch_shapes=[pltpu.SemaphoreType.DMA((N_DEV - 1,)), pltpu.SemaphoreType.DMA((N_DEV - 1,)),
                        pltpu.SemaphoreType.DMA(())],
    )(block)


def _sibling_swap(name, block):
    def body(x_ref, out_ref, send_sem, recv_sem):
        x, y, c = lax.axis_index("x"), lax.axis_index("y"), lax.axis_index("c")
        cp = pltpu.make_async_remote_copy(src_ref=x_ref, dst_ref=out_ref, send_sem=send_sem, recv_sem=recv_sem,
                                          device_id=(x, y, 1 - c), device_id_type=pl.DeviceIdType.MESH)
        cp.start()
        cp.wait()

    return pl.pallas_call(
        body, name=name,
        in_specs=[pl.BlockSpec(memory_space=pl.ANY)],
        out_specs=pl.BlockSpec(memory_space=pl.ANY),
        out_shape=jax.ShapeDtypeStruct(block.shape, block.dtype),
        scratch_shapes=[pltpu.SemaphoreType.DMA(()), pltpu.SemaphoreType.DMA(())],
    )(block)


def _chip_exchange(name, send):
    def body(send_ref, recv_ref, send_sems, recv_sems, local_sem):
        x, y, c = lax.axis_index("x"), lax.axis_index("y"), lax.axis_index("c")
        me = 2 * x + y
        own = pltpu.make_async_copy(send_ref.at[me], recv_ref.at[me], local_sem)
        own.start()
        copies = []
        for k in range(1, 4):
            px = 1 - x if (k >> 1) & 1 else x
            py = 1 - y if k & 1 else y
            peer = 2 * px + py
            out = pltpu.make_async_remote_copy(
                src_ref=send_ref.at[peer], dst_ref=recv_ref.at[me],
                send_sem=send_sems.at[k - 1], recv_sem=recv_sems.at[k - 1],
                device_id=(px, py, c), device_id_type=pl.DeviceIdType.MESH)
            out.start()
            back = pltpu.make_async_remote_copy(
                src_ref=send_ref.at[me], dst_ref=recv_ref.at[peer],
                send_sem=send_sems.at[k - 1], recv_sem=recv_sems.at[k - 1],
                device_id=(px, py, c), device_id_type=pl.DeviceIdType.MESH)
            copies.append((out, back))
        for out, back in copies:
            out.wait_send()
            back.wait_recv()
        own.wait()

    return pl.pallas_call(
        body, name=name,
        in_specs=[pl.BlockSpec(memory_space=pl.ANY)],
        out_specs=pl.BlockSpec(memory_space=pl.ANY),
        out_shape=jax.ShapeDtypeStruct(send.shape, send.dtype),
        scratch_shapes=[pltpu.SemaphoreType.DMA((3,)), pltpu.SemaphoreType.DMA((3,)), pltpu.SemaphoreType.DMA(())],
    )(send)


def _reduce_scatter(parts):
    lanes = 128
    c = lax.axis_index("c")

    def core_half(core):
        return jnp.concatenate(
            [lax.dynamic_index_in_dim(p.reshape(4, 2, p.shape[1], lanes), core, axis=1, keepdims=False)
             for p in parts], axis=1)

    mine = core_half(c)
    rows = mine.shape[1]
    mine = mine.reshape(4 * rows, lanes)
    theirs = core_half(1 - c).reshape(4 * rows, lanes)
    from_sibling = _sibling_swap("grads_to_sibling", theirs)
    tm = max(t for t in range(16, ADAM_TM + 1, 16) if rows % t == 0)
    chip_sum = _rowwise("grads_chip_sum", lambda a, b: (a.astype(F32) + b.astype(F32),),
                        [_rows(mine), _rows(from_sibling)], [((4 * rows, lanes), BF16, "rows")],
                        rows=4 * rows, tm=tm)[0]
    return _chip_exchange("scatter_grads", chip_sum.reshape(4, rows, lanes))


def _adam(name, recv, w, m, v, *, tm=None):
    R = w.shape[0]
    n_parts = recv.shape[0]
    tm = max(t for t in range(8, min(tm or ADAM_TM, R) + 1, 8) if R % t == 0)
    c1 = 1.0 / (1.0 - ADAM_B1 ** ADAM_STEP)
    c2 = 1.0 / (1.0 - ADAM_B2 ** ADAM_STEP)

    def body(r_ref, w_ref, m_ref, v_ref, g_ref, d_ref, nm_ref, nv_ref):
        g = r_ref[0].astype(F32)
        for j in range(1, n_parts):
            g = g + r_ref[j].astype(F32)
        wv = w_ref[...]
        nm = ADAM_B1 * m_ref[...] + (1.0 - ADAM_B1) * g
        nv = ADAM_B2 * v_ref[...] + (1.0 - ADAM_B2) * (g * g)
        m_hat = nm * c1
        v_hat = nv * c2
        g_ref[...] = g
        d_ref[...] = -ADAM_LR * (m_hat / (jnp.sqrt(v_hat) + ADAM_EPS) + ADAM_WD * wv)
        nm_ref[...] = nm
        nv_ref[...] = nv

    row = pl.BlockSpec((tm, 128), lambda i: (i, 0))
    return pl.pallas_call(
        body, name=name, grid=(R // tm,),
        in_specs=[pl.BlockSpec((n_parts, tm, 128), lambda i: (0, i, 0)), row, row, row],
        out_specs=[row, row, row, row],
        out_shape=[jax.ShapeDtypeStruct((R, 128), F32)] * 4,
        compiler_params=_cparams("parallel"),
    )(recv, w, m, v)


def _pack_rows(arrs):
    return jnp.concatenate([a.reshape(-1, 128) for a in arrs], axis=0)


def _unstack(g, shape, axis):
    t = jnp.moveaxis(g, 0, axis)
    return t.reshape(shape)


def _restack(full, axis):
    s = full.shape
    t = full.reshape(s[:axis] + (N_DEV, s[axis] // N_DEV) + s[axis + 1:])
    return jnp.moveaxis(t, axis, 0)


def _pad_heads(w, heads, hd, axis):
    s = w.shape
    t = w.reshape(s[:axis] + (heads, hd) + s[axis + 1:])
    pad = [(0, 0)] * t.ndim
    pad[axis + 1] = (0, HEAD_PAD - hd)
    t = jnp.pad(t, pad)
    return t.reshape(s[:axis] + (heads * HEAD_PAD,) + s[axis + 1:])


def _unpad_heads(w, heads, hd, axis):
    s = w.shape
    t = w.reshape(s[:axis] + (heads, HEAD_PAD) + s[axis + 1:])
    t = lax.slice_in_dim(t, 0, hd, axis=axis + 1)
    return t.reshape(s[:axis] + (heads * hd,) + s[axis + 1:])


def _layer_weights(full, l):
    w_in = full["w_in"][l]
    cq, kva, qs, ks, vs, qm, gates = (w_in[:, 0:256], w_in[:, 256:416], w_in[:, 416:928], w_in[:, 928:1056],
                                       w_in[:, 1056:1184], w_in[:, 1184:1696], w_in[:, 1696:4768])
    wa = jnp.concatenate([gates, cq, jnp.pad(kva, ((0, 0), (0, 96)))], axis=1)
    wb = jnp.concatenate([_pad_heads(qs, SWA_HEADS, SWA_HD, 1), qm, _pad_heads(ks, SWA_KV_HEADS, SWA_HD, 1),
                          _pad_heads(vs, SWA_KV_HEADS, SWA_HD, 1)], axis=1)
    wuq = _pad_heads(full["w_uq"][l], MLA_HEADS, MLA_NOPE + MLA_ROPE, 1)
    ukv = full["w_ukv"][l].reshape(MLA_KV_LORA, MLA_HEADS, MLA_NOPE + MLA_V)
    wuk = _pad_heads(ukv[:, :, :MLA_NOPE].reshape(MLA_KV_LORA, -1), MLA_HEADS, MLA_NOPE, 1)
    wuv = _pad_heads(ukv[:, :, MLA_NOPE:].reshape(MLA_KV_LORA, -1), MLA_HEADS, MLA_V, 1)
    wo_mla = _pad_heads(full["w_o_mla"][l], MLA_HEADS, MLA_V, 0)
    wo_swa = _pad_heads(full["w_o_swa"][l], SWA_HEADS, SWA_HD, 0)
    wo_mem = full["w_o_mem"][l]
    w = dict(wag=wa[:, :3072], wat=wa[:, 3072:], wb=wb, wuq=wuq, wuk=wuk, wuv=wuv, wo_mla=wo_mla, wo_swa=wo_swa,
             wo_mem=wo_mem, wmem=full["w_mem_kv"][l], wout=full["w_out"][l], wup=full["w_up"][l],
             wdown=full["w_down"][l])
    w.update({k + "_t": v.T for k, v in w.items()})
    return w


def _layer_weight_grads(g):
    dwa_g, dwa_t, dwb = g["wag"], g["wat"], g["wb"]
    d_in = jnp.concatenate([
        dwa_t[:, 0:256], dwa_t[:, 256:416],
        _unpad_heads(dwb[:, 0:1024], SWA_HEADS, SWA_HD, 1),
        _unpad_heads(dwb[:, 1536:1792], SWA_KV_HEADS, SWA_HD, 1),
        _unpad_heads(dwb[:, 1792:2048], SWA_KV_HEADS, SWA_HD, 1),
        dwb[:, 1024:1536], dwa_g], axis=1)
    duk = _unpad_heads(g["wuk"], MLA_HEADS, MLA_NOPE, 1).reshape(MLA_KV_LORA, MLA_HEADS, MLA_NOPE)
    duv = _unpad_heads(g["wuv"], MLA_HEADS, MLA_V, 1).reshape(MLA_KV_LORA, MLA_HEADS, MLA_V)
    return dict(
        w_in=d_in,
        w_uq=_unpad_heads(g["wuq"], MLA_HEADS, MLA_NOPE + MLA_ROPE, 1),
        w_ukv=jnp.concatenate([duk, duv], axis=2).reshape(MLA_KV_LORA, -1),
        w_mem_kv=g["wmem"],
        w_o_mla=_unpad_heads(g["wo_mla"], MLA_HEADS, MLA_V, 0),
        w_o_swa=_unpad_heads(g["wo_swa"], SWA_HEADS, SWA_HD, 0),
        w_o_mem=g["wo_mem"], w_out=g["wout"], w_up=g["wup"], w_down=g["wdown"])


def _rope_tables(S):
    pos = jnp.arange(S, dtype=F32)
    inv = 1.0 / (ROPE_THETA ** (jnp.arange(0, MLA_ROPE, 2, dtype=F32) / MLA_ROPE))
    ang = pos[:, None] * inv[None, :]
    cos, sin = jnp.cos(ang), jnp.sin(ang)
    z16 = jnp.zeros((S, 16), F32)
    z32 = jnp.zeros((S, 32), F32)
    c = jnp.concatenate([jnp.ones((S, 64), F32), cos, cos, z32], axis=1)
    ck = jnp.concatenate([jnp.zeros((S, 64), F32), cos, cos, z32], axis=1)
    s1 = jnp.concatenate([jnp.zeros((S, 80), F32), sin, z32], axis=1)
    s2 = jnp.concatenate([jnp.zeros((S, 64), F32), -sin, z16, z32], axis=1)
    return c, ck, s1, s2


def _t5_bucket(dist):
    n = jnp.maximum(dist, 0)
    max_exact = REL_BUCKETS // 2
    nf = jnp.maximum(n, 1).astype(F32)
    large = max_exact + (jnp.log(nf / max_exact) / math.log(REL_MAX_DIST / max_exact)
                         * (REL_BUCKETS - max_exact)).astype(jnp.int32)
    large = jnp.minimum(large, REL_BUCKETS - 1)
    return jnp.where(n < max_exact, n, large)


def _bias_onehot():
    qi = jnp.arange(WINDOW)[:, None]
    kj = jnp.arange(2 * WINDOW)[None, :]
    dist = qi + WINDOW - kj
    valid = (dist >= 0) & (dist < WINDOW)
    bucket = _t5_bucket(dist)
    onehot = (bucket[None] == jnp.arange(REL_BUCKETS)[:, None, None]) & valid[None]
    return (onehot.reshape(REL_BUCKETS, -1).astype(F32),
            jnp.where(valid, 0.0, NEG).astype(F32).reshape(1, -1))


def _rstd(x):
    return lax.rsqrt(jnp.mean(x * x, axis=-1, keepdims=True) + EPS)


def _norm_bwd(dh, x, g):
    r = _rstd(x)
    xh = x * r
    w = dh * g
    dx = r * (w - xh * jnp.mean(w * xh, axis=-1, keepdims=True))
    return dx, jnp.sum(dh * xh, axis=0, keepdims=True)


def _tile_lanes(t, n):
    return jnp.tile(t, (1, n // t.shape[1])) if n != t.shape[1] else t


def _rope_fwd(a, c, s1, s2):
    n = a.shape[1]
    return (a * _tile_lanes(c, n) + pltpu.roll(a, 16, 1) * _tile_lanes(s1, n)
            + pltpu.roll(a, n - 16, 1) * _tile_lanes(s2, n))


def _rope_bwd(d, c, s1, s2):
    n = d.shape[1]
    return (d * _tile_lanes(c, n) + pltpu.roll(d * _tile_lanes(s1, n), n - 16, 1)
            + pltpu.roll(d * _tile_lanes(s2, n), 16, 1))


def _sigmoid(x):
    return 1.0 / (1.0 + jnp.exp(-x))


def _rmsnorm(name, x, g, dtype):
    def fn(xv, gv):
        return ((xv * _rstd(xv)) * gv,)
    return _rowwise(name, fn, [_rows(x), _full(g)], [(x.shape, dtype, "rows")], rows=x.shape[0])[0]


def _residual_norm_bwd(name, dres, dh, x, g):
    def fn(dr, dhv, xv, gv):
        dx, dg = _norm_bwd(dhv, xv, gv)
        return dr + dx, dg
    return _rowwise(name, fn, [_rows(dres), _rows(dh), _rows(x), _full(g)],
                    [(x.shape, F32, "rows"), (g.shape, F32, "acc")], rows=x.shape[0])


def _norm_bwd_epilogue(dh, dres, x, g):
    dx, dg = _norm_bwd(dh, x, g)
    return dres + dx, dg


def _add_and_norm(acc, r, g):
    xs = acc + r
    return xs, xs * _rstd(xs) * g


def _layer_fwd(l, x, h, mem, w, p, next_norm, tabs, swa_bias, S):
    c, ck, s1, s2 = tabs
    n = f"l{l}_"
    gates = _mm(n + "proj_gates", h, w["wag"], [BF16], tm=MM_TM_BF16)
    proj_a = _mm(n + "proj_tail", h, w["wat"], [F32])
    proj_b = _mm(n + "proj_b", h, w["wb"], [BF16], tm=MM_TM_BF16)

    def prep(cq, kva, qn, kvn, ckv, s1v, s2v):
        cqn = cq * _rstd(cq) * qn
        ckv_ = kva[:, :128]
        ckvn = ckv_ * _rstd(ckv_) * kvn
        pe = pltpu.roll(kva[:, 128:], 64, 1)
        return cqn, ckvn, _rope_fwd(pe, ckv, s1v, s2v)

    cqn, ckvn, kpe = _rowwise(
        n + "mla_prep", prep,
        [_rows(proj_a, 256, 0), _rows(proj_a, 256, 1), _full(p["mla_q_norm"]), _full(p["mla_kv_norm"]),
         _rows(ck), _rows(s1), _rows(s2)],
        [((S, 256), BF16, "rows"), ((S, 128), BF16, "rows"), ((S, 128), F32, "rows")], rows=S)

    q_mla = _mm(n + "q_mla", cqn, w["wuq"], [BF16],
                epi=lambda acc, cv, s1v, s2v: (_rope_fwd(acc, cv, s1v, s2v) * (MLA_SCALE * LOG2E),),
                extras=[(c, "m"), (s1, "m"), (s2, "m")])
    k_mla = _mm(n + "k_mla", ckvn, w["wuk"], [BF16],
                epi=lambda acc, kp: (acc + _tile_lanes(kp, acc.shape[1]),), extras=[(kpe, "m")])
    den = ((jnp.arange(MLA_HEADS * HEAD_PAD) % HEAD_PAD) // 2 == DEN_LANE // 2).astype(F32)[None]
    v_mla = _mm(n + "v_mla", ckvn, w["wuv"], [BF16], epi=lambda acc, dv: (acc + dv,), extras=[(den, "n")])
    o_mla, lse_mla = _causal_fwd(n + "mla_fwd", q_mla, k_mla, v_mla, heads=MLA_HEADS, tile=MLA_TILE,
                                 chunk=MLA_CHUNK_FWD)
    o_swa, lse_swa = _swa_fwd(n + "swa_fwd", proj_b, swa_bias, p["sinks"], tq=SWA_TQ)
    mn = _rmsnorm(n + "mem_norm", mem, p["mem_norm"], BF16)
    kvm = _mm(n + "kv_mem", mn, w["wmem"], [BF16])
    o_mem, lse_mem = _mem_fwd(n + "mem_fwd", proj_b, kvm, tq=MEM_TQ, chunk=MEM_CHUNK)
    t0 = _mm(n + "t_mla", o_mla, w["wo_mla"], [BF16], tm=MM_TM_BF16)
    t1 = _mm(n + "t_swa", o_swa, w["wo_swa"], [BF16], tm=MM_TM_BF16)
    t2 = _mm(n + "t_mem", o_mem, w["wo_mem"], [BF16], tm=MM_TM_BF16)

    def merge(g0, g1, g2, bg, a0, a1, a2):
        y = (_sigmoid(g0 + bg[:, 0:1024]) * a0 + _sigmoid(g1 + bg[:, 1024:2048]) * a1
             + _sigmoid(g2 + bg[:, 2048:3072]) * a2)
        return (y,)

    y = _rowwise(n + "merge", merge,
                 [_rows(gates, 1024, 0), _rows(gates, 1024, 1), _rows(gates, 1024, 2), _full(p["b_gate"]),
                  _rows(t0), _rows(t1), _rows(t2)], [((S, D_MODEL), BF16, "rows")], rows=S)[0]
    x1, h2 = _mm(n + "out_proj", y, w["wout"], [F32, BF16], epi=_add_and_norm,
                 extras=[(x, "mn"), (p["mlp_norm"], "n")], tn=D_MODEL)
    act = _mm(n + "mlp_up", h2, w["wup"], [BF16], epi=lambda acc: (jnp.square(jnp.maximum(acc, 0.0)),),
              tm=MM_TM_BF16)
    if next_norm is None:
        x2 = _mm(n + "mlp_down", act, w["wdown"], [F32], epi=lambda acc, r: (acc + r,), extras=[(x1, "mn")],
                 tk=MM_TK_DEEP)
        h_next = None
    else:
        x2, h_next = _mm(n + "mlp_down", act, w["wdown"], [F32, BF16], epi=_add_and_norm,
                         extras=[(x1, "mn"), (next_norm, "n")], tn=D_MODEL, tk=MM_TK_DEEP)
    saved = dict(x=x, h=h, gates=gates, proj_a=proj_a, proj_b=proj_b, cqn=cqn, ckvn=ckvn, q_mla=q_mla, k_mla=k_mla, v_mla=v_mla,
                 o_mla=o_mla, lse_mla=lse_mla, o_swa=o_swa, lse_swa=lse_swa, mn=mn, kvm=kvm, o_mem=o_mem,
                 lse_mem=lse_mem, t0=t0, t1=t1, t2=t2, y=y, x1=x1, h2=h2, act=act)
    return x2, h_next, saved


def _layer_bwd(l, dx2, mem, w, p, tabs, swa_bias, sv, S):
    c, ck, s1, s2 = tabs
    n = f"l{l}_b_"
    gw = {}
    gs = {}
    du = _mm(n + "d_act", dx2, w["wdown_t"], [BF16],
             epi=lambda acc, av: (acc * (2.0 * jnp.sqrt(av.astype(F32))),), extras=[(sv["act"], "mn")], tn=1024)
    gw["wdown"] = _mm_tn(n + "g_wdown", sv["act"], dx2)
    gw["wup"] = _mm_tn(n + "g_wup", sv["h2"], du)
    dx1, gs["mlp_norm"] = _mm(n + "d_h2", du, w["wup_t"], [F32], epi=_norm_bwd_epilogue,
                              extras=[(dx2, "mn"), (sv["x1"], "mn"), (p["mlp_norm"], "n")], tn=D_MODEL, col_sums=1)
    gw["wout"] = _mm_tn(n + "g_wout", sv["y"], dx1)
    dy = _mm(n + "d_y", dx1, w["wout_t"], [F32])

    def merge_bwd(dyv, g0, g1, g2, bg, a0, a1, a2):
        outs, dgs = [], []
        for b, (gv, av) in enumerate(((g0, a0), (g1, a1), (g2, a2))):
            sg = _sigmoid(gv + bg[:, b * 1024:(b + 1) * 1024])
            outs.append(dyv * sg)
            dgs.append(dyv * av * sg * (1.0 - sg))
        dg = jnp.concatenate(dgs, axis=1)
        return outs[0], outs[1], outs[2], dg, jnp.sum(dg, axis=0, keepdims=True)

    pa = sv["proj_a"]
    gt = sv["gates"]
    dt0, dt1, dt2, dgates, gs["b_gate"] = _rowwise(
        n + "merge", merge_bwd,
        [_rows(dy), _rows(gt, 1024, 0), _rows(gt, 1024, 1), _rows(gt, 1024, 2), _full(p["b_gate"]),
         _rows(sv["t0"]), _rows(sv["t1"]), _rows(sv["t2"])],
        [((S, D_MODEL), BF16, "rows")] * 3 + [((S, 3 * D_MODEL), BF16, "rows"), ((1, 3 * D_MODEL), F32, "acc")],
        rows=S)
    gw["wo_mla"] = _mm_tn(n + "g_wo_mla", sv["o_mla"], dt0)
    gw["wo_swa"] = _mm_tn(n + "g_wo_swa", sv["o_swa"], dt1)
    gw["wo_mem"] = _mm_tn(n + "g_wo_mem", sv["o_mem"], dt2)
    do_mla = _mm(n + "d_o_mla", dt0, w["wo_mla_t"], [BF16], epi=lambda acc, ov: (_with_neg_delta(acc, ov),),
                 extras=[(sv["o_mla"], "mn")], tn=MLA_HEADS * HEAD_PAD)
    do_swa = _mm(n + "d_o_swa", dt1, w["wo_swa_t"], [BF16])
    do_mem = _mm(n + "d_o_mem", dt2, w["wo_mem_t"], [BF16])
    pb = sv["proj_b"]
    dq_mla, dk_mla, dv_mla = _causal_bwd(
        n + "mla_bwd", sv["q_mla"], sv["k_mla"], sv["v_mla"], do_mla, sv["lse_mla"], heads=MLA_HEADS,
        tile=MLA_TILE_BWD, chunk=MLA_CHUNK)
    dq_swa, dk_swa, dv_swa, dk_edge, dv_edge, dbias, dsink = _swa_bwd(
        n + "swa_bwd", pb, swa_bias, p["sinks"], sv["o_swa"], do_swa, sv["lse_swa"], tq=SWA_TQ)
    dq_mem, dkvm = _mem_bwd(n + "mem_bwd", pb, sv["kvm"], sv["o_mem"], do_mem, sv["lse_mem"], tq=MEM_TQ,
                            chunk=MEM_CHUNK)
    gs["dbias"] = dbias
    gs["sinks"] = dsink[:, 0, 0]
    gw["wmem"] = _mm_tn(n + "g_wmem", sv["mn"], dkvm)
    dmn = _mm(n + "d_mn", dkvm, w["wmem_t"], [F32])
    _, gs["mem_norm"] = _residual_norm_bwd(n + "mem_norm", dmn, dmn, mem, p["mem_norm"])
    dq_pre = _rowwise(n + "q_unrope", lambda d, cv, s1v, s2v: (_rope_bwd(d * MLA_SCALE, cv, s1v, s2v),),
                      [_rows(dq_mla), _rows(c), _rows(s1), _rows(s2)], [((S, 1024), BF16, "rows")], rows=S)[0]
    gw["wuq"] = _mm_tn(n + "g_wuq", sv["cqn"], dq_pre)
    gw["wuk"] = _mm_tn(n + "g_wuk", sv["ckvn"], dk_mla)
    gw["wuv"] = _mm_tn(n + "g_wuv", sv["ckvn"], dv_mla)
    dcqn = _mm(n + "d_cqn", dq_pre, w["wuq_t"], [F32])
    dckvn = _mm(n + "d_ckvn_k", dk_mla, w["wuk_t"], [F32])
    dckvn = _mm(n + "d_ckvn_v", dv_mla, w["wuv_t"], [F32], epi=lambda acc, r: (acc + r,), extras=[(dckvn, "mn")])

    def mla_norm_bwd(dcq_n, dckv_n, dk, cq, kva, qn, kvn, ckv, s1v, s2v):
        dcq, dqn = _norm_bwd(dcq_n, cq, qn)
        dckv, dkvn = _norm_bwd(dckv_n, kva[:, :128], kvn)
        dkpe = dk[:, 0:128]
        for hh in range(1, MLA_HEADS):
            dkpe = dkpe + dk[:, hh * 128:(hh + 1) * 128]
        dpe = pltpu.roll(_rope_bwd(dkpe, ckv, s1v, s2v), 64, 1)
        return jnp.concatenate([dcq, dckv, dpe], axis=1), dqn, dkvn

    dtail, gs["mla_q_norm"], gs["mla_kv_norm"] = _rowwise(
        n + "mla_norm", mla_norm_bwd,
        [_rows(dcqn), _rows(dckvn), _rows(dk_mla), _rows(pa, 256, 0), _rows(pa, 256, 1),
         _full(p["mla_q_norm"]), _full(p["mla_kv_norm"]), _rows(ck), _rows(s1), _rows(s2)],
        [((S, 512), BF16, "rows"), ((1, 256), F32, "acc"), ((1, 128), F32, "acc")], rows=S)

    dproj_b = _dproj_b(n + "dproj_b", dq_swa, dq_mem, dk_swa, dv_swa, dk_edge, dv_edge, tq=SWA_TQ)
    h = sv["h"]
    gw["wag"] = _mm_tn(n + "g_wa_gates", h, dgates)
    gw["wat"] = _mm_tn(n + "g_wa_tail", h, dtail)
    gw["wb"] = _mm_tn(n + "g_wb", h, dproj_b)
    dh = _mm(n + "d_h_gates", dgates, w["wag_t"], [F32])
    dh = _mm(n + "d_h_tail", dtail, w["wat_t"], [F32], epi=lambda acc, r: (acc + r,), extras=[(dh, "mn")])
    dx, gs["attn_norm"] = _mm(n + "d_h_b", dproj_b, w["wb_t"], [F32],
                              epi=lambda acc, prev, dr, xv, gv: _norm_bwd_epilogue(acc + prev, dr, xv, gv),
                              extras=[(dh, "mn"), (dx1, "mn"), (sv["x"], "mn"), (p["attn_norm"], "n")],
                              tn=D_MODEL, tm=MM_TM // 2, col_sums=1)
    return dx, gw, gs


def _local_step(x, mem, loss_target, full, small):
    S = x.shape[0]
    tabs = _rope_tables(S)
    onehot, band = _bias_onehot()
    hi = lax.Precision.HIGHEST
    swa_bias = _mm("swa_bias", small["rel_bias"].T, onehot, [F32], epi=lambda acc, mk: (acc + mk,),
                   extras=[(band, "n")], cast=None, precision=hi, tn=8192).reshape(SWA_HEADS, WINDOW, 2 * WINDOW)
    ws, ps = [], []
    for l in range(DEPTH):
        ws.append(_layer_weights(full, l))
        ps.append(dict(
            attn_norm=small["attn_norm"][l][None], mem_norm=small["mem_norm"][l][None],
            b_gate=small["b_gate"][l][None], mla_q_norm=small["mla_q_norm"][l][None],
            mla_kv_norm=small["mla_kv_norm"][l][None], mlp_norm=small["mlp_norm"][l][None],
            sinks=jnp.broadcast_to(small["attn_sinks"][l][:, None, None], (SWA_HEADS, 8, 128))))
    saved = []
    xc = x
    hc = _rmsnorm("l0_attn_norm", x, ps[0]["attn_norm"], BF16)
    for l in range(DEPTH):
        next_norm = ps[l + 1]["attn_norm"] if l + 1 < DEPTH else None
        xc, hc, sv = _layer_fwd(l, xc, hc, mem, ws[l], ps[l], next_norm, tabs, swa_bias, S)
        saved.append(sv)

    fn_g = small["final_norm"][None]

    def loss_fn(xv, gv, tv):
        r = _rstd(xv)
        xh = xv * r
        err = xh * gv - tv
        dyv = err * (1.0 / D_MODEL)
        wv = dyv * gv
        dx = r * (wv - xh * jnp.mean(wv * xh, axis=-1, keepdims=True))
        part = 0.5 * jnp.sum(err * err) * (1.0 / D_MODEL)
        return dx, jnp.sum(dyv * xh, axis=0, keepdims=True), jnp.zeros((8, 128), F32) + part

    dx, g_final, loss_acc = _rowwise(
        "loss", loss_fn, [_rows(xc), _full(fn_g), _rows(loss_target)],
        [((S, D_MODEL), F32, "rows"), ((1, D_MODEL), F32, "acc"), ((8, 128), F32, "acc")], rows=S)

    gws, gss = [None] * DEPTH, [None] * DEPTH
    for l in reversed(range(DEPTH)):
        dx, gw, gs = _layer_bwd(l, dx, mem, ws[l], ps[l], tabs, swa_bias, saved[l], S)
        gws[l] = _layer_weight_grads(gw)
        gss[l] = gs

    dbias = (gss[0]["dbias"] + gss[1]["dbias"]).reshape(SWA_HEADS, -1)
    g_rel = _mm("g_rel_bias", dbias, onehot.T, [F32], cast=None, precision=hi, tk=8192).T
    wgrads = {k: jnp.stack([gws[l][k] for l in range(DEPTH)]) for k in gws[0]}
    sgrads = dict(
        rel_bias=g_rel,
        final_norm=g_final[0],
        attn_sinks=jnp.stack([gss[l]["sinks"] for l in range(DEPTH)]),
        **{k: jnp.concatenate([gss[l][k] for l in range(DEPTH)], axis=0)
           for k in ("attn_norm", "mem_norm", "b_gate", "mla_q_norm", "mla_kv_norm", "mlp_norm")})
    return loss_acc[0, 0], dx, wgrads, sgrads


def _pack_small(vals, loss):
    rows = []
    for name, shape in SMALL:
        flat = vals[name].astype(F32).reshape(-1)
        pad = (-flat.shape[0]) % 1024
        rows.append(jnp.pad(flat, (0, pad)).reshape(-1, 128))
    rows.append(jnp.zeros((8, 128), F32) + loss)
    return jnp.concatenate(rows, axis=0)


def _unpack_small(packed):
    out, r = {}, 0
    for name, shape in SMALL:
        size = math.prod(shape)
        nrows = 8 * -(-size // 1024)
        out[name] = packed[r:r + nrows].reshape(-1)[:size].reshape(shape)
        r += nrows
    return out, packed[r, 0]


def kernel(x, mem, rel_bias, attn_norm, mem_norm, w_in, b_gate, mla_q_norm, w_uq, mla_kv_norm, w_ukv, attn_sinks, w_mem_kv, w_o_mla, w_o_swa, w_o_mem, w_out, mlp_norm, w_up, w_down, final_norm, loss_target, m_rel_bias, m_attn_norm, m_mem_norm, m_w_in, m_b_gate, m_mla_q_norm, m_w_uq, m_mla_kv_norm, m_w_ukv, m_attn_sinks, m_w_mem_kv, m_w_o_mla, m_w_o_swa, m_w_o_mem, m_w_out, m_mlp_norm, m_w_up, m_w_down, m_final_norm, v_rel_bias, v_attn_norm, v_mem_norm, v_w_in, v_b_gate, v_mla_q_norm, v_w_uq, v_mla_kv_norm, v_w_ukv, v_attn_sinks, v_w_mem_kv, v_w_o_mla, v_w_o_swa, v_w_o_mem, v_w_out, v_mlp_norm, v_w_up, v_w_down, v_final_norm):
    wv = dict(rel_bias=rel_bias, attn_norm=attn_norm, mem_norm=mem_norm, w_in=w_in, b_gate=b_gate,
              mla_q_norm=mla_q_norm, w_uq=w_uq, mla_kv_norm=mla_kv_norm, w_ukv=w_ukv, attn_sinks=attn_sinks,
              w_mem_kv=w_mem_kv, w_o_mla=w_o_mla, w_o_swa=w_o_swa, w_o_mem=w_o_mem, w_out=w_out,
              mlp_norm=mlp_norm, w_up=w_up, w_down=w_down, final_norm=final_norm)
    mv = dict(rel_bias=m_rel_bias, attn_norm=m_attn_norm, mem_norm=m_mem_norm, w_in=m_w_in, b_gate=m_b_gate,
              mla_q_norm=m_mla_q_norm, w_uq=m_w_uq, mla_kv_norm=m_mla_kv_norm, w_ukv=m_w_ukv,
              attn_sinks=m_attn_sinks, w_mem_kv=m_w_mem_kv, w_o_mla=m_w_o_mla, w_o_swa=m_w_o_swa,
              w_o_mem=m_w_o_mem, w_out=m_w_out, mlp_norm=m_mlp_norm, w_up=m_w_up, w_down=m_w_down,
              final_norm=m_final_norm)
    vv = dict(rel_bias=v_rel_bias, attn_norm=v_attn_norm, mem_norm=v_mem_norm, w_in=v_w_in, b_gate=v_b_gate,
              mla_q_norm=v_mla_q_norm, w_uq=v_w_uq, mla_kv_norm=v_mla_kv_norm, w_ukv=v_w_ukv,
              attn_sinks=v_attn_sinks, w_mem_kv=v_w_mem_kv, w_o_mla=v_w_o_mla, w_o_swa=v_w_o_swa,
              w_o_mem=v_w_o_mem, w_out=v_w_out, mlp_norm=v_mlp_norm, w_up=v_w_up, w_down=v_w_down,
              final_norm=v_final_norm)

    shard_rows = [math.prod(_shard_shape(shape, axis)) // 128 for _, shape, axis in WSPECS]
    gathered = _gather_forwarded("gather_weights", _pack_rows([wv[name].astype(BF16) for name, _, _ in WSPECS]))
    full, r = {}, 0
    for (name, shape, axis), nr in zip(WSPECS, shard_rows):
        full[name] = _unstack(gathered[:, r:r + nr].reshape((N_DEV,) + _shard_shape(shape, axis)), shape, axis)
        r += nr

    loss_part, grad_x, wgrads, sgrads = _local_step(x[0], mem[0], loss_target[0], full,
                                                    {name: wv[name] for name, _ in SMALL})

    recv = _reduce_scatter([_restack(wgrads[name], axis).astype(BF16).reshape(N_DEV, -1, 128)
                            for name, _, axis in WSPECS])
    outs = _adam("adam_sharded", recv, *[_pack_rows([d[name] for name, _, _ in WSPECS]) for d in (wv, mv, vv)])
    res = {}
    r = 0
    for (name, shape, axis), nr in zip(WSPECS, shard_rows):
        res[name] = [o[r:r + nr].reshape(_shard_shape(shape, axis)) for o in outs]
        r += nr

    small_recv = _exchange("gather_small", _pack_small(sgrads, loss_part), per_peer=False)
    zero = jnp.zeros((), F32)
    souts = _adam("adam_small", small_recv, *[_pack_small(d, zero) for d in (wv, mv, vv)])
    loss = None
    for i, o in enumerate(souts):
        vals, extra = _unpack_small(o)
        if i == 0:
            loss = extra
        for name, _ in SMALL:
            res.setdefault(name, []).append(vals[name])

    out = [loss, grad_x[None]]
    for i in range(4):
        out.extend(res[name][i] for name in WEIGHT_ORDER)
    return tuple(out)
```

```python
import math

import jax
import jax.numpy as jnp
from jax import lax
from jax.experimental import pallas as pl
from jax.experimental.pallas import tpu as pltpu

F32 = jnp.float32
BF16 = jnp.bfloat16

N_DEV = 8
D_MODEL = 1024
DEPTH = 2
MLA_HEADS = 8
MLA_Q_LORA = 256
MLA_KV_LORA = 128
MLA_NOPE = 64
MLA_ROPE = 32
MLA_V = 64
ROPE_THETA = 10000.0
SWA_HEADS = 8
SWA_KV_HEADS = 2
SWA_HD = 64
WINDOW = 128
REL_BUCKETS = 32
REL_MAX_DIST = 128
MEM_LEN = 256
MEM_HEADS = 4
MEM_HD = 128
D_FF = 4 * D_MODEL
EPS = 1e-6
HEAD_PAD = 128
ADAM_LR = 0.001
ADAM_B1 = 0.9
ADAM_B2 = 0.999
ADAM_EPS = 1e-08
ADAM_WD = 0.01
ADAM_STEP = 10

NEG = -1e30
VMEM_LIMIT = 48 * 1024 * 1024

MM_TM = 1024
MM_TN = 1024
MM_TK = 1024
TN_T1 = 1024
TN_TN = 1024
TN_TS = 2048
MM_TK_DEEP = 2048
MM_TM_BF16 = 2048
ROW_TM = 256
MLA_TILE = 4096
MLA_TILE_BWD = 2048
MLA_BWD_VMEM_LIMIT = VMEM_LIMIT
MLA_CHUNK = 256
MLA_CHUNK_FWD = 512
MLA_QK = MLA_NOPE + MLA_ROPE
MLA_SCALE = MLA_QK ** -0.5
LOG2E = math.log2(math.e)
DEN_LANE = MLA_V
SWA_TQ = 1024
SWA_AHEAD = 1
MEM_TQ = 1024
MEM_CHUNK = 256
ADAM_TM = 1200

WSPECS = (
    ("w_in", (DEPTH, D_MODEL, 4768), 2),
    ("w_uq", (DEPTH, MLA_Q_LORA, 768), 2),
    ("w_ukv", (DEPTH, MLA_KV_LORA, 1024), 2),
    ("w_mem_kv", (DEPTH, D_MODEL, 1024), 1),
    ("w_o_mla", (DEPTH, 512, D_MODEL), 2),
    ("w_o_swa", (DEPTH, 512, D_MODEL), 2),
    ("w_o_mem", (DEPTH, 512, D_MODEL), 2),
    ("w_out", (DEPTH, D_MODEL, D_MODEL), 1),
    ("w_up", (DEPTH, D_MODEL, D_FF), 2),
    ("w_down", (DEPTH, D_FF, D_MODEL), 1),
)
SMALL = (
    ("rel_bias", (REL_BUCKETS, SWA_HEADS)),
    ("attn_norm", (DEPTH, D_MODEL)),
    ("mem_norm", (DEPTH, D_MODEL)),
    ("b_gate", (DEPTH, 3 * D_MODEL)),
    ("mla_q_norm", (DEPTH, MLA_Q_LORA)),
    ("mla_kv_norm", (DEPTH, MLA_KV_LORA)),
    ("attn_sinks", (DEPTH, SWA_HEADS)),
    ("mlp_norm", (DEPTH, D_MODEL)),
    ("final_norm", (D_MODEL,)),
)
WEIGHT_ORDER = ("rel_bias", "attn_norm", "mem_norm", "w_in", "b_gate", "mla_q_norm", "w_uq", "mla_kv_norm",
                "w_ukv", "attn_sinks", "w_mem_kv", "w_o_mla", "w_o_swa", "w_o_mem", "w_out", "mlp_norm",
                "w_up", "w_down", "final_norm")


def _cparams(*sem):
    return pltpu.CompilerParams(dimension_semantics=sem, vmem_limit_bytes=VMEM_LIMIT)


def _shard_shape(shape, axis):
    s = list(shape)
    s[axis] //= N_DEV
    return tuple(s)


def _mm(name, a, b, out_dtypes, *, epi=None, extras=(), a_fn=None, cast=BF16, precision=None,
        tm=None, tn=None, tk=None, col_sums=0):
    M, K = a.shape
    K2, N = b.shape
    assert K == K2, (name, a.shape, b.shape)
    tm = min(tm or MM_TM, M)
    tn = min(tn or MM_TN, N)
    tk = min(tk or MM_TK, K)
    assert M % tm == 0 and N % tn == 0 and K % tk == 0, (name, a.shape, b.shape, tm, tn, tk)
    assert col_sums == 0 or tn == N, (name, tn, N)
    nk = K // tk
    n_ex = len(extras)
    n_out = len(out_dtypes)

    def body(*refs):
        a_ref, b_ref = refs[0], refs[1]
        ex_refs = refs[2:2 + n_ex]
        out_refs = refs[2 + n_ex:2 + n_ex + n_out]
        av = a_ref[...]
        if a_fn is not None:
            av = a_fn(av)
        bv = b_ref[...]
        if cast is not None:
            av = av.astype(cast)
            bv = bv.astype(cast)
        part = jnp.dot(av, bv, preferred_element_type=F32, precision=precision)

        def finish(acc):
            outs = epi(acc, *[r[...] for r in ex_refs]) if epi is not None else (acc,)
            for r, o in zip(out_refs, outs[:n_out]):
                r[...] = o.astype(r.dtype)
            i = pl.program_id(0)
            for r, o in zip(refs[2 + n_ex + n_out:2 + n_ex + n_out + col_sums], outs[n_out:]):
                @pl.when(i == 0)
                def _(r=r, o=o):
                    r[...] = o

                @pl.when(i > 0)
                def _(r=r, o=o):
                    r[...] += o

        if nk == 1:
            finish(part)
        else:
            acc_ref = refs[-1]
            k = pl.program_id(2)

            @pl.when(k == 0)
            def _():
                acc_ref[...] = part

            @pl.when(k > 0)
            def _():
                acc_ref[...] += part

            @pl.when(k == nk - 1)
            def _():
                finish(acc_ref[...])

    in_specs = [pl.BlockSpec((tm, tk), lambda i, j, k: (i, k)),
                pl.BlockSpec((tk, tn), lambda i, j, k: (k, j))]
    for arr, kind in extras:
        if kind == "mn":
            in_specs.append(pl.BlockSpec((tm, tn), lambda i, j, k: (i, j)))
        elif kind == "m":
            in_specs.append(pl.BlockSpec((tm, arr.shape[1]), lambda i, j, k: (i, 0)))
        else:
            in_specs.append(pl.BlockSpec((1, tn), lambda i, j, k: (0, j)))
    outs = pl.pallas_call(
        body, name=name, grid=(M // tm, N // tn, nk),
        in_specs=in_specs,
        out_specs=([pl.BlockSpec((tm, tn), lambda i, j, k: (i, j)) for _ in out_dtypes]
                   + [pl.BlockSpec((1, tn), lambda i, j, k: (0, 0))] * col_sums),
        out_shape=([jax.ShapeDtypeStruct((M, N), dt) for dt in out_dtypes]
                   + [jax.ShapeDtypeStruct((1, N), F32)] * col_sums),
        scratch_shapes=[pltpu.VMEM((tm, tn), F32)] if nk > 1 else [],
        compiler_params=(_cparams("arbitrary", "arbitrary", "arbitrary") if col_sums
                         else _cparams("parallel", "parallel", "arbitrary")),
    )(a, b, *[arr for arr, _ in extras])
    return outs[0] if n_out + col_sums == 1 else outs


def _mm_tn(name, a, b, *, t1=None, tn=None, ts=None):
    S, K1 = a.shape
    S2, N = b.shape
    assert S == S2, (name, a.shape, b.shape)
    t1 = min(t1 or TN_T1, K1)
    tn = min(tn or TN_TN, N)
    ts = min(ts or TN_TS, S)
    assert K1 % t1 == 0 and N % tn == 0 and S % ts == 0, (name, a.shape, b.shape)

    def body(a_ref, b_ref, o_ref):
        s = pl.program_id(2)
        part = lax.dot_general(a_ref[...].astype(BF16), b_ref[...].astype(BF16),
                               (((0,), (0,)), ((), ())), preferred_element_type=F32)

        @pl.when(s == 0)
        def _():
            o_ref[...] = part

        @pl.when(s > 0)
        def _():
            o_ref[...] += part

    return pl.pallas_call(
        body, name=name, grid=(K1 // t1, N // tn, S // ts),
        in_specs=[pl.BlockSpec((ts, t1), lambda i, j, s: (s, i)),
                  pl.BlockSpec((ts, tn), lambda i, j, s: (s, j))],
        out_specs=pl.BlockSpec((t1, tn), lambda i, j, s: (i, j)),
        out_shape=jax.ShapeDtypeStruct((K1, N), F32),
        compiler_params=_cparams("parallel", "parallel", "arbitrary"),
    )(a, b)


def _rows(arr, width=None, blk=0):
    return (arr, ("rows", arr.shape[1] if width is None else width, blk))


def _full(arr):
    return (arr, ("full",))


def _rowwise(name, fn, ins, outs, *, rows, tm=None):
    tm = min(tm or ROW_TM, rows)
    assert rows % tm == 0, (name, rows, tm)
    n_in = len(ins)

    def body(*refs):
        i = pl.program_id(0)
        vals = fn(*[r[...] for r in refs[:n_in]])
        for (shape, dt, kind), r, v in zip(outs, refs[n_in:], vals):
            if kind == "rows":
                r[...] = v.astype(dt)
            else:
                @pl.when(i == 0)
                def _(r=r, v=v):
                    r[...] = v

                @pl.when(i > 0)
                def _(r=r, v=v):
                    r[...] += v

    in_specs = []
    for arr, spec in ins:
        if spec[0] == "rows":
            in_specs.append(pl.BlockSpec((tm, spec[1]), lambda i, b=spec[2]: (i, b)))
        else:
            in_specs.append(pl.BlockSpec(arr.shape, lambda i, n=arr.ndim: (0,) * n))
    out_specs = []
    for shape, dt, kind in outs:
        if kind == "rows":
            out_specs.append(pl.BlockSpec((tm, shape[1]), lambda i: (i, 0)))
        else:
            out_specs.append(pl.BlockSpec(shape, lambda i, n=len(shape): (0,) * n))
    res = pl.pallas_call(
        body, name=name, grid=(rows // tm,),
        in_specs=in_specs, out_specs=out_specs,
        out_shape=[jax.ShapeDtypeStruct(shape, dt) for shape, dt, _ in outs],
        compiler_params=_cparams("arbitrary"),
    )(*[arr for arr, _ in ins])
    return res


MEM_SCALE = MEM_HD ** -0.5
MEM_Q0 = 2
NT_DIMS = (((1,), (1,)), ((), ()))


def _head_lanes(h):
    return slice(h * HEAD_PAD, (h + 1) * HEAD_PAD)


def _mem_fwd(name, proj_b, kvm, *, tq, chunk):
    S = proj_b.shape[0]
    tq = min(tq, S)
    C = min(chunk, tq)
    tiles = [(h, c) for c in range(tq // C) for h in range(MEM_HEADS)]

    def body(q_ref, kv_ref, o_ref, lse_ref):
        def logits(h, c):
            return lax.dot_general(q_ref[c * C:(c + 1) * C, _head_lanes(h)], kv_ref[:, _head_lanes(h)], NT_DIMS,
                                   preferred_element_type=F32) * MEM_SCALE

        nxt = logits(*tiles[0])
        for n, (h, c) in enumerate(tiles):
            s = nxt
            if n + 1 < len(tiles):
                nxt = logits(*tiles[n + 1])
            rows = slice(c * C, (c + 1) * C)
            m = jnp.max(s, axis=1, keepdims=True)
            p = jnp.exp(s - m)
            l = jnp.sum(p, axis=1, keepdims=True)
            o = jnp.dot(p.astype(BF16), kv_ref[:, _head_lanes(MEM_HEADS + h)], preferred_element_type=F32) / l
            o_ref[rows, _head_lanes(h)] = o.astype(o_ref.dtype)
            lse_ref[h, rows, :] = m + jnp.log(l)

    return pl.pallas_call(
        body, name=name, grid=(S // tq,),
        in_specs=[pl.BlockSpec((tq, MEM_HEADS * HEAD_PAD), lambda i: (i, MEM_Q0)),
                  pl.BlockSpec(kvm.shape, lambda i: (0, 0))],
        out_specs=[pl.BlockSpec((tq, MEM_HEADS * HEAD_PAD), lambda i: (i, 0)),
                   pl.BlockSpec((MEM_HEADS, tq, 1), lambda i: (0, i, 0))],
        out_shape=[jax.ShapeDtypeStruct((S, MEM_HEADS * HEAD_PAD), BF16),
                   jax.ShapeDtypeStruct((MEM_HEADS, S, 1), F32)],
        compiler_params=_cparams("parallel"),
    )(proj_b, kvm)


def _mem_bwd(name, proj_b, kvm, o, do, lse, *, tq, chunk):
    S = proj_b.shape[0]
    tq = min(tq, S)
    C = min(chunk, tq)
    nq = S // tq
    tiles = [(h, c) for c in range(tq // C) for h in range(MEM_HEADS)]

    def body(q_ref, kv_ref, o_ref, do_ref, lse_ref, dq_ref, dkv_ref, acc_sc):
        i = pl.program_id(0)

        @pl.when(i == 0)
        def _():
            acc_sc[...] = jnp.zeros(acc_sc.shape, F32)

        def mats(h, c):
            rows = slice(c * C, (c + 1) * C)
            q = q_ref[rows, _head_lanes(h)]
            dov = do_ref[rows, _head_lanes(h)]
            s = lax.dot_general(q, kv_ref[:, _head_lanes(h)], NT_DIMS, preferred_element_type=F32) * MEM_SCALE
            dp = lax.dot_general(dov, kv_ref[:, _head_lanes(MEM_HEADS + h)], NT_DIMS, preferred_element_type=F32)
            return q, dov, s, dp

        nxt = mats(*tiles[0])
        for n, (h, c) in enumerate(tiles):
            q, dov, s, dp = nxt
            if n + 1 < len(tiles):
                nxt = mats(*tiles[n + 1])
            rows = slice(c * C, (c + 1) * C)
            p = jnp.exp(s - lse_ref[h, rows, :])
            delta = jnp.sum(dov.astype(F32) * o_ref[rows, _head_lanes(h)].astype(F32), axis=1, keepdims=True)
            ds = (p * (dp - delta) * MEM_SCALE).astype(BF16)
            dq_ref[rows, _head_lanes(h)] = jnp.dot(ds, kv_ref[:, _head_lanes(h)],
                                                   preferred_element_type=F32).astype(dq_ref.dtype)
            acc_sc[_head_lanes(h), :] += jnp.dot(q.T, ds, preferred_element_type=F32)
            acc_sc[_head_lanes(MEM_HEADS + h), :] += jnp.dot(dov.T, p.astype(BF16), preferred_element_type=F32)

        @pl.when(i == nq - 1)
        def _():
            dkv_ref[...] = acc_sc[...].T

    qblk = pl.BlockSpec((tq, MEM_HEADS * HEAD_PAD), lambda i: (i, 0))
    return pl.pallas_call(
        body, name=name, grid=(nq,),
        in_specs=[pl.BlockSpec((tq, MEM_HEADS * HEAD_PAD), lambda i: (i, MEM_Q0)),
                  pl.BlockSpec(kvm.shape, lambda i: (0, 0)), qblk, qblk,
                  pl.BlockSpec((MEM_HEADS, tq, 1), lambda i: (0, i, 0))],
        out_specs=[qblk, pl.BlockSpec(kvm.shape, lambda i: (0, 0))],
        out_shape=[jax.ShapeDtypeStruct((S, MEM_HEADS * HEAD_PAD), BF16), jax.ShapeDtypeStruct(kvm.shape, F32)],
        scratch_shapes=[pltpu.VMEM((kvm.shape[1], kvm.shape[0]), F32)],
        compiler_params=_cparams("arbitrary"),
    )(proj_b, kvm, o, do, lse)


def _causal_fwd(name, q_arr, k_arr, v_arr, *, heads, tile, chunk):
    S = q_arr.shape[0]
    T = min(tile, S)
    C = min(chunk, T)
    nt = S // T
    nc = T // C

    pairs = [(qi, kk) for qi in range(nt) for kk in range(qi + 1)]
    q_tab = jnp.asarray([p[0] for p in pairs], jnp.int32)
    k_tab = jnp.asarray([p[1] for p in pairs], jnp.int32)

    def body(qt_ref, kt_ref, q_ref, k_ref, v_ref, o_ref, lse_ref, m_sc, acc_sc):
        t = pl.program_id(1)
        qi = qt_ref[t]
        kk = kt_ref[t]

        @pl.when(kk == 0)
        def _():
            m_sc[...] = jnp.full(m_sc.shape, NEG, F32)
            acc_sc[...] = jnp.zeros(acc_sc.shape, F32)

        def logits(c, ncols, masked):
            s = lax.dot_general(q_ref[pl.ds(c * C, C), :], k_ref[0:ncols, :], (((1,), (1,)), ((), ())),
                                preferred_element_type=F32)
            if masked:
                r = c * C + lax.broadcasted_iota(jnp.int32, (C, ncols), 0)
                cidx = lax.broadcasted_iota(jnp.int32, (C, ncols), 1)
                s = jnp.where(cidx <= r, s, NEG)
            return s

        def update(c, ncols, s):
            rows = pl.ds(c * C, C)
            m_prev = m_sc[rows, :]
            m_new = jnp.maximum(m_prev, jnp.max(s, axis=1, keepdims=True))
            p = jnp.exp2(s - m_new).astype(BF16)
            acc_sc[rows, :] = jnp.exp2(m_prev - m_new) * acc_sc[rows, :] + jnp.dot(
                p, v_ref[0:ncols, :], preferred_element_type=F32)
            m_sc[rows, :] = m_new

        def tile_step(ncols_of, masked):
            s = logits(0, ncols_of(0), masked)
            for c in range(nc):
                s_next = logits(c + 1, ncols_of(c + 1), masked) if c + 1 < nc else None
                update(c, ncols_of(c), s)
                s = s_next

        @pl.when(kk < qi)
        def _():
            tile_step(lambda c: T, False)

        @pl.when(kk == qi)
        def _():
            tile_step(lambda c: (c + 1) * C, True)

        @pl.when(kk == qi)
        def _():
            acc = acc_sc[...]
            l = acc[:, DEN_LANE:DEN_LANE + 1]
            o_ref[...] = (acc / l).astype(o_ref.dtype)
            lse_ref[0] = m_sc[...] + jnp.log2(l)

    grid_spec = pltpu.PrefetchScalarGridSpec(
        num_scalar_prefetch=2, grid=(heads, len(pairs)),
        in_specs=[pl.BlockSpec((T, HEAD_PAD), lambda h, t, qt, kt: (qt[t], h)),
                  pl.BlockSpec((T, HEAD_PAD), lambda h, t, qt, kt: (kt[t], h)),
                  pl.BlockSpec((T, HEAD_PAD), lambda h, t, qt, kt: (kt[t], h))],
        out_specs=[pl.BlockSpec((T, HEAD_PAD), lambda h, t, qt, kt: (qt[t], h)),
                   pl.BlockSpec((1, T, 1), lambda h, t, qt, kt: (h, qt[t], 0))],
        scratch_shapes=[pltpu.VMEM((T, 1), F32), pltpu.VMEM((T, HEAD_PAD), F32)])
    return pl.pallas_call(
        body, name=name, grid_spec=grid_spec,
        out_shape=[jax.ShapeDtypeStruct((S, heads * HEAD_PAD), BF16),
                   jax.ShapeDtypeStruct((heads, S, 1), F32)],
        compiler_params=_cparams("parallel", "arbitrary"),
    )(q_tab, k_tab, q_arr, k_arr, v_arr)


def _with_neg_delta(do, o):
    lane = lax.broadcasted_iota(jnp.int32, (1, HEAD_PAD), 1)
    outs = []
    for h in range(do.shape[1] // HEAD_PAD):
        a = do[:, _head_lanes(h)]
        nd = -jnp.sum(a * o[:, _head_lanes(h)].astype(F32), axis=1, keepdims=True)
        hi = nd.astype(BF16).astype(F32)
        a = jnp.where(lane == DEN_LANE, hi, a)
        outs.append(jnp.where(lane == DEN_LANE + 1, nd - hi, a))
    return jnp.concatenate(outs, axis=1)


def _causal_bwd(name, q_arr, k_arr, v_arr, do_arr, lse, *, heads, tile, chunk):
    S = q_arr.shape[0]
    T = min(tile, S)
    C = min(chunk, T)
    nt = S // T
    nc = T // C

    pairs = [(kj, qq) for kj in range(nt) for qq in range(kj, nt)]
    k_tab = jnp.asarray([p[0] for p in pairs], jnp.int32)
    q_tab = jnp.asarray([p[1] for p in pairs], jnp.int32)

    def body(kt_ref, qt_ref, q_ref, k_ref, v_ref, do_ref, lse_ref, dq_ref, dk_ref, dv_ref, dk_sc, dv_sc):
        t = pl.program_id(1)
        kj = kt_ref[t]
        qq = qt_ref[t]
        qb = qq

        @pl.when(t == 0)
        def _():
            dq_ref[...] = jnp.zeros(dq_ref.shape, F32)

        @pl.when(qq == kj)
        def _():
            dk_sc[...] = jnp.zeros(dk_sc.shape, F32)
            dv_sc[...] = jnp.zeros(dv_sc.shape, F32)

        def logits(c, ncols, masked):
            rows = pl.ds(c * C, C)
            s = lax.dot_general(q_ref[rows, :], k_ref[0:ncols, :], (((1,), (1,)), ((), ())),
                                preferred_element_type=F32)
            if masked:
                r = c * C + lax.broadcasted_iota(jnp.int32, (C, ncols), 0)
                cidx = lax.broadcasted_iota(jnp.int32, (C, ncols), 1)
                s = jnp.where(cidx <= r, s, NEG)
            dp = lax.dot_general(do_ref[rows, :], v_ref[0:ncols, :], (((1,), (1,)), ((), ())),
                                 preferred_element_type=F32)
            return s, dp

        def update(c, ncols, s, dp):
            rows = pl.ds(c * C, C)
            p = jnp.exp2(s - lse_ref[0, rows, :])
            ds = (p * dp).astype(BF16)
            dv_sc[:, 0:ncols] += jnp.dot(do_ref[rows, 0:MLA_V].T, p.astype(BF16), preferred_element_type=F32)
            dk_sc[:, 0:ncols] += jnp.dot(q_ref[rows, 0:MLA_QK].T, ds, preferred_element_type=F32)
            row0 = pl.multiple_of(qb * T + c * C, C)
            dq_ref[pl.ds(row0, C), :] += jnp.dot(ds, k_ref[0:ncols, :], preferred_element_type=F32)

        def tile_step(ncols_of, masked):
            cur = logits(0, ncols_of(0), masked)
            for c in range(nc):
                nxt = logits(c + 1, ncols_of(c + 1), masked) if c + 1 < nc else None
                update(c, ncols_of(c), *cur)
                cur = nxt

        @pl.when(qq > kj)
        def _():
            tile_step(lambda c: T, False)

        @pl.when(qq == kj)
        def _():
            tile_step(lambda c: (c + 1) * C, True)

        @pl.when(qq == nt - 1)
        def _():
            dk_ref[:, 0:MLA_QK] = dk_sc[...].T * math.log(2.0)
            dk_ref[:, MLA_QK:] = jnp.zeros((T, HEAD_PAD - MLA_QK), F32)
            dv_ref[:, 0:MLA_V] = dv_sc[...].T
            dv_ref[:, MLA_V:] = jnp.zeros((T, HEAD_PAD - MLA_V), F32)

    qrow = pl.BlockSpec((T, HEAD_PAD), lambda h, t, kt, qt: (qt[t], h))
    krow = pl.BlockSpec((T, HEAD_PAD), lambda h, t, kt, qt: (kt[t], h))
    qcol = pl.BlockSpec((1, T, 1), lambda h, t, kt, qt: (h, qt[t], 0))
    grid_spec = pltpu.PrefetchScalarGridSpec(
        num_scalar_prefetch=2, grid=(heads, len(pairs)),
        in_specs=[qrow, krow, krow, qrow, qcol],
        out_specs=[pl.BlockSpec((S, HEAD_PAD), lambda h, t, kt, qt: (0, h)), krow, krow],
        scratch_shapes=[pltpu.VMEM((MLA_QK, T), F32), pltpu.VMEM((MLA_V, T), F32)])
    return pl.pallas_call(
        body, name=name, grid_spec=grid_spec,
        out_shape=[jax.ShapeDtypeStruct((S, heads * HEAD_PAD), F32)] * 3,
        compiler_params=pltpu.CompilerParams(dimension_semantics=("arbitrary", "arbitrary"),
                                             vmem_limit_bytes=MLA_BWD_VMEM_LIMIT),
    )(k_tab, q_tab, q_arr, k_arr, v_arr, do_arr, lse)


SWA_R = SWA_HEADS // SWA_KV_HEADS
SWA_SCALE = SWA_HD ** -0.5
SWA_Q0, SWA_K0, SWA_V0 = 0, 12, 14


def _swa_specs(tq):
    nsb = tq // WINDOW
    return [
        pl.BlockSpec((tq, SWA_R * HEAD_PAD), lambda g, i: (i, g)),
        pl.BlockSpec((tq, HEAD_PAD), lambda g, i: (i, SWA_K0 + g)),
        pl.BlockSpec((WINDOW, HEAD_PAD), lambda g, i: (jnp.maximum(nsb * i - 1, 0), SWA_K0 + g)),
        pl.BlockSpec((tq, HEAD_PAD), lambda g, i: (i, SWA_V0 + g)),
        pl.BlockSpec((WINDOW, HEAD_PAD), lambda g, i: (jnp.maximum(nsb * i - 1, 0), SWA_V0 + g)),
        pl.BlockSpec((SWA_R, WINDOW, 2 * WINDOW), lambda g, i: (g, 0, 0)),
        pl.BlockSpec((SWA_R, 8, 128), lambda g, i: (g, 0, 0)),
    ]


def _swa_block(i, sb, q_ref, kc_ref, kp_ref, vc_ref, vp_ref, bias, sink):
    rows = slice(sb * WINDOW, (sb + 1) * WINDOW)
    qs = jnp.concatenate([q_ref[rows, hh * HEAD_PAD:(hh + 1) * HEAD_PAD] for hh in range(SWA_R)], axis=0)
    if sb == 0:
        kp, vp = kp_ref[...], vp_ref[...]
    else:
        prev = slice((sb - 1) * WINDOW, sb * WINDOW)
        kp, vp = kc_ref[prev, :], vc_ref[prev, :]
    kk = jnp.concatenate([kp, kc_ref[rows, :]], axis=0)
    vv = jnp.concatenate([vp, vc_ref[rows, :]], axis=0)
    s = lax.dot_general(qs, kk, (((1,), (1,)), ((), ())), preferred_element_type=F32) * SWA_SCALE + bias
    if sb == 0:
        col = lax.broadcasted_iota(jnp.int32, (1, 2 * WINDOW), 1)
        s = s + jnp.where((col < WINDOW) & (i == 0), NEG, 0.0)
    return rows, qs, kk, vv, s


def _stack_heads(ref, rows, lead=None):
    if lead is None:
        return jnp.concatenate([ref[rows, hh * HEAD_PAD:(hh + 1) * HEAD_PAD] for hh in range(SWA_R)], axis=0)
    return jnp.concatenate([ref[hh, rows, :] for hh in range(SWA_R)], axis=0)


def _swa_fwd(name, proj_b, bias, sinks, *, tq):
    S = proj_b.shape[0]
    tq = min(tq, S)
    nsb = tq // WINDOW

    def body(q_ref, kc_ref, kp_ref, vc_ref, vp_ref, bias_ref, sink_ref, o_ref, lse_ref):
        i = pl.program_id(1)
        bias_v = bias_ref[...].reshape(SWA_R * WINDOW, 2 * WINDOW)
        sink = jnp.concatenate([jnp.zeros((WINDOW, 1), F32) + sink_ref[hh, 0:1, 0:1] for hh in range(SWA_R)], axis=0)
        ahead = [_swa_block(i, sb, q_ref, kc_ref, kp_ref, vc_ref, vp_ref, bias_v, sink)
                 for sb in range(min(SWA_AHEAD, nsb))]
        for sb in range(nsb):
            rows, _, _, vv, s = ahead.pop(0)
            if sb + SWA_AHEAD < nsb:
                ahead.append(_swa_block(i, sb + SWA_AHEAD, q_ref, kc_ref, kp_ref, vc_ref, vp_ref, bias_v, sink))
            m = jnp.maximum(jnp.max(s, axis=1, keepdims=True), sink)
            p = jnp.exp(s - m)
            l = jnp.sum(p, axis=1, keepdims=True) + jnp.exp(sink - m)
            o = jnp.dot(p.astype(BF16), vv, preferred_element_type=F32) / l
            lse_v = m + jnp.log(l)
            for hh in range(SWA_R):
                o_ref[rows, hh * HEAD_PAD:(hh + 1) * HEAD_PAD] = o[hh * WINDOW:(hh + 1) * WINDOW].astype(o_ref.dtype)
                lse_ref[hh, rows, :] = lse_v[hh * WINDOW:(hh + 1) * WINDOW]

    return pl.pallas_call(
        body, name=name, grid=(SWA_KV_HEADS, S // tq),
        in_specs=_swa_specs(tq),
        out_specs=[pl.BlockSpec((tq, SWA_R * HEAD_PAD), lambda g, i: (i, g)),
                   pl.BlockSpec((SWA_R, tq, 1), lambda g, i: (g, i, 0))],
        out_shape=[jax.ShapeDtypeStruct((S, SWA_HEADS * HEAD_PAD), BF16),
                   jax.ShapeDtypeStruct((SWA_HEADS, S, 1), F32)],
        compiler_params=_cparams("parallel", "parallel"),
    )(proj_b, proj_b, proj_b, proj_b, proj_b, bias, sinks)


def _swa_bwd(name, proj_b, bias, sinks, o, do, lse, *, tq):
    S = proj_b.shape[0]
    tq = min(tq, S)
    nsb = tq // WINDOW
    nq = S // tq

    def body(q_ref, kc_ref, kp_ref, vc_ref, vp_ref, bias_ref, sink_ref, o_ref, do_ref, lse_ref,
             dq_ref, dk_ref, dv_ref, dke_ref, dve_ref, dbias_ref, dsink_ref):
        i = pl.program_id(1)

        @pl.when(i == 0)
        def _():
            dbias_ref[...] = jnp.zeros(dbias_ref.shape, F32)
            dsink_ref[...] = jnp.zeros(dsink_ref.shape, F32)

        bias_v = bias_ref[...].reshape(SWA_R * WINDOW, 2 * WINDOW)
        sink = jnp.concatenate([jnp.zeros((WINDOW, 1), F32) + sink_ref[hh, 0:1, 0:1] for hh in range(SWA_R)], axis=0)
        dk_own, dv_own, dk_prev, dv_prev = [], [], [], []
        dbias_acc = jnp.zeros((SWA_R * WINDOW, 2 * WINDOW), F32)
        def block(sb):
            rows, qs, kk, vv, s = _swa_block(i, sb, q_ref, kc_ref, kp_ref, vc_ref, vp_ref, bias_v, sink)
            do_s = _stack_heads(do_ref, rows)
            dp = lax.dot_general(do_s, vv, (((1,), (1,)), ((), ())), preferred_element_type=F32)
            return rows, qs, kk, do_s, s, dp

        nxt = block(0)
        for sb in range(nsb):
            rows, qs, kk, do_s, s, dp = nxt
            if sb + 1 < nsb:
                nxt = block(sb + 1)
            lse_v = _stack_heads(lse_ref, rows, lead=True)
            delta = jnp.sum(do_s.astype(F32) * _stack_heads(o_ref, rows).astype(F32), axis=1, keepdims=True)
            p = jnp.exp(s - lse_v)
            dsp = p * (dp - delta)
            dbias_acc = dbias_acc + dsp
            ds = (dsp * SWA_SCALE).astype(BF16)
            dq = jnp.dot(ds, kk, preferred_element_type=F32)
            dkk = jnp.dot(qs.T, ds, preferred_element_type=F32)
            dvv = jnp.dot(do_s.T, p.astype(BF16), preferred_element_type=F32)
            dk_prev.append(dkk[:, :WINDOW].T)
            dk_own.append(dkk[:, WINDOW:].T)
            dv_prev.append(dvv[:, :WINDOW].T)
            dv_own.append(dvv[:, WINDOW:].T)
            psink = jnp.exp(sink - lse_v) * delta
            for hh in range(SWA_R):
                hrows = slice(hh * WINDOW, (hh + 1) * WINDOW)
                dq_ref[rows, hh * HEAD_PAD:(hh + 1) * HEAD_PAD] = dq[hrows].astype(dq_ref.dtype)
                dsink_ref[hh] += jnp.zeros((8, 128), F32) - jnp.sum(psink[hrows])
        dbias_ref[...] += dbias_acc.reshape(SWA_R, WINDOW, 2 * WINDOW)
        for sb in range(nsb):
            rows = slice(sb * WINDOW, (sb + 1) * WINDOW)
            if sb + 1 < nsb:
                dk_ref[rows, :] = dk_own[sb] + dk_prev[sb + 1]
                dv_ref[rows, :] = dv_own[sb] + dv_prev[sb + 1]
            else:
                dk_ref[rows, :] = dk_own[sb]
                dv_ref[rows, :] = dv_own[sb]
        dke_ref[...] = dk_prev[0]
        dve_ref[...] = dv_prev[0]

    in_specs = _swa_specs(tq) + [
        pl.BlockSpec((tq, SWA_R * HEAD_PAD), lambda g, i: (i, g)),
        pl.BlockSpec((tq, SWA_R * HEAD_PAD), lambda g, i: (i, g)),
        pl.BlockSpec((SWA_R, tq, 1), lambda g, i: (g, i, 0)),
    ]
    kv_blk = pl.BlockSpec((tq, HEAD_PAD), lambda g, i: (i, g))
    edge_blk = pl.BlockSpec((WINDOW, HEAD_PAD), lambda g, i: (i, g))
    return pl.pallas_call(
        body, name=name, grid=(SWA_KV_HEADS, nq),
        in_specs=in_specs,
        out_specs=[pl.BlockSpec((tq, SWA_R * HEAD_PAD), lambda g, i: (i, g)), kv_blk, kv_blk, edge_blk, edge_blk,
                   pl.BlockSpec((SWA_R, WINDOW, 2 * WINDOW), lambda g, i: (g, 0, 0)),
                   pl.BlockSpec((SWA_R, 8, 128), lambda g, i: (g, 0, 0))],
        out_shape=[jax.ShapeDtypeStruct((S, SWA_HEADS * HEAD_PAD), BF16),
                   jax.ShapeDtypeStruct((S, SWA_KV_HEADS * HEAD_PAD), F32),
                   jax.ShapeDtypeStruct((S, SWA_KV_HEADS * HEAD_PAD), F32),
                   jax.ShapeDtypeStruct((nq * WINDOW, SWA_KV_HEADS * HEAD_PAD), F32),
                   jax.ShapeDtypeStruct((nq * WINDOW, SWA_KV_HEADS * HEAD_PAD), F32),
                   jax.ShapeDtypeStruct((SWA_HEADS, WINDOW, 2 * WINDOW), F32),
                   jax.ShapeDtypeStruct((SWA_HEADS, 8, 128), F32)],
        compiler_params=_cparams("arbitrary", "arbitrary"),
    )(proj_b, proj_b, proj_b, proj_b, proj_b, bias, sinks, o, do, lse)


def _dproj_b(name, dq_swa, dq_mem, dk, dv, dk_edge, dv_edge, *, tq):
    S = dq_swa.shape[0]
    tq = min(tq, S)
    nq = S // tq

    def body(dqs_ref, dqm_ref, dk_ref, dv_ref, dke_ref, dve_ref, o_ref):
        i = pl.program_id(0)
        o_ref[:, 0:1024] = dqs_ref[...]
        o_ref[:, 1024:1536] = dqm_ref[...].astype(o_ref.dtype)
        o_ref[:, 1536:1792] = dk_ref[...].astype(o_ref.dtype)
        o_ref[:, 1792:2048] = dv_ref[...].astype(o_ref.dtype)

        @pl.when(i < nq - 1)
        def _():
            last = slice(tq - WINDOW, tq)
            o_ref[last, 1536:1792] = (dk_ref[last, :] + dke_ref[...]).astype(o_ref.dtype)
            o_ref[last, 1792:2048] = (dv_ref[last, :] + dve_ref[...]).astype(o_ref.dtype)

    edge = pl.BlockSpec((WINDOW, SWA_KV_HEADS * HEAD_PAD), lambda i: (jnp.minimum(i + 1, nq - 1), 0))
    return pl.pallas_call(
        body, name=name, grid=(nq,),
        in_specs=[pl.BlockSpec((tq, 1024), lambda i: (i, 0)), pl.BlockSpec((tq, 512), lambda i: (i, 0)),
                  pl.BlockSpec((tq, 256), lambda i: (i, 0)), pl.BlockSpec((tq, 256), lambda i: (i, 0)), edge, edge],
        out_specs=pl.BlockSpec((tq, 2048), lambda i: (i, 0)),
        out_shape=jax.ShapeDtypeStruct((S, 2048), BF16),
        compiler_params=_cparams("parallel"),
    )(dq_swa, dq_mem, dk, dv, dk_edge, dv_edge)


def _gather_forwarded(name, block):
    def body(x_ref, out_ref, send_sems, recv_sems, local_sem):
        x, y, c = lax.axis_index("x"), lax.axis_index("y"), lax.axis_index("c")
        me, sibling = (x, y, c), (x, y, 1 - c)
        chips = [(1 - x, y), (x, 1 - y), (1 - x, 1 - y)]

        def slot(px, py, pc):
            return out_ref.at[4 * px + 2 * py + pc]

        def copy(k, blk, to, src=None):
            return pltpu.make_async_remote_copy(
                src_ref=slot(*blk) if src is None else src, dst_ref=slot(*blk),
                send_sem=send_sems.at[k], recv_sem=recv_sems.at[k],
                device_id=to, device_id_type=pl.DeviceIdType.MESH)

        mine = pltpu.make_async_copy(x_ref, slot(*me), local_sem)
        mine.start()
        first = [copy(0, me, sibling, src=x_ref)]
        first += [copy(1 + j, me, (*chip, c), src=x_ref) for j, chip in enumerate(chips)]
        for cp in first:
            cp.start()
        passed = [copy(4 + j, (*chip, c), sibling) for j, chip in enumerate(chips)]
        for j, chip in enumerate(chips):
            copy(1 + j, (*chip, c), me).wait_recv()
            passed[j].start()
        copy(0, sibling, me).wait_recv()
        for j, chip in enumerate(chips):
            copy(4 + j, (*chip, 1 - c), me).wait_recv()
        for cp in first + passed:
            cp.wait_send()
        mine.wait()

    return pl.pallas_call(
        body, name=name,
        in_specs=[pl.BlockSpec(memory_space=pl.ANY)],
        out_specs=pl.BlockSpec(memory_space=pl.ANY),
        out_shape=jax.ShapeDtypeStruct((N_DEV,) + tuple(block.shape), block.dtype),
        scratch_shapes=[pltpu.SemaphoreType.DMA((N_DEV - 1,)), pltpu.SemaphoreType.DMA((N_DEV - 1,)),
                        pltpu.SemaphoreType.DMA(())],
    )(block)


def _sibling_swap(name, block):
    def body(x_ref, out_ref, send_sem, recv_sem):
        x, y, c = lax.axis_index("x"), lax.axis_index("y"), lax.axis_index("c")
        cp = pltpu.make_async_remote_copy(src_ref=x_ref, dst_ref=out_ref, send_sem=send_sem, recv_sem=recv_sem,
                                          device_id=(x, y, 1 - c), device_id_type=pl.DeviceIdType.MESH)
        cp.start()
        cp.wait()

    return pl.pallas_call(
        body, name=name,
        in_specs=[pl.BlockSpec(memory_space=pl.ANY)],
        out_specs=pl.BlockSpec(memory_space=pl.ANY),
        out_shape=jax.ShapeDtypeStruct(block.shape, block.dtype),
        scratch_shapes=[pltpu.SemaphoreType.DMA(()), pltpu.SemaphoreType.DMA(())],
    )(block)


def _chip_exchange(name, send):
    def body(send_ref, recv_ref, send_sems, recv_sems, local_sem):
        x, y, c = lax.axis_index("x"), lax.axis_index("y"), lax.axis_index("c")
        me = 2 * x + y
        own = pltpu.make_async_copy(send_ref.at[me], recv_ref.at[me], local_sem)
        own.start()
        copies = []
        for k in range(1, 4):
            px = 1 - x if (k >> 1) & 1 else x
            py = 1 - y if k & 1 else y
            peer = 2 * px + py
            out = pltpu.make_async_remote_copy(
                src_ref=send_ref.at[peer], dst_ref=recv_ref.at[me],
                send_sem=send_sems.at[k - 1], recv_sem=recv_sems.at[k - 1],
                device_id=(px, py, c), device_id_type=pl.DeviceIdType.MESH)
            out.start()
            back = pltpu.make_async_remote_copy(
                src_ref=send_ref.at[me], dst_ref=recv_ref.at[peer],
                send_sem=send_sems.at[k - 1], recv_sem=recv_sems.at[k - 1],
                device_id=(px, py, c), device_id_type=pl.DeviceIdType.MESH)
            copies.append((out, back))
        for out, back in copies:
            out.wait_send()
            back.wait_recv()
        own.wait()

    return pl.pallas_call(
        body, name=name,
        in_specs=[pl.BlockSpec(memory_space=pl.ANY)],
        out_specs=pl.BlockSpec(memory_space=pl.ANY),
        out_shape=jax.ShapeDtypeStruct(send.shape, send.dtype),
        scratch_shapes=[pltpu.SemaphoreType.DMA((3,)), pltpu.SemaphoreType.DMA((3,)), pltpu.SemaphoreType.DMA(())],
    )(send)


def _reduce_scatter(parts):
    lanes = 128
    c = lax.axis_index("c")

    def core_half(core):
        return jnp.concatenate(
            [lax.dynamic_index_in_dim(p.reshape(4, 2, p.shape[1], lanes), core, axis=1, keepdims=False)
             for p in parts], axis=1)

    mine = core_half(c)
    rows = mine.shape[1]
    mine = mine.reshape(4 * rows, lanes)
    theirs = core_half(1 - c).reshape(4 * rows, lanes)
    from_sibling = _sibling_swap("grads_to_sibling", theirs)
    tm = max(t for t in range(16, ADAM_TM + 1, 16) if rows % t == 0)
    chip_sum = _rowwise("grads_chip_sum", lambda a, b: (a.astype(F32) + b.astype(F32),),
                        [_rows(mine), _rows(from_sibling)], [((4 * rows, lanes), BF16, "rows")],
                        rows=4 * rows, tm=tm)[0]
    return _chip_exchange("scatter_grads", chip_sum.reshape(4, rows, lanes))


def _adam(name, recv, w, m, v, *, tm=None):
    R = w.shape[0]
    n_parts = recv.shape[0]
    tm = max(t for t in range(8, min(tm or ADAM_TM, R) + 1, 8) if R % t == 0)
    c1 = 1.0 / (1.0 - ADAM_B1 ** ADAM_STEP)
    c2 = 1.0 / (1.0 - ADAM_B2 ** ADAM_STEP)

    def body(r_ref, w_ref, m_ref, v_ref, g_ref, d_ref, nm_ref, nv_ref):
        g = r_ref[0].astype(F32)
        for j in range(1, n_parts):
            g = g + r_ref[j].astype(F32)
        wv = w_ref[...]
        nm = ADAM_B1 * m_ref[...] + (1.0 - ADAM_B1) * g
        nv = ADAM_B2 * v_ref[...] + (1.0 - ADAM_B2) * (g * g)
        m_hat = nm * c1
        v_hat = nv * c2
        g_ref[...] = g
        d_ref[...] = -ADAM_LR * (m_hat / (jnp.sqrt(v_hat) + ADAM_EPS) + ADAM_WD * wv)
        nm_ref[...] = nm
        nv_ref[...] = nv

    row = pl.BlockSpec((tm, 128), lambda i: (i, 0))
    return pl.pallas_call(
        body, name=name, grid=(R // tm,),
        in_specs=[pl.BlockSpec((n_parts, tm, 128), lambda i: (0, i, 0)), row, row, row],
        out_specs=[row, row, row, row],
        out_shape=[jax.ShapeDtypeStruct((R, 128), F32)] * 4,
        compiler_params=_cparams("parallel"),
    )(recv, w, m, v)


def _pack_rows(arrs):
    return jnp.concatenate([a.reshape(-1, 128) for a in arrs], axis=0)


def _unstack(g, shape, axis):
    t = jnp.moveaxis(g, 0, axis)
    return t.reshape(shape)


def _restack(full, axis):
    s = full.shape
    t = full.reshape(s[:axis] + (N_DEV, s[axis] // N_DEV) + s[axis + 1:])
    return jnp.moveaxis(t, axis, 0)


def _pad_heads(w, heads, hd, axis):
    s = w.shape
    t = w.reshape(s[:axis] + (heads, hd) + s[axis + 1:])
    pad = [(0, 0)] * t.ndim
    pad[axis + 1] = (0, HEAD_PAD - hd)
    t = jnp.pad(t, pad)
    return t.reshape(s[:axis] + (heads * HEAD_PAD,) + s[axis + 1:])


def _unpad_heads(w, heads, hd, axis):
    s = w.shape
    t = w.reshape(s[:axis] + (heads, HEAD_PAD) + s[axis + 1:])
    t = lax.slice_in_dim(t, 0, hd, axis=axis + 1)
    return t.reshape(s[:axis] + (heads * hd,) + s[axis + 1:])


def _layer_weights(full, l):
    w_in = full["w_in"][l]
    cq, kva, qs, ks, vs, qm, gates = (w_in[:, 0:256], w_in[:, 256:416], w_in[:, 416:928], w_in[:, 928:1056],
                                       w_in[:, 1056:1184], w_in[:, 1184:1696], w_in[:, 1696:4768])
    wa = jnp.concatenate([gates, cq, jnp.pad(kva, ((0, 0), (0, 96)))], axis=1)
    wb = jnp.concatenate([_pad_heads(qs, SWA_HEADS, SWA_HD, 1), qm, _pad_heads(ks, SWA_KV_HEADS, SWA_HD, 1),
                          _pad_heads(vs, SWA_KV_HEADS, SWA_HD, 1)], axis=1)
    wuq = _pad_heads(full["w_uq"][l], MLA_HEADS, MLA_NOPE + MLA_ROPE, 1)
    ukv = full["w_ukv"][l].reshape(MLA_KV_LORA, MLA_HEADS, MLA_NOPE + MLA_V)
    wuk = _pad_heads(ukv[:, :, :MLA_NOPE].reshape(MLA_KV_LORA, -1), MLA_HEADS, MLA_NOPE, 1)
    wuv = _pad_heads(ukv[:, :, MLA_NOPE:].reshape(MLA_KV_LORA, -1), MLA_HEADS, MLA_V, 1)
    wo_mla = _pad_heads(full["w_o_mla"][l], MLA_HEADS, MLA_V, 0)
    wo_swa = _pad_heads(full["w_o_swa"][l], SWA_HEADS, SWA_HD, 0)
    wo_mem = full["w_o_mem"][l]
    w = dict(wag=wa[:, :3072], wat=wa[:, 3072:], wb=wb, wuq=wuq, wuk=wuk, wuv=wuv, wo_mla=wo_mla, wo_swa=wo_swa,
             wo_mem=wo_mem, wmem=full["w_mem_kv"][l], wout=full["w_out"][l], wup=full["w_up"][l],
             wdown=full["w_down"][l])
    w.update({k + "_t": v.T for k, v in w.items()})
    return w


def _layer_weight_grads(g):
    dwa_g, dwa_t, dwb = g["wag"], g["wat"], g["wb"]
    d_in = jnp.concatenate([
        dwa_t[:, 0:256], dwa_t[:, 256:416],
        _unpad_heads(dwb[:, 0:1024], SWA_HEADS, SWA_HD, 1),
        _unpad_heads(dwb[:, 1536:1792], SWA_KV_HEADS, SWA_HD, 1),
        _unpad_heads(dwb[:, 1792:2048], SWA_KV_HEADS, SWA_HD, 1),
        dwb[:, 1024:1536], dwa_g], axis=1)
    duk = _unpad_heads(g["wuk"], MLA_HEADS, MLA_NOPE, 1).reshape(MLA_KV_LORA, MLA_HEADS, MLA_NOPE)
    duv = _unpad_heads(g["wuv"], MLA_HEADS, MLA_V, 1).reshape(MLA_KV_LORA, MLA_HEADS, MLA_V)
    return dict(
        w_in=d_in,
        w_uq=_unpad_heads(g["wuq"], MLA_HEADS, MLA_NOPE + MLA_ROPE, 1),
        w_ukv=jnp.concatenate([duk, duv], axis=2).reshape(MLA_KV_LORA, -1),
        w_mem_kv=g["wmem"],
        w_o_mla=_unpad_heads(g["wo_mla"], MLA_HEADS, MLA_V, 0),
        w_o_swa=_unpad_heads(g["wo_swa"], SWA_HEADS, SWA_HD, 0),
        w_o_mem=g["wo_mem"], w_out=g["wout"], w_up=g["wup"], w_down=g["wdown"])


def _rope_tables(S):
    pos = jnp.arange(S, dtype=F32)
    inv = 1.0 / (ROPE_THETA ** (jnp.arange(0, MLA_ROPE, 2, dtype=F32) / MLA_ROPE))
    ang = pos[:, None] * inv[None, :]
    cos, sin = jnp.cos(ang), jnp.sin(ang)
    z16 = jnp.zeros((S, 16), F32)
    z32 = jnp.zeros((S, 32), F32)
    c = jnp.concatenate([jnp.ones((S, 64), F32), cos, cos, z32], axis=1)
    ck = jnp.concatenate([jnp.zeros((S, 64), F32), cos, cos, z32], axis=1)
    s1 = jnp.concatenate([jnp.zeros((S, 80), F32), sin, z32], axis=1)
    s2 = jnp.concatenate([jnp.zeros((S, 64), F32), -sin, z16, z32], axis=1)
    return c, ck, s1, s2


def _t5_bucket(dist):
    n = jnp.maximum(dist, 0)
    max_exact = REL_BUCKETS // 2
    nf = jnp.maximum(n, 1).astype(F32)
    large = max_exact + (jnp.log(nf / max_exact) / math.log(REL_MAX_DIST / max_exact)
                         * (REL_BUCKETS - max_exact)).astype(jnp.int32)
    large = jnp.minimum(large, REL_BUCKETS - 1)
    return jnp.where(n < max_exact, n, large)


def _bias_onehot():
    qi = jnp.arange(WINDOW)[:, None]
    kj = jnp.arange(2 * WINDOW)[None, :]
    dist = qi + WINDOW - kj
    valid = (dist >= 0) & (dist < WINDOW)
    bucket = _t5_bucket(dist)
    onehot = (bucket[None] == jnp.arange(REL_BUCKETS)[:, None, None]) & valid[None]
    return (onehot.reshape(REL_BUCKETS, -1).astype(F32),
            jnp.where(valid, 0.0, NEG).astype(F32).reshape(1, -1))


def _rstd(x):
    return lax.rsqrt(jnp.mean(x * x, axis=-1, keepdims=True) + EPS)


def _norm_bwd(dh, x, g):
    r = _rstd(x)
    xh = x * r
    w = dh * g
    dx = r * (w - xh * jnp.mean(w * xh, axis=-1, keepdims=True))
    return dx, jnp.sum(dh * xh, axis=0, keepdims=True)


def _tile_lanes(t, n):
    return jnp.tile(t, (1, n // t.shape[1])) if n != t.shape[1] else t


def _rope_fwd(a, c, s1, s2):
    n = a.shape[1]
    return (a * _tile_lanes(c, n) + pltpu.roll(a, 16, 1) * _tile_lanes(s1, n)
            + pltpu.roll(a, n - 16, 1) * _tile_lanes(s2, n))


def _rope_bwd(d, c, s1, s2):
    n = d.shape[1]
    return (d * _tile_lanes(c, n) + pltpu.roll(d * _tile_lanes(s1, n), n - 16, 1)
            + pltpu.roll(d * _tile_lanes(s2, n), 16, 1))


def _sigmoid(x):
    return 1.0 / (1.0 + jnp.exp(-x))


def _rmsnorm(name, x, g, dtype):
    def fn(xv, gv):
        return ((xv * _rstd(xv)) * gv,)
    return _rowwise(name, fn, [_rows(x), _full(g)], [(x.shape, dtype, "rows")], rows=x.shape[0])[0]


def _residual_norm_bwd(name, dres, dh, x, g):
    def fn(dr, dhv, xv, gv):
        dx, dg = _norm_bwd(dhv, xv, gv)
        return dr + dx, dg
    return _rowwise(name, fn, [_rows(dres), _rows(dh), _rows(x), _full(g)],
                    [(x.shape, F32, "rows"), (g.shape, F32, "acc")], rows=x.shape[0])


def _norm_bwd_epilogue(dh, dres, x, g):
    dx, dg = _norm_bwd(dh, x, g)
    return dres + dx, dg


def _add_and_norm(acc, r, g):
    xs = acc + r
    return xs, xs * _rstd(xs) * g


def _layer_fwd(l, x, h, mem, w, p, next_norm, tabs, swa_bias, S):
    c, ck, s1, s2 = tabs
    n = f"l{l}_"
    gates = _mm(n + "proj_gates", h, w["wag"], [BF16], tm=MM_TM_BF16)
    proj_a = _mm(n + "proj_tail", h, w["wat"], [F32])
    proj_b = _mm(n + "proj_b", h, w["wb"], [BF16], tm=MM_TM_BF16)

    def prep(cq, kva, qn, kvn, ckv, s1v, s2v):
        cqn = cq * _rstd(cq) * qn
        ckv_ = kva[:, :128]
        ckvn = ckv_ * _rstd(ckv_) * kvn
        pe = pltpu.roll(kva[:, 128:], 64, 1)
        return cqn, ckvn, _rope_fwd(pe, ckv, s1v, s2v)

    cqn, ckvn, kpe = _rowwise(
        n + "mla_prep", prep,
        [_rows(proj_a, 256, 0), _rows(proj_a, 256, 1), _full(p["mla_q_norm"]), _full(p["mla_kv_norm"]),
         _rows(ck), _rows(s1), _rows(s2)],
        [((S, 256), BF16, "rows"), ((S, 128), BF16, "rows"), ((S, 128), F32, "rows")], rows=S)

    q_mla = _mm(n + "q_mla", cqn, w["wuq"], [BF16],
                epi=lambda acc, cv, s1v, s2v: (_rope_fwd(acc, cv, s1v, s2v) * (MLA_SCALE * LOG2E),),
                extras=[(c, "m"), (s1, "m"), (s2, "m")])
    k_mla = _mm(n + "k_mla", ckvn, w["wuk"], [BF16],
                epi=lambda acc, kp: (acc + _tile_lanes(kp, acc.shape[1]),), extras=[(kpe, "m")])
    den = ((jnp.arange(MLA_HEADS * HEAD_PAD) % HEAD_PAD) // 2 == DEN_LANE // 2).astype(F32)[None]
    v_mla = _mm(n + "v_mla", ckvn, w["wuv"], [BF16], epi=lambda acc, dv: (acc + dv,), extras=[(den, "n")])
    o_mla, lse_mla = _causal_fwd(n + "mla_fwd", q_mla, k_mla, v_mla, heads=MLA_HEADS, tile=MLA_TILE,
                                 chunk=MLA_CHUNK_FWD)
    o_swa, lse_swa = _swa_fwd(n + "swa_fwd", proj_b, swa_bias, p["sinks"], tq=SWA_TQ)
    mn = _rmsnorm(n + "mem_norm", mem, p["mem_norm"], BF16)
    kvm = _mm(n + "kv_mem", mn, w["wmem"], [BF16])
    o_mem, lse_mem = _mem_fwd(n + "mem_fwd", proj_b, kvm, tq=MEM_TQ, chunk=MEM_CHUNK)
    t0 = _mm(n + "t_mla", o_mla, w["wo_mla"], [BF16], tm=MM_TM_BF16)
    t1 = _mm(n + "t_swa", o_swa, w["wo_swa"], [BF16], tm=MM_TM_BF16)
    t2 = _mm(n + "t_mem", o_mem, w["wo_mem"], [BF16], tm=MM_TM_BF16)

    def merge(g0, g1, g2, bg, a0, a1, a2):
        y = (_sigmoid(g0 + bg[:, 0:1024]) * a0 + _sigmoid(g1 + bg[:, 1024:2048]) * a1
             + _sigmoid(g2 + bg[:, 2048:3072]) * a2)
        return (y,)

    y = _rowwise(n + "merge", merge,
                 [_rows(gates, 1024, 0), _rows(gates, 1024, 1), _rows(gates, 1024, 2), _full(p["b_gate"]),
                  _rows(t0), _rows(t1), _rows(t2)], [((S, D_MODEL), BF16, "rows")], rows=S)[0]
    x1, h2 = _mm(n + "out_proj", y, w["wout"], [F32, BF16], epi=_add_and_norm,
                 extras=[(x, "mn"), (p["mlp_norm"], "n")], tn=D_MODEL)
    act = _mm(n + "mlp_up", h2, w["wup"], [BF16], epi=lambda acc: (jnp.square(jnp.maximum(acc, 0.0)),),
              tm=MM_TM_BF16)
    if next_norm is None:
        x2 = _mm(n + "mlp_down", act, w["wdown"], [F32], epi=lambda acc, r: (acc + r,), extras=[(x1, "mn")],
                 tk=MM_TK_DEEP)
        h_next = None
    else:
        x2, h_next = _mm(n + "mlp_down", act, w["wdown"], [F32, BF16], epi=_add_and_norm,
                         extras=[(x1, "mn"), (next_norm, "n")], tn=D_MODEL, tk=MM_TK_DEEP)
    saved = dict(x=x, h=h, gates=gates, proj_a=proj_a, proj_b=proj_b, cqn=cqn, ckvn=ckvn, q_mla=q_mla, k_mla=k_mla, v_mla=v_mla,
                 o_mla=o_mla, lse_mla=lse_mla, o_swa=o_swa, lse_swa=lse_swa, mn=mn, kvm=kvm, o_mem=o_mem,
                 lse_mem=lse_mem, t0=t0, t1=t1, t2=t2, y=y, x1=x1, h2=h2, act=act)
    return x2, h_next, saved


def _layer_bwd(l, dx2, mem, w, p, tabs, swa_bias, sv, S):
    c, ck, s1, s2 = tabs
    n = f"l{l}_b_"
    gw = {}
    gs = {}
    du = _mm(n + "d_act", dx2, w["wdown_t"], [BF16],
             epi=lambda acc, av: (acc * (2.0 * jnp.sqrt(av.astype(F32))),), extras=[(sv["act"], "mn")],
             tm=MM_TM_BF16)
    gw["wdown"] = _mm_tn(n + "g_wdown", sv["act"], dx2)
    gw["wup"] = _mm_tn(n + "g_wup", sv["h2"], du)
    dx1, gs["mlp_norm"] = _mm(n + "d_h2", du, w["wup_t"], [F32], epi=_norm_bwd_epilogue,
                              extras=[(dx2, "mn"), (sv["x1"], "mn"), (p["mlp_norm"], "n")], tn=D_MODEL, col_sums=1)
    gw["wout"] = _mm_tn(n + "g_wout", sv["y"], dx1)
    dy = _mm(n + "d_y", dx1, w["wout_t"], [F32], tm=MM_TM_BF16)

    def merge_bwd(dyv, g0, g1, g2, bg, a0, a1, a2):
        outs, dgs = [], []
        for b, (gv, av) in enumerate(((g0, a0), (g1, a1), (g2, a2))):
            sg = _sigmoid(gv + bg[:, b * 1024:(b + 1) * 1024])
            outs.append(dyv * sg)
            dgs.append(dyv * av * sg * (1.0 - sg))
        dg = jnp.concatenate(dgs, axis=1)
        return outs[0], outs[1], outs[2], dg, jnp.sum(dg, axis=0, keepdims=True)

    pa = sv["proj_a"]
    gt = sv["gates"]
    dt0, dt1, dt2, dgates, gs["b_gate"] = _rowwise(
        n + "merge", merge_bwd,
        [_rows(dy), _rows(gt, 1024, 0), _rows(gt, 1024, 1), _rows(gt, 1024, 2), _full(p["b_gate"]),
         _rows(sv["t0"]), _rows(sv["t1"]), _rows(sv["t2"])],
        [((S, D_MODEL), BF16, "rows")] * 3 + [((S, 3 * D_MODEL), BF16, "rows"), ((1, 3 * D_MODEL), F32, "acc")],
        rows=S)
    gw["wo_mla"] = _mm_tn(n + "g_wo_mla", sv["o_mla"], dt0)
    gw["wo_swa"] = _mm_tn(n + "g_wo_swa", sv["o_swa"], dt1)
    gw["wo_mem"] = _mm_tn(n + "g_wo_mem", sv["o_mem"], dt2)
    do_mla = _mm(n + "d_o_mla", dt0, w["wo_mla_t"], [BF16], epi=lambda acc, ov: (_with_neg_delta(acc, ov),),
                 extras=[(sv["o_mla"], "mn")], tn=MLA_HEADS * HEAD_PAD, tm=MM_TM_BF16)
    do_swa = _mm(n + "d_o_swa", dt1, w["wo_swa_t"], [BF16], tm=MM_TM_BF16)
    do_mem = _mm(n + "d_o_mem", dt2, w["wo_mem_t"], [BF16], tm=MM_TM_BF16)
    pb = sv["proj_b"]
    dq_mla, dk_mla, dv_mla = _causal_bwd(
        n + "mla_bwd", sv["q_mla"], sv["k_mla"], sv["v_mla"], do_mla, sv["lse_mla"], heads=MLA_HEADS,
        tile=MLA_TILE_BWD, chunk=MLA_CHUNK)
    dq_swa, dk_swa, dv_swa, dk_edge, dv_edge, dbias, dsink = _swa_bwd(
        n + "swa_bwd", pb, swa_bias, p["sinks"], sv["o_swa"], do_swa, sv["lse_swa"], tq=SWA_TQ)
    dq_mem, dkvm = _mem_bwd(n + "mem_bwd", pb, sv["kvm"], sv["o_mem"], do_mem, sv["lse_mem"], tq=MEM_TQ,
                            chunk=MEM_CHUNK)
    gs["dbias"] = dbias
    gs["sinks"] = dsink[:, 0, 0]
    gw["wmem"] = _mm_tn(n + "g_wmem", sv["mn"], dkvm)
    dmn = _mm(n + "d_mn", dkvm, w["wmem_t"], [F32])
    _, gs["mem_norm"] = _residual_norm_bwd(n + "mem_norm", dmn, dmn, mem, p["mem_norm"])
    dq_pre = _rowwise(n + "q_unrope", lambda d, cv, s1v, s2v: (_rope_bwd(d * MLA_SCALE, cv, s1v, s2v),),
                      [_rows(dq_mla), _rows(c), _rows(s1), _rows(s2)], [((S, 1024), BF16, "rows")], rows=S)[0]
    gw["wuq"] = _mm_tn(n + "g_wuq", sv["cqn"], dq_pre)
    gw["wuk"] = _mm_tn(n + "g_wuk", sv["ckvn"], dk_mla)
    gw["wuv"] = _mm_tn(n + "g_wuv", sv["ckvn"], dv_mla)
    dcqn = _mm(n + "d_cqn", dq_pre, w["wuq_t"], [F32])
    dckvn = _mm(n + "d_ckvn_k", dk_mla, w["wuk_t"], [F32])
    dckvn = _mm(n + "d_ckvn_v", dv_mla, w["wuv_t"], [F32], epi=lambda acc, r: (acc + r,), extras=[(dckvn, "mn")])

    def mla_norm_bwd(dcq_n, dckv_n, dk, cq, kva, qn, kvn, ckv, s1v, s2v):
        dcq, dqn = _norm_bwd(dcq_n, cq, qn)
        dckv, dkvn = _norm_bwd(dckv_n, kva[:, :128], kvn)
        dkpe = dk[:, 0:128]
        for hh in range(1, MLA_HEADS):
            dkpe = dkpe + dk[:, hh * 128:(hh + 1) * 128]
        dpe = pltpu.roll(_rope_bwd(dkpe, ckv, s1v, s2v), 64, 1)
        return jnp.concatenate([dcq, dckv, dpe], axis=1), dqn, dkvn

    dtail, gs["mla_q_norm"], gs["mla_kv_norm"] = _rowwise(
        n + "mla_norm", mla_norm_bwd,
        [_rows(dcqn), _rows(dckvn), _rows(dk_mla), _rows(pa, 256, 0), _rows(pa, 256, 1),
         _full(p["mla_q_norm"]), _full(p["mla_kv_norm"]), _rows(ck), _rows(s1), _rows(s2)],
        [((S, 512), BF16, "rows"), ((1, 256), F32, "acc"), ((1, 128), F32, "acc")], rows=S)

    dproj_b = _dproj_b(n + "dproj_b", dq_swa, dq_mem, dk_swa, dv_swa, dk_edge, dv_edge, tq=SWA_TQ)
    h = sv["h"]
    gw["wag"] = _mm_tn(n + "g_wa_gates", h, dgates)
    gw["wat"] = _mm_tn(n + "g_wa_tail", h, dtail)
    gw["wb"] = _mm_tn(n + "g_wb", h, dproj_b)
    dh = _mm(n + "d_h_gates", dgates, w["wag_t"], [F32], tm=MM_TM_BF16)
    dh = _mm(n + "d_h_tail", dtail, w["wat_t"], [F32], epi=lambda acc, r: (acc + r,), extras=[(dh, "mn")])
    dx, gs["attn_norm"] = _mm(n + "d_h_b", dproj_b, w["wb_t"], [F32],
                              epi=lambda acc, prev, dr, xv, gv: _norm_bwd_epilogue(acc + prev, dr, xv, gv),
                              extras=[(dh, "mn"), (dx1, "mn"), (sv["x"], "mn"), (p["attn_norm"], "n")],
                              tn=D_MODEL, tm=MM_TM // 2, col_sums=1)
    return dx, gw, gs


def _local_step(x, mem, loss_target, full, small):
    S = x.shape[0]
    tabs = _rope_tables(S)
    onehot, band = _bias_onehot()
    hi = lax.Precision.HIGHEST
    swa_bias = _mm("swa_bias", small["rel_bias"].T, onehot, [F32], epi=lambda acc, mk: (acc + mk,),
                   extras=[(band, "n")], cast=None, precision=hi, tn=8192).reshape(SWA_HEADS, WINDOW, 2 * WINDOW)
    ws, ps = [], []
    for l in range(DEPTH):
        ws.append(_layer_weights(full, l))
        ps.append(dict(
            attn_norm=small["attn_norm"][l][None], mem_norm=small["mem_norm"][l][None],
            b_gate=small["b_gate"][l][None], mla_q_norm=small["mla_q_norm"][l][None],
            mla_kv_norm=small["mla_kv_norm"][l][None], mlp_norm=small["mlp_norm"][l][None],
            sinks=jnp.broadcast_to(small["attn_sinks"][l][:, None, None], (SWA_HEADS, 8, 128))))
    saved = []
    xc = x
    hc = _rmsnorm("l0_attn_norm", x, ps[0]["attn_norm"], BF16)
    for l in range(DEPTH):
        next_norm = ps[l + 1]["attn_norm"] if l + 1 < DEPTH else None
        xc, hc, sv = _layer_fwd(l, xc, hc, mem, ws[l], ps[l], next_norm, tabs, swa_bias, S)
        saved.append(sv)

    fn_g = small["final_norm"][None]

    def loss_fn(xv, gv, tv):
        r = _rstd(xv)
        xh = xv * r
        err = xh * gv - tv
        dyv = err * (1.0 / D_MODEL)
        wv = dyv * gv
        dx = r * (wv - xh * jnp.mean(wv * xh, axis=-1, keepdims=True))
        part = 0.5 * jnp.sum(err * err) * (1.0 / D_MODEL)
        return dx, jnp.sum(dyv * xh, axis=0, keepdims=True), jnp.zeros((8, 128), F32) + part

    dx, g_final, loss_acc = _rowwise(
        "loss", loss_fn, [_rows(xc), _full(fn_g), _rows(loss_target)],
        [((S, D_MODEL), F32, "rows"), ((1, D_MODEL), F32, "acc"), ((8, 128), F32, "acc")], rows=S)

    gws, gss = [None] * DEPTH, [None] * DEPTH
    for l in reversed(range(DEPTH)):
        dx, gw, gs = _layer_bwd(l, dx, mem, ws[l], ps[l], tabs, swa_bias, saved[l], S)
        gws[l] = _layer_weight_grads(gw)
        gss[l] = gs

    dbias = (gss[0]["dbias"] + gss[1]["dbias"]).reshape(SWA_HEADS, -1)
    g_rel = _mm("g_rel_bias", dbias, onehot.T, [F32], cast=None, precision=hi, tk=8192).T
    wgrads = {k: jnp.stack([gws[l][k] for l in range(DEPTH)]) for k in gws[0]}
    sgrads = dict(
        rel_bias=g_rel,
        final_norm=g_final[0],
        attn_sinks=jnp.stack([gss[l]["sinks"] for l in range(DEPTH)]),
        **{k: jnp.concatenate([gss[l][k] for l in range(DEPTH)], axis=0)
           for k in ("attn_norm", "mem_norm", "b_gate", "mla_q_norm", "mla_kv_norm", "mlp_norm")})
    return loss_acc[0, 0], dx, wgrads, sgrads


def _pack_small(vals, loss):
    rows = []
    for name, shape in SMALL:
        flat = vals[name].astype(F32).reshape(-1)
        pad = (-flat.shape[0]) % 1024
        rows.append(jnp.pad(flat, (0, pad)).reshape(-1, 128))
    rows.append(jnp.zeros((8, 128), F32) + loss)
    return jnp.concatenate(rows, axis=0)


def _unpack_small(packed):
    out, r = {}, 0
    for name, shape in SMALL:
        size = math.prod(shape)
        nrows = 8 * -(-size // 1024)
        out[name] = packed[r:r + nrows].reshape(-1)[:size].reshape(shape)
        r += nrows
    return out, packed[r, 0]


def kernel(x, mem, rel_bias, attn_norm, mem_norm, w_in, b_gate, mla_q_norm, w_uq, mla_kv_norm, w_ukv, attn_sinks, w_mem_kv, w_o_mla, w_o_swa, w_o_mem, w_out, mlp_norm, w_up, w_down, final_norm, loss_target, m_rel_bias, m_attn_norm, m_mem_norm, m_w_in, m_b_gate, m_mla_q_norm, m_w_uq, m_mla_kv_norm, m_w_ukv, m_attn_sinks, m_w_mem_kv, m_w_o_mla, m_w_o_swa, m_w_o_mem, m_w_out, m_mlp_norm, m_w_up, m_w_down, m_final_norm, v_rel_bias, v_attn_norm, v_mem_norm, v_w_in, v_b_gate, v_mla_q_norm, v_w_uq, v_mla_kv_norm, v_w_ukv, v_attn_sinks, v_w_mem_kv, v_w_o_mla, v_w_o_swa, v_w_o_mem, v_w_out, v_mlp_norm, v_w_up, v_w_down, v_final_norm):
    wv = dict(rel_bias=rel_bias, attn_norm=attn_norm, mem_norm=mem_norm, w_in=w_in, b_gate=b_gate,
              mla_q_norm=mla_q_norm, w_uq=w_uq, mla_kv_norm=mla_kv_norm, w_ukv=w_ukv, attn_sinks=attn_sinks,
              w_mem_kv=w_mem_kv, w_o_mla=w_o_mla, w_o_swa=w_o_swa, w_o_mem=w_o_mem, w_out=w_out,
              mlp_norm=mlp_norm, w_up=w_up, w_down=w_down, final_norm=final_norm)
    mv = dict(rel_bias=m_rel_bias, attn_norm=m_attn_norm, mem_norm=m_mem_norm, w_in=m_w_in, b_gate=m_b_gate,
              mla_q_norm=m_mla_q_norm, w_uq=m_w_uq, mla_kv_norm=m_mla_kv_norm, w_ukv=m_w_ukv,
              attn_sinks=m_attn_sinks, w_mem_kv=m_w_mem_kv, w_o_mla=m_w_o_mla, w_o_swa=m_w_o_swa,
              w_o_mem=m_w_o_mem, w_out=m_w_out, mlp_norm=m_mlp_norm, w_up=m_w_up, w_down=m_w_down,
              final_norm=m_final_norm)
    vv = dict(rel_bias=v_rel_bias, attn_norm=v_attn_norm, mem_norm=v_mem_norm, w_in=v_w_in, b_gate=v_b_gate,
              mla_q_norm=v_mla_q_norm, w_uq=v_w_uq, mla_kv_norm=v_mla_kv_norm, w_ukv=v_w_ukv,
              attn_sinks=v_attn_sinks, w_mem_kv=v_w_mem_kv, w_o_mla=v_w_o_mla, w_o_swa=v_w_o_swa,
              w_o_mem=v_w_o_mem, w_out=v_w_out, mlp_norm=v_mlp_norm, w_up=v_w_up, w_down=v_w_down,
              final_norm=v_final_norm)

    shard_rows = [math.prod(_shard_shape(shape, axis)) // 128 for _, shape, axis in WSPECS]
    gathered = _gather_forwarded("gather_weights", _pack_rows([wv[name].astype(BF16) for name, _, _ in WSPECS]))
    full, r = {}, 0
    for (name, shape, axis), nr in zip(WSPECS, shard_rows):
        full[name] = _unstack(gathered[:, r:r + nr].reshape((N_DEV,) + _shard_shape(shape, axis)), shape, axis)
        r += nr

    loss_part, grad_x, wgrads, sgrads = _local_step(x[0], mem[0], loss_target[0], full,
                                                    {name: wv[name] for name, _ in SMALL})

    recv = _reduce_scatter([_restack(wgrads[name], axis).astype(BF16).reshape(N_DEV, -1, 128)
                            for name, _, axis in WSPECS])
    outs = _adam("adam_sharded", recv, *[_pack_rows([d[name] for name, _, _ in WSPECS]) for d in (wv, mv, vv)])
    res = {}
    r = 0
    for (name, shape, axis), nr in zip(WSPECS, shard_rows):
        res[name] = [o[r:r + nr].reshape(_shard_shape(shape, axis)) for o in outs]
        r += nr

    small_recv = _gather_forwarded("gather_small", _pack_small(sgrads, loss_part))
    zero = jnp.zeros((), F32)
    souts = _adam("adam_small", small_recv, *[_pack_small(d, zero) for d in (wv, mv, vv)])
    loss = None
    for i, o in enumerate(souts):
        vals, extra = _unpack_small(o)
        if i == 0:
            loss = extra
        for name, _ in SMALL:
            res.setdefault(name, []).append(vals[name])

    out = [loss, grad_x[None]]
    for i in range(4):
        out.extend(res[name][i] for name in WEIGHT_ORDER)
    return tuple(out)
```

```python
import math

import jax
import jax.numpy as jnp
from jax import lax
from jax.experimental import pallas as pl
from jax.experimental.pallas import tpu as pltpu

F32 = jnp.float32
BF16 = jnp.bfloat16

N_DEV = 8
D_MODEL = 1024
DEPTH = 2
MLA_HEADS = 8
MLA_Q_LORA = 256
MLA_KV_LORA = 128
MLA_NOPE = 64
MLA_ROPE = 32
MLA_V = 64
ROPE_THETA = 10000.0
SWA_HEADS = 8
SWA_KV_HEADS = 2
SWA_HD = 64
WINDOW = 128
REL_BUCKETS = 32
REL_MAX_DIST = 128
MEM_LEN = 256
MEM_HEADS = 4
MEM_HD = 128
D_FF = 4 * D_MODEL
EPS = 1e-6
HEAD_PAD = 128
ADAM_LR = 0.001
ADAM_B1 = 0.9
ADAM_B2 = 0.999
ADAM_EPS = 1e-08
ADAM_WD = 0.01
ADAM_STEP = 10

NEG = -1e30
VMEM_LIMIT = 48 * 1024 * 1024

MM_TM = 1024
MM_TN = 1024
MM_TK = 1024
TN_T1 = 1024
TN_TN = 1024
TN_TS = 2048
MM_TK_DEEP = 2048
MM_TM_BF16 = 2048
ROW_TM = 512
MLA_TILE = 4096
MLA_TILE_BWD = 2048
MLA_BWD_VMEM_LIMIT = VMEM_LIMIT
MLA_CHUNK = 256
MLA_CHUNK_FWD = 512
MLA_QK = MLA_NOPE + MLA_ROPE
MLA_SCALE = MLA_QK ** -0.5
LOG2E = math.log2(math.e)
DEN_LANE = MLA_V
SWA_TQ = 1024
SWA_AHEAD = 1
MEM_TQ = 1024
MEM_CHUNK = 256
ADAM_TM = 2000

WSPECS = (
    ("w_in", (DEPTH, D_MODEL, 4768), 2),
    ("w_uq", (DEPTH, MLA_Q_LORA, 768), 2),
    ("w_ukv", (DEPTH, MLA_KV_LORA, 1024), 2),
    ("w_mem_kv", (DEPTH, D_MODEL, 1024), 1),
    ("w_o_mla", (DEPTH, 512, D_MODEL), 2),
    ("w_o_swa", (DEPTH, 512, D_MODEL), 2),
    ("w_o_mem", (DEPTH, 512, D_MODEL), 2),
    ("w_out", (DEPTH, D_MODEL, D_MODEL), 1),
    ("w_up", (DEPTH, D_MODEL, D_FF), 2),
    ("w_down", (DEPTH, D_FF, D_MODEL), 1),
)
SMALL = (
    ("rel_bias", (REL_BUCKETS, SWA_HEADS)),
    ("attn_norm", (DEPTH, D_MODEL)),
    ("mem_norm", (DEPTH, D_MODEL)),
    ("b_gate", (DEPTH, 3 * D_MODEL)),
    ("mla_q_norm", (DEPTH, MLA_Q_LORA)),
    ("mla_kv_norm", (DEPTH, MLA_KV_LORA)),
    ("attn_sinks", (DEPTH, SWA_HEADS)),
    ("mlp_norm", (DEPTH, D_MODEL)),
    ("final_norm", (D_MODEL,)),
)
WEIGHT_ORDER = ("rel_bias", "attn_norm", "mem_norm", "w_in", "b_gate", "mla_q_norm", "w_uq", "mla_kv_norm",
                "w_ukv", "attn_sinks", "w_mem_kv", "w_o_mla", "w_o_swa", "w_o_mem", "w_out", "mlp_norm",
                "w_up", "w_down", "final_norm")


def _cparams(*sem):
    return pltpu.CompilerParams(dimension_semantics=sem, vmem_limit_bytes=VMEM_LIMIT)


def _shard_shape(shape, axis):
    s = list(shape)
    s[axis] //= N_DEV
    return tuple(s)


def _mm(name, a, b, out_dtypes, *, epi=None, extras=(), a_fn=None, cast=BF16, precision=None,
        tm=None, tn=None, tk=None, col_sums=0):
    M, K = a.shape
    K2, N = b.shape
    assert K == K2, (name, a.shape, b.shape)
    tm = min(tm or MM_TM, M)
    tn = min(tn or MM_TN, N)
    tk = min(tk or MM_TK, K)
    assert M % tm == 0 and N % tn == 0 and K % tk == 0, (name, a.shape, b.shape, tm, tn, tk)
    assert col_sums == 0 or tn == N, (name, tn, N)
    nk = K // tk
    n_ex = len(extras)
    n_out = len(out_dtypes)

    def body(*refs):
        a_ref, b_ref = refs[0], refs[1]
        ex_refs = refs[2:2 + n_ex]
        out_refs = refs[2 + n_ex:2 + n_ex + n_out]
        av = a_ref[...]
        if a_fn is not None:
            av = a_fn(av)
        bv = b_ref[...]
        if cast is not None:
            av = av.astype(cast)
            bv = bv.astype(cast)
        part = jnp.dot(av, bv, preferred_element_type=F32, precision=precision)

        def finish(acc):
            outs = epi(acc, *[r[...] for r in ex_refs]) if epi is not None else (acc,)
            for r, o in zip(out_refs, outs[:n_out]):
                r[...] = o.astype(r.dtype)
            i = pl.program_id(0)
            for r, o in zip(refs[2 + n_ex + n_out:2 + n_ex + n_out + col_sums], outs[n_out:]):
                @pl.when(i == 0)
                def _(r=r, o=o):
                    r[...] = o

                @pl.when(i > 0)
                def _(r=r, o=o):
                    r[...] += o

        if nk == 1:
            finish(part)
        else:
            acc_ref = refs[-1]
            k = pl.program_id(2)

            @pl.when(k == 0)
            def _():
                acc_ref[...] = part

            @pl.when(k > 0)
            def _():
                acc_ref[...] += part

            @pl.when(k == nk - 1)
            def _():
                finish(acc_ref[...])

    in_specs = [pl.BlockSpec((tm, tk), lambda i, j, k: (i, k)),
                pl.BlockSpec((tk, tn), lambda i, j, k: (k, j))]
    for arr, kind in extras:
        if kind == "mn":
            in_specs.append(pl.BlockSpec((tm, tn), lambda i, j, k: (i, j)))
        elif kind == "m":
            in_specs.append(pl.BlockSpec((tm, arr.shape[1]), lambda i, j, k: (i, 0)))
        else:
            in_specs.append(pl.BlockSpec((1, tn), lambda i, j, k: (0, j)))
    outs = pl.pallas_call(
        body, name=name, grid=(M // tm, N // tn, nk),
        in_specs=in_specs,
        out_specs=([pl.BlockSpec((tm, tn), lambda i, j, k: (i, j)) for _ in out_dtypes]
                   + [pl.BlockSpec((1, tn), lambda i, j, k: (0, 0))] * col_sums),
        out_shape=([jax.ShapeDtypeStruct((M, N), dt) for dt in out_dtypes]
                   + [jax.ShapeDtypeStruct((1, N), F32)] * col_sums),
        scratch_shapes=[pltpu.VMEM((tm, tn), F32)] if nk > 1 else [],
        compiler_params=(_cparams("arbitrary", "arbitrary", "arbitrary") if col_sums
                         else _cparams("parallel", "parallel", "arbitrary")),
    )(a, b, *[arr for arr, _ in extras])
    return outs[0] if n_out + col_sums == 1 else outs


def _mm_tn(name, a, b, *, t1=None, tn=None, ts=None):
    S, K1 = a.shape
    S2, N = b.shape
    assert S == S2, (name, a.shape, b.shape)
    t1 = min(t1 or TN_T1, K1)
    tn = min(tn or TN_TN, N)
    ts = min(ts or TN_TS, S)
    assert K1 % t1 == 0 and N % tn == 0 and S % ts == 0, (name, a.shape, b.shape)

    def body(a_ref, b_ref, o_ref):
        s = pl.program_id(2)
        part = lax.dot_general(a_ref[...].astype(BF16), b_ref[...].astype(BF16),
                               (((0,), (0,)), ((), ())), preferred_element_type=F32)

        @pl.when(s == 0)
        def _():
            o_ref[...] = part

        @pl.when(s > 0)
        def _():
            o_ref[...] += part

    return pl.pallas_call(
        body, name=name, grid=(K1 // t1, N // tn, S // ts),
        in_specs=[pl.BlockSpec((ts, t1), lambda i, j, s: (s, i)),
                  pl.BlockSpec((ts, tn), lambda i, j, s: (s, j))],
        out_specs=pl.BlockSpec((t1, tn), lambda i, j, s: (i, j)),
        out_shape=jax.ShapeDtypeStruct((K1, N), F32),
        compiler_params=_cparams("parallel", "parallel", "arbitrary"),
    )(a, b)


def _rows(arr, width=None, blk=0):
    return (arr, ("rows", arr.shape[1] if width is None else width, blk))


def _full(arr):
    return (arr, ("full",))


def _rowwise(name, fn, ins, outs, *, rows, tm=None):
    tm = min(tm or ROW_TM, rows)
    assert rows % tm == 0, (name, rows, tm)
    n_in = len(ins)

    def body(*refs):
        i = pl.program_id(0)
        vals = fn(*[r[...] for r in refs[:n_in]])
        for (shape, dt, kind), r, v in zip(outs, refs[n_in:], vals):
            if kind == "rows":
                r[...] = v.astype(dt)
            else:
                @pl.when(i == 0)
                def _(r=r, v=v):
                    r[...] = v

                @pl.when(i > 0)
                def _(r=r, v=v):
                    r[...] += v

    in_specs = []
    for arr, spec in ins:
        if spec[0] == "rows":
            in_specs.append(pl.BlockSpec((tm, spec[1]), lambda i, b=spec[2]: (i, b)))
        else:
            in_specs.append(pl.BlockSpec(arr.shape, lambda i, n=arr.ndim: (0,) * n))
    out_specs = []
    for shape, dt, kind in outs:
        if kind == "rows":
            out_specs.append(pl.BlockSpec((tm, shape[1]), lambda i: (i, 0)))
        else:
            out_specs.append(pl.BlockSpec(shape, lambda i, n=len(shape): (0,) * n))
    res = pl.pallas_call(
        body, name=name, grid=(rows // tm,),
        in_specs=in_specs, out_specs=out_specs,
        out_shape=[jax.ShapeDtypeStruct(shape, dt) for shape, dt, _ in outs],
        compiler_params=_cparams("arbitrary"),
    )(*[arr for arr, _ in ins])
    return res


MEM_SCALE = MEM_HD ** -0.5
MEM_Q0 = 2
NT_DIMS = (((1,), (1,)), ((), ()))


def _head_lanes(h):
    return slice(h * HEAD_PAD, (h + 1) * HEAD_PAD)


def _mem_fwd(name, proj_b, kvm, *, tq, chunk):
    S = proj_b.shape[0]
    tq = min(tq, S)
    C = min(chunk, tq)
    tiles = [(h, c) for c in range(tq // C) for h in range(MEM_HEADS)]

    def body(q_ref, kv_ref, o_ref, lse_ref):
        def logits(h, c):
            return lax.dot_general(q_ref[c * C:(c + 1) * C, _head_lanes(h)], kv_ref[:, _head_lanes(h)], NT_DIMS,
                                   preferred_element_type=F32) * MEM_SCALE

        nxt = logits(*tiles[0])
        for n, (h, c) in enumerate(tiles):
            s = nxt
            if n + 1 < len(tiles):
                nxt = logits(*tiles[n + 1])
            rows = slice(c * C, (c + 1) * C)
            m = jnp.max(s, axis=1, keepdims=True)
            p = jnp.exp(s - m)
            l = jnp.sum(p, axis=1, keepdims=True)
            o = jnp.dot(p.astype(BF16), kv_ref[:, _head_lanes(MEM_HEADS + h)], preferred_element_type=F32) / l
            o_ref[rows, _head_lanes(h)] = o.astype(o_ref.dtype)
            lse_ref[h, rows, :] = m + jnp.log(l)

    return pl.pallas_call(
        body, name=name, grid=(S // tq,),
        in_specs=[pl.BlockSpec((tq, MEM_HEADS * HEAD_PAD), lambda i: (i, MEM_Q0)),
                  pl.BlockSpec(kvm.shape, lambda i: (0, 0))],
        out_specs=[pl.BlockSpec((tq, MEM_HEADS * HEAD_PAD), lambda i: (i, 0)),
                   pl.BlockSpec((MEM_HEADS, tq, 1), lambda i: (0, i, 0))],
        out_shape=[jax.ShapeDtypeStruct((S, MEM_HEADS * HEAD_PAD), BF16),
                   jax.ShapeDtypeStruct((MEM_HEADS, S, 1), F32)],
        compiler_params=_cparams("parallel"),
    )(proj_b, kvm)


def _mem_bwd(name, proj_b, kvm, o, do, lse, *, tq, chunk):
    S = proj_b.shape[0]
    tq = min(tq, S)
    C = min(chunk, tq)
    nq = S // tq
    tiles = [(h, c) for c in range(tq // C) for h in range(MEM_HEADS)]

    def body(q_ref, kv_ref, o_ref, do_ref, lse_ref, dq_ref, dkv_ref, acc_sc):
        i = pl.program_id(0)

        @pl.when(i == 0)
        def _():
            acc_sc[...] = jnp.zeros(acc_sc.shape, F32)

        def mats(h, c):
            rows = slice(c * C, (c + 1) * C)
            q = q_ref[rows, _head_lanes(h)]
            dov = do_ref[rows, _head_lanes(h)]
            s = lax.dot_general(q, kv_ref[:, _head_lanes(h)], NT_DIMS, preferred_element_type=F32) * MEM_SCALE
            dp = lax.dot_general(dov, kv_ref[:, _head_lanes(MEM_HEADS + h)], NT_DIMS, preferred_element_type=F32)
            return q, dov, s, dp

        nxt = mats(*tiles[0])
        for n, (h, c) in enumerate(tiles):
            q, dov, s, dp = nxt
            if n + 1 < len(tiles):
                nxt = mats(*tiles[n + 1])
            rows = slice(c * C, (c + 1) * C)
            p = jnp.exp(s - lse_ref[h, rows, :])
            delta = jnp.sum(dov.astype(F32) * o_ref[rows, _head_lanes(h)].astype(F32), axis=1, keepdims=True)
            ds = (p * (dp - delta) * MEM_SCALE).astype(BF16)
            dq_ref[rows, _head_lanes(h)] = jnp.dot(ds, kv_ref[:, _head_lanes(h)],
                                                   preferred_element_type=F32).astype(dq_ref.dtype)
            acc_sc[_head_lanes(h), :] += jnp.dot(q.T, ds, preferred_element_type=F32)
            acc_sc[_head_lanes(MEM_HEADS + h), :] += jnp.dot(dov.T, p.astype(BF16), preferred_element_type=F32)

        @pl.when(i == nq - 1)
        def _():
            dkv_ref[...] = acc_sc[...].T

    qblk = pl.BlockSpec((tq, MEM_HEADS * HEAD_PAD), lambda i: (i, 0))
    return pl.pallas_call(
        body, name=name, grid=(nq,),
        in_specs=[pl.BlockSpec((tq, MEM_HEADS * HEAD_PAD), lambda i: (i, MEM_Q0)),
                  pl.BlockSpec(kvm.shape, lambda i: (0, 0)), qblk, qblk,
                  pl.BlockSpec((MEM_HEADS, tq, 1), lambda i: (0, i, 0))],
        out_specs=[qblk, pl.BlockSpec(kvm.shape, lambda i: (0, 0))],
        out_shape=[jax.ShapeDtypeStruct((S, MEM_HEADS * HEAD_PAD), BF16), jax.ShapeDtypeStruct(kvm.shape, F32)],
        scratch_shapes=[pltpu.VMEM((kvm.shape[1], kvm.shape[0]), F32)],
        compiler_params=_cparams("arbitrary"),
    )(proj_b, kvm, o, do, lse)


def _causal_fwd(name, q_arr, k_arr, v_arr, *, heads, tile, chunk):
    S = q_arr.shape[0]
    T = min(tile, S)
    C = min(chunk, T)
    nt = S // T
    nc = T // C

    pairs = [(qi, kk) for qi in range(nt) for kk in range(qi + 1)]
    q_tab = jnp.asarray([p[0] for p in pairs], jnp.int32)
    k_tab = jnp.asarray([p[1] for p in pairs], jnp.int32)

    def body(qt_ref, kt_ref, q_ref, k_ref, v_ref, o_ref, lse_ref, m_sc, acc_sc):
        t = pl.program_id(1)
        qi = qt_ref[t]
        kk = kt_ref[t]

        @pl.when(kk == 0)
        def _():
            m_sc[...] = jnp.full(m_sc.shape, NEG, F32)
            acc_sc[...] = jnp.zeros(acc_sc.shape, F32)

        def logits(c, ncols, masked):
            s = lax.dot_general(q_ref[pl.ds(c * C, C), :], k_ref[0:ncols, :], (((1,), (1,)), ((), ())),
                                preferred_element_type=F32)
            if masked:
                r = c * C + lax.broadcasted_iota(jnp.int32, (C, ncols), 0)
                cidx = lax.broadcasted_iota(jnp.int32, (C, ncols), 1)
                s = jnp.where(cidx <= r, s, NEG)
            return s

        def update(c, ncols, s):
            rows = pl.ds(c * C, C)
            m_prev = m_sc[rows, :]
            m_new = jnp.maximum(m_prev, jnp.max(s, axis=1, keepdims=True))
            p = jnp.exp2(s - m_new).astype(BF16)
            acc_sc[rows, :] = jnp.exp2(m_prev - m_new) * acc_sc[rows, :] + jnp.dot(
                p, v_ref[0:ncols, :], preferred_element_type=F32)
            m_sc[rows, :] = m_new

        def tile_step(ncols_of, masked):
            s = logits(0, ncols_of(0), masked)
            for c in range(nc):
                s_next = logits(c + 1, ncols_of(c + 1), masked) if c + 1 < nc else None
                update(c, ncols_of(c), s)
                s = s_next

        @pl.when(kk < qi)
        def _():
            tile_step(lambda c: T, False)

        @pl.when(kk == qi)
        def _():
            tile_step(lambda c: (c + 1) * C, True)

        @pl.when(kk == qi)
        def _():
            acc = acc_sc[...]
            l = acc[:, DEN_LANE:DEN_LANE + 1]
            o_ref[...] = (acc / l).astype(o_ref.dtype)
            lse_ref[0] = m_sc[...] + jnp.log2(l)

    grid_spec = pltpu.PrefetchScalarGridSpec(
        num_scalar_prefetch=2, grid=(heads, len(pairs)),
        in_specs=[pl.BlockSpec((T, HEAD_PAD), lambda h, t, qt, kt: (qt[t], h)),
                  pl.BlockSpec((T, HEAD_PAD), lambda h, t, qt, kt: (kt[t], h)),
                  pl.BlockSpec((T, HEAD_PAD), lambda h, t, qt, kt: (kt[t], h))],
        out_specs=[pl.BlockSpec((T, HEAD_PAD), lambda h, t, qt, kt: (qt[t], h)),
                   pl.BlockSpec((1, T, 1), lambda h, t, qt, kt: (h, qt[t], 0))],
        scratch_shapes=[pltpu.VMEM((T, 1), F32), pltpu.VMEM((T, HEAD_PAD), F32)])
    return pl.pallas_call(
        body, name=name, grid_spec=grid_spec,
        out_shape=[jax.ShapeDtypeStruct((S, heads * HEAD_PAD), BF16),
                   jax.ShapeDtypeStruct((heads, S, 1), F32)],
        compiler_params=_cparams("parallel", "arbitrary"),
    )(q_tab, k_tab, q_arr, k_arr, v_arr)


def _with_neg_delta(do, o):
    lane = lax.broadcasted_iota(jnp.int32, (1, HEAD_PAD), 1)
    outs = []
    for h in range(do.shape[1] // HEAD_PAD):
        a = do[:, _head_lanes(h)]
        nd = -jnp.sum(a * o[:, _head_lanes(h)].astype(F32), axis=1, keepdims=True)
        hi = nd.astype(BF16).astype(F32)
        a = jnp.where(lane == DEN_LANE, hi, a)
        outs.append(jnp.where(lane == DEN_LANE + 1, nd - hi, a))
    return jnp.concatenate(outs, axis=1)


def _causal_bwd(name, q_arr, k_arr, v_arr, do_arr, lse, *, heads, tile, chunk):
    S = q_arr.shape[0]
    T = min(tile, S)
    C = min(chunk, T)
    nt = S // T
    nc = T // C

    pairs = [(kj, qq) for kj in range(nt) for qq in range(kj, nt)]
    k_tab = jnp.asarray([p[0] for p in pairs], jnp.int32)
    q_tab = jnp.asarray([p[1] for p in pairs], jnp.int32)

    def body(kt_ref, qt_ref, q_ref, k_ref, v_ref, do_ref, lse_ref, dq_ref, dk_ref, dv_ref, dk_sc, dv_sc):
        t = pl.program_id(1)
        kj = kt_ref[t]
        qq = qt_ref[t]
        qb = qq

        @pl.when(t == 0)
        def _():
            dq_ref[...] = jnp.zeros(dq_ref.shape, F32)

        @pl.when(qq == kj)
        def _():
            dk_sc[...] = jnp.zeros(dk_sc.shape, F32)
            dv_sc[...] = jnp.zeros(dv_sc.shape, F32)

        def logits(c, ncols, masked):
            rows = pl.ds(c * C, C)
            s = lax.dot_general(q_ref[rows, :], k_ref[0:ncols, :], (((1,), (1,)), ((), ())),
                                preferred_element_type=F32)
            if masked:
                r = c * C + lax.broadcasted_iota(jnp.int32, (C, ncols), 0)
                cidx = lax.broadcasted_iota(jnp.int32, (C, ncols), 1)
                s = jnp.where(cidx <= r, s, NEG)
            dp = lax.dot_general(do_ref[rows, :], v_ref[0:ncols, :], (((1,), (1,)), ((), ())),
                                 preferred_element_type=F32)
            return s, dp

        def update(c, ncols, s, dp):
            rows = pl.ds(c * C, C)
            p = jnp.exp2(s - lse_ref[0, rows, :])
            ds = (p * dp).astype(BF16)
            dv_sc[:, 0:ncols] += jnp.dot(do_ref[rows, 0:MLA_V].T, p.astype(BF16), preferred_element_type=F32)
            dk_sc[:, 0:ncols] += jnp.dot(q_ref[rows, 0:MLA_QK].T, ds, preferred_element_type=F32)
            row0 = pl.multiple_of(qb * T + c * C, C)
            dq_ref[pl.ds(row0, C), :] += jnp.dot(ds, k_ref[0:ncols, :], preferred_element_type=F32)

        def tile_step(ncols_of, masked):
            cur = logits(0, ncols_of(0), masked)
            for c in range(nc):
                nxt = logits(c + 1, ncols_of(c + 1), masked) if c + 1 < nc else None
                update(c, ncols_of(c), *cur)
                cur = nxt

        @pl.when(qq > kj)
        def _():
            tile_step(lambda c: T, False)

        @pl.when(qq == kj)
        def _():
            tile_step(lambda c: (c + 1) * C, True)

        @pl.when(qq == nt - 1)
        def _():
            dk_ref[:, 0:MLA_QK] = dk_sc[...].T * math.log(2.0)
            dk_ref[:, MLA_QK:] = jnp.zeros((T, HEAD_PAD - MLA_QK), F32)
            dv_ref[:, 0:MLA_V] = dv_sc[...].T
            dv_ref[:, MLA_V:] = jnp.zeros((T, HEAD_PAD - MLA_V), F32)

    qrow = pl.BlockSpec((T, HEAD_PAD), lambda h, t, kt, qt: (qt[t], h))
    krow = pl.BlockSpec((T, HEAD_PAD), lambda h, t, kt, qt: (kt[t], h))
    qcol = pl.BlockSpec((1, T, 1), lambda h, t, kt, qt: (h, qt[t], 0))
    grid_spec = pltpu.PrefetchScalarGridSpec(
        num_scalar_prefetch=2, grid=(heads, len(pairs)),
        in_specs=[qrow, krow, krow, qrow, qcol],
        out_specs=[pl.BlockSpec((S, HEAD_PAD), lambda h, t, kt, qt: (0, h)), krow, krow],
        scratch_shapes=[pltpu.VMEM((MLA_QK, T), F32), pltpu.VMEM((MLA_V, T), F32)])
    return pl.pallas_call(
        body, name=name, grid_spec=grid_spec,
        out_shape=[jax.ShapeDtypeStruct((S, heads * HEAD_PAD), F32)] * 3,
        compiler_params=pltpu.CompilerParams(dimension_semantics=("arbitrary", "arbitrary"),
                                             vmem_limit_bytes=MLA_BWD_VMEM_LIMIT),
    )(k_tab, q_tab, q_arr, k_arr, v_arr, do_arr, lse)


SWA_R = SWA_HEADS // SWA_KV_HEADS
SWA_SCALE = SWA_HD ** -0.5
SWA_Q0, SWA_K0, SWA_V0 = 0, 12, 14


def _swa_specs(tq):
    nsb = tq // WINDOW
    return [
        pl.BlockSpec((tq, SWA_R * HEAD_PAD), lambda g, i: (i, g)),
        pl.BlockSpec((tq, HEAD_PAD), lambda g, i: (i, SWA_K0 + g)),
        pl.BlockSpec((WINDOW, HEAD_PAD), lambda g, i: (jnp.maximum(nsb * i - 1, 0), SWA_K0 + g)),
        pl.BlockSpec((tq, HEAD_PAD), lambda g, i: (i, SWA_V0 + g)),
        pl.BlockSpec((WINDOW, HEAD_PAD), lambda g, i: (jnp.maximum(nsb * i - 1, 0), SWA_V0 + g)),
        pl.BlockSpec((SWA_R, WINDOW, 2 * WINDOW), lambda g, i: (g, 0, 0)),
        pl.BlockSpec((SWA_R, 8, 128), lambda g, i: (g, 0, 0)),
    ]


def _swa_block(i, sb, q_ref, kc_ref, kp_ref, vc_ref, vp_ref, bias, sink):
    rows = slice(sb * WINDOW, (sb + 1) * WINDOW)
    qs = jnp.concatenate([q_ref[rows, hh * HEAD_PAD:(hh + 1) * HEAD_PAD] for hh in range(SWA_R)], axis=0)
    if sb == 0:
        kp, vp = kp_ref[...], vp_ref[...]
    else:
        prev = slice((sb - 1) * WINDOW, sb * WINDOW)
        kp, vp = kc_ref[prev, :], vc_ref[prev, :]
    kk = jnp.concatenate([kp, kc_ref[rows, :]], axis=0)
    vv = jnp.concatenate([vp, vc_ref[rows, :]], axis=0)
    s = lax.dot_general(qs, kk, (((1,), (1,)), ((), ())), preferred_element_type=F32) * SWA_SCALE + bias
    if sb == 0:
        col = lax.broadcasted_iota(jnp.int32, (1, 2 * WINDOW), 1)
        s = s + jnp.where((col < WINDOW) & (i == 0), NEG, 0.0)
    return rows, qs, kk, vv, s


def _stack_heads(ref, rows, lead=None):
    if lead is None:
        return jnp.concatenate([ref[rows, hh * HEAD_PAD:(hh + 1) * HEAD_PAD] for hh in range(SWA_R)], axis=0)
    return jnp.concatenate([ref[hh, rows, :] for hh in range(SWA_R)], axis=0)


def _swa_fwd(name, proj_b, bias, sinks, *, tq):
    S = proj_b.shape[0]
    tq = min(tq, S)
    nsb = tq // WINDOW

    def body(q_ref, kc_ref, kp_ref, vc_ref, vp_ref, bias_ref, sink_ref, o_ref, lse_ref):
        i = pl.program_id(1)
        bias_v = bias_ref[...].reshape(SWA_R * WINDOW, 2 * WINDOW)
        sink = jnp.concatenate([jnp.zeros((WINDOW, 1), F32) + sink_ref[hh, 0:1, 0:1] for hh in range(SWA_R)], axis=0)
        ahead = [_swa_block(i, sb, q_ref, kc_ref, kp_ref, vc_ref, vp_ref, bias_v, sink)
                 for sb in range(min(SWA_AHEAD, nsb))]
        for sb in range(nsb):
            rows, _, _, vv, s = ahead.pop(0)
            if sb + SWA_AHEAD < nsb:
                ahead.append(_swa_block(i, sb + SWA_AHEAD, q_ref, kc_ref, kp_ref, vc_ref, vp_ref, bias_v, sink))
            m = jnp.maximum(jnp.max(s, axis=1, keepdims=True), sink)
            p = jnp.exp(s - m)
            l = jnp.sum(p, axis=1, keepdims=True) + jnp.exp(sink - m)
            o = jnp.dot(p.astype(BF16), vv, preferred_element_type=F32) / l
            lse_v = m + jnp.log(l)
            for hh in range(SWA_R):
                o_ref[rows, hh * HEAD_PAD:(hh + 1) * HEAD_PAD] = o[hh * WINDOW:(hh + 1) * WINDOW].astype(o_ref.dtype)
                lse_ref[hh, rows, :] = lse_v[hh * WINDOW:(hh + 1) * WINDOW]

    return pl.pallas_call(
        body, name=name, grid=(SWA_KV_HEADS, S // tq),
        in_specs=_swa_specs(tq),
        out_specs=[pl.BlockSpec((tq, SWA_R * HEAD_PAD), lambda g, i: (i, g)),
                   pl.BlockSpec((SWA_R, tq, 1), lambda g, i: (g, i, 0))],
        out_shape=[jax.ShapeDtypeStruct((S, SWA_HEADS * HEAD_PAD), BF16),
                   jax.ShapeDtypeStruct((SWA_HEADS, S, 1), F32)],
        compiler_params=_cparams("parallel", "parallel"),
    )(proj_b, proj_b, proj_b, proj_b, proj_b, bias, sinks)


def _swa_bwd(name, proj_b, bias, sinks, o, do, lse, *, tq):
    S = proj_b.shape[0]
    tq = min(tq, S)
    nsb = tq // WINDOW
    nq = S // tq

    def body(q_ref, kc_ref, kp_ref, vc_ref, vp_ref, bias_ref, sink_ref, o_ref, do_ref, lse_ref,
             dq_ref, dk_ref, dv_ref, dke_ref, dve_ref, dbias_ref, dsink_ref):
        i = pl.program_id(1)

        @pl.when(i == 0)
        def _():
            dbias_ref[...] = jnp.zeros(dbias_ref.shape, F32)
            dsink_ref[...] = jnp.zeros(dsink_ref.shape, F32)

        bias_v = bias_ref[...].reshape(SWA_R * WINDOW, 2 * WINDOW)
        sink = jnp.concatenate([jnp.zeros((WINDOW, 1), F32) + sink_ref[hh, 0:1, 0:1] for hh in range(SWA_R)], axis=0)
        dk_own, dv_own, dk_prev, dv_prev = [], [], [], []
        dbias_acc = jnp.zeros((SWA_R * WINDOW, 2 * WINDOW), F32)
        def block(sb):
            rows, qs, kk, vv, s = _swa_block(i, sb, q_ref, kc_ref, kp_ref, vc_ref, vp_ref, bias_v, sink)
            do_s = _stack_heads(do_ref, rows)
            dp = lax.dot_general(do_s, vv, (((1,), (1,)), ((), ())), preferred_element_type=F32)
            return rows, qs, kk, do_s, s, dp

        nxt = block(0)
        for sb in range(nsb):
            rows, qs, kk, do_s, s, dp = nxt
            if sb + 1 < nsb:
                nxt = block(sb + 1)
            lse_v = _stack_heads(lse_ref, rows, lead=True)
            delta = jnp.sum(do_s.astype(F32) * _stack_heads(o_ref, rows).astype(F32), axis=1, keepdims=True)
            p = jnp.exp(s - lse_v)
            dsp = p * (dp - delta)
            dbias_acc = dbias_acc + dsp
            ds = (dsp * SWA_SCALE).astype(BF16)
            dq = jnp.dot(ds, kk, preferred_element_type=F32)
            dkk = jnp.dot(qs.T, ds, preferred_element_type=F32)
            dvv = jnp.dot(do_s.T, p.astype(BF16), preferred_element_type=F32)
            dk_prev.append(dkk[:, :WINDOW].T)
            dk_own.append(dkk[:, WINDOW:].T)
            dv_prev.append(dvv[:, :WINDOW].T)
            dv_own.append(dvv[:, WINDOW:].T)
            psink = jnp.exp(sink - lse_v) * delta
            for hh in range(SWA_R):
                hrows = slice(hh * WINDOW, (hh + 1) * WINDOW)
                dq_ref[rows, hh * HEAD_PAD:(hh + 1) * HEAD_PAD] = dq[hrows].astype(dq_ref.dtype)
                dsink_ref[hh] += jnp.zeros((8, 128), F32) - jnp.sum(psink[hrows])
        dbias_ref[...] += dbias_acc.reshape(SWA_R, WINDOW, 2 * WINDOW)
        for sb in range(nsb):
            rows = slice(sb * WINDOW, (sb + 1) * WINDOW)
            if sb + 1 < nsb:
                dk_ref[rows, :] = dk_own[sb] + dk_prev[sb + 1]
                dv_ref[rows, :] = dv_own[sb] + dv_prev[sb + 1]
            else:
                dk_ref[rows, :] = dk_own[sb]
                dv_ref[rows, :] = dv_own[sb]
        dke_ref[...] = dk_prev[0]
        dve_ref[...] = dv_prev[0]

    in_specs = _swa_specs(tq) + [
        pl.BlockSpec((tq, SWA_R * HEAD_PAD), lambda g, i: (i, g)),
        pl.BlockSpec((tq, SWA_R * HEAD_PAD), lambda g, i: (i, g)),
        pl.BlockSpec((SWA_R, tq, 1), lambda g, i: (g, i, 0)),
    ]
    kv_blk = pl.BlockSpec((tq, HEAD_PAD), lambda g, i: (i, g))
    edge_blk = pl.BlockSpec((WINDOW, HEAD_PAD), lambda g, i: (i, g))
    return pl.pallas_call(
        body, name=name, grid=(SWA_KV_HEADS, nq),
        in_specs=in_specs,
        out_specs=[pl.BlockSpec((tq, SWA_R * HEAD_PAD), lambda g, i: (i, g)), kv_blk, kv_blk, edge_blk, edge_blk,
                   pl.BlockSpec((SWA_R, WINDOW, 2 * WINDOW), lambda g, i: (g, 0, 0)),
                   pl.BlockSpec((SWA_R, 8, 128), lambda g, i: (g, 0, 0))],
        out_shape=[jax.ShapeDtypeStruct((S, SWA_HEADS * HEAD_PAD), BF16),
                   jax.ShapeDtypeStruct((S, SWA_KV_HEADS * HEAD_PAD), F32),
                   jax.ShapeDtypeStruct((S, SWA_KV_HEADS * HEAD_PAD), F32),
                   jax.ShapeDtypeStruct((nq * WINDOW, SWA_KV_HEADS * HEAD_PAD), F32),
                   jax.ShapeDtypeStruct((nq * WINDOW, SWA_KV_HEADS * HEAD_PAD), F32),
                   jax.ShapeDtypeStruct((SWA_HEADS, WINDOW, 2 * WINDOW), F32),
                   jax.ShapeDtypeStruct((SWA_HEADS, 8, 128), F32)],
        compiler_params=_cparams("arbitrary", "arbitrary"),
    )(proj_b, proj_b, proj_b, proj_b, proj_b, bias, sinks, o, do, lse)


def _dproj_b(name, dq_swa, dq_mem, dk, dv, dk_edge, dv_edge, *, tq):
    S = dq_swa.shape[0]
    tq = min(tq, S)
    nq = S // tq

    def body(dqs_ref, dqm_ref, dk_ref, dv_ref, dke_ref, dve_ref, o_ref):
        i = pl.program_id(0)
        o_ref[:, 0:1024] = dqs_ref[...]
        o_ref[:, 1024:1536] = dqm_ref[...].astype(o_ref.dtype)
        o_ref[:, 1536:1792] = dk_ref[...].astype(o_ref.dtype)
        o_ref[:, 1792:2048] = dv_ref[...].astype(o_ref.dtype)

        @pl.when(i < nq - 1)
        def _():
            last = slice(tq - WINDOW, tq)
            o_ref[last, 1536:1792] = (dk_ref[last, :] + dke_ref[...]).astype(o_ref.dtype)
            o_ref[last, 1792:2048] = (dv_ref[last, :] + dve_ref[...]).astype(o_ref.dtype)

    edge = pl.BlockSpec((WINDOW, SWA_KV_HEADS * HEAD_PAD), lambda i: (jnp.minimum(i + 1, nq - 1), 0))
    return pl.pallas_call(
        body, name=name, grid=(nq,),
        in_specs=[pl.BlockSpec((tq, 1024), lambda i: (i, 0)), pl.BlockSpec((tq, 512), lambda i: (i, 0)),
                  pl.BlockSpec((tq, 256), lambda i: (i, 0)), pl.BlockSpec((tq, 256), lambda i: (i, 0)), edge, edge],
        out_specs=pl.BlockSpec((tq, 2048), lambda i: (i, 0)),
        out_shape=jax.ShapeDtypeStruct((S, 2048), BF16),
        compiler_params=_cparams("parallel"),
    )(dq_swa, dq_mem, dk, dv, dk_edge, dv_edge)


def _gather_forwarded(name, block):
    def body(x_ref, out_ref, send_sems, recv_sems, local_sem):
        x, y, c = lax.axis_index("x"), lax.axis_index("y"), lax.axis_index("c")
        me, sibling = (x, y, c), (x, y, 1 - c)
        chips = [(1 - x, y), (x, 1 - y), (1 - x, 1 - y)]

        def slot(px, py, pc):
            return out_ref.at[4 * px + 2 * py + pc]

        def copy(k, blk, to, src=None):
            return pltpu.make_async_remote_copy(
                src_ref=slot(*blk) if src is None else src, dst_ref=slot(*blk),
                send_sem=send_sems.at[k], recv_sem=recv_sems.at[k],
                device_id=to, device_id_type=pl.DeviceIdType.MESH)

        mine = pltpu.make_async_copy(x_ref, slot(*me), local_sem)
        mine.start()
        first = [copy(0, me, sibling, src=x_ref)]
        first += [copy(1 + j, me, (*chip, c), src=x_ref) for j, chip in enumerate(chips)]
        for cp in first:
            cp.start()
        passed = [copy(4 + j, (*chip, c), sibling) for j, chip in enumerate(chips)]
        for j, chip in enumerate(chips):
            copy(1 + j, (*chip, c), me).wait_recv()
            passed[j].start()
        copy(0, sibling, me).wait_recv()
        for j, chip in enumerate(chips):
            copy(4 + j, (*chip, 1 - c), me).wait_recv()
        for cp in first + passed:
            cp.wait_send()
        mine.wait()

    return pl.pallas_call(
        body, name=name,
        in_specs=[pl.BlockSpec(memory_space=pl.ANY)],
        out_specs=pl.BlockSpec(memory_space=pl.ANY),
        out_shape=jax.ShapeDtypeStruct((N_DEV,) + tuple(block.shape), block.dtype),
        scratch_shapes=[pltpu.SemaphoreType.DMA((N_DEV - 1,)), pltpu.SemaphoreType.DMA((N_DEV - 1,)),
                        pltpu.SemaphoreType.DMA(())],
    )(block)


def _sibling_swap(name, block):
    def body(x_ref, out_ref, send_sem, recv_sem):
        x, y, c = lax.axis_index("x"), lax.axis_index("y"), lax.axis_index("c")
        cp = pltpu.make_async_remote_copy(src_ref=x_ref, dst_ref=out_ref, send_sem=send_sem, recv_sem=recv_sem,
                                          device_id=(x, y, 1 - c), device_id_type=pl.DeviceIdType.MESH)
        cp.start()
        cp.wait()

    return pl.pallas_call(
        body, name=name,
        in_specs=[pl.BlockSpec(memory_space=pl.ANY)],
        out_specs=pl.BlockSpec(memory_space=pl.ANY),
        out_shape=jax.ShapeDtypeStruct(block.shape, block.dtype),
        scratch_shapes=[pltpu.SemaphoreType.DMA(()), pltpu.SemaphoreType.DMA(())],
    )(block)


def _chip_exchange(name, send):
    def body(send_ref, recv_ref, send_sems, recv_sems, local_sem):
        x, y, c = lax.axis_index("x"), lax.axis_index("y"), lax.axis_index("c")
        me = 2 * x + y
        own = pltpu.make_async_copy(send_ref.at[me], recv_ref.at[me], local_sem)
        own.start()
        copies = []
        for k in range(1, 4):
            px = 1 - x if (k >> 1) & 1 else x
            py = 1 - y if k & 1 else y
            peer = 2 * px + py
            out = pltpu.make_async_remote_copy(
                src_ref=send_ref.at[peer], dst_ref=recv_ref.at[me],
                send_sem=send_sems.at[k - 1], recv_sem=recv_sems.at[k - 1],
                device_id=(px, py, c), device_id_type=pl.DeviceIdType.MESH)
            out.start()
            back = pltpu.make_async_remote_copy(
                src_ref=send_ref.at[me], dst_ref=recv_ref.at[peer],
                send_sem=send_sems.at[k - 1], recv_sem=recv_sems.at[k - 1],
                device_id=(px, py, c), device_id_type=pl.DeviceIdType.MESH)
            copies.append((out, back))
        for out, back in copies:
            out.wait_send()
            back.wait_recv()
        own.wait()

    return pl.pallas_call(
        body, name=name,
        in_specs=[pl.BlockSpec(memory_space=pl.ANY)],
        out_specs=pl.BlockSpec(memory_space=pl.ANY),
        out_shape=jax.ShapeDtypeStruct(send.shape, send.dtype),
        scratch_shapes=[pltpu.SemaphoreType.DMA((3,)), pltpu.SemaphoreType.DMA((3,)), pltpu.SemaphoreType.DMA(())],
    )(send)


def _reduce_scatter(parts):
    lanes = 128
    c = lax.axis_index("c")

    def core_half(core):
        return jnp.concatenate(
            [lax.dynamic_index_in_dim(p.reshape(4, 2, p.shape[1], lanes), core, axis=1, keepdims=False)
             for p in parts], axis=1)

    mine = core_half(c)
    rows = mine.shape[1]
    mine = mine.reshape(4 * rows, lanes)
    theirs = core_half(1 - c).reshape(4 * rows, lanes)
    from_sibling = _sibling_swap("grads_to_sibling", theirs)
    tm = max(t for t in range(16, 4 * ADAM_TM + 1, 16) if (4 * rows) % t == 0)
    chip_sum = _rowwise("grads_chip_sum", lambda a, b: (a.astype(F32) + b.astype(F32),),
                        [_rows(mine), _rows(from_sibling)], [((4 * rows, lanes), BF16, "rows")],
                        rows=4 * rows, tm=tm)[0]
    return _chip_exchange("scatter_grads", chip_sum.reshape(4, rows, lanes))


def _adam(name, recv, w, m, v, *, tm=None):
    R = w.shape[0]
    n_parts = recv.shape[0]
    tm = max(t for t in range(8, min(tm or ADAM_TM, R) + 1, 8) if R % t == 0)
    c1 = 1.0 / (1.0 - ADAM_B1 ** ADAM_STEP)
    c2 = 1.0 / (1.0 - ADAM_B2 ** ADAM_STEP)

    def body(r_ref, w_ref, m_ref, v_ref, g_ref, d_ref, nm_ref, nv_ref):
        g = r_ref[0].astype(F32)
        for j in range(1, n_parts):
            g = g + r_ref[j].astype(F32)
        wv = w_ref[...]
        nm = ADAM_B1 * m_ref[...] + (1.0 - ADAM_B1) * g
        nv = ADAM_B2 * v_ref[...] + (1.0 - ADAM_B2) * (g * g)
        m_hat = nm * c1
        v_hat = nv * c2
        g_ref[...] = g
        d_ref[...] = -ADAM_LR * (m_hat / (jnp.sqrt(v_hat) + ADAM_EPS) + ADAM_WD * wv)
        nm_ref[...] = nm
        nv_ref[...] = nv

    row = pl.BlockSpec((tm, 128), lambda i: (i, 0))
    return pl.pallas_call(
        body, name=name, grid=(R // tm,),
        in_specs=[pl.BlockSpec((n_parts, tm, 128), lambda i: (0, i, 0)), row, row, row],
        out_specs=[row, row, row, row],
        out_shape=[jax.ShapeDtypeStruct((R, 128), F32)] * 4,
        compiler_params=_cparams("parallel"),
    )(recv, w, m, v)


def _pack_rows(arrs):
    return jnp.concatenate([a.reshape(-1, 128) for a in arrs], axis=0)


def _unstack(g, shape, axis):
    t = jnp.moveaxis(g, 0, axis)
    return t.reshape(shape)


def _restack(full, axis):
    s = full.shape
    t = full.reshape(s[:axis] + (N_DEV, s[axis] // N_DEV) + s[axis + 1:])
    return jnp.moveaxis(t, axis, 0)


def _pad_heads(w, heads, hd, axis):
    s = w.shape
    t = w.reshape(s[:axis] + (heads, hd) + s[axis + 1:])
    pad = [(0, 0)] * t.ndim
    pad[axis + 1] = (0, HEAD_PAD - hd)
    t = jnp.pad(t, pad)
    return t.reshape(s[:axis] + (heads * HEAD_PAD,) + s[axis + 1:])


def _unpad_heads(w, heads, hd, axis):
    s = w.shape
    t = w.reshape(s[:axis] + (heads, HEAD_PAD) + s[axis + 1:])
    t = lax.slice_in_dim(t, 0, hd, axis=axis + 1)
    return t.reshape(s[:axis] + (heads * hd,) + s[axis + 1:])


def _layer_weights(full, l):
    w_in = full["w_in"][l]
    cq, kva, qs, ks, vs, qm, gates = (w_in[:, 0:256], w_in[:, 256:416], w_in[:, 416:928], w_in[:, 928:1056],
                                       w_in[:, 1056:1184], w_in[:, 1184:1696], w_in[:, 1696:4768])
    wa = jnp.concatenate([gates, cq, jnp.pad(kva, ((0, 0), (0, 96)))], axis=1)
    wb = jnp.concatenate([_pad_heads(qs, SWA_HEADS, SWA_HD, 1), qm, _pad_heads(ks, SWA_KV_HEADS, SWA_HD, 1),
                          _pad_heads(vs, SWA_KV_HEADS, SWA_HD, 1)], axis=1)
    wuq = _pad_heads(full["w_uq"][l], MLA_HEADS, MLA_NOPE + MLA_ROPE, 1)
    ukv = full["w_ukv"][l].reshape(MLA_KV_LORA, MLA_HEADS, MLA_NOPE + MLA_V)
    wuk = _pad_heads(ukv[:, :, :MLA_NOPE].reshape(MLA_KV_LORA, -1), MLA_HEADS, MLA_NOPE, 1)
    wuv = _pad_heads(ukv[:, :, MLA_NOPE:].reshape(MLA_KV_LORA, -1), MLA_HEADS, MLA_V, 1)
    wo_mla = _pad_heads(full["w_o_mla"][l], MLA_HEADS, MLA_V, 0)
    wo_swa = _pad_heads(full["w_o_swa"][l], SWA_HEADS, SWA_HD, 0)
    wo_mem = full["w_o_mem"][l]
    w = dict(wag=wa[:, :3072], wat=wa[:, 3072:], wb=wb, wuq=wuq, wuk=wuk, wuv=wuv, wo_mla=wo_mla, wo_swa=wo_swa,
             wo_mem=wo_mem, wmem=full["w_mem_kv"][l], wout=full["w_out"][l], wup=full["w_up"][l],
             wdown=full["w_down"][l])
    w.update({k + "_t": v.T for k, v in w.items()})
    return w


def _layer_weight_grads(g):
    dwa_g, dwa_t, dwb = g["wag"], g["wat"], g["wb"]
    d_in = jnp.concatenate([
        dwa_t[:, 0:256], dwa_t[:, 256:416],
        _unpad_heads(dwb[:, 0:1024], SWA_HEADS, SWA_HD, 1),
        _unpad_heads(dwb[:, 1536:1792], SWA_KV_HEADS, SWA_HD, 1),
        _unpad_heads(dwb[:, 1792:2048], SWA_KV_HEADS, SWA_HD, 1),
        dwb[:, 1024:1536], dwa_g], axis=1)
    duk = _unpad_heads(g["wuk"], MLA_HEADS, MLA_NOPE, 1).reshape(MLA_KV_LORA, MLA_HEADS, MLA_NOPE)
    duv = _unpad_heads(g["wuv"], MLA_HEADS, MLA_V, 1).reshape(MLA_KV_LORA, MLA_HEADS, MLA_V)
    return dict(
        w_in=d_in,
        w_uq=_unpad_heads(g["wuq"], MLA_HEADS, MLA_NOPE + MLA_ROPE, 1),
        w_ukv=jnp.concatenate([duk, duv], axis=2).reshape(MLA_KV_LORA, -1),
        w_mem_kv=g["wmem"],
        w_o_mla=_unpad_heads(g["wo_mla"], MLA_HEADS, MLA_V, 0),
        w_o_swa=_unpad_heads(g["wo_swa"], SWA_HEADS, SWA_HD, 0),
        w_o_mem=g["wo_mem"], w_out=g["wout"], w_up=g["wup"], w_down=g["wdown"])


def _rope_tables(S):
    pos = jnp.arange(S, dtype=F32)
    inv = 1.0 / (ROPE_THETA ** (jnp.arange(0, MLA_ROPE, 2, dtype=F32) / MLA_ROPE))
    ang = pos[:, None] * inv[None, :]
    cos, sin = jnp.cos(ang), jnp.sin(ang)
    z16 = jnp.zeros((S, 16), F32)
    z32 = jnp.zeros((S, 32), F32)
    c = jnp.concatenate([jnp.ones((S, 64), F32), cos, cos, z32], axis=1)
    ck = jnp.concatenate([jnp.zeros((S, 64), F32), cos, cos, z32], axis=1)
    s1 = jnp.concatenate([jnp.zeros((S, 80), F32), sin, z32], axis=1)
    s2 = jnp.concatenate([jnp.zeros((S, 64), F32), -sin, z16, z32], axis=1)
    return c, ck, s1, s2


def _t5_bucket(dist):
    n = jnp.maximum(dist, 0)
    max_exact = REL_BUCKETS // 2
    nf = jnp.maximum(n, 1).astype(F32)
    large = max_exact + (jnp.log(nf / max_exact) / math.log(REL_MAX_DIST / max_exact)
                         * (REL_BUCKETS - max_exact)).astype(jnp.int32)
    large = jnp.minimum(large, REL_BUCKETS - 1)
    return jnp.where(n < max_exact, n, large)


def _bias_onehot():
    qi = jnp.arange(WINDOW)[:, None]
    kj = jnp.arange(2 * WINDOW)[None, :]
    dist = qi + WINDOW - kj
    valid = (dist >= 0) & (dist < WINDOW)
    bucket = _t5_bucket(dist)
    onehot = (bucket[None] == jnp.arange(REL_BUCKETS)[:, None, None]) & valid[None]
    return (onehot.reshape(REL_BUCKETS, -1).astype(F32),
            jnp.where(valid, 0.0, NEG).astype(F32).reshape(1, -1))


def _rstd(x):
    return lax.rsqrt(jnp.mean(x * x, axis=-1, keepdims=True) + EPS)


def _norm_bwd(dh, x, g):
    r = _rstd(x)
    xh = x * r
    w = dh * g
    dx = r * (w - xh * jnp.mean(w * xh, axis=-1, keepdims=True))
    return dx, jnp.sum(dh * xh, axis=0, keepdims=True)


def _tile_lanes(t, n):
    return jnp.tile(t, (1, n // t.shape[1])) if n != t.shape[1] else t


def _rope_fwd(a, c, s1, s2):
    n = a.shape[1]
    return (a * _tile_lanes(c, n) + pltpu.roll(a, 16, 1) * _tile_lanes(s1, n)
            + pltpu.roll(a, n - 16, 1) * _tile_lanes(s2, n))


def _rope_bwd(d, c, s1, s2):
    n = d.shape[1]
    return (d * _tile_lanes(c, n) + pltpu.roll(d * _tile_lanes(s1, n), n - 16, 1)
            + pltpu.roll(d * _tile_lanes(s2, n), 16, 1))


def _sigmoid(x):
    return 1.0 / (1.0 + jnp.exp(-x))


def _rmsnorm(name, x, g, dtype):
    def fn(xv, gv):
        return ((xv * _rstd(xv)) * gv,)
    return _rowwise(name, fn, [_rows(x), _full(g)], [(x.shape, dtype, "rows")], rows=x.shape[0])[0]


def _residual_norm_bwd(name, dres, dh, x, g):
    def fn(dr, dhv, xv, gv):
        dx, dg = _norm_bwd(dhv, xv, gv)
        return dr + dx, dg
    return _rowwise(name, fn, [_rows(dres), _rows(dh), _rows(x), _full(g)],
                    [(x.shape, F32, "rows"), (g.shape, F32, "acc")], rows=x.shape[0])


def _norm_bwd_epilogue(dh, dres, x, g):
    dx, dg = _norm_bwd(dh, x, g)
    return dres + dx, dg


def _add_and_norm(acc, r, g):
    xs = acc + r
    return xs, xs * _rstd(xs) * g


def _layer_fwd(l, x, h, mem, w, p, next_norm, tabs, swa_bias, S):
    c, ck, s1, s2 = tabs
    n = f"l{l}_"
    gates = _mm(n + "proj_gates", h, w["wag"], [BF16], tm=MM_TM_BF16)
    proj_a = _mm(n + "proj_tail", h, w["wat"], [F32])
    proj_b = _mm(n + "proj_b", h, w["wb"], [BF16], tm=MM_TM_BF16)

    def prep(cq, kva, qn, kvn, ckv, s1v, s2v):
        cqn = cq * _rstd(cq) * qn
        ckv_ = kva[:, :128]
        ckvn = ckv_ * _rstd(ckv_) * kvn
        pe = pltpu.roll(kva[:, 128:], 64, 1)
        return cqn, ckvn, _rope_fwd(pe, ckv, s1v, s2v)

    cqn, ckvn, kpe = _rowwise(
        n + "mla_prep", prep,
        [_rows(proj_a, 256, 0), _rows(proj_a, 256, 1), _full(p["mla_q_norm"]), _full(p["mla_kv_norm"]),
         _rows(ck), _rows(s1), _rows(s2)],
        [((S, 256), BF16, "rows"), ((S, 128), BF16, "rows"), ((S, 128), F32, "rows")], rows=S)

    q_mla = _mm(n + "q_mla", cqn, w["wuq"], [BF16],
                epi=lambda acc, cv, s1v, s2v: (_rope_fwd(acc, cv, s1v, s2v) * (MLA_SCALE * LOG2E),),
                extras=[(c, "m"), (s1, "m"), (s2, "m")])
    k_mla = _mm(n + "k_mla", ckvn, w["wuk"], [BF16],
                epi=lambda acc, kp: (acc + _tile_lanes(kp, acc.shape[1]),), extras=[(kpe, "m")])
    den = ((jnp.arange(MLA_HEADS * HEAD_PAD) % HEAD_PAD) // 2 == DEN_LANE // 2).astype(F32)[None]
    v_mla = _mm(n + "v_mla", ckvn, w["wuv"], [BF16], epi=lambda acc, dv: (acc + dv,), extras=[(den, "n")])
    o_mla, lse_mla = _causal_fwd(n + "mla_fwd", q_mla, k_mla, v_mla, heads=MLA_HEADS, tile=MLA_TILE,
                                 chunk=MLA_CHUNK_FWD)
    o_swa, lse_swa = _swa_fwd(n + "swa_fwd", proj_b, swa_bias, p["sinks"], tq=SWA_TQ)
    mn = _rmsnorm(n + "mem_norm", mem, p["mem_norm"], BF16)
    kvm = _mm(n + "kv_mem", mn, w["wmem"], [BF16])
    o_mem, lse_mem = _mem_fwd(n + "mem_fwd", proj_b, kvm, tq=MEM_TQ, chunk=MEM_CHUNK)
    t0 = _mm(n + "t_mla", o_mla, w["wo_mla"], [BF16], tm=MM_TM_BF16)
    t1 = _mm(n + "t_swa", o_swa, w["wo_swa"], [BF16], tm=MM_TM_BF16)
    t2 = _mm(n + "t_mem", o_mem, w["wo_mem"], [BF16], tm=MM_TM_BF16)

    def merge(g0, g1, g2, bg, a0, a1, a2):
        y = (_sigmoid(g0 + bg[:, 0:1024]) * a0 + _sigmoid(g1 + bg[:, 1024:2048]) * a1
             + _sigmoid(g2 + bg[:, 2048:3072]) * a2)
        return (y,)

    y = _rowwise(n + "merge", merge,
                 [_rows(gates, 1024, 0), _rows(gates, 1024, 1), _rows(gates, 1024, 2), _full(p["b_gate"]),
                  _rows(t0), _rows(t1), _rows(t2)], [((S, D_MODEL), BF16, "rows")], rows=S)[0]
    x1, h2 = _mm(n + "out_proj", y, w["wout"], [F32, BF16], epi=_add_and_norm,
                 extras=[(x, "mn"), (p["mlp_norm"], "n")], tn=D_MODEL)
    act = _mm(n + "mlp_up", h2, w["wup"], [BF16], epi=lambda acc: (jnp.square(jnp.maximum(acc, 0.0)),),
              tm=MM_TM_BF16)
    if next_norm is None:
        x2 = _mm(n + "mlp_down", act, w["wdown"], [F32], epi=lambda acc, r: (acc + r,), extras=[(x1, "mn")],
                 tk=MM_TK_DEEP)
        h_next = None
    else:
        x2, h_next = _mm(n + "mlp_down", act, w["wdown"], [F32, BF16], epi=_add_and_norm,
                         extras=[(x1, "mn"), (next_norm, "n")], tn=D_MODEL, tk=MM_TK_DEEP)
    saved = dict(x=x, h=h, gates=gates, proj_a=proj_a, proj_b=proj_b, cqn=cqn, ckvn=ckvn, q_mla=q_mla, k_mla=k_mla, v_mla=v_mla,
                 o_mla=o_mla, lse_mla=lse_mla, o_swa=o_swa, lse_swa=lse_swa, mn=mn, kvm=kvm, o_mem=o_mem,
                 lse_mem=lse_mem, t0=t0, t1=t1, t2=t2, y=y, x1=x1, h2=h2, act=act)
    return x2, h_next, saved


def _layer_bwd(l, dx2, mem, w, p, tabs, swa_bias, sv, S):
    c, ck, s1, s2 = tabs
    n = f"l{l}_b_"
    gw = {}
    gs = {}
    du = _mm(n + "d_act", dx2, w["wdown_t"], [BF16],
             epi=lambda acc, av: (acc * (2.0 * jnp.sqrt(av.astype(F32))),), extras=[(sv["act"], "mn")],
             tm=MM_TM_BF16)
    gw["wdown"] = _mm_tn(n + "g_wdown", sv["act"], dx2)
    gw["wup"] = _mm_tn(n + "g_wup", sv["h2"], du)
    dx1, gs["mlp_norm"] = _mm(n + "d_h2", du, w["wup_t"], [F32], epi=_norm_bwd_epilogue,
                              extras=[(dx2, "mn"), (sv["x1"], "mn"), (p["mlp_norm"], "n")], tn=D_MODEL, col_sums=1)
    gw["wout"] = _mm_tn(n + "g_wout", sv["y"], dx1)
    dy = _mm(n + "d_y", dx1, w["wout_t"], [F32], tm=MM_TM_BF16)

    def merge_bwd(dyv, g0, g1, g2, bg, a0, a1, a2):
        outs, dgs = [], []
        for b, (gv, av) in enumerate(((g0, a0), (g1, a1), (g2, a2))):
            sg = _sigmoid(gv + bg[:, b * 1024:(b + 1) * 1024])
            outs.append(dyv * sg)
            dgs.append(dyv * av * sg * (1.0 - sg))
        dg = jnp.concatenate(dgs, axis=1)
        return outs[0], outs[1], outs[2], dg, jnp.sum(dg, axis=0, keepdims=True)

    pa = sv["proj_a"]
    gt = sv["gates"]
    dt0, dt1, dt2, dgates, gs["b_gate"] = _rowwise(
        n + "merge", merge_bwd,
        [_rows(dy), _rows(gt, 1024, 0), _rows(gt, 1024, 1), _rows(gt, 1024, 2), _full(p["b_gate"]),
         _rows(sv["t0"]), _rows(sv["t1"]), _rows(sv["t2"])],
        [((S, D_MODEL), BF16, "rows")] * 3 + [((S, 3 * D_MODEL), BF16, "rows"), ((1, 3 * D_MODEL), F32, "acc")],
        rows=S)
    gw["wo_mla"] = _mm_tn(n + "g_wo_mla", sv["o_mla"], dt0)
    gw["wo_swa"] = _mm_tn(n + "g_wo_swa", sv["o_swa"], dt1)
    gw["wo_mem"] = _mm_tn(n + "g_wo_mem", sv["o_mem"], dt2)
    do_mla = _mm(n + "d_o_mla", dt0, w["wo_mla_t"], [BF16], epi=lambda acc, ov: (_with_neg_delta(acc, ov),),
                 extras=[(sv["o_mla"], "mn")], tn=MLA_HEADS * HEAD_PAD, tm=MM_TM_BF16)
    do_swa = _mm(n + "d_o_swa", dt1, w["wo_swa_t"], [BF16], tm=MM_TM_BF16)
    do_mem = _mm(n + "d_o_mem", dt2, w["wo_mem_t"], [BF16], tm=MM_TM_BF16)
    pb = sv["proj_b"]
    dq_mla, dk_mla, dv_mla = _causal_bwd(
        n + "mla_bwd", sv["q_mla"], sv["k_mla"], sv["v_mla"], do_mla, sv["lse_mla"], heads=MLA_HEADS,
        tile=MLA_TILE_BWD, chunk=MLA_CHUNK)
    dq_swa, dk_swa, dv_swa, dk_edge, dv_edge, dbias, dsink = _swa_bwd(
        n + "swa_bwd", pb, swa_bias, p["sinks"], sv["o_swa"], do_swa, sv["lse_swa"], tq=SWA_TQ)
    dq_mem, dkvm = _mem_bwd(n + "mem_bwd", pb, sv["kvm"], sv["o_mem"], do_mem, sv["lse_mem"], tq=MEM_TQ,
                            chunk=MEM_CHUNK)
    gs["dbias"] = dbias
    gs["sinks"] = dsink[:, 0, 0]
    gw["wmem"] = _mm_tn(n + "g_wmem", sv["mn"], dkvm)
    dmn = _mm(n + "d_mn", dkvm, w["wmem_t"], [F32])
    _, gs["mem_norm"] = _residual_norm_bwd(n + "mem_norm", dmn, dmn, mem, p["mem_norm"])
    dq_pre = _rowwise(n + "q_unrope", lambda d, cv, s1v, s2v: (_rope_bwd(d * MLA_SCALE, cv, s1v, s2v),),
                      [_rows(dq_mla), _rows(c), _rows(s1), _rows(s2)], [((S, 1024), BF16, "rows")], rows=S)[0]
    gw["wuq"] = _mm_tn(n + "g_wuq", sv["cqn"], dq_pre)
    gw["wuk"] = _mm_tn(n + "g_wuk", sv["ckvn"], dk_mla)
    gw["wuv"] = _mm_tn(n + "g_wuv", sv["ckvn"], dv_mla)
    dcqn = _mm(n + "d_cqn", dq_pre, w["wuq_t"], [F32])
    dckvn = _mm(n + "d_ckvn_k", dk_mla, w["wuk_t"], [F32])
    dckvn = _mm(n + "d_ckvn_v", dv_mla, w["wuv_t"], [F32], epi=lambda acc, r: (acc + r,), extras=[(dckvn, "mn")])

    def mla_norm_bwd(dcq_n, dckv_n, dk, cq, kva, qn, kvn, ckv, s1v, s2v):
        dcq, dqn = _norm_bwd(dcq_n, cq, qn)
        dckv, dkvn = _norm_bwd(dckv_n, kva[:, :128], kvn)
        dkpe = dk[:, 0:128]
        for hh in range(1, MLA_HEADS):
            dkpe = dkpe + dk[:, hh * 128:(hh + 1) * 128]
        dpe = pltpu.roll(_rope_bwd(dkpe, ckv, s1v, s2v), 64, 1)
        return jnp.concatenate([dcq, dckv, dpe], axis=1), dqn, dkvn

    dtail, gs["mla_q_norm"], gs["mla_kv_norm"] = _rowwise(
        n + "mla_norm", mla_norm_bwd,
        [_rows(dcqn), _rows(dckvn), _rows(dk_mla), _rows(pa, 256, 0), _rows(pa, 256, 1),
         _full(p["mla_q_norm"]), _full(p["mla_kv_norm"]), _rows(ck), _rows(s1), _rows(s2)],
        [((S, 512), BF16, "rows"), ((1, 256), F32, "acc"), ((1, 128), F32, "acc")], rows=S)

    dproj_b = _dproj_b(n + "dproj_b", dq_swa, dq_mem, dk_swa, dv_swa, dk_edge, dv_edge, tq=SWA_TQ)
    h = sv["h"]
    gw["wag"] = _mm_tn(n + "g_wa_gates", h, dgates)
    gw["wat"] = _mm_tn(n + "g_wa_tail", h, dtail)
    gw["wb"] = _mm_tn(n + "g_wb", h, dproj_b)
    dh = _mm(n + "d_h_gates", dgates, w["wag_t"], [F32], tm=MM_TM_BF16)
    dh = _mm(n + "d_h_tail", dtail, w["wat_t"], [F32], epi=lambda acc, r: (acc + r,), extras=[(dh, "mn")])
    dx, gs["attn_norm"] = _mm(n + "d_h_b", dproj_b, w["wb_t"], [F32],
                              epi=lambda acc, prev, dr, xv, gv: _norm_bwd_epilogue(acc + prev, dr, xv, gv),
                              extras=[(dh, "mn"), (dx1, "mn"), (sv["x"], "mn"), (p["attn_norm"], "n")],
                              tn=D_MODEL, tm=MM_TM // 2, col_sums=1)
    return dx, gw, gs


def _local_step(x, mem, loss_target, full, small):
    S = x.shape[0]
    tabs = _rope_tables(S)
    onehot, band = _bias_onehot()
    hi = lax.Precision.HIGHEST
    swa_bias = _mm("swa_bias", small["rel_bias"].T, onehot, [F32], epi=lambda acc, mk: (acc + mk,),
                   extras=[(band, "n")], cast=None, precision=hi, tn=8192).reshape(SWA_HEADS, WINDOW, 2 * WINDOW)
    ws, ps = [], []
    for l in range(DEPTH):
        ws.append(_layer_weights(full, l))
        ps.append(dict(
            attn_norm=small["attn_norm"][l][None], mem_norm=small["mem_norm"][l][None],
            b_gate=small["b_gate"][l][None], mla_q_norm=small["mla_q_norm"][l][None],
            mla_kv_norm=small["mla_kv_norm"][l][None], mlp_norm=small["mlp_norm"][l][None],
            sinks=jnp.broadcast_to(small["attn_sinks"][l][:, None, None], (SWA_HEADS, 8, 128))))
    saved = []
    xc = x
    hc = _rmsnorm("l0_attn_norm", x, ps[0]["attn_norm"], BF16)
    for l in range(DEPTH):
        next_norm = ps[l + 1]["attn_norm"] if l + 1 < DEPTH else None
        xc, hc, sv = _layer_fwd(l, xc, hc, mem, ws[l], ps[l], next_norm, tabs, swa_bias, S)
        saved.append(sv)

    fn_g = small["final_norm"][None]

    def loss_fn(xv, gv, tv):
        r = _rstd(xv)
        xh = xv * r
        err = xh * gv - tv
        dyv = err * (1.0 / D_MODEL)
        wv = dyv * gv
        dx = r * (wv - xh * jnp.mean(wv * xh, axis=-1, keepdims=True))
        part = 0.5 * jnp.sum(err * err) * (1.0 / D_MODEL)
        return dx, jnp.sum(dyv * xh, axis=0, keepdims=True), jnp.zeros((8, 128), F32) + part

    dx, g_final, loss_acc = _rowwise(
        "loss", loss_fn, [_rows(xc), _full(fn_g), _rows(loss_target)],
        [((S, D_MODEL), F32, "rows"), ((1, D_MODEL), F32, "acc"), ((8, 128), F32, "acc")], rows=S)

    gws, gss = [None] * DEPTH, [None] * DEPTH
    for l in reversed(range(DEPTH)):
        dx, gw, gs = _layer_bwd(l, dx, mem, ws[l], ps[l], tabs, swa_bias, saved[l], S)
        gws[l] = _layer_weight_grads(gw)
        gss[l] = gs

    dbias = (gss[0]["dbias"] + gss[1]["dbias"]).reshape(SWA_HEADS, -1)
    g_rel = _mm("g_rel_bias", dbias, onehot.T, [F32], cast=None, precision=hi, tk=8192).T
    wgrads = {k: jnp.stack([gws[l][k] for l in range(DEPTH)]) for k in gws[0]}
    sgrads = dict(
        rel_bias=g_rel,
        final_norm=g_final[0],
        attn_sinks=jnp.stack([gss[l]["sinks"] for l in range(DEPTH)]),
        **{k: jnp.concatenate([gss[l][k] for l in range(DEPTH)], axis=0)
           for k in ("attn_norm", "mem_norm", "b_gate", "mla_q_norm", "mla_kv_norm", "mlp_norm")})
    return loss_acc[0, 0], dx, wgrads, sgrads


def _pack_small(vals, loss):
    rows = []
    for name, shape in SMALL:
        flat = vals[name].astype(F32).reshape(-1)
        pad = (-flat.shape[0]) % 1024
        rows.append(jnp.pad(flat, (0, pad)).reshape(-1, 128))
    rows.append(jnp.zeros((8, 128), F32) + loss)
    return jnp.concatenate(rows, axis=0)


def _unpack_small(packed):
    out, r = {}, 0
    for name, shape in SMALL:
        size = math.prod(shape)
        nrows = 8 * -(-size // 1024)
        out[name] = packed[r:r + nrows].reshape(-1)[:size].reshape(shape)
        r += nrows
    return out, packed[r, 0]


def kernel(x, mem, rel_bias, attn_norm, mem_norm, w_in, b_gate, mla_q_norm, w_uq, mla_kv_norm, w_ukv, attn_sinks, w_mem_kv, w_o_mla, w_o_swa, w_o_mem, w_out, mlp_norm, w_up, w_down, final_norm, loss_target, m_rel_bias, m_attn_norm, m_mem_norm, m_w_in, m_b_gate, m_mla_q_norm, m_w_uq, m_mla_kv_norm, m_w_ukv, m_attn_sinks, m_w_mem_kv, m_w_o_mla, m_w_o_swa, m_w_o_mem, m_w_out, m_mlp_norm, m_w_up, m_w_down, m_final_norm, v_rel_bias, v_attn_norm, v_mem_norm, v_w_in, v_b_gate, v_mla_q_norm, v_w_uq, v_mla_kv_norm, v_w_ukv, v_attn_sinks, v_w_mem_kv, v_w_o_mla, v_w_o_swa, v_w_o_mem, v_w_out, v_mlp_norm, v_w_up, v_w_down, v_final_norm):
    wv = dict(rel_bias=rel_bias, attn_norm=attn_norm, mem_norm=mem_norm, w_in=w_in, b_gate=b_gate,
              mla_q_norm=mla_q_norm, w_uq=w_uq, mla_kv_norm=mla_kv_norm, w_ukv=w_ukv, attn_sinks=attn_sinks,
              w_mem_kv=w_mem_kv, w_o_mla=w_o_mla, w_o_swa=w_o_swa, w_o_mem=w_o_mem, w_out=w_out,
              mlp_norm=mlp_norm, w_up=w_up, w_down=w_down, final_norm=final_norm)
    mv = dict(rel_bias=m_rel_bias, attn_norm=m_attn_norm, mem_norm=m_mem_norm, w_in=m_w_in, b_gate=m_b_gate,
              mla_q_norm=m_mla_q_norm, w_uq=m_w_uq, mla_kv_norm=m_mla_kv_norm, w_ukv=m_w_ukv,
              attn_sinks=m_attn_sinks, w_mem_kv=m_w_mem_kv, w_o_mla=m_w_o_mla, w_o_swa=m_w_o_swa,
              w_o_mem=m_w_o_mem, w_out=m_w_out, mlp_norm=m_mlp_norm, w_up=m_w_up, w_down=m_w_down,
              final_norm=m_final_norm)
    vv = dict(rel_bias=v_rel_bias, attn_norm=v_attn_norm, mem_norm=v_mem_norm, w_in=v_w_in, b_gate=v_b_gate,
              mla_q_norm=v_mla_q_norm, w_uq=v_w_uq, mla_kv_norm=v_mla_kv_norm, w_ukv=v_w_ukv,
              attn_sinks=v_attn_sinks, w_mem_kv=v_w_mem_kv, w_o_mla=v_w_o_mla, w_o_swa=v_w_o_swa,
              w_o_mem=v_w_o_mem, w_out=v_w_out, mlp_norm=v_mlp_norm, w_up=v_w_up, w_down=v_w_down,
              final_norm=v_final_norm)

    shard_rows = [math.prod(_shard_shape(shape, axis)) // 128 for _, shape, axis in WSPECS]
    gathered = _gather_forwarded("gather_weights", _pack_rows([wv[name].astype(BF16) for name, _, _ in WSPECS]))
    full, r = {}, 0
    for (name, shape, axis), nr in zip(WSPECS, shard_rows):
        full[name] = _unstack(gathered[:, r:r + nr].reshape((N_DEV,) + _shard_shape(shape, axis)), shape, axis)
        r += nr

    loss_part, grad_x, wgrads, sgrads = _local_step(x[0], mem[0], loss_target[0], full,
                                                    {name: wv[name] for name, _ in SMALL})

    recv = _reduce_scatter([_restack(wgrads[name], axis).astype(BF16).reshape(N_DEV, -1, 128)
                            for name, _, axis in WSPECS])
    outs = _adam("adam_sharded", recv, *[_pack_rows([d[name] for name, _, _ in WSPECS]) for d in (wv, mv, vv)])
    res = {}
    r = 0
    for (name, shape, axis), nr in zip(WSPECS, shard_rows):
        res[name] = [o[r:r + nr].reshape(_shard_shape(shape, axis)) for o in outs]
        r += nr

    small_recv = _gather_forwarded("gather_small", _pack_small(sgrads, loss_part))
    zero = jnp.zeros((), F32)
    souts = _adam("adam_small", small_recv, *[_pack_small(d, zero) for d in (wv, mv, vv)])
    loss = None
    for i, o in enumerate(souts):
        vals, extra = _unpack_small(o)
        if i == 0:
            loss = extra
        for name, _ in SMALL:
            res.setdefault(name, []).append(vals[name])

    out = [loss, grad_x[None]]
    for i in range(4):
        out.extend(res[name][i] for name in WEIGHT_ORDER)
    return tuple(out)
```

```python
import math

import jax
import jax.numpy as jnp
from jax import lax
from jax.experimental import pallas as pl
from jax.experimental.pallas import tpu as pltpu

F32 = jnp.float32
BF16 = jnp.bfloat16

N_DEV = 8
D_MODEL = 1024
DEPTH = 2
MLA_HEADS = 8
MLA_Q_LORA = 256
MLA_KV_LORA = 128
MLA_NOPE = 64
MLA_ROPE = 32
MLA_V = 64
ROPE_THETA = 10000.0
SWA_HEADS = 8
SWA_KV_HEADS = 2
SWA_HD = 64
WINDOW = 128
REL_BUCKETS = 32
REL_MAX_DIST = 128
MEM_LEN = 256
MEM_HEADS = 4
MEM_HD = 128
D_FF = 4 * D_MODEL
EPS = 1e-6
HEAD_PAD = 128
ADAM_LR = 0.001
ADAM_B1 = 0.9
ADAM_B2 = 0.999
ADAM_EPS = 1e-08
ADAM_WD = 0.01
ADAM_STEP = 10

NEG = -1e30
VMEM_LIMIT = 48 * 1024 * 1024

MM_TM = 1024
MM_TN = 1024
MM_TK = 1024
TN_T1 = 1024
TN_TN = 1024
TN_TS = 2048
MM_TK_DEEP = 2048
MM_TM_BF16 = 2048
ROW_TM = 1024
MLA_TILE = 4096
MLA_TILE_BWD = 2048
MLA_BWD_VMEM_LIMIT = VMEM_LIMIT
MLA_CHUNK = 256
MLA_CHUNK_FWD = 512
MLA_QK = MLA_NOPE + MLA_ROPE
MLA_SCALE = MLA_QK ** -0.5
LOG2E = math.log2(math.e)
DEN_LANE = MLA_V
SWA_TQ = 1024
SWA_AHEAD = 1
MEM_TQ = 1024
MEM_CHUNK = 256
ADAM_TM = 2000

WSPECS = (
    ("w_in", (DEPTH, D_MODEL, 4768), 2),
    ("w_uq", (DEPTH, MLA_Q_LORA, 768), 2),
    ("w_ukv", (DEPTH, MLA_KV_LORA, 1024), 2),
    ("w_mem_kv", (DEPTH, D_MODEL, 1024), 1),
    ("w_o_mla", (DEPTH, 512, D_MODEL), 2),
    ("w_o_swa", (DEPTH, 512, D_MODEL), 2),
    ("w_o_mem", (DEPTH, 512, D_MODEL), 2),
    ("w_out", (DEPTH, D_MODEL, D_MODEL), 1),
    ("w_up", (DEPTH, D_MODEL, D_FF), 2),
    ("w_down", (DEPTH, D_FF, D_MODEL), 1),
)
SMALL = (
    ("rel_bias", (REL_BUCKETS, SWA_HEADS)),
    ("attn_norm", (DEPTH, D_MODEL)),
    ("mem_norm", (DEPTH, D_MODEL)),
    ("b_gate", (DEPTH, 3 * D_MODEL)),
    ("mla_q_norm", (DEPTH, MLA_Q_LORA)),
    ("mla_kv_norm", (DEPTH, MLA_KV_LORA)),
    ("attn_sinks", (DEPTH, SWA_HEADS)),
    ("mlp_norm", (DEPTH, D_MODEL)),
    ("final_norm", (D_MODEL,)),
)
WEIGHT_ORDER = ("rel_bias", "attn_norm", "mem_norm", "w_in", "b_gate", "mla_q_norm", "w_uq", "mla_kv_norm",
                "w_ukv", "attn_sinks", "w_mem_kv", "w_o_mla", "w_o_swa", "w_o_mem", "w_out", "mlp_norm",
                "w_up", "w_down", "final_norm")


def _cparams(*sem):
    return pltpu.CompilerParams(dimension_semantics=sem, vmem_limit_bytes=VMEM_LIMIT)


def _shard_shape(shape, axis):
    s = list(shape)
    s[axis] //= N_DEV
    return tuple(s)


def _mm(name, a, b, out_dtypes, *, epi=None, extras=(), a_fn=None, cast=BF16, precision=None,
        tm=None, tn=None, tk=None, col_sums=0):
    M, K = a.shape
    K2, N = b.shape
    assert K == K2, (name, a.shape, b.shape)
    tm = min(tm or MM_TM, M)
    tn = min(tn or MM_TN, N)
    tk = min(tk or MM_TK, K)
    assert M % tm == 0 and N % tn == 0 and K % tk == 0, (name, a.shape, b.shape, tm, tn, tk)
    assert col_sums == 0 or tn == N, (name, tn, N)
    nk = K // tk
    n_ex = len(extras)
    n_out = len(out_dtypes)

    def body(*refs):
        a_ref, b_ref = refs[0], refs[1]
        ex_refs = refs[2:2 + n_ex]
        out_refs = refs[2 + n_ex:2 + n_ex + n_out]
        av = a_ref[...]
        if a_fn is not None:
            av = a_fn(av)
        bv = b_ref[...]
        if cast is not None:
            av = av.astype(cast)
            bv = bv.astype(cast)
        part = jnp.dot(av, bv, preferred_element_type=F32, precision=precision)

        def finish(acc):
            outs = epi(acc, *[r[...] for r in ex_refs]) if epi is not None else (acc,)
            for r, o in zip(out_refs, outs[:n_out]):
                r[...] = o.astype(r.dtype)
            i = pl.program_id(0)
            for r, o in zip(refs[2 + n_ex + n_out:2 + n_ex + n_out + col_sums], outs[n_out:]):
                @pl.when(i == 0)
                def _(r=r, o=o):
                    r[...] = o

                @pl.when(i > 0)
                def _(r=r, o=o):
                    r[...] += o

        if nk == 1:
            finish(part)
        else:
            acc_ref = refs[-1]
            k = pl.program_id(2)

            @pl.when(k == 0)
            def _():
                acc_ref[...] = part

            @pl.when(k > 0)
            def _():
                acc_ref[...] += part

            @pl.when(k == nk - 1)
            def _():
                finish(acc_ref[...])

    in_specs = [pl.BlockSpec((tm, tk), lambda i, j, k: (i, k)),
                pl.BlockSpec((tk, tn), lambda i, j, k: (k, j))]
    for arr, kind in extras:
        if kind == "mn":
            in_specs.append(pl.BlockSpec((tm, tn), lambda i, j, k: (i, j)))
        elif kind == "m":
            in_specs.append(pl.BlockSpec((tm, arr.shape[1]), lambda i, j, k: (i, 0)))
        else:
            in_specs.append(pl.BlockSpec((1, tn), lambda i, j, k: (0, j)))
    outs = pl.pallas_call(
        body, name=name, grid=(M // tm, N // tn, nk),
        in_specs=in_specs,
        out_specs=([pl.BlockSpec((tm, tn), lambda i, j, k: (i, j)) for _ in out_dtypes]
                   + [pl.BlockSpec((1, tn), lambda i, j, k: (0, 0))] * col_sums),
        out_shape=([jax.ShapeDtypeStruct((M, N), dt) for dt in out_dtypes]
                   + [jax.ShapeDtypeStruct((1, N), F32)] * col_sums),
        scratch_shapes=[pltpu.VMEM((tm, tn), F32)] if nk > 1 else [],
        compiler_params=(_cparams("arbitrary", "arbitrary", "arbitrary") if col_sums
                         else _cparams("parallel", "parallel", "arbitrary")),
    )(a, b, *[arr for arr, _ in extras])
    return outs[0] if n_out + col_sums == 1 else outs


def _mm_tn(name, a, b, *, t1=None, tn=None, ts=None):
    S, K1 = a.shape
    S2, N = b.shape
    assert S == S2, (name, a.shape, b.shape)
    t1 = min(t1 or TN_T1, K1)
    tn = min(tn or TN_TN, N)
    ts = min(ts or TN_TS, S)
    assert K1 % t1 == 0 and N % tn == 0 and S % ts == 0, (name, a.shape, b.shape)

    def body(a_ref, b_ref, o_ref):
        s = pl.program_id(2)
        part = lax.dot_general(a_ref[...].astype(BF16), b_ref[...].astype(BF16),
                               (((0,), (0,)), ((), ())), preferred_element_type=F32)

        @pl.when(s == 0)
        def _():
            o_ref[...] = part

        @pl.when(s > 0)
        def _():
            o_ref[...] += part

    return pl.pallas_call(
        body, name=name, grid=(K1 // t1, N // tn, S // ts),
        in_specs=[pl.BlockSpec((ts, t1), lambda i, j, s: (s, i)),
                  pl.BlockSpec((ts, tn), lambda i, j, s: (s, j))],
        out_specs=pl.BlockSpec((t1, tn), lambda i, j, s: (i, j)),
        out_shape=jax.ShapeDtypeStruct((K1, N), F32),
        compiler_params=_cparams("parallel", "parallel", "arbitrary"),
    )(a, b)


def _rows(arr, width=None, blk=0):
    return (arr, ("rows", arr.shape[1] if width is None else width, blk))


def _full(arr):
    return (arr, ("full",))


def _rowwise(name, fn, ins, outs, *, rows, tm=None):
    tm = min(tm or ROW_TM, rows)
    assert rows % tm == 0, (name, rows, tm)
    n_in = len(ins)

    def body(*refs):
        i = pl.program_id(0)
        vals = fn(*[r[...] for r in refs[:n_in]])
        for (shape, dt, kind), r, v in zip(outs, refs[n_in:], vals):
            if kind == "rows":
                r[...] = v.astype(dt)
            else:
                @pl.when(i == 0)
                def _(r=r, v=v):
                    r[...] = v

                @pl.when(i > 0)
                def _(r=r, v=v):
                    r[...] += v

    in_specs = []
    for arr, spec in ins:
        if spec[0] == "rows":
            in_specs.append(pl.BlockSpec((tm, spec[1]), lambda i, b=spec[2]: (i, b)))
        else:
            in_specs.append(pl.BlockSpec(arr.shape, lambda i, n=arr.ndim: (0,) * n))
    out_specs = []
    for shape, dt, kind in outs:
        if kind == "rows":
            out_specs.append(pl.BlockSpec((tm, shape[1]), lambda i: (i, 0)))
        else:
            out_specs.append(pl.BlockSpec(shape, lambda i, n=len(shape): (0,) * n))
    res = pl.pallas_call(
        body, name=name, grid=(rows // tm,),
        in_specs=in_specs, out_specs=out_specs,
        out_shape=[jax.ShapeDtypeStruct(shape, dt) for shape, dt, _ in outs],
        compiler_params=_cparams("arbitrary"),
    )(*[arr for arr, _ in ins])
    return res


MEM_SCALE = MEM_HD ** -0.5
MEM_Q0 = 2
NT_DIMS = (((1,), (1,)), ((), ()))


def _head_lanes(h):
    return slice(h * HEAD_PAD, (h + 1) * HEAD_PAD)


def _mem_fwd(name, proj_b, kvm, *, tq, chunk):
    S = proj_b.shape[0]
    tq = min(tq, S)
    C = min(chunk, tq)
    tiles = [(h, c) for c in range(tq // C) for h in range(MEM_HEADS)]

    def body(q_ref, kv_ref, o_ref, lse_ref):
        def logits(h, c):
            return lax.dot_general(q_ref[c * C:(c + 1) * C, _head_lanes(h)], kv_ref[:, _head_lanes(h)], NT_DIMS,
                                   preferred_element_type=F32) * MEM_SCALE

        nxt = logits(*tiles[0])
        for n, (h, c) in enumerate(tiles):
            s = nxt
            if n + 1 < len(tiles):
                nxt = logits(*tiles[n + 1])
            rows = slice(c * C, (c + 1) * C)
            m = jnp.max(s, axis=1, keepdims=True)
            p = jnp.exp(s - m)
            l = jnp.sum(p, axis=1, keepdims=True)
            o = jnp.dot(p.astype(BF16), kv_ref[:, _head_lanes(MEM_HEADS + h)], preferred_element_type=F32) / l
            o_ref[rows, _head_lanes(h)] = o.astype(o_ref.dtype)
            lse_ref[h, rows, :] = m + jnp.log(l)

    return pl.pallas_call(
        body, name=name, grid=(S // tq,),
        in_specs=[pl.BlockSpec((tq, MEM_HEADS * HEAD_PAD), lambda i: (i, MEM_Q0)),
                  pl.BlockSpec(kvm.shape, lambda i: (0, 0))],
        out_specs=[pl.BlockSpec((tq, MEM_HEADS * HEAD_PAD), lambda i: (i, 0)),
                   pl.BlockSpec((MEM_HEADS, tq, 1), lambda i: (0, i, 0))],
        out_shape=[jax.ShapeDtypeStruct((S, MEM_HEADS * HEAD_PAD), BF16),
                   jax.ShapeDtypeStruct((MEM_HEADS, S, 1), F32)],
        compiler_params=_cparams("parallel"),
    )(proj_b, kvm)


def _mem_bwd(name, proj_b, kvm, o, do, lse, *, tq, chunk):
    S = proj_b.shape[0]
    tq = min(tq, S)
    C = min(chunk, tq)
    nq = S // tq
    tiles = [(h, c) for c in range(tq // C) for h in range(MEM_HEADS)]

    def body(q_ref, kv_ref, o_ref, do_ref, lse_ref, dq_ref, dkv_ref, acc_sc):
        i = pl.program_id(0)

        @pl.when(i == 0)
        def _():
            acc_sc[...] = jnp.zeros(acc_sc.shape, F32)

        def mats(h, c):
            rows = slice(c * C, (c + 1) * C)
            q = q_ref[rows, _head_lanes(h)]
            dov = do_ref[rows, _head_lanes(h)]
            s = lax.dot_general(q, kv_ref[:, _head_lanes(h)], NT_DIMS, preferred_element_type=F32) * MEM_SCALE
            dp = lax.dot_general(dov, kv_ref[:, _head_lanes(MEM_HEADS + h)], NT_DIMS, preferred_element_type=F32)
            return q, dov, s, dp

        nxt = mats(*tiles[0])
        for n, (h, c) in enumerate(tiles):
            q, dov, s, dp = nxt
            if n + 1 < len(tiles):
                nxt = mats(*tiles[n + 1])
            rows = slice(c * C, (c + 1) * C)
            p = jnp.exp(s - lse_ref[h, rows, :])
            delta = jnp.sum(dov.astype(F32) * o_ref[rows, _head_lanes(h)].astype(F32), axis=1, keepdims=True)
            ds = (p * (dp - delta) * MEM_SCALE).astype(BF16)
            dq_ref[rows, _head_lanes(h)] = jnp.dot(ds, kv_ref[:, _head_lanes(h)],
                                                   preferred_element_type=F32).astype(dq_ref.dtype)
            acc_sc[_head_lanes(h), :] += jnp.dot(q.T, ds, preferred_element_type=F32)
            acc_sc[_head_lanes(MEM_HEADS + h), :] += jnp.dot(dov.T, p.astype(BF16), preferred_element_type=F32)

        @pl.when(i == nq - 1)
        def _():
            dkv_ref[...] = acc_sc[...].T

    qblk = pl.BlockSpec((tq, MEM_HEADS * HEAD_PAD), lambda i: (i, 0))
    return pl.pallas_call(
        body, name=name, grid=(nq,),
        in_specs=[pl.BlockSpec((tq, MEM_HEADS * HEAD_PAD), lambda i: (i, MEM_Q0)),
                  pl.BlockSpec(kvm.shape, lambda i: (0, 0)), qblk, qblk,
                  pl.BlockSpec((MEM_HEADS, tq, 1), lambda i: (0, i, 0))],
        out_specs=[qblk, pl.BlockSpec(kvm.shape, lambda i: (0, 0))],
        out_shape=[jax.ShapeDtypeStruct((S, MEM_HEADS * HEAD_PAD), BF16), jax.ShapeDtypeStruct(kvm.shape, F32)],
        scratch_shapes=[pltpu.VMEM((kvm.shape[1], kvm.shape[0]), F32)],
        compiler_params=_cparams("arbitrary"),
    )(proj_b, kvm, o, do, lse)


def _causal_fwd(name, q_arr, k_arr, v_arr, *, heads, tile, chunk):
    S = q_arr.shape[0]
    T = min(tile, S)
    C = min(chunk, T)
    nt = S // T
    nc = T // C

    pairs = [(qi, kk) for qi in range(nt) for kk in range(qi + 1)]
    q_tab = jnp.asarray([p[0] for p in pairs], jnp.int32)
    k_tab = jnp.asarray([p[1] for p in pairs], jnp.int32)

    def body(qt_ref, kt_ref, q_ref, k_ref, v_ref, o_ref, lse_ref, m_sc, acc_sc):
        t = pl.program_id(1)
        qi = qt_ref[t]
        kk = kt_ref[t]

        @pl.when(kk == 0)
        def _():
            m_sc[...] = jnp.full(m_sc.shape, NEG, F32)
            acc_sc[...] = jnp.zeros(acc_sc.shape, F32)

        def logits(c, ncols, masked):
            s = lax.dot_general(q_ref[pl.ds(c * C, C), :], k_ref[0:ncols, :], (((1,), (1,)), ((), ())),
                                preferred_element_type=F32)
            if masked:
                r = c * C + lax.broadcasted_iota(jnp.int32, (C, ncols), 0)
                cidx = lax.broadcasted_iota(jnp.int32, (C, ncols), 1)
                s = jnp.where(cidx <= r, s, NEG)
            return s

        def update(c, ncols, s):
            rows = pl.ds(c * C, C)
            m_prev = m_sc[rows, :]
            m_new = jnp.maximum(m_prev, jnp.max(s, axis=1, keepdims=True))
            p = jnp.exp2(s - m_new).astype(BF16)
            acc_sc[rows, :] = jnp.exp2(m_prev - m_new) * acc_sc[rows, :] + jnp.dot(
                p, v_ref[0:ncols, :], preferred_element_type=F32)
            m_sc[rows, :] = m_new

        def tile_step(ncols_of, masked):
            s = logits(0, ncols_of(0), masked)
            for c in range(nc):
                s_next = logits(c + 1, ncols_of(c + 1), masked) if c + 1 < nc else None
                update(c, ncols_of(c), s)
                s = s_next

        @pl.when(kk < qi)
        def _():
            tile_step(lambda c: T, False)

        @pl.when(kk == qi)
        def _():
            tile_step(lambda c: (c + 1) * C, True)

        @pl.when(kk == qi)
        def _():
            acc = acc_sc[...]
            l = acc[:, DEN_LANE:DEN_LANE + 1]
            o_ref[...] = (acc / l).astype(o_ref.dtype)
            lse_ref[0] = m_sc[...] + jnp.log2(l)

    grid_spec = pltpu.PrefetchScalarGridSpec(
        num_scalar_prefetch=2, grid=(heads, len(pairs)),
        in_specs=[pl.BlockSpec((T, HEAD_PAD), lambda h, t, qt, kt: (qt[t], h)),
                  pl.BlockSpec((T, HEAD_PAD), lambda h, t, qt, kt: (kt[t], h)),
                  pl.BlockSpec((T, HEAD_PAD), lambda h, t, qt, kt: (kt[t], h))],
        out_specs=[pl.BlockSpec((T, HEAD_PAD), lambda h, t, qt, kt: (qt[t], h)),
                   pl.BlockSpec((1, T, 1), lambda h, t, qt, kt: (h, qt[t], 0))],
        scratch_shapes=[pltpu.VMEM((T, 1), F32), pltpu.VMEM((T, HEAD_PAD), F32)])
    return pl.pallas_call(
        body, name=name, grid_spec=grid_spec,
        out_shape=[jax.ShapeDtypeStruct((S, heads * HEAD_PAD), BF16),
                   jax.ShapeDtypeStruct((heads, S, 1), F32)],
        compiler_params=_cparams("parallel", "arbitrary"),
    )(q_tab, k_tab, q_arr, k_arr, v_arr)


def _with_neg_delta(do, o):
    lane = lax.broadcasted_iota(jnp.int32, (1, HEAD_PAD), 1)
    outs = []
    for h in range(do.shape[1] // HEAD_PAD):
        a = do[:, _head_lanes(h)]
        nd = -jnp.sum(a * o[:, _head_lanes(h)].astype(F32), axis=1, keepdims=True)
        hi = nd.astype(BF16).astype(F32)
        a = jnp.where(lane == DEN_LANE, hi, a)
        outs.append(jnp.where(lane == DEN_LANE + 1, nd - hi, a))
    return jnp.concatenate(outs, axis=1)


def _causal_bwd(name, q_arr, k_arr, v_arr, do_arr, lse, *, heads, tile, chunk):
    S = q_arr.shape[0]
    T = min(tile, S)
    C = min(chunk, T)
    nt = S // T
    nc = T // C

    pairs = [(kj, qq) for kj in range(nt) for qq in range(kj, nt)]
    k_tab = jnp.asarray([p[0] for p in pairs], jnp.int32)
    q_tab = jnp.asarray([p[1] for p in pairs], jnp.int32)

    def body(kt_ref, qt_ref, q_ref, k_ref, v_ref, do_ref, lse_ref, dq_ref, dk_ref, dv_ref, dk_sc, dv_sc):
        t = pl.program_id(1)
        kj = kt_ref[t]
        qq = qt_ref[t]
        qb = qq

        @pl.when(t == 0)
        def _():
            dq_ref[...] = jnp.zeros(dq_ref.shape, F32)

        @pl.when(qq == kj)
        def _():
            dk_sc[...] = jnp.zeros(dk_sc.shape, F32)
            dv_sc[...] = jnp.zeros(dv_sc.shape, F32)

        def logits(c, ncols, masked):
            rows = pl.ds(c * C, C)
            s = lax.dot_general(q_ref[rows, :], k_ref[0:ncols, :], (((1,), (1,)), ((), ())),
                                preferred_element_type=F32)
            if masked:
                r = c * C + lax.broadcasted_iota(jnp.int32, (C, ncols), 0)
                cidx = lax.broadcasted_iota(jnp.int32, (C, ncols), 1)
                s = jnp.where(cidx <= r, s, NEG)
            dp = lax.dot_general(do_ref[rows, :], v_ref[0:ncols, :], (((1,), (1,)), ((), ())),
                                 preferred_element_type=F32)
            return s, dp

        def update(c, ncols, s, dp):
            rows = pl.ds(c * C, C)
            p = jnp.exp2(s - lse_ref[0, rows, :])
            ds = (p * dp).astype(BF16)
            dv_sc[:, 0:ncols] += jnp.dot(do_ref[rows, 0:MLA_V].T, p.astype(BF16), preferred_element_type=F32)
            dk_sc[:, 0:ncols] += jnp.dot(q_ref[rows, 0:MLA_QK].T, ds, preferred_element_type=F32)
            row0 = pl.multiple_of(qb * T + c * C, C)
            dq_ref[pl.ds(row0, C), :] += jnp.dot(ds, k_ref[0:ncols, :], preferred_element_type=F32)

        def tile_step(ncols_of, masked):
            cur = logits(0, ncols_of(0), masked)
            for c in range(nc):
                nxt = logits(c + 1, ncols_of(c + 1), masked) if c + 1 < nc else None
                update(c, ncols_of(c), *cur)
                cur = nxt

        @pl.when(qq > kj)
        def _():
            tile_step(lambda c: T, False)

        @pl.when(qq == kj)
        def _():
            tile_step(lambda c: (c + 1) * C, True)

        @pl.when(qq == nt - 1)
        def _():
            dk_ref[:, 0:MLA_QK] = dk_sc[...].T * math.log(2.0)
            dk_ref[:, MLA_QK:] = jnp.zeros((T, HEAD_PAD - MLA_QK), F32)
            dv_ref[:, 0:MLA_V] = dv_sc[...].T
            dv_ref[:, MLA_V:] = jnp.zeros((T, HEAD_PAD - MLA_V), F32)

    qrow = pl.BlockSpec((T, HEAD_PAD), lambda h, t, kt, qt: (qt[t], h))
    krow = pl.BlockSpec((T, HEAD_PAD), lambda h, t, kt, qt: (kt[t], h))
    qcol = pl.BlockSpec((1, T, 1), lambda h, t, kt, qt: (h, qt[t], 0))
    grid_spec = pltpu.PrefetchScalarGridSpec(
        num_scalar_prefetch=2, grid=(heads, len(pairs)),
        in_specs=[qrow, krow, krow, qrow, qcol],
        out_specs=[pl.BlockSpec((S, HEAD_PAD), lambda h, t, kt, qt: (0, h)), krow, krow],
        scratch_shapes=[pltpu.VMEM((MLA_QK, T), F32), pltpu.VMEM((MLA_V, T), F32)])
    return pl.pallas_call(
        body, name=name, grid_spec=grid_spec,
        out_shape=[jax.ShapeDtypeStruct((S, heads * HEAD_PAD), F32)] * 3,
        compiler_params=pltpu.CompilerParams(dimension_semantics=("arbitrary", "arbitrary"),
                                             vmem_limit_bytes=MLA_BWD_VMEM_LIMIT),
    )(k_tab, q_tab, q_arr, k_arr, v_arr, do_arr, lse)


SWA_R = SWA_HEADS // SWA_KV_HEADS
SWA_SCALE = SWA_HD ** -0.5
SWA_Q0, SWA_K0, SWA_V0 = 0, 12, 14


def _swa_specs(tq):
    nsb = tq // WINDOW
    return [
        pl.BlockSpec((tq, SWA_R * HEAD_PAD), lambda g, i: (i, g)),
        pl.BlockSpec((tq, HEAD_PAD), lambda g, i: (i, SWA_K0 + g)),
        pl.BlockSpec((WINDOW, HEAD_PAD), lambda g, i: (jnp.maximum(nsb * i - 1, 0), SWA_K0 + g)),
        pl.BlockSpec((tq, HEAD_PAD), lambda g, i: (i, SWA_V0 + g)),
        pl.BlockSpec((WINDOW, HEAD_PAD), lambda g, i: (jnp.maximum(nsb * i - 1, 0), SWA_V0 + g)),
        pl.BlockSpec((SWA_R, WINDOW, 2 * WINDOW), lambda g, i: (g, 0, 0)),
        pl.BlockSpec((SWA_R, 8, 128), lambda g, i: (g, 0, 0)),
    ]


def _swa_block(i, sb, q_ref, kc_ref, kp_ref, vc_ref, vp_ref, bias, sink):
    rows = slice(sb * WINDOW, (sb + 1) * WINDOW)
    qs = jnp.concatenate([q_ref[rows, hh * HEAD_PAD:(hh + 1) * HEAD_PAD] for hh in range(SWA_R)], axis=0)
    if sb == 0:
        kp, vp = kp_ref[...], vp_ref[...]
    else:
        prev = slice((sb - 1) * WINDOW, sb * WINDOW)
        kp, vp = kc_ref[prev, :], vc_ref[prev, :]
    kk = jnp.concatenate([kp, kc_ref[rows, :]], axis=0)
    vv = jnp.concatenate([vp, vc_ref[rows, :]], axis=0)
    s = lax.dot_general(qs, kk, (((1,), (1,)), ((), ())), preferred_element_type=F32) * SWA_SCALE + bias
    if sb == 0:
        col = lax.broadcasted_iota(jnp.int32, (1, 2 * WINDOW), 1)
        s = s + jnp.where((col < WINDOW) & (i == 0), NEG, 0.0)
    return rows, qs, kk, vv, s


def _stack_heads(ref, rows, lead=None):
    if lead is None:
        return jnp.concatenate([ref[rows, hh * HEAD_PAD:(hh + 1) * HEAD_PAD] for hh in range(SWA_R)], axis=0)
    return jnp.concatenate([ref[hh, rows, :] for hh in range(SWA_R)], axis=0)


def _swa_fwd(name, proj_b, bias, sinks, *, tq):
    S = proj_b.shape[0]
    tq = min(tq, S)
    nsb = tq // WINDOW

    def body(q_ref, kc_ref, kp_ref, vc_ref, vp_ref, bias_ref, sink_ref, o_ref, lse_ref):
        i = pl.program_id(1)
        bias_v = bias_ref[...].reshape(SWA_R * WINDOW, 2 * WINDOW)
        sink = jnp.concatenate([jnp.zeros((WINDOW, 1), F32) + sink_ref[hh, 0:1, 0:1] for hh in range(SWA_R)], axis=0)
        ahead = [_swa_block(i, sb, q_ref, kc_ref, kp_ref, vc_ref, vp_ref, bias_v, sink)
                 for sb in range(min(SWA_AHEAD, nsb))]
        for sb in range(nsb):
            rows, _, _, vv, s = ahead.pop(0)
            if sb + SWA_AHEAD < nsb:
                ahead.append(_swa_block(i, sb + SWA_AHEAD, q_ref, kc_ref, kp_ref, vc_ref, vp_ref, bias_v, sink))
            m = jnp.maximum(jnp.max(s, axis=1, keepdims=True), sink)
            p = jnp.exp(s - m)
            l = jnp.sum(p, axis=1, keepdims=True) + jnp.exp(sink - m)
            o = jnp.dot(p.astype(BF16), vv, preferred_element_type=F32) / l
            lse_v = m + jnp.log(l)
            for hh in range(SWA_R):
                o_ref[rows, hh * HEAD_PAD:(hh + 1) * HEAD_PAD] = o[hh * WINDOW:(hh + 1) * WINDOW].astype(o_ref.dtype)
                lse_ref[hh, rows, :] = lse_v[hh * WINDOW:(hh + 1) * WINDOW]

    return pl.pallas_call(
        body, name=name, grid=(SWA_KV_HEADS, S // tq),
        in_specs=_swa_specs(tq),
        out_specs=[pl.BlockSpec((tq, SWA_R * HEAD_PAD), lambda g, i: (i, g)),
                   pl.BlockSpec((SWA_R, tq, 1), lambda g, i: (g, i, 0))],
        out_shape=[jax.ShapeDtypeStruct((S, SWA_HEADS * HEAD_PAD), BF16),
                   jax.ShapeDtypeStruct((SWA_HEADS, S, 1), F32)],
        compiler_params=_cparams("parallel", "parallel"),
    )(proj_b, proj_b, proj_b, proj_b, proj_b, bias, sinks)


def _swa_bwd(name, proj_b, bias, sinks, o, do, lse, *, tq):
    S = proj_b.shape[0]
    tq = min(tq, S)
    nsb = tq // WINDOW
    nq = S // tq

    def body(q_ref, kc_ref, kp_ref, vc_ref, vp_ref, bias_ref, sink_ref, o_ref, do_ref, lse_ref,
             dq_ref, dk_ref, dv_ref, dke_ref, dve_ref, dbias_ref, dsink_ref):
        i = pl.program_id(1)

        @pl.when(i == 0)
        def _():
            dbias_ref[...] = jnp.zeros(dbias_ref.shape, F32)
            dsink_ref[...] = jnp.zeros(dsink_ref.shape, F32)

        bias_v = bias_ref[...].reshape(SWA_R * WINDOW, 2 * WINDOW)
        sink = jnp.concatenate([jnp.zeros((WINDOW, 1), F32) + sink_ref[hh, 0:1, 0:1] for hh in range(SWA_R)], axis=0)
        dk_own, dv_own, dk_prev, dv_prev = [], [], [], []
        dbias_acc = jnp.zeros((SWA_R * WINDOW, 2 * WINDOW), F32)
        def block(sb):
            rows, qs, kk, vv, s = _swa_block(i, sb, q_ref, kc_ref, kp_ref, vc_ref, vp_ref, bias_v, sink)
            do_s = _stack_heads(do_ref, rows)
            dp = lax.dot_general(do_s, vv, (((1,), (1,)), ((), ())), preferred_element_type=F32)
            return rows, qs, kk, do_s, s, dp

        nxt = block(0)
        for sb in range(nsb):
            rows, qs, kk, do_s, s, dp = nxt
            if sb + 1 < nsb:
                nxt = block(sb + 1)
            lse_v = _stack_heads(lse_ref, rows, lead=True)
            delta = jnp.sum(do_s.astype(F32) * _stack_heads(o_ref, rows).astype(F32), axis=1, keepdims=True)
            p = jnp.exp(s - lse_v)
            dsp = p * (dp - delta)
            dbias_acc = dbias_acc + dsp
            ds = (dsp * SWA_SCALE).astype(BF16)
            dq = jnp.dot(ds, kk, preferred_element_type=F32)
            dkk = jnp.dot(qs.T, ds, preferred_element_type=F32)
            dvv = jnp.dot(do_s.T, p.astype(BF16), preferred_element_type=F32)
            dk_prev.append(dkk[:, :WINDOW].T)
            dk_own.append(dkk[:, WINDOW:].T)
            dv_prev.append(dvv[:, :WINDOW].T)
            dv_own.append(dvv[:, WINDOW:].T)
            psink = jnp.exp(sink - lse_v) * delta
            for hh in range(SWA_R):
                hrows = slice(hh * WINDOW, (hh + 1) * WINDOW)
                dq_ref[rows, hh * HEAD_PAD:(hh + 1) * HEAD_PAD] = dq[hrows].astype(dq_ref.dtype)
                dsink_ref[hh] += jnp.zeros((8, 128), F32) - jnp.sum(psink[hrows])
        dbias_ref[...] += dbias_acc.reshape(SWA_R, WINDOW, 2 * WINDOW)
        for sb in range(nsb):
            rows = slice(sb * WINDOW, (sb + 1) * WINDOW)
            if sb + 1 < nsb:
                dk_ref[rows, :] = dk_own[sb] + dk_prev[sb + 1]
                dv_ref[rows, :] = dv_own[sb] + dv_prev[sb + 1]
            else:
                dk_ref[rows, :] = dk_own[sb]
                dv_ref[rows, :] = dv_own[sb]
        dke_ref[...] = dk_prev[0]
        dve_ref[...] = dv_prev[0]

    in_specs = _swa_specs(tq) + [
        pl.BlockSpec((tq, SWA_R * HEAD_PAD), lambda g, i: (i, g)),
        pl.BlockSpec((tq, SWA_R * HEAD_PAD), lambda g, i: (i, g)),
        pl.BlockSpec((SWA_R, tq, 1), lambda g, i: (g, i, 0)),
    ]
    kv_blk = pl.BlockSpec((tq, HEAD_PAD), lambda g, i: (i, g))
    edge_blk = pl.BlockSpec((WINDOW, HEAD_PAD), lambda g, i: (i, g))
    return pl.pallas_call(
        body, name=name, grid=(SWA_KV_HEADS, nq),
        in_specs=in_specs,
        out_specs=[pl.BlockSpec((tq, SWA_R * HEAD_PAD), lambda g, i: (i, g)), kv_blk, kv_blk, edge_blk, edge_blk,
                   pl.BlockSpec((SWA_R, WINDOW, 2 * WINDOW), lambda g, i: (g, 0, 0)),
                   pl.BlockSpec((SWA_R, 8, 128), lambda g, i: (g, 0, 0))],
        out_shape=[jax.ShapeDtypeStruct((S, SWA_HEADS * HEAD_PAD), BF16),
                   jax.ShapeDtypeStruct((S, SWA_KV_HEADS * HEAD_PAD), F32),
                   jax.ShapeDtypeStruct((S, SWA_KV_HEADS * HEAD_PAD), F32),
                   jax.ShapeDtypeStruct((nq * WINDOW, SWA_KV_HEADS * HEAD_PAD), F32),
                   jax.ShapeDtypeStruct((nq * WINDOW, SWA_KV_HEADS * HEAD_PAD), F32),
                   jax.ShapeDtypeStruct((SWA_HEADS, WINDOW, 2 * WINDOW), F32),
                   jax.ShapeDtypeStruct((SWA_HEADS, 8, 128), F32)],
        compiler_params=_cparams("arbitrary", "arbitrary"),
    )(proj_b, proj_b, proj_b, proj_b, proj_b, bias, sinks, o, do, lse)


def _dproj_b(name, dq_swa, dq_mem, dk, dv, dk_edge, dv_edge, *, tq):
    S = dq_swa.shape[0]
    tq = min(tq, S)
    nq = S // tq

    def body(dqs_ref, dqm_ref, dk_ref, dv_ref, dke_ref, dve_ref, o_ref):
        i = pl.program_id(0)
        o_ref[:, 0:1024] = dqs_ref[...]
        o_ref[:, 1024:1536] = dqm_ref[...].astype(o_ref.dtype)
        o_ref[:, 1536:1792] = dk_ref[...].astype(o_ref.dtype)
        o_ref[:, 1792:2048] = dv_ref[...].astype(o_ref.dtype)

        @pl.when(i < nq - 1)
        def _():
            last = slice(tq - WINDOW, tq)
            o_ref[last, 1536:1792] = (dk_ref[last, :] + dke_ref[...]).astype(o_ref.dtype)
            o_ref[last, 1792:2048] = (dv_ref[last, :] + dve_ref[...]).astype(o_ref.dtype)

    edge = pl.BlockSpec((WINDOW, SWA_KV_HEADS * HEAD_PAD), lambda i: (jnp.minimum(i + 1, nq - 1), 0))
    return pl.pallas_call(
        body, name=name, grid=(nq,),
        in_specs=[pl.BlockSpec((tq, 1024), lambda i: (i, 0)), pl.BlockSpec((tq, 512), lambda i: (i, 0)),
                  pl.BlockSpec((tq, 256), lambda i: (i, 0)), pl.BlockSpec((tq, 256), lambda i: (i, 0)), edge, edge],
        out_specs=pl.BlockSpec((tq, 2048), lambda i: (i, 0)),
        out_shape=jax.ShapeDtypeStruct((S, 2048), BF16),
        compiler_params=_cparams("parallel"),
    )(dq_swa, dq_mem, dk, dv, dk_edge, dv_edge)


def _gather_forwarded(name, block):
    def body(x_ref, out_ref, send_sems, recv_sems, local_sem):
        x, y, c = lax.axis_index("x"), lax.axis_index("y"), lax.axis_index("c")
        me, sibling = (x, y, c), (x, y, 1 - c)
        chips = [(1 - x, y), (x, 1 - y), (1 - x, 1 - y)]

        def slot(px, py, pc):
            return out_ref.at[4 * px + 2 * py + pc]

        def copy(k, blk, to, src=None):
            return pltpu.make_async_remote_copy(
                src_ref=slot(*blk) if src is None else src, dst_ref=slot(*blk),
                send_sem=send_sems.at[k], recv_sem=recv_sems.at[k],
                device_id=to, device_id_type=pl.DeviceIdType.MESH)

        mine = pltpu.make_async_copy(x_ref, slot(*me), local_sem)
        mine.start()
        first = [copy(0, me, sibling, src=x_ref)]
        first += [copy(1 + j, me, (*chip, c), src=x_ref) for j, chip in enumerate(chips)]
        for cp in first:
            cp.start()
        passed = [copy(4 + j, (*chip, c), sibling) for j, chip in enumerate(chips)]
        for j, chip in enumerate(chips):
            copy(1 + j, (*chip, c), me).wait_recv()
            passed[j].start()
        copy(0, sibling, me).wait_recv()
        for j, chip in enumerate(chips):
            copy(4 + j, (*chip, 1 - c), me).wait_recv()
        for cp in first + passed:
            cp.wait_send()
        mine.wait()

    return pl.pallas_call(
        body, name=name,
        in_specs=[pl.BlockSpec(memory_space=pl.ANY)],
        out_specs=pl.BlockSpec(memory_space=pl.ANY),
        out_shape=jax.ShapeDtypeStruct((N_DEV,) + tuple(block.shape), block.dtype),
        scratch_shapes=[pltpu.SemaphoreType.DMA((N_DEV - 1,)), pltpu.SemaphoreType.DMA((N_DEV - 1,)),
                        pltpu.SemaphoreType.DMA(())],
    )(block)


def _sibling_swap(name, block):
    def body(x_ref, out_ref, send_sem, recv_sem):
        x, y, c = lax.axis_index("x"), lax.axis_index("y"), lax.axis_index("c")
        cp = pltpu.make_async_remote_copy(src_ref=x_ref, dst_ref=out_ref, send_sem=send_sem, recv_sem=recv_sem,
                                          device_id=(x, y, 1 - c), device_id_type=pl.DeviceIdType.MESH)
        cp.start()
        cp.wait()

    return pl.pallas_call(
        body, name=name,
        in_specs=[pl.BlockSpec(memory_space=pl.ANY)],
        out_specs=pl.BlockSpec(memory_space=pl.ANY),
        out_shape=jax.ShapeDtypeStruct(block.shape, block.dtype),
        scratch_shapes=[pltpu.SemaphoreType.DMA(()), pltpu.SemaphoreType.DMA(())],
    )(block)


def _chip_exchange(name, send):
    def body(send_ref, recv_ref, send_sems, recv_sems, local_sem):
        x, y, c = lax.axis_index("x"), lax.axis_index("y"), lax.axis_index("c")
        me = 2 * x + y
        own = pltpu.make_async_copy(send_ref.at[me], recv_ref.at[me], local_sem)
        own.start()
        copies = []
        for k in range(1, 4):
            px = 1 - x if (k >> 1) & 1 else x
            py = 1 - y if k & 1 else y
            peer = 2 * px + py
            out = pltpu.make_async_remote_copy(
                src_ref=send_ref.at[peer], dst_ref=recv_ref.at[me],
                send_sem=send_sems.at[k - 1], recv_sem=recv_sems.at[k - 1],
                device_id=(px, py, c), device_id_type=pl.DeviceIdType.MESH)
            out.start()
            back = pltpu.make_async_remote_copy(
                src_ref=send_ref.at[me], dst_ref=recv_ref.at[peer],
                send_sem=send_sems.at[k - 1], recv_sem=recv_sems.at[k - 1],
                device_id=(px, py, c), device_id_type=pl.DeviceIdType.MESH)
            copies.append((out, back))
        for out, back in copies:
            out.wait_send()
            back.wait_recv()
        own.wait()

    return pl.pallas_call(
        body, name=name,
        in_specs=[pl.BlockSpec(memory_space=pl.ANY)],
        out_specs=pl.BlockSpec(memory_space=pl.ANY),
        out_shape=jax.ShapeDtypeStruct(send.shape, send.dtype),
        scratch_shapes=[pltpu.SemaphoreType.DMA((3,)), pltpu.SemaphoreType.DMA((3,)), pltpu.SemaphoreType.DMA(())],
    )(send)


def _reduce_scatter(parts):
    lanes = 128
    c = lax.axis_index("c")

    def core_half(core):
        return jnp.concatenate(
            [lax.dynamic_index_in_dim(p.reshape(4, 2, p.shape[1], lanes), core, axis=1, keepdims=False)
             for p in parts], axis=1)

    mine = core_half(c)
    rows = mine.shape[1]
    mine = mine.reshape(4 * rows, lanes)
    theirs = core_half(1 - c).reshape(4 * rows, lanes)
    from_sibling = _sibling_swap("grads_to_sibling", theirs)
    tm = max(t for t in range(16, 4 * ADAM_TM + 1, 16) if (4 * rows) % t == 0)
    chip_sum = _rowwise("grads_chip_sum", lambda a, b: (a.astype(F32) + b.astype(F32),),
                        [_rows(mine), _rows(from_sibling)], [((4 * rows, lanes), BF16, "rows")],
                        rows=4 * rows, tm=tm)[0]
    return _chip_exchange("scatter_grads", chip_sum.reshape(4, rows, lanes))


def _adam(name, recv, w, m, v, *, tm=None):
    R = w.shape[0]
    n_parts = recv.shape[0]
    tm = max(t for t in range(8, min(tm or ADAM_TM, R) + 1, 8) if R % t == 0)
    c1 = 1.0 / (1.0 - ADAM_B1 ** ADAM_STEP)
    c2 = 1.0 / (1.0 - ADAM_B2 ** ADAM_STEP)

    def body(r_ref, w_ref, m_ref, v_ref, g_ref, d_ref, nm_ref, nv_ref):
        g = r_ref[0].astype(F32)
        for j in range(1, n_parts):
            g = g + r_ref[j].astype(F32)
        wv = w_ref[...]
        nm = ADAM_B1 * m_ref[...] + (1.0 - ADAM_B1) * g
        nv = ADAM_B2 * v_ref[...] + (1.0 - ADAM_B2) * (g * g)
        m_hat = nm * c1
        v_hat = nv * c2
        g_ref[...] = g
        d_ref[...] = -ADAM_LR * (m_hat / (jnp.sqrt(v_hat) + ADAM_EPS) + ADAM_WD * wv)
        nm_ref[...] = nm
        nv_ref[...] = nv

    row = pl.BlockSpec((tm, 128), lambda i: (i, 0))
    return pl.pallas_call(
        body, name=name, grid=(R // tm,),
        in_specs=[pl.BlockSpec((n_parts, tm, 128), lambda i: (0, i, 0)), row, row, row],
        out_specs=[row, row, row, row],
        out_shape=[jax.ShapeDtypeStruct((R, 128), F32)] * 4,
        compiler_params=_cparams("parallel"),
    )(recv, w, m, v)


def _pack_rows(arrs):
    return jnp.concatenate([a.reshape(-1, 128) for a in arrs], axis=0)


def _unstack(g, shape, axis):
    t = jnp.moveaxis(g, 0, axis)
    return t.reshape(shape)


def _restack(full, axis):
    s = full.shape
    t = full.reshape(s[:axis] + (N_DEV, s[axis] // N_DEV) + s[axis + 1:])
    return jnp.moveaxis(t, axis, 0)


def _pad_heads(w, heads, hd, axis):
    s = w.shape
    t = w.reshape(s[:axis] + (heads, hd) + s[axis + 1:])
    pad = [(0, 0)] * t.ndim
    pad[axis + 1] = (0, HEAD_PAD - hd)
    t = jnp.pad(t, pad)
    return t.reshape(s[:axis] + (heads * HEAD_PAD,) + s[axis + 1:])


def _unpad_heads(w, heads, hd, axis):
    s = w.shape
    t = w.reshape(s[:axis] + (heads, HEAD_PAD) + s[axis + 1:])
    t = lax.slice_in_dim(t, 0, hd, axis=axis + 1)
    return t.reshape(s[:axis] + (heads * hd,) + s[axis + 1:])


def _layer_weights(full, l):
    w_in = full["w_in"][l]
    cq, kva, qs, ks, vs, qm, gates = (w_in[:, 0:256], w_in[:, 256:416], w_in[:, 416:928], w_in[:, 928:1056],
                                       w_in[:, 1056:1184], w_in[:, 1184:1696], w_in[:, 1696:4768])
    wa = jnp.concatenate([gates, cq, jnp.pad(kva, ((0, 0), (0, 96)))], axis=1)
    wb = jnp.concatenate([_pad_heads(qs, SWA_HEADS, SWA_HD, 1), qm, _pad_heads(ks, SWA_KV_HEADS, SWA_HD, 1),
                          _pad_heads(vs, SWA_KV_HEADS, SWA_HD, 1)], axis=1)
    wuq = _pad_heads(full["w_uq"][l], MLA_HEADS, MLA_NOPE + MLA_ROPE, 1)
    ukv = full["w_ukv"][l].reshape(MLA_KV_LORA, MLA_HEADS, MLA_NOPE + MLA_V)
    wuk = _pad_heads(ukv[:, :, :MLA_NOPE].reshape(MLA_KV_LORA, -1), MLA_HEADS, MLA_NOPE, 1)
    wuv = _pad_heads(ukv[:, :, MLA_NOPE:].reshape(MLA_KV_LORA, -1), MLA_HEADS, MLA_V, 1)
    wo_mla = _pad_heads(full["w_o_mla"][l], MLA_HEADS, MLA_V, 0)
    wo_swa = _pad_heads(full["w_o_swa"][l], SWA_HEADS, SWA_HD, 0)
    wo_mem = full["w_o_mem"][l]
    w = dict(wag=wa[:, :3072], wat=wa[:, 3072:], wb=wb, wuq=wuq, wuk=wuk, wuv=wuv, wo_mla=wo_mla, wo_swa=wo_swa,
             wo_mem=wo_mem, wmem=full["w_mem_kv"][l], wout=full["w_out"][l], wup=full["w_up"][l],
             wdown=full["w_down"][l])
    w.update({k + "_t": v.T for k, v in w.items()})
    return w


def _layer_weight_grads(g):
    dwa_g, dwa_t, dwb = g["wag"], g["wat"], g["wb"]
    d_in = jnp.concatenate([
        dwa_t[:, 0:256], dwa_t[:, 256:416],
        _unpad_heads(dwb[:, 0:1024], SWA_HEADS, SWA_HD, 1),
        _unpad_heads(dwb[:, 1536:1792], SWA_KV_HEADS, SWA_HD, 1),
        _unpad_heads(dwb[:, 1792:2048], SWA_KV_HEADS, SWA_HD, 1),
        dwb[:, 1024:1536], dwa_g], axis=1)
    duk = _unpad_heads(g["wuk"], MLA_HEADS, MLA_NOPE, 1).reshape(MLA_KV_LORA, MLA_HEADS, MLA_NOPE)
    duv = _unpad_heads(g["wuv"], MLA_HEADS, MLA_V, 1).reshape(MLA_KV_LORA, MLA_HEADS, MLA_V)
    return dict(
        w_in=d_in,
        w_uq=_unpad_heads(g["wuq"], MLA_HEADS, MLA_NOPE + MLA_ROPE, 1),
        w_ukv=jnp.concatenate([duk, duv], axis=2).reshape(MLA_KV_LORA, -1),
        w_mem_kv=g["wmem"],
        w_o_mla=_unpad_heads(g["wo_mla"], MLA_HEADS, MLA_V, 0),
        w_o_swa=_unpad_heads(g["wo_swa"], SWA_HEADS, SWA_HD, 0),
        w_o_mem=g["wo_mem"], w_out=g["wout"], w_up=g["wup"], w_down=g["wdown"])


def _rope_tables(S):
    pos = jnp.arange(S, dtype=F32)
    inv = 1.0 / (ROPE_THETA ** (jnp.arange(0, MLA_ROPE, 2, dtype=F32) / MLA_ROPE))
    ang = pos[:, None] * inv[None, :]
    cos, sin = jnp.cos(ang), jnp.sin(ang)
    z16 = jnp.zeros((S, 16), F32)
    z32 = jnp.zeros((S, 32), F32)
    c = jnp.concatenate([jnp.ones((S, 64), F32), cos, cos, z32], axis=1)
    ck = jnp.concatenate([jnp.zeros((S, 64), F32), cos, cos, z32], axis=1)
    s1 = jnp.concatenate([jnp.zeros((S, 80), F32), sin, z32], axis=1)
    s2 = jnp.concatenate([jnp.zeros((S, 64), F32), -sin, z16, z32], axis=1)
    return c, ck, s1, s2


def _t5_bucket(dist):
    n = jnp.maximum(dist, 0)
    max_exact = REL_BUCKETS // 2
    nf = jnp.maximum(n, 1).astype(F32)
    large = max_exact + (jnp.log(nf / max_exact) / math.log(REL_MAX_DIST / max_exact)
                         * (REL_BUCKETS - max_exact)).astype(jnp.int32)
    large = jnp.minimum(large, REL_BUCKETS - 1)
    return jnp.where(n < max_exact, n, large)


def _bias_onehot():
    qi = jnp.arange(WINDOW)[:, None]
    kj = jnp.arange(2 * WINDOW)[None, :]
    dist = qi + WINDOW - kj
    valid = (dist >= 0) & (dist < WINDOW)
    bucket = _t5_bucket(dist)
    onehot = (bucket[None] == jnp.arange(REL_BUCKETS)[:, None, None]) & valid[None]
    return (onehot.reshape(REL_BUCKETS, -1).astype(F32),
            jnp.where(valid, 0.0, NEG).astype(F32).reshape(1, -1))


def _rstd(x):
    return lax.rsqrt(jnp.mean(x * x, axis=-1, keepdims=True) + EPS)


def _norm_bwd(dh, x, g):
    r = _rstd(x)
    xh = x * r
    w = dh * g
    dx = r * (w - xh * jnp.mean(w * xh, axis=-1, keepdims=True))
    return dx, jnp.sum(dh * xh, axis=0, keepdims=True)


def _tile_lanes(t, n):
    return jnp.tile(t, (1, n // t.shape[1])) if n != t.shape[1] else t


def _rope_fwd(a, c, s1, s2):
    n = a.shape[1]
    return (a * _tile_lanes(c, n) + pltpu.roll(a, 16, 1) * _tile_lanes(s1, n)
            + pltpu.roll(a, n - 16, 1) * _tile_lanes(s2, n))


def _rope_bwd(d, c, s1, s2):
    n = d.shape[1]
    return (d * _tile_lanes(c, n) + pltpu.roll(d * _tile_lanes(s1, n), n - 16, 1)
            + pltpu.roll(d * _tile_lanes(s2, n), 16, 1))


def _sigmoid(x):
    return 1.0 / (1.0 + jnp.exp(-x))


def _rmsnorm(name, x, g, dtype):
    def fn(xv, gv):
        return ((xv * _rstd(xv)) * gv,)
    return _rowwise(name, fn, [_rows(x), _full(g)], [(x.shape, dtype, "rows")], rows=x.shape[0])[0]


def _residual_norm_bwd(name, dres, dh, x, g):
    def fn(dr, dhv, xv, gv):
        dx, dg = _norm_bwd(dhv, xv, gv)
        return dr + dx, dg
    return _rowwise(name, fn, [_rows(dres), _rows(dh), _rows(x), _full(g)],
                    [(x.shape, F32, "rows"), (g.shape, F32, "acc")], rows=x.shape[0])


def _norm_bwd_epilogue(dh, dres, x, g):
    dx, dg = _norm_bwd(dh, x, g)
    return dres + dx, dg


def _add_and_norm(acc, r, g):
    xs = acc + r
    return xs, xs * _rstd(xs) * g


def _layer_fwd(l, x, h, mem, w, p, next_norm, tabs, swa_bias, S):
    c, ck, s1, s2 = tabs
    n = f"l{l}_"
    gates = _mm(n + "proj_gates", h, w["wag"], [BF16], tm=MM_TM_BF16)
    proj_a = _mm(n + "proj_tail", h, w["wat"], [F32])
    proj_b = _mm(n + "proj_b", h, w["wb"], [BF16], tm=MM_TM_BF16)

    def prep(cq, kva, qn, kvn, ckv, s1v, s2v):
        cqn = cq * _rstd(cq) * qn
        ckv_ = kva[:, :128]
        ckvn = ckv_ * _rstd(ckv_) * kvn
        pe = pltpu.roll(kva[:, 128:], 64, 1)
        return cqn, ckvn, _rope_fwd(pe, ckv, s1v, s2v)

    cqn, ckvn, kpe = _rowwise(
        n + "mla_prep", prep,
        [_rows(proj_a, 256, 0), _rows(proj_a, 256, 1), _full(p["mla_q_norm"]), _full(p["mla_kv_norm"]),
         _rows(ck), _rows(s1), _rows(s2)],
        [((S, 256), BF16, "rows"), ((S, 128), BF16, "rows"), ((S, 128), F32, "rows")], rows=S)

    q_mla = _mm(n + "q_mla", cqn, w["wuq"], [BF16],
                epi=lambda acc, cv, s1v, s2v: (_rope_fwd(acc, cv, s1v, s2v) * (MLA_SCALE * LOG2E),),
                extras=[(c, "m"), (s1, "m"), (s2, "m")])
    k_mla = _mm(n + "k_mla", ckvn, w["wuk"], [BF16],
                epi=lambda acc, kp: (acc + _tile_lanes(kp, acc.shape[1]),), extras=[(kpe, "m")])
    den = ((jnp.arange(MLA_HEADS * HEAD_PAD) % HEAD_PAD) // 2 == DEN_LANE // 2).astype(F32)[None]
    v_mla = _mm(n + "v_mla", ckvn, w["wuv"], [BF16], epi=lambda acc, dv: (acc + dv,), extras=[(den, "n")])
    o_mla, lse_mla = _causal_fwd(n + "mla_fwd", q_mla, k_mla, v_mla, heads=MLA_HEADS, tile=MLA_TILE,
                                 chunk=MLA_CHUNK_FWD)
    o_swa, lse_swa = _swa_fwd(n + "swa_fwd", proj_b, swa_bias, p["sinks"], tq=SWA_TQ)
    mn = _rmsnorm(n + "mem_norm", mem, p["mem_norm"], BF16)
    kvm = _mm(n + "kv_mem", mn, w["wmem"], [BF16])
    o_mem, lse_mem = _mem_fwd(n + "mem_fwd", proj_b, kvm, tq=MEM_TQ, chunk=MEM_CHUNK)
    t0 = _mm(n + "t_mla", o_mla, w["wo_mla"], [BF16], tm=MM_TM_BF16)
    t1 = _mm(n + "t_swa", o_swa, w["wo_swa"], [BF16], tm=MM_TM_BF16)
    t2 = _mm(n + "t_mem", o_mem, w["wo_mem"], [BF16], tm=MM_TM_BF16)

    def merge(g0, g1, g2, bg, a0, a1, a2):
        y = (_sigmoid(g0 + bg[:, 0:1024]) * a0 + _sigmoid(g1 + bg[:, 1024:2048]) * a1
             + _sigmoid(g2 + bg[:, 2048:3072]) * a2)
        return (y,)

    y = _rowwise(n + "merge", merge,
                 [_rows(gates, 1024, 0), _rows(gates, 1024, 1), _rows(gates, 1024, 2), _full(p["b_gate"]),
                  _rows(t0), _rows(t1), _rows(t2)], [((S, D_MODEL), BF16, "rows")], rows=S)[0]
    x1, h2 = _mm(n + "out_proj", y, w["wout"], [F32, BF16], epi=_add_and_norm,
                 extras=[(x, "mn"), (p["mlp_norm"], "n")], tn=D_MODEL)
    act = _mm(n + "mlp_up", h2, w["wup"], [BF16], epi=lambda acc: (jnp.square(jnp.maximum(acc, 0.0)),),
              tm=MM_TM_BF16)
    if next_norm is None:
        x2 = _mm(n + "mlp_down", act, w["wdown"], [F32], epi=lambda acc, r: (acc + r,), extras=[(x1, "mn")],
                 tk=MM_TK_DEEP)
        h_next = None
    else:
        x2, h_next = _mm(n + "mlp_down", act, w["wdown"], [F32, BF16], epi=_add_and_norm,
                         extras=[(x1, "mn"), (next_norm, "n")], tn=D_MODEL, tk=MM_TK_DEEP)
    saved = dict(x=x, h=h, gates=gates, proj_a=proj_a, proj_b=proj_b, cqn=cqn, ckvn=ckvn, q_mla=q_mla, k_mla=k_mla, v_mla=v_mla,
                 o_mla=o_mla, lse_mla=lse_mla, o_swa=o_swa, lse_swa=lse_swa, mn=mn, kvm=kvm, o_mem=o_mem,
                 lse_mem=lse_mem, t0=t0, t1=t1, t2=t2, y=y, x1=x1, h2=h2, act=act)
    return x2, h_next, saved


def _layer_bwd(l, dx2, mem, w, p, tabs, swa_bias, sv, S):
    c, ck, s1, s2 = tabs
    n = f"l{l}_b_"
    gw = {}
    gs = {}
    du = _mm(n + "d_act", dx2, w["wdown_t"], [BF16],
             epi=lambda acc, av: (acc * (2.0 * jnp.sqrt(av.astype(F32))),), extras=[(sv["act"], "mn")],
             tm=MM_TM_BF16)
    gw["wdown"] = _mm_tn(n + "g_wdown", sv["act"], dx2)
    gw["wup"] = _mm_tn(n + "g_wup", sv["h2"], du)
    dx1, gs["mlp_norm"] = _mm(n + "d_h2", du, w["wup_t"], [F32], epi=_norm_bwd_epilogue,
                              extras=[(dx2, "mn"), (sv["x1"], "mn"), (p["mlp_norm"], "n")], tn=D_MODEL, col_sums=1)
    gw["wout"] = _mm_tn(n + "g_wout", sv["y"], dx1)
    dy = _mm(n + "d_y", dx1, w["wout_t"], [F32], tm=MM_TM_BF16)

    def merge_bwd(dyv, g0, g1, g2, bg, a0, a1, a2):
        outs, dgs = [], []
        for b, (gv, av) in enumerate(((g0, a0), (g1, a1), (g2, a2))):
            sg = _sigmoid(gv + bg[:, b * 1024:(b + 1) * 1024])
            outs.append(dyv * sg)
            dgs.append(dyv * av * sg * (1.0 - sg))
        dg = jnp.concatenate(dgs, axis=1)
        return outs[0], outs[1], outs[2], dg, jnp.sum(dg, axis=0, keepdims=True)

    pa = sv["proj_a"]
    gt = sv["gates"]
    dt0, dt1, dt2, dgates, gs["b_gate"] = _rowwise(
        n + "merge", merge_bwd,
        [_rows(dy), _rows(gt, 1024, 0), _rows(gt, 1024, 1), _rows(gt, 1024, 2), _full(p["b_gate"]),
         _rows(sv["t0"]), _rows(sv["t1"]), _rows(sv["t2"])],
        [((S, D_MODEL), BF16, "rows")] * 3 + [((S, 3 * D_MODEL), BF16, "rows"), ((1, 3 * D_MODEL), F32, "acc")],
        rows=S, tm=ROW_TM // 2)
    gw["wo_mla"] = _mm_tn(n + "g_wo_mla", sv["o_mla"], dt0)
    gw["wo_swa"] = _mm_tn(n + "g_wo_swa", sv["o_swa"], dt1)
    gw["wo_mem"] = _mm_tn(n + "g_wo_mem", sv["o_mem"], dt2)
    do_mla = _mm(n + "d_o_mla", dt0, w["wo_mla_t"], [BF16], epi=lambda acc, ov: (_with_neg_delta(acc, ov),),
                 extras=[(sv["o_mla"], "mn")], tn=MLA_HEADS * HEAD_PAD, tm=MM_TM_BF16)
    do_swa = _mm(n + "d_o_swa", dt1, w["wo_swa_t"], [BF16], tm=MM_TM_BF16)
    do_mem = _mm(n + "d_o_mem", dt2, w["wo_mem_t"], [BF16], tm=MM_TM_BF16)
    pb = sv["proj_b"]
    dq_mla, dk_mla, dv_mla = _causal_bwd(
        n + "mla_bwd", sv["q_mla"], sv["k_mla"], sv["v_mla"], do_mla, sv["lse_mla"], heads=MLA_HEADS,
        tile=MLA_TILE_BWD, chunk=MLA_CHUNK)
    dq_swa, dk_swa, dv_swa, dk_edge, dv_edge, dbias, dsink = _swa_bwd(
        n + "swa_bwd", pb, swa_bias, p["sinks"], sv["o_swa"], do_swa, sv["lse_swa"], tq=SWA_TQ)
    dq_mem, dkvm = _mem_bwd(n + "mem_bwd", pb, sv["kvm"], sv["o_mem"], do_mem, sv["lse_mem"], tq=MEM_TQ,
                            chunk=MEM_CHUNK)
    gs["dbias"] = dbias
    gs["sinks"] = dsink[:, 0, 0]
    gw["wmem"] = _mm_tn(n + "g_wmem", sv["mn"], dkvm)
    dmn = _mm(n + "d_mn", dkvm, w["wmem_t"], [F32])
    _, gs["mem_norm"] = _residual_norm_bwd(n + "mem_norm", dmn, dmn, mem, p["mem_norm"])
    dq_pre = _rowwise(n + "q_unrope", lambda d, cv, s1v, s2v: (_rope_bwd(d * MLA_SCALE, cv, s1v, s2v),),
                      [_rows(dq_mla), _rows(c), _rows(s1), _rows(s2)], [((S, 1024), BF16, "rows")], rows=S)[0]
    gw["wuq"] = _mm_tn(n + "g_wuq", sv["cqn"], dq_pre)
    gw["wuk"] = _mm_tn(n + "g_wuk", sv["ckvn"], dk_mla)
    gw["wuv"] = _mm_tn(n + "g_wuv", sv["ckvn"], dv_mla)
    dcqn = _mm(n + "d_cqn", dq_pre, w["wuq_t"], [F32])
    dckvn = _mm(n + "d_ckvn_k", dk_mla, w["wuk_t"], [F32])
    dckvn = _mm(n + "d_ckvn_v", dv_mla, w["wuv_t"], [F32], epi=lambda acc, r: (acc + r,), extras=[(dckvn, "mn")])

    def mla_norm_bwd(dcq_n, dckv_n, dk, cq, kva, qn, kvn, ckv, s1v, s2v):
        dcq, dqn = _norm_bwd(dcq_n, cq, qn)
        dckv, dkvn = _norm_bwd(dckv_n, kva[:, :128], kvn)
        dkpe = dk[:, 0:128]
        for hh in range(1, MLA_HEADS):
            dkpe = dkpe + dk[:, hh * 128:(hh + 1) * 128]
        dpe = pltpu.roll(_rope_bwd(dkpe, ckv, s1v, s2v), 64, 1)
        return jnp.concatenate([dcq, dckv, dpe], axis=1), dqn, dkvn

    dtail, gs["mla_q_norm"], gs["mla_kv_norm"] = _rowwise(
        n + "mla_norm", mla_norm_bwd,
        [_rows(dcqn), _rows(dckvn), _rows(dk_mla), _rows(pa, 256, 0), _rows(pa, 256, 1),
         _full(p["mla_q_norm"]), _full(p["mla_kv_norm"]), _rows(ck), _rows(s1), _rows(s2)],
        [((S, 512), BF16, "rows"), ((1, 256), F32, "acc"), ((1, 128), F32, "acc")], rows=S)

    dproj_b = _dproj_b(n + "dproj_b", dq_swa, dq_mem, dk_swa, dv_swa, dk_edge, dv_edge, tq=SWA_TQ)
    h = sv["h"]
    gw["wag"] = _mm_tn(n + "g_wa_gates", h, dgates)
    gw["wat"] = _mm_tn(n + "g_wa_tail", h, dtail)
    gw["wb"] = _mm_tn(n + "g_wb", h, dproj_b)
    dh = _mm(n + "d_h_gates", dgates, w["wag_t"], [F32], tm=MM_TM_BF16)
    dh = _mm(n + "d_h_tail", dtail, w["wat_t"], [F32], epi=lambda acc, r: (acc + r,), extras=[(dh, "mn")])
    dx, gs["attn_norm"] = _mm(n + "d_h_b", dproj_b, w["wb_t"], [F32],
                              epi=lambda acc, prev, dr, xv, gv: _norm_bwd_epilogue(acc + prev, dr, xv, gv),
                              extras=[(dh, "mn"), (dx1, "mn"), (sv["x"], "mn"), (p["attn_norm"], "n")],
                              tn=D_MODEL, tm=MM_TM // 2, col_sums=1)
    return dx, gw, gs


def _local_step(x, mem, loss_target, full, small):
    S = x.shape[0]
    tabs = _rope_tables(S)
    onehot, band = _bias_onehot()
    hi = lax.Precision.HIGHEST
    swa_bias = _mm("swa_bias", small["rel_bias"].T, onehot, [F32], epi=lambda acc, mk: (acc + mk,),
                   extras=[(band, "n")], cast=None, precision=hi, tn=8192).reshape(SWA_HEADS, WINDOW, 2 * WINDOW)
    ws, ps = [], []
    for l in range(DEPTH):
        ws.append(_layer_weights(full, l))
        ps.append(dict(
            attn_norm=small["attn_norm"][l][None], mem_norm=small["mem_norm"][l][None],
            b_gate=small["b_gate"][l][None], mla_q_norm=small["mla_q_norm"][l][None],
            mla_kv_norm=small["mla_kv_norm"][l][None], mlp_norm=small["mlp_norm"][l][None],
            sinks=jnp.broadcast_to(small["attn_sinks"][l][:, None, None], (SWA_HEADS, 8, 128))))
    saved = []
    xc = x
    hc = _rmsnorm("l0_attn_norm", x, ps[0]["attn_norm"], BF16)
    for l in range(DEPTH):
        next_norm = ps[l + 1]["attn_norm"] if l + 1 < DEPTH else None
        xc, hc, sv = _layer_fwd(l, xc, hc, mem, ws[l], ps[l], next_norm, tabs, swa_bias, S)
        saved.append(sv)

    fn_g = small["final_norm"][None]

    def loss_fn(xv, gv, tv):
        r = _rstd(xv)
        xh = xv * r
        err = xh * gv - tv
        dyv = err * (1.0 / D_MODEL)
        wv = dyv * gv
        dx = r * (wv - xh * jnp.mean(wv * xh, axis=-1, keepdims=True))
        part = 0.5 * jnp.sum(err * err) * (1.0 / D_MODEL)
        return dx, jnp.sum(dyv * xh, axis=0, keepdims=True), jnp.zeros((8, 128), F32) + part

    dx, g_final, loss_acc = _rowwise(
        "loss", loss_fn, [_rows(xc), _full(fn_g), _rows(loss_target)],
        [((S, D_MODEL), F32, "rows"), ((1, D_MODEL), F32, "acc"), ((8, 128), F32, "acc")], rows=S)

    gws, gss = [None] * DEPTH, [None] * DEPTH
    for l in reversed(range(DEPTH)):
        dx, gw, gs = _layer_bwd(l, dx, mem, ws[l], ps[l], tabs, swa_bias, saved[l], S)
        gws[l] = _layer_weight_grads(gw)
        gss[l] = gs

    dbias = (gss[0]["dbias"] + gss[1]["dbias"]).reshape(SWA_HEADS, -1)
    g_rel = _mm("g_rel_bias", dbias, onehot.T, [F32], cast=None, precision=hi, tk=8192).T
    wgrads = {k: jnp.stack([gws[l][k] for l in range(DEPTH)]) for k in gws[0]}
    sgrads = dict(
        rel_bias=g_rel,
        final_norm=g_final[0],
        attn_sinks=jnp.stack([gss[l]["sinks"] for l in range(DEPTH)]),
        **{k: jnp.concatenate([gss[l][k] for l in range(DEPTH)], axis=0)
           for k in ("attn_norm", "mem_norm", "b_gate", "mla_q_norm", "mla_kv_norm", "mlp_norm")})
    return loss_acc[0, 0], dx, wgrads, sgrads


def _pack_small(vals, loss):
    rows = []
    for name, shape in SMALL:
        flat = vals[name].astype(F32).reshape(-1)
        pad = (-flat.shape[0]) % 1024
        rows.append(jnp.pad(flat, (0, pad)).reshape(-1, 128))
    rows.append(jnp.zeros((8, 128), F32) + loss)
    return jnp.concatenate(rows, axis=0)


def _unpack_small(packed):
    out, r = {}, 0
    for name, shape in SMALL:
        size = math.prod(shape)
        nrows = 8 * -(-size // 1024)
        out[name] = packed[r:r + nrows].reshape(-1)[:size].reshape(shape)
        r += nrows
    return out, packed[r, 0]


def kernel(x, mem, rel_bias, attn_norm, mem_norm, w_in, b_gate, mla_q_norm, w_uq, mla_kv_norm, w_ukv, attn_sinks, w_mem_kv, w_o_mla, w_o_swa, w_o_mem, w_out, mlp_norm, w_up, w_down, final_norm, loss_target, m_rel_bias, m_attn_norm, m_mem_norm, m_w_in, m_b_gate, m_mla_q_norm, m_w_uq, m_mla_kv_norm, m_w_ukv, m_attn_sinks, m_w_mem_kv, m_w_o_mla, m_w_o_swa, m_w_o_mem, m_w_out, m_mlp_norm, m_w_up, m_w_down, m_final_norm, v_rel_bias, v_attn_norm, v_mem_norm, v_w_in, v_b_gate, v_mla_q_norm, v_w_uq, v_mla_kv_norm, v_w_ukv, v_attn_sinks, v_w_mem_kv, v_w_o_mla, v_w_o_swa, v_w_o_mem, v_w_out, v_mlp_norm, v_w_up, v_w_down, v_final_norm):
    wv = dict(rel_bias=rel_bias, attn_norm=attn_norm, mem_norm=mem_norm, w_in=w_in, b_gate=b_gate,
              mla_q_norm=mla_q_norm, w_uq=w_uq, mla_kv_norm=mla_kv_norm, w_ukv=w_ukv, attn_sinks=attn_sinks,
              w_mem_kv=w_mem_kv, w_o_mla=w_o_mla, w_o_swa=w_o_swa, w_o_mem=w_o_mem, w_out=w_out,
              mlp_norm=mlp_norm, w_up=w_up, w_down=w_down, final_norm=final_norm)
    mv = dict(rel_bias=m_rel_bias, attn_norm=m_attn_norm, mem_norm=m_mem_norm, w_in=m_w_in, b_gate=m_b_gate,
              mla_q_norm=m_mla_q_norm, w_uq=m_w_uq, mla_kv_norm=m_mla_kv_norm, w_ukv=m_w_ukv,
              attn_sinks=m_attn_sinks, w_mem_kv=m_w_mem_kv, w_o_mla=m_w_o_mla, w_o_swa=m_w_o_swa,
              w_o_mem=m_w_o_mem, w_out=m_w_out, mlp_norm=m_mlp_norm, w_up=m_w_up, w_down=m_w_down,
              final_norm=m_final_norm)
    vv = dict(rel_bias=v_rel_bias, attn_norm=v_attn_norm, mem_norm=v_mem_norm, w_in=v_w_in, b_gate=v_b_gate,
              mla_q_norm=v_mla_q_norm, w_uq=v_w_uq, mla_kv_norm=v_mla_kv_norm, w_ukv=v_w_ukv,
              attn_sinks=v_attn_sinks, w_mem_kv=v_w_mem_kv, w_o_mla=v_w_o_mla, w_o_swa=v_w_o_swa,
              w_o_mem=v_w_o_mem, w_out=v_w_out, mlp_norm=v_mlp_norm, w_up=v_w_up, w_down=v_w_down,
              final_norm=v_final_norm)

    shard_rows = [math.prod(_shard_shape(shape, axis)) // 128 for _, shape, axis in WSPECS]
    gathered = _gather_forwarded("gather_weights", _pack_rows([wv[name].astype(BF16) for name, _, _ in WSPECS]))
    full, r = {}, 0
    for (name, shape, axis), nr in zip(WSPECS, shard_rows):
        full[name] = _unstack(gathered[:, r:r + nr].reshape((N_DEV,) + _shard_shape(shape, axis)), shape, axis)
        r += nr

    loss_part, grad_x, wgrads, sgrads = _local_step(x[0], mem[0], loss_target[0], full,
                                                    {name: wv[name] for name, _ in SMALL})

    recv = _reduce_scatter([_restack(wgrads[name], axis).astype(BF16).reshape(N_DEV, -1, 128)
                            for name, _, axis in WSPECS])
    outs = _adam("adam_sharded", recv, *[_pack_rows([d[name] for name, _, _ in WSPECS]) for d in (wv, mv, vv)])
    res = {}
    r = 0
    for (name, shape, axis), nr in zip(WSPECS, shard_rows):
        res[name] = [o[r:r + nr].reshape(_shard_shape(shape, axis)) for o in outs]
        r += nr

    small_recv = _gather_forwarded("gather_small", _pack_small(sgrads, loss_part))
    zero = jnp.zeros((), F32)
    souts = _adam("adam_small", small_recv, *[_pack_small(d, zero) for d in (wv, mv, vv)])
    loss = None
    for i, o in enumerate(souts):
        vals, extra = _unpack_small(o)
        if i == 0:
            loss = extra
        for name, _ in SMALL:
            res.setdefault(name, []).append(vals[name])

    out = [loss, grad_x[None]]
    for i in range(4):
        out.extend(res[name][i] for name in WEIGHT_ORDER)
    return tuple(out)
```

```python
import math

import jax
import jax.numpy as jnp
from jax import lax
from jax.experimental import pallas as pl
from jax.experimental.pallas import tpu as pltpu

F32 = jnp.float32
BF16 = jnp.bfloat16

N_DEV = 8
D_MODEL = 1024
DEPTH = 2
MLA_HEADS = 8
MLA_Q_LORA = 256
MLA_KV_LORA = 128
MLA_NOPE = 64
MLA_ROPE = 32
MLA_V = 64
ROPE_THETA = 10000.0
SWA_HEADS = 8
SWA_KV_HEADS = 2
SWA_HD = 64
WINDOW = 128
REL_BUCKETS = 32
REL_MAX_DIST = 128
MEM_LEN = 256
MEM_HEADS = 4
MEM_HD = 128
D_FF = 4 * D_MODEL
EPS = 1e-6
HEAD_PAD = 128
ADAM_LR = 0.001
ADAM_B1 = 0.9
ADAM_B2 = 0.999
ADAM_EPS = 1e-08
ADAM_WD = 0.01
ADAM_STEP = 10

NEG = -1e30
VMEM_LIMIT = 48 * 1024 * 1024

MM_TM = 1024
MM_TN = 1024
MM_TK = 1024
TN_T1 = 1024
TN_TN = 1024
TN_TS = 2048
MM_TK_DEEP = 2048
MM_TM_BF16 = 2048
ROW_TM = 1024
MLA_TILE = 4096
MLA_TILE_BWD = 2048
MLA_BWD_VMEM_LIMIT = VMEM_LIMIT
MLA_CHUNK = 256
MLA_CHUNK_FWD = 512
MLA_QK = MLA_NOPE + MLA_ROPE
MLA_SCALE = MLA_QK ** -0.5
LOG2E = math.log2(math.e)
DEN_LANE = MLA_V
SWA_TQ = 1024
SWA_AHEAD = 1
MEM_TQ = 1024
MEM_CHUNK = 256
ADAM_TM = 2000

WSPECS = (
    ("w_in", (DEPTH, D_MODEL, 4768), 2),
    ("w_uq", (DEPTH, MLA_Q_LORA, 768), 2),
    ("w_ukv", (DEPTH, MLA_KV_LORA, 1024), 2),
    ("w_mem_kv", (DEPTH, D_MODEL, 1024), 1),
    ("w_o_mla", (DEPTH, 512, D_MODEL), 2),
    ("w_o_swa", (DEPTH, 512, D_MODEL), 2),
    ("w_o_mem", (DEPTH, 512, D_MODEL), 2),
    ("w_out", (DEPTH, D_MODEL, D_MODEL), 1),
    ("w_up", (DEPTH, D_MODEL, D_FF), 2),
    ("w_down", (DEPTH, D_FF, D_MODEL), 1),
)
SMALL = (
    ("rel_bias", (REL_BUCKETS, SWA_HEADS)),
    ("attn_norm", (DEPTH, D_MODEL)),
    ("mem_norm", (DEPTH, D_MODEL)),
    ("b_gate", (DEPTH, 3 * D_MODEL)),
    ("mla_q_norm", (DEPTH, MLA_Q_LORA)),
    ("mla_kv_norm", (DEPTH, MLA_KV_LORA)),
    ("attn_sinks", (DEPTH, SWA_HEADS)),
    ("mlp_norm", (DEPTH, D_MODEL)),
    ("final_norm", (D_MODEL,)),
)
WEIGHT_ORDER = ("rel_bias", "attn_norm", "mem_norm", "w_in", "b_gate", "mla_q_norm", "w_uq", "mla_kv_norm",
                "w_ukv", "attn_sinks", "w_mem_kv", "w_o_mla", "w_o_swa", "w_o_mem", "w_out", "mlp_norm",
                "w_up", "w_down", "final_norm")


def _cparams(*sem):
    return pltpu.CompilerParams(dimension_semantics=sem, vmem_limit_bytes=VMEM_LIMIT)


def _shard_shape(shape, axis):
    s = list(shape)
    s[axis] //= N_DEV
    return tuple(s)


def _mm(name, a, b, out_dtypes, *, epi=None, extras=(), a_fn=None, cast=BF16, precision=None,
        tm=None, tn=None, tk=None, col_sums=0):
    M, K = a.shape
    K2, N = b.shape
    assert K == K2, (name, a.shape, b.shape)
    tm = min(tm or MM_TM, M)
    tn = min(tn or MM_TN, N)
    tk = min(tk or MM_TK, K)
    assert M % tm == 0 and N % tn == 0 and K % tk == 0, (name, a.shape, b.shape, tm, tn, tk)
    assert col_sums == 0 or tn == N, (name, tn, N)
    nk = K // tk
    n_ex = len(extras)
    n_out = len(out_dtypes)

    def body(*refs):
        a_ref, b_ref = refs[0], refs[1]
        ex_refs = refs[2:2 + n_ex]
        out_refs = refs[2 + n_ex:2 + n_ex + n_out]
        av = a_ref[...]
        if a_fn is not None:
            av = a_fn(av)
        bv = b_ref[...]
        if cast is not None:
            av = av.astype(cast)
            bv = bv.astype(cast)
        part = jnp.dot(av, bv, preferred_element_type=F32, precision=precision)

        def finish(acc):
            outs = epi(acc, *[r[...] for r in ex_refs]) if epi is not None else (acc,)
            for r, o in zip(out_refs, outs[:n_out]):
                r[...] = o.astype(r.dtype)
            i = pl.program_id(0)
            for r, o in zip(refs[2 + n_ex + n_out:2 + n_ex + n_out + col_sums], outs[n_out:]):
                @pl.when(i == 0)
                def _(r=r, o=o):
                    r[...] = o

                @pl.when(i > 0)
                def _(r=r, o=o):
                    r[...] += o

        if nk == 1:
            finish(part)
        else:
            acc_ref = refs[-1]
            k = pl.program_id(2)

            @pl.when(k == 0)
            def _():
                acc_ref[...] = part

            @pl.when(k > 0)
            def _():
                acc_ref[...] += part

            @pl.when(k == nk - 1)
            def _():
                finish(acc_ref[...])

    in_specs = [pl.BlockSpec((tm, tk), lambda i, j, k: (i, k)),
                pl.BlockSpec((tk, tn), lambda i, j, k: (k, j))]
    for arr, kind in extras:
        if kind == "mn":
            in_specs.append(pl.BlockSpec((tm, tn), lambda i, j, k: (i, j)))
        elif kind == "m":
            in_specs.append(pl.BlockSpec((tm, arr.shape[1]), lambda i, j, k: (i, 0)))
        else:
            in_specs.append(pl.BlockSpec((1, tn), lambda i, j, k: (0, j)))
    outs = pl.pallas_call(
        body, name=name, grid=(M // tm, N // tn, nk),
        in_specs=in_specs,
        out_specs=([pl.BlockSpec((tm, tn), lambda i, j, k: (i, j)) for _ in out_dtypes]
                   + [pl.BlockSpec((1, tn), lambda i, j, k: (0, 0))] * col_sums),
        out_shape=([jax.ShapeDtypeStruct((M, N), dt) for dt in out_dtypes]
                   + [jax.ShapeDtypeStruct((1, N), F32)] * col_sums),
        scratch_shapes=[pltpu.VMEM((tm, tn), F32)] if nk > 1 else [],
        compiler_params=(_cparams("arbitrary", "arbitrary", "arbitrary") if col_sums
                         else _cparams("parallel", "parallel", "arbitrary")),
    )(a, b, *[arr for arr, _ in extras])
    return outs[0] if n_out + col_sums == 1 else outs


def _mm_tn(name, a, b, *, t1=None, tn=None, ts=None):
    S, K1 = a.shape
    S2, N = b.shape
    assert S == S2, (name, a.shape, b.shape)
    t1 = min(t1 or TN_T1, K1)
    tn = min(tn or TN_TN, N)
    ts = min(ts or TN_TS, S)
    assert K1 % t1 == 0 and N % tn == 0 and S % ts == 0, (name, a.shape, b.shape)

    def body(a_ref, b_ref, o_ref):
        s = pl.program_id(2)
        part = lax.dot_general(a_ref[...].astype(BF16), b_ref[...].astype(BF16),
                               (((0,), (0,)), ((), ())), preferred_element_type=F32)

        @pl.when(s == 0)
        def _():
            o_ref[...] = part

        @pl.when(s > 0)
        def _():
            o_ref[...] += part

    return pl.pallas_call(
        body, name=name, grid=(K1 // t1, N // tn, S // ts),
        in_specs=[pl.BlockSpec((ts, t1), lambda i, j, s: (s, i)),
                  pl.BlockSpec((ts, tn), lambda i, j, s: (s, j))],
        out_specs=pl.BlockSpec((t1, tn), lambda i, j, s: (i, j)),
        out_shape=jax.ShapeDtypeStruct((K1, N), F32),
        compiler_params=_cparams("parallel", "parallel", "arbitrary"),
    )(a, b)


def _rows(arr, width=None, blk=0):
    return (arr, ("rows", arr.shape[1] if width is None else width, blk))


def _full(arr):
    return (arr, ("full",))


def _rowwise(name, fn, ins, outs, *, rows, tm=None):
    tm = min(tm or ROW_TM, rows)
    assert rows % tm == 0, (name, rows, tm)
    n_in = len(ins)

    def body(*refs):
        i = pl.program_id(0)
        vals = fn(*[r[...] for r in refs[:n_in]])
        for (shape, dt, kind), r, v in zip(outs, refs[n_in:], vals):
            if kind == "rows":
                r[...] = v.astype(dt)
            else:
                @pl.when(i == 0)
                def _(r=r, v=v):
                    r[...] = v

                @pl.when(i > 0)
                def _(r=r, v=v):
                    r[...] += v

    in_specs = []
    for arr, spec in ins:
        if spec[0] == "rows":
            in_specs.append(pl.BlockSpec((tm, spec[1]), lambda i, b=spec[2]: (i, b)))
        else:
            in_specs.append(pl.BlockSpec(arr.shape, lambda i, n=arr.ndim: (0,) * n))
    out_specs = []
    for shape, dt, kind in outs:
        if kind == "rows":
            out_specs.append(pl.BlockSpec((tm, shape[1]), lambda i: (i, 0)))
        else:
            out_specs.append(pl.BlockSpec(shape, lambda i, n=len(shape): (0,) * n))
    res = pl.pallas_call(
        body, name=name, grid=(rows // tm,),
        in_specs=in_specs, out_specs=out_specs,
        out_shape=[jax.ShapeDtypeStruct(shape, dt) for shape, dt, _ in outs],
        compiler_params=_cparams("arbitrary"),
    )(*[arr for arr, _ in ins])
    return res


MEM_SCALE = MEM_HD ** -0.5
MEM_Q0 = 2
NT_DIMS = (((1,), (1,)), ((), ()))


def _head_lanes(h):
    return slice(h * HEAD_PAD, (h + 1) * HEAD_PAD)


def _mem_fwd(name, proj_b, kvm, *, tq, chunk):
    S = proj_b.shape[0]
    tq = min(tq, S)
    C = min(chunk, tq)
    tiles = [(h, c) for c in range(tq // C) for h in range(MEM_HEADS)]

    def body(q_ref, kv_ref, o_ref, lse_ref):
        def logits(h, c):
            return lax.dot_general(q_ref[c * C:(c + 1) * C, _head_lanes(h)], kv_ref[:, _head_lanes(h)], NT_DIMS,
                                   preferred_element_type=F32) * MEM_SCALE

        nxt = logits(*tiles[0])
        for n, (h, c) in enumerate(tiles):
            s = nxt
            if n + 1 < len(tiles):
                nxt = logits(*tiles[n + 1])
            rows = slice(c * C, (c + 1) * C)
            m = jnp.max(s, axis=1, keepdims=True)
            p = jnp.exp(s - m)
            l = jnp.sum(p, axis=1, keepdims=True)
            o = jnp.dot(p.astype(BF16), kv_ref[:, _head_lanes(MEM_HEADS + h)], preferred_element_type=F32) / l
            o_ref[rows, _head_lanes(h)] = o.astype(o_ref.dtype)
            lse_ref[h, rows, :] = m + jnp.log(l)

    return pl.pallas_call(
        body, name=name, grid=(S // tq,),
        in_specs=[pl.BlockSpec((tq, MEM_HEADS * HEAD_PAD), lambda i: (i, MEM_Q0)),
                  pl.BlockSpec(kvm.shape, lambda i: (0, 0))],
        out_specs=[pl.BlockSpec((tq, MEM_HEADS * HEAD_PAD), lambda i: (i, 0)),
                   pl.BlockSpec((MEM_HEADS, tq, 1), lambda i: (0, i, 0))],
        out_shape=[jax.ShapeDtypeStruct((S, MEM_HEADS * HEAD_PAD), BF16),
                   jax.ShapeDtypeStruct((MEM_HEADS, S, 1), F32)],
        compiler_params=_cparams("parallel"),
    )(proj_b, kvm)


def _mem_bwd(name, proj_b, kvm, o, do, lse, *, tq, chunk):
    S = proj_b.shape[0]
    tq = min(tq, S)
    C = min(chunk, tq)
    nq = S // tq
    tiles = [(h, c) for c in range(tq // C) for h in range(MEM_HEADS)]

    def body(q_ref, kv_ref, o_ref, do_ref, lse_ref, dq_ref, dkv_ref, acc_sc):
        i = pl.program_id(0)

        @pl.when(i == 0)
        def _():
            acc_sc[...] = jnp.zeros(acc_sc.shape, F32)

        def mats(h, c):
            rows = slice(c * C, (c + 1) * C)
            q = q_ref[rows, _head_lanes(h)]
            dov = do_ref[rows, _head_lanes(h)]
            s = lax.dot_general(q, kv_ref[:, _head_lanes(h)], NT_DIMS, preferred_element_type=F32) * MEM_SCALE
            dp = lax.dot_general(dov, kv_ref[:, _head_lanes(MEM_HEADS + h)], NT_DIMS, preferred_element_type=F32)
            return q, dov, s, dp

        nxt = mats(*tiles[0])
        for n, (h, c) in enumerate(tiles):
            q, dov, s, dp = nxt
            if n + 1 < len(tiles):
                nxt = mats(*tiles[n + 1])
            rows = slice(c * C, (c + 1) * C)
            p = jnp.exp(s - lse_ref[h, rows, :])
            delta = jnp.sum(dov.astype(F32) * o_ref[rows, _head_lanes(h)].astype(F32), axis=1, keepdims=True)
            ds = (p * (dp - delta) * MEM_SCALE).astype(BF16)
            dq_ref[rows, _head_lanes(h)] = jnp.dot(ds, kv_ref[:, _head_lanes(h)],
                                                   preferred_element_type=F32).astype(dq_ref.dtype)
            acc_sc[_head_lanes(h), :] += jnp.dot(q.T, ds, preferred_element_type=F32)
            acc_sc[_head_lanes(MEM_HEADS + h), :] += jnp.dot(dov.T, p.astype(BF16), preferred_element_type=F32)

        @pl.when(i == nq - 1)
        def _():
            dkv_ref[...] = acc_sc[...].T

    qblk = pl.BlockSpec((tq, MEM_HEADS * HEAD_PAD), lambda i: (i, 0))
    return pl.pallas_call(
        body, name=name, grid=(nq,),
        in_specs=[pl.BlockSpec((tq, MEM_HEADS * HEAD_PAD), lambda i: (i, MEM_Q0)),
                  pl.BlockSpec(kvm.shape, lambda i: (0, 0)), qblk, qblk,
                  pl.BlockSpec((MEM_HEADS, tq, 1), lambda i: (0, i, 0))],
        out_specs=[qblk, pl.BlockSpec(kvm.shape, lambda i: (0, 0))],
        out_shape=[jax.ShapeDtypeStruct((S, MEM_HEADS * HEAD_PAD), BF16), jax.ShapeDtypeStruct(kvm.shape, F32)],
        scratch_shapes=[pltpu.VMEM((kvm.shape[1], kvm.shape[0]), F32)],
        compiler_params=_cparams("arbitrary"),
    )(proj_b, kvm, o, do, lse)


def _causal_fwd(name, q_arr, k_arr, v_arr, *, heads, tile, chunk):
    S = q_arr.shape[0]
    T = min(tile, S)
    C = min(chunk, T)
    nt = S // T
    nc = T // C

    pairs = [(qi, kk) for qi in range(nt) for kk in range(qi + 1)]
    q_tab = jnp.asarray([p[0] for p in pairs], jnp.int32)
    k_tab = jnp.asarray([p[1] for p in pairs], jnp.int32)

    def body(qt_ref, kt_ref, q_ref, k_ref, v_ref, o_ref, lse_ref, m_sc, acc_sc):
        t = pl.program_id(1)
        qi = qt_ref[t]
        kk = kt_ref[t]

        @pl.when(kk == 0)
        def _():
            m_sc[...] = jnp.full(m_sc.shape, NEG, F32)
            acc_sc[...] = jnp.zeros(acc_sc.shape, F32)

        def logits(c, ncols, masked):
            s = lax.dot_general(q_ref[pl.ds(c * C, C), :], k_ref[0:ncols, :], (((1,), (1,)), ((), ())),
                                preferred_element_type=F32)
            if masked:
                r = c * C + lax.broadcasted_iota(jnp.int32, (C, ncols), 0)
                cidx = lax.broadcasted_iota(jnp.int32, (C, ncols), 1)
                s = jnp.where(cidx <= r, s, NEG)
            return s

        def update(c, ncols, s):
            rows = pl.ds(c * C, C)
            m_prev = m_sc[rows, :]
            m_new = jnp.maximum(m_prev, jnp.max(s, axis=1, keepdims=True))
            p = jnp.exp2(s - m_new).astype(BF16)
            acc_sc[rows, :] = jnp.exp2(m_prev - m_new) * acc_sc[rows, :] + jnp.dot(
                p, v_ref[0:ncols, :], preferred_element_type=F32)
            m_sc[rows, :] = m_new

        def tile_step(ncols_of, masked):
            s = logits(0, ncols_of(0), masked)
            for c in range(nc):
                s_next = logits(c + 1, ncols_of(c + 1), masked) if c + 1 < nc else None
                update(c, ncols_of(c), s)
                s = s_next

        @pl.when(kk < qi)
        def _():
            tile_step(lambda c: T, False)

        @pl.when(kk == qi)
        def _():
            tile_step(lambda c: (c + 1) * C, True)

        @pl.when(kk == qi)
        def _():
            acc = acc_sc[...]
            l = acc[:, DEN_LANE:DEN_LANE + 1]
            o_ref[...] = (acc / l).astype(o_ref.dtype)
            lse_ref[0] = m_sc[...] + jnp.log2(l)

    grid_spec = pltpu.PrefetchScalarGridSpec(
        num_scalar_prefetch=2, grid=(heads, len(pairs)),
        in_specs=[pl.BlockSpec((T, HEAD_PAD), lambda h, t, qt, kt: (qt[t], h)),
                  pl.BlockSpec((T, HEAD_PAD), lambda h, t, qt, kt: (kt[t], h)),
                  pl.BlockSpec((T, HEAD_PAD), lambda h, t, qt, kt: (kt[t], h))],
        out_specs=[pl.BlockSpec((T, HEAD_PAD), lambda h, t, qt, kt: (qt[t], h)),
                   pl.BlockSpec((1, T, 1), lambda h, t, qt, kt: (h, qt[t], 0))],
        scratch_shapes=[pltpu.VMEM((T, 1), F32), pltpu.VMEM((T, HEAD_PAD), F32)])
    return pl.pallas_call(
        body, name=name, grid_spec=grid_spec,
        out_shape=[jax.ShapeDtypeStruct((S, heads * HEAD_PAD), BF16),
                   jax.ShapeDtypeStruct((heads, S, 1), F32)],
        compiler_params=_cparams("parallel", "arbitrary"),
    )(q_tab, k_tab, q_arr, k_arr, v_arr)


def _with_neg_delta(do, o):
    lane = lax.broadcasted_iota(jnp.int32, (1, HEAD_PAD), 1)
    outs = []
    for h in range(do.shape[1] // HEAD_PAD):
        a = do[:, _head_lanes(h)]
        nd = -jnp.sum(a * o[:, _head_lanes(h)].astype(F32), axis=1, keepdims=True)
        hi = nd.astype(BF16).astype(F32)
        a = jnp.where(lane == DEN_LANE, hi, a)
        outs.append(jnp.where(lane == DEN_LANE + 1, nd - hi, a))
    return jnp.concatenate(outs, axis=1)


def _causal_bwd(name, q_arr, k_arr, v_arr, do_arr, lse, *, heads, tile, chunk):
    S = q_arr.shape[0]
    T = min(tile, S)
    C = min(chunk, T)
    nt = S // T
    nc = T // C

    pairs = [(kj, qq) for kj in range(nt) for qq in range(kj, nt)]
    k_tab = jnp.asarray([p[0] for p in pairs], jnp.int32)
    q_tab = jnp.asarray([p[1] for p in pairs], jnp.int32)

    def body(kt_ref, qt_ref, q_ref, k_ref, v_ref, do_ref, lse_ref, dq_ref, dk_ref, dv_ref, dk_sc, dv_sc):
        t = pl.program_id(1)
        kj = kt_ref[t]
        qq = qt_ref[t]
        qb = qq

        @pl.when(t == 0)
        def _():
            dq_ref[...] = jnp.zeros(dq_ref.shape, F32)

        @pl.when(qq == kj)
        def _():
            dk_sc[...] = jnp.zeros(dk_sc.shape, F32)
            dv_sc[...] = jnp.zeros(dv_sc.shape, F32)

        def logits(c, ncols, masked):
            rows = pl.ds(c * C, C)
            s = lax.dot_general(q_ref[rows, :], k_ref[0:ncols, :], (((1,), (1,)), ((), ())),
                                preferred_element_type=F32)
            if masked:
                r = c * C + lax.broadcasted_iota(jnp.int32, (C, ncols), 0)
                cidx = lax.broadcasted_iota(jnp.int32, (C, ncols), 1)
                s = jnp.where(cidx <= r, s, NEG)
            dp = lax.dot_general(do_ref[rows, :], v_ref[0:ncols, :], (((1,), (1,)), ((), ())),
                                 preferred_element_type=F32)
            return s, dp

        def update(c, ncols, s, dp):
            rows = pl.ds(c * C, C)
            p = jnp.exp2(s - lse_ref[0, rows, :])
            ds = (p * dp).astype(BF16)
            dv_sc[:, 0:ncols] += jnp.dot(do_ref[rows, 0:MLA_V].T, p.astype(BF16), preferred_element_type=F32)
            dk_sc[:, 0:ncols] += jnp.dot(q_ref[rows, 0:MLA_QK].T, ds, preferred_element_type=F32)
            row0 = pl.multiple_of(qb * T + c * C, C)
            dq_ref[pl.ds(row0, C), :] += jnp.dot(ds, k_ref[0:ncols, :], preferred_element_type=F32)

        def tile_step(ncols_of, masked):
            cur = logits(0, ncols_of(0), masked)
            for c in range(nc):
                nxt = logits(c + 1, ncols_of(c + 1), masked) if c + 1 < nc else None
                update(c, ncols_of(c), *cur)
                cur = nxt

        @pl.when(qq > kj)
        def _():
            tile_step(lambda c: T, False)

        @pl.when(qq == kj)
        def _():
            tile_step(lambda c: (c + 1) * C, True)

        @pl.when(qq == nt - 1)
        def _():
            dk_ref[:, 0:MLA_QK] = dk_sc[...].T * math.log(2.0)
            dk_ref[:, MLA_QK:] = jnp.zeros((T, HEAD_PAD - MLA_QK), F32)
            dv_ref[:, 0:MLA_V] = dv_sc[...].T
            dv_ref[:, MLA_V:] = jnp.zeros((T, HEAD_PAD - MLA_V), F32)

    qrow = pl.BlockSpec((T, HEAD_PAD), lambda h, t, kt, qt: (qt[t], h))
    krow = pl.BlockSpec((T, HEAD_PAD), lambda h, t, kt, qt: (kt[t], h))
    qcol = pl.BlockSpec((1, T, 1), lambda h, t, kt, qt: (h, qt[t], 0))
    grid_spec = pltpu.PrefetchScalarGridSpec(
        num_scalar_prefetch=2, grid=(heads, len(pairs)),
        in_specs=[qrow, krow, krow, qrow, qcol],
        out_specs=[pl.BlockSpec((S, HEAD_PAD), lambda h, t, kt, qt: (0, h)), krow, krow],
        scratch_shapes=[pltpu.VMEM((MLA_QK, T), F32), pltpu.VMEM((MLA_V, T), F32)])
    return pl.pallas_call(
        body, name=name, grid_spec=grid_spec,
        out_shape=[jax.ShapeDtypeStruct((S, heads * HEAD_PAD), F32)] * 3,
        compiler_params=pltpu.CompilerParams(dimension_semantics=("arbitrary", "arbitrary"),
                                             vmem_limit_bytes=MLA_BWD_VMEM_LIMIT),
    )(k_tab, q_tab, q_arr, k_arr, v_arr, do_arr, lse)


SWA_R = SWA_HEADS // SWA_KV_HEADS
SWA_SCALE = SWA_HD ** -0.5
SWA_Q0, SWA_K0, SWA_V0 = 0, 12, 14


def _swa_specs(tq):
    nsb = tq // WINDOW
    return [
        pl.BlockSpec((tq, SWA_R * HEAD_PAD), lambda g, i: (i, g)),
        pl.BlockSpec((tq, HEAD_PAD), lambda g, i: (i, SWA_K0 + g)),
        pl.BlockSpec((WINDOW, HEAD_PAD), lambda g, i: (jnp.maximum(nsb * i - 1, 0), SWA_K0 + g)),
        pl.BlockSpec((tq, HEAD_PAD), lambda g, i: (i, SWA_V0 + g)),
        pl.BlockSpec((WINDOW, HEAD_PAD), lambda g, i: (jnp.maximum(nsb * i - 1, 0), SWA_V0 + g)),
        pl.BlockSpec((SWA_R, WINDOW, 2 * WINDOW), lambda g, i: (g, 0, 0)),
        pl.BlockSpec((SWA_R, 8, 128), lambda g, i: (g, 0, 0)),
    ]


def _swa_block(i, sb, q_ref, kc_ref, kp_ref, vc_ref, vp_ref, bias, sink):
    rows = slice(sb * WINDOW, (sb + 1) * WINDOW)
    qs = jnp.concatenate([q_ref[rows, hh * HEAD_PAD:(hh + 1) * HEAD_PAD] for hh in range(SWA_R)], axis=0)
    if sb == 0:
        kp, vp = kp_ref[...], vp_ref[...]
    else:
        prev = slice((sb - 1) * WINDOW, sb * WINDOW)
        kp, vp = kc_ref[prev, :], vc_ref[prev, :]
    kk = jnp.concatenate([kp, kc_ref[rows, :]], axis=0)
    vv = jnp.concatenate([vp, vc_ref[rows, :]], axis=0)
    s = lax.dot_general(qs, kk, (((1,), (1,)), ((), ())), preferred_element_type=F32) * SWA_SCALE + bias
    if sb == 0:
        col = lax.broadcasted_iota(jnp.int32, (1, 2 * WINDOW), 1)
        s = s + jnp.where((col < WINDOW) & (i == 0), NEG, 0.0)
    return rows, qs, kk, vv, s


def _stack_heads(ref, rows, lead=None):
    if lead is None:
        return jnp.concatenate([ref[rows, hh * HEAD_PAD:(hh + 1) * HEAD_PAD] for hh in range(SWA_R)], axis=0)
    return jnp.concatenate([ref[hh, rows, :] for hh in range(SWA_R)], axis=0)


def _swa_fwd(name, proj_b, bias, sinks, *, tq):
    S = proj_b.shape[0]
    tq = min(tq, S)
    nsb = tq // WINDOW

    def body(q_ref, kc_ref, kp_ref, vc_ref, vp_ref, bias_ref, sink_ref, o_ref, lse_ref):
        i = pl.program_id(1)
        bias_v = bias_ref[...].reshape(SWA_R * WINDOW, 2 * WINDOW)
        sink = jnp.concatenate([jnp.zeros((WINDOW, 1), F32) + sink_ref[hh, 0:1, 0:1] for hh in range(SWA_R)], axis=0)
        ahead = [_swa_block(i, sb, q_ref, kc_ref, kp_ref, vc_ref, vp_ref, bias_v, sink)
                 for sb in range(min(SWA_AHEAD, nsb))]
        for sb in range(nsb):
            rows, _, _, vv, s = ahead.pop(0)
            if sb + SWA_AHEAD < nsb:
                ahead.append(_swa_block(i, sb + SWA_AHEAD, q_ref, kc_ref, kp_ref, vc_ref, vp_ref, bias_v, sink))
            m = jnp.maximum(jnp.max(s, axis=1, keepdims=True), sink)
            p = jnp.exp(s - m)
            l = jnp.sum(p, axis=1, keepdims=True) + jnp.exp(sink - m)
            o = jnp.dot(p.astype(BF16), vv, preferred_element_type=F32) / l
            lse_v = m + jnp.log(l)
            for hh in range(SWA_R):
                o_ref[rows, hh * HEAD_PAD:(hh + 1) * HEAD_PAD] = o[hh * WINDOW:(hh + 1) * WINDOW].astype(o_ref.dtype)
                lse_ref[hh, rows, :] = lse_v[hh * WINDOW:(hh + 1) * WINDOW]

    return pl.pallas_call(
        body, name=name, grid=(SWA_KV_HEADS, S // tq),
        in_specs=_swa_specs(tq),
        out_specs=[pl.BlockSpec((tq, SWA_R * HEAD_PAD), lambda g, i: (i, g)),
                   pl.BlockSpec((SWA_R, tq, 1), lambda g, i: (g, i, 0))],
        out_shape=[jax.ShapeDtypeStruct((S, SWA_HEADS * HEAD_PAD), BF16),
                   jax.ShapeDtypeStruct((SWA_HEADS, S, 1), F32)],
        compiler_params=_cparams("parallel", "parallel"),
    )(proj_b, proj_b, proj_b, proj_b, proj_b, bias, sinks)


def _swa_bwd(name, proj_b, bias, sinks, o, do, lse, *, tq):
    S = proj_b.shape[0]
    tq = min(tq, S)
    nsb = tq // WINDOW
    nq = S // tq

    def body(q_ref, kc_ref, kp_ref, vc_ref, vp_ref, bias_ref, sink_ref, o_ref, do_ref, lse_ref,
             dq_ref, dk_ref, dv_ref, dke_ref, dve_ref, dbias_ref, dsink_ref):
        i = pl.program_id(1)

        @pl.when(i == 0)
        def _():
            dbias_ref[...] = jnp.zeros(dbias_ref.shape, F32)
            dsink_ref[...] = jnp.zeros(dsink_ref.shape, F32)

        bias_v = bias_ref[...].reshape(SWA_R * WINDOW, 2 * WINDOW)
        sink = jnp.concatenate([jnp.zeros((WINDOW, 1), F32) + sink_ref[hh, 0:1, 0:1] for hh in range(SWA_R)], axis=0)
        dk_own, dv_own, dk_prev, dv_prev = [], [], [], []
        dbias_acc = jnp.zeros((SWA_R * WINDOW, 2 * WINDOW), F32)
        def block(sb):
            rows, qs, kk, vv, s = _swa_block(i, sb, q_ref, kc_ref, kp_ref, vc_ref, vp_ref, bias_v, sink)
            do_s = _stack_heads(do_ref, rows)
            dp = lax.dot_general(do_s, vv, (((1,), (1,)), ((), ())), preferred_element_type=F32)
            return rows, qs, kk, do_s, s, dp

        nxt = block(0)
        for sb in range(nsb):
            rows, qs, kk, do_s, s, dp = nxt
            if sb + 1 < nsb:
                nxt = block(sb + 1)
            lse_v = _stack_heads(lse_ref, rows, lead=True)
            delta = jnp.sum(do_s.astype(F32) * _stack_heads(o_ref, rows).astype(F32), axis=1, keepdims=True)
            p = jnp.exp(s - lse_v)
            dsp = p * (dp - delta)
            dbias_acc = dbias_acc + dsp
            ds = (dsp * SWA_SCALE).astype(BF16)
            dq = jnp.dot(ds, kk, preferred_element_type=F32)
            dkk = jnp.dot(qs.T, ds, preferred_element_type=F32)
            dvv = jnp.dot(do_s.T, p.astype(BF16), preferred_element_type=F32)
            dk_prev.append(dkk[:, :WINDOW].T)
            dk_own.append(dkk[:, WINDOW:].T)
            dv_prev.append(dvv[:, :WINDOW].T)
            dv_own.append(dvv[:, WINDOW:].T)
            psink = jnp.exp(sink - lse_v) * delta
            for hh in range(SWA_R):
                hrows = slice(hh * WINDOW, (hh + 1) * WINDOW)
                dq_ref[rows, hh * HEAD_PAD:(hh + 1) * HEAD_PAD] = dq[hrows].astype(dq_ref.dtype)
                dsink_ref[hh] += jnp.zeros((8, 128), F32) - jnp.sum(psink[hrows])
        dbias_ref[...] += dbias_acc.reshape(SWA_R, WINDOW, 2 * WINDOW)
        for sb in range(nsb):
            rows = slice(sb * WINDOW, (sb + 1) * WINDOW)
            if sb + 1 < nsb:
                dk_ref[rows, :] = dk_own[sb] + dk_prev[sb + 1]
                dv_ref[rows, :] = dv_own[sb] + dv_prev[sb + 1]
            else:
                dk_ref[rows, :] = dk_own[sb]
                dv_ref[rows, :] = dv_own[sb]
        dke_ref[...] = dk_prev[0]
        dve_ref[...] = dv_prev[0]

    in_specs = _swa_specs(tq) + [
        pl.BlockSpec((tq, SWA_R * HEAD_PAD), lambda g, i: (i, g)),
        pl.BlockSpec((tq, SWA_R * HEAD_PAD), lambda g, i: (i, g)),
        pl.BlockSpec((SWA_R, tq, 1), lambda g, i: (g, i, 0)),
    ]
    kv_blk = pl.BlockSpec((tq, HEAD_PAD), lambda g, i: (i, g))
    edge_blk = pl.BlockSpec((WINDOW, HEAD_PAD), lambda g, i: (i, g))
    return pl.pallas_call(
        body, name=name, grid=(SWA_KV_HEADS, nq),
        in_specs=in_specs,
        out_specs=[pl.BlockSpec((tq, SWA_R * HEAD_PAD), lambda g, i: (i, g)), kv_blk, kv_blk, edge_blk, edge_blk,
                   pl.BlockSpec((SWA_R, WINDOW, 2 * WINDOW), lambda g, i: (g, 0, 0)),
                   pl.BlockSpec((SWA_R, 8, 128), lambda g, i: (g, 0, 0))],
        out_shape=[jax.ShapeDtypeStruct((S, SWA_HEADS * HEAD_PAD), BF16),
                   jax.ShapeDtypeStruct((S, SWA_KV_HEADS * HEAD_PAD), F32),
                   jax.ShapeDtypeStruct((S, SWA_KV_HEADS * HEAD_PAD), F32),
                   jax.ShapeDtypeStruct((nq * WINDOW, SWA_KV_HEADS * HEAD_PAD), F32),
                   jax.ShapeDtypeStruct((nq * WINDOW, SWA_KV_HEADS * HEAD_PAD), F32),
                   jax.ShapeDtypeStruct((SWA_HEADS, WINDOW, 2 * WINDOW), F32),
                   jax.ShapeDtypeStruct((SWA_HEADS, 8, 128), F32)],
        compiler_params=_cparams("arbitrary", "arbitrary"),
    )(proj_b, proj_b, proj_b, proj_b, proj_b, bias, sinks, o, do, lse)


def _dproj_b(name, dq_swa, dq_mem, dk, dv, dk_edge, dv_edge, *, tq):
    S = dq_swa.shape[0]
    tq = min(tq, S)
    nq = S // tq

    def body(dqs_ref, dqm_ref, dk_ref, dv_ref, dke_ref, dve_ref, o_ref):
        i = pl.program_id(0)
        o_ref[:, 0:1024] = dqs_ref[...]
        o_ref[:, 1024:1536] = dqm_ref[...].astype(o_ref.dtype)
        o_ref[:, 1536:1792] = dk_ref[...].astype(o_ref.dtype)
        o_ref[:, 1792:2048] = dv_ref[...].astype(o_ref.dtype)

        @pl.when(i < nq - 1)
        def _():
            last = slice(tq - WINDOW, tq)
            o_ref[last, 1536:1792] = (dk_ref[last, :] + dke_ref[...]).astype(o_ref.dtype)
            o_ref[last, 1792:2048] = (dv_ref[last, :] + dve_ref[...]).astype(o_ref.dtype)

    edge = pl.BlockSpec((WINDOW, SWA_KV_HEADS * HEAD_PAD), lambda i: (jnp.minimum(i + 1, nq - 1), 0))
    return pl.pallas_call(
        body, name=name, grid=(nq,),
        in_specs=[pl.BlockSpec((tq, 1024), lambda i: (i, 0)), pl.BlockSpec((tq, 512), lambda i: (i, 0)),
                  pl.BlockSpec((tq, 256), lambda i: (i, 0)), pl.BlockSpec((tq, 256), lambda i: (i, 0)), edge, edge],
        out_specs=pl.BlockSpec((tq, 2048), lambda i: (i, 0)),
        out_shape=jax.ShapeDtypeStruct((S, 2048), BF16),
        compiler_params=_cparams("parallel"),
    )(dq_swa, dq_mem, dk, dv, dk_edge, dv_edge)


def _gather_forwarded(name, block):
    def body(x_ref, out_ref, send_sems, recv_sems, local_sem):
        x, y, c = lax.axis_index("x"), lax.axis_index("y"), lax.axis_index("c")
        me, sibling = (x, y, c), (x, y, 1 - c)
        chips = [(1 - x, y), (x, 1 - y), (1 - x, 1 - y)]

        def slot(px, py, pc):
            return out_ref.at[4 * px + 2 * py + pc]

        def copy(k, blk, to, src=None):
            return pltpu.make_async_remote_copy(
                src_ref=slot(*blk) if src is None else src, dst_ref=slot(*blk),
                send_sem=send_sems.at[k], recv_sem=recv_sems.at[k],
                device_id=to, device_id_type=pl.DeviceIdType.MESH)

        mine = pltpu.make_async_copy(x_ref, slot(*me), local_sem)
        mine.start()
        first = [copy(0, me, sibling, src=x_ref)]
        first += [copy(1 + j, me, (*chip, c), src=x_ref) for j, chip in enumerate(chips)]
        for cp in first:
            cp.start()
        passed = [copy(4 + j, (*chip, c), sibling) for j, chip in enumerate(chips)]
        for j, chip in enumerate(chips):
            copy(1 + j, (*chip, c), me).wait_recv()
            passed[j].start()
        copy(0, sibling, me).wait_recv()
        for j, chip in enumerate(chips):
            copy(4 + j, (*chip, 1 - c), me).wait_recv()
        for cp in first + passed:
            cp.wait_send()
        mine.wait()

    return pl.pallas_call(
        body, name=name,
        in_specs=[pl.BlockSpec(memory_space=pl.ANY)],
        out_specs=pl.BlockSpec(memory_space=pl.ANY),
        out_shape=jax.ShapeDtypeStruct((N_DEV,) + tuple(block.shape), block.dtype),
        scratch_shapes=[pltpu.SemaphoreType.DMA((N_DEV - 1,)), pltpu.SemaphoreType.DMA((N_DEV - 1,)),
                        pltpu.SemaphoreType.DMA(())],
    )(block)


def _sibling_swap(name, block):
    def body(x_ref, out_ref, send_sem, recv_sem):
        x, y, c = lax.axis_index("x"), lax.axis_index("y"), lax.axis_index("c")
        cp = pltpu.make_async_remote_copy(src_ref=x_ref, dst_ref=out_ref, send_sem=send_sem, recv_sem=recv_sem,
                                          device_id=(x, y, 1 - c), device_id_type=pl.DeviceIdType.MESH)
        cp.start()
        cp.wait()

    return pl.pallas_call(
        body, name=name,
        in_specs=[pl.BlockSpec(memory_space=pl.ANY)],
        out_specs=pl.BlockSpec(memory_space=pl.ANY),
        out_shape=jax.ShapeDtypeStruct(block.shape, block.dtype),
        scratch_shapes=[pltpu.SemaphoreType.DMA(()), pltpu.SemaphoreType.DMA(())],
    )(block)


def _chip_exchange(name, send):
    def body(send_ref, recv_ref, send_sems, recv_sems, local_sem):
        x, y, c = lax.axis_index("x"), lax.axis_index("y"), lax.axis_index("c")
        me = 2 * x + y
        own = pltpu.make_async_copy(send_ref.at[me], recv_ref.at[me], local_sem)
        own.start()
        copies = []
        for k in range(1, 4):
            px = 1 - x if (k >> 1) & 1 else x
            py = 1 - y if k & 1 else y
            peer = 2 * px + py
            out = pltpu.make_async_remote_copy(
                src_ref=send_ref.at[peer], dst_ref=recv_ref.at[me],
                send_sem=send_sems.at[k - 1], recv_sem=recv_sems.at[k - 1],
                device_id=(px, py, c), device_id_type=pl.DeviceIdType.MESH)
            out.start()
            back = pltpu.make_async_remote_copy(
                src_ref=send_ref.at[me], dst_ref=recv_ref.at[peer],
                send_sem=send_sems.at[k - 1], recv_sem=recv_sems.at[k - 1],
                device_id=(px, py, c), device_id_type=pl.DeviceIdType.MESH)
            copies.append((out, back))
        for out, back in copies:
            out.wait_send()
            back.wait_recv()
        own.wait()

    return pl.pallas_call(
        body, name=name,
        in_specs=[pl.BlockSpec(memory_space=pl.ANY)],
        out_specs=pl.BlockSpec(memory_space=pl.ANY),
        out_shape=jax.ShapeDtypeStruct(send.shape, send.dtype),
        scratch_shapes=[pltpu.SemaphoreType.DMA((3,)), pltpu.SemaphoreType.DMA((3,)), pltpu.SemaphoreType.DMA(())],
    )(send)


def _reduce_scatter(parts):
    lanes = 128
    c = lax.axis_index("c")

    def core_half(core):
        return jnp.concatenate(
            [lax.dynamic_index_in_dim(p.reshape(4, 2, p.shape[1], lanes), core, axis=1, keepdims=False)
             for p in parts], axis=1)

    mine = core_half(c)
    rows = mine.shape[1]
    mine = mine.reshape(4 * rows, lanes)
    theirs = core_half(1 - c).reshape(4 * rows, lanes)
    from_sibling = _sibling_swap("grads_to_sibling", theirs)
    tm = max(t for t in range(16, 4 * ADAM_TM + 1, 16) if (4 * rows) % t == 0)
    chip_sum = _rowwise("grads_chip_sum", lambda a, b: (a.astype(F32) + b.astype(F32),),
                        [_rows(mine), _rows(from_sibling)], [((4 * rows, lanes), BF16, "rows")],
                        rows=4 * rows, tm=tm)[0]
    return _chip_exchange("scatter_grads", chip_sum.reshape(4, rows, lanes))


def _adam(name, recv, w, m, v, *, tm=None):
    R = w.shape[0]
    n_parts = recv.shape[0]
    tm = max(t for t in range(8, min(tm or ADAM_TM, R) + 1, 8) if R % t == 0)
    c1 = 1.0 / (1.0 - ADAM_B1 ** ADAM_STEP)
    c2 = 1.0 / (1.0 - ADAM_B2 ** ADAM_STEP)

    def body(r_ref, w_ref, m_ref, v_ref, g_ref, d_ref, nm_ref, nv_ref):
        g = r_ref[0].astype(F32)
        for j in range(1, n_parts):
            g = g + r_ref[j].astype(F32)
        wv = w_ref[...]
        nm = ADAM_B1 * m_ref[...] + (1.0 - ADAM_B1) * g
        nv = ADAM_B2 * v_ref[...] + (1.0 - ADAM_B2) * (g * g)
        m_hat = nm * c1
        v_hat = nv * c2
        g_ref[...] = g
        d_ref[...] = -ADAM_LR * (m_hat / (jnp.sqrt(v_hat) + ADAM_EPS) + ADAM_WD * wv)
        nm_ref[...] = nm
        nv_ref[...] = nv

    row = pl.BlockSpec((tm, 128), lambda i: (i, 0))
    return pl.pallas_call(
        body, name=name, grid=(R // tm,),
        in_specs=[pl.BlockSpec((n_parts, tm, 128), lambda i: (0, i, 0)), row, row, row],
        out_specs=[row, row, row, row],
        out_shape=[jax.ShapeDtypeStruct((R, 128), F32)] * 4,
        compiler_params=_cparams("parallel"),
    )(recv, w, m, v)


def _pack_rows(arrs):
    return jnp.concatenate([a.reshape(-1, 128) for a in arrs], axis=0)


def _unstack(g, shape, axis):
    t = jnp.moveaxis(g, 0, axis)
    return t.reshape(shape)


def _restack(full, axis):
    s = full.shape
    t = full.reshape(s[:axis] + (N_DEV, s[axis] // N_DEV) + s[axis + 1:])
    return jnp.moveaxis(t, axis, 0)


def _pad_heads(w, heads, hd, axis):
    s = w.shape
    t = w.reshape(s[:axis] + (heads, hd) + s[axis + 1:])
    pad = [(0, 0)] * t.ndim
    pad[axis + 1] = (0, HEAD_PAD - hd)
    t = jnp.pad(t, pad)
    return t.reshape(s[:axis] + (heads * HEAD_PAD,) + s[axis + 1:])


def _unpad_heads(w, heads, hd, axis):
    s = w.shape
    t = w.reshape(s[:axis] + (heads, HEAD_PAD) + s[axis + 1:])
    t = lax.slice_in_dim(t, 0, hd, axis=axis + 1)
    return t.reshape(s[:axis] + (heads * hd,) + s[axis + 1:])


def _layer_weights(full, l):
    w_in = full["w_in"][l]
    cq, kva, qs, ks, vs, qm, gates = (w_in[:, 0:256], w_in[:, 256:416], w_in[:, 416:928], w_in[:, 928:1056],
                                       w_in[:, 1056:1184], w_in[:, 1184:1696], w_in[:, 1696:4768])
    wa = jnp.concatenate([gates, cq, jnp.pad(kva, ((0, 0), (0, 96)))], axis=1)
    wb = jnp.concatenate([_pad_heads(qs, SWA_HEADS, SWA_HD, 1), qm, _pad_heads(ks, SWA_KV_HEADS, SWA_HD, 1),
                          _pad_heads(vs, SWA_KV_HEADS, SWA_HD, 1)], axis=1)
    wuq = _pad_heads(full["w_uq"][l], MLA_HEADS, MLA_NOPE + MLA_ROPE, 1)
    ukv = full["w_ukv"][l].reshape(MLA_KV_LORA, MLA_HEADS, MLA_NOPE + MLA_V)
    wuk = _pad_heads(ukv[:, :, :MLA_NOPE].reshape(MLA_KV_LORA, -1), MLA_HEADS, MLA_NOPE, 1)
    wuv = _pad_heads(ukv[:, :, MLA_NOPE:].reshape(MLA_KV_LORA, -1), MLA_HEADS, MLA_V, 1)
    wo_mla = _pad_heads(full["w_o_mla"][l], MLA_HEADS, MLA_V, 0)
    wo_swa = _pad_heads(full["w_o_swa"][l], SWA_HEADS, SWA_HD, 0)
    wo_mem = full["w_o_mem"][l]
    w = dict(wag=wa[:, :3072], wat=wa[:, 3072:], wb=wb, wuq=wuq, wuk=wuk, wuv=wuv, wo_mla=wo_mla, wo_swa=wo_swa,
             wo_mem=wo_mem, wmem=full["w_mem_kv"][l], wout=full["w_out"][l], wup=full["w_up"][l],
             wdown=full["w_down"][l])
    w.update({k + "_t": v.T for k, v in w.items()})
    return w


def _layer_weight_grads(g):
    dwa_g, dwa_t, dwb = g["wag"], g["wat"], g["wb"]
    d_in = jnp.concatenate([
        dwa_t[:, 0:256], dwa_t[:, 256:416],
        _unpad_heads(dwb[:, 0:1024], SWA_HEADS, SWA_HD, 1),
        _unpad_heads(dwb[:, 1536:1792], SWA_KV_HEADS, SWA_HD, 1),
        _unpad_heads(dwb[:, 1792:2048], SWA_KV_HEADS, SWA_HD, 1),
        dwb[:, 1024:1536], dwa_g], axis=1)
    duk = _unpad_heads(g["wuk"], MLA_HEADS, MLA_NOPE, 1).reshape(MLA_KV_LORA, MLA_HEADS, MLA_NOPE)
    duv = _unpad_heads(g["wuv"], MLA_HEADS, MLA_V, 1).reshape(MLA_KV_LORA, MLA_HEADS, MLA_V)
    return dict(
        w_in=d_in,
        w_uq=_unpad_heads(g["wuq"], MLA_HEADS, MLA_NOPE + MLA_ROPE, 1),
        w_ukv=jnp.concatenate([duk, duv], axis=2).reshape(MLA_KV_LORA, -1),
        w_mem_kv=g["wmem"],
        w_o_mla=_unpad_heads(g["wo_mla"], MLA_HEADS, MLA_V, 0),
        w_o_swa=_unpad_heads(g["wo_swa"], SWA_HEADS, SWA_HD, 0),
        w_o_mem=g["wo_mem"], w_out=g["wout"], w_up=g["wup"], w_down=g["wdown"])


def _rope_tables(S):
    pos = jnp.arange(S, dtype=F32)
    inv = 1.0 / (ROPE_THETA ** (jnp.arange(0, MLA_ROPE, 2, dtype=F32) / MLA_ROPE))
    ang = pos[:, None] * inv[None, :]
    cos, sin = jnp.cos(ang), jnp.sin(ang)
    z16 = jnp.zeros((S, 16), F32)
    z32 = jnp.zeros((S, 32), F32)
    c = jnp.concatenate([jnp.ones((S, 64), F32), cos, cos, z32], axis=1)
    ck = jnp.concatenate([jnp.zeros((S, 64), F32), cos, cos, z32], axis=1)
    s1 = jnp.concatenate([jnp.zeros((S, 80), F32), sin, z32], axis=1)
    s2 = jnp.concatenate([jnp.zeros((S, 64), F32), -sin, z16, z32], axis=1)
    return c, ck, s1, s2


def _t5_bucket(dist):
    n = jnp.maximum(dist, 0)
    max_exact = REL_BUCKETS // 2
    nf = jnp.maximum(n, 1).astype(F32)
    large = max_exact + (jnp.log(nf / max_exact) / math.log(REL_MAX_DIST / max_exact)
                         * (REL_BUCKETS - max_exact)).astype(jnp.int32)
    large = jnp.minimum(large, REL_BUCKETS - 1)
    return jnp.where(n < max_exact, n, large)


def _bias_onehot():
    qi = jnp.arange(WINDOW)[:, None]
    kj = jnp.arange(2 * WINDOW)[None, :]
    dist = qi + WINDOW - kj
    valid = (dist >= 0) & (dist < WINDOW)
    bucket = _t5_bucket(dist)
    onehot = (bucket[None] == jnp.arange(REL_BUCKETS)[:, None, None]) & valid[None]
    return (onehot.reshape(REL_BUCKETS, -1).astype(F32),
            jnp.where(valid, 0.0, NEG).astype(F32).reshape(1, -1))


def _rstd(x):
    return lax.rsqrt(jnp.mean(x * x, axis=-1, keepdims=True) + EPS)


def _norm_bwd(dh, x, g):
    r = _rstd(x)
    xh = x * r
    w = dh * g
    dx = r * (w - xh * jnp.mean(w * xh, axis=-1, keepdims=True))
    return dx, jnp.sum(dh * xh, axis=0, keepdims=True)


def _tile_lanes(t, n):
    return jnp.tile(t, (1, n // t.shape[1])) if n != t.shape[1] else t


def _rope_fwd(a, c, s1, s2):
    n = a.shape[1]
    return (a * _tile_lanes(c, n) + pltpu.roll(a, 16, 1) * _tile_lanes(s1, n)
            + pltpu.roll(a, n - 16, 1) * _tile_lanes(s2, n))


def _rope_bwd(d, c, s1, s2):
    n = d.shape[1]
    return (d * _tile_lanes(c, n) + pltpu.roll(d * _tile_lanes(s1, n), n - 16, 1)
            + pltpu.roll(d * _tile_lanes(s2, n), 16, 1))


def _sigmoid(x):
    return 1.0 / (1.0 + jnp.exp(-x))


def _rmsnorm(name, x, g, dtype):
    def fn(xv, gv):
        return ((xv * _rstd(xv)) * gv,)
    return _rowwise(name, fn, [_rows(x), _full(g)], [(x.shape, dtype, "rows")], rows=x.shape[0])[0]


def _residual_norm_bwd(name, dres, dh, x, g):
    def fn(dr, dhv, xv, gv):
        dx, dg = _norm_bwd(dhv, xv, gv)
        return dr + dx, dg
    return _rowwise(name, fn, [_rows(dres), _rows(dh), _rows(x), _full(g)],
                    [(x.shape, F32, "rows"), (g.shape, F32, "acc")], rows=x.shape[0])


def _norm_bwd_epilogue(dh, dres, x, g):
    dx, dg = _norm_bwd(dh, x, g)
    return dres + dx, dg


def _add_and_norm(acc, r, g):
    xs = acc + r
    return xs, xs * _rstd(xs) * g


def _layer_fwd(l, x, h, mem, w, p, next_norm, tabs, swa_bias, S):
    c, ck, s1, s2 = tabs
    n = f"l{l}_"
    gates = _mm(n + "proj_gates", h, w["wag"], [BF16], tm=MM_TM_BF16)
    proj_a = _mm(n + "proj_tail", h, w["wat"], [F32], tm=MM_TM_BF16)
    proj_b = _mm(n + "proj_b", h, w["wb"], [BF16], tm=MM_TM_BF16)

    def prep(cq, kva, qn, kvn, ckv, s1v, s2v):
        cqn = cq * _rstd(cq) * qn
        ckv_ = kva[:, :128]
        ckvn = ckv_ * _rstd(ckv_) * kvn
        pe = pltpu.roll(kva[:, 128:], 64, 1)
        return cqn, ckvn, _rope_fwd(pe, ckv, s1v, s2v)

    cqn, ckvn, kpe = _rowwise(
        n + "mla_prep", prep,
        [_rows(proj_a, 256, 0), _rows(proj_a, 256, 1), _full(p["mla_q_norm"]), _full(p["mla_kv_norm"]),
         _rows(ck), _rows(s1), _rows(s2)],
        [((S, 256), BF16, "rows"), ((S, 128), BF16, "rows"), ((S, 128), F32, "rows")], rows=S)

    q_mla = _mm(n + "q_mla", cqn, w["wuq"], [BF16],
                epi=lambda acc, cv, s1v, s2v: (_rope_fwd(acc, cv, s1v, s2v) * (MLA_SCALE * LOG2E),),
                extras=[(c, "m"), (s1, "m"), (s2, "m")], tm=MM_TM_BF16)
    k_mla = _mm(n + "k_mla", ckvn, w["wuk"], [BF16],
                epi=lambda acc, kp: (acc + _tile_lanes(kp, acc.shape[1]),), extras=[(kpe, "m")], tm=MM_TM_BF16)
    den = ((jnp.arange(MLA_HEADS * HEAD_PAD) % HEAD_PAD) // 2 == DEN_LANE // 2).astype(F32)[None]
    v_mla = _mm(n + "v_mla", ckvn, w["wuv"], [BF16], epi=lambda acc, dv: (acc + dv,), extras=[(den, "n")],
                tm=MM_TM_BF16)
    o_mla, lse_mla = _causal_fwd(n + "mla_fwd", q_mla, k_mla, v_mla, heads=MLA_HEADS, tile=MLA_TILE,
                                 chunk=MLA_CHUNK_FWD)
    o_swa, lse_swa = _swa_fwd(n + "swa_fwd", proj_b, swa_bias, p["sinks"], tq=SWA_TQ)
    mn = _rmsnorm(n + "mem_norm", mem, p["mem_norm"], BF16)
    kvm = _mm(n + "kv_mem", mn, w["wmem"], [BF16])
    o_mem, lse_mem = _mem_fwd(n + "mem_fwd", proj_b, kvm, tq=MEM_TQ, chunk=MEM_CHUNK)
    t0 = _mm(n + "t_mla", o_mla, w["wo_mla"], [BF16], tm=MM_TM_BF16)
    t1 = _mm(n + "t_swa", o_swa, w["wo_swa"], [BF16], tm=MM_TM_BF16)
    t2 = _mm(n + "t_mem", o_mem, w["wo_mem"], [BF16], tm=MM_TM_BF16)

    def merge(g0, g1, g2, bg, a0, a1, a2):
        y = (_sigmoid(g0 + bg[:, 0:1024]) * a0 + _sigmoid(g1 + bg[:, 1024:2048]) * a1
             + _sigmoid(g2 + bg[:, 2048:3072]) * a2)
        return (y,)

    y = _rowwise(n + "merge", merge,
                 [_rows(gates, 1024, 0), _rows(gates, 1024, 1), _rows(gates, 1024, 2), _full(p["b_gate"]),
                  _rows(t0), _rows(t1), _rows(t2)], [((S, D_MODEL), BF16, "rows")], rows=S)[0]
    x1, h2 = _mm(n + "out_proj", y, w["wout"], [F32, BF16], epi=_add_and_norm,
                 extras=[(x, "mn"), (p["mlp_norm"], "n")], tn=D_MODEL)
    act = _mm(n + "mlp_up", h2, w["wup"], [BF16], epi=lambda acc: (jnp.square(jnp.maximum(acc, 0.0)),),
              tm=MM_TM_BF16)
    if next_norm is None:
        x2 = _mm(n + "mlp_down", act, w["wdown"], [F32], epi=lambda acc, r: (acc + r,), extras=[(x1, "mn")],
                 tk=MM_TK_DEEP)
        h_next = None
    else:
        x2, h_next = _mm(n + "mlp_down", act, w["wdown"], [F32, BF16], epi=_add_and_norm,
                         extras=[(x1, "mn"), (next_norm, "n")], tn=D_MODEL, tk=MM_TK_DEEP)
    saved = dict(x=x, h=h, gates=gates, proj_a=proj_a, proj_b=proj_b, cqn=cqn, ckvn=ckvn, q_mla=q_mla, k_mla=k_mla, v_mla=v_mla,
                 o_mla=o_mla, lse_mla=lse_mla, o_swa=o_swa, lse_swa=lse_swa, mn=mn, kvm=kvm, o_mem=o_mem,
                 lse_mem=lse_mem, t0=t0, t1=t1, t2=t2, y=y, x1=x1, h2=h2, act=act)
    return x2, h_next, saved


def _layer_bwd(l, dx2, mem, w, p, tabs, swa_bias, sv, S):
    c, ck, s1, s2 = tabs
    n = f"l{l}_b_"
    gw = {}
    gs = {}
    du = _mm(n + "d_act", dx2, w["wdown_t"], [BF16],
             epi=lambda acc, av: (acc * (2.0 * jnp.sqrt(av.astype(F32))),), extras=[(sv["act"], "mn")],
             tm=MM_TM_BF16)
    gw["wdown"] = _mm_tn(n + "g_wdown", sv["act"], dx2)
    gw["wup"] = _mm_tn(n + "g_wup", sv["h2"], du)
    dx1, gs["mlp_norm"] = _mm(n + "d_h2", du, w["wup_t"], [F32], epi=_norm_bwd_epilogue,
                              extras=[(dx2, "mn"), (sv["x1"], "mn"), (p["mlp_norm"], "n")], tn=D_MODEL, col_sums=1)
    gw["wout"] = _mm_tn(n + "g_wout", sv["y"], dx1)
    dy = _mm(n + "d_y", dx1, w["wout_t"], [F32], tm=MM_TM_BF16)

    def merge_bwd(dyv, g0, g1, g2, bg, a0, a1, a2):
        outs, dgs = [], []
        for b, (gv, av) in enumerate(((g0, a0), (g1, a1), (g2, a2))):
            sg = _sigmoid(gv + bg[:, b * 1024:(b + 1) * 1024])
            outs.append(dyv * sg)
            dgs.append(dyv * av * sg * (1.0 - sg))
        dg = jnp.concatenate(dgs, axis=1)
        return outs[0], outs[1], outs[2], dg, jnp.sum(dg, axis=0, keepdims=True)

    pa = sv["proj_a"]
    gt = sv["gates"]
    dt0, dt1, dt2, dgates, gs["b_gate"] = _rowwise(
        n + "merge", merge_bwd,
        [_rows(dy), _rows(gt, 1024, 0), _rows(gt, 1024, 1), _rows(gt, 1024, 2), _full(p["b_gate"]),
         _rows(sv["t0"]), _rows(sv["t1"]), _rows(sv["t2"])],
        [((S, D_MODEL), BF16, "rows")] * 3 + [((S, 3 * D_MODEL), BF16, "rows"), ((1, 3 * D_MODEL), F32, "acc")],
        rows=S, tm=ROW_TM // 2)
    gw["wo_mla"] = _mm_tn(n + "g_wo_mla", sv["o_mla"], dt0)
    gw["wo_swa"] = _mm_tn(n + "g_wo_swa", sv["o_swa"], dt1)
    gw["wo_mem"] = _mm_tn(n + "g_wo_mem", sv["o_mem"], dt2)
    do_mla = _mm(n + "d_o_mla", dt0, w["wo_mla_t"], [BF16], epi=lambda acc, ov: (_with_neg_delta(acc, ov),),
                 extras=[(sv["o_mla"], "mn")], tn=MLA_HEADS * HEAD_PAD, tm=MM_TM_BF16)
    do_swa = _mm(n + "d_o_swa", dt1, w["wo_swa_t"], [BF16], tm=MM_TM_BF16)
    do_mem = _mm(n + "d_o_mem", dt2, w["wo_mem_t"], [BF16], tm=MM_TM_BF16)
    pb = sv["proj_b"]
    dq_mla, dk_mla, dv_mla = _causal_bwd(
        n + "mla_bwd", sv["q_mla"], sv["k_mla"], sv["v_mla"], do_mla, sv["lse_mla"], heads=MLA_HEADS,
        tile=MLA_TILE_BWD, chunk=MLA_CHUNK)
    dq_swa, dk_swa, dv_swa, dk_edge, dv_edge, dbias, dsink = _swa_bwd(
        n + "swa_bwd", pb, swa_bias, p["sinks"], sv["o_swa"], do_swa, sv["lse_swa"], tq=SWA_TQ)
    dq_mem, dkvm = _mem_bwd(n + "mem_bwd", pb, sv["kvm"], sv["o_mem"], do_mem, sv["lse_mem"], tq=MEM_TQ,
                            chunk=MEM_CHUNK)
    gs["dbias"] = dbias
    gs["sinks"] = dsink[:, 0, 0]
    gw["wmem"] = _mm_tn(n + "g_wmem", sv["mn"], dkvm)
    dmn = _mm(n + "d_mn", dkvm, w["wmem_t"], [F32])
    _, gs["mem_norm"] = _residual_norm_bwd(n + "mem_norm", dmn, dmn, mem, p["mem_norm"])
    dq_pre = _rowwise(n + "q_unrope", lambda d, cv, s1v, s2v: (_rope_bwd(d * MLA_SCALE, cv, s1v, s2v),),
                      [_rows(dq_mla), _rows(c), _rows(s1), _rows(s2)], [((S, 1024), BF16, "rows")], rows=S)[0]
    gw["wuq"] = _mm_tn(n + "g_wuq", sv["cqn"], dq_pre)
    gw["wuk"] = _mm_tn(n + "g_wuk", sv["ckvn"], dk_mla)
    gw["wuv"] = _mm_tn(n + "g_wuv", sv["ckvn"], dv_mla)
    dcqn = _mm(n + "d_cqn", dq_pre, w["wuq_t"], [F32], tm=MM_TM_BF16)
    dckvn = _mm(n + "d_ckvn_k", dk_mla, w["wuk_t"], [F32], tm=MM_TM_BF16)
    dckvn = _mm(n + "d_ckvn_v", dv_mla, w["wuv_t"], [F32], epi=lambda acc, r: (acc + r,), extras=[(dckvn, "mn")],
                tm=MM_TM_BF16)

    def mla_norm_bwd(dcq_n, dckv_n, dk, cq, kva, qn, kvn, ckv, s1v, s2v):
        dcq, dqn = _norm_bwd(dcq_n, cq, qn)
        dckv, dkvn = _norm_bwd(dckv_n, kva[:, :128], kvn)
        dkpe = dk[:, 0:128]
        for hh in range(1, MLA_HEADS):
            dkpe = dkpe + dk[:, hh * 128:(hh + 1) * 128]
        dpe = pltpu.roll(_rope_bwd(dkpe, ckv, s1v, s2v), 64, 1)
        return jnp.concatenate([dcq, dckv, dpe], axis=1), dqn, dkvn

    dtail, gs["mla_q_norm"], gs["mla_kv_norm"] = _rowwise(
        n + "mla_norm", mla_norm_bwd,
        [_rows(dcqn), _rows(dckvn), _rows(dk_mla), _rows(pa, 256, 0), _rows(pa, 256, 1),
         _full(p["mla_q_norm"]), _full(p["mla_kv_norm"]), _rows(ck), _rows(s1), _rows(s2)],
        [((S, 512), BF16, "rows"), ((1, 256), F32, "acc"), ((1, 128), F32, "acc")], rows=S)

    dproj_b = _dproj_b(n + "dproj_b", dq_swa, dq_mem, dk_swa, dv_swa, dk_edge, dv_edge, tq=SWA_TQ)
    h = sv["h"]
    gw["wag"] = _mm_tn(n + "g_wa_gates", h, dgates)
    gw["wat"] = _mm_tn(n + "g_wa_tail", h, dtail)
    gw["wb"] = _mm_tn(n + "g_wb", h, dproj_b)
    dh = _mm(n + "d_h_gates", dgates, w["wag_t"], [F32], tm=MM_TM_BF16)
    dh = _mm(n + "d_h_tail", dtail, w["wat_t"], [F32], epi=lambda acc, r: (acc + r,), extras=[(dh, "mn")],
             tm=MM_TM_BF16)
    dx, gs["attn_norm"] = _mm(n + "d_h_b", dproj_b, w["wb_t"], [F32],
                              epi=lambda acc, prev, dr, xv, gv: _norm_bwd_epilogue(acc + prev, dr, xv, gv),
                              extras=[(dh, "mn"), (dx1, "mn"), (sv["x"], "mn"), (p["attn_norm"], "n")],
                              tn=D_MODEL, tm=MM_TM // 2, col_sums=1)
    return dx, gw, gs


def _local_step(x, mem, loss_target, full, small):
    S = x.shape[0]
    tabs = _rope_tables(S)
    onehot, band = _bias_onehot()
    hi = lax.Precision.HIGHEST
    swa_bias = _mm("swa_bias", small["rel_bias"].T, onehot, [F32], epi=lambda acc, mk: (acc + mk,),
                   extras=[(band, "n")], cast=None, precision=hi, tn=8192).reshape(SWA_HEADS, WINDOW, 2 * WINDOW)
    ws, ps = [], []
    for l in range(DEPTH):
        ws.append(_layer_weights(full, l))
        ps.append(dict(
            attn_norm=small["attn_norm"][l][None], mem_norm=small["mem_norm"][l][None],
            b_gate=small["b_gate"][l][None], mla_q_norm=small["mla_q_norm"][l][None],
            mla_kv_norm=small["mla_kv_norm"][l][None], mlp_norm=small["mlp_norm"][l][None],
            sinks=jnp.broadcast_to(small["attn_sinks"][l][:, None, None], (SWA_HEADS, 8, 128))))
    saved = []
    xc = x
    hc = _rmsnorm("l0_attn_norm", x, ps[0]["attn_norm"], BF16)
    for l in range(DEPTH):
        next_norm = ps[l + 1]["attn_norm"] if l + 1 < DEPTH else None
        xc, hc, sv = _layer_fwd(l, xc, hc, mem, ws[l], ps[l], next_norm, tabs, swa_bias, S)
        saved.append(sv)

    fn_g = small["final_norm"][None]

    def loss_fn(xv, gv, tv):
        r = _rstd(xv)
        xh = xv * r
        err = xh * gv - tv
        dyv = err * (1.0 / D_MODEL)
        wv = dyv * gv
        dx = r * (wv - xh * jnp.mean(wv * xh, axis=-1, keepdims=True))
        part = 0.5 * jnp.sum(err * err) * (1.0 / D_MODEL)
        return dx, jnp.sum(dyv * xh, axis=0, keepdims=True), jnp.zeros((8, 128), F32) + part

    dx, g_final, loss_acc = _rowwise(
        "loss", loss_fn, [_rows(xc), _full(fn_g), _rows(loss_target)],
        [((S, D_MODEL), F32, "rows"), ((1, D_MODEL), F32, "acc"), ((8, 128), F32, "acc")], rows=S)

    gws, gss = [None] * DEPTH, [None] * DEPTH
    for l in reversed(range(DEPTH)):
        dx, gw, gs = _layer_bwd(l, dx, mem, ws[l], ps[l], tabs, swa_bias, saved[l], S)
        gws[l] = _layer_weight_grads(gw)
        gss[l] = gs

    dbias = (gss[0]["dbias"] + gss[1]["dbias"]).reshape(SWA_HEADS, -1)
    g_rel = _mm("g_rel_bias", dbias, onehot.T, [F32], cast=None, precision=hi, tk=8192).T
    wgrads = {k: jnp.stack([gws[l][k] for l in range(DEPTH)]) for k in gws[0]}
    sgrads = dict(
        rel_bias=g_rel,
        final_norm=g_final[0],
        attn_sinks=jnp.stack([gss[l]["sinks"] for l in range(DEPTH)]),
        **{k: jnp.concatenate([gss[l][k] for l in range(DEPTH)], axis=0)
           for k in ("attn_norm", "mem_norm", "b_gate", "mla_q_norm", "mla_kv_norm", "mlp_norm")})
    return loss_acc[0, 0], dx, wgrads, sgrads


def _pack_small(vals, loss):
    rows = []
    for name, shape in SMALL:
        flat = vals[name].astype(F32).reshape(-1)
        pad = (-flat.shape[0]) % 1024
        rows.append(jnp.pad(flat, (0, pad)).reshape(-1, 128))
    rows.append(jnp.zeros((8, 128), F32) + loss)
    return jnp.concatenate(rows, axis=0)


def _unpack_small(packed):
    out, r = {}, 0
    for name, shape in SMALL:
        size = math.prod(shape)
        nrows = 8 * -(-size // 1024)
        out[name] = packed[r:r + nrows].reshape(-1)[:size].reshape(shape)
        r += nrows
    return out, packed[r, 0]


def kernel(x, mem, rel_bias, attn_norm, mem_norm, w_in, b_gate, mla_q_norm, w_uq, mla_kv_norm, w_ukv, attn_sinks, w_mem_kv, w_o_mla, w_o_swa, w_o_mem, w_out, mlp_norm, w_up, w_down, final_norm, loss_target, m_rel_bias, m_attn_norm, m_mem_norm, m_w_in, m_b_gate, m_mla_q_norm, m_w_uq, m_mla_kv_norm, m_w_ukv, m_attn_sinks, m_w_mem_kv, m_w_o_mla, m_w_o_swa, m_w_o_mem, m_w_out, m_mlp_norm, m_w_up, m_w_down, m_final_norm, v_rel_bias, v_attn_norm, v_mem_norm, v_w_in, v_b_gate, v_mla_q_norm, v_w_uq, v_mla_kv_norm, v_w_ukv, v_attn_sinks, v_w_mem_kv, v_w_o_mla, v_w_o_swa, v_w_o_mem, v_w_out, v_mlp_norm, v_w_up, v_w_down, v_final_norm):
    wv = dict(rel_bias=rel_bias, attn_norm=attn_norm, mem_norm=mem_norm, w_in=w_in, b_gate=b_gate,
              mla_q_norm=mla_q_norm, w_uq=w_uq, mla_kv_norm=mla_kv_norm, w_ukv=w_ukv, attn_sinks=attn_sinks,
              w_mem_kv=w_mem_kv, w_o_mla=w_o_mla, w_o_swa=w_o_swa, w_o_mem=w_o_mem, w_out=w_out,
              mlp_norm=mlp_norm, w_up=w_up, w_down=w_down, final_norm=final_norm)
    mv = dict(rel_bias=m_rel_bias, attn_norm=m_attn_norm, mem_norm=m_mem_norm, w_in=m_w_in, b_gate=m_b_gate,
              mla_q_norm=m_mla_q_norm, w_uq=m_w_uq, mla_kv_norm=m_mla_kv_norm, w_ukv=m_w_ukv,
              attn_sinks=m_attn_sinks, w_mem_kv=m_w_mem_kv, w_o_mla=m_w_o_mla, w_o_swa=m_w_o_swa,
              w_o_mem=m_w_o_mem, w_out=m_w_out, mlp_norm=m_mlp_norm, w_up=m_w_up, w_down=m_w_down,
              final_norm=m_final_norm)
    vv = dict(rel_bias=v_rel_bias, attn_norm=v_attn_norm, mem_norm=v_mem_norm, w_in=v_w_in, b_gate=v_b_gate,
              mla_q_norm=v_mla_q_norm, w_uq=v_w_uq, mla_kv_norm=v_mla_kv_norm, w_ukv=v_w_ukv,
              attn_sinks=v_attn_sinks, w_mem_kv=v_w_mem_kv, w_o_mla=v_w_o_mla, w_o_swa=v_w_o_swa,
              w_o_mem=v_w_o_mem, w_out=v_w_out, mlp_norm=v_mlp_norm, w_up=v_w_up, w_down=v_w_down,
              final_norm=v_final_norm)

    shard_rows = [math.prod(_shard_shape(shape, axis)) // 128 for _, shape, axis in WSPECS]
    gathered = _gather_forwarded("gather_weights", _pack_rows([wv[name].astype(BF16) for name, _, _ in WSPECS]))
    full, r = {}, 0
    for (name, shape, axis), nr in zip(WSPECS, shard_rows):
        full[name] = _unstack(gathered[:, r:r + nr].reshape((N_DEV,) + _shard_shape(shape, axis)), shape, axis)
        r += nr

    loss_part, grad_x, wgrads, sgrads = _local_step(x[0], mem[0], loss_target[0], full,
                                                    {name: wv[name] for name, _ in SMALL})

    recv = _reduce_scatter([_restack(wgrads[name], axis).astype(BF16).reshape(N_DEV, -1, 128)
                            for name, _, axis in WSPECS])
    outs = _adam("adam_sharded", recv, *[_pack_rows([d[name] for name, _, _ in WSPECS]) for d in (wv, mv, vv)])
    res = {}
    r = 0
    for (name, shape, axis), nr in zip(WSPECS, shard_rows):
        res[name] = [o[r:r + nr].reshape(_shard_shape(shape, axis)) for o in outs]
        r += nr

    small_recv = _gather_forwarded("gather_small", _pack_small(sgrads, loss_part))
    zero = jnp.zeros((), F32)
    souts = _adam("adam_small", small_recv, *[_pack_small(d, zero) for d in (wv, mv, vv)])
    loss = None
    for i, o in enumerate(souts):
        vals, extra = _unpack_small(o)
        if i == 0:
            loss = extra
        for name, _ in SMALL:
            res.setdefault(name, []).append(vals[name])

    out = [loss, grad_x[None]]
    for i in range(4):
        out.extend(res[name][i] for name in WEIGHT_ORDER)
    return tuple(out)
```
